```python
import numpy as np
import jax
import jax.numpy as jnp
from jax import lax

D_MODEL = 1024
BATCH = 2
SEQ = 8192
DEPTH = 1

NSA_Q_HEADS = 8
NSA_KV_HEADS = 2
NSA_GROUP = NSA_Q_HEADS // NSA_KV_HEADS
NSA_HEAD_DIM = 64
CMP_BLOCK = 32
CMP_STRIDE = 16
CMP_HIDDEN = 128
SEL_BLOCK = 64
SEL_TOPK = 16
WINDOW = 512
Q_BLOCK = 128
FORCE_BONUS = 1.0e4

MLSTM_HEADS = 4
MLSTM_HEAD_DIM = 128
MLSTM_CHUNK = 64
CONV_WIDTH = 4

D_FF = 2816
N_MOD = 9
EPS = 1e-6
NEG = -1e30

NSA_WIDTH = NSA_Q_HEADS * NSA_HEAD_DIM
KV_WIDTH = NSA_KV_HEADS * NSA_HEAD_DIM
MLSTM_WIDTH = MLSTM_HEADS * MLSTM_HEAD_DIM
MIX_WIDTH = NSA_WIDTH + MLSTM_WIDTH
IN_SIZES = (NSA_WIDTH, KV_WIDTH, KV_WIDTH, KV_WIDTH, KV_WIDTH, KV_WIDTH, KV_WIDTH, 3 * NSA_Q_HEADS, MLSTM_WIDTH, MLSTM_WIDTH, MLSTM_WIDTH, MLSTM_WIDTH, MLSTM_HEADS, MLSTM_HEADS)
IN_WIDTH = NSA_WIDTH + 6 * KV_WIDTH + 3 * NSA_Q_HEADS + 4 * MLSTM_WIDTH + 2 * MLSTM_HEADS

kernel_name = 'hybrid_nsa_mlstm_macaron_adaln'


def rmsnorm(x, w):
    xf = x.astype(jnp.float32)
    y = xf * lax.rsqrt(jnp.mean(xf * xf, axis=-1, keepdims=True) + EPS)
    return (y * w.astype(jnp.float32)).astype(x.dtype)


def modulate(xn, shift, scale):
    return xn * (1.0 + scale[:, None, :]) + shift[:, None, :]


def swiglu_ffn(h, w_in, w_out):
    g, u = jnp.split(h @ w_in, 2, axis=-1)
    return (jax.nn.silu(g) * u) @ w_out


def masked_softmax(s, mask):
    s = jnp.where(mask, s, NEG)
    e = jnp.where(mask, jnp.exp(s - jnp.max(s, axis=-1, keepdims=True)), 0.0)
    return e / jnp.maximum(jnp.sum(e, axis=-1, keepdims=True), 1e-30)


def alibi_slopes(n):
    return jnp.asarray(2.0 ** (-8.0 * (np.arange(n) + 1) / n), dtype=jnp.float32)


def compress(kv_raw, pe, w1, w2):
    B, S, H, dk = kv_raw.shape
    n_cmp = (S - CMP_BLOCK) // CMP_STRIDE + 1
    idx = np.arange(n_cmp)[:, None] * CMP_STRIDE + np.arange(CMP_BLOCK)[None, :]
    blocks = kv_raw[:, idx] + pe[None, None, :, None, :]
    flat = blocks.transpose(0, 1, 3, 2, 4).reshape(B, n_cmp, H, CMP_BLOCK * dk)
    return jax.nn.gelu(flat @ w1) @ w2


def nsa_attention(q, k_cmp, v_cmp, k_sel, v_sel, k_win, v_win, gates):
    B, S, Hkv, G, dk = q.shape
    dt = v_cmp.dtype
    n_cmp = k_cmp.shape[1]
    n_sel = S // SEL_BLOCK
    topk = min(SEL_TOPK, n_sel)
    cs = np.arange(n_cmp) * CMP_STRIDE
    ce = cs + CMP_BLOCK - 1
    ss = np.arange(n_sel) * SEL_BLOCK
    se = ss + SEL_BLOCK - 1
    overlap = jnp.asarray(((cs[:, None] <= se[None, :]) & (ce[:, None] >= ss[None, :])).astype(np.float32))
    cmp_end = jnp.asarray(ce, dtype=jnp.int32)
    slopes = alibi_slopes(NSA_Q_HEADS).reshape(Hkv, G)
    k_sel_b = k_sel.reshape(B, n_sel, SEL_BLOCK, Hkv, dk).transpose(0, 3, 1, 2, 4)
    v_sel_b = v_sel.reshape(B, n_sel, SEL_BLOCK, Hkv, dk).transpose(0, 3, 1, 2, 4)
    k_win_p = jnp.pad(k_win, ((0, 0), (WINDOW, 0), (0, 0), (0, 0)))
    v_win_p = jnp.pad(v_win, ((0, 0), (WINDOW, 0), (0, 0), (0, 0)))
    bi = jnp.arange(B)[:, None, None, None]
    hi = jnp.arange(Hkv)[None, :, None, None]
    jsel = jnp.arange(n_sel)

    def block(qb):
        start = qb * Q_BLOCK
        t = start + jnp.arange(Q_BLOCK)
        qblk = lax.dynamic_slice_in_dim(q, start, Q_BLOCK, axis=1)
        gblk = lax.dynamic_slice_in_dim(gates, start, Q_BLOCK, axis=1)
        dist_c = t[:, None] - cmp_end[None, :]
        s_c = jnp.einsum('bqhgd,bchd->bhgqc', qblk, k_cmp).astype(jnp.float32) - slopes[:, :, None, None] * dist_c.astype(jnp.float32)
        p_c = masked_softmax(s_c, dist_c >= 0)
        o_c = jnp.einsum('bhgqc,bchd->bqhgd', p_c.astype(dt), v_cmp)
        imp = jnp.einsum('bhgqc,cj->bhqj', p_c, overlap)
        cur = t // SEL_BLOCK
        valid = jsel[None, :] * SEL_BLOCK <= t[:, None]
        forced = (jsel[None, :] == 0) | (jsel[None, :] == cur[:, None]) | (jsel[None, :] == cur[:, None] - 1)
        score = jnp.where(valid, imp + jnp.where(forced, FORCE_BONUS, 0.0), -FORCE_BONUS)
        _, sel = lax.top_k(score, topk)
        ks = k_sel_b[bi, hi, sel]
        vs = v_sel_b[bi, hi, sel]
        pos = sel[..., None] * SEL_BLOCK + jnp.arange(SEL_BLOCK)
        dist_s = (t[None, None, :, None, None] - pos)[:, :, None]
        s_s = jnp.einsum('bqhgd,bhqnld->bhgqnl', qblk, ks).astype(jnp.float32) - slopes[None, :, :, None, None, None] * dist_s.astype(jnp.float32)
        mask_s = jnp.broadcast_to(dist_s >= 0, s_s.shape)
        flat_shape = s_s.shape[:4] + (topk * SEL_BLOCK,)
        p_s = masked_softmax(s_s.reshape(flat_shape), mask_s.reshape(flat_shape)).reshape(s_s.shape)
        o_s = jnp.einsum('bhgqnl,bhqnld->bqhgd', p_s.astype(dt), vs)
        kw = lax.dynamic_slice_in_dim(k_win_p, start, Q_BLOCK + WINDOW, axis=1)
        vw = lax.dynamic_slice_in_dim(v_win_p, start, Q_BLOCK + WINDOW, axis=1)
        kpos = start - WINDOW + jnp.arange(Q_BLOCK + WINDOW)
        dist_w = t[:, None] - kpos[None, :]
        mask_w = (dist_w >= 0) & (dist_w < WINDOW) & (kpos[None, :] >= 0)
        s_w = jnp.einsum('bqhgd,bkhd->bhgqk', qblk, kw).astype(jnp.float32) - slopes[:, :, None, None] * dist_w.astype(jnp.float32)
        p_w = masked_softmax(s_w, mask_w)
        o_w = jnp.einsum('bhgqk,bkhd->bqhgd', p_w.astype(dt), vw)
        return gblk[..., 0:1] * o_c + gblk[..., 1:2] * o_s + gblk[..., 2:3] * o_w

    out = lax.map(block, jnp.arange(S // Q_BLOCK))
    return out.transpose(1, 0, 2, 3, 4, 5).reshape(B, S, Hkv * G * dk)


def causal_depthwise_conv(x, w, b):
    C = x.shape[-1]
    y = lax.conv_general_dilated(x, w[:, None, :], window_strides=(1,), padding=[(CONV_WIDTH - 1, 0)], dimension_numbers=('NWC', 'WIO', 'NWC'), feature_group_count=C)
    return y + b


def mlstm_chunkwise(q, k, v, li, lf):
    B, H, S, d = q.shape
    L = MLSTM_CHUNK
    nc = S // L
    to_chunks = lambda a: jnp.moveaxis(a.reshape((B, H, nc, L) + a.shape[3:]), 2, 0)
    tril = jnp.tril(jnp.ones((L, L), dtype=bool))

    def step(carry, inp):
        C, n, m = carry
        qc, kc, vc, lic, lfc = inp
        qf = qc.astype(jnp.float32)
        kf = kc.astype(jnp.float32)
        vf = vc.astype(jnp.float32)
        b = jnp.cumsum(lfc, axis=-1)
        D = jnp.where(tril, b[..., :, None] - b[..., None, :] + lic[..., None, :], NEG)
        m_inter = b + m[..., None]
        m_t = jnp.maximum(jnp.max(D, axis=-1), m_inter)
        w = jnp.exp(D - m_t[..., None]) * jnp.einsum('bhtd,bhsd->bhts', qf, kf)
        decay = jnp.exp(m_inter - m_t)
        num = jnp.einsum('bhts,bhse->bhte', w, vf) + decay[..., None] * jnp.einsum('bhtd,bhde->bhte', qf, C)
        den = jnp.sum(w, axis=-1) + decay * jnp.einsum('bhtd,bhd->bht', qf, n)
        h = num / jnp.maximum(jnp.abs(den), jnp.exp(-m_t))[..., None]
        bL = b[..., -1]
        g = bL[..., None] - b + lic
        m_new = jnp.maximum(bL + m, jnp.max(g, axis=-1))
        wk = jnp.exp(g - m_new[..., None])
        dC = jnp.exp(bL + m - m_new)
        C = dC[..., None, None] * C + jnp.einsum('bhs,bhsd,bhse->bhde', wk, kf, vf)
        n = dC[..., None] * n + jnp.einsum('bhs,bhsd->bhd', wk, kf)
        return (C, n, m_new), h

    init = (jnp.zeros((B, H, d, d), jnp.float32), jnp.zeros((B, H, d), jnp.float32), jnp.zeros((B, H), jnp.float32))
    _, hs = lax.scan(step, init, (to_chunks(q), to_chunks(k), to_chunks(v), to_chunks(li), to_chunks(lf)))
    return hs.transpose(1, 2, 0, 3, 4).reshape(B, H, S, d)


def token_mix(h, w_in, b_in, cmp_k_pe, cmp_k_w1, cmp_k_w2, cmp_v_pe, cmp_v_w1, cmp_v_w2, conv_w, conv_b, mlstm_norm_w, w_out):
    B, S, _ = h.shape
    proj = h @ w_in + b_in
    cuts = [int(v) for v in np.cumsum(IN_SIZES)[:-1]]
    (q_a, kc, vc, ksl, vsl, kw, vw, g_a, q_m, k_m, v_m, o_m, i_m, f_m) = jnp.split(proj, cuts, axis=-1)
    kv = lambda a: a.reshape(B, S, NSA_KV_HEADS, NSA_HEAD_DIM)
    q_a = q_a.reshape(B, S, NSA_KV_HEADS, NSA_GROUP, NSA_HEAD_DIM) * (NSA_HEAD_DIM ** -0.5)
    k_cmp = compress(kv(kc), cmp_k_pe, cmp_k_w1, cmp_k_w2)
    v_cmp = compress(kv(vc), cmp_v_pe, cmp_v_w1, cmp_v_w2)
    gates = jax.nn.sigmoid(g_a).reshape(B, S, NSA_KV_HEADS, NSA_GROUP, 3)
    o_nsa = nsa_attention(q_a, k_cmp, v_cmp, kv(ksl), kv(vsl), kv(kw), kv(vw), gates)
    qk = jax.nn.silu(causal_depthwise_conv(jnp.concatenate([q_m, k_m], axis=-1), conv_w, conv_b))
    q_m, k_m = jnp.split(qk, 2, axis=-1)
    heads = lambda a: a.reshape(B, S, MLSTM_HEADS, MLSTM_HEAD_DIM).transpose(0, 2, 1, 3)
    li = i_m.astype(jnp.float32).transpose(0, 2, 1)
    lf = jax.nn.log_sigmoid(f_m.astype(jnp.float32)).transpose(0, 2, 1)
    h_m = mlstm_chunkwise(heads(q_m), heads(k_m) * (MLSTM_HEAD_DIM ** -0.5), heads(v_m), li, lf)
    h_m = h_m * lax.rsqrt(jnp.mean(h_m * h_m, axis=-1, keepdims=True) + EPS)
    h_m = h_m.transpose(0, 2, 1, 3).reshape(B, S, MLSTM_WIDTH) * mlstm_norm_w.astype(jnp.float32)
    o_mlstm = h_m.astype(h.dtype) * jax.nn.sigmoid(o_m)
    return jnp.concatenate([o_nsa, o_mlstm], axis=-1) @ w_out


def setup_inputs(seed: int = 0) -> dict:
    key = jax.random.key(seed)
    ks = jax.random.split(key, 26)
    f32 = jnp.float32
    L = DEPTH
    D = D_MODEL
    dk = NSA_HEAD_DIM

    def nrm(k, shape, scale):
        return jax.random.normal(k, shape, f32) * scale

    b_in = nrm(ks[9], (L, IN_WIDTH), 0.01)
    b_in = b_in.at[:, IN_WIDTH - MLSTM_HEADS:].add(jnp.linspace(3.0, 6.0, MLSTM_HEADS, dtype=f32))
    return {
        'x': nrm(ks[0], (BATCH, SEQ, D), 1.0),
        'c': nrm(ks[1], (BATCH, D), 1.0),
        'w_ada': nrm(ks[2], (L, D, N_MOD * D), 0.5 * D ** -0.5),
        'b_ada': nrm(ks[3], (L, N_MOD * D), 0.01),
        'norm_ffn1_w': 1.0 + nrm(ks[4], (L, D), 0.01),
        'ffn1_w_in': nrm(ks[5], (L, D, 2 * D_FF), D ** -0.5),
        'ffn1_w_out': nrm(ks[6], (L, D_FF, D), D_FF ** -0.5),
        'norm_mix_w': 1.0 + nrm(ks[7], (L, D), 0.01),
        'w_in': nrm(ks[8], (L, D, IN_WIDTH), D ** -0.5),
        'b_in': b_in,
        'cmp_k_pe': nrm(ks[10], (L, CMP_BLOCK, dk), 0.02),
        'cmp_k_w1': nrm(ks[11], (L, CMP_BLOCK * dk, CMP_HIDDEN), (CMP_BLOCK * dk) ** -0.5),
        'cmp_k_w2': nrm(ks[12], (L, CMP_HIDDEN, dk), CMP_HIDDEN ** -0.5),
        'cmp_v_pe': nrm(ks[13], (L, CMP_BLOCK, dk), 0.02),
        'cmp_v_w1': nrm(ks[14], (L, CMP_BLOCK * dk, CMP_HIDDEN), (CMP_BLOCK * dk) ** -0.5),
        'cmp_v_w2': nrm(ks[15], (L, CMP_HIDDEN, dk), CMP_HIDDEN ** -0.5),
        'conv_w': nrm(ks[16], (L, CONV_WIDTH, 2 * MLSTM_WIDTH), CONV_WIDTH ** -0.5),
        'conv_b': nrm(ks[17], (L, 2 * MLSTM_WIDTH), 0.01),
        'mlstm_norm_w': 1.0 + nrm(ks[18], (L, MLSTM_WIDTH), 0.01),
        'w_out': nrm(ks[19], (L, MIX_WIDTH, D), MIX_WIDTH ** -0.5),
        'norm_ffn2_w': 1.0 + nrm(ks[20], (L, D), 0.01),
        'ffn2_w_in': nrm(ks[21], (L, D, 2 * D_FF), D ** -0.5),
        'ffn2_w_out': nrm(ks[22], (L, D_FF, D), D_FF ** -0.5),
        'final_norm_w': 1.0 + nrm(ks[23], (D,), 0.01),
    }


def reference(x, c, w_ada, b_ada, norm_ffn1_w, ffn1_w_in, ffn1_w_out, norm_mix_w, w_in, b_in, cmp_k_pe, cmp_k_w1, cmp_k_w2, cmp_v_pe, cmp_v_w1, cmp_v_w2, conv_w, conv_b, mlstm_norm_w, w_out, norm_ffn2_w, ffn2_w_in, ffn2_w_out, final_norm_w):
    for l in range(DEPTH):
        mod = jax.nn.silu(c) @ w_ada[l] + b_ada[l]
        sh1, sc1, g1, shm, scm, gm, sh2, sc2, g2 = jnp.split(mod, N_MOD, axis=-1)
        h = modulate(rmsnorm(x, norm_ffn1_w[l]), sh1, sc1)
        x = x + 0.5 * g1[:, None, :] * swiglu_ffn(h, ffn1_w_in[l], ffn1_w_out[l])
        h = modulate(rmsnorm(x, norm_mix_w[l]), shm, scm)
        x = x + gm[:, None, :] * token_mix(h, w_in[l], b_in[l], cmp_k_pe[l], cmp_k_w1[l], cmp_k_w2[l], cmp_v_pe[l], cmp_v_w1[l], cmp_v_w2[l], conv_w[l], conv_b[l], mlstm_norm_w[l], w_out[l])
        h = modulate(rmsnorm(x, norm_ffn2_w[l]), sh2, sc2)
        x = x + 0.5 * g2[:, None, :] * swiglu_ffn(h, ffn2_w_in[l], ffn2_w_out[l])
    return rmsnorm(x, final_norm_w)
```

```python
import functools

import numpy as np
import jax
import jax.numpy as jnp
from jax import lax
from jax.experimental import pallas as pl
from jax.experimental.pallas import tpu as pltpu

NSA_Q_HEADS = 8
NSA_KV_HEADS = 2
NSA_GROUP = NSA_Q_HEADS // NSA_KV_HEADS
NSA_HEAD_DIM = 64
CMP_BLOCK = 32
CMP_STRIDE = 16
CMP_HIDDEN = 128
SEL_BLOCK = 64
SEL_TOPK = 16
WINDOW = 512
FORCE_BONUS = 1.0e4
MLSTM_HEADS = 4
MLSTM_HEAD_DIM = 128
CONV_WIDTH = 4
D_FF = 2816
N_MOD = 9
EPS = 1e-6
NEG = -1e30

NSA_WIDTH = NSA_Q_HEADS * NSA_HEAD_DIM
KV_WIDTH = NSA_KV_HEADS * NSA_HEAD_DIM
MLSTM_WIDTH = MLSTM_HEADS * MLSTM_HEAD_DIM
N_GATES = 3 * NSA_Q_HEADS

LANES = 128
QT = 128
TOKEN_TILE = 512
FF_CHUNK = 256
MLSTM_L = 128
VMEM_LIMIT = 56 * 1024 * 1024

F32 = jnp.float32
BF16 = jnp.bfloat16
HIGHEST = lax.Precision.HIGHEST


def _cparams(n_axes):
    return pltpu.CompilerParams(dimension_semantics=("arbitrary",) * n_axes,
                                vmem_limit_bytes=VMEM_LIMIT)


def _dot(a, b):
    return jnp.dot(a, b, preferred_element_type=F32)


def _dot_nt(a, b):
    return lax.dot_general(a, b, (((1,), (1,)), ((), ())), preferred_element_type=F32)


def _dot_tn(a, b):
    return lax.dot_general(a, b, (((0,), (0,)), ((), ())), preferred_element_type=F32)


def _sigmoid(x):
    return 1.0 / (1.0 + jnp.exp(-x))


def _silu(x):
    return x * _sigmoid(x)


def _norm_mod(x, nw, sh, sc):
    ms = jnp.mean(x * x, axis=-1, keepdims=True)
    y = x * lax.rsqrt(ms + EPS) * nw
    return y * (1.0 + sc) + sh


def _adaln_kernel(c_ref, w_ref, b_ref, o_ref):
    a = _silu(c_ref[...])
    o_ref[...] = jnp.dot(a, w_ref[...], preferred_element_type=F32, precision=HIGHEST) + b_ref[...]


def _adaln(c8, w_ada, b_ada):
    rows, d = c8.shape
    n = w_ada.shape[1]
    tn = n // N_MOD
    return pl.pallas_call(
        _adaln_kernel,
        grid=(N_MOD,),
        in_specs=[pl.BlockSpec((rows, d), lambda j: (0, 0)),
                  pl.BlockSpec((d, tn), lambda j: (0, j)),
                  pl.BlockSpec((1, tn), lambda j: (0, j))],
        out_specs=pl.BlockSpec((rows, tn), lambda j: (0, j)),
        out_shape=jax.ShapeDtypeStruct((rows, n), F32),
        compiler_params=_cparams(1),
        name="adaln",
    )(c8, w_ada, b_ada)


def _ffn_kernel(*refs, mod_rows, with_mix, mix_gate_row, with_final):
    it = iter(refs)
    x_ref = next(it)
    mod_ref = next(it)
    nw_ref = next(it)
    win_ref = next(it)
    wo_ref = next(it)
    if with_mix:
        ma_ref = next(it)
        mb_ref = next(it)
        wmix_ref = next(it)
    if with_final:
        fnw_ref = next(it)
    o_ref = next(it)
    acc_ref = next(it)

    sh_row, sc_row, g_row = mod_rows
    x = x_ref[...]
    if with_mix:
        half = ma_ref.shape[1]
        mixed = _dot(ma_ref[...], wmix_ref[0:half, :]) + _dot(mb_ref[...], wmix_ref[half:2 * half, :])
        x = x + mod_ref[0, mix_gate_row:mix_gate_row + 1, :] * mixed
    h = _norm_mod(x, nw_ref[...], mod_ref[0, sh_row:sh_row + 1, :], mod_ref[0, sc_row:sc_row + 1, :])
    hb = h.astype(BF16)
    d_ff = wo_ref.shape[0]
    n_chunks = d_ff // FF_CHUNK
    acc_ref[...] = jnp.zeros_like(acc_ref)

    def body(j, carry):
        c0 = pl.multiple_of(j * FF_CHUNK, FF_CHUNK)
        g = _dot(hb, win_ref[:, pl.ds(c0, FF_CHUNK)])
        u = _dot(hb, win_ref[:, pl.ds(pl.multiple_of(d_ff + c0, LANES), FF_CHUNK)])
        act = (_silu(g) * u).astype(BF16)
        acc_ref[...] += _dot(act, wo_ref[pl.ds(c0, FF_CHUNK), :])
        return carry

    lax.fori_loop(0, n_chunks, body, 0)
    y = x + 0.5 * mod_ref[0, g_row:g_row + 1, :] * acc_ref[...]
    if with_final:
        ms = jnp.mean(y * y, axis=-1, keepdims=True)
        y = y * lax.rsqrt(ms + EPS) * fnw_ref[...]
    o_ref[...] = y


def _ffn(x2d, mod, nw, w_in_bf, w_out_bf, mod_rows, tiles_per_batch, mix=None, mix_gate_row=None, final_nw=None):
    t, d = x2d.shape
    tm = TOKEN_TILE
    d_ff = w_out_bf.shape[0]
    assert d_ff % FF_CHUNK == 0
    row = lambda i: (i, 0)
    const = lambda i: (0, 0)
    in_specs = [pl.BlockSpec((tm, d), row),
                pl.BlockSpec((1, N_MOD, d), lambda i: (i // tiles_per_batch, 0, 0)),
                pl.BlockSpec((1, d), const),
                pl.BlockSpec(w_in_bf.shape, const),
                pl.BlockSpec(w_out_bf.shape, const)]
    args = [x2d, mod, nw, w_in_bf, w_out_bf]
    if mix is not None:
        ma, mb, wmix = mix
        in_specs += [pl.BlockSpec((tm, ma.shape[1]), row), pl.BlockSpec((tm, mb.shape[1]), row),
                     pl.BlockSpec(wmix.shape, const)]
        args += [ma, mb, wmix]
    if final_nw is not None:
        in_specs.append(pl.BlockSpec((1, d), const))
        args.append(final_nw)
    kern = functools.partial(_ffn_kernel, mod_rows=mod_rows, with_mix=mix is not None,
                             mix_gate_row=mix_gate_row, with_final=final_nw is not None)
    return pl.pallas_call(
        kern,
        grid=(t // tm,),
        in_specs=in_specs,
        out_specs=pl.BlockSpec((tm, d), row),
        out_shape=jax.ShapeDtypeStruct((t, d), F32),
        scratch_shapes=[pltpu.VMEM((tm, d), F32)],
        compiler_params=_cparams(1),
        name="ffn_mix" if mix is not None else "ffn",
    )(*args)


QPAD_W = NSA_Q_HEADS * LANES
KV6_OFF = QPAD_W
ML_OFF = KV6_OFF + 6 * KV_WIDTH
MISC_OFF = ML_OFF + 4 * MLSTM_WIDTH
PROJ_W = MISC_OFF + LANES
MISC_I = N_GATES
MISC_F = N_GATES + MLSTM_HEADS


def _inproj_kernel(x_ref, mod_ref, nw_ref, w_ref, b_ref,
                   qpad_ref, kc_ref, vc_ref, ksel_ref, kwin_ref, vtsel_ref, vtwin_ref,
                   ml_ref, misc_ref, misct_ref):
    h = _norm_mod(x_ref[...], nw_ref[...], mod_ref[0, 3:4, :], mod_ref[0, 4:5, :])
    hb = h.astype(BF16)

    def proj(c0, width):
        return _dot(hb, w_ref[:, c0:c0 + width]) + b_ref[:, c0:c0 + width]

    scale = NSA_HEAD_DIM ** -0.5
    for i in range(NSA_Q_HEADS):
        qpad_ref[:, i * LANES:(i + 1) * LANES] = (proj(i * LANES, LANES) * scale).astype(BF16)
    kc_ref[...] = proj(KV6_OFF, KV_WIDTH).astype(BF16)
    vc_ref[...] = proj(KV6_OFF + KV_WIDTH, KV_WIDTH).astype(BF16)
    ksel_ref[...] = proj(KV6_OFF + 2 * KV_WIDTH, KV_WIDTH).astype(BF16)
    vtsel_ref[0] = proj(KV6_OFF + 3 * KV_WIDTH, KV_WIDTH).T.astype(BF16)
    kwin_ref[...] = proj(KV6_OFF + 4 * KV_WIDTH, KV_WIDTH).astype(BF16)
    vtwin_ref[0] = proj(KV6_OFF + 5 * KV_WIDTH, KV_WIDTH).T.astype(BF16)
    for i in range(4):
        ml_ref[:, i * MLSTM_WIDTH:(i + 1) * MLSTM_WIDTH] = proj(ML_OFF + i * MLSTM_WIDTH, MLSTM_WIDTH)
    misc = proj(MISC_OFF, LANES)
    misc_ref[...] = misc
    misct_ref[0] = misc.T


def _inproj(x2d, mod, nw, w_pack, b_pack, batch, seq):
    t, d = x2d.shape
    tm = TOKEN_TILE
    tpb = seq // tm
    row = lambda i: (i, 0)
    const = lambda i: (0, 0)
    trow = lambda i: (i // tpb, 0, i % tpb)
    out_shapes = [
        jax.ShapeDtypeStruct((t, QPAD_W), BF16),
        jax.ShapeDtypeStruct((t, KV_WIDTH), BF16),
        jax.ShapeDtypeStruct((t, KV_WIDTH), BF16),
        jax.ShapeDtypeStruct((t, KV_WIDTH), BF16),
        jax.ShapeDtypeStruct((t, KV_WIDTH), BF16),
        jax.ShapeDtypeStruct((batch, KV_WIDTH, seq), BF16),
        jax.ShapeDtypeStruct((batch, KV_WIDTH, seq), BF16),
        jax.ShapeDtypeStruct((t, 4 * MLSTM_WIDTH), F32),
        jax.ShapeDtypeStruct((t, LANES), F32),
        jax.ShapeDtypeStruct((batch, LANES, seq), F32),
    ]
    out_specs = [
        pl.BlockSpec((tm, QPAD_W), row),
        pl.BlockSpec((tm, KV_WIDTH), row),
        pl.BlockSpec((tm, KV_WIDTH), row),
        pl.BlockSpec((tm, KV_WIDTH), row),
        pl.BlockSpec((tm, KV_WIDTH), row),
        pl.BlockSpec((1, KV_WIDTH, tm), trow),
        pl.BlockSpec((1, KV_WIDTH, tm), trow),
        pl.BlockSpec((tm, 4 * MLSTM_WIDTH), row),
        pl.BlockSpec((tm, LANES), row),
        pl.BlockSpec((1, LANES, tm), trow),
    ]
    return pl.pallas_call(
        _inproj_kernel,
        grid=(t // tm,),
        in_specs=[pl.BlockSpec((tm, d), row),
                  pl.BlockSpec((1, N_MOD, d), lambda i: (i // tpb, 0, 0)),
                  pl.BlockSpec((1, d), const),
                  pl.BlockSpec(w_pack.shape, const),
                  pl.BlockSpec(b_pack.shape, const)],
        out_specs=out_specs,
        out_shape=out_shapes,
        compiler_params=_cparams(1),
        name="inproj",
    )(x2d, mod, nw, w_pack, b_pack)


def _pack_inproj(w_in, b_in):
    d = w_in.shape[0]
    offs = np.cumsum([0, NSA_WIDTH] + [KV_WIDTH] * 6 + [N_GATES] + [MLSTM_WIDTH] * 4 + [MLSTM_HEADS] * 2)
    wq = w_in[:, :NSA_WIDTH].reshape(d, NSA_Q_HEADS, NSA_HEAD_DIM)
    bq = b_in[:NSA_WIDTH].reshape(NSA_Q_HEADS, NSA_HEAD_DIM)
    wq_pad = jnp.zeros((d, NSA_Q_HEADS, LANES), w_in.dtype)
    bq_pad = jnp.zeros((NSA_Q_HEADS, LANES), b_in.dtype)
    for i in range(NSA_Q_HEADS):
        lo = NSA_HEAD_DIM * (i // NSA_GROUP)
        wq_pad = wq_pad.at[:, i, lo:lo + NSA_HEAD_DIM].set(wq[:, i])
        bq_pad = bq_pad.at[i, lo:lo + NSA_HEAD_DIM].set(bq[i])
    g0, m0, i0 = int(offs[7]), int(offs[8]), int(offs[12])
    misc_w = jnp.concatenate([w_in[:, g0:g0 + N_GATES], w_in[:, i0:i0 + 2 * MLSTM_HEADS],
                              jnp.zeros((d, LANES - N_GATES - 2 * MLSTM_HEADS), w_in.dtype)], axis=1)
    misc_b = jnp.concatenate([b_in[g0:g0 + N_GATES], b_in[i0:i0 + 2 * MLSTM_HEADS],
                              jnp.zeros((LANES - N_GATES - 2 * MLSTM_HEADS,), b_in.dtype)])
    w_pack = jnp.concatenate([wq_pad.reshape(d, QPAD_W), w_in[:, NSA_WIDTH:g0], w_in[:, m0:i0], misc_w], axis=1)
    b_pack = jnp.concatenate([bq_pad.reshape(QPAD_W), b_in[NSA_WIDTH:g0], b_in[m0:i0], misc_b])
    assert w_pack.shape[1] == PROJ_W
    return w_pack.astype(BF16), b_pack.reshape(1, PROJ_W).astype(F32)


def _compress_kernel(xk_ref, xv_ref, wa_ref, wb_ref, pe_ref, w1_ref, w2_ref, kc_ref, vtc_ref):
    n_half = xk_ref.shape[1]

    def one(x_ref, s):
        x = x_ref[0]
        a = _dot(x, wa_ref[s])
        b = _dot(x, wb_ref[s])
        pe_term = jnp.dot(pe_ref[s], w1_ref[s], preferred_element_type=F32, precision=HIGHEST)[0:1, :]
        pe2 = jnp.concatenate([pe_term] * NSA_KV_HEADS, axis=1)
        pre = a + pltpu.roll(b, n_half - 1, 0) + pe2
        hid = 0.5 * pre * (1.0 + jnp.tanh(0.7978845608028654 * (pre + 0.044715 * pre * pre * pre)))
        return _dot(hid.astype(BF16), w2_ref[s])

    kc_ref[0] = one(xk_ref, 0).astype(BF16)
    vtc_ref[0] = one(xv_ref, 1).T.astype(BF16)


def _compress(xk, xv, wa, wb, pe8, w1, w2e):
    batch, n_half, width = xk.shape
    bsel = lambda b: (b, 0, 0)
    c3 = lambda b: (0, 0, 0)
    return pl.pallas_call(
        _compress_kernel,
        grid=(batch,),
        in_specs=[pl.BlockSpec((1, n_half, width), bsel), pl.BlockSpec((1, n_half, width), bsel),
                  pl.BlockSpec(wa.shape, c3), pl.BlockSpec(wb.shape, c3),
                  pl.BlockSpec(pe8.shape, c3), pl.BlockSpec(w1.shape, c3), pl.BlockSpec(w2e.shape, c3)],
        out_specs=[pl.BlockSpec((1, n_half, KV_WIDTH), bsel), pl.BlockSpec((1, KV_WIDTH, n_half), bsel)],
        out_shape=[jax.ShapeDtypeStruct((batch, n_half, KV_WIDTH), BF16),
                   jax.ShapeDtypeStruct((batch, KV_WIDTH, n_half), BF16)],
        compiler_params=_cparams(1),
        name="compress",
    )(xk, xv, wa, wb, pe8, w1, w2e)


def _pack_compress(pe, w1, w2):
    half = CMP_BLOCK // 2
    w1r = w1.reshape(CMP_BLOCK, NSA_HEAD_DIM, CMP_HIDDEN)

    def expand(w_half):
        z = jnp.zeros((half, NSA_KV_HEADS, NSA_HEAD_DIM, NSA_KV_HEADS, CMP_HIDDEN), w1.dtype)
        for hh in range(NSA_KV_HEADS):
            z = z.at[:, hh, :, hh, :].set(w_half)
        return z.reshape(half * KV_WIDTH, NSA_KV_HEADS * CMP_HIDDEN)

    wa = expand(w1r[:half]).astype(BF16)
    wb = expand(w1r[half:]).astype(BF16)
    w2e = jnp.zeros((NSA_KV_HEADS, CMP_HIDDEN, NSA_KV_HEADS, NSA_HEAD_DIM), w2.dtype)
    for hh in range(NSA_KV_HEADS):
        w2e = w2e.at[hh, :, hh, :].set(w2)
    w2e = w2e.reshape(NSA_KV_HEADS * CMP_HIDDEN, KV_WIDTH).astype(BF16)
    pe8 = jnp.concatenate([pe.reshape(1, CMP_BLOCK * NSA_HEAD_DIM),
                           jnp.zeros((7, CMP_BLOCK * NSA_HEAD_DIM), pe.dtype)], axis=0)
    return wa, wb, pe8, w1, w2e


def _group_q(q_ref, h):
    return jnp.concatenate(
        [q_ref[0, :, (NSA_GROUP * h + g) * LANES:(NSA_GROUP * h + g + 1) * LANES] for g in range(NSA_GROUP)], axis=0)


def _slope(head):
    return float(2.0 ** (-8.0 * (head + 1) / NSA_Q_HEADS))


def _gate_row(gt_ref, head, branch):
    r = head * 3 + branch
    return _sigmoid(gt_ref[0, r:r + 1, :])


def _nsa_cmp_kernel(q_ref, kc_ref, vtc_ref, gt_ref, oc_ref, sel_ref, cnt_ref):
    ncp = kc_ref.shape[1]
    n_sel = sel_ref.shape[3]
    start = pl.program_id(1) * QT
    c_io = lax.broadcasted_iota(jnp.int32, (ncp, QT), 0)
    t_c = start + lax.broadcasted_iota(jnp.int32, (ncp, QT), 1)
    dist = t_c - (c_io * CMP_STRIDE + (CMP_BLOCK - 1))
    maskc = dist >= 0
    distf = dist.astype(F32)

    j_o = lax.broadcasted_iota(jnp.int32, (n_sel, ncp), 0)
    c_o = lax.broadcasted_iota(jnp.int32, (n_sel, ncp), 1)
    ov = ((c_o * CMP_STRIDE <= j_o * SEL_BLOCK + (SEL_BLOCK - 1))
          & (c_o * CMP_STRIDE + (CMP_BLOCK - 1) >= j_o * SEL_BLOCK))
    ovt = jnp.where(ov, 1.0, 0.0).astype(BF16)

    j_io = lax.broadcasted_iota(jnp.int32, (n_sel, QT), 0)
    t_j = start + lax.broadcasted_iota(jnp.int32, (n_sel, QT), 1)
    cur = t_j // SEL_BLOCK
    valid = j_io * SEL_BLOCK <= t_j
    forced = (j_io == 0) | (j_io == cur) | (j_io == cur - 1)
    ones8 = jnp.ones((8, QT), BF16)

    for h in range(NSA_KV_HEADS):
        qh = _group_q(q_ref, h)
        st = _dot_nt(kc_ref[0], qh)
        psum = jnp.zeros((ncp, QT), F32)
        p_chunks = []
        for g in range(NSA_GROUP):
            s = st[:, g * QT:(g + 1) * QT] - _slope(NSA_GROUP * h + g) * distf
            s = jnp.where(maskc, s, NEG)
            m = jnp.max(s, axis=0, keepdims=True)
            e = jnp.where(maskc, jnp.exp(s - m), 0.0)
            p = e / jnp.maximum(jnp.sum(e, axis=0, keepdims=True), 1e-30)
            psum = psum + p
            p_chunks.append(p.astype(BF16))
        pt = jnp.concatenate(p_chunks, axis=1)
        ot = _dot(vtc_ref[0, h * NSA_HEAD_DIM:(h + 1) * NSA_HEAD_DIM, :], pt)
        for g in range(NSA_GROUP):
            head = NSA_GROUP * h + g
            oc_ref[0, head] = ot[:, g * QT:(g + 1) * QT] * _gate_row(gt_ref, head, 0)

        hi = psum.astype(BF16)
        lo = (psum - hi.astype(F32)).astype(BF16)
        imp = _dot(ovt, hi) + _dot(ovt, lo)
        score = jnp.where(valid, imp + jnp.where(forced, FORCE_BONUS, 0.0), -FORCE_BONUS)

        def pick(_, carry):
            sc, sel = carry
            mx = jnp.max(sc, axis=0, keepdims=True)
            idx = jnp.min(jnp.where(sc == mx, j_io, n_sel), axis=0, keepdims=True)
            hit = j_io == idx
            return jnp.where(hit, -3.0e38, sc), jnp.where(hit, 1.0, sel)

        _, sel = lax.fori_loop(0, min(SEL_TOPK, n_sel), pick, (score, jnp.zeros((n_sel, QT), F32)))
        sel_ref[0, h, 0] = sel
        cnt = _dot_nt(ones8, sel.astype(BF16))
        cnt_ref[0, 0, h] = cnt[0:1, :]


def _nsa_cmp(qpad3, kc, vtc, gates_t, batch, seq):
    nb = seq // QT
    n_sel = seq // SEL_BLOCK
    ncp = kc.shape[1]
    return pl.pallas_call(
        _nsa_cmp_kernel,
        grid=(batch, nb),
        in_specs=[pl.BlockSpec((1, QT, QPAD_W), lambda b, i: (b, i, 0)),
                  pl.BlockSpec((1, ncp, KV_WIDTH), lambda b, i: (b, 0, 0)),
                  pl.BlockSpec((1, KV_WIDTH, ncp), lambda b, i: (b, 0, 0)),
                  pl.BlockSpec((1, N_GATES, QT), lambda b, i: (b, 0, i))],
        out_specs=[pl.BlockSpec((1, NSA_Q_HEADS, NSA_HEAD_DIM, QT), lambda b, i: (b, 0, 0, i)),
                   pl.BlockSpec((1, NSA_KV_HEADS, 1, n_sel, QT), lambda b, i: (b, 0, i, 0, 0)),
                   pl.BlockSpec((1, 1, NSA_KV_HEADS, 1, n_sel), lambda b, i: (b, i, 0, 0, 0))],
        out_shape=[jax.ShapeDtypeStruct((batch, NSA_Q_HEADS, NSA_HEAD_DIM, seq), F32),
                   jax.ShapeDtypeStruct((batch, NSA_KV_HEADS, nb, n_sel, QT), F32),
                   jax.ShapeDtypeStruct((batch, nb, NSA_KV_HEADS, 1, n_sel), F32)],
        compiler_params=_cparams(2),
        name="nsa_cmp",
    )(qpad3, kc, vtc, gates_t)


def _nsa_sel_kernel(flags_ref, q_ref, ksel_ref, vtsel_ref, kwin_ref, vtwin_ref, selm_ref, gt_ref, oc_ref,
                    o_ref, m_ref, l_ref, acc_ref):
    b = pl.program_id(0)
    qb = pl.program_id(1)
    nb = pl.num_programs(1)
    start = qb * QT
    rows = NSA_GROUP * QT
    k_io = lax.broadcasted_iota(jnp.int32, (QT, QT), 0)
    d0 = lax.broadcasted_iota(jnp.int32, (QT, QT), 1) - k_io
    first_half = k_io < SEL_BLOCK

    def reset():
        m_ref[...] = jnp.full((1, rows), NEG, F32)
        l_ref[...] = jnp.zeros((1, rows), F32)
        acc_ref[...] = jnp.zeros((NSA_HEAD_DIM, rows), F32)

    def visit(p, h, qh, k_ref, vt_ref, mask):
        off = pl.multiple_of(p * QT, QT)
        st = _dot_nt(k_ref[0, pl.ds(off, QT), :], qh)
        distf = (d0 + (start - p * QT)).astype(F32)
        e_chunks = []
        a_chunks = []
        for g in range(NSA_GROUP):
            sl = slice(g * QT, (g + 1) * QT)
            s = st[:, sl] - _slope(NSA_GROUP * h + g) * distf
            s = jnp.where(mask, s, NEG)
            m_old = m_ref[:, sl]
            m_new = jnp.maximum(m_old, jnp.max(s, axis=0, keepdims=True))
            alpha = jnp.exp(m_old - m_new)
            e = jnp.where(mask, jnp.exp(s - m_new), 0.0)
            l_ref[:, sl] = alpha * l_ref[:, sl] + jnp.sum(e, axis=0, keepdims=True)
            m_ref[:, sl] = m_new
            e_chunks.append(e.astype(BF16))
            a_chunks.append(alpha)
        et = jnp.concatenate(e_chunks, axis=1)
        alpha_all = jnp.concatenate(a_chunks, axis=1)
        vt = vt_ref[0, h * NSA_HEAD_DIM:(h + 1) * NSA_HEAD_DIM, pl.ds(off, QT)]
        acc_ref[...] = alpha_all * acc_ref[...] + _dot(vt, et)

    def finish(h, branch, total):
        inv = 1.0 / jnp.maximum(l_ref[...], 1e-30)
        out = []
        for g in range(NSA_GROUP):
            sl = slice(g * QT, (g + 1) * QT)
            head = NSA_GROUP * h + g
            out.append(total[g] + acc_ref[:, sl] * (inv[:, sl] * _gate_row(gt_ref, head, branch)))
        return out

    for h in range(NSA_KV_HEADS):
        qh = _group_q(q_ref, h)
        total = [oc_ref[0, NSA_GROUP * h + g] for g in range(NSA_GROUP)]

        reset()
        fbase = ((b * nb + qb) * NSA_KV_HEADS + h) * nb

        def sel_body(p, carry):
            @pl.when(flags_ref[fbase + p] > 0)
            def _():
                mrow = selm_ref[0, h, 0, p]
                chosen = jnp.where(first_half, mrow[0:1, :], mrow[1:2, :]) > 0.5
                dist = d0 + (start - p * QT)
                visit(p, h, qh, ksel_ref, vtsel_ref, chosen & (dist >= 0))
            return carry

        lax.fori_loop(0, qb + 1, sel_body, 0)
        total = finish(h, 1, total)

        reset()

        def win_body(p, carry):
            dist = d0 + (start - p * QT)
            visit(p, h, qh, kwin_ref, vtwin_ref, (dist >= 0) & (dist < WINDOW))
            return carry

        lax.fori_loop(jnp.maximum(qb - WINDOW // QT, 0), qb + 1, win_body, 0)
        total = finish(h, 2, total)
        for g in range(NSA_GROUP):
            o_ref[0, NSA_GROUP * h + g] = total[g]


def _nsa_sel(flags, qpad3, ksel, vtsel, kwin, vtwin, selm, gates_t, oc, batch, seq):
    nb = seq // QT
    res = lambda b, i, f: (b, 0, 0)
    grid_spec = pltpu.PrefetchScalarGridSpec(
        num_scalar_prefetch=1,
        grid=(batch, nb),
        in_specs=[pl.BlockSpec((1, QT, QPAD_W), lambda b, i, f: (b, i, 0)),
                  pl.BlockSpec((1, seq, KV_WIDTH), res),
                  pl.BlockSpec((1, KV_WIDTH, seq), res),
                  pl.BlockSpec((1, seq, KV_WIDTH), res),
                  pl.BlockSpec((1, KV_WIDTH, seq), res),
                  pl.BlockSpec((1, NSA_KV_HEADS, 1, nb, 2, QT), lambda b, i, f: (b, 0, i, 0, 0, 0)),
                  pl.BlockSpec((1, N_GATES, QT), lambda b, i, f: (b, 0, i)),
                  pl.BlockSpec((1, NSA_Q_HEADS, NSA_HEAD_DIM, QT), lambda b, i, f: (b, 0, 0, i))],
        out_specs=pl.BlockSpec((1, NSA_Q_HEADS, NSA_HEAD_DIM, QT), lambda b, i, f: (b, 0, 0, i)),
        scratch_shapes=[pltpu.VMEM((1, NSA_GROUP * QT), F32),
                        pltpu.VMEM((1, NSA_GROUP * QT), F32),
                        pltpu.VMEM((NSA_HEAD_DIM, NSA_GROUP * QT), F32)],
    )
    return pl.pallas_call(
        _nsa_sel_kernel,
        grid_spec=grid_spec,
        out_shape=jax.ShapeDtypeStruct((batch, NSA_Q_HEADS, NSA_HEAD_DIM, seq), F32),
        compiler_params=_cparams(2),
        name="nsa_sel",
    )(flags, qpad3, ksel, vtsel, kwin, vtwin, selm, gates_t, oc)


def _mlstm_kernel(q_ref, k_ref, v_ref, og_ref, misc_ref, misct_ref, cw_ref, cb_ref, nw_ref,
                  o_ref, xq_ref, xk_ref, c_ref, n_ref, m_ref):
    tm = q_ref.shape[0]
    L = MLSTM_L
    hd = MLSTM_HEAD_DIM
    halo = 8

    @pl.when(pl.program_id(1) == 0)
    def _():
        xq_ref[0:halo, :] = jnp.zeros((halo, MLSTM_WIDTH), F32)
        xk_ref[0:halo, :] = jnp.zeros((halo, MLSTM_WIDTH), F32)
        c_ref[...] = jnp.zeros_like(c_ref)
        n_ref[...] = jnp.zeros_like(n_ref)
        m_ref[...] = jnp.zeros_like(m_ref)

    def conv(x_ref, buf_ref, col0):
        buf_ref[halo:halo + tm, :] = x_ref[...]
        y = cb_ref[:, col0:col0 + MLSTM_WIDTH]
        for kk in range(CONV_WIDTH):
            r0 = halo - (CONV_WIDTH - 1) + kk
            y = y + cw_ref[kk:kk + 1, col0:col0 + MLSTM_WIDTH] * buf_ref[r0:r0 + tm, :]
        tail = buf_ref[tm:tm + halo, :]
        buf_ref[0:halo, :] = tail
        return _silu(y)

    qc = conv(q_ref, xq_ref, 0)
    kc = conv(k_ref, xk_ref, MLSTM_WIDTH) * (hd ** -0.5)

    misc = misc_ref[...]
    misct = misct_ref[0]
    lf_c = -jnp.log(1.0 + jnp.exp(-misc))
    lf_r = -jnp.log(1.0 + jnp.exp(-misct))
    r_io = lax.broadcasted_iota(jnp.int32, (tm, tm), 0)
    c_io = lax.broadcasted_iota(jnp.int32, (tm, tm), 1)
    same = (r_io // L) == (c_io // L)
    tril_blk = jnp.where(same & (c_io <= r_io), 1.0, 0.0)
    triu_blk = jnp.where(same & (r_io <= c_io), 1.0, 0.0)
    b_c = jnp.dot(tril_blk, lf_c, preferred_element_type=F32, precision=HIGHEST)
    b_r = jnp.dot(lf_r, triu_blk, preferred_element_type=F32, precision=HIGHEST)

    tri = lax.broadcasted_iota(jnp.int32, (L, L), 1) <= lax.broadcasted_iota(jnp.int32, (L, L), 0)

    for ci in range(tm // L):
        r0 = ci * L
        for h in range(MLSTM_HEADS):
            cs = slice(h * hd, (h + 1) * hd)
            q = qc[r0:r0 + L, cs]
            k = kc[r0:r0 + L, cs]
            v = v_ref[r0:r0 + L, cs]
            bcol = b_c[r0:r0 + L, MISC_F + h:MISC_F + h + 1]
            licol = misc[r0:r0 + L, MISC_I + h:MISC_I + h + 1]
            brow = b_r[MISC_F + h:MISC_F + h + 1, r0:r0 + L]
            lirow = misct[MISC_I + h:MISC_I + h + 1, r0:r0 + L]
            m_prev = m_ref[h]
            cmat = c_ref[h]
            nrow = n_ref[h]

            dmat = jnp.where(tri, bcol - brow + lirow, NEG)
            m_inter = bcol + m_prev
            m_t = jnp.maximum(jnp.max(dmat, axis=-1, keepdims=True), m_inter)
            qb16 = q.astype(BF16)
            kb16 = k.astype(BF16)
            vb16 = v.astype(BF16)
            w = jnp.exp(dmat - m_t) * _dot_nt(qb16, kb16)
            decay = jnp.exp(m_inter - m_t)
            num = _dot(w.astype(BF16), vb16) + decay * _dot(qb16, cmat.astype(BF16))
            den = jnp.sum(w, axis=-1, keepdims=True) + decay * jnp.sum(q * nrow, axis=-1, keepdims=True)
            hout = num / jnp.maximum(jnp.abs(den), jnp.exp(-m_t))

            b_last = brow[:, L - 1:L]
            gcol = b_last - bcol + licol
            m_new = jnp.maximum(b_last + m_prev, jnp.max(gcol, axis=0, keepdims=True))
            wk = jnp.exp(gcol - m_new)
            d_c = jnp.exp(b_last + m_prev - m_new)
            kw = wk * k
            c_ref[h] = d_c * cmat + _dot_tn(kw.astype(BF16), vb16)
            n_ref[h] = d_c * nrow + jnp.sum(kw, axis=0, keepdims=True)
            m_ref[h] = m_new

            hn = hout * lax.rsqrt(jnp.mean(hout * hout, axis=-1, keepdims=True) + EPS)
            hn = hn * nw_ref[:, cs]
            o_ref[r0:r0 + L, cs] = (hn * _sigmoid(og_ref[r0:r0 + L, cs])).astype(o_ref.dtype)


def _mlstm(ml, misc, misc_t, conv_w, conv_b, norm_w, batch, seq):
    t = ml.shape[0]
    tm = TOKEN_TILE
    tpb = seq // tm
    col = lambda j: (lambda b, i: (b * tpb + i, j))
    const = lambda b, i: (0, 0)
    return pl.pallas_call(
        _mlstm_kernel,
        grid=(batch, tpb),
        in_specs=[pl.BlockSpec((tm, MLSTM_WIDTH), col(0)),
                  pl.BlockSpec((tm, MLSTM_WIDTH), col(1)),
                  pl.BlockSpec((tm, MLSTM_WIDTH), col(2)),
                  pl.BlockSpec((tm, MLSTM_WIDTH), col(3)),
                  pl.BlockSpec((tm, LANES), col(0)),
                  pl.BlockSpec((1, LANES, tm), lambda b, i: (b, 0, i)),
                  pl.BlockSpec(conv_w.shape, const),
                  pl.BlockSpec(conv_b.shape, const),
                  pl.BlockSpec(norm_w.shape, const)],
        out_specs=pl.BlockSpec((tm, MLSTM_WIDTH), col(0)),
        out_shape=jax.ShapeDtypeStruct((t, MLSTM_WIDTH), BF16),
        scratch_shapes=[pltpu.VMEM((tm + 8, MLSTM_WIDTH), F32),
                        pltpu.VMEM((tm + 8, MLSTM_WIDTH), F32),
                        pltpu.VMEM((MLSTM_HEADS, MLSTM_HEAD_DIM, MLSTM_HEAD_DIM), F32),
                        pltpu.VMEM((MLSTM_HEADS, 1, MLSTM_HEAD_DIM), F32),
                        pltpu.VMEM((MLSTM_HEADS, 1, 1), F32)],
        compiler_params=_cparams(2),
        name="mlstm",
    )(ml, ml, ml, ml, misc, misc_t, conv_w, conv_b, norm_w)


def _layer(x2d, c, batch, seq, w_ada, b_ada, norm_ffn1_w, ffn1_w_in, ffn1_w_out, norm_mix_w, w_in, b_in,
           cmp_k_pe, cmp_k_w1, cmp_k_w2, cmp_v_pe, cmp_v_w1, cmp_v_w2, conv_w, conv_b, mlstm_norm_w, w_out,
           norm_ffn2_w, ffn2_w_in, ffn2_w_out, final_nw):
    d = x2d.shape[1]
    tpb = seq // TOKEN_TILE
    c8 = jnp.concatenate([c, jnp.zeros((8 - batch % 8 if batch % 8 else 0, d), c.dtype)], axis=0)
    mod = _adaln(c8, w_ada, b_ada.reshape(1, -1))[:batch].reshape(batch, N_MOD, d)

    x1 = _ffn(x2d, mod, norm_ffn1_w.reshape(1, d), ffn1_w_in.astype(BF16), ffn1_w_out.astype(BF16),
              (0, 1, 2), tpb)

    w_pack, b_pack = _pack_inproj(w_in, b_in)
    (qpad, kc, vc, ksel, kwin, vtsel, vtwin, ml, misc, misc_t) = _inproj(
        x1, mod, norm_mix_w.reshape(1, d), w_pack, b_pack, batch, seq)

    n_half = seq // CMP_STRIDE
    pk = _pack_compress(cmp_k_pe, cmp_k_w1, cmp_k_w2)
    pv = _pack_compress(cmp_v_pe, cmp_v_w1, cmp_v_w2)
    stacked = [jnp.stack([a, bb]) for a, bb in zip(pk, pv)]
    kcmp, vtcmp = _compress(kc.reshape(batch, n_half, CMP_STRIDE * KV_WIDTH),
                            vc.reshape(batch, n_half, CMP_STRIDE * KV_WIDTH), *stacked)

    nb = seq // QT
    qpad3 = qpad.reshape(batch, seq, QPAD_W)
    gates_t = misc_t[:, :N_GATES, :]
    oc, sel, cnt = _nsa_cmp(qpad3, kcmp, vtcmp, gates_t, batch, seq)
    pair = cnt.reshape(batch, nb, NSA_KV_HEADS, nb, 2).sum(axis=-1)
    flags = (pair > 0.5).astype(jnp.int32).reshape(-1)
    selm = sel.reshape(batch, NSA_KV_HEADS, nb, nb, 2, QT)
    o_t = _nsa_sel(flags, qpad3, ksel.reshape(batch, seq, KV_WIDTH), vtsel, kwin.reshape(batch, seq, KV_WIDTH),
                   vtwin, selm, gates_t, oc, batch, seq)
    o_nsa = o_t.transpose(0, 3, 1, 2).reshape(batch * seq, NSA_WIDTH).astype(BF16)

    o_ml = _mlstm(ml, misc, misc_t, conv_w, conv_b.reshape(1, -1), mlstm_norm_w.reshape(1, -1), batch, seq)

    return _ffn(x1, mod, norm_ffn2_w.reshape(1, d), ffn2_w_in.astype(BF16), ffn2_w_out.astype(BF16),
                (6, 7, 8), tpb, mix=(o_nsa, o_ml, w_out.astype(BF16)), mix_gate_row=5, final_nw=final_nw)


def kernel(x, c, w_ada, b_ada, norm_ffn1_w, ffn1_w_in, ffn1_w_out, norm_mix_w, w_in, b_in, cmp_k_pe, cmp_k_w1, cmp_k_w2, cmp_v_pe, cmp_v_w1, cmp_v_w2, conv_w, conv_b, mlstm_norm_w, w_out, norm_ffn2_w, ffn2_w_in, ffn2_w_out, final_norm_w):
    batch, seq, d = x.shape
    depth = w_ada.shape[0]
    assert depth == 1 and seq % TOKEN_TILE == 0 and seq // SEL_BLOCK >= SEL_TOPK
    y = _layer(x.reshape(batch * seq, d), c, batch, seq, w_ada[0], b_ada[0], norm_ffn1_w[0], ffn1_w_in[0],
               ffn1_w_out[0], norm_mix_w[0], w_in[0], b_in[0], cmp_k_pe[0], cmp_k_w1[0], cmp_k_w2[0],
               cmp_v_pe[0], cmp_v_w1[0], cmp_v_w2[0], conv_w[0], conv_b[0], mlstm_norm_w[0], w_out[0],
               norm_ffn2_w[0], ffn2_w_in[0], ffn2_w_out[0], final_norm_w.reshape(1, d))
    return y.reshape(batch, seq, d)
```

```python
import functools

import numpy as np
import jax
import jax.numpy as jnp
from jax import lax
from jax.experimental import pallas as pl
from jax.experimental.pallas import tpu as pltpu

NSA_Q_HEADS = 8
NSA_KV_HEADS = 2
NSA_GROUP = NSA_Q_HEADS // NSA_KV_HEADS
NSA_HEAD_DIM = 64
CMP_BLOCK = 32
CMP_STRIDE = 16
CMP_HIDDEN = 128
SEL_BLOCK = 64
SEL_TOPK = 16
WINDOW = 512
FORCE_BONUS = 1.0e4
MLSTM_HEADS = 4
MLSTM_HEAD_DIM = 128
CONV_WIDTH = 4
D_FF = 2816
N_MOD = 9
EPS = 1e-6
NEG = -1e30

NSA_WIDTH = NSA_Q_HEADS * NSA_HEAD_DIM
KV_WIDTH = NSA_KV_HEADS * NSA_HEAD_DIM
MLSTM_WIDTH = MLSTM_HEADS * MLSTM_HEAD_DIM
N_GATES = 3 * NSA_Q_HEADS

LANES = 128
QT = 128
TOKEN_TILE = 512
FF_CHUNK = 256
MLSTM_L = 128
SEL_UNROLL = 4
VMEM_LIMIT = 56 * 1024 * 1024

F32 = jnp.float32
BF16 = jnp.bfloat16
HIGHEST = lax.Precision.HIGHEST


def _cparams(n_axes):
    return pltpu.CompilerParams(dimension_semantics=("arbitrary",) * n_axes,
                                vmem_limit_bytes=VMEM_LIMIT)


def _dot(a, b):
    return jnp.dot(a, b, preferred_element_type=F32)


def _dot_nt(a, b):
    return lax.dot_general(a, b, (((1,), (1,)), ((), ())), preferred_element_type=F32)


def _dot_tn(a, b):
    return lax.dot_general(a, b, (((0,), (0,)), ((), ())), preferred_element_type=F32)


def _sigmoid(x):
    return 1.0 / (1.0 + jnp.exp(-x))


def _silu(x):
    return x * _sigmoid(x)


def _norm_mod(x, nw, sh, sc):
    ms = jnp.mean(x * x, axis=-1, keepdims=True)
    y = x * lax.rsqrt(ms + EPS) * nw
    return y * (1.0 + sc) + sh


def _adaln_kernel(c_ref, w_ref, b_ref, o_ref):
    a = _silu(c_ref[...])
    o_ref[...] = jnp.dot(a, w_ref[...], preferred_element_type=F32, precision=HIGHEST) + b_ref[...]


def _adaln(c8, w_ada, b_ada):
    rows, d = c8.shape
    n = w_ada.shape[1]
    tn = n // N_MOD
    return pl.pallas_call(
        _adaln_kernel,
        grid=(N_MOD,),
        in_specs=[pl.BlockSpec((rows, d), lambda j: (0, 0)),
                  pl.BlockSpec((d, tn), lambda j: (0, j)),
                  pl.BlockSpec((1, tn), lambda j: (0, j))],
        out_specs=pl.BlockSpec((rows, tn), lambda j: (0, j)),
        out_shape=jax.ShapeDtypeStruct((rows, n), F32),
        compiler_params=_cparams(1),
        name="adaln",
    )(c8, w_ada, b_ada)


def _ffn_kernel(*refs, mod_rows, with_mix, mix_gate_row, with_final):
    it = iter(refs)
    x_ref = next(it)
    mod_ref = next(it)
    nw_ref = next(it)
    win_ref = next(it)
    wo_ref = next(it)
    if with_mix:
        ma_ref = next(it)
        mb_ref = next(it)
        wmix_ref = next(it)
    if with_final:
        fnw_ref = next(it)
    o_ref = next(it)

    sh_row, sc_row, g_row = mod_rows
    x = x_ref[...]
    if with_mix:
        half = ma_ref.shape[1]
        mixed = (_dot_tn(ma_ref[0].astype(BF16), wmix_ref[0:half, :])
                 + _dot(mb_ref[...], wmix_ref[half:2 * half, :]))
        x = x + mod_ref[0, mix_gate_row:mix_gate_row + 1, :] * mixed
    h = _norm_mod(x, nw_ref[...], mod_ref[0, sh_row:sh_row + 1, :], mod_ref[0, sc_row:sc_row + 1, :])
    hb = h.astype(BF16)
    d_ff = wo_ref.shape[0]
    n_chunks = d_ff // FF_CHUNK
    acc = None
    for j in range(n_chunks):
        c0 = j * FF_CHUNK
        g = _dot(hb, win_ref[:, c0:c0 + FF_CHUNK])
        u = _dot(hb, win_ref[:, d_ff + c0:d_ff + c0 + FF_CHUNK])
        act = (_silu(g) * u).astype(BF16)
        part = _dot(act, wo_ref[c0:c0 + FF_CHUNK, :])
        acc = part if acc is None else acc + part
    y = x + 0.5 * mod_ref[0, g_row:g_row + 1, :] * acc
    if with_final:
        ms = jnp.mean(y * y, axis=-1, keepdims=True)
        y = y * lax.rsqrt(ms + EPS) * fnw_ref[...]
    o_ref[...] = y


def _ffn(x2d, mod, nw, w_in_bf, w_out_bf, mod_rows, tiles_per_batch, mix=None, mix_gate_row=None, final_nw=None):
    t, d = x2d.shape
    tm = TOKEN_TILE
    d_ff = w_out_bf.shape[0]
    assert d_ff % FF_CHUNK == 0
    row = lambda i: (i, 0)
    const = lambda i: (0, 0)
    in_specs = [pl.BlockSpec((tm, d), row),
                pl.BlockSpec((1, N_MOD, d), lambda i: (i // tiles_per_batch, 0, 0)),
                pl.BlockSpec((1, d), const),
                pl.BlockSpec(w_in_bf.shape, const),
                pl.BlockSpec(w_out_bf.shape, const)]
    args = [x2d, mod, nw, w_in_bf, w_out_bf]
    if mix is not None:
        ma, mb, wmix = mix
        in_specs += [pl.BlockSpec((1, ma.shape[1], tm), lambda i: (i // tiles_per_batch, 0, i % tiles_per_batch)),
                     pl.BlockSpec((tm, mb.shape[1]), row),
                     pl.BlockSpec(wmix.shape, const)]
        args += [ma, mb, wmix]
    if final_nw is not None:
        in_specs.append(pl.BlockSpec((1, d), const))
        args.append(final_nw)
    kern = functools.partial(_ffn_kernel, mod_rows=mod_rows, with_mix=mix is not None,
                             mix_gate_row=mix_gate_row, with_final=final_nw is not None)
    return pl.pallas_call(
        kern,
        grid=(t // tm,),
        in_specs=in_specs,
        out_specs=pl.BlockSpec((tm, d), row),
        out_shape=jax.ShapeDtypeStruct((t, d), F32),
        compiler_params=_cparams(1),
        name="ffn_mix" if mix is not None else "ffn",
    )(*args)


QPAD_W = NSA_Q_HEADS * LANES
KV6_OFF = QPAD_W
ML_OFF = KV6_OFF + 6 * KV_WIDTH
MISC_OFF = ML_OFF + 4 * MLSTM_WIDTH
PROJ_W = MISC_OFF + LANES
MISC_I = N_GATES
MISC_F = N_GATES + MLSTM_HEADS


def _inproj_kernel(x_ref, mod_ref, nw_ref, w_ref, b_ref,
                   qpad_ref, kc_ref, vc_ref, ksel_ref, kwin_ref, vtsel_ref, vtwin_ref,
                   ml_ref, misc_ref, misct_ref):
    h = _norm_mod(x_ref[...], nw_ref[...], mod_ref[0, 3:4, :], mod_ref[0, 4:5, :])
    hb = h.astype(BF16)

    def proj(c0, width):
        return _dot(hb, w_ref[:, c0:c0 + width]) + b_ref[:, c0:c0 + width]

    scale = NSA_HEAD_DIM ** -0.5 * LOG2E
    for i in range(NSA_Q_HEADS):
        qpad_ref[:, i * LANES:(i + 1) * LANES] = (proj(i * LANES, LANES) * scale).astype(BF16)
    kc_ref[...] = proj(KV6_OFF, KV_WIDTH).astype(BF16)
    vc_ref[...] = proj(KV6_OFF + KV_WIDTH, KV_WIDTH).astype(BF16)
    ksel_ref[...] = proj(KV6_OFF + 2 * KV_WIDTH, KV_WIDTH).astype(BF16)
    vtsel_ref[0] = proj(KV6_OFF + 3 * KV_WIDTH, KV_WIDTH).T.astype(BF16)
    kwin_ref[...] = proj(KV6_OFF + 4 * KV_WIDTH, KV_WIDTH).astype(BF16)
    vtwin_ref[0] = proj(KV6_OFF + 5 * KV_WIDTH, KV_WIDTH).T.astype(BF16)
    for i in range(4):
        ml_ref[:, i * MLSTM_WIDTH:(i + 1) * MLSTM_WIDTH] = proj(ML_OFF + i * MLSTM_WIDTH, MLSTM_WIDTH)
    misc = proj(MISC_OFF, LANES)
    misc_ref[...] = misc
    misct_ref[0] = misc.T


def _inproj(x2d, mod, nw, w_pack, b_pack, batch, seq):
    t, d = x2d.shape
    tm = TOKEN_TILE
    tpb = seq // tm
    row = lambda i: (i, 0)
    const = lambda i: (0, 0)
    trow = lambda i: (i // tpb, 0, i % tpb)
    out_shapes = [
        jax.ShapeDtypeStruct((t, QPAD_W), BF16),
        jax.ShapeDtypeStruct((t, KV_WIDTH), BF16),
        jax.ShapeDtypeStruct((t, KV_WIDTH), BF16),
        jax.ShapeDtypeStruct((t, KV_WIDTH), BF16),
        jax.ShapeDtypeStruct((t, KV_WIDTH), BF16),
        jax.ShapeDtypeStruct((batch, KV_WIDTH, seq), BF16),
        jax.ShapeDtypeStruct((batch, KV_WIDTH, seq), BF16),
        jax.ShapeDtypeStruct((t, 4 * MLSTM_WIDTH), F32),
        jax.ShapeDtypeStruct((t, LANES), F32),
        jax.ShapeDtypeStruct((batch, LANES, seq), F32),
    ]
    out_specs = [
        pl.BlockSpec((tm, QPAD_W), row),
        pl.BlockSpec((tm, KV_WIDTH), row),
        pl.BlockSpec((tm, KV_WIDTH), row),
        pl.BlockSpec((tm, KV_WIDTH), row),
        pl.BlockSpec((tm, KV_WIDTH), row),
        pl.BlockSpec((1, KV_WIDTH, tm), trow),
        pl.BlockSpec((1, KV_WIDTH, tm), trow),
        pl.BlockSpec((tm, 4 * MLSTM_WIDTH), row),
        pl.BlockSpec((tm, LANES), row),
        pl.BlockSpec((1, LANES, tm), trow),
    ]
    return pl.pallas_call(
        _inproj_kernel,
        grid=(t // tm,),
        in_specs=[pl.BlockSpec((tm, d), row),
                  pl.BlockSpec((1, N_MOD, d), lambda i: (i // tpb, 0, 0)),
                  pl.BlockSpec((1, d), const),
                  pl.BlockSpec(w_pack.shape, const),
                  pl.BlockSpec(b_pack.shape, const)],
        out_specs=out_specs,
        out_shape=out_shapes,
        compiler_params=_cparams(1),
        name="inproj",
    )(x2d, mod, nw, w_pack, b_pack)


def _pack_inproj(w_in, b_in):
    d = w_in.shape[0]
    offs = np.cumsum([0, NSA_WIDTH] + [KV_WIDTH] * 6 + [N_GATES] + [MLSTM_WIDTH] * 4 + [MLSTM_HEADS] * 2)
    def pad_heads(a):
        a = a.reshape(a.shape[:-1] + (NSA_KV_HEADS, NSA_GROUP, NSA_HEAD_DIM))
        z = jnp.zeros_like(a[..., 0, :, :])
        lo = jnp.concatenate([a[..., 0, :, :], z], axis=-1)
        hi = jnp.concatenate([z, a[..., 1, :, :]], axis=-1)
        return jnp.stack([lo, hi], axis=-3).reshape(a.shape[:-3] + (QPAD_W,))

    wq_pad = pad_heads(w_in[:, :NSA_WIDTH])
    bq_pad = pad_heads(b_in[:NSA_WIDTH])
    g0, m0, i0 = int(offs[7]), int(offs[8]), int(offs[12])
    misc_w = jnp.concatenate([w_in[:, g0:g0 + N_GATES], w_in[:, i0:i0 + 2 * MLSTM_HEADS],
                              jnp.zeros((d, LANES - N_GATES - 2 * MLSTM_HEADS), w_in.dtype)], axis=1)
    misc_b = jnp.concatenate([b_in[g0:g0 + N_GATES], b_in[i0:i0 + 2 * MLSTM_HEADS],
                              jnp.zeros((LANES - N_GATES - 2 * MLSTM_HEADS,), b_in.dtype)])
    w_pack = jnp.concatenate([wq_pad, w_in[:, NSA_WIDTH:g0], w_in[:, m0:i0], misc_w], axis=1)
    b_pack = jnp.concatenate([bq_pad, b_in[NSA_WIDTH:g0], b_in[m0:i0], misc_b])
    assert w_pack.shape[1] == PROJ_W
    return w_pack.astype(BF16), b_pack.reshape(1, PROJ_W).astype(F32)


def _compress_kernel(xk_ref, xv_ref, wa_ref, wb_ref, pe_ref, w1_ref, w2_ref, kc_ref, vtc_ref):
    n_half = xk_ref.shape[1]

    def one(x_ref, s):
        x = x_ref[0]
        a = _dot(x, wa_ref[s])
        b = _dot(x, wb_ref[s])
        pe_term = jnp.dot(pe_ref[s], w1_ref[s], preferred_element_type=F32, precision=HIGHEST)[0:1, :]
        pe2 = jnp.concatenate([pe_term] * NSA_KV_HEADS, axis=1)
        pre = a + pltpu.roll(b, n_half - 1, 0) + pe2
        hid = 0.5 * pre * (1.0 + jnp.tanh(0.7978845608028654 * (pre + 0.044715 * pre * pre * pre)))
        return _dot(hid.astype(BF16), w2_ref[s])

    kc_ref[0] = one(xk_ref, 0).astype(BF16)
    vtc_ref[0] = one(xv_ref, 1).T.astype(BF16)


def _compress(xk, xv, wa, wb, pe8, w1, w2e):
    batch, n_half, width = xk.shape
    bsel = lambda b: (b, 0, 0)
    c3 = lambda b: (0, 0, 0)
    return pl.pallas_call(
        _compress_kernel,
        grid=(batch,),
        in_specs=[pl.BlockSpec((1, n_half, width), bsel), pl.BlockSpec((1, n_half, width), bsel),
                  pl.BlockSpec(wa.shape, c3), pl.BlockSpec(wb.shape, c3),
                  pl.BlockSpec(pe8.shape, c3), pl.BlockSpec(w1.shape, c3), pl.BlockSpec(w2e.shape, c3)],
        out_specs=[pl.BlockSpec((1, n_half, KV_WIDTH), bsel), pl.BlockSpec((1, KV_WIDTH, n_half), bsel)],
        out_shape=[jax.ShapeDtypeStruct((batch, n_half, KV_WIDTH), BF16),
                   jax.ShapeDtypeStruct((batch, KV_WIDTH, n_half), BF16)],
        compiler_params=_cparams(1),
        name="compress",
    )(xk, xv, wa, wb, pe8, w1, w2e)


def _pack_compress(pe, w1, w2):
    half = CMP_BLOCK // 2
    w1r = w1.reshape(CMP_BLOCK, NSA_HEAD_DIM, CMP_HIDDEN)

    eye = jnp.eye(NSA_KV_HEADS, dtype=w1.dtype)

    def expand(w_half):
        z = w_half[:, None, :, None, :] * eye[None, :, None, :, None]
        return z.reshape(half * KV_WIDTH, NSA_KV_HEADS * CMP_HIDDEN)

    wa = expand(w1r[:half]).astype(BF16)
    wb = expand(w1r[half:]).astype(BF16)
    w2e = (w2[None, :, None, :] * eye[:, None, :, None]).reshape(NSA_KV_HEADS * CMP_HIDDEN, KV_WIDTH).astype(BF16)
    pe8 = jnp.concatenate([pe.reshape(1, CMP_BLOCK * NSA_HEAD_DIM),
                           jnp.zeros((7, CMP_BLOCK * NSA_HEAD_DIM), pe.dtype)], axis=0)
    return wa, wb, pe8, w1, w2e


def _group_q(q_ref, h):
    return jnp.concatenate(
        [q_ref[0, :, (NSA_GROUP * h + g) * LANES:(NSA_GROUP * h + g + 1) * LANES] for g in range(NSA_GROUP)], axis=0)


LOG2E = 1.4426950408889634


def _slope2(head):
    return float(2.0 ** (-8.0 * (head + 1) / NSA_Q_HEADS)) * LOG2E


def _gate_row(gt_ref, head, branch):
    r = head * 3 + branch
    return _sigmoid(gt_ref[0, r:r + 1, :])


def _nsa_cmp_kernel(q_ref, kc_ref, vtc_ref, gt_ref, oc_ref, sel_ref, cnt_ref):
    ncp = kc_ref.shape[1]
    n_sel = sel_ref.shape[3]
    start = pl.program_id(1) * QT
    c_io = lax.broadcasted_iota(jnp.int32, (ncp, QT), 0)
    t_c = start + lax.broadcasted_iota(jnp.int32, (ncp, QT), 1)
    dist = t_c - (c_io * CMP_STRIDE + (CMP_BLOCK - 1))
    maskc = dist >= 0
    distf = dist.astype(F32)

    j_o = lax.broadcasted_iota(jnp.int32, (n_sel, ncp), 0)
    c_o = lax.broadcasted_iota(jnp.int32, (n_sel, ncp), 1)
    ov = ((c_o * CMP_STRIDE <= j_o * SEL_BLOCK + (SEL_BLOCK - 1))
          & (c_o * CMP_STRIDE + (CMP_BLOCK - 1) >= j_o * SEL_BLOCK))
    ovt = jnp.where(ov, 1.0, 0.0).astype(BF16)

    j_io = lax.broadcasted_iota(jnp.int32, (n_sel, QT), 0)
    t_j = start + lax.broadcasted_iota(jnp.int32, (n_sel, QT), 1)
    cur = t_j // SEL_BLOCK
    valid = j_io * SEL_BLOCK <= t_j
    forced = (j_io == 0) | (j_io == cur) | (j_io == cur - 1)
    ones8 = jnp.ones((8, QT), BF16)

    scores = []
    for h in range(NSA_KV_HEADS):
        qh = _group_q(q_ref, h)
        st = _dot_nt(kc_ref[0], qh)
        psum = jnp.zeros((ncp, QT), F32)
        p_chunks = []
        for g in range(NSA_GROUP):
            s = st[:, g * QT:(g + 1) * QT] - _slope2(NSA_GROUP * h + g) * distf
            s = jnp.where(maskc, s, NEG)
            m = jnp.max(s, axis=0, keepdims=True)
            e = jnp.where(maskc, jnp.exp2(s - m), 0.0)
            p = e / jnp.maximum(jnp.sum(e, axis=0, keepdims=True), 1e-30)
            psum = psum + p
            p_chunks.append(p.astype(BF16))
        pt = jnp.concatenate(p_chunks, axis=1)
        ot = _dot(vtc_ref[0, h * NSA_HEAD_DIM:(h + 1) * NSA_HEAD_DIM, :], pt)
        for g in range(NSA_GROUP):
            head = NSA_GROUP * h + g
            oc_ref[0, head] = ot[:, g * QT:(g + 1) * QT] * _gate_row(gt_ref, head, 0)

        hi = psum.astype(BF16)
        lo = (psum - hi.astype(F32)).astype(BF16)
        imp = _dot(ovt, hi) + _dot(ovt, lo)
        scores.append(jnp.where(valid, imp + jnp.where(forced, FORCE_BONUS, 0.0), -FORCE_BONUS))

    def pick(_, carry):
        out = []
        for sc, sel in carry:
            mx = jnp.max(sc, axis=0, keepdims=True)
            idx = jnp.min(jnp.where(sc == mx, j_io, n_sel), axis=0, keepdims=True)
            hit = j_io == idx
            out.append((jnp.where(hit, -3.0e38, sc), jnp.where(hit, 1.0, sel)))
        return tuple(out)

    picked = lax.fori_loop(0, min(SEL_TOPK, n_sel), pick,
                           tuple((sc, jnp.zeros((n_sel, QT), F32)) for sc in scores))
    for h in range(NSA_KV_HEADS):
        sel = picked[h][1]
        sel_ref[0, h, 0] = sel
        cnt = _dot_nt(ones8, sel.astype(BF16))
        cnt_ref[0, 0, h] = cnt[0:1, :]


def _nsa_cmp(qpad3, kc, vtc, gates_t, batch, seq):
    nb = seq // QT
    n_sel = seq // SEL_BLOCK
    ncp = kc.shape[1]
    return pl.pallas_call(
        _nsa_cmp_kernel,
        grid=(batch, nb),
        in_specs=[pl.BlockSpec((1, QT, QPAD_W), lambda b, i: (b, i, 0)),
                  pl.BlockSpec((1, ncp, KV_WIDTH), lambda b, i: (b, 0, 0)),
                  pl.BlockSpec((1, KV_WIDTH, ncp), lambda b, i: (b, 0, 0)),
                  pl.BlockSpec((1, N_GATES, QT), lambda b, i: (b, 0, i))],
        out_specs=[pl.BlockSpec((1, NSA_Q_HEADS, NSA_HEAD_DIM, QT), lambda b, i: (b, 0, 0, i)),
                   pl.BlockSpec((1, NSA_KV_HEADS, 1, n_sel, QT), lambda b, i: (b, 0, i, 0, 0)),
                   pl.BlockSpec((1, 1, NSA_KV_HEADS, 1, n_sel), lambda b, i: (b, i, 0, 0, 0))],
        out_shape=[jax.ShapeDtypeStruct((batch, NSA_Q_HEADS, NSA_HEAD_DIM, seq), F32),
                   jax.ShapeDtypeStruct((batch, NSA_KV_HEADS, nb, n_sel, QT), F32),
                   jax.ShapeDtypeStruct((batch, nb, NSA_KV_HEADS, 1, n_sel), F32)],
        compiler_params=_cparams(2),
        name="nsa_cmp",
    )(qpad3, kc, vtc, gates_t)


def _nsa_sel_kernel(flags_ref, q_ref, ksel_ref, vtsel_ref, kwin_ref, vtwin_ref, selm_ref, gt_ref, oc_ref,
                    o_ref, m_sel, l_sel, acc_sel, bias_ref, list_ref):
    b = pl.program_id(0)
    qb = pl.program_id(1)
    nb = pl.num_programs(1)
    start = qb * QT
    rows = NSA_GROUP * QT
    k_io = lax.broadcasted_iota(jnp.int32, (QT, QT), 0)
    d0 = lax.broadcasted_iota(jnp.int32, (QT, QT), 1) - k_io
    first_half = k_io < SEL_BLOCK
    masked_shift = -(1 << 24)

    @pl.when((b == 0) & (qb == 0))
    def _():
        kf = k_io.astype(F32)
        for head in range(NSA_Q_HEADS):
            bias_ref[head] = _slope2(head) * kf

        def clear(i, carry):
            for h in range(NSA_KV_HEADS):
                list_ref[h, i] = 0
            return carry

        lax.fori_loop(0, nb, clear, 0)

    def step(h, qh, k_ref, vt_ref, tiles, shifts, extra_masks, window, state):
        fresh = state is None
        if not fresh:
            m_ref, l_ref, acc_ref = state
        n_t = len(tiles)
        offs = [pl.multiple_of(t * QT, QT) for t in tiles]
        sts = [_dot_nt(k_ref[0, pl.ds(off, QT), :], qh) for off in offs]
        masks = []
        for u in range(n_t):
            dist = d0 + shifts[u]
            mk = dist >= 0
            if window:
                mk = mk & (dist < WINDOW)
            if extra_masks is not None:
                mk = mk & extra_masks[u]
            masks.append(mk)
        tile_pos = [(t * QT).astype(F32) for t in tiles]
        e_rows = [[] for _ in range(n_t)]
        alphas = []
        l_rows = []
        for g in range(NSA_GROUP):
            sl = slice(g * QT, (g + 1) * QT)
            head = NSA_GROUP * h + g
            bias = bias_ref[head]
            offs_c = [_slope2(head) * tp for tp in tile_pos]
            ss = [jnp.where(masks[u], sts[u][:, sl] + bias, NEG) for u in range(n_t)]
            mx = None
            for u in range(n_t):
                cand = jnp.max(ss[u], axis=0, keepdims=True) + offs_c[u]
                mx = cand if mx is None else jnp.maximum(mx, cand)
            if fresh:
                m_new = mx
            else:
                m_old = m_ref[h, :, sl]
                m_new = jnp.maximum(m_old, mx)
                alphas.append(jnp.exp2(m_old - m_new))
                m_ref[h, :, sl] = m_new
            m_use = jnp.where(m_new < 0.5 * NEG, 0.0, m_new)
            lsum = None
            for u in range(n_t):
                e = jnp.exp2(ss[u] - (m_use - offs_c[u]))
                part = jnp.sum(e, axis=0, keepdims=True)
                lsum = part if lsum is None else lsum + part
                e_rows[u].append(e.astype(BF16))
            if fresh:
                l_rows.append(lsum)
            else:
                l_ref[h, :, sl] = alphas[g] * l_ref[h, :, sl] + lsum
        et = jnp.concatenate([jnp.concatenate(r, axis=1) for r in e_rows], axis=0)
        vt = jnp.concatenate(
            [vt_ref[0, h * NSA_HEAD_DIM:(h + 1) * NSA_HEAD_DIM, pl.ds(off, QT)] for off in offs], axis=1)
        pv = _dot(vt, et)
        if fresh:
            return jnp.concatenate(l_rows, axis=1), pv
        acc_ref[h] = jnp.concatenate(alphas, axis=1) * acc_ref[h] + pv
        return None

    def gated(h, branch, l, acc):
        inv = 1.0 / jnp.maximum(l, 1e-30)
        out = []
        for g in range(NSA_GROUP):
            sl = slice(g * QT, (g + 1) * QT)
            out.append(acc[:, sl] * (inv[:, sl] * _gate_row(gt_ref, NSA_GROUP * h + g, branch)))
        return out

    heads = range(NSA_KV_HEADS)
    qhs = [_group_q(q_ref, h) for h in heads]

    n_tiles = []
    for h in heads:
        fbase = ((b * nb + qb) * NSA_KV_HEADS + h) * nb

        def build(p, n, h=h, fbase=fbase):
            hit = flags_ref[fbase + p] > 0

            @pl.when(hit)
            def _():
                list_ref[h, n] = p
            return n + hit.astype(jnp.int32)

        n_tiles.append(lax.fori_loop(0, qb + 1, build, 0))

    m_sel[...] = jnp.full(m_sel.shape, NEG, F32)
    l_sel[...] = jnp.zeros(l_sel.shape, F32)
    acc_sel[...] = jnp.zeros(acc_sel.shape, F32)

    def sel_body(i, carry):
        for h in heads:
            tiles, shifts, chosen = [], [], []
            for u in range(SEL_UNROLL):
                idx = i * SEL_UNROLL + u
                live = idx < n_tiles[h]
                t = jnp.where(live, list_ref[h, jnp.minimum(idx, qb)], 0)
                tiles.append(t)
                shifts.append(jnp.where(live, start - t * QT, masked_shift))
                mrow = selm_ref[0, h, 0, pl.ds(2 * t, 2), :]
                chosen.append(jnp.where(first_half, mrow[0:1, :], mrow[1:2, :]) > 0.5)
            step(h, qhs[h], ksel_ref, vtsel_ref, tiles, shifts, chosen, window=False,
                 state=(m_sel, l_sel, acc_sel))
        return carry

    n_steps = (jnp.maximum(n_tiles[0], n_tiles[1]) + SEL_UNROLL - 1) // SEL_UNROLL
    lax.fori_loop(0, n_steps, sel_body, 0)

    tiles, shifts = [], []
    for u in range(WINDOW // QT + 1):
        raw = qb - WINDOW // QT + u
        t = jnp.maximum(raw, 0)
        tiles.append(t)
        shifts.append(jnp.where(raw >= 0, start - t * QT, masked_shift))
    for h in heads:
        l_w, acc_w = step(h, qhs[h], kwin_ref, vtwin_ref, tiles, shifts, None, window=True, state=None)
        o_s = gated(h, 1, l_sel[h], acc_sel[h])
        o_w = gated(h, 2, l_w, acc_w)
        for g in range(NSA_GROUP):
            head = NSA_GROUP * h + g
            o_ref[0, head] = oc_ref[0, head] + o_s[g] + o_w[g]


def _nsa_sel(flags, qpad3, ksel, vtsel, kwin, vtwin, selm, gates_t, oc, batch, seq):
    nb = seq // QT
    res = lambda b, i, f: (b, 0, 0)
    grid_spec = pltpu.PrefetchScalarGridSpec(
        num_scalar_prefetch=1,
        grid=(batch, nb),
        in_specs=[pl.BlockSpec((1, QT, QPAD_W), lambda b, i, f: (b, i, 0)),
                  pl.BlockSpec((1, seq, KV_WIDTH), res),
                  pl.BlockSpec((1, KV_WIDTH, seq), res),
                  pl.BlockSpec((1, seq, KV_WIDTH), res),
                  pl.BlockSpec((1, KV_WIDTH, seq), res),
                  pl.BlockSpec((1, NSA_KV_HEADS, 1, 2 * nb, QT), lambda b, i, f: (b, 0, i, 0, 0)),
                  pl.BlockSpec((1, N_GATES, QT), lambda b, i, f: (b, 0, i)),
                  pl.BlockSpec((1, NSA_Q_HEADS, NSA_HEAD_DIM, QT), lambda b, i, f: (b, 0, 0, i))],
        out_specs=pl.BlockSpec((1, NSA_Q_HEADS, NSA_HEAD_DIM, QT), lambda b, i, f: (b, 0, 0, i)),
        scratch_shapes=[pltpu.VMEM((NSA_KV_HEADS, 1, NSA_GROUP * QT), F32),
                        pltpu.VMEM((NSA_KV_HEADS, 1, NSA_GROUP * QT), F32),
                        pltpu.VMEM((NSA_KV_HEADS, NSA_HEAD_DIM, NSA_GROUP * QT), F32),
                        pltpu.VMEM((NSA_Q_HEADS, QT, QT), F32),
                        pltpu.SMEM((NSA_KV_HEADS, nb), jnp.int32)],
    )
    return pl.pallas_call(
        _nsa_sel_kernel,
        grid_spec=grid_spec,
        out_shape=jax.ShapeDtypeStruct((batch, NSA_Q_HEADS, NSA_HEAD_DIM, seq), F32),
        compiler_params=_cparams(2),
        name="nsa_sel",
    )(flags, qpad3, ksel, vtsel, kwin, vtwin, selm, gates_t, oc)


def _mlstm_kernel(q_ref, k_ref, v_ref, og_ref, misc_ref, misct_ref, cw_ref, cb_ref, nw_ref,
                  o_ref, xq_ref, xk_ref, c_ref, n_ref, m_ref):
    tm = q_ref.shape[0]
    L = MLSTM_L
    hd = MLSTM_HEAD_DIM
    halo = 8

    @pl.when(pl.program_id(1) == 0)
    def _():
        xq_ref[0:halo, :] = jnp.zeros((halo, MLSTM_WIDTH), F32)
        xk_ref[0:halo, :] = jnp.zeros((halo, MLSTM_WIDTH), F32)
        c_ref[...] = jnp.zeros_like(c_ref)
        n_ref[...] = jnp.zeros_like(n_ref)
        m_ref[...] = jnp.zeros_like(m_ref)

    def conv(x_ref, buf_ref, col0):
        buf_ref[halo:halo + tm, :] = x_ref[...]
        y = cb_ref[:, col0:col0 + MLSTM_WIDTH]
        for kk in range(CONV_WIDTH):
            r0 = halo - (CONV_WIDTH - 1) + kk
            y = y + cw_ref[kk:kk + 1, col0:col0 + MLSTM_WIDTH] * buf_ref[r0:r0 + tm, :]
        tail = buf_ref[tm:tm + halo, :]
        buf_ref[0:halo, :] = tail
        return _silu(y)

    qc = conv(q_ref, xq_ref, 0)
    kc = conv(k_ref, xk_ref, MLSTM_WIDTH) * (hd ** -0.5)

    misc = misc_ref[...]
    misct = misct_ref[0]
    lf_c = -jnp.log(1.0 + jnp.exp(-misc))
    lf_r = -jnp.log(1.0 + jnp.exp(-misct))
    r_io = lax.broadcasted_iota(jnp.int32, (tm, tm), 0)
    c_io = lax.broadcasted_iota(jnp.int32, (tm, tm), 1)
    same = (r_io // L) == (c_io // L)
    tril_blk = jnp.where(same & (c_io <= r_io), 1.0, 0.0)
    triu_blk = jnp.where(same & (r_io <= c_io), 1.0, 0.0)
    b_c = jnp.dot(tril_blk, lf_c, preferred_element_type=F32, precision=HIGHEST)
    b_r = jnp.dot(lf_r, triu_blk, preferred_element_type=F32, precision=HIGHEST)

    tri = lax.broadcasted_iota(jnp.int32, (L, L), 1) <= lax.broadcasted_iota(jnp.int32, (L, L), 0)

    for ci in range(tm // L):
        r0 = ci * L
        for h in range(MLSTM_HEADS):
            cs = slice(h * hd, (h + 1) * hd)
            q = qc[r0:r0 + L, cs]
            k = kc[r0:r0 + L, cs]
            v = v_ref[r0:r0 + L, cs]
            bcol = b_c[r0:r0 + L, MISC_F + h:MISC_F + h + 1]
            licol = misc[r0:r0 + L, MISC_I + h:MISC_I + h + 1]
            brow = b_r[MISC_F + h:MISC_F + h + 1, r0:r0 + L]
            lirow = misct[MISC_I + h:MISC_I + h + 1, r0:r0 + L]
            m_prev = m_ref[h]
            cmat = c_ref[h]
            nrow = n_ref[h]

            dmat = jnp.where(tri, bcol - brow + lirow, NEG)
            m_inter = bcol + m_prev
            m_t = jnp.maximum(jnp.max(dmat, axis=-1, keepdims=True), m_inter)
            qb16 = q.astype(BF16)
            kb16 = k.astype(BF16)
            vb16 = v.astype(BF16)
            w = jnp.exp(dmat - m_t) * _dot_nt(qb16, kb16)
            decay = jnp.exp(m_inter - m_t)
            num = _dot(w.astype(BF16), vb16) + decay * _dot(qb16, cmat.astype(BF16))
            den = jnp.sum(w, axis=-1, keepdims=True) + decay * jnp.sum(q * nrow, axis=-1, keepdims=True)
            hout = num / jnp.maximum(jnp.abs(den), jnp.exp(-m_t))

            b_last = brow[:, L - 1:L]
            gcol = b_last - bcol + licol
            m_new = jnp.maximum(b_last + m_prev, jnp.max(gcol, axis=0, keepdims=True))
            wk = jnp.exp(gcol - m_new)
            d_c = jnp.exp(b_last + m_prev - m_new)
            kw = wk * k
            c_ref[h] = d_c * cmat + _dot_tn(kw.astype(BF16), vb16)
            n_ref[h] = d_c * nrow + jnp.sum(kw, axis=0, keepdims=True)
            m_ref[h] = m_new

            hn = hout * lax.rsqrt(jnp.mean(hout * hout, axis=-1, keepdims=True) + EPS)
            hn = hn * nw_ref[:, cs]
            o_ref[r0:r0 + L, cs] = (hn * _sigmoid(og_ref[r0:r0 + L, cs])).astype(o_ref.dtype)


def _mlstm(ml, misc, misc_t, conv_w, conv_b, norm_w, batch, seq):
    t = ml.shape[0]
    tm = TOKEN_TILE
    tpb = seq // tm
    col = lambda j: (lambda b, i: (b * tpb + i, j))
    const = lambda b, i: (0, 0)
    return pl.pallas_call(
        _mlstm_kernel,
        grid=(batch, tpb),
        in_specs=[pl.BlockSpec((tm, MLSTM_WIDTH), col(0)),
                  pl.BlockSpec((tm, MLSTM_WIDTH), col(1)),
                  pl.BlockSpec((tm, MLSTM_WIDTH), col(2)),
                  pl.BlockSpec((tm, MLSTM_WIDTH), col(3)),
                  pl.BlockSpec((tm, LANES), col(0)),
                  pl.BlockSpec((1, LANES, tm), lambda b, i: (b, 0, i)),
                  pl.BlockSpec(conv_w.shape, const),
                  pl.BlockSpec(conv_b.shape, const),
                  pl.BlockSpec(norm_w.shape, const)],
        out_specs=pl.BlockSpec((tm, MLSTM_WIDTH), col(0)),
        out_shape=jax.ShapeDtypeStruct((t, MLSTM_WIDTH), BF16),
        scratch_shapes=[pltpu.VMEM((tm + 8, MLSTM_WIDTH), F32),
                        pltpu.VMEM((tm + 8, MLSTM_WIDTH), F32),
                        pltpu.VMEM((MLSTM_HEADS, MLSTM_HEAD_DIM, MLSTM_HEAD_DIM), F32),
                        pltpu.VMEM((MLSTM_HEADS, 1, MLSTM_HEAD_DIM), F32),
                        pltpu.VMEM((MLSTM_HEADS, 1, 1), F32)],
        compiler_params=_cparams(2),
        name="mlstm",
    )(ml, ml, ml, ml, misc, misc_t, conv_w, conv_b, norm_w)


def _layer(x2d, c, batch, seq, w_ada, b_ada, norm_ffn1_w, ffn1_w_in, ffn1_w_out, norm_mix_w, w_in, b_in,
           cmp_k_pe, cmp_k_w1, cmp_k_w2, cmp_v_pe, cmp_v_w1, cmp_v_w2, conv_w, conv_b, mlstm_norm_w, w_out,
           norm_ffn2_w, ffn2_w_in, ffn2_w_out, final_nw):
    d = x2d.shape[1]
    tpb = seq // TOKEN_TILE
    c8 = jnp.concatenate([c, jnp.zeros((8 - batch % 8 if batch % 8 else 0, d), c.dtype)], axis=0)
    mod = _adaln(c8, w_ada, b_ada.reshape(1, -1))[:batch].reshape(batch, N_MOD, d)

    x1 = _ffn(x2d, mod, norm_ffn1_w.reshape(1, d), ffn1_w_in.astype(BF16), ffn1_w_out.astype(BF16),
              (0, 1, 2), tpb)

    w_pack, b_pack = _pack_inproj(w_in, b_in)
    (qpad, kc, vc, ksel, kwin, vtsel, vtwin, ml, misc, misc_t) = _inproj(
        x1, mod, norm_mix_w.reshape(1, d), w_pack, b_pack, batch, seq)

    n_half = seq // CMP_STRIDE
    pk = _pack_compress(cmp_k_pe, cmp_k_w1, cmp_k_w2)
    pv = _pack_compress(cmp_v_pe, cmp_v_w1, cmp_v_w2)
    stacked = [jnp.stack([a, bb]) for a, bb in zip(pk, pv)]
    kcmp, vtcmp = _compress(kc.reshape(batch, n_half, CMP_STRIDE * KV_WIDTH),
                            vc.reshape(batch, n_half, CMP_STRIDE * KV_WIDTH), *stacked)

    nb = seq // QT
    qpad3 = qpad.reshape(batch, seq, QPAD_W)
    gates_t = misc_t[:, :N_GATES, :]
    oc, sel, cnt = _nsa_cmp(qpad3, kcmp, vtcmp, gates_t, batch, seq)
    pair = cnt.reshape(batch, nb, NSA_KV_HEADS, nb, 2).sum(axis=-1)
    flags = (pair > 0.5).astype(jnp.int32).reshape(-1)
    o_t = _nsa_sel(flags, qpad3, ksel.reshape(batch, seq, KV_WIDTH), vtsel, kwin.reshape(batch, seq, KV_WIDTH),
                   vtwin, sel, gates_t, oc, batch, seq)
    o_nsa = o_t.reshape(batch, NSA_WIDTH, seq)

    o_ml = _mlstm(ml, misc, misc_t, conv_w, conv_b.reshape(1, -1), mlstm_norm_w.reshape(1, -1), batch, seq)

    return _ffn(x1, mod, norm_ffn2_w.reshape(1, d), ffn2_w_in.astype(BF16), ffn2_w_out.astype(BF16),
                (6, 7, 8), tpb, mix=(o_nsa, o_ml, w_out.astype(BF16)), mix_gate_row=5, final_nw=final_nw)


def kernel(x, c, w_ada, b_ada, norm_ffn1_w, ffn1_w_in, ffn1_w_out, norm_mix_w, w_in, b_in, cmp_k_pe, cmp_k_w1, cmp_k_w2, cmp_v_pe, cmp_v_w1, cmp_v_w2, conv_w, conv_b, mlstm_norm_w, w_out, norm_ffn2_w, ffn2_w_in, ffn2_w_out, final_norm_w):
    batch, seq, d = x.shape
    depth = w_ada.shape[0]
    assert depth == 1 and seq % TOKEN_TILE == 0 and seq // SEL_BLOCK >= SEL_TOPK
    y = _layer(x.reshape(batch * seq, d), c, batch, seq, w_ada[0], b_ada[0], norm_ffn1_w[0], ffn1_w_in[0],
               ffn1_w_out[0], norm_mix_w[0], w_in[0], b_in[0], cmp_k_pe[0], cmp_k_w1[0], cmp_k_w2[0],
               cmp_v_pe[0], cmp_v_w1[0], cmp_v_w2[0], conv_w[0], conv_b[0], mlstm_norm_w[0], w_out[0],
               norm_ffn2_w[0], ffn2_w_in[0], ffn2_w_out[0], final_norm_w.reshape(1, d))
    return y.reshape(batch, seq, d)
```

```python
import functools

import numpy as np
import jax
import jax.numpy as jnp
from jax import lax
from jax.experimental import pallas as pl
from jax.experimental.pallas import tpu as pltpu

NSA_Q_HEADS = 8
NSA_KV_HEADS = 2
NSA_GROUP = NSA_Q_HEADS // NSA_KV_HEADS
NSA_HEAD_DIM = 64
CMP_BLOCK = 32
CMP_STRIDE = 16
CMP_HIDDEN = 128
SEL_BLOCK = 64
SEL_TOPK = 16
WINDOW = 512
FORCE_BONUS = 1.0e4
MLSTM_HEADS = 4
MLSTM_HEAD_DIM = 128
CONV_WIDTH = 4
D_FF = 2816
N_MOD = 9
EPS = 1e-6
NEG = -1e30

NSA_WIDTH = NSA_Q_HEADS * NSA_HEAD_DIM
KV_WIDTH = NSA_KV_HEADS * NSA_HEAD_DIM
MLSTM_WIDTH = MLSTM_HEADS * MLSTM_HEAD_DIM
N_GATES = 3 * NSA_Q_HEADS

LANES = 128
QT = 128
TOKEN_TILE = 512
FF_CHUNK = 256
MLSTM_L = 128
SEL_UNROLL = 6
VT_ROWS = 80
VMEM_LIMIT = 56 * 1024 * 1024

F32 = jnp.float32
BF16 = jnp.bfloat16
HIGHEST = lax.Precision.HIGHEST
LOG2E = 1.4426950408889634


def _cparams(n_axes):
    return pltpu.CompilerParams(dimension_semantics=("arbitrary",) * n_axes,
                                vmem_limit_bytes=VMEM_LIMIT)


def _dot(a, b):
    return jnp.dot(a, b, preferred_element_type=F32)


def _dot_nt(a, b):
    return lax.dot_general(a, b, (((1,), (1,)), ((), ())), preferred_element_type=F32)


def _dot_tn(a, b):
    return lax.dot_general(a, b, (((0,), (0,)), ((), ())), preferred_element_type=F32)


def _sigmoid(x):
    return 1.0 / (1.0 + jnp.exp(-x))


def _silu(x):
    return x * _sigmoid(x)


def _norm_mod(x, nw, sh, sc):
    ms = jnp.mean(x * x, axis=-1, keepdims=True)
    y = x * lax.rsqrt(ms + EPS) * nw
    return y * (1.0 + sc) + sh


def _adaln_kernel(c_ref, w_ref, b_ref, o_ref):
    a = _silu(c_ref[...])
    o_ref[...] = jnp.dot(a, w_ref[...], preferred_element_type=F32, precision=HIGHEST) + b_ref[...]


def _adaln(c8, w_ada, b_ada):
    rows, d = c8.shape
    n = w_ada.shape[1]
    tn = n // N_MOD
    return pl.pallas_call(
        _adaln_kernel,
        grid=(N_MOD,),
        in_specs=[pl.BlockSpec((rows, d), lambda j: (0, 0)),
                  pl.BlockSpec((d, tn), lambda j: (0, j)),
                  pl.BlockSpec((1, tn), lambda j: (0, j))],
        out_specs=pl.BlockSpec((rows, tn), lambda j: (0, j)),
        out_shape=jax.ShapeDtypeStruct((rows, n), F32),
        compiler_params=_cparams(1),
        name="adaln",
    )(c8, w_ada, b_ada)


def _ffn_kernel(*refs, mod_rows, with_mix, mix_gate_row, with_final):
    it = iter(refs)
    x_ref = next(it)
    mod_ref = next(it)
    nw_ref = next(it)
    win_ref = next(it)
    wo_ref = next(it)
    if with_mix:
        ma_ref = next(it)
        mb_ref = next(it)
        wmix_ref = next(it)
    if with_final:
        fnw_ref = next(it)
    o_ref = next(it)

    sh_row, sc_row, g_row = mod_rows
    x = x_ref[...]
    if with_mix:
        half = ma_ref.shape[1]
        mixed = (_dot_tn(ma_ref[0].astype(BF16), wmix_ref[0:half, :])
                 + _dot(mb_ref[...], wmix_ref[half:2 * half, :]))
        x = x + mod_ref[0, mix_gate_row:mix_gate_row + 1, :] * mixed
    h = _norm_mod(x, nw_ref[...], mod_ref[0, sh_row:sh_row + 1, :], mod_ref[0, sc_row:sc_row + 1, :])
    hb = h.astype(BF16)
    d_ff = wo_ref.shape[0]
    n_chunks = d_ff // FF_CHUNK
    acc = None
    for j in range(n_chunks):
        c0 = j * FF_CHUNK
        g = _dot(hb, win_ref[:, c0:c0 + FF_CHUNK])
        u = _dot(hb, win_ref[:, d_ff + c0:d_ff + c0 + FF_CHUNK])
        act = (_silu(g) * u).astype(BF16)
        part = _dot(act, wo_ref[c0:c0 + FF_CHUNK, :])
        acc = part if acc is None else acc + part
    y = x + 0.5 * mod_ref[0, g_row:g_row + 1, :] * acc
    if with_final:
        ms = jnp.mean(y * y, axis=-1, keepdims=True)
        y = y * lax.rsqrt(ms + EPS) * fnw_ref[...]
    o_ref[...] = y


def _ffn(x2d, mod, nw, w_in_bf, w_out_bf, mod_rows, tiles_per_batch, mix=None, mix_gate_row=None, final_nw=None):
    t, d = x2d.shape
    tm = TOKEN_TILE
    d_ff = w_out_bf.shape[0]
    assert d_ff % FF_CHUNK == 0
    row = lambda i: (i, 0)
    const = lambda i: (0, 0)
    in_specs = [pl.BlockSpec((tm, d), row),
                pl.BlockSpec((1, N_MOD, d), lambda i: (i // tiles_per_batch, 0, 0)),
                pl.BlockSpec((1, d), const),
                pl.BlockSpec(w_in_bf.shape, const),
                pl.BlockSpec(w_out_bf.shape, const)]
    args = [x2d, mod, nw, w_in_bf, w_out_bf]
    if mix is not None:
        ma, mb, wmix = mix
        in_specs += [pl.BlockSpec((1, ma.shape[1], tm), lambda i: (i // tiles_per_batch, 0, i % tiles_per_batch)),
                     pl.BlockSpec((tm, mb.shape[1]), row),
                     pl.BlockSpec(wmix.shape, const)]
        args += [ma, mb, wmix]
    if final_nw is not None:
        in_specs.append(pl.BlockSpec((1, d), const))
        args.append(final_nw)
    kern = functools.partial(_ffn_kernel, mod_rows=mod_rows, with_mix=mix is not None,
                             mix_gate_row=mix_gate_row, with_final=final_nw is not None)
    return pl.pallas_call(
        kern,
        grid=(t // tm,),
        in_specs=in_specs,
        out_specs=pl.BlockSpec((tm, d), row),
        out_shape=jax.ShapeDtypeStruct((t, d), F32),
        compiler_params=_cparams(1),
        name="ffn_mix" if mix is not None else "ffn",
    )(*args)


QPAD_W = NSA_Q_HEADS * LANES
KV6_OFF = QPAD_W
ML_OFF = KV6_OFF + 6 * KV_WIDTH
MISC_OFF = ML_OFF + 4 * MLSTM_WIDTH
PROJ_W = MISC_OFF + LANES
MISC_I = N_GATES
MISC_F = N_GATES + MLSTM_HEADS


def _vt_with_ones(v):
    vt = v.T
    n = vt.shape[1]
    pad = jnp.where(lax.broadcasted_iota(jnp.int32, (VT_ROWS - NSA_HEAD_DIM, n), 0) == 0, 1.0, 0.0)
    parts = []
    for h in range(NSA_KV_HEADS):
        parts += [vt[h * NSA_HEAD_DIM:(h + 1) * NSA_HEAD_DIM, :], pad]
    return jnp.concatenate(parts, axis=0).astype(BF16)


def _inproj_kernel(x_ref, mod_ref, nw_ref, w_ref, b_ref,
                   qpad_ref, kc_ref, vc_ref, ksel_ref, kwin_ref, vtsel_ref, vtwin_ref,
                   ml_ref, misc_ref, misct_ref):
    h = _norm_mod(x_ref[...], nw_ref[...], mod_ref[0, 3:4, :], mod_ref[0, 4:5, :])
    hb = h.astype(BF16)

    def proj(c0, width):
        return _dot(hb, w_ref[:, c0:c0 + width]) + b_ref[:, c0:c0 + width]

    scale = NSA_HEAD_DIM ** -0.5 * LOG2E
    for i in range(NSA_Q_HEADS):
        qpad_ref[0, i * LANES:(i + 1) * LANES, :] = (proj(i * LANES, LANES) * scale).T.astype(BF16)
    kc_ref[...] = proj(KV6_OFF, KV_WIDTH).astype(BF16)
    vc_ref[...] = proj(KV6_OFF + KV_WIDTH, KV_WIDTH).astype(BF16)
    ksel_ref[...] = proj(KV6_OFF + 2 * KV_WIDTH, KV_WIDTH).astype(BF16)
    vtsel_ref[0] = _vt_with_ones(proj(KV6_OFF + 3 * KV_WIDTH, KV_WIDTH))
    kwin_ref[...] = proj(KV6_OFF + 4 * KV_WIDTH, KV_WIDTH).astype(BF16)
    vtwin_ref[0] = _vt_with_ones(proj(KV6_OFF + 5 * KV_WIDTH, KV_WIDTH))
    for i in range(4):
        ml_ref[:, i * MLSTM_WIDTH:(i + 1) * MLSTM_WIDTH] = proj(ML_OFF + i * MLSTM_WIDTH, MLSTM_WIDTH)
    misc = proj(MISC_OFF, LANES)
    misc_ref[...] = misc
    misct_ref[0] = misc.T


def _inproj(x2d, mod, nw, w_pack, b_pack, batch, seq):
    t, d = x2d.shape
    tm = TOKEN_TILE
    tpb = seq // tm
    row = lambda i: (i, 0)
    const = lambda i: (0, 0)
    trow = lambda i: (i // tpb, 0, i % tpb)
    vt_rows = NSA_KV_HEADS * VT_ROWS
    out_shapes = [
        jax.ShapeDtypeStruct((batch, QPAD_W, seq), BF16),
        jax.ShapeDtypeStruct((t, KV_WIDTH), BF16),
        jax.ShapeDtypeStruct((t, KV_WIDTH), BF16),
        jax.ShapeDtypeStruct((t, KV_WIDTH), BF16),
        jax.ShapeDtypeStruct((t, KV_WIDTH), BF16),
        jax.ShapeDtypeStruct((batch, vt_rows, seq), BF16),
        jax.ShapeDtypeStruct((batch, vt_rows, seq), BF16),
        jax.ShapeDtypeStruct((t, 4 * MLSTM_WIDTH), F32),
        jax.ShapeDtypeStruct((t, LANES), F32),
        jax.ShapeDtypeStruct((batch, LANES, seq), F32),
    ]
    out_specs = [
        pl.BlockSpec((1, QPAD_W, tm), trow),
        pl.BlockSpec((tm, KV_WIDTH), row),
        pl.BlockSpec((tm, KV_WIDTH), row),
        pl.BlockSpec((tm, KV_WIDTH), row),
        pl.BlockSpec((tm, KV_WIDTH), row),
        pl.BlockSpec((1, vt_rows, tm), trow),
        pl.BlockSpec((1, vt_rows, tm), trow),
        pl.BlockSpec((tm, 4 * MLSTM_WIDTH), row),
        pl.BlockSpec((tm, LANES), row),
        pl.BlockSpec((1, LANES, tm), trow),
    ]
    return pl.pallas_call(
        _inproj_kernel,
        grid=(t // tm,),
        in_specs=[pl.BlockSpec((tm, d), row),
                  pl.BlockSpec((1, N_MOD, d), lambda i: (i // tpb, 0, 0)),
                  pl.BlockSpec((1, d), const),
                  pl.BlockSpec(w_pack.shape, const),
                  pl.BlockSpec(b_pack.shape, const)],
        out_specs=out_specs,
        out_shape=out_shapes,
        compiler_params=_cparams(1),
        name="inproj",
    )(x2d, mod, nw, w_pack, b_pack)


def _pack_inproj(w_in, b_in):
    d = w_in.shape[0]
    offs = np.cumsum([0, NSA_WIDTH] + [KV_WIDTH] * 6 + [N_GATES] + [MLSTM_WIDTH] * 4 + [MLSTM_HEADS] * 2)

    def pad_heads(a):
        a = a.reshape(a.shape[:-1] + (NSA_KV_HEADS, NSA_GROUP, NSA_HEAD_DIM))
        z = jnp.zeros_like(a[..., 0, :, :])
        lo = jnp.concatenate([a[..., 0, :, :], z], axis=-1)
        hi = jnp.concatenate([z, a[..., 1, :, :]], axis=-1)
        return jnp.stack([lo, hi], axis=-3).reshape(a.shape[:-3] + (QPAD_W,))

    wq_pad = pad_heads(w_in[:, :NSA_WIDTH])
    bq_pad = pad_heads(b_in[:NSA_WIDTH])
    g0, m0, i0 = int(offs[7]), int(offs[8]), int(offs[12])
    misc_w = jnp.concatenate([w_in[:, g0:g0 + N_GATES], w_in[:, i0:i0 + 2 * MLSTM_HEADS],
                              jnp.zeros((d, LANES - N_GATES - 2 * MLSTM_HEADS), w_in.dtype)], axis=1)
    misc_b = jnp.concatenate([b_in[g0:g0 + N_GATES], b_in[i0:i0 + 2 * MLSTM_HEADS],
                              jnp.zeros((LANES - N_GATES - 2 * MLSTM_HEADS,), b_in.dtype)])
    w_pack = jnp.concatenate([wq_pad, w_in[:, NSA_WIDTH:g0], w_in[:, m0:i0], misc_w], axis=1)
    b_pack = jnp.concatenate([bq_pad, b_in[NSA_WIDTH:g0], b_in[m0:i0], misc_b])
    assert w_pack.shape[1] == PROJ_W
    return w_pack.astype(BF16), b_pack.reshape(1, PROJ_W).astype(F32)


def _compress_kernel(xk_ref, xv_ref, wa_ref, wb_ref, pe_ref, w1_ref, w2_ref, kc_ref, vtc_ref):
    n_half = xk_ref.shape[1]

    def one(x_ref, s):
        x = x_ref[0]
        a = _dot(x, wa_ref[s])
        b = _dot(x, wb_ref[s])
        pe_term = jnp.dot(pe_ref[s], w1_ref[s], preferred_element_type=F32, precision=HIGHEST)[0:1, :]
        pe2 = jnp.concatenate([pe_term] * NSA_KV_HEADS, axis=1)
        pre = a + pltpu.roll(b, n_half - 1, 0) + pe2
        hid = 0.5 * pre * (1.0 + jnp.tanh(0.7978845608028654 * (pre + 0.044715 * pre * pre * pre)))
        return _dot(hid.astype(BF16), w2_ref[s])

    kc_ref[0] = one(xk_ref, 0).astype(BF16)
    vtc_ref[0] = _vt_with_ones(one(xv_ref, 1))


def _compress(xk, xv, wa, wb, pe8, w1, w2e):
    batch, n_half, width = xk.shape
    bsel = lambda b: (b, 0, 0)
    c3 = lambda b: (0, 0, 0)
    vt_rows = NSA_KV_HEADS * VT_ROWS
    return pl.pallas_call(
        _compress_kernel,
        grid=(batch,),
        in_specs=[pl.BlockSpec((1, n_half, width), bsel), pl.BlockSpec((1, n_half, width), bsel),
                  pl.BlockSpec(wa.shape, c3), pl.BlockSpec(wb.shape, c3),
                  pl.BlockSpec(pe8.shape, c3), pl.BlockSpec(w1.shape, c3), pl.BlockSpec(w2e.shape, c3)],
        out_specs=[pl.BlockSpec((1, n_half, KV_WIDTH), bsel), pl.BlockSpec((1, vt_rows, n_half), bsel)],
        out_shape=[jax.ShapeDtypeStruct((batch, n_half, KV_WIDTH), BF16),
                   jax.ShapeDtypeStruct((batch, vt_rows, n_half), BF16)],
        compiler_params=_cparams(1),
        name="compress",
    )(xk, xv, wa, wb, pe8, w1, w2e)


def _pack_compress(pe, w1, w2):
    half = CMP_BLOCK // 2
    w1r = w1.reshape(CMP_BLOCK, NSA_HEAD_DIM, CMP_HIDDEN)
    eye = jnp.eye(NSA_KV_HEADS, dtype=w1.dtype)

    def expand(w_half):
        z = w_half[:, None, :, None, :] * eye[None, :, None, :, None]
        return z.reshape(half * KV_WIDTH, NSA_KV_HEADS * CMP_HIDDEN)

    wa = expand(w1r[:half]).astype(BF16)
    wb = expand(w1r[half:]).astype(BF16)
    w2e = (w2[None, :, None, :] * eye[:, None, :, None]).reshape(NSA_KV_HEADS * CMP_HIDDEN, KV_WIDTH).astype(BF16)
    pe8 = jnp.concatenate([pe.reshape(1, CMP_BLOCK * NSA_HEAD_DIM),
                           jnp.zeros((7, CMP_BLOCK * NSA_HEAD_DIM), pe.dtype)], axis=0)
    return wa, wb, pe8, w1, w2e


def _q_aug(q_ref, qs_ref, h):
    qt = jnp.concatenate(
        [q_ref[0, (NSA_GROUP * h + g) * LANES:(NSA_GROUP * h + g + 1) * LANES, :] for g in range(NSA_GROUP)], axis=1)
    return jnp.concatenate([qt, qs_ref[h]], axis=0)


def _slope2(head):
    return float(2.0 ** (-8.0 * (head + 1) / NSA_Q_HEADS)) * LOG2E


def _bf16_split3(x):
    parts = []
    for _ in range(3):
        p = float(np.asarray(x, np.float32).astype(BF16).astype(np.float32))
        parts.append(p)
        x = x - p
    return parts


def _init_alibi_operands(qs_ref, kpos_ref, key_stride):
    lane = lax.broadcasted_iota(jnp.int32, (QT, LANES), 1)
    row = lax.broadcasted_iota(jnp.int32, (QT, LANES), 0)
    kpos_ref[...] = jnp.where(lane < 3, row.astype(F32), 0.0).astype(BF16)
    srow = lax.broadcasted_iota(jnp.int32, (LANES, QT), 0)
    for h in range(NSA_KV_HEADS):
        for g in range(NSA_GROUP):
            hi, mid, lo = _bf16_split3(key_stride * _slope2(NSA_GROUP * h + g))
            blk = jnp.where(srow == 0, hi, jnp.where(srow == 1, mid, jnp.where(srow == 2, lo, 0.0)))
            qs_ref[h, :, g * QT:(g + 1) * QT] = blk.astype(BF16)


def _gate_row(gt_ref, head, branch):
    r = head * 3 + branch
    return _sigmoid(gt_ref[0, r:r + 1, :])


def _softmax_jobs(jobs):
    scores = [_dot(jnp.concatenate(job[1], axis=0), job[0]) for job in jobs]
    staged = []
    for (q_aug, k_tiles, masks, offs, value_groups, state), st_all in zip(jobs, scores):
        n_t = len(k_tiles)
        per_g = []
        for g in range(NSA_GROUP):
            sl = slice(g * QT, (g + 1) * QT)
            ss = []
            for u in range(n_t):
                s = st_all[u * QT:(u + 1) * QT, sl]
                ss.append(s if masks[u] is None else jnp.where(masks[u], s, NEG))
            mx = None
            for u in range(n_t):
                cand = jnp.max(ss[u], axis=0, keepdims=True) + offs[u][g]
                mx = cand if mx is None else jnp.maximum(mx, cand)
            alpha = None
            if state is None:
                m_new = mx
            else:
                m_ref, _, h = state
                m_old = m_ref[h, :, sl]
                m_new = jnp.maximum(m_old, mx)
                alpha = jnp.exp2(m_old - m_new)
                m_ref[h, :, sl] = m_new
            m_use = jnp.where(m_new < 0.5 * NEG, 0.0, m_new)
            per_g.append((ss, m_use, alpha))
        staged.append(per_g)
    ets = []
    for (q_aug, k_tiles, masks, offs, value_groups, state), per_g in zip(jobs, staged):
        n_t = len(k_tiles)
        rows = [jnp.concatenate([jnp.exp2(per_g[g][0][u] - (per_g[g][1] - offs[u][g])).astype(BF16)
                                 for g in range(NSA_GROUP)], axis=1) for u in range(n_t)]
        ets.append(jnp.concatenate(rows, axis=0))
    results = []
    for (q_aug, k_tiles, masks, offs, value_groups, state), per_g, et in zip(jobs, staged, ets):
        mats = [jnp.concatenate(grp, axis=1) for grp in value_groups]
        pv_all = _dot(jnp.concatenate(mats, axis=0), et)
        pvs, r0 = [], 0
        for mat in mats:
            pvs.append(pv_all[r0:r0 + mat.shape[0], :])
            r0 += mat.shape[0]
        if state is None:
            results.append(pvs)
        else:
            _, acc_ref, h = state
            acc_ref[h] = jnp.concatenate([pg[2] for pg in per_g], axis=1) * acc_ref[h] + pvs[0]
            results.append(None)
    return results


def _gated(gt_ref, h, branch, pv):
    inv = 1.0 / jnp.maximum(pv[NSA_HEAD_DIM:NSA_HEAD_DIM + 1, :], 1e-30)
    out = []
    for g in range(NSA_GROUP):
        sl = slice(g * QT, (g + 1) * QT)
        out.append(pv[0:NSA_HEAD_DIM, sl] * (inv[:, sl] * _gate_row(gt_ref, NSA_GROUP * h + g, branch)))
    return out


def _nsa_cmp_kernel(q_ref, kc_ref, vtc_ref, gt_ref, oc_ref, sel_ref, cnt_ref,
                    qs_ref, kpos_ref, ovt_ref, pvo_ref, pvi_ref):
    ncp = kc_ref.shape[1]
    n_ct = ncp // QT
    n_sel = sel_ref.shape[3]
    start = pl.program_id(1) * QT
    tile_span = QT * CMP_STRIDE

    @pl.when((pl.program_id(0) == 0) & (pl.program_id(1) == 0))
    def _():
        _init_alibi_operands(qs_ref, kpos_ref, float(CMP_STRIDE))
        j_o = lax.broadcasted_iota(jnp.int32, (n_sel, ncp), 0)
        c_o = lax.broadcasted_iota(jnp.int32, (n_sel, ncp), 1)
        ov = ((c_o * CMP_STRIDE <= j_o * SEL_BLOCK + (SEL_BLOCK - 1))
              & (c_o * CMP_STRIDE + (CMP_BLOCK - 1) >= j_o * SEL_BLOCK))
        ovt_ref[...] = jnp.where(ov, 1.0, 0.0).astype(BF16)

    a0 = (lax.broadcasted_iota(jnp.int32, (QT, QT), 1)
          - CMP_STRIDE * lax.broadcasted_iota(jnp.int32, (QT, QT), 0))
    q_augs = [_q_aug(q_ref, qs_ref, h) for h in range(NSA_KV_HEADS)]
    n_vis = jnp.minimum((start + QT - CMP_BLOCK) // tile_span + 1, n_ct)

    for k in range(1, n_ct + 1):
        @pl.when(n_vis == k)
        def _(k=k):
            jobs = []
            for h in range(NSA_KV_HEADS):
                k_tiles, masks, offs, vts, ovs = [], [], [], [], []
                for ci in range(k):
                    cs = slice(ci * QT, (ci + 1) * QT)
                    first_end = ci * tile_span + CMP_BLOCK - 1
                    k_tiles.append(jnp.concatenate([kc_ref[0, cs, :], kpos_ref[...]], axis=1))
                    masks.append(a0 + (start - first_end) >= 0)
                    offs.append([_slope2(NSA_GROUP * h + g) * first_end for g in range(NSA_GROUP)])
                    vts.append(vtc_ref[0, h * VT_ROWS:(h + 1) * VT_ROWS, cs])
                    ovs.append(ovt_ref[:, cs])
                jobs.append((q_augs[h], k_tiles, masks, offs, [vts, ovs], None))
            for h, (pvo, pvi) in enumerate(_softmax_jobs(jobs)):
                pvo_ref[h] = pvo
                pvi_ref[h] = pvi

    j_io = lax.broadcasted_iota(jnp.int32, (n_sel, QT), 0)
    t_j = start + lax.broadcasted_iota(jnp.int32, (n_sel, QT), 1)
    cur = t_j // SEL_BLOCK
    valid = j_io * SEL_BLOCK <= t_j
    forced = (j_io == 0) | (j_io == cur) | (j_io == cur - 1)
    ones8 = jnp.ones((8, QT), BF16)

    scores = []
    for h in range(NSA_KV_HEADS):
        pvo = pvo_ref[h]
        o_c = _gated(gt_ref, h, 0, pvo)
        inv = 1.0 / jnp.maximum(pvo[NSA_HEAD_DIM:NSA_HEAD_DIM + 1, :], 1e-30)
        imp = None
        for g in range(NSA_GROUP):
            sl = slice(g * QT, (g + 1) * QT)
            oc_ref[0, NSA_GROUP * h + g] = o_c[g]
            term = pvi_ref[h, :, sl] * inv[:, sl]
            imp = term if imp is None else imp + term
        scores.append(jnp.where(valid, imp + jnp.where(forced, FORCE_BONUS, 0.0), -FORCE_BONUS))

    taken = -(2.0 ** 126)

    def pick(_, carry):
        out = []
        for sc in carry:
            mx = jnp.max(sc, axis=0, keepdims=True)
            idx = jnp.min(jnp.where(sc == mx, j_io, n_sel), axis=0, keepdims=True)
            out.append(jnp.where(j_io == idx, taken, sc))
        return tuple(out)

    picked = lax.fori_loop(0, min(SEL_TOPK, n_sel), pick, tuple(scores))
    for h in range(NSA_KV_HEADS):
        sel = jnp.where(picked[h] <= 0.5 * taken, 1.0, 0.0)
        sel_ref[0, h, 0] = sel
        cnt = _dot_nt(ones8, sel.astype(BF16))
        cnt_ref[0, 0, h] = cnt[0:1, :]


def _nsa_cmp(qpad3, kc, vtc, gates_t, batch, seq):
    nb = seq // QT
    n_sel = seq // SEL_BLOCK
    ncp = kc.shape[1]
    assert ncp % QT == 0
    vt_rows = NSA_KV_HEADS * VT_ROWS
    return pl.pallas_call(
        _nsa_cmp_kernel,
        grid=(batch, nb),
        in_specs=[pl.BlockSpec((1, QPAD_W, QT), lambda b, i: (b, 0, i)),
                  pl.BlockSpec((1, ncp, KV_WIDTH), lambda b, i: (b, 0, 0)),
                  pl.BlockSpec((1, vt_rows, ncp), lambda b, i: (b, 0, 0)),
                  pl.BlockSpec((1, N_GATES, QT), lambda b, i: (b, 0, i))],
        out_specs=[pl.BlockSpec((1, NSA_Q_HEADS, NSA_HEAD_DIM, QT), lambda b, i: (b, 0, 0, i)),
                   pl.BlockSpec((1, NSA_KV_HEADS, 1, n_sel, QT), lambda b, i: (b, 0, i, 0, 0)),
                   pl.BlockSpec((1, 1, NSA_KV_HEADS, 1, n_sel), lambda b, i: (b, i, 0, 0, 0))],
        out_shape=[jax.ShapeDtypeStruct((batch, NSA_Q_HEADS, NSA_HEAD_DIM, seq), F32),
                   jax.ShapeDtypeStruct((batch, NSA_KV_HEADS, nb, n_sel, QT), F32),
                   jax.ShapeDtypeStruct((batch, nb, NSA_KV_HEADS, 1, n_sel), F32)],
        scratch_shapes=[pltpu.VMEM((NSA_KV_HEADS, LANES, NSA_GROUP * QT), BF16),
                        pltpu.VMEM((QT, LANES), BF16),
                        pltpu.VMEM((n_sel, ncp), BF16),
                        pltpu.VMEM((NSA_KV_HEADS, VT_ROWS, NSA_GROUP * QT), F32),
                        pltpu.VMEM((NSA_KV_HEADS, n_sel, NSA_GROUP * QT), F32)],
        compiler_params=_cparams(2),
        name="nsa_cmp",
    )(qpad3, kc, vtc, gates_t)


def _nsa_sel_kernel(flags_ref, q_ref, ksel_ref, vtsel_ref, kwin_ref, vtwin_ref, selm_ref, gt_ref, oc_ref,
                    o_ref, m_sel, acc_sel, qs_ref, kpos_ref, part_ref, list_ref):
    b = pl.program_id(0)
    qb = pl.program_id(1)
    nb = pl.num_programs(1)
    heads = range(NSA_KV_HEADS)
    k_io = lax.broadcasted_iota(jnp.int32, (QT, QT), 0)
    q_io = lax.broadcasted_iota(jnp.int32, (QT, QT), 1)
    causal = q_io >= k_io
    in_window = q_io < k_io
    first_half = k_io < SEL_BLOCK

    @pl.when((b == 0) & (qb == 0))
    def _():
        _init_alibi_operands(qs_ref, kpos_ref, 1.0)

        def clear(i, carry):
            for h in heads:
                list_ref[h, i] = 0
            return carry

        lax.fori_loop(0, nb, clear, 0)

    def k_aug(k_ref, t):
        off = pl.multiple_of(t * QT, QT)
        return jnp.concatenate([k_ref[0, pl.ds(off, QT), :], kpos_ref[...]], axis=1)

    def v_ext(vt_ref, h, t):
        return vt_ref[0, h * VT_ROWS:(h + 1) * VT_ROWS, pl.ds(pl.multiple_of(t * QT, QT), QT)]

    def tile_offs(h, t, live=None):
        pos = (t * QT).astype(F32)
        out = [_slope2(NSA_GROUP * h + g) * pos for g in range(NSA_GROUP)]
        return out if live is None else [jnp.where(live, o, NEG) for o in out]

    def chosen(h, t):
        mrow = selm_ref[0, h, 0, pl.ds(2 * t, 2), :]
        return jnp.where(first_half, mrow[0:1, :], mrow[1:2, :]) > 0.5

    q_augs = [_q_aug(q_ref, qs_ref, h) for h in heads]

    n_tiles = []
    for h in heads:
        fbase = ((b * nb + qb) * NSA_KV_HEADS + h) * nb

        def build(p, n, h=h, fbase=fbase):
            hit = flags_ref[fbase + p] > 0

            @pl.when(hit)
            def _():
                list_ref[h, n] = p
            return n + hit.astype(jnp.int32)

        n_tiles.append(lax.fori_loop(0, qb, build, 0))

    m_sel[...] = jnp.full(m_sel.shape, NEG, F32)
    acc_sel[...] = jnp.zeros(acc_sel.shape, F32)
    n_win = WINDOW // QT + 1
    jobs = []
    for h in heads:
        jobs.append((q_augs[h], [k_aug(ksel_ref, qb)], [chosen(h, qb) & causal], [tile_offs(h, qb)],
                     [[v_ext(vtsel_ref, h, qb)]], (m_sel, acc_sel, h)))
    for h in heads:
        k_tiles, masks, offs, vts = [], [], [], []
        for u in range(n_win):
            raw = qb - (n_win - 1) + u
            t = jnp.maximum(raw, 0)
            k_tiles.append(k_aug(kwin_ref, t))
            masks.append(in_window if u == 0 else causal if u == n_win - 1 else None)
            offs.append(tile_offs(h, t, raw >= 0))
            vts.append(v_ext(vtwin_ref, h, t))
        jobs.append((q_augs[h], k_tiles, masks, offs, [vts], None))
    results = _softmax_jobs(jobs)
    for h in heads:
        o_w = _gated(gt_ref, h, 2, results[NSA_KV_HEADS + h][0])
        for g in range(NSA_GROUP):
            head = NSA_GROUP * h + g
            part_ref[head] = oc_ref[0, head] + o_w[g]

    def sel_body(i, carry):
        jobs = []
        for h in heads:
            k_tiles, masks, offs, vts = [], [], [], []
            for u in range(SEL_UNROLL):
                idx = i * SEL_UNROLL + u
                live = idx < n_tiles[h]
                t = jnp.where(live, list_ref[h, jnp.minimum(idx, nb - 1)], 0)
                k_tiles.append(k_aug(ksel_ref, t))
                masks.append(chosen(h, t))
                offs.append(tile_offs(h, t, live))
                vts.append(v_ext(vtsel_ref, h, t))
            jobs.append((q_augs[h], k_tiles, masks, offs, [vts], (m_sel, acc_sel, h)))
        _softmax_jobs(jobs)
        return carry

    n_steps = (functools.reduce(jnp.maximum, n_tiles) + SEL_UNROLL - 1) // SEL_UNROLL
    lax.fori_loop(0, n_steps, sel_body, 0)

    for h in heads:
        o_s = _gated(gt_ref, h, 1, acc_sel[h])
        for g in range(NSA_GROUP):
            head = NSA_GROUP * h + g
            o_ref[0, head] = part_ref[head] + o_s[g]


def _nsa_sel(flags, qpad3, ksel, vtsel, kwin, vtwin, selm, gates_t, oc, batch, seq):
    nb = seq // QT
    res = lambda b, i, f: (b, 0, 0)
    vt_rows = NSA_KV_HEADS * VT_ROWS
    grid_spec = pltpu.PrefetchScalarGridSpec(
        num_scalar_prefetch=1,
        grid=(batch, nb),
        in_specs=[pl.BlockSpec((1, QPAD_W, QT), lambda b, i, f: (b, 0, i)),
                  pl.BlockSpec((1, seq, KV_WIDTH), res),
                  pl.BlockSpec((1, vt_rows, seq), res),
                  pl.BlockSpec((1, seq, KV_WIDTH), res),
                  pl.BlockSpec((1, vt_rows, seq), res),
                  pl.BlockSpec((1, NSA_KV_HEADS, 1, 2 * nb, QT), lambda b, i, f: (b, 0, i, 0, 0)),
                  pl.BlockSpec((1, N_GATES, QT), lambda b, i, f: (b, 0, i)),
                  pl.BlockSpec((1, NSA_Q_HEADS, NSA_HEAD_DIM, QT), lambda b, i, f: (b, 0, 0, i))],
        out_specs=pl.BlockSpec((1, NSA_Q_HEADS, NSA_HEAD_DIM, QT), lambda b, i, f: (b, 0, 0, i)),
        scratch_shapes=[pltpu.VMEM((NSA_KV_HEADS, 1, NSA_GROUP * QT), F32),
                        pltpu.VMEM((NSA_KV_HEADS, VT_ROWS, NSA_GROUP * QT), F32),
                        pltpu.VMEM((NSA_KV_HEADS, LANES, NSA_GROUP * QT), BF16),
                        pltpu.VMEM((QT, LANES), BF16),
                        pltpu.VMEM((NSA_Q_HEADS, NSA_HEAD_DIM, QT), F32),
                        pltpu.SMEM((NSA_KV_HEADS, nb), jnp.int32)],
    )
    return pl.pallas_call(
        _nsa_sel_kernel,
        grid_spec=grid_spec,
        out_shape=jax.ShapeDtypeStruct((batch, NSA_Q_HEADS, NSA_HEAD_DIM, seq), F32),
        compiler_params=_cparams(2),
        name="nsa_sel",
    )(flags, qpad3, ksel, vtsel, kwin, vtwin, selm, gates_t, oc)


def _mlstm_kernel(q_ref, k_ref, v_ref, og_ref, misc_ref, misct_ref, cw_ref, cb_ref, nw_ref,
                  o_ref, xq_ref, xk_ref, c_ref, n_ref, m_ref):
    tm = q_ref.shape[0]
    L = MLSTM_L
    hd = MLSTM_HEAD_DIM
    halo = 8

    @pl.when(pl.program_id(1) == 0)
    def _():
        xq_ref[0:halo, :] = jnp.zeros((halo, MLSTM_WIDTH), F32)
        xk_ref[0:halo, :] = jnp.zeros((halo, MLSTM_WIDTH), F32)
        c_ref[...] = jnp.zeros_like(c_ref)
        n_ref[...] = jnp.zeros_like(n_ref)
        m_ref[...] = jnp.zeros_like(m_ref)

    def conv(x_ref, buf_ref, col0):
        buf_ref[halo:halo + tm, :] = x_ref[...]
        y = cb_ref[:, col0:col0 + MLSTM_WIDTH]
        for kk in range(CONV_WIDTH):
            r0 = halo - (CONV_WIDTH - 1) + kk
            y = y + cw_ref[kk:kk + 1, col0:col0 + MLSTM_WIDTH] * buf_ref[r0:r0 + tm, :]
        tail = buf_ref[tm:tm + halo, :]
        buf_ref[0:halo, :] = tail
        return _silu(y)

    qc = conv(q_ref, xq_ref, 0)
    kc = conv(k_ref, xk_ref, MLSTM_WIDTH) * (hd ** -0.5)

    misc = misc_ref[...]
    misct = misct_ref[0]
    lf_c = -jnp.log(1.0 + jnp.exp(-misc))
    lf_r = -jnp.log(1.0 + jnp.exp(-misct))
    r_io = lax.broadcasted_iota(jnp.int32, (tm, tm), 0)
    c_io = lax.broadcasted_iota(jnp.int32, (tm, tm), 1)
    same = (r_io // L) == (c_io // L)
    tril_blk = jnp.where(same & (c_io <= r_io), 1.0, 0.0)
    triu_blk = jnp.where(same & (r_io <= c_io), 1.0, 0.0)
    b_c = jnp.dot(tril_blk, lf_c, preferred_element_type=F32, precision=HIGHEST)
    b_r = jnp.dot(lf_r, triu_blk, preferred_element_type=F32, precision=HIGHEST)

    tri = lax.broadcasted_iota(jnp.int32, (L, L), 1) <= lax.broadcasted_iota(jnp.int32, (L, L), 0)

    for ci in range(tm // L):
        r0 = ci * L
        for h in range(MLSTM_HEADS):
            cs = slice(h * hd, (h + 1) * hd)
            q = qc[r0:r0 + L, cs]
            k = kc[r0:r0 + L, cs]
            v = v_ref[r0:r0 + L, cs]
            bcol = b_c[r0:r0 + L, MISC_F + h:MISC_F + h + 1]
            licol = misc[r0:r0 + L, MISC_I + h:MISC_I + h + 1]
            brow = b_r[MISC_F + h:MISC_F + h + 1, r0:r0 + L]
            lirow = misct[MISC_I + h:MISC_I + h + 1, r0:r0 + L]
            m_prev = m_ref[h]
            cmat = c_ref[h]
            nrow = n_ref[h]

            dmat = jnp.where(tri, bcol - brow + lirow, NEG)
            m_inter = bcol + m_prev
            m_t = jnp.maximum(jnp.max(dmat, axis=-1, keepdims=True), m_inter)
            qb16 = q.astype(BF16)
            kb16 = k.astype(BF16)
            vb16 = v.astype(BF16)
            w = jnp.exp(dmat - m_t) * _dot_nt(qb16, kb16)
            decay = jnp.exp(m_inter - m_t)
            num = _dot(w.astype(BF16), vb16) + decay * _dot(qb16, cmat.astype(BF16))
            den = jnp.sum(w, axis=-1, keepdims=True) + decay * jnp.sum(q * nrow, axis=-1, keepdims=True)
            hout = num / jnp.maximum(jnp.abs(den), jnp.exp(-m_t))

            b_last = brow[:, L - 1:L]
            gcol = b_last - bcol + licol
            m_new = jnp.maximum(b_last + m_prev, jnp.max(gcol, axis=0, keepdims=True))
            wk = jnp.exp(gcol - m_new)
            d_c = jnp.exp(b_last + m_prev - m_new)
            kw = wk * k
            c_ref[h] = d_c * cmat + _dot_tn(kw.astype(BF16), vb16)
            n_ref[h] = d_c * nrow + jnp.sum(kw, axis=0, keepdims=True)
            m_ref[h] = m_new

            hn = hout * lax.rsqrt(jnp.mean(hout * hout, axis=-1, keepdims=True) + EPS)
            hn = hn * nw_ref[:, cs]
            o_ref[r0:r0 + L, cs] = (hn * _sigmoid(og_ref[r0:r0 + L, cs])).astype(o_ref.dtype)


def _mlstm(ml, misc, misc_t, conv_w, conv_b, norm_w, batch, seq):
    t = ml.shape[0]
    tm = TOKEN_TILE
    tpb = seq // tm
    col = lambda j: (lambda b, i: (b * tpb + i, j))
    const = lambda b, i: (0, 0)
    return pl.pallas_call(
        _mlstm_kernel,
        grid=(batch, tpb),
        in_specs=[pl.BlockSpec((tm, MLSTM_WIDTH), col(0)),
                  pl.BlockSpec((tm, MLSTM_WIDTH), col(1)),
                  pl.BlockSpec((tm, MLSTM_WIDTH), col(2)),
                  pl.BlockSpec((tm, MLSTM_WIDTH), col(3)),
                  pl.BlockSpec((tm, LANES), col(0)),
                  pl.BlockSpec((1, LANES, tm), lambda b, i: (b, 0, i)),
                  pl.BlockSpec(conv_w.shape, const),
                  pl.BlockSpec(conv_b.shape, const),
                  pl.BlockSpec(norm_w.shape, const)],
        out_specs=pl.BlockSpec((tm, MLSTM_WIDTH), col(0)),
        out_shape=jax.ShapeDtypeStruct((t, MLSTM_WIDTH), BF16),
        scratch_shapes=[pltpu.VMEM((tm + 8, MLSTM_WIDTH), F32),
                        pltpu.VMEM((tm + 8, MLSTM_WIDTH), F32),
                        pltpu.VMEM((MLSTM_HEADS, MLSTM_HEAD_DIM, MLSTM_HEAD_DIM), F32),
                        pltpu.VMEM((MLSTM_HEADS, 1, MLSTM_HEAD_DIM), F32),
                        pltpu.VMEM((MLSTM_HEADS, 1, 1), F32)],
        compiler_params=_cparams(2),
        name="mlstm",
    )(ml, ml, ml, ml, misc, misc_t, conv_w, conv_b, norm_w)


def _layer(x2d, c, batch, seq, w_ada, b_ada, norm_ffn1_w, ffn1_w_in, ffn1_w_out, norm_mix_w, w_in, b_in,
           cmp_k_pe, cmp_k_w1, cmp_k_w2, cmp_v_pe, cmp_v_w1, cmp_v_w2, conv_w, conv_b, mlstm_norm_w, w_out,
           norm_ffn2_w, ffn2_w_in, ffn2_w_out, final_nw):
    d = x2d.shape[1]
    tpb = seq // TOKEN_TILE
    c8 = jnp.concatenate([c, jnp.zeros((8 - batch % 8 if batch % 8 else 0, d), c.dtype)], axis=0)
    mod = _adaln(c8, w_ada, b_ada.reshape(1, -1))[:batch].reshape(batch, N_MOD, d)

    x1 = _ffn(x2d, mod, norm_ffn1_w.reshape(1, d), ffn1_w_in.astype(BF16), ffn1_w_out.astype(BF16),
              (0, 1, 2), tpb)

    w_pack, b_pack = _pack_inproj(w_in, b_in)
    (qpad, kc, vc, ksel, kwin, vtsel, vtwin, ml, misc, misc_t) = _inproj(
        x1, mod, norm_mix_w.reshape(1, d), w_pack, b_pack, batch, seq)

    n_half = seq // CMP_STRIDE
    pk = _pack_compress(cmp_k_pe, cmp_k_w1, cmp_k_w2)
    pv = _pack_compress(cmp_v_pe, cmp_v_w1, cmp_v_w2)
    stacked = [jnp.stack([a, bb]) for a, bb in zip(pk, pv)]
    kcmp, vtcmp = _compress(kc.reshape(batch, n_half, CMP_STRIDE * KV_WIDTH),
                            vc.reshape(batch, n_half, CMP_STRIDE * KV_WIDTH), *stacked)

    nb = seq // QT
    qpad3 = qpad
    gates_t = misc_t[:, :N_GATES, :]
    oc, sel, cnt = _nsa_cmp(qpad3, kcmp, vtcmp, gates_t, batch, seq)
    pair = cnt.reshape(batch, nb, NSA_KV_HEADS, nb, 2).sum(axis=-1)
    flags = (pair > 0.5).astype(jnp.int32).reshape(-1)
    o_t = _nsa_sel(flags, qpad3, ksel.reshape(batch, seq, KV_WIDTH), vtsel, kwin.reshape(batch, seq, KV_WIDTH),
                   vtwin, sel, gates_t, oc, batch, seq)
    o_nsa = o_t.reshape(batch, NSA_WIDTH, seq)

    o_ml = _mlstm(ml, misc, misc_t, conv_w, conv_b.reshape(1, -1), mlstm_norm_w.reshape(1, -1), batch, seq)

    return _ffn(x1, mod, norm_ffn2_w.reshape(1, d), ffn2_w_in.astype(BF16), ffn2_w_out.astype(BF16),
                (6, 7, 8), tpb, mix=(o_nsa, o_ml, w_out.astype(BF16)), mix_gate_row=5, final_nw=final_nw)


def kernel(x, c, w_ada, b_ada, norm_ffn1_w, ffn1_w_in, ffn1_w_out, norm_mix_w, w_in, b_in, cmp_k_pe, cmp_k_w1, cmp_k_w2, cmp_v_pe, cmp_v_w1, cmp_v_w2, conv_w, conv_b, mlstm_norm_w, w_out, norm_ffn2_w, ffn2_w_in, ffn2_w_out, final_norm_w):
    batch, seq, d = x.shape
    depth = w_ada.shape[0]
    assert depth == 1 and seq % (QT * CMP_STRIDE) == 0 and seq // SEL_BLOCK >= SEL_TOPK
    y = _layer(x.reshape(batch * seq, d), c, batch, seq, w_ada[0], b_ada[0], norm_ffn1_w[0], ffn1_w_in[0],
               ffn1_w_out[0], norm_mix_w[0], w_in[0], b_in[0], cmp_k_pe[0], cmp_k_w1[0], cmp_k_w2[0],
               cmp_v_pe[0], cmp_v_w1[0], cmp_v_w2[0], conv_w[0], conv_b[0], mlstm_norm_w[0], w_out[0],
               norm_ffn2_w[0], ffn2_w_in[0], ffn2_w_out[0], final_norm_w.reshape(1, d))
    return y.reshape(batch, seq, d)
```

```python
import functools

import numpy as np
import jax
import jax.numpy as jnp
from jax import lax
from jax.experimental import pallas as pl
from jax.experimental.pallas import tpu as pltpu

NSA_Q_HEADS = 8
NSA_KV_HEADS = 2
NSA_GROUP = NSA_Q_HEADS // NSA_KV_HEADS
NSA_HEAD_DIM = 64
CMP_BLOCK = 32
CMP_STRIDE = 16
CMP_HIDDEN = 128
SEL_BLOCK = 64
SEL_TOPK = 16
WINDOW = 512
FORCE_BONUS = 1.0e4
MLSTM_HEADS = 4
MLSTM_HEAD_DIM = 128
CONV_WIDTH = 4
D_FF = 2816
N_MOD = 9
EPS = 1e-6
NEG = -1e30

NSA_WIDTH = NSA_Q_HEADS * NSA_HEAD_DIM
KV_WIDTH = NSA_KV_HEADS * NSA_HEAD_DIM
MLSTM_WIDTH = MLSTM_HEADS * MLSTM_HEAD_DIM
N_GATES = 3 * NSA_Q_HEADS

LANES = 128
QT = 128
TOKEN_TILE = 512
FF_CHUNK = 256
MLSTM_L = 128
SEL_UNROLL = 6
TOPK_ROWS = 32
VT_ROWS = 80
VMEM_LIMIT = 56 * 1024 * 1024

F32 = jnp.float32
BF16 = jnp.bfloat16
HIGHEST = lax.Precision.HIGHEST
LOG2E = 1.4426950408889634


def _cparams(n_axes):
    return pltpu.CompilerParams(dimension_semantics=("arbitrary",) * n_axes,
                                vmem_limit_bytes=VMEM_LIMIT)


def _dot(a, b):
    return jnp.dot(a, b, preferred_element_type=F32)


def _dot_nt(a, b):
    return lax.dot_general(a, b, (((1,), (1,)), ((), ())), preferred_element_type=F32)


def _dot_tn(a, b):
    return lax.dot_general(a, b, (((0,), (0,)), ((), ())), preferred_element_type=F32)


def _sigmoid(x):
    return 1.0 / (1.0 + jnp.exp(-x))


def _silu(x):
    return x * _sigmoid(x)


def _norm_mod(x, nw, sh, sc):
    ms = jnp.mean(x * x, axis=-1, keepdims=True)
    y = x * lax.rsqrt(ms + EPS) * nw
    return y * (1.0 + sc) + sh


def _adaln_kernel(c_ref, w_ref, b_ref, o_ref):
    a = _silu(c_ref[...])
    o_ref[...] = jnp.dot(a, w_ref[...], preferred_element_type=F32, precision=HIGHEST) + b_ref[...]


def _adaln(c8, w_ada, b_ada):
    rows, d = c8.shape
    n = w_ada.shape[1]
    tn = n // N_MOD
    return pl.pallas_call(
        _adaln_kernel,
        grid=(N_MOD,),
        in_specs=[pl.BlockSpec((rows, d), lambda j: (0, 0)),
                  pl.BlockSpec((d, tn), lambda j: (0, j)),
                  pl.BlockSpec((1, tn), lambda j: (0, j))],
        out_specs=pl.BlockSpec((rows, tn), lambda j: (0, j)),
        out_shape=jax.ShapeDtypeStruct((rows, n), F32),
        compiler_params=_cparams(1),
        name="adaln",
    )(c8, w_ada, b_ada)


def _ffn_kernel(*refs, mod_rows, with_mix, mix_gate_row, with_final):
    it = iter(refs)
    x_ref = next(it)
    mod_ref = next(it)
    nw_ref = next(it)
    win_ref = next(it)
    wo_ref = next(it)
    if with_mix:
        ma_ref = next(it)
        mb_ref = next(it)
        wmix_ref = next(it)
    if with_final:
        fnw_ref = next(it)
    o_ref = next(it)

    sh_row, sc_row, g_row = mod_rows
    x = x_ref[...]
    if with_mix:
        half = ma_ref.shape[1]
        mixed = (_dot_tn(ma_ref[0].astype(BF16), wmix_ref[0:half, :])
                 + _dot(mb_ref[...], wmix_ref[half:2 * half, :]))
        x = x + mod_ref[0, mix_gate_row:mix_gate_row + 1, :] * mixed
    h = _norm_mod(x, nw_ref[...], mod_ref[0, sh_row:sh_row + 1, :], mod_ref[0, sc_row:sc_row + 1, :])
    hb = h.astype(BF16)
    d_ff = wo_ref.shape[0]
    n_chunks = d_ff // FF_CHUNK
    acc = None
    for j in range(n_chunks):
        c0 = j * FF_CHUNK
        g = _dot(hb, win_ref[:, c0:c0 + FF_CHUNK])
        u = _dot(hb, win_ref[:, d_ff + c0:d_ff + c0 + FF_CHUNK])
        act = (_silu(g) * u).astype(BF16)
        part = _dot(act, wo_ref[c0:c0 + FF_CHUNK, :])
        acc = part if acc is None else acc + part
    y = x + 0.5 * mod_ref[0, g_row:g_row + 1, :] * acc
    if with_final:
        ms = jnp.mean(y * y, axis=-1, keepdims=True)
        y = y * lax.rsqrt(ms + EPS) * fnw_ref[...]
    o_ref[...] = y


def _ffn(x2d, mod, nw, w_in_bf, w_out_bf, mod_rows, tiles_per_batch, mix=None, mix_gate_row=None, final_nw=None):
    t, d = x2d.shape
    tm = TOKEN_TILE
    d_ff = w_out_bf.shape[0]
    assert d_ff % FF_CHUNK == 0
    row = lambda i: (i, 0)
    const = lambda i: (0, 0)
    in_specs = [pl.BlockSpec((tm, d), row),
                pl.BlockSpec((1, N_MOD, d), lambda i: (i // tiles_per_batch, 0, 0)),
                pl.BlockSpec((1, d), const),
                pl.BlockSpec(w_in_bf.shape, const),
                pl.BlockSpec(w_out_bf.shape, const)]
    args = [x2d, mod, nw, w_in_bf, w_out_bf]
    if mix is not None:
        ma, mb, wmix = mix
        in_specs += [pl.BlockSpec((1, ma.shape[1], tm), lambda i: (i // tiles_per_batch, 0, i % tiles_per_batch)),
                     pl.BlockSpec((tm, mb.shape[1]), row),
                     pl.BlockSpec(wmix.shape, const)]
        args += [ma, mb, wmix]
    if final_nw is not None:
        in_specs.append(pl.BlockSpec((1, d), const))
        args.append(final_nw)
    kern = functools.partial(_ffn_kernel, mod_rows=mod_rows, with_mix=mix is not None,
                             mix_gate_row=mix_gate_row, with_final=final_nw is not None)
    return pl.pallas_call(
        kern,
        grid=(t // tm,),
        in_specs=in_specs,
        out_specs=pl.BlockSpec((tm, d), row),
        out_shape=jax.ShapeDtypeStruct((t, d), F32),
        compiler_params=_cparams(1),
        name="ffn_mix" if mix is not None else "ffn",
    )(*args)


QPAD_W = NSA_Q_HEADS * LANES
KV6_OFF = QPAD_W
ML_OFF = KV6_OFF + 6 * KV_WIDTH
MISC_OFF = ML_OFF + 4 * MLSTM_WIDTH
PROJ_W = MISC_OFF + LANES
MISC_I = N_GATES
MISC_F = N_GATES + MLSTM_HEADS


def _vt_with_ones(v):
    vt = v.T
    n = vt.shape[1]
    pad = jnp.where(lax.broadcasted_iota(jnp.int32, (VT_ROWS - NSA_HEAD_DIM, n), 0) == 0, 1.0, 0.0)
    parts = []
    for h in range(NSA_KV_HEADS):
        parts += [vt[h * NSA_HEAD_DIM:(h + 1) * NSA_HEAD_DIM, :], pad]
    return jnp.concatenate(parts, axis=0).astype(BF16)


def _inproj_kernel(x_ref, mod_ref, nw_ref, w_ref, b_ref,
                   qpad_ref, kc_ref, vc_ref, ksel_ref, kwin_ref, vtsel_ref, vtwin_ref,
                   ml_ref, misc_ref, misct_ref):
    h = _norm_mod(x_ref[...], nw_ref[...], mod_ref[0, 3:4, :], mod_ref[0, 4:5, :])
    hb = h.astype(BF16)

    def proj(c0, width):
        return _dot(hb, w_ref[:, c0:c0 + width]) + b_ref[:, c0:c0 + width]

    scale = NSA_HEAD_DIM ** -0.5 * LOG2E
    for i in range(0, NSA_Q_HEADS, 2):
        pair = proj(i * LANES, 2 * LANES) * scale
        qpad_ref[0, i * LANES:(i + 1) * LANES, :] = pair[:, :LANES].T.astype(BF16)
        qpad_ref[0, (i + 1) * LANES:(i + 2) * LANES, :] = pair[:, LANES:].T.astype(BF16)
    kv_c = proj(KV6_OFF, 2 * KV_WIDTH)
    kc_ref[...] = kv_c[:, :KV_WIDTH].astype(BF16)
    vc_ref[...] = kv_c[:, KV_WIDTH:].astype(BF16)
    kv_s = proj(KV6_OFF + 2 * KV_WIDTH, 2 * KV_WIDTH)
    ksel_ref[...] = kv_s[:, :KV_WIDTH].astype(BF16)
    vtsel_ref[0] = _vt_with_ones(kv_s[:, KV_WIDTH:])
    kv_w = proj(KV6_OFF + 4 * KV_WIDTH, 2 * KV_WIDTH)
    kwin_ref[...] = kv_w[:, :KV_WIDTH].astype(BF16)
    vtwin_ref[0] = _vt_with_ones(kv_w[:, KV_WIDTH:])
    for i in range(4):
        ml_ref[:, i * MLSTM_WIDTH:(i + 1) * MLSTM_WIDTH] = proj(ML_OFF + i * MLSTM_WIDTH, MLSTM_WIDTH)
    misc = proj(MISC_OFF, LANES)
    misc_ref[...] = misc
    misct_ref[0] = misc.T


def _inproj(x2d, mod, nw, w_pack, b_pack, batch, seq):
    t, d = x2d.shape
    tm = TOKEN_TILE
    tpb = seq // tm
    row = lambda i: (i, 0)
    const = lambda i: (0, 0)
    trow = lambda i: (i // tpb, 0, i % tpb)
    vt_rows = NSA_KV_HEADS * VT_ROWS
    out_shapes = [
        jax.ShapeDtypeStruct((batch, QPAD_W, seq), BF16),
        jax.ShapeDtypeStruct((t, KV_WIDTH), BF16),
        jax.ShapeDtypeStruct((t, KV_WIDTH), BF16),
        jax.ShapeDtypeStruct((t, KV_WIDTH), BF16),
        jax.ShapeDtypeStruct((t, KV_WIDTH), BF16),
        jax.ShapeDtypeStruct((batch, vt_rows, seq), BF16),
        jax.ShapeDtypeStruct((batch, vt_rows, seq), BF16),
        jax.ShapeDtypeStruct((t, 4 * MLSTM_WIDTH), F32),
        jax.ShapeDtypeStruct((t, LANES), F32),
        jax.ShapeDtypeStruct((batch, LANES, seq), F32),
    ]
    out_specs = [
        pl.BlockSpec((1, QPAD_W, tm), trow),
        pl.BlockSpec((tm, KV_WIDTH), row),
        pl.BlockSpec((tm, KV_WIDTH), row),
        pl.BlockSpec((tm, KV_WIDTH), row),
        pl.BlockSpec((tm, KV_WIDTH), row),
        pl.BlockSpec((1, vt_rows, tm), trow),
        pl.BlockSpec((1, vt_rows, tm), trow),
        pl.BlockSpec((tm, 4 * MLSTM_WIDTH), row),
        pl.BlockSpec((tm, LANES), row),
        pl.BlockSpec((1, LANES, tm), trow),
    ]
    return pl.pallas_call(
        _inproj_kernel,
        grid=(t // tm,),
        in_specs=[pl.BlockSpec((tm, d), row),
                  pl.BlockSpec((1, N_MOD, d), lambda i: (i // tpb, 0, 0)),
                  pl.BlockSpec((1, d), const),
                  pl.BlockSpec(w_pack.shape, const),
                  pl.BlockSpec(b_pack.shape, const)],
        out_specs=out_specs,
        out_shape=out_shapes,
        compiler_params=_cparams(1),
        name="inproj",
    )(x2d, mod, nw, w_pack, b_pack)


def _pack_inproj(w_in, b_in):
    d = w_in.shape[0]
    offs = np.cumsum([0, NSA_WIDTH] + [KV_WIDTH] * 6 + [N_GATES] + [MLSTM_WIDTH] * 4 + [MLSTM_HEADS] * 2)

    def pad_heads(a):
        a = a.reshape(a.shape[:-1] + (NSA_KV_HEADS, NSA_GROUP, NSA_HEAD_DIM))
        z = jnp.zeros_like(a[..., 0, :, :])
        lo = jnp.concatenate([a[..., 0, :, :], z], axis=-1)
        hi = jnp.concatenate([z, a[..., 1, :, :]], axis=-1)
        return jnp.stack([lo, hi], axis=-3).reshape(a.shape[:-3] + (QPAD_W,))

    wq_pad = pad_heads(w_in[:, :NSA_WIDTH])
    bq_pad = pad_heads(b_in[:NSA_WIDTH])
    g0, m0, i0 = int(offs[7]), int(offs[8]), int(offs[12])
    misc_w = jnp.concatenate([w_in[:, g0:g0 + N_GATES], w_in[:, i0:i0 + 2 * MLSTM_HEADS],
                              jnp.zeros((d, LANES - N_GATES - 2 * MLSTM_HEADS), w_in.dtype)], axis=1)
    misc_b = jnp.concatenate([b_in[g0:g0 + N_GATES], b_in[i0:i0 + 2 * MLSTM_HEADS],
                              jnp.zeros((LANES - N_GATES - 2 * MLSTM_HEADS,), b_in.dtype)])
    w_pack = jnp.concatenate([wq_pad, w_in[:, NSA_WIDTH:g0], w_in[:, m0:i0], misc_w], axis=1)
    b_pack = jnp.concatenate([bq_pad, b_in[NSA_WIDTH:g0], b_in[m0:i0], misc_b])
    assert w_pack.shape[1] == PROJ_W
    return w_pack.astype(BF16), b_pack.reshape(1, PROJ_W).astype(F32)


def _compress_kernel(xk_ref, xv_ref, wa_ref, wb_ref, pe_ref, w1_ref, w2_ref, kc_ref, vtc_ref):
    n_half = xk_ref.shape[1]

    def one(x_ref, s):
        x = x_ref[0]
        a = _dot(x, wa_ref[s])
        b = _dot(x, wb_ref[s])
        pe_term = jnp.dot(pe_ref[s], w1_ref[s], preferred_element_type=F32, precision=HIGHEST)[0:1, :]
        pe2 = jnp.concatenate([pe_term] * NSA_KV_HEADS, axis=1)
        pre = a + pltpu.roll(b, n_half - 1, 0) + pe2
        hid = 0.5 * pre * (1.0 + jnp.tanh(0.7978845608028654 * (pre + 0.044715 * pre * pre * pre)))
        return _dot(hid.astype(BF16), w2_ref[s])

    kc_ref[0] = one(xk_ref, 0).astype(BF16)
    vtc_ref[0] = _vt_with_ones(one(xv_ref, 1))


def _compress(xk, xv, wa, wb, pe8, w1, w2e):
    batch, n_half, width = xk.shape
    bsel = lambda b: (b, 0, 0)
    c3 = lambda b: (0, 0, 0)
    vt_rows = NSA_KV_HEADS * VT_ROWS
    return pl.pallas_call(
        _compress_kernel,
        grid=(batch,),
        in_specs=[pl.BlockSpec((1, n_half, width), bsel), pl.BlockSpec((1, n_half, width), bsel),
                  pl.BlockSpec(wa.shape, c3), pl.BlockSpec(wb.shape, c3),
                  pl.BlockSpec(pe8.shape, c3), pl.BlockSpec(w1.shape, c3), pl.BlockSpec(w2e.shape, c3)],
        out_specs=[pl.BlockSpec((1, n_half, KV_WIDTH), bsel), pl.BlockSpec((1, vt_rows, n_half), bsel)],
        out_shape=[jax.ShapeDtypeStruct((batch, n_half, KV_WIDTH), BF16),
                   jax.ShapeDtypeStruct((batch, vt_rows, n_half), BF16)],
        compiler_params=_cparams(1),
        name="compress",
    )(xk, xv, wa, wb, pe8, w1, w2e)


def _pack_compress(pe, w1, w2):
    half = CMP_BLOCK // 2
    w1r = w1.reshape(CMP_BLOCK, NSA_HEAD_DIM, CMP_HIDDEN)
    eye = jnp.eye(NSA_KV_HEADS, dtype=w1.dtype)

    def expand(w_half):
        z = w_half[:, None, :, None, :] * eye[None, :, None, :, None]
        return z.reshape(half * KV_WIDTH, NSA_KV_HEADS * CMP_HIDDEN)

    wa = expand(w1r[:half]).astype(BF16)
    wb = expand(w1r[half:]).astype(BF16)
    w2e = (w2[None, :, None, :] * eye[:, None, :, None]).reshape(NSA_KV_HEADS * CMP_HIDDEN, KV_WIDTH).astype(BF16)
    pe8 = jnp.concatenate([pe.reshape(1, CMP_BLOCK * NSA_HEAD_DIM),
                           jnp.zeros((7, CMP_BLOCK * NSA_HEAD_DIM), pe.dtype)], axis=0)
    return wa, wb, pe8, w1, w2e


def _q_aug(q_ref, qs_ref, h):
    qt = jnp.concatenate(
        [q_ref[0, (NSA_GROUP * h + g) * LANES:(NSA_GROUP * h + g + 1) * LANES, :] for g in range(NSA_GROUP)], axis=1)
    return jnp.concatenate([qt, qs_ref[h]], axis=0)


def _slope2(head):
    return float(2.0 ** (-8.0 * (head + 1) / NSA_Q_HEADS)) * LOG2E


def _bf16_split3(x):
    parts = []
    for _ in range(3):
        p = float(np.asarray(x, np.float32).astype(BF16).astype(np.float32))
        parts.append(p)
        x = x - p
    return parts


def _init_alibi_operands(qs_ref, kpos_ref, key_stride):
    lane = lax.broadcasted_iota(jnp.int32, (QT, LANES), 1)
    row = lax.broadcasted_iota(jnp.int32, (QT, LANES), 0)
    kpos_ref[...] = jnp.where(lane < 3, row.astype(F32), 0.0).astype(BF16)
    srow = lax.broadcasted_iota(jnp.int32, (LANES, QT), 0)
    for h in range(NSA_KV_HEADS):
        for g in range(NSA_GROUP):
            hi, mid, lo = _bf16_split3(key_stride * _slope2(NSA_GROUP * h + g))
            blk = jnp.where(srow == 0, hi, jnp.where(srow == 1, mid, jnp.where(srow == 2, lo, 0.0)))
            qs_ref[h, :, g * QT:(g + 1) * QT] = blk.astype(BF16)


def _gate_row(gt_ref, head, branch):
    r = head * 3 + branch
    return _sigmoid(gt_ref[0, r:r + 1, :])


def _softmax_jobs(jobs):
    scores = [_dot(jnp.concatenate(job[1], axis=0), job[0]) for job in jobs]
    staged = []
    for (q_aug, k_tiles, masks, offs, value_groups, state), st_all in zip(jobs, scores):
        n_t = len(k_tiles)
        per_g = []
        for g in range(NSA_GROUP):
            sl = slice(g * QT, (g + 1) * QT)
            ss = []
            for u in range(n_t):
                s = st_all[u * QT:(u + 1) * QT, sl]
                ss.append(s if masks[u] is None else jnp.where(masks[u], s, NEG))
            mx = None
            for u in range(n_t):
                cand = jnp.max(ss[u], axis=0, keepdims=True) + offs[u][g]
                mx = cand if mx is None else jnp.maximum(mx, cand)
            alpha = None
            if state is None:
                m_new = mx
            else:
                m_ref, _, h = state
                m_old = m_ref[h, :, sl]
                m_new = jnp.maximum(m_old, mx)
                alpha = jnp.exp2(m_old - m_new)
                m_ref[h, :, sl] = m_new
            m_use = jnp.where(m_new < 0.5 * NEG, 0.0, m_new)
            per_g.append((ss, m_use, alpha))
        staged.append(per_g)
    ets = []
    for (q_aug, k_tiles, masks, offs, value_groups, state), per_g in zip(jobs, staged):
        n_t = len(k_tiles)
        rows = [jnp.concatenate([jnp.exp2(per_g[g][0][u] - (per_g[g][1] - offs[u][g])).astype(BF16)
                                 for g in range(NSA_GROUP)], axis=1) for u in range(n_t)]
        ets.append(jnp.concatenate(rows, axis=0))
    results = []
    for (q_aug, k_tiles, masks, offs, value_groups, state), per_g, et in zip(jobs, staged, ets):
        mats = [jnp.concatenate(grp, axis=1) for grp in value_groups]
        pv_all = _dot(jnp.concatenate(mats, axis=0), et)
        pvs, r0 = [], 0
        for mat in mats:
            pvs.append(pv_all[r0:r0 + mat.shape[0], :])
            r0 += mat.shape[0]
        if state is None:
            results.append(pvs)
        else:
            _, acc_ref, h = state
            acc_ref[h] = jnp.concatenate([pg[2] for pg in per_g], axis=1) * acc_ref[h] + pvs[0]
            results.append(None)
    return results


def _gated(gt_ref, h, branch, pv):
    inv = 1.0 / jnp.maximum(pv[NSA_HEAD_DIM:NSA_HEAD_DIM + 1, :], 1e-30)
    out = []
    for g in range(NSA_GROUP):
        sl = slice(g * QT, (g + 1) * QT)
        out.append(pv[0:NSA_HEAD_DIM, sl] * (inv[:, sl] * _gate_row(gt_ref, NSA_GROUP * h + g, branch)))
    return out


def _nsa_cmp_kernel(q_ref, kc_ref, vtc_ref, gt_ref, oc_ref, sel_ref, cnt_ref,
                    qs_ref, kpos_ref, ovt_ref, pvo_ref, pvi_ref):
    ncp = kc_ref.shape[1]
    n_ct = ncp // QT
    n_sel = sel_ref.shape[3]
    start = pl.program_id(1) * QT
    tile_span = QT * CMP_STRIDE

    @pl.when((pl.program_id(0) == 0) & (pl.program_id(1) == 0))
    def _():
        _init_alibi_operands(qs_ref, kpos_ref, float(CMP_STRIDE))
        j_o = lax.broadcasted_iota(jnp.int32, (n_sel, ncp), 0)
        c_o = lax.broadcasted_iota(jnp.int32, (n_sel, ncp), 1)
        ov = ((c_o * CMP_STRIDE <= j_o * SEL_BLOCK + (SEL_BLOCK - 1))
              & (c_o * CMP_STRIDE + (CMP_BLOCK - 1) >= j_o * SEL_BLOCK))
        ovt_ref[...] = jnp.where(ov, 1.0, 0.0).astype(BF16)

    a0 = (lax.broadcasted_iota(jnp.int32, (QT, QT), 1)
          - CMP_STRIDE * lax.broadcasted_iota(jnp.int32, (QT, QT), 0))
    q_augs = [_q_aug(q_ref, qs_ref, h) for h in range(NSA_KV_HEADS)]
    n_vis = jnp.minimum((start + QT - CMP_BLOCK) // tile_span + 1, n_ct)

    for k in range(1, n_ct + 1):
        @pl.when(n_vis == k)
        def _(k=k):
            jobs = []
            for h in range(NSA_KV_HEADS):
                k_tiles, masks, offs, vts, ovs = [], [], [], [], []
                for ci in range(k):
                    cs = slice(ci * QT, (ci + 1) * QT)
                    first_end = ci * tile_span + CMP_BLOCK - 1
                    k_tiles.append(jnp.concatenate([kc_ref[0, cs, :], kpos_ref[...]], axis=1))
                    masks.append(a0 + (start - first_end) >= 0)
                    offs.append([_slope2(NSA_GROUP * h + g) * first_end for g in range(NSA_GROUP)])
                    vts.append(vtc_ref[0, h * VT_ROWS:(h + 1) * VT_ROWS, cs])
                    ovs.append(ovt_ref[:, cs])
                jobs.append((q_augs[h], k_tiles, masks, offs, [vts, ovs], None))
            for h, (pvo, pvi) in enumerate(_softmax_jobs(jobs)):
                pvo_ref[h] = pvo
                pvi_ref[h] = pvi

    j_io = lax.broadcasted_iota(jnp.int32, (n_sel, QT), 0)
    t_j = start + lax.broadcasted_iota(jnp.int32, (n_sel, QT), 1)
    cur = t_j // SEL_BLOCK
    valid = j_io * SEL_BLOCK <= t_j
    forced = (j_io == 0) | (j_io == cur) | (j_io == cur - 1)
    ones8 = jnp.ones((8, QT), BF16)

    scores = []
    for h in range(NSA_KV_HEADS):
        pvo = pvo_ref[h]
        o_c = _gated(gt_ref, h, 0, pvo)
        inv = 1.0 / jnp.maximum(pvo[NSA_HEAD_DIM:NSA_HEAD_DIM + 1, :], 1e-30)
        imp = None
        for g in range(NSA_GROUP):
            sl = slice(g * QT, (g + 1) * QT)
            oc_ref[0, NSA_GROUP * h + g] = o_c[g]
            term = pvi_ref[h, :, sl] * inv[:, sl]
            imp = term if imp is None else imp + term
        scores.append(jnp.where(valid, imp + jnp.where(forced, FORCE_BONUS, 0.0), -FORCE_BONUS))

    taken = -(2.0 ** 126)
    n_cls = n_sel // TOPK_ROWS
    cls = jnp.minimum((start + QT + SEL_BLOCK * TOPK_ROWS - 1) // (SEL_BLOCK * TOPK_ROWS), n_cls)

    for k in range(1, n_cls + 1):
        @pl.when(cls == k)
        def _(rows=k * TOPK_ROWS):
            j_sub = j_io[:rows]

            def pick(_, carry):
                out = []
                for sc in carry:
                    mx = jnp.max(sc, axis=0, keepdims=True)
                    idx = jnp.min(jnp.where(sc == mx, j_sub, n_sel), axis=0, keepdims=True)
                    out.append(jnp.where(j_sub == idx, taken, sc))
                return tuple(out)

            picked = lax.fori_loop(0, SEL_TOPK, pick, tuple(sc[:rows] for sc in scores))
            for h in range(NSA_KV_HEADS):
                sel = jnp.where(picked[h] <= 0.5 * taken, 1.0, 0.0)
                if rows < n_sel:
                    sel = jnp.concatenate([sel, jnp.zeros((n_sel - rows, QT), F32)], axis=0)
                sel_ref[0, h, 0] = sel
                cnt = _dot_nt(ones8, sel.astype(BF16))
                cnt_ref[0, 0, h] = cnt[0:1, :]


def _nsa_cmp(qpad3, kc, vtc, gates_t, batch, seq):
    nb = seq // QT
    n_sel = seq // SEL_BLOCK
    ncp = kc.shape[1]
    assert ncp % QT == 0
    vt_rows = NSA_KV_HEADS * VT_ROWS
    return pl.pallas_call(
        _nsa_cmp_kernel,
        grid=(batch, nb),
        in_specs=[pl.BlockSpec((1, QPAD_W, QT), lambda b, i: (b, 0, i)),
                  pl.BlockSpec((1, ncp, KV_WIDTH), lambda b, i: (b, 0, 0)),
                  pl.BlockSpec((1, vt_rows, ncp), lambda b, i: (b, 0, 0)),
                  pl.BlockSpec((1, N_GATES, QT), lambda b, i: (b, 0, i))],
        out_specs=[pl.BlockSpec((1, NSA_Q_HEADS, NSA_HEAD_DIM, QT), lambda b, i: (b, 0, 0, i)),
                   pl.BlockSpec((1, NSA_KV_HEADS, 1, n_sel, QT), lambda b, i: (b, 0, i, 0, 0)),
                   pl.BlockSpec((1, 1, NSA_KV_HEADS, 1, n_sel), lambda b, i: (b, i, 0, 0, 0))],
        out_shape=[jax.ShapeDtypeStruct((batch, NSA_Q_HEADS, NSA_HEAD_DIM, seq), F32),
                   jax.ShapeDtypeStruct((batch, NSA_KV_HEADS, nb, n_sel, QT), F32),
                   jax.ShapeDtypeStruct((batch, nb, NSA_KV_HEADS, 1, n_sel), F32)],
        scratch_shapes=[pltpu.VMEM((NSA_KV_HEADS, LANES, NSA_GROUP * QT), BF16),
                        pltpu.VMEM((QT, LANES), BF16),
                        pltpu.VMEM((n_sel, ncp), BF16),
                        pltpu.VMEM((NSA_KV_HEADS, VT_ROWS, NSA_GROUP * QT), F32),
                        pltpu.VMEM((NSA_KV_HEADS, n_sel, NSA_GROUP * QT), F32)],
        compiler_params=_cparams(2),
        name="nsa_cmp",
    )(qpad3, kc, vtc, gates_t)


def _nsa_sel_kernel(count_ref, list_ref, q_ref, ksel_ref, vtsel_ref, kwin_ref, vtwin_ref, selm_ref, gt_ref,
                    oc_ref, o_ref, m_sel, acc_sel, qs_ref, kpos_ref, part_ref):
    b = pl.program_id(0)
    qb = pl.program_id(1)
    nb = pl.num_programs(1)
    heads = range(NSA_KV_HEADS)
    k_io = lax.broadcasted_iota(jnp.int32, (QT, QT), 0)
    q_io = lax.broadcasted_iota(jnp.int32, (QT, QT), 1)
    causal = q_io >= k_io
    in_window = q_io < k_io
    first_half = k_io < SEL_BLOCK

    @pl.when((b == 0) & (qb == 0))
    def _():
        _init_alibi_operands(qs_ref, kpos_ref, 1.0)

    def k_aug(k_ref, t):
        off = pl.multiple_of(t * QT, QT)
        return jnp.concatenate([k_ref[0, pl.ds(off, QT), :], kpos_ref[...]], axis=1)

    def v_ext(vt_ref, h, t):
        return vt_ref[0, h * VT_ROWS:(h + 1) * VT_ROWS, pl.ds(pl.multiple_of(t * QT, QT), QT)]

    def tile_offs(h, t, live=None):
        pos = (t * QT).astype(F32)
        out = [_slope2(NSA_GROUP * h + g) * pos for g in range(NSA_GROUP)]
        return out if live is None else [jnp.where(live, o, NEG) for o in out]

    def chosen(h, t):
        mrow = selm_ref[0, h, 0, pl.ds(2 * t, 2), :]
        return jnp.where(first_half, mrow[0:1, :], mrow[1:2, :]) > 0.5

    q_augs = [_q_aug(q_ref, qs_ref, h) for h in heads]

    slot = [(b * nb + qb) * NSA_KV_HEADS + h for h in heads]
    n_tiles = [count_ref[slot[h]] for h in heads]

    m_sel[...] = jnp.full(m_sel.shape, NEG, F32)
    acc_sel[...] = jnp.zeros(acc_sel.shape, F32)
    n_win = WINDOW // QT + 1
    jobs = []
    for h in heads:
        jobs.append((q_augs[h], [k_aug(ksel_ref, qb)], [chosen(h, qb) & causal], [tile_offs(h, qb)],
                     [[v_ext(vtsel_ref, h, qb)]], (m_sel, acc_sel, h)))
    for h in heads:
        k_tiles, masks, offs, vts = [], [], [], []
        for u in range(n_win):
            raw = qb - (n_win - 1) + u
            t = jnp.maximum(raw, 0)
            k_tiles.append(k_aug(kwin_ref, t))
            masks.append(in_window if u == 0 else causal if u == n_win - 1 else None)
            offs.append(tile_offs(h, t, raw >= 0))
            vts.append(v_ext(vtwin_ref, h, t))
        jobs.append((q_augs[h], k_tiles, masks, offs, [vts], None))
    results = _softmax_jobs(jobs)
    for h in heads:
        o_w = _gated(gt_ref, h, 2, results[NSA_KV_HEADS + h][0])
        for g in range(NSA_GROUP):
            head = NSA_GROUP * h + g
            part_ref[head] = oc_ref[0, head] + o_w[g]

    def sel_body(i, carry):
        jobs = []
        for h in heads:
            k_tiles, masks, offs, vts = [], [], [], []
            for u in range(SEL_UNROLL):
                idx = i * SEL_UNROLL + u
                live = idx < n_tiles[h]
                t = jnp.where(live, list_ref[slot[h] * nb + jnp.minimum(idx, nb - 1)], 0)
                k_tiles.append(k_aug(ksel_ref, t))
                masks.append(chosen(h, t))
                offs.append(tile_offs(h, t, live))
                vts.append(v_ext(vtsel_ref, h, t))
            jobs.append((q_augs[h], k_tiles, masks, offs, [vts], (m_sel, acc_sel, h)))
        _softmax_jobs(jobs)
        return carry

    n_steps = (functools.reduce(jnp.maximum, n_tiles) + SEL_UNROLL - 1) // SEL_UNROLL
    lax.fori_loop(0, n_steps, sel_body, 0)

    for h in heads:
        o_s = _gated(gt_ref, h, 1, acc_sel[h])
        for g in range(NSA_GROUP):
            head = NSA_GROUP * h + g
            o_ref[0, head] = part_ref[head] + o_s[g]


def _tile_lists(cnt, batch, nb):
    pair = cnt.reshape(batch, nb, NSA_KV_HEADS, nb, 2).sum(axis=-1)
    p_io = lax.broadcasted_iota(jnp.int32, pair.shape, 3)
    flag = (pair > 0.5) & (p_io < lax.broadcasted_iota(jnp.int32, pair.shape, 1))
    rank = jnp.cumsum(flag.astype(jnp.int32), axis=-1) - 1
    slot_io = lax.broadcasted_iota(jnp.int32, pair.shape + (nb,), 4)
    hit = flag[..., None] & (rank[..., None] == slot_io)
    lists = jnp.sum(jnp.where(hit, p_io[..., None], 0), axis=3)
    counts = jnp.sum(flag.astype(jnp.int32), axis=-1)
    return counts.reshape(-1), lists.reshape(-1)


def _nsa_sel(counts, lists, qpad3, ksel, vtsel, kwin, vtwin, selm, gates_t, oc, batch, seq):
    nb = seq // QT
    res = lambda b, i, c, l: (b, 0, 0)
    vt_rows = NSA_KV_HEADS * VT_ROWS
    grid_spec = pltpu.PrefetchScalarGridSpec(
        num_scalar_prefetch=2,
        grid=(batch, nb),
        in_specs=[pl.BlockSpec((1, QPAD_W, QT), lambda b, i, c, l: (b, 0, i)),
                  pl.BlockSpec((1, seq, KV_WIDTH), res),
                  pl.BlockSpec((1, vt_rows, seq), res),
                  pl.BlockSpec((1, seq, KV_WIDTH), res),
                  pl.BlockSpec((1, vt_rows, seq), res),
                  pl.BlockSpec((1, NSA_KV_HEADS, 1, 2 * nb, QT), lambda b, i, c, l: (b, 0, i, 0, 0)),
                  pl.BlockSpec((1, N_GATES, QT), lambda b, i, c, l: (b, 0, i)),
                  pl.BlockSpec((1, NSA_Q_HEADS, NSA_HEAD_DIM, QT), lambda b, i, c, l: (b, 0, 0, i))],
        out_specs=pl.BlockSpec((1, NSA_Q_HEADS, NSA_HEAD_DIM, QT), lambda b, i, c, l: (b, 0, 0, i)),
        scratch_shapes=[pltpu.VMEM((NSA_KV_HEADS, 1, NSA_GROUP * QT), F32),
                        pltpu.VMEM((NSA_KV_HEADS, VT_ROWS, NSA_GROUP * QT), F32),
                        pltpu.VMEM((NSA_KV_HEADS, LANES, NSA_GROUP * QT), BF16),
                        pltpu.VMEM((QT, LANES), BF16),
                        pltpu.VMEM((NSA_Q_HEADS, NSA_HEAD_DIM, QT), F32)],
    )
    return pl.pallas_call(
        _nsa_sel_kernel,
        grid_spec=grid_spec,
        out_shape=jax.ShapeDtypeStruct((batch, NSA_Q_HEADS, NSA_HEAD_DIM, seq), F32),
        compiler_params=_cparams(2),
        name="nsa_sel",
    )(counts, lists, qpad3, ksel, vtsel, kwin, vtwin, selm, gates_t, oc)


def _mlstm_kernel(q_ref, k_ref, v_ref, og_ref, misc_ref, misct_ref, cw_ref, cb_ref, nw_ref,
                  o_ref, xq_ref, xk_ref, c_ref, n_ref, m_ref):
    tm = q_ref.shape[0]
    L = MLSTM_L
    hd = MLSTM_HEAD_DIM
    halo = 8

    @pl.when(pl.program_id(1) == 0)
    def _():
        xq_ref[0:halo, :] = jnp.zeros((halo, MLSTM_WIDTH), F32)
        xk_ref[0:halo, :] = jnp.zeros((halo, MLSTM_WIDTH), F32)
        c_ref[...] = jnp.zeros_like(c_ref)
        n_ref[...] = jnp.zeros_like(n_ref)
        m_ref[...] = jnp.zeros_like(m_ref)

    def conv(x_ref, buf_ref, col0):
        buf_ref[halo:halo + tm, :] = x_ref[...]
        y = cb_ref[:, col0:col0 + MLSTM_WIDTH]
        for kk in range(CONV_WIDTH):
            r0 = halo - (CONV_WIDTH - 1) + kk
            y = y + cw_ref[kk:kk + 1, col0:col0 + MLSTM_WIDTH] * buf_ref[r0:r0 + tm, :]
        tail = buf_ref[tm:tm + halo, :]
        buf_ref[0:halo, :] = tail
        return _silu(y)

    qc = conv(q_ref, xq_ref, 0)
    kc = conv(k_ref, xk_ref, MLSTM_WIDTH) * (hd ** -0.5)

    misc = misc_ref[...]
    misct = misct_ref[0]
    lf_c = -jnp.log(1.0 + jnp.exp(-misc))
    lf_r = -jnp.log(1.0 + jnp.exp(-misct))
    r_io = lax.broadcasted_iota(jnp.int32, (tm, tm), 0)
    c_io = lax.broadcasted_iota(jnp.int32, (tm, tm), 1)
    same = (r_io // L) == (c_io // L)
    tril_blk = jnp.where(same & (c_io <= r_io), 1.0, 0.0)
    triu_blk = jnp.where(same & (r_io <= c_io), 1.0, 0.0)
    b_c = jnp.dot(tril_blk, lf_c, preferred_element_type=F32, precision=HIGHEST)
    b_r = jnp.dot(lf_r, triu_blk, preferred_element_type=F32, precision=HIGHEST)

    tri = lax.broadcasted_iota(jnp.int32, (L, L), 1) <= lax.broadcasted_iota(jnp.int32, (L, L), 0)

    for ci in range(tm // L):
        r0 = ci * L
        for h in range(MLSTM_HEADS):
            cs = slice(h * hd, (h + 1) * hd)
            q = qc[r0:r0 + L, cs]
            k = kc[r0:r0 + L, cs]
            v = v_ref[r0:r0 + L, cs]
            bcol = b_c[r0:r0 + L, MISC_F + h:MISC_F + h + 1]
            licol = misc[r0:r0 + L, MISC_I + h:MISC_I + h + 1]
            brow = b_r[MISC_F + h:MISC_F + h + 1, r0:r0 + L]
            lirow = misct[MISC_I + h:MISC_I + h + 1, r0:r0 + L]
            m_prev = m_ref[h]
            cmat = c_ref[h]
            nrow = n_ref[h]

            dmat = jnp.where(tri, bcol - brow + lirow, NEG)
            m_inter = bcol + m_prev
            m_t = jnp.maximum(jnp.max(dmat, axis=-1, keepdims=True), m_inter)
            qb16 = q.astype(BF16)
            kb16 = k.astype(BF16)
            vb16 = v.astype(BF16)
            w = jnp.exp(dmat - m_t) * _dot_nt(qb16, kb16)
            decay = jnp.exp(m_inter - m_t)
            num = _dot(w.astype(BF16), vb16) + decay * _dot(qb16, cmat.astype(BF16))
            den = jnp.sum(w, axis=-1, keepdims=True) + decay * jnp.sum(q * nrow, axis=-1, keepdims=True)
            hout = num / jnp.maximum(jnp.abs(den), jnp.exp(-m_t))

            b_last = brow[:, L - 1:L]
            gcol = b_last - bcol + licol
            m_new = jnp.maximum(b_last + m_prev, jnp.max(gcol, axis=0, keepdims=True))
            wk = jnp.exp(gcol - m_new)
            d_c = jnp.exp(b_last + m_prev - m_new)
            kw = wk * k
            c_ref[h] = d_c * cmat + _dot_tn(kw.astype(BF16), vb16)
            n_ref[h] = d_c * nrow + jnp.sum(kw, axis=0, keepdims=True)
            m_ref[h] = m_new

            hn = hout * lax.rsqrt(jnp.mean(hout * hout, axis=-1, keepdims=True) + EPS)
            hn = hn * nw_ref[:, cs]
            o_ref[r0:r0 + L, cs] = (hn * _sigmoid(og_ref[r0:r0 + L, cs])).astype(o_ref.dtype)


def _mlstm(ml, misc, misc_t, conv_w, conv_b, norm_w, batch, seq):
    t = ml.shape[0]
    tm = TOKEN_TILE
    tpb = seq // tm
    col = lambda j: (lambda b, i: (b * tpb + i, j))
    const = lambda b, i: (0, 0)
    return pl.pallas_call(
        _mlstm_kernel,
        grid=(batch, tpb),
        in_specs=[pl.BlockSpec((tm, MLSTM_WIDTH), col(0)),
                  pl.BlockSpec((tm, MLSTM_WIDTH), col(1)),
                  pl.BlockSpec((tm, MLSTM_WIDTH), col(2)),
                  pl.BlockSpec((tm, MLSTM_WIDTH), col(3)),
                  pl.BlockSpec((tm, LANES), col(0)),
                  pl.BlockSpec((1, LANES, tm), lambda b, i: (b, 0, i)),
                  pl.BlockSpec(conv_w.shape, const),
                  pl.BlockSpec(conv_b.shape, const),
                  pl.BlockSpec(norm_w.shape, const)],
        out_specs=pl.BlockSpec((tm, MLSTM_WIDTH), col(0)),
        out_shape=jax.ShapeDtypeStruct((t, MLSTM_WIDTH), BF16),
        scratch_shapes=[pltpu.VMEM((tm + 8, MLSTM_WIDTH), F32),
                        pltpu.VMEM((tm + 8, MLSTM_WIDTH), F32),
                        pltpu.VMEM((MLSTM_HEADS, MLSTM_HEAD_DIM, MLSTM_HEAD_DIM), F32),
                        pltpu.VMEM((MLSTM_HEADS, 1, MLSTM_HEAD_DIM), F32),
                        pltpu.VMEM((MLSTM_HEADS, 1, 1), F32)],
        compiler_params=_cparams(2),
        name="mlstm",
    )(ml, ml, ml, ml, misc, misc_t, conv_w, conv_b, norm_w)


def _layer(x2d, c, batch, seq, w_ada, b_ada, norm_ffn1_w, ffn1_w_in, ffn1_w_out, norm_mix_w, w_in, b_in,
           cmp_k_pe, cmp_k_w1, cmp_k_w2, cmp_v_pe, cmp_v_w1, cmp_v_w2, conv_w, conv_b, mlstm_norm_w, w_out,
           norm_ffn2_w, ffn2_w_in, ffn2_w_out, final_nw):
    d = x2d.shape[1]
    tpb = seq // TOKEN_TILE
    c8 = jnp.concatenate([c, jnp.zeros((8 - batch % 8 if batch % 8 else 0, d), c.dtype)], axis=0)
    mod = _adaln(c8, w_ada, b_ada.reshape(1, -1))[:batch].reshape(batch, N_MOD, d)

    x1 = _ffn(x2d, mod, norm_ffn1_w.reshape(1, d), ffn1_w_in.astype(BF16), ffn1_w_out.astype(BF16),
              (0, 1, 2), tpb)

    w_pack, b_pack = _pack_inproj(w_in, b_in)
    (qpad, kc, vc, ksel, kwin, vtsel, vtwin, ml, misc, misc_t) = _inproj(
        x1, mod, norm_mix_w.reshape(1, d), w_pack, b_pack, batch, seq)

    n_half = seq // CMP_STRIDE
    pk = _pack_compress(cmp_k_pe, cmp_k_w1, cmp_k_w2)
    pv = _pack_compress(cmp_v_pe, cmp_v_w1, cmp_v_w2)
    stacked = [jnp.stack([a, bb]) for a, bb in zip(pk, pv)]
    kcmp, vtcmp = _compress(kc.reshape(batch, n_half, CMP_STRIDE * KV_WIDTH),
                            vc.reshape(batch, n_half, CMP_STRIDE * KV_WIDTH), *stacked)

    nb = seq // QT
    qpad3 = qpad
    gates_t = misc_t[:, :N_GATES, :]
    oc, sel, cnt = _nsa_cmp(qpad3, kcmp, vtcmp, gates_t, batch, seq)
    counts, lists = _tile_lists(cnt, batch, nb)
    o_t = _nsa_sel(counts, lists, qpad3, ksel.reshape(batch, seq, KV_WIDTH), vtsel, kwin.reshape(batch, seq, KV_WIDTH),
                   vtwin, sel, gates_t, oc, batch, seq)
    o_nsa = o_t.reshape(batch, NSA_WIDTH, seq)

    o_ml = _mlstm(ml, misc, misc_t, conv_w, conv_b.reshape(1, -1), mlstm_norm_w.reshape(1, -1), batch, seq)

    return _ffn(x1, mod, norm_ffn2_w.reshape(1, d), ffn2_w_in.astype(BF16), ffn2_w_out.astype(BF16),
                (6, 7, 8), tpb, mix=(o_nsa, o_ml, w_out.astype(BF16)), mix_gate_row=5, final_nw=final_nw)


def kernel(x, c, w_ada, b_ada, norm_ffn1_w, ffn1_w_in, ffn1_w_out, norm_mix_w, w_in, b_in, cmp_k_pe, cmp_k_w1, cmp_k_w2, cmp_v_pe, cmp_v_w1, cmp_v_w2, conv_w, conv_b, mlstm_norm_w, w_out, norm_ffn2_w, ffn2_w_in, ffn2_w_out, final_norm_w):
    batch, seq, d = x.shape
    depth = w_ada.shape[0]
    assert depth == 1 and seq % (QT * CMP_STRIDE) == 0 and seq // SEL_BLOCK >= SEL_TOPK
    y = _layer(x.reshape(batch * seq, d), c, batch, seq, w_ada[0], b_ada[0], norm_ffn1_w[0], ffn1_w_in[0],
               ffn1_w_out[0], norm_mix_w[0], w_in[0], b_in[0], cmp_k_pe[0], cmp_k_w1[0], cmp_k_w2[0],
               cmp_v_pe[0], cmp_v_w1[0], cmp_v_w2[0], conv_w[0], conv_b[0], mlstm_norm_w[0], w_out[0],
               norm_ffn2_w[0], ffn2_w_in[0], ffn2_w_out[0], final_norm_w.reshape(1, d))
    return y.reshape(batch, seq, d)
```

```python
import functools

import numpy as np
import jax
import jax.numpy as jnp
from jax import lax
from jax.experimental import pallas as pl
from jax.experimental.pallas import tpu as pltpu

NSA_Q_HEADS = 8
NSA_KV_HEADS = 2
NSA_GROUP = NSA_Q_HEADS // NSA_KV_HEADS
NSA_HEAD_DIM = 64
CMP_BLOCK = 32
CMP_STRIDE = 16
CMP_HIDDEN = 128
SEL_BLOCK = 64
SEL_TOPK = 16
WINDOW = 512
FORCE_BONUS = 1.0e4
MLSTM_HEADS = 4
MLSTM_HEAD_DIM = 128
CONV_WIDTH = 4
D_FF = 2816
N_MOD = 9
EPS = 1e-6
NEG = -1e30

NSA_WIDTH = NSA_Q_HEADS * NSA_HEAD_DIM
KV_WIDTH = NSA_KV_HEADS * NSA_HEAD_DIM
MLSTM_WIDTH = MLSTM_HEADS * MLSTM_HEAD_DIM
N_GATES = 3 * NSA_Q_HEADS

LANES = 128
QT = 128
TOKEN_TILE = 512
FF_CHUNK = 256
MLSTM_L = 128
SEL_UNROLL = 6
TOPK_ROWS = 32
VT_ROWS = 80
VMEM_LIMIT = 56 * 1024 * 1024

F32 = jnp.float32
BF16 = jnp.bfloat16
HIGHEST = lax.Precision.HIGHEST
LOG2E = 1.4426950408889634


def _cparams(n_axes):
    return pltpu.CompilerParams(dimension_semantics=("arbitrary",) * n_axes,
                                vmem_limit_bytes=VMEM_LIMIT)


def _dot(a, b):
    return jnp.dot(a, b, preferred_element_type=F32)


def _dot_nt(a, b):
    return lax.dot_general(a, b, (((1,), (1,)), ((), ())), preferred_element_type=F32)


def _dot_tn(a, b):
    return lax.dot_general(a, b, (((0,), (0,)), ((), ())), preferred_element_type=F32)


def _sigmoid(x):
    return 1.0 / (1.0 + jnp.exp(-x))


def _silu(x):
    return x * _sigmoid(x)


def _norm_mod(x, nw, sh, sc):
    ms = jnp.mean(x * x, axis=-1, keepdims=True)
    y = x * lax.rsqrt(ms + EPS) * nw
    return y * (1.0 + sc) + sh


def _adaln_kernel(c_ref, w_ref, b_ref, o_ref):
    a = _silu(c_ref[...])
    o_ref[...] = jnp.dot(a, w_ref[...], preferred_element_type=F32, precision=HIGHEST) + b_ref[...]


def _adaln(c8, w_ada, b_ada):
    rows, d = c8.shape
    n = w_ada.shape[1]
    tn = n // N_MOD
    return pl.pallas_call(
        _adaln_kernel,
        grid=(N_MOD,),
        in_specs=[pl.BlockSpec((rows, d), lambda j: (0, 0)),
                  pl.BlockSpec((d, tn), lambda j: (0, j)),
                  pl.BlockSpec((1, tn), lambda j: (0, j))],
        out_specs=pl.BlockSpec((rows, tn), lambda j: (0, j)),
        out_shape=jax.ShapeDtypeStruct((rows, n), F32),
        compiler_params=_cparams(1),
        name="adaln",
    )(c8, w_ada, b_ada)


def _ffn_kernel(*refs, mod_rows, with_mix, mix_gate_row, with_final):
    it = iter(refs)
    x_ref = next(it)
    mod_ref = next(it)
    nw_ref = next(it)
    win_ref = next(it)
    wo_ref = next(it)
    if with_mix:
        ma_ref = next(it)
        mb_ref = next(it)
        wmix_ref = next(it)
    if with_final:
        fnw_ref = next(it)
    o_ref = next(it)

    sh_row, sc_row, g_row = mod_rows
    x = x_ref[...]
    if with_mix:
        mix_t = jnp.concatenate([ma_ref[0].astype(BF16), mb_ref[0].astype(BF16)], axis=0)
        x = x + mod_ref[0, mix_gate_row:mix_gate_row + 1, :] * _dot_tn(mix_t, wmix_ref[...])
    h = _norm_mod(x, nw_ref[...], mod_ref[0, sh_row:sh_row + 1, :], mod_ref[0, sc_row:sc_row + 1, :])
    hb = h.astype(BF16)
    d_ff = wo_ref.shape[0]
    n_chunks = d_ff // FF_CHUNK
    acc = None
    for j in range(n_chunks):
        c0 = j * FF_CHUNK
        g = _dot(hb, win_ref[:, c0:c0 + FF_CHUNK])
        u = _dot(hb, win_ref[:, d_ff + c0:d_ff + c0 + FF_CHUNK])
        act = (_silu(g) * u).astype(BF16)
        part = _dot(act, wo_ref[c0:c0 + FF_CHUNK, :])
        acc = part if acc is None else acc + part
    y = x + 0.5 * mod_ref[0, g_row:g_row + 1, :] * acc
    if with_final:
        ms = jnp.mean(y * y, axis=-1, keepdims=True)
        y = y * lax.rsqrt(ms + EPS) * fnw_ref[...]
    o_ref[...] = y


def _ffn(x2d, mod, nw, w_in_bf, w_out_bf, mod_rows, tiles_per_batch, mix=None, mix_gate_row=None, final_nw=None):
    t, d = x2d.shape
    tm = TOKEN_TILE
    d_ff = w_out_bf.shape[0]
    assert d_ff % FF_CHUNK == 0
    row = lambda i: (i, 0)
    const = lambda i: (0, 0)
    in_specs = [pl.BlockSpec((tm, d), row),
                pl.BlockSpec((1, N_MOD, d), lambda i: (i // tiles_per_batch, 0, 0)),
                pl.BlockSpec((1, d), const),
                pl.BlockSpec(w_in_bf.shape, const),
                pl.BlockSpec(w_out_bf.shape, const)]
    args = [x2d, mod, nw, w_in_bf, w_out_bf]
    if mix is not None:
        ma, mb, wmix = mix
        trow = lambda i: (i // tiles_per_batch, 0, i % tiles_per_batch)
        in_specs += [pl.BlockSpec((1, ma.shape[1], tm), trow),
                     pl.BlockSpec((1, mb.shape[1], tm), trow),
                     pl.BlockSpec(wmix.shape, const)]
        args += [ma, mb, wmix]
    if final_nw is not None:
        in_specs.append(pl.BlockSpec((1, d), const))
        args.append(final_nw)
    kern = functools.partial(_ffn_kernel, mod_rows=mod_rows, with_mix=mix is not None,
                             mix_gate_row=mix_gate_row, with_final=final_nw is not None)
    return pl.pallas_call(
        kern,
        grid=(t // tm,),
        in_specs=in_specs,
        out_specs=pl.BlockSpec((tm, d), row),
        out_shape=jax.ShapeDtypeStruct((t, d), F32),
        compiler_params=_cparams(1),
        name="ffn_mix" if mix is not None else "ffn",
    )(*args)


QPAD_W = NSA_Q_HEADS * LANES
KV6_OFF = QPAD_W
ML_OFF = KV6_OFF + 6 * KV_WIDTH
MISC_OFF = ML_OFF + 4 * MLSTM_WIDTH
PROJ_W = MISC_OFF + LANES
MISC_I = N_GATES
MISC_F = N_GATES + MLSTM_HEADS


def _vt_with_ones(v):
    vt = v.T
    n = vt.shape[1]
    pad = jnp.where(lax.broadcasted_iota(jnp.int32, (VT_ROWS - NSA_HEAD_DIM, n), 0) == 0, 1.0, 0.0)
    parts = []
    for h in range(NSA_KV_HEADS):
        parts += [vt[h * NSA_HEAD_DIM:(h + 1) * NSA_HEAD_DIM, :], pad]
    return jnp.concatenate(parts, axis=0).astype(BF16)


def _inproj_kernel(x_ref, mod_ref, nw_ref, w_ref, b_ref,
                   qpad_ref, kc_ref, vc_ref, ksel_ref, kwin_ref, vtsel_ref, vtwin_ref,
                   ml_ref, vtm_ref, ogt_ref, misc_ref, misct_ref):
    h = _norm_mod(x_ref[...], nw_ref[...], mod_ref[0, 3:4, :], mod_ref[0, 4:5, :])
    hb = h.astype(BF16)

    def proj(c0, width):
        return _dot(hb, w_ref[:, c0:c0 + width]) + b_ref[:, c0:c0 + width]

    scale = NSA_HEAD_DIM ** -0.5 * LOG2E
    for i in range(0, NSA_Q_HEADS, 2):
        pair = proj(i * LANES, 2 * LANES) * scale
        qpad_ref[0, i * LANES:(i + 1) * LANES, :] = pair[:, :LANES].T.astype(BF16)
        qpad_ref[0, (i + 1) * LANES:(i + 2) * LANES, :] = pair[:, LANES:].T.astype(BF16)
    kv_c = proj(KV6_OFF, 2 * KV_WIDTH)
    kc_ref[...] = kv_c[:, :KV_WIDTH].astype(BF16)
    vc_ref[...] = kv_c[:, KV_WIDTH:].astype(BF16)
    kv_s = proj(KV6_OFF + 2 * KV_WIDTH, 2 * KV_WIDTH)
    ksel_ref[...] = kv_s[:, :KV_WIDTH].astype(BF16)
    vtsel_ref[0] = _vt_with_ones(kv_s[:, KV_WIDTH:])
    kv_w = proj(KV6_OFF + 4 * KV_WIDTH, 2 * KV_WIDTH)
    kwin_ref[...] = kv_w[:, :KV_WIDTH].astype(BF16)
    vtwin_ref[0] = _vt_with_ones(kv_w[:, KV_WIDTH:])
    for i in range(2):
        ml_ref[:, i * MLSTM_WIDTH:(i + 1) * MLSTM_WIDTH] = proj(ML_OFF + i * MLSTM_WIDTH, MLSTM_WIDTH)
    vtm_ref[0] = proj(ML_OFF + 2 * MLSTM_WIDTH, MLSTM_WIDTH).T.astype(BF16)
    ogt_ref[0] = proj(ML_OFF + 3 * MLSTM_WIDTH, MLSTM_WIDTH).T
    misc = proj(MISC_OFF, LANES)
    misc_ref[...] = misc
    misct_ref[0] = misc.T


def _inproj(x2d, mod, nw, w_pack, b_pack, batch, seq):
    t, d = x2d.shape
    tm = TOKEN_TILE
    tpb = seq // tm
    row = lambda i: (i, 0)
    const = lambda i: (0, 0)
    trow = lambda i: (i // tpb, 0, i % tpb)
    vt_rows = NSA_KV_HEADS * VT_ROWS
    out_shapes = [
        jax.ShapeDtypeStruct((batch, QPAD_W, seq), BF16),
        jax.ShapeDtypeStruct((t, KV_WIDTH), BF16),
        jax.ShapeDtypeStruct((t, KV_WIDTH), BF16),
        jax.ShapeDtypeStruct((t, KV_WIDTH), BF16),
        jax.ShapeDtypeStruct((t, KV_WIDTH), BF16),
        jax.ShapeDtypeStruct((batch, vt_rows, seq), BF16),
        jax.ShapeDtypeStruct((batch, vt_rows, seq), BF16),
        jax.ShapeDtypeStruct((t, 2 * MLSTM_WIDTH), F32),
        jax.ShapeDtypeStruct((batch, MLSTM_WIDTH, seq), BF16),
        jax.ShapeDtypeStruct((batch, MLSTM_WIDTH, seq), F32),
        jax.ShapeDtypeStruct((t, LANES), F32),
        jax.ShapeDtypeStruct((batch, LANES, seq), F32),
    ]
    out_specs = [
        pl.BlockSpec((1, QPAD_W, tm), trow),
        pl.BlockSpec((tm, KV_WIDTH), row),
        pl.BlockSpec((tm, KV_WIDTH), row),
        pl.BlockSpec((tm, KV_WIDTH), row),
        pl.BlockSpec((tm, KV_WIDTH), row),
        pl.BlockSpec((1, vt_rows, tm), trow),
        pl.BlockSpec((1, vt_rows, tm), trow),
        pl.BlockSpec((tm, 2 * MLSTM_WIDTH), row),
        pl.BlockSpec((1, MLSTM_WIDTH, tm), trow),
        pl.BlockSpec((1, MLSTM_WIDTH, tm), trow),
        pl.BlockSpec((tm, LANES), row),
        pl.BlockSpec((1, LANES, tm), trow),
    ]
    return pl.pallas_call(
        _inproj_kernel,
        grid=(t // tm,),
        in_specs=[pl.BlockSpec((tm, d), row),
                  pl.BlockSpec((1, N_MOD, d), lambda i: (i // tpb, 0, 0)),
                  pl.BlockSpec((1, d), const),
                  pl.BlockSpec(w_pack.shape, const),
                  pl.BlockSpec(b_pack.shape, const)],
        out_specs=out_specs,
        out_shape=out_shapes,
        compiler_params=_cparams(1),
        name="inproj",
    )(x2d, mod, nw, w_pack, b_pack)


def _pack_inproj(w_in, b_in):
    d = w_in.shape[0]
    offs = np.cumsum([0, NSA_WIDTH] + [KV_WIDTH] * 6 + [N_GATES] + [MLSTM_WIDTH] * 4 + [MLSTM_HEADS] * 2)

    def pad_heads(a):
        a = a.reshape(a.shape[:-1] + (NSA_KV_HEADS, NSA_GROUP, NSA_HEAD_DIM))
        z = jnp.zeros_like(a[..., 0, :, :])
        lo = jnp.concatenate([a[..., 0, :, :], z], axis=-1)
        hi = jnp.concatenate([z, a[..., 1, :, :]], axis=-1)
        return jnp.stack([lo, hi], axis=-3).reshape(a.shape[:-3] + (QPAD_W,))

    wq_pad = pad_heads(w_in[:, :NSA_WIDTH])
    bq_pad = pad_heads(b_in[:NSA_WIDTH])
    g0, m0, i0 = int(offs[7]), int(offs[8]), int(offs[12])
    misc_w = jnp.concatenate([w_in[:, g0:g0 + N_GATES], w_in[:, i0:i0 + 2 * MLSTM_HEADS],
                              jnp.zeros((d, LANES - N_GATES - 2 * MLSTM_HEADS), w_in.dtype)], axis=1)
    misc_b = jnp.concatenate([b_in[g0:g0 + N_GATES], b_in[i0:i0 + 2 * MLSTM_HEADS],
                              jnp.zeros((LANES - N_GATES - 2 * MLSTM_HEADS,), b_in.dtype)])
    w_pack = jnp.concatenate([wq_pad, w_in[:, NSA_WIDTH:g0], w_in[:, m0:i0], misc_w], axis=1)
    b_pack = jnp.concatenate([bq_pad, b_in[NSA_WIDTH:g0], b_in[m0:i0], misc_b])
    assert w_pack.shape[1] == PROJ_W
    return w_pack.astype(BF16), b_pack.reshape(1, PROJ_W).astype(F32)


def _compress_kernel(xk_ref, xv_ref, wa_ref, wb_ref, pe_ref, w1_ref, w2_ref, kc_ref, vtc_ref):
    n_half = xk_ref.shape[1]

    def one(x_ref, s):
        x = x_ref[0]
        a = _dot(x, wa_ref[s])
        b = _dot(x, wb_ref[s])
        pe_term = jnp.dot(pe_ref[s], w1_ref[s], preferred_element_type=F32, precision=HIGHEST)[0:1, :]
        pe2 = jnp.concatenate([pe_term] * NSA_KV_HEADS, axis=1)
        pre = a + pltpu.roll(b, n_half - 1, 0) + pe2
        hid = 0.5 * pre * (1.0 + jnp.tanh(0.7978845608028654 * (pre + 0.044715 * pre * pre * pre)))
        return _dot(hid.astype(BF16), w2_ref[s])

    kc_ref[0] = one(xk_ref, 0).astype(BF16)
    vtc_ref[0] = _vt_with_ones(one(xv_ref, 1))


def _compress(xk, xv, wa, wb, pe8, w1, w2e):
    batch, n_half, width = xk.shape
    bsel = lambda b: (b, 0, 0)
    c3 = lambda b: (0, 0, 0)
    vt_rows = NSA_KV_HEADS * VT_ROWS
    return pl.pallas_call(
        _compress_kernel,
        grid=(batch,),
        in_specs=[pl.BlockSpec((1, n_half, width), bsel), pl.BlockSpec((1, n_half, width), bsel),
                  pl.BlockSpec(wa.shape, c3), pl.BlockSpec(wb.shape, c3),
                  pl.BlockSpec(pe8.shape, c3), pl.BlockSpec(w1.shape, c3), pl.BlockSpec(w2e.shape, c3)],
        out_specs=[pl.BlockSpec((1, n_half, KV_WIDTH), bsel), pl.BlockSpec((1, vt_rows, n_half), bsel)],
        out_shape=[jax.ShapeDtypeStruct((batch, n_half, KV_WIDTH), BF16),
                   jax.ShapeDtypeStruct((batch, vt_rows, n_half), BF16)],
        compiler_params=_cparams(1),
        name="compress",
    )(xk, xv, wa, wb, pe8, w1, w2e)


def _pack_compress(pe, w1, w2):
    half = CMP_BLOCK // 2
    w1r = w1.reshape(CMP_BLOCK, NSA_HEAD_DIM, CMP_HIDDEN)
    eye = jnp.eye(NSA_KV_HEADS, dtype=w1.dtype)

    def expand(w_half):
        z = w_half[:, None, :, None, :] * eye[None, :, None, :, None]
        return z.reshape(half * KV_WIDTH, NSA_KV_HEADS * CMP_HIDDEN)

    wa = expand(w1r[:half]).astype(BF16)
    wb = expand(w1r[half:]).astype(BF16)
    w2e = (w2[None, :, None, :] * eye[:, None, :, None]).reshape(NSA_KV_HEADS * CMP_HIDDEN, KV_WIDTH).astype(BF16)
    pe8 = jnp.concatenate([pe.reshape(1, CMP_BLOCK * NSA_HEAD_DIM),
                           jnp.zeros((7, CMP_BLOCK * NSA_HEAD_DIM), pe.dtype)], axis=0)
    return wa, wb, pe8, w1, w2e


def _q_aug(q_ref, qs_ref, h):
    qt = jnp.concatenate(
        [q_ref[0, (NSA_GROUP * h + g) * LANES:(NSA_GROUP * h + g + 1) * LANES, :] for g in range(NSA_GROUP)], axis=1)
    return jnp.concatenate([qt, qs_ref[h]], axis=0)


def _slope2(head):
    return float(2.0 ** (-8.0 * (head + 1) / NSA_Q_HEADS)) * LOG2E


def _bf16_split3(x):
    parts = []
    for _ in range(3):
        p = float(np.asarray(x, np.float32).astype(BF16).astype(np.float32))
        parts.append(p)
        x = x - p
    return parts


def _init_alibi_operands(qs_ref, kpos_ref, key_stride):
    lane = lax.broadcasted_iota(jnp.int32, (QT, LANES), 1)
    row = lax.broadcasted_iota(jnp.int32, (QT, LANES), 0)
    kpos_ref[...] = jnp.where(lane < 3, row.astype(F32), 0.0).astype(BF16)
    srow = lax.broadcasted_iota(jnp.int32, (LANES, QT), 0)
    for h in range(NSA_KV_HEADS):
        for g in range(NSA_GROUP):
            hi, mid, lo = _bf16_split3(key_stride * _slope2(NSA_GROUP * h + g))
            blk = jnp.where(srow == 0, hi, jnp.where(srow == 1, mid, jnp.where(srow == 2, lo, 0.0)))
            qs_ref[h, :, g * QT:(g + 1) * QT] = blk.astype(BF16)


def _gate_row(gt_ref, head, branch):
    r = head * 3 + branch
    return _sigmoid(gt_ref[0, r:r + 1, :])


def _softmax_jobs(jobs):
    scores = [_dot(jnp.concatenate(job[1], axis=0), job[0]) for job in jobs]
    staged = []
    for (q_aug, k_tiles, masks, offs, value_groups, state), st_all in zip(jobs, scores):
        n_t = len(k_tiles)
        per_g = []
        for g in range(NSA_GROUP):
            sl = slice(g * QT, (g + 1) * QT)
            ss = []
            for u in range(n_t):
                s = st_all[u * QT:(u + 1) * QT, sl]
                ss.append(s if masks[u] is None else jnp.where(masks[u], s, NEG))
            mx = None
            for u in range(n_t):
                cand = jnp.max(ss[u], axis=0, keepdims=True) + offs[u][g]
                mx = cand if mx is None else jnp.maximum(mx, cand)
            alpha = None
            if state is None:
                m_new = mx
            else:
                m_ref, _, h = state
                m_old = m_ref[h, :, sl]
                m_new = jnp.maximum(m_old, mx)
                alpha = jnp.exp2(m_old - m_new)
                m_ref[h, :, sl] = m_new
            m_use = jnp.where(m_new < 0.5 * NEG, 0.0, m_new)
            per_g.append((ss, m_use, alpha))
        staged.append(per_g)
    ets = []
    for (q_aug, k_tiles, masks, offs, value_groups, state), per_g in zip(jobs, staged):
        n_t = len(k_tiles)
        rows = [jnp.concatenate([jnp.exp2(per_g[g][0][u] - (per_g[g][1] - offs[u][g])).astype(BF16)
                                 for g in range(NSA_GROUP)], axis=1) for u in range(n_t)]
        ets.append(jnp.concatenate(rows, axis=0))
    results = []
    for (q_aug, k_tiles, masks, offs, value_groups, state), per_g, et in zip(jobs, staged, ets):
        mats = [jnp.concatenate(grp, axis=1) for grp in value_groups]
        pv_all = _dot(jnp.concatenate(mats, axis=0), et)
        pvs, r0 = [], 0
        for mat in mats:
            pvs.append(pv_all[r0:r0 + mat.shape[0], :])
            r0 += mat.shape[0]
        if state is None:
            results.append(pvs)
        else:
            _, acc_ref, h = state
            acc_ref[h] = jnp.concatenate([pg[2] for pg in per_g], axis=1) * acc_ref[h] + pvs[0]
            results.append(None)
    return results


def _gated(gt_ref, h, branch, pv):
    inv = 1.0 / jnp.maximum(pv[NSA_HEAD_DIM:NSA_HEAD_DIM + 1, :], 1e-30)
    out = []
    for g in range(NSA_GROUP):
        sl = slice(g * QT, (g + 1) * QT)
        out.append(pv[0:NSA_HEAD_DIM, sl] * (inv[:, sl] * _gate_row(gt_ref, NSA_GROUP * h + g, branch)))
    return out


def _nsa_cmp_kernel(q_ref, kc_ref, vtc_ref, gt_ref, oc_ref, sel_ref, cnt_ref,
                    qs_ref, kpos_ref, ovt_ref, pvo_ref, pvi_ref):
    ncp = kc_ref.shape[1]
    n_ct = ncp // QT
    n_sel = sel_ref.shape[3]
    start = pl.program_id(1) * QT
    tile_span = QT * CMP_STRIDE

    @pl.when((pl.program_id(0) == 0) & (pl.program_id(1) == 0))
    def _():
        _init_alibi_operands(qs_ref, kpos_ref, float(CMP_STRIDE))
        j_o = lax.broadcasted_iota(jnp.int32, (n_sel, ncp), 0)
        c_o = lax.broadcasted_iota(jnp.int32, (n_sel, ncp), 1)
        ov = ((c_o * CMP_STRIDE <= j_o * SEL_BLOCK + (SEL_BLOCK - 1))
              & (c_o * CMP_STRIDE + (CMP_BLOCK - 1) >= j_o * SEL_BLOCK))
        ovt_ref[...] = jnp.where(ov, 1.0, 0.0).astype(BF16)

    a0 = (lax.broadcasted_iota(jnp.int32, (QT, QT), 1)
          - CMP_STRIDE * lax.broadcasted_iota(jnp.int32, (QT, QT), 0))
    q_augs = [_q_aug(q_ref, qs_ref, h) for h in range(NSA_KV_HEADS)]
    n_vis = jnp.minimum((start + QT - CMP_BLOCK) // tile_span + 1, n_ct)

    for k in range(1, n_ct + 1):
        @pl.when(n_vis == k)
        def _(k=k):
            jobs = []
            for h in range(NSA_KV_HEADS):
                k_tiles, masks, offs, vts, ovs = [], [], [], [], []
                for ci in range(k):
                    cs = slice(ci * QT, (ci + 1) * QT)
                    first_end = ci * tile_span + CMP_BLOCK - 1
                    k_tiles.append(jnp.concatenate([kc_ref[0, cs, :], kpos_ref[...]], axis=1))
                    masks.append(a0 + (start - first_end) >= 0)
                    offs.append([_slope2(NSA_GROUP * h + g) * first_end for g in range(NSA_GROUP)])
                    vts.append(vtc_ref[0, h * VT_ROWS:(h + 1) * VT_ROWS, cs])
                    ovs.append(ovt_ref[:, cs])
                jobs.append((q_augs[h], k_tiles, masks, offs, [vts, ovs], None))
            for h, (pvo, pvi) in enumerate(_softmax_jobs(jobs)):
                pvo_ref[h] = pvo
                pvi_ref[h] = pvi

    j_io = lax.broadcasted_iota(jnp.int32, (n_sel, QT), 0)
    t_j = start + lax.broadcasted_iota(jnp.int32, (n_sel, QT), 1)
    cur = t_j // SEL_BLOCK
    valid = j_io * SEL_BLOCK <= t_j
    forced = (j_io == 0) | (j_io == cur) | (j_io == cur - 1)
    ones8 = jnp.ones((8, QT), BF16)

    scores = []
    for h in range(NSA_KV_HEADS):
        pvo = pvo_ref[h]
        o_c = _gated(gt_ref, h, 0, pvo)
        inv = 1.0 / jnp.maximum(pvo[NSA_HEAD_DIM:NSA_HEAD_DIM + 1, :], 1e-30)
        imp = None
        for g in range(NSA_GROUP):
            sl = slice(g * QT, (g + 1) * QT)
            oc_ref[0, NSA_GROUP * h + g] = o_c[g]
            term = pvi_ref[h, :, sl] * inv[:, sl]
            imp = term if imp is None else imp + term
        scores.append(jnp.where(valid, imp + jnp.where(forced, FORCE_BONUS, 0.0), -FORCE_BONUS))

    taken = -(2.0 ** 126)
    n_cls = n_sel // TOPK_ROWS
    cls = jnp.minimum((start + QT + SEL_BLOCK * TOPK_ROWS - 1) // (SEL_BLOCK * TOPK_ROWS), n_cls)

    for k in range(1, n_cls + 1):
        @pl.when(cls == k)
        def _(rows=k * TOPK_ROWS):
            j_sub = j_io[:rows]

            def pick(_, carry):
                out = []
                for sc in carry:
                    mx = jnp.max(sc, axis=0, keepdims=True)
                    idx = jnp.min(jnp.where(sc == mx, j_sub, n_sel), axis=0, keepdims=True)
                    out.append(jnp.where(j_sub == idx, taken, sc))
                return tuple(out)

            picked = lax.fori_loop(0, SEL_TOPK, pick, tuple(sc[:rows] for sc in scores))
            for h in range(NSA_KV_HEADS):
                sel = jnp.where(picked[h] <= 0.5 * taken, 1.0, 0.0)
                if rows < n_sel:
                    sel = jnp.concatenate([sel, jnp.zeros((n_sel - rows, QT), F32)], axis=0)
                sel_ref[0, h, 0] = sel
                cnt = _dot_nt(ones8, sel.astype(BF16))
                cnt_ref[0, 0, h] = cnt[0:1, :]


def _nsa_cmp(qpad3, kc, vtc, gates_t, batch, seq):
    nb = seq // QT
    n_sel = seq // SEL_BLOCK
    ncp = kc.shape[1]
    assert ncp % QT == 0
    vt_rows = NSA_KV_HEADS * VT_ROWS
    return pl.pallas_call(
        _nsa_cmp_kernel,
        grid=(batch, nb),
        in_specs=[pl.BlockSpec((1, QPAD_W, QT), lambda b, i: (b, 0, i)),
                  pl.BlockSpec((1, ncp, KV_WIDTH), lambda b, i: (b, 0, 0)),
                  pl.BlockSpec((1, vt_rows, ncp), lambda b, i: (b, 0, 0)),
                  pl.BlockSpec((1, N_GATES, QT), lambda b, i: (b, 0, i))],
        out_specs=[pl.BlockSpec((1, NSA_Q_HEADS, NSA_HEAD_DIM, QT), lambda b, i: (b, 0, 0, i)),
                   pl.BlockSpec((1, NSA_KV_HEADS, 1, n_sel, QT), lambda b, i: (b, 0, i, 0, 0)),
                   pl.BlockSpec((1, 1, NSA_KV_HEADS, 1, n_sel), lambda b, i: (b, i, 0, 0, 0))],
        out_shape=[jax.ShapeDtypeStruct((batch, NSA_Q_HEADS, NSA_HEAD_DIM, seq), F32),
                   jax.ShapeDtypeStruct((batch, NSA_KV_HEADS, nb, n_sel, QT), F32),
                   jax.ShapeDtypeStruct((batch, nb, NSA_KV_HEADS, 1, n_sel), F32)],
        scratch_shapes=[pltpu.VMEM((NSA_KV_HEADS, LANES, NSA_GROUP * QT), BF16),
                        pltpu.VMEM((QT, LANES), BF16),
                        pltpu.VMEM((n_sel, ncp), BF16),
                        pltpu.VMEM((NSA_KV_HEADS, VT_ROWS, NSA_GROUP * QT), F32),
                        pltpu.VMEM((NSA_KV_HEADS, n_sel, NSA_GROUP * QT), F32)],
        compiler_params=_cparams(2),
        name="nsa_cmp",
    )(qpad3, kc, vtc, gates_t)


def _nsa_sel_kernel(count_ref, list_ref, q_ref, ksel_ref, vtsel_ref, kwin_ref, vtwin_ref, selm_ref, gt_ref,
                    oc_ref, o_ref, m_sel, acc_sel, qs_ref, kpos_ref, part_ref):
    b = pl.program_id(0)
    qb = pl.program_id(1)
    nb = pl.num_programs(1)
    heads = range(NSA_KV_HEADS)
    k_io = lax.broadcasted_iota(jnp.int32, (QT, QT), 0)
    q_io = lax.broadcasted_iota(jnp.int32, (QT, QT), 1)
    causal = q_io >= k_io
    in_window = q_io < k_io
    first_half = k_io < SEL_BLOCK

    @pl.when((b == 0) & (qb == 0))
    def _():
        _init_alibi_operands(qs_ref, kpos_ref, 1.0)

    def k_aug(k_ref, t):
        off = pl.multiple_of(t * QT, QT)
        return jnp.concatenate([k_ref[0, pl.ds(off, QT), :], kpos_ref[...]], axis=1)

    def v_ext(vt_ref, h, t):
        return vt_ref[0, h * VT_ROWS:(h + 1) * VT_ROWS, pl.ds(pl.multiple_of(t * QT, QT), QT)]

    def tile_offs(h, t, live=None):
        pos = (t * QT).astype(F32)
        out = [_slope2(NSA_GROUP * h + g) * pos for g in range(NSA_GROUP)]
        return out if live is None else [jnp.where(live, o, NEG) for o in out]

    def chosen(h, t):
        mrow = selm_ref[0, h, 0, pl.ds(2 * t, 2), :]
        return jnp.where(first_half, mrow[0:1, :], mrow[1:2, :]) > 0.5

    q_augs = [_q_aug(q_ref, qs_ref, h) for h in heads]

    slot = [(b * nb + qb) * NSA_KV_HEADS + h for h in heads]
    n_tiles = [count_ref[slot[h]] for h in heads]

    m_sel[...] = jnp.full(m_sel.shape, NEG, F32)
    acc_sel[...] = jnp.zeros(acc_sel.shape, F32)
    n_win = WINDOW // QT + 1
    jobs = []
    for h in heads:
        jobs.append((q_augs[h], [k_aug(ksel_ref, qb)], [chosen(h, qb) & causal], [tile_offs(h, qb)],
                     [[v_ext(vtsel_ref, h, qb)]], (m_sel, acc_sel, h)))
    for h in heads:
        k_tiles, masks, offs, vts = [], [], [], []
        for u in range(n_win):
            raw = qb - (n_win - 1) + u
            t = jnp.maximum(raw, 0)
            k_tiles.append(k_aug(kwin_ref, t))
            masks.append(in_window if u == 0 else causal if u == n_win - 1 else None)
            offs.append(tile_offs(h, t, raw >= 0))
            vts.append(v_ext(vtwin_ref, h, t))
        jobs.append((q_augs[h], k_tiles, masks, offs, [vts], None))
    results = _softmax_jobs(jobs)
    for h in heads:
        o_w = _gated(gt_ref, h, 2, results[NSA_KV_HEADS + h][0])
        for g in range(NSA_GROUP):
            head = NSA_GROUP * h + g
            part_ref[head] = oc_ref[0, head] + o_w[g]

    def sel_body(i, carry):
        jobs = []
        for h in heads:
            k_tiles, masks, offs, vts = [], [], [], []
            for u in range(SEL_UNROLL):
                idx = i * SEL_UNROLL + u
                live = idx < n_tiles[h]
                t = jnp.where(live, list_ref[slot[h] * nb + jnp.minimum(idx, nb - 1)], 0)
                k_tiles.append(k_aug(ksel_ref, t))
                masks.append(chosen(h, t))
                offs.append(tile_offs(h, t, live))
                vts.append(v_ext(vtsel_ref, h, t))
            jobs.append((q_augs[h], k_tiles, masks, offs, [vts], (m_sel, acc_sel, h)))
        _softmax_jobs(jobs)
        return carry

    n_steps = (functools.reduce(jnp.maximum, n_tiles) + SEL_UNROLL - 1) // SEL_UNROLL
    lax.fori_loop(0, n_steps, sel_body, 0)

    for h in heads:
        o_s = _gated(gt_ref, h, 1, acc_sel[h])
        for g in range(NSA_GROUP):
            head = NSA_GROUP * h + g
            o_ref[0, head] = part_ref[head] + o_s[g]


def _tile_lists(cnt, batch, nb):
    pair = cnt.reshape(batch, nb, NSA_KV_HEADS, nb, 2).sum(axis=-1)
    p_io = lax.broadcasted_iota(jnp.int32, pair.shape, 3)
    flag = (pair > 0.5) & (p_io < lax.broadcasted_iota(jnp.int32, pair.shape, 1))
    rank = jnp.cumsum(flag.astype(jnp.int32), axis=-1) - 1
    slot_io = lax.broadcasted_iota(jnp.int32, pair.shape + (nb,), 4)
    hit = flag[..., None] & (rank[..., None] == slot_io)
    lists = jnp.sum(jnp.where(hit, p_io[..., None], 0), axis=3)
    counts = jnp.sum(flag.astype(jnp.int32), axis=-1)
    return counts.reshape(-1), lists.reshape(-1)


def _nsa_sel(counts, lists, qpad3, ksel, vtsel, kwin, vtwin, selm, gates_t, oc, batch, seq):
    nb = seq // QT
    res = lambda b, i, c, l: (b, 0, 0)
    vt_rows = NSA_KV_HEADS * VT_ROWS
    grid_spec = pltpu.PrefetchScalarGridSpec(
        num_scalar_prefetch=2,
        grid=(batch, nb),
        in_specs=[pl.BlockSpec((1, QPAD_W, QT), lambda b, i, c, l: (b, 0, i)),
                  pl.BlockSpec((1, seq, KV_WIDTH), res),
                  pl.BlockSpec((1, vt_rows, seq), res),
                  pl.BlockSpec((1, seq, KV_WIDTH), res),
                  pl.BlockSpec((1, vt_rows, seq), res),
                  pl.BlockSpec((1, NSA_KV_HEADS, 1, 2 * nb, QT), lambda b, i, c, l: (b, 0, i, 0, 0)),
                  pl.BlockSpec((1, N_GATES, QT), lambda b, i, c, l: (b, 0, i)),
                  pl.BlockSpec((1, NSA_Q_HEADS, NSA_HEAD_DIM, QT), lambda b, i, c, l: (b, 0, 0, i))],
        out_specs=pl.BlockSpec((1, NSA_Q_HEADS, NSA_HEAD_DIM, QT), lambda b, i, c, l: (b, 0, 0, i)),
        scratch_shapes=[pltpu.VMEM((NSA_KV_HEADS, 1, NSA_GROUP * QT), F32),
                        pltpu.VMEM((NSA_KV_HEADS, VT_ROWS, NSA_GROUP * QT), F32),
                        pltpu.VMEM((NSA_KV_HEADS, LANES, NSA_GROUP * QT), BF16),
                        pltpu.VMEM((QT, LANES), BF16),
                        pltpu.VMEM((NSA_Q_HEADS, NSA_HEAD_DIM, QT), F32)],
    )
    return pl.pallas_call(
        _nsa_sel_kernel,
        grid_spec=grid_spec,
        out_shape=jax.ShapeDtypeStruct((batch, NSA_Q_HEADS, NSA_HEAD_DIM, seq), F32),
        compiler_params=_cparams(2),
        name="nsa_sel",
    )(counts, lists, qpad3, ksel, vtsel, kwin, vtwin, selm, gates_t, oc)


def _mlstm_kernel(q_ref, k_ref, vt_ref, ogt_ref, misc_ref, misct_ref, cw_ref, cb_ref, nw_ref,
                  o_ref, xq_ref, xk_ref, c_ref, n_ref, m_ref):
    tm = q_ref.shape[0]
    L = MLSTM_L
    hd = MLSTM_HEAD_DIM
    halo = 8

    @pl.when(pl.program_id(1) == 0)
    def _():
        xq_ref[0:halo, :] = jnp.zeros((halo, MLSTM_WIDTH), F32)
        xk_ref[0:halo, :] = jnp.zeros((halo, MLSTM_WIDTH), F32)
        c_ref[...] = jnp.zeros_like(c_ref)
        n_ref[...] = jnp.zeros_like(n_ref)
        m_ref[...] = jnp.zeros_like(m_ref)

    def conv(x_ref, buf_ref, col0):
        buf_ref[halo:halo + tm, :] = x_ref[...]
        y = cb_ref[:, col0:col0 + MLSTM_WIDTH]
        for kk in range(CONV_WIDTH):
            r0 = halo - (CONV_WIDTH - 1) + kk
            y = y + cw_ref[kk:kk + 1, col0:col0 + MLSTM_WIDTH] * buf_ref[r0:r0 + tm, :]
        tail = buf_ref[tm:tm + halo, :]
        buf_ref[0:halo, :] = tail
        return _silu(y)

    qc = conv(q_ref, xq_ref, 0)
    kc = conv(k_ref, xk_ref, MLSTM_WIDTH) * (hd ** -0.5)

    misc = misc_ref[...]
    misct = misct_ref[0]
    lf_c = -jnp.log(1.0 + jnp.exp(-misc))
    lf_r = -jnp.log(1.0 + jnp.exp(-misct))
    row_io = lax.broadcasted_iota(jnp.int32, (L, L), 0)
    col_io = lax.broadcasted_iota(jnp.int32, (L, L), 1)
    tril = jnp.where(col_io <= row_io, 1.0, 0.0).astype(BF16)
    triu = jnp.where(row_io <= col_io, 1.0, 0.0).astype(BF16)

    def split3(x):
        hi = x.astype(BF16)
        r1 = x - hi.astype(F32)
        mid = r1.astype(BF16)
        return hi, mid, (r1 - mid.astype(F32)).astype(BF16)

    lf_c3 = split3(lf_c)
    lf_r3 = split3(lf_r)
    b_c = jnp.concatenate([sum(_dot(tril, p[ci * L:(ci + 1) * L, :]) for p in lf_c3)
                           for ci in range(tm // L)], axis=0)
    b_r = jnp.concatenate([sum(_dot(p[:, ci * L:(ci + 1) * L], triu) for p in lf_r3)
                           for ci in range(tm // L)], axis=1)

    tri_t = lax.broadcasted_iota(jnp.int32, (L, L), 0) <= lax.broadcasted_iota(jnp.int32, (L, L), 1)
    nw_cols = [jnp.broadcast_to(nw_ref[:, h * hd:(h + 1) * hd], (hd, hd)).T for h in range(MLSTM_HEADS)]

    def split3_rows(row):
        hi = row.astype(BF16)
        r1 = row - hi.astype(F32)
        mid = r1.astype(BF16)
        lo = (r1 - mid.astype(F32)).astype(BF16)
        sub = lax.broadcasted_iota(jnp.int32, (8, row.shape[1]), 0)
        out = jnp.where(sub == 0, hi.astype(F32), jnp.where(sub == 1, mid.astype(F32),
                                                            jnp.where(sub == 2, lo.astype(F32), 0.0)))
        return out.astype(BF16)

    def rows_sum3(x8):
        return x8[0:1, :] + x8[1:2, :] + x8[2:3, :]

    for ci in range(tm // L):
        r0 = ci * L
        for h in range(MLSTM_HEADS):
            cs = slice(h * hd, (h + 1) * hd)
            qb16 = qc[r0:r0 + L, cs].astype(BF16)
            kb16 = kc[r0:r0 + L, cs].astype(BF16)
            vt = vt_ref[0, cs, r0:r0 + L]
            a_col = (misc[r0:r0 + L, MISC_I + h:MISC_I + h + 1]
                     - b_c[r0:r0 + L, MISC_F + h:MISC_F + h + 1])
            brow = b_r[MISC_F + h:MISC_F + h + 1, r0:r0 + L]
            lirow = misct[MISC_I + h:MISC_I + h + 1, r0:r0 + L]
            m_prev = m_ref[h]
            ct = c_ref[h]
            nrow = n_ref[h]

            dmat = jnp.where(tri_t, brow + a_col, NEG)
            m_inter = brow + m_prev
            m_t = jnp.maximum(jnp.max(dmat, axis=0, keepdims=True), m_inter)
            w = jnp.exp(dmat - m_t) * _dot_nt(kb16, qb16)
            decay = jnp.exp(m_inter - m_t)
            num = _dot(vt, w.astype(BF16)) + decay * _dot_nt(ct.astype(BF16), qb16)
            nq = rows_sum3(_dot_nt(split3_rows(nrow), qb16))
            den = jnp.sum(w, axis=0, keepdims=True) + decay * nq
            hout = num * (1.0 / jnp.maximum(jnp.abs(den), jnp.exp(-m_t)))

            b_last = brow[:, L - 1:L]
            grow = b_last - brow + lirow
            m_new = jnp.maximum(b_last + m_prev, jnp.max(grow, axis=1, keepdims=True))
            wk = jnp.exp(grow - m_new)
            d_c = jnp.exp(b_last + m_prev - m_new)
            c_ref[h] = d_c * ct + _dot((vt.astype(F32) * wk).astype(BF16), kb16)
            n_ref[h] = d_c * nrow + rows_sum3(_dot(split3_rows(wk), kb16))
            m_ref[h] = m_new

            hn = hout * lax.rsqrt(jnp.mean(hout * hout, axis=0, keepdims=True) + EPS) * nw_cols[h]
            o_ref[0, cs, r0:r0 + L] = (hn * _sigmoid(ogt_ref[0, cs, r0:r0 + L])).astype(o_ref.dtype)


def _mlstm(ml, vt_m, og_t, misc, misc_t, conv_w, conv_b, norm_w, batch, seq):
    tm = TOKEN_TILE
    tpb = seq // tm
    col = lambda j: (lambda b, i: (b * tpb + i, j))
    const = lambda b, i: (0, 0)
    trow = lambda b, i: (b, 0, i)
    return pl.pallas_call(
        _mlstm_kernel,
        grid=(batch, tpb),
        in_specs=[pl.BlockSpec((tm, MLSTM_WIDTH), col(0)),
                  pl.BlockSpec((tm, MLSTM_WIDTH), col(1)),
                  pl.BlockSpec((1, MLSTM_WIDTH, tm), trow),
                  pl.BlockSpec((1, MLSTM_WIDTH, tm), trow),
                  pl.BlockSpec((tm, LANES), col(0)),
                  pl.BlockSpec((1, LANES, tm), trow),
                  pl.BlockSpec(conv_w.shape, const),
                  pl.BlockSpec(conv_b.shape, const),
                  pl.BlockSpec(norm_w.shape, const)],
        out_specs=pl.BlockSpec((1, MLSTM_WIDTH, tm), trow),
        out_shape=jax.ShapeDtypeStruct((batch, MLSTM_WIDTH, seq), BF16),
        scratch_shapes=[pltpu.VMEM((tm + 8, MLSTM_WIDTH), F32),
                        pltpu.VMEM((tm + 8, MLSTM_WIDTH), F32),
                        pltpu.VMEM((MLSTM_HEADS, MLSTM_HEAD_DIM, MLSTM_HEAD_DIM), F32),
                        pltpu.VMEM((MLSTM_HEADS, 1, MLSTM_HEAD_DIM), F32),
                        pltpu.VMEM((MLSTM_HEADS, 1, 1), F32)],
        compiler_params=_cparams(2),
        name="mlstm",
    )(ml, ml, vt_m, og_t, misc, misc_t, conv_w, conv_b, norm_w)


def _layer(x2d, c, batch, seq, w_ada, b_ada, norm_ffn1_w, ffn1_w_in, ffn1_w_out, norm_mix_w, w_in, b_in,
           cmp_k_pe, cmp_k_w1, cmp_k_w2, cmp_v_pe, cmp_v_w1, cmp_v_w2, conv_w, conv_b, mlstm_norm_w, w_out,
           norm_ffn2_w, ffn2_w_in, ffn2_w_out, final_nw):
    d = x2d.shape[1]
    tpb = seq // TOKEN_TILE
    c8 = jnp.concatenate([c, jnp.zeros((8 - batch % 8 if batch % 8 else 0, d), c.dtype)], axis=0)
    mod = _adaln(c8, w_ada, b_ada.reshape(1, -1))[:batch].reshape(batch, N_MOD, d)

    x1 = _ffn(x2d, mod, norm_ffn1_w.reshape(1, d), ffn1_w_in.astype(BF16), ffn1_w_out.astype(BF16),
              (0, 1, 2), tpb)

    w_pack, b_pack = _pack_inproj(w_in, b_in)
    (qpad, kc, vc, ksel, kwin, vtsel, vtwin, ml, vt_m, og_t, misc, misc_t) = _inproj(
        x1, mod, norm_mix_w.reshape(1, d), w_pack, b_pack, batch, seq)

    n_half = seq // CMP_STRIDE
    pk = _pack_compress(cmp_k_pe, cmp_k_w1, cmp_k_w2)
    pv = _pack_compress(cmp_v_pe, cmp_v_w1, cmp_v_w2)
    stacked = [jnp.stack([a, bb]) for a, bb in zip(pk, pv)]
    kcmp, vtcmp = _compress(kc.reshape(batch, n_half, CMP_STRIDE * KV_WIDTH),
                            vc.reshape(batch, n_half, CMP_STRIDE * KV_WIDTH), *stacked)

    nb = seq // QT
    qpad3 = qpad
    gates_t = misc_t[:, :N_GATES, :]
    oc, sel, cnt = _nsa_cmp(qpad3, kcmp, vtcmp, gates_t, batch, seq)
    counts, lists = _tile_lists(cnt, batch, nb)
    o_t = _nsa_sel(counts, lists, qpad3, ksel.reshape(batch, seq, KV_WIDTH), vtsel, kwin.reshape(batch, seq, KV_WIDTH),
                   vtwin, sel, gates_t, oc, batch, seq)
    o_nsa = o_t.reshape(batch, NSA_WIDTH, seq)

    o_ml = _mlstm(ml, vt_m, og_t, misc, misc_t, conv_w, conv_b.reshape(1, -1), mlstm_norm_w.reshape(1, -1), batch, seq)

    return _ffn(x1, mod, norm_ffn2_w.reshape(1, d), ffn2_w_in.astype(BF16), ffn2_w_out.astype(BF16),
                (6, 7, 8), tpb, mix=(o_nsa, o_ml, w_out.astype(BF16)), mix_gate_row=5, final_nw=final_nw)


def kernel(x, c, w_ada, b_ada, norm_ffn1_w, ffn1_w_in, ffn1_w_out, norm_mix_w, w_in, b_in, cmp_k_pe, cmp_k_w1, cmp_k_w2, cmp_v_pe, cmp_v_w1, cmp_v_w2, conv_w, conv_b, mlstm_norm_w, w_out, norm_ffn2_w, ffn2_w_in, ffn2_w_out, final_norm_w):
    batch, seq, d = x.shape
    depth = w_ada.shape[0]
    assert depth == 1 and seq % (QT * CMP_STRIDE) == 0 and seq // SEL_BLOCK >= SEL_TOPK
    y = _layer(x.reshape(batch * seq, d), c, batch, seq, w_ada[0], b_ada[0], norm_ffn1_w[0], ffn1_w_in[0],
               ffn1_w_out[0], norm_mix_w[0], w_in[0], b_in[0], cmp_k_pe[0], cmp_k_w1[0], cmp_k_w2[0],
               cmp_v_pe[0], cmp_v_w1[0], cmp_v_w2[0], conv_w[0], conv_b[0], mlstm_norm_w[0], w_out[0],
               norm_ffn2_w[0], ffn2_w_in[0], ffn2_w_out[0], final_norm_w.reshape(1, d))
    return y.reshape(batch, seq, d)
```

```python
import functools

import numpy as np
import jax
import jax.numpy as jnp
from jax import lax
from jax.experimental import pallas as pl
from jax.experimental.pallas import tpu as pltpu

NSA_Q_HEADS = 8
NSA_KV_HEADS = 2
NSA_GROUP = NSA_Q_HEADS // NSA_KV_HEADS
NSA_HEAD_DIM = 64
CMP_BLOCK = 32
CMP_STRIDE = 16
CMP_HIDDEN = 128
SEL_BLOCK = 64
SEL_TOPK = 16
WINDOW = 512
FORCE_BONUS = 1.0e4
N_FORCED = 3
MLSTM_HEADS = 4
MLSTM_HEAD_DIM = 128
CONV_WIDTH = 4
D_FF = 2816
N_MOD = 9
EPS = 1e-6
NEG = -1e30

NSA_WIDTH = NSA_Q_HEADS * NSA_HEAD_DIM
KV_WIDTH = NSA_KV_HEADS * NSA_HEAD_DIM
MLSTM_WIDTH = MLSTM_HEADS * MLSTM_HEAD_DIM
N_GATES = 3 * NSA_Q_HEADS

LANES = 128
QT = 128
TOKEN_TILE = 512
FF_CHUNK = 256
MLSTM_L = 128
SEL_UNROLL = 6
SEL_CHAIN = 2
TOPK_ROWS = 32
VT_ROWS = 80
VMEM_LIMIT = 56 * 1024 * 1024

F32 = jnp.float32
BF16 = jnp.bfloat16
HIGHEST = lax.Precision.HIGHEST
LOG2E = 1.4426950408889634


def _cparams(n_axes):
    return pltpu.CompilerParams(dimension_semantics=("arbitrary",) * n_axes,
                                vmem_limit_bytes=VMEM_LIMIT)


def _dot(a, b):
    return jnp.dot(a, b, preferred_element_type=F32)


def _dot_nt(a, b):
    return lax.dot_general(a, b, (((1,), (1,)), ((), ())), preferred_element_type=F32)


def _dot_tn(a, b):
    return lax.dot_general(a, b, (((0,), (0,)), ((), ())), preferred_element_type=F32)


def _sigmoid(x):
    return 1.0 / (1.0 + jnp.exp(-x))


def _silu(x):
    return x * _sigmoid(x)


def _norm_mod(x, nw, sh, sc):
    ms = jnp.mean(x * x, axis=-1, keepdims=True)
    y = x * lax.rsqrt(ms + EPS) * nw
    return y * (1.0 + sc) + sh


def _adaln_kernel(c_ref, w_ref, b_ref, o_ref):
    a = _silu(c_ref[...])
    o_ref[...] = jnp.dot(a, w_ref[...], preferred_element_type=F32, precision=HIGHEST) + b_ref[...]


def _adaln(c8, w_ada, b_ada):
    rows, d = c8.shape
    n = w_ada.shape[1]
    tn = n // N_MOD
    return pl.pallas_call(
        _adaln_kernel,
        grid=(N_MOD,),
        in_specs=[pl.BlockSpec((rows, d), lambda j: (0, 0)),
                  pl.BlockSpec((d, tn), lambda j: (0, j)),
                  pl.BlockSpec((1, tn), lambda j: (0, j))],
        out_specs=pl.BlockSpec((rows, tn), lambda j: (0, j)),
        out_shape=jax.ShapeDtypeStruct((rows, n), F32),
        compiler_params=_cparams(1),
        name="adaln",
    )(c8, w_ada, b_ada)


def _ffn_kernel(*refs, mod_rows, with_mix, mix_gate_row, with_final):
    it = iter(refs)
    x_ref = next(it)
    mod_ref = next(it)
    nw_ref = next(it)
    win_ref = next(it)
    wo_ref = next(it)
    if with_mix:
        ma_ref = next(it)
        mb_ref = next(it)
        wmix_ref = next(it)
    if with_final:
        fnw_ref = next(it)
    o_ref = next(it)

    sh_row, sc_row, g_row = mod_rows
    x = x_ref[...]
    if with_mix:
        mix_t = jnp.concatenate([ma_ref[0].astype(BF16), mb_ref[0].astype(BF16)], axis=0)
        x = x + mod_ref[0, mix_gate_row:mix_gate_row + 1, :] * _dot_tn(mix_t, wmix_ref[...])
    h = _norm_mod(x, nw_ref[...], mod_ref[0, sh_row:sh_row + 1, :], mod_ref[0, sc_row:sc_row + 1, :])
    hb = h.astype(BF16)
    d_ff = wo_ref.shape[0]
    n_chunks = d_ff // FF_CHUNK
    acc = None
    for j in range(n_chunks):
        c0 = j * FF_CHUNK
        g = _dot(hb, win_ref[:, c0:c0 + FF_CHUNK])
        u = _dot(hb, win_ref[:, d_ff + c0:d_ff + c0 + FF_CHUNK])
        act = (_silu(g) * u).astype(BF16)
        part = _dot(act, wo_ref[c0:c0 + FF_CHUNK, :])
        acc = part if acc is None else acc + part
    y = x + 0.5 * mod_ref[0, g_row:g_row + 1, :] * acc
    if with_final:
        ms = jnp.mean(y * y, axis=-1, keepdims=True)
        y = y * lax.rsqrt(ms + EPS) * fnw_ref[...]
    o_ref[...] = y


def _ffn(x2d, mod, nw, w_in_bf, w_out_bf, mod_rows, tiles_per_batch, mix=None, mix_gate_row=None, final_nw=None):
    t, d = x2d.shape
    tm = TOKEN_TILE
    d_ff = w_out_bf.shape[0]
    assert d_ff % FF_CHUNK == 0
    row = lambda i: (i, 0)
    const = lambda i: (0, 0)
    in_specs = [pl.BlockSpec((tm, d), row),
                pl.BlockSpec((1, N_MOD, d), lambda i: (i // tiles_per_batch, 0, 0)),
                pl.BlockSpec((1, d), const),
                pl.BlockSpec(w_in_bf.shape, const),
                pl.BlockSpec(w_out_bf.shape, const)]
    args = [x2d, mod, nw, w_in_bf, w_out_bf]
    if mix is not None:
        ma, mb, wmix = mix
        trow = lambda i: (i // tiles_per_batch, 0, i % tiles_per_batch)
        in_specs += [pl.BlockSpec((1, ma.shape[1], tm), trow),
                     pl.BlockSpec((1, mb.shape[1], tm), trow),
                     pl.BlockSpec(wmix.shape, const)]
        args += [ma, mb, wmix]
    if final_nw is not None:
        in_specs.append(pl.BlockSpec((1, d), const))
        args.append(final_nw)
    kern = functools.partial(_ffn_kernel, mod_rows=mod_rows, with_mix=mix is not None,
                             mix_gate_row=mix_gate_row, with_final=final_nw is not None)
    return pl.pallas_call(
        kern,
        grid=(t // tm,),
        in_specs=in_specs,
        out_specs=pl.BlockSpec((tm, d), row),
        out_shape=jax.ShapeDtypeStruct((t, d), F32),
        compiler_params=_cparams(1),
        name="ffn_mix" if mix is not None else "ffn",
    )(*args)


QPAD_W = NSA_Q_HEADS * LANES
KV6_OFF = QPAD_W
ML_OFF = KV6_OFF + 6 * KV_WIDTH
MISC_OFF = ML_OFF + 4 * MLSTM_WIDTH
PROJ_W = MISC_OFF + LANES
MISC_I = N_GATES
MISC_F = N_GATES + MLSTM_HEADS


def _vt_with_ones(v):
    vt = v.T
    n = vt.shape[1]
    pad = jnp.where(lax.broadcasted_iota(jnp.int32, (VT_ROWS - NSA_HEAD_DIM, n), 0) == 0, 1.0, 0.0)
    parts = []
    for h in range(NSA_KV_HEADS):
        parts += [vt[h * NSA_HEAD_DIM:(h + 1) * NSA_HEAD_DIM, :], pad]
    return jnp.concatenate(parts, axis=0).astype(BF16)


def _inproj_kernel(x_ref, mod_ref, nw_ref, w_ref, b_ref,
                   qpad_ref, kc_ref, vc_ref, ksel_ref, kwin_ref, vtsel_ref, vtwin_ref,
                   ml_ref, vtm_ref, ogt_ref, misc_ref, misct_ref):
    h = _norm_mod(x_ref[...], nw_ref[...], mod_ref[0, 3:4, :], mod_ref[0, 4:5, :])
    hb = h.astype(BF16)

    def proj(c0, width):
        return _dot(hb, w_ref[:, c0:c0 + width]) + b_ref[:, c0:c0 + width]

    scale = NSA_HEAD_DIM ** -0.5 * LOG2E
    for i in range(0, NSA_Q_HEADS, 2):
        pair = proj(i * LANES, 2 * LANES) * scale
        qpad_ref[0, i * LANES:(i + 1) * LANES, :] = pair[:, :LANES].T.astype(BF16)
        qpad_ref[0, (i + 1) * LANES:(i + 2) * LANES, :] = pair[:, LANES:].T.astype(BF16)
    kv_c = proj(KV6_OFF, 2 * KV_WIDTH)
    kc_ref[...] = kv_c[:, :KV_WIDTH].astype(BF16)
    vc_ref[...] = kv_c[:, KV_WIDTH:].astype(BF16)
    kv_s = proj(KV6_OFF + 2 * KV_WIDTH, 2 * KV_WIDTH)
    ksel_ref[...] = kv_s[:, :KV_WIDTH].astype(BF16)
    vtsel_ref[0] = _vt_with_ones(kv_s[:, KV_WIDTH:])
    kv_w = proj(KV6_OFF + 4 * KV_WIDTH, 2 * KV_WIDTH)
    kwin_ref[...] = kv_w[:, :KV_WIDTH].astype(BF16)
    vtwin_ref[0] = _vt_with_ones(kv_w[:, KV_WIDTH:])
    for i in range(2):
        ml_ref[:, i * MLSTM_WIDTH:(i + 1) * MLSTM_WIDTH] = proj(ML_OFF + i * MLSTM_WIDTH, MLSTM_WIDTH)
    vtm_ref[0] = proj(ML_OFF + 2 * MLSTM_WIDTH, MLSTM_WIDTH).T.astype(BF16)
    ogt_ref[0] = proj(ML_OFF + 3 * MLSTM_WIDTH, MLSTM_WIDTH).T
    misc = proj(MISC_OFF, LANES)
    misc_ref[...] = misc
    misct_ref[0] = misc.T


def _inproj(x2d, mod, nw, w_pack, b_pack, batch, seq):
    t, d = x2d.shape
    tm = TOKEN_TILE
    tpb = seq // tm
    row = lambda i: (i, 0)
    const = lambda i: (0, 0)
    trow = lambda i: (i // tpb, 0, i % tpb)
    vt_rows = NSA_KV_HEADS * VT_ROWS
    out_shapes = [
        jax.ShapeDtypeStruct((batch, QPAD_W, seq), BF16),
        jax.ShapeDtypeStruct((t, KV_WIDTH), BF16),
        jax.ShapeDtypeStruct((t, KV_WIDTH), BF16),
        jax.ShapeDtypeStruct((t, KV_WIDTH), BF16),
        jax.ShapeDtypeStruct((t, KV_WIDTH), BF16),
        jax.ShapeDtypeStruct((batch, vt_rows, seq), BF16),
        jax.ShapeDtypeStruct((batch, vt_rows, seq), BF16),
        jax.ShapeDtypeStruct((t, 2 * MLSTM_WIDTH), F32),
        jax.ShapeDtypeStruct((batch, MLSTM_WIDTH, seq), BF16),
        jax.ShapeDtypeStruct((batch, MLSTM_WIDTH, seq), F32),
        jax.ShapeDtypeStruct((t, LANES), F32),
        jax.ShapeDtypeStruct((batch, LANES, seq), F32),
    ]
    out_specs = [
        pl.BlockSpec((1, QPAD_W, tm), trow),
        pl.BlockSpec((tm, KV_WIDTH), row),
        pl.BlockSpec((tm, KV_WIDTH), row),
        pl.BlockSpec((tm, KV_WIDTH), row),
        pl.BlockSpec((tm, KV_WIDTH), row),
        pl.BlockSpec((1, vt_rows, tm), trow),
        pl.BlockSpec((1, vt_rows, tm), trow),
        pl.BlockSpec((tm, 2 * MLSTM_WIDTH), row),
        pl.BlockSpec((1, MLSTM_WIDTH, tm), trow),
        pl.BlockSpec((1, MLSTM_WIDTH, tm), trow),
        pl.BlockSpec((tm, LANES), row),
        pl.BlockSpec((1, LANES, tm), trow),
    ]
    return pl.pallas_call(
        _inproj_kernel,
        grid=(t // tm,),
        in_specs=[pl.BlockSpec((tm, d), row),
                  pl.BlockSpec((1, N_MOD, d), lambda i: (i // tpb, 0, 0)),
                  pl.BlockSpec((1, d), const),
                  pl.BlockSpec(w_pack.shape, const),
                  pl.BlockSpec(b_pack.shape, const)],
        out_specs=out_specs,
        out_shape=out_shapes,
        compiler_params=_cparams(1),
        name="inproj",
    )(x2d, mod, nw, w_pack, b_pack)


def _pack_inproj(w_in, b_in):
    d = w_in.shape[0]
    offs = np.cumsum([0, NSA_WIDTH] + [KV_WIDTH] * 6 + [N_GATES] + [MLSTM_WIDTH] * 4 + [MLSTM_HEADS] * 2)

    def pad_heads(a):
        a = a.reshape(a.shape[:-1] + (NSA_KV_HEADS, NSA_GROUP, NSA_HEAD_DIM))
        z = jnp.zeros_like(a[..., 0, :, :])
        lo = jnp.concatenate([a[..., 0, :, :], z], axis=-1)
        hi = jnp.concatenate([z, a[..., 1, :, :]], axis=-1)
        return jnp.stack([lo, hi], axis=-3).reshape(a.shape[:-3] + (QPAD_W,))

    wq_pad = pad_heads(w_in[:, :NSA_WIDTH])
    bq_pad = pad_heads(b_in[:NSA_WIDTH])
    g0, m0, i0 = int(offs[7]), int(offs[8]), int(offs[12])
    misc_w = jnp.concatenate([w_in[:, g0:g0 + N_GATES], w_in[:, i0:i0 + 2 * MLSTM_HEADS],
                              jnp.zeros((d, LANES - N_GATES - 2 * MLSTM_HEADS), w_in.dtype)], axis=1)
    misc_b = jnp.concatenate([b_in[g0:g0 + N_GATES], b_in[i0:i0 + 2 * MLSTM_HEADS],
                              jnp.zeros((LANES - N_GATES - 2 * MLSTM_HEADS,), b_in.dtype)])
    w_pack = jnp.concatenate([wq_pad, w_in[:, NSA_WIDTH:g0], w_in[:, m0:i0], misc_w], axis=1)
    b_pack = jnp.concatenate([bq_pad, b_in[NSA_WIDTH:g0], b_in[m0:i0], misc_b])
    assert w_pack.shape[1] == PROJ_W
    return w_pack.astype(BF16), b_pack.reshape(1, PROJ_W).astype(F32)


def _compress_kernel(xk_ref, xv_ref, wa_ref, wb_ref, pe_ref, w1_ref, w2_ref, kc_ref, vtc_ref):
    n_half = xk_ref.shape[1]

    def one(x_ref, s):
        x = x_ref[0]
        a = _dot(x, wa_ref[s])
        b = _dot(x, wb_ref[s])
        pe_term = jnp.dot(pe_ref[s], w1_ref[s], preferred_element_type=F32, precision=HIGHEST)[0:1, :]
        pe2 = jnp.concatenate([pe_term] * NSA_KV_HEADS, axis=1)
        pre = a + pltpu.roll(b, n_half - 1, 0) + pe2
        hid = 0.5 * pre * (1.0 + jnp.tanh(0.7978845608028654 * (pre + 0.044715 * pre * pre * pre)))
        return _dot(hid.astype(BF16), w2_ref[s])

    kc_ref[0] = one(xk_ref, 0).astype(BF16)
    vtc_ref[0] = _vt_with_ones(one(xv_ref, 1))


def _compress(xk, xv, wa, wb, pe8, w1, w2e):
    batch, n_half, width = xk.shape
    bsel = lambda b: (b, 0, 0)
    c3 = lambda b: (0, 0, 0)
    vt_rows = NSA_KV_HEADS * VT_ROWS
    return pl.pallas_call(
        _compress_kernel,
        grid=(batch,),
        in_specs=[pl.BlockSpec((1, n_half, width), bsel), pl.BlockSpec((1, n_half, width), bsel),
                  pl.BlockSpec(wa.shape, c3), pl.BlockSpec(wb.shape, c3),
                  pl.BlockSpec(pe8.shape, c3), pl.BlockSpec(w1.shape, c3), pl.BlockSpec(w2e.shape, c3)],
        out_specs=[pl.BlockSpec((1, n_half, KV_WIDTH), bsel), pl.BlockSpec((1, vt_rows, n_half), bsel)],
        out_shape=[jax.ShapeDtypeStruct((batch, n_half, KV_WIDTH), BF16),
                   jax.ShapeDtypeStruct((batch, vt_rows, n_half), BF16)],
        compiler_params=_cparams(1),
        name="compress",
    )(xk, xv, wa, wb, pe8, w1, w2e)


def _pack_compress(pe, w1, w2):
    half = CMP_BLOCK // 2
    w1r = w1.reshape(CMP_BLOCK, NSA_HEAD_DIM, CMP_HIDDEN)
    eye = jnp.eye(NSA_KV_HEADS, dtype=w1.dtype)

    def expand(w_half):
        z = w_half[:, None, :, None, :] * eye[None, :, None, :, None]
        return z.reshape(half * KV_WIDTH, NSA_KV_HEADS * CMP_HIDDEN)

    wa = expand(w1r[:half]).astype(BF16)
    wb = expand(w1r[half:]).astype(BF16)
    w2e = (w2[None, :, None, :] * eye[:, None, :, None]).reshape(NSA_KV_HEADS * CMP_HIDDEN, KV_WIDTH).astype(BF16)
    pe8 = jnp.concatenate([pe.reshape(1, CMP_BLOCK * NSA_HEAD_DIM),
                           jnp.zeros((7, CMP_BLOCK * NSA_HEAD_DIM), pe.dtype)], axis=0)
    return wa, wb, pe8, w1, w2e


def _q_aug(q_ref, qs_ref, h):
    qt = jnp.concatenate(
        [q_ref[0, (NSA_GROUP * h + g) * LANES:(NSA_GROUP * h + g + 1) * LANES, :] for g in range(NSA_GROUP)], axis=1)
    return jnp.concatenate([qt, qs_ref[h]], axis=0)


def _slope2(head):
    return float(2.0 ** (-8.0 * (head + 1) / NSA_Q_HEADS)) * LOG2E


def _bf16_split3(x):
    parts = []
    for _ in range(3):
        p = float(np.asarray(x, np.float32).astype(BF16).astype(np.float32))
        parts.append(p)
        x = x - p
    return parts


def _init_alibi_operands(qs_ref, kpos_ref, key_stride):
    lane = lax.broadcasted_iota(jnp.int32, (QT, LANES), 1)
    row = lax.broadcasted_iota(jnp.int32, (QT, LANES), 0)
    kpos_ref[...] = jnp.where(lane < 3, row.astype(F32), 0.0).astype(BF16)
    srow = lax.broadcasted_iota(jnp.int32, (LANES, QT), 0)
    for h in range(NSA_KV_HEADS):
        for g in range(NSA_GROUP):
            hi, mid, lo = _bf16_split3(key_stride * _slope2(NSA_GROUP * h + g))
            blk = jnp.where(srow == 0, hi, jnp.where(srow == 1, mid, jnp.where(srow == 2, lo, 0.0)))
            qs_ref[h, :, g * QT:(g + 1) * QT] = blk.astype(BF16)


def _gate_row(gt_ref, head, branch):
    r = head * 3 + branch
    return _sigmoid(gt_ref[0, r:r + 1, :])


def _softmax_jobs(jobs):
    scores = [_dot(jnp.concatenate(job[1], axis=0), job[0]) for job in jobs]
    staged = []
    for (q_aug, k_tiles, masks, offs, value_groups, state), st_all in zip(jobs, scores):
        n_t = len(k_tiles)
        per_g = []
        for g in range(NSA_GROUP):
            sl = slice(g * QT, (g + 1) * QT)
            ss = []
            for u in range(n_t):
                s = st_all[u * QT:(u + 1) * QT, sl]
                ss.append(s if masks[u] is None else jnp.where(masks[u], s, NEG))
            mx = None
            for u in range(n_t):
                cand = jnp.max(ss[u], axis=0, keepdims=True) + offs[u][g]
                mx = cand if mx is None else jnp.maximum(mx, cand)
            alpha = None
            if state is None:
                m_new = mx
            else:
                m_ref, _, h = state
                m_old = m_ref[h, :, sl]
                m_new = jnp.maximum(m_old, mx)
                alpha = jnp.exp2(m_old - m_new)
                m_ref[h, :, sl] = m_new
            m_use = jnp.where(m_new < 0.5 * NEG, 0.0, m_new)
            per_g.append((ss, m_use, alpha))
        staged.append(per_g)
    ets = []
    for (q_aug, k_tiles, masks, offs, value_groups, state), per_g in zip(jobs, staged):
        n_t = len(k_tiles)
        rows = [jnp.concatenate([jnp.exp2(per_g[g][0][u] - (per_g[g][1] - offs[u][g])).astype(BF16)
                                 for g in range(NSA_GROUP)], axis=1) for u in range(n_t)]
        ets.append(jnp.concatenate(rows, axis=0))
    results = []
    for (q_aug, k_tiles, masks, offs, value_groups, state), per_g, et in zip(jobs, staged, ets):
        mats = [jnp.concatenate(grp, axis=1) for grp in value_groups]
        pv_all = _dot(jnp.concatenate(mats, axis=0), et)
        pvs, r0 = [], 0
        for mat in mats:
            pvs.append(pv_all[r0:r0 + mat.shape[0], :])
            r0 += mat.shape[0]
        if state is None:
            results.append(pvs)
        else:
            _, acc_ref, h = state
            acc_ref[h] = jnp.concatenate([pg[2] for pg in per_g], axis=1) * acc_ref[h] + pvs[0]
            results.append(None)
    return results


def _gated(gt_ref, h, branch, pv):
    inv = 1.0 / jnp.maximum(pv[NSA_HEAD_DIM:NSA_HEAD_DIM + 1, :], 1e-30)
    out = []
    for g in range(NSA_GROUP):
        sl = slice(g * QT, (g + 1) * QT)
        out.append(pv[0:NSA_HEAD_DIM, sl] * (inv[:, sl] * _gate_row(gt_ref, NSA_GROUP * h + g, branch)))
    return out


def _nsa_cmp_kernel(q_ref, kc_ref, vtc_ref, gt_ref, oc_ref, sel_ref, cnt_ref,
                    qs_ref, kpos_ref, ovt_ref, pvo_ref, pvi_ref):
    ncp = kc_ref.shape[1]
    n_ct = ncp // QT
    n_sel = sel_ref.shape[3]
    start = pl.program_id(1) * QT
    tile_span = QT * CMP_STRIDE

    @pl.when((pl.program_id(0) == 0) & (pl.program_id(1) == 0))
    def _():
        _init_alibi_operands(qs_ref, kpos_ref, float(CMP_STRIDE))
        j_o = lax.broadcasted_iota(jnp.int32, (n_sel, ncp), 0)
        c_o = lax.broadcasted_iota(jnp.int32, (n_sel, ncp), 1)
        ov = ((c_o * CMP_STRIDE <= j_o * SEL_BLOCK + (SEL_BLOCK - 1))
              & (c_o * CMP_STRIDE + (CMP_BLOCK - 1) >= j_o * SEL_BLOCK))
        ovt_ref[...] = jnp.where(ov, 1.0, 0.0).astype(BF16)

    a0 = (lax.broadcasted_iota(jnp.int32, (QT, QT), 1)
          - CMP_STRIDE * lax.broadcasted_iota(jnp.int32, (QT, QT), 0))
    q_augs = [_q_aug(q_ref, qs_ref, h) for h in range(NSA_KV_HEADS)]
    n_vis = jnp.minimum((start + QT - CMP_BLOCK) // tile_span + 1, n_ct)

    for k in range(1, n_ct + 1):
        @pl.when(n_vis == k)
        def _(k=k):
            jobs = []
            for h in range(NSA_KV_HEADS):
                k_tiles, masks, offs, vts, ovs = [], [], [], [], []
                for ci in range(k):
                    cs = slice(ci * QT, (ci + 1) * QT)
                    first_end = ci * tile_span + CMP_BLOCK - 1
                    k_tiles.append(jnp.concatenate([kc_ref[0, cs, :], kpos_ref[...]], axis=1))
                    masks.append(a0 + (start - first_end) >= 0)
                    offs.append([_slope2(NSA_GROUP * h + g) * first_end for g in range(NSA_GROUP)])
                    vts.append(vtc_ref[0, h * VT_ROWS:(h + 1) * VT_ROWS, cs])
                    ovs.append(ovt_ref[:, cs])
                jobs.append((q_augs[h], k_tiles, masks, offs, [vts, ovs], None))
            for h, (pvo, pvi) in enumerate(_softmax_jobs(jobs)):
                pvo_ref[h] = pvo
                pvi_ref[h] = pvi

    j_io = lax.broadcasted_iota(jnp.int32, (n_sel, QT), 0)
    t_j = start + lax.broadcasted_iota(jnp.int32, (n_sel, QT), 1)
    cur = t_j // SEL_BLOCK
    valid = j_io * SEL_BLOCK <= t_j
    forced = (j_io == 0) | (j_io == cur) | (j_io == cur - 1)
    ones8 = jnp.ones((8, QT), BF16)

    taken = -(2.0 ** 126)
    scores = []
    for h in range(NSA_KV_HEADS):
        pvo = pvo_ref[h]
        o_c = _gated(gt_ref, h, 0, pvo)
        inv = 1.0 / jnp.maximum(pvo[NSA_HEAD_DIM:NSA_HEAD_DIM + 1, :], 1e-30)
        imp = None
        for g in range(NSA_GROUP):
            sl = slice(g * QT, (g + 1) * QT)
            oc_ref[0, NSA_GROUP * h + g] = o_c[g]
            term = pvi_ref[h, :, sl] * inv[:, sl]
            imp = term if imp is None else imp + term
        scores.append(jnp.where(forced, taken, jnp.where(valid, imp, -FORCE_BONUS)))

    n_cls = n_sel // TOPK_ROWS
    cls = jnp.minimum((start + QT + SEL_BLOCK * TOPK_ROWS - 1) // (SEL_BLOCK * TOPK_ROWS), n_cls)

    for k in range(1, n_cls + 1):
        @pl.when(cls == k)
        def _(rows=k * TOPK_ROWS):
            j_sub = j_io[:rows]

            def pick(_, carry):
                out = []
                for sc in carry:
                    mx = jnp.max(sc, axis=0, keepdims=True)
                    idx = jnp.min(jnp.where(sc == mx, j_sub, n_sel), axis=0, keepdims=True)
                    out.append(jnp.where(j_sub == idx, taken, sc))
                return tuple(out)

            picked = lax.fori_loop(0, SEL_TOPK - N_FORCED, pick, tuple(sc[:rows] for sc in scores))
            for h in range(NSA_KV_HEADS):
                sel = jnp.where(picked[h] <= 0.5 * taken, 1.0, 0.0)
                if rows < n_sel:
                    sel = jnp.concatenate([sel, jnp.zeros((n_sel - rows, QT), F32)], axis=0)
                sel_ref[0, h, 0] = sel
                cnt = _dot_nt(ones8, sel.astype(BF16))
                cnt_ref[0, 0, h] = cnt[0:1, :]


def _nsa_cmp(qpad3, kc, vtc, gates_t, batch, seq):
    nb = seq // QT
    n_sel = seq // SEL_BLOCK
    ncp = kc.shape[1]
    assert ncp % QT == 0
    vt_rows = NSA_KV_HEADS * VT_ROWS
    return pl.pallas_call(
        _nsa_cmp_kernel,
        grid=(batch, nb),
        in_specs=[pl.BlockSpec((1, QPAD_W, QT), lambda b, i: (b, 0, i)),
                  pl.BlockSpec((1, ncp, KV_WIDTH), lambda b, i: (b, 0, 0)),
                  pl.BlockSpec((1, vt_rows, ncp), lambda b, i: (b, 0, 0)),
                  pl.BlockSpec((1, N_GATES, QT), lambda b, i: (b, 0, i))],
        out_specs=[pl.BlockSpec((1, NSA_Q_HEADS, NSA_HEAD_DIM, QT), lambda b, i: (b, 0, 0, i)),
                   pl.BlockSpec((1, NSA_KV_HEADS, 1, n_sel, QT), lambda b, i: (b, 0, i, 0, 0)),
                   pl.BlockSpec((1, 1, NSA_KV_HEADS, 1, n_sel), lambda b, i: (b, i, 0, 0, 0))],
        out_shape=[jax.ShapeDtypeStruct((batch, NSA_Q_HEADS, NSA_HEAD_DIM, seq), F32),
                   jax.ShapeDtypeStruct((batch, NSA_KV_HEADS, nb, n_sel, QT), F32),
                   jax.ShapeDtypeStruct((batch, nb, NSA_KV_HEADS, 1, n_sel), F32)],
        scratch_shapes=[pltpu.VMEM((NSA_KV_HEADS, LANES, NSA_GROUP * QT), BF16),
                        pltpu.VMEM((QT, LANES), BF16),
                        pltpu.VMEM((n_sel, ncp), BF16),
                        pltpu.VMEM((NSA_KV_HEADS, VT_ROWS, NSA_GROUP * QT), F32),
                        pltpu.VMEM((NSA_KV_HEADS, n_sel, NSA_GROUP * QT), F32)],
        compiler_params=_cparams(2),
        name="nsa_cmp",
    )(qpad3, kc, vtc, gates_t)


def _nsa_sel_kernel(count_ref, list_ref, q_ref, ksel_ref, vtsel_ref, kwin_ref, vtwin_ref, selm_ref, gt_ref,
                    oc_ref, o_ref, m_sel, acc_sel, qs_ref, kpos_ref, part_ref):
    b = pl.program_id(0)
    qb = pl.program_id(1)
    nb = pl.num_programs(1)
    heads = range(NSA_KV_HEADS)
    k_io = lax.broadcasted_iota(jnp.int32, (QT, QT), 0)
    q_io = lax.broadcasted_iota(jnp.int32, (QT, QT), 1)
    causal = q_io >= k_io
    in_window = q_io < k_io
    first_half = k_io < SEL_BLOCK

    @pl.when((b == 0) & (qb == 0))
    def _():
        _init_alibi_operands(qs_ref, kpos_ref, 1.0)

    def k_aug(k_ref, t):
        off = pl.multiple_of(t * QT, QT)
        return jnp.concatenate([k_ref[0, pl.ds(off, QT), :], kpos_ref[...]], axis=1)

    def v_ext(vt_ref, h, t):
        return vt_ref[0, h * VT_ROWS:(h + 1) * VT_ROWS, pl.ds(pl.multiple_of(t * QT, QT), QT)]

    def tile_offs(h, t, live=None):
        pos = (t * QT).astype(F32)
        out = [_slope2(NSA_GROUP * h + g) * pos for g in range(NSA_GROUP)]
        return out if live is None else [jnp.where(live, o, NEG) for o in out]

    def chosen(h, t):
        mrow = selm_ref[0, h, 0, pl.ds(2 * t, 2), :]
        return jnp.where(first_half, mrow[0:1, :], mrow[1:2, :]) > 0.5

    q_augs = [_q_aug(q_ref, qs_ref, h) for h in heads]

    slot = [(b * nb + qb) * NSA_KV_HEADS + h for h in heads]
    n_tiles = [count_ref[slot[h]] for h in heads]

    m_sel[...] = jnp.full(m_sel.shape, NEG, F32)
    acc_sel[...] = jnp.zeros(acc_sel.shape, F32)
    n_win = WINDOW // QT + 1
    jobs = []
    for h in heads:
        jobs.append((q_augs[h], [k_aug(ksel_ref, qb)], [chosen(h, qb) & causal], [tile_offs(h, qb)],
                     [[v_ext(vtsel_ref, h, qb)]], (m_sel, acc_sel, h)))
    for h in heads:
        k_tiles, masks, offs, vts = [], [], [], []
        for u in range(n_win):
            raw = qb - (n_win - 1) + u
            t = jnp.maximum(raw, 0)
            k_tiles.append(k_aug(kwin_ref, t))
            masks.append(in_window if u == 0 else causal if u == n_win - 1 else None)
            offs.append(tile_offs(h, t, raw >= 0))
            vts.append(v_ext(vtwin_ref, h, t))
        jobs.append((q_augs[h], k_tiles, masks, offs, [vts], None))
    results = _softmax_jobs(jobs)
    for h in heads:
        o_w = _gated(gt_ref, h, 2, results[NSA_KV_HEADS + h][0])
        for g in range(NSA_GROUP):
            head = NSA_GROUP * h + g
            part_ref[head] = oc_ref[0, head] + o_w[g]

    def sel_body(i, carry):
        jobs = []
        for step in range(SEL_CHAIN):
            for h in heads:
                k_tiles, masks, offs, vts = [], [], [], []
                for u in range(SEL_UNROLL):
                    idx = (i * SEL_CHAIN + step) * SEL_UNROLL + u
                    live = idx < n_tiles[h]
                    t = jnp.where(live, list_ref[slot[h] * nb + jnp.minimum(idx, nb - 1)], 0)
                    k_tiles.append(k_aug(ksel_ref, t))
                    masks.append(chosen(h, t))
                    offs.append(tile_offs(h, t, live))
                    vts.append(v_ext(vtsel_ref, h, t))
                jobs.append((q_augs[h], k_tiles, masks, offs, [vts], (m_sel, acc_sel, h)))
        _softmax_jobs(jobs)
        return carry

    per_trip = SEL_UNROLL * SEL_CHAIN
    n_trips = (functools.reduce(jnp.maximum, n_tiles) + per_trip - 1) // per_trip
    lax.fori_loop(0, n_trips, sel_body, 0)

    for h in heads:
        o_s = _gated(gt_ref, h, 1, acc_sel[h])
        for g in range(NSA_GROUP):
            head = NSA_GROUP * h + g
            o_ref[0, head] = part_ref[head] + o_s[g]


def _tile_lists(cnt, batch, nb):
    pair = cnt.reshape(batch, nb, NSA_KV_HEADS, nb, 2).sum(axis=-1)
    p_io = lax.broadcasted_iota(jnp.int32, pair.shape, 3)
    flag = (pair > 0.5) & (p_io < lax.broadcasted_iota(jnp.int32, pair.shape, 1))
    rank = jnp.cumsum(flag.astype(jnp.int32), axis=-1) - 1
    slot_io = lax.broadcasted_iota(jnp.int32, pair.shape + (nb,), 4)
    hit = flag[..., None] & (rank[..., None] == slot_io)
    lists = jnp.sum(jnp.where(hit, p_io[..., None], 0), axis=3)
    counts = jnp.sum(flag.astype(jnp.int32), axis=-1)
    return counts.reshape(-1), lists.reshape(-1)


def _nsa_sel(counts, lists, qpad3, ksel, vtsel, kwin, vtwin, selm, gates_t, oc, batch, seq):
    nb = seq // QT
    res = lambda b, i, c, l: (b, 0, 0)
    vt_rows = NSA_KV_HEADS * VT_ROWS
    grid_spec = pltpu.PrefetchScalarGridSpec(
        num_scalar_prefetch=2,
        grid=(batch, nb),
        in_specs=[pl.BlockSpec((1, QPAD_W, QT), lambda b, i, c, l: (b, 0, i)),
                  pl.BlockSpec((1, seq, KV_WIDTH), res),
                  pl.BlockSpec((1, vt_rows, seq), res),
                  pl.BlockSpec((1, seq, KV_WIDTH), res),
                  pl.BlockSpec((1, vt_rows, seq), res),
                  pl.BlockSpec((1, NSA_KV_HEADS, 1, 2 * nb, QT), lambda b, i, c, l: (b, 0, i, 0, 0)),
                  pl.BlockSpec((1, N_GATES, QT), lambda b, i, c, l: (b, 0, i)),
                  pl.BlockSpec((1, NSA_Q_HEADS, NSA_HEAD_DIM, QT), lambda b, i, c, l: (b, 0, 0, i))],
        out_specs=pl.BlockSpec((1, NSA_Q_HEADS, NSA_HEAD_DIM, QT), lambda b, i, c, l: (b, 0, 0, i)),
        scratch_shapes=[pltpu.VMEM((NSA_KV_HEADS, 1, NSA_GROUP * QT), F32),
                        pltpu.VMEM((NSA_KV_HEADS, VT_ROWS, NSA_GROUP * QT), F32),
                        pltpu.VMEM((NSA_KV_HEADS, LANES, NSA_GROUP * QT), BF16),
                        pltpu.VMEM((QT, LANES), BF16),
                        pltpu.VMEM((NSA_Q_HEADS, NSA_HEAD_DIM, QT), F32)],
    )
    return pl.pallas_call(
        _nsa_sel_kernel,
        grid_spec=grid_spec,
        out_shape=jax.ShapeDtypeStruct((batch, NSA_Q_HEADS, NSA_HEAD_DIM, seq), F32),
        compiler_params=_cparams(2),
        name="nsa_sel",
    )(counts, lists, qpad3, ksel, vtsel, kwin, vtwin, selm, gates_t, oc)


def _mlstm_kernel(q_ref, k_ref, vt_ref, ogt_ref, misc_ref, misct_ref, cw_ref, cb_ref, nw_ref,
                  o_ref, xq_ref, xk_ref, c_ref, n_ref, m_ref):
    tm = q_ref.shape[0]
    L = MLSTM_L
    hd = MLSTM_HEAD_DIM
    halo = 8

    @pl.when(pl.program_id(1) == 0)
    def _():
        xq_ref[0:halo, :] = jnp.zeros((halo, MLSTM_WIDTH), F32)
        xk_ref[0:halo, :] = jnp.zeros((halo, MLSTM_WIDTH), F32)
        c_ref[...] = jnp.zeros_like(c_ref)
        n_ref[...] = jnp.zeros_like(n_ref)
        m_ref[...] = jnp.zeros_like(m_ref)

    def conv(x_ref, buf_ref, col0):
        buf_ref[halo:halo + tm, :] = x_ref[...]
        y = cb_ref[:, col0:col0 + MLSTM_WIDTH]
        for kk in range(CONV_WIDTH):
            r0 = halo - (CONV_WIDTH - 1) + kk
            y = y + cw_ref[kk:kk + 1, col0:col0 + MLSTM_WIDTH] * buf_ref[r0:r0 + tm, :]
        tail = buf_ref[tm:tm + halo, :]
        buf_ref[0:halo, :] = tail
        return _silu(y)

    qc = conv(q_ref, xq_ref, 0)
    kc = conv(k_ref, xk_ref, MLSTM_WIDTH) * (hd ** -0.5)

    misc = misc_ref[...]
    misct = misct_ref[0]
    lf_c = -jnp.log(1.0 + jnp.exp(-misc))
    lf_r = -jnp.log(1.0 + jnp.exp(-misct))
    row_io = lax.broadcasted_iota(jnp.int32, (L, L), 0)
    col_io = lax.broadcasted_iota(jnp.int32, (L, L), 1)
    tril = jnp.where(col_io <= row_io, 1.0, 0.0).astype(BF16)
    triu = jnp.where(row_io <= col_io, 1.0, 0.0).astype(BF16)

    def split3(x):
        hi = x.astype(BF16)
        r1 = x - hi.astype(F32)
        mid = r1.astype(BF16)
        return hi, mid, (r1 - mid.astype(F32)).astype(BF16)

    lf_c3 = split3(lf_c)
    lf_r3 = split3(lf_r)
    b_c = jnp.concatenate([sum(_dot(tril, p[ci * L:(ci + 1) * L, :]) for p in lf_c3)
                           for ci in range(tm // L)], axis=0)
    b_r = jnp.concatenate([sum(_dot(p[:, ci * L:(ci + 1) * L], triu) for p in lf_r3)
                           for ci in range(tm // L)], axis=1)

    tri_t = lax.broadcasted_iota(jnp.int32, (L, L), 0) <= lax.broadcasted_iota(jnp.int32, (L, L), 1)
    nw_cols = [jnp.broadcast_to(nw_ref[:, h * hd:(h + 1) * hd], (hd, hd)).T for h in range(MLSTM_HEADS)]

    def split3_rows(row):
        hi = row.astype(BF16)
        r1 = row - hi.astype(F32)
        mid = r1.astype(BF16)
        lo = (r1 - mid.astype(F32)).astype(BF16)
        sub = lax.broadcasted_iota(jnp.int32, (8, row.shape[1]), 0)
        out = jnp.where(sub == 0, hi.astype(F32), jnp.where(sub == 1, mid.astype(F32),
                                                            jnp.where(sub == 2, lo.astype(F32), 0.0)))
        return out.astype(BF16)

    def rows_sum3(x8):
        return x8[0:1, :] + x8[1:2, :] + x8[2:3, :]

    for ci in range(tm // L):
        r0 = ci * L
        for h in range(MLSTM_HEADS):
            cs = slice(h * hd, (h + 1) * hd)
            qb16 = qc[r0:r0 + L, cs].astype(BF16)
            kb16 = kc[r0:r0 + L, cs].astype(BF16)
            vt = vt_ref[0, cs, r0:r0 + L]
            a_col = (misc[r0:r0 + L, MISC_I + h:MISC_I + h + 1]
                     - b_c[r0:r0 + L, MISC_F + h:MISC_F + h + 1])
            brow = b_r[MISC_F + h:MISC_F + h + 1, r0:r0 + L]
            lirow = misct[MISC_I + h:MISC_I + h + 1, r0:r0 + L]
            m_prev = m_ref[h]
            ct = c_ref[h]
            nrow = n_ref[h]

            dmat = jnp.where(tri_t, brow + a_col, NEG)
            m_inter = brow + m_prev
            m_t = jnp.maximum(jnp.max(dmat, axis=0, keepdims=True), m_inter)
            w = jnp.exp(dmat - m_t) * _dot_nt(kb16, qb16)
            decay = jnp.exp(m_inter - m_t)
            num = _dot(vt, w.astype(BF16)) + decay * _dot_nt(ct.astype(BF16), qb16)
            nq = rows_sum3(_dot_nt(split3_rows(nrow), qb16))
            den = jnp.sum(w, axis=0, keepdims=True) + decay * nq
            hout = num * (1.0 / jnp.maximum(jnp.abs(den), jnp.exp(-m_t)))

            b_last = brow[:, L - 1:L]
            grow = b_last - brow + lirow
            m_new = jnp.maximum(b_last + m_prev, jnp.max(grow, axis=1, keepdims=True))
            wk = jnp.exp(grow - m_new)
            d_c = jnp.exp(b_last + m_prev - m_new)
            c_ref[h] = d_c * ct + _dot((vt.astype(F32) * wk).astype(BF16), kb16)
            n_ref[h] = d_c * nrow + rows_sum3(_dot(split3_rows(wk), kb16))
            m_ref[h] = m_new

            hn = hout * lax.rsqrt(jnp.mean(hout * hout, axis=0, keepdims=True) + EPS) * nw_cols[h]
            o_ref[0, cs, r0:r0 + L] = (hn * _sigmoid(ogt_ref[0, cs, r0:r0 + L])).astype(o_ref.dtype)


def _mlstm(ml, vt_m, og_t, misc, misc_t, conv_w, conv_b, norm_w, batch, seq):
    tm = TOKEN_TILE
    tpb = seq // tm
    col = lambda j: (lambda b, i: (b * tpb + i, j))
    const = lambda b, i: (0, 0)
    trow = lambda b, i: (b, 0, i)
    return pl.pallas_call(
        _mlstm_kernel,
        grid=(batch, tpb),
        in_specs=[pl.BlockSpec((tm, MLSTM_WIDTH), col(0)),
                  pl.BlockSpec((tm, MLSTM_WIDTH), col(1)),
                  pl.BlockSpec((1, MLSTM_WIDTH, tm), trow),
                  pl.BlockSpec((1, MLSTM_WIDTH, tm), trow),
                  pl.BlockSpec((tm, LANES), col(0)),
                  pl.BlockSpec((1, LANES, tm), trow),
                  pl.BlockSpec(conv_w.shape, const),
                  pl.BlockSpec(conv_b.shape, const),
                  pl.BlockSpec(norm_w.shape, const)],
        out_specs=pl.BlockSpec((1, MLSTM_WIDTH, tm), trow),
        out_shape=jax.ShapeDtypeStruct((batch, MLSTM_WIDTH, seq), BF16),
        scratch_shapes=[pltpu.VMEM((tm + 8, MLSTM_WIDTH), F32),
                        pltpu.VMEM((tm + 8, MLSTM_WIDTH), F32),
                        pltpu.VMEM((MLSTM_HEADS, MLSTM_HEAD_DIM, MLSTM_HEAD_DIM), F32),
                        pltpu.VMEM((MLSTM_HEADS, 1, MLSTM_HEAD_DIM), F32),
                        pltpu.VMEM((MLSTM_HEADS, 1, 1), F32)],
        compiler_params=_cparams(2),
        name="mlstm",
    )(ml, ml, vt_m, og_t, misc, misc_t, conv_w, conv_b, norm_w)


def _layer(x2d, c, batch, seq, w_ada, b_ada, norm_ffn1_w, ffn1_w_in, ffn1_w_out, norm_mix_w, w_in, b_in,
           cmp_k_pe, cmp_k_w1, cmp_k_w2, cmp_v_pe, cmp_v_w1, cmp_v_w2, conv_w, conv_b, mlstm_norm_w, w_out,
           norm_ffn2_w, ffn2_w_in, ffn2_w_out, final_nw):
    d = x2d.shape[1]
    tpb = seq // TOKEN_TILE
    c8 = jnp.concatenate([c, jnp.zeros((8 - batch % 8 if batch % 8 else 0, d), c.dtype)], axis=0)
    mod = _adaln(c8, w_ada, b_ada.reshape(1, -1))[:batch].reshape(batch, N_MOD, d)

    x1 = _ffn(x2d, mod, norm_ffn1_w.reshape(1, d), ffn1_w_in.astype(BF16), ffn1_w_out.astype(BF16),
              (0, 1, 2), tpb)

    w_pack, b_pack = _pack_inproj(w_in, b_in)
    (qpad, kc, vc, ksel, kwin, vtsel, vtwin, ml, vt_m, og_t, misc, misc_t) = _inproj(
        x1, mod, norm_mix_w.reshape(1, d), w_pack, b_pack, batch, seq)

    n_half = seq // CMP_STRIDE
    pk = _pack_compress(cmp_k_pe, cmp_k_w1, cmp_k_w2)
    pv = _pack_compress(cmp_v_pe, cmp_v_w1, cmp_v_w2)
    stacked = [jnp.stack([a, bb]) for a, bb in zip(pk, pv)]
    kcmp, vtcmp = _compress(kc.reshape(batch, n_half, CMP_STRIDE * KV_WIDTH),
                            vc.reshape(batch, n_half, CMP_STRIDE * KV_WIDTH), *stacked)

    nb = seq // QT
    qpad3 = qpad
    gates_t = misc_t[:, :N_GATES, :]
    oc, sel, cnt = _nsa_cmp(qpad3, kcmp, vtcmp, gates_t, batch, seq)
    counts, lists = _tile_lists(cnt, batch, nb)
    o_t = _nsa_sel(counts, lists, qpad3, ksel.reshape(batch, seq, KV_WIDTH), vtsel, kwin.reshape(batch, seq, KV_WIDTH),
                   vtwin, sel, gates_t, oc, batch, seq)
    o_nsa = o_t.reshape(batch, NSA_WIDTH, seq)

    o_ml = _mlstm(ml, vt_m, og_t, misc, misc_t, conv_w, conv_b.reshape(1, -1), mlstm_norm_w.reshape(1, -1), batch, seq)

    return _ffn(x1, mod, norm_ffn2_w.reshape(1, d), ffn2_w_in.astype(BF16), ffn2_w_out.astype(BF16),
                (6, 7, 8), tpb, mix=(o_nsa, o_ml, w_out.astype(BF16)), mix_gate_row=5, final_nw=final_nw)


def kernel(x, c, w_ada, b_ada, norm_ffn1_w, ffn1_w_in, ffn1_w_out, norm_mix_w, w_in, b_in, cmp_k_pe, cmp_k_w1, cmp_k_w2, cmp_v_pe, cmp_v_w1, cmp_v_w2, conv_w, conv_b, mlstm_norm_w, w_out, norm_ffn2_w, ffn2_w_in, ffn2_w_out, final_norm_w):
    batch, seq, d = x.shape
    depth = w_ada.shape[0]
    assert depth == 1 and seq % (QT * CMP_STRIDE) == 0 and seq // SEL_BLOCK >= SEL_TOPK
    y = _layer(x.reshape(batch * seq, d), c, batch, seq, w_ada[0], b_ada[0], norm_ffn1_w[0], ffn1_w_in[0],
               ffn1_w_out[0], norm_mix_w[0], w_in[0], b_in[0], cmp_k_pe[0], cmp_k_w1[0], cmp_k_w2[0],
               cmp_v_pe[0], cmp_v_w1[0], cmp_v_w2[0], conv_w[0], conv_b[0], mlstm_norm_w[0], w_out[0],
               norm_ffn2_w[0], ffn2_w_in[0], ffn2_w_out[0], final_norm_w.reshape(1, d))
    return y.reshape(batch, seq, d)
```

```python
import functools

import numpy as np
import jax
import jax.numpy as jnp
from jax import lax
from jax.experimental import pallas as pl
from jax.experimental.pallas import tpu as pltpu

NSA_Q_HEADS = 8
NSA_KV_HEADS = 2
NSA_GROUP = NSA_Q_HEADS // NSA_KV_HEADS
NSA_HEAD_DIM = 64
CMP_BLOCK = 32
CMP_STRIDE = 16
CMP_HIDDEN = 128
SEL_BLOCK = 64
SEL_TOPK = 16
WINDOW = 512
FORCE_BONUS = 1.0e4
N_FORCED = 3
assert FORCE_BONUS > NSA_GROUP
MLSTM_HEADS = 4
MLSTM_HEAD_DIM = 128
CONV_WIDTH = 4
D_FF = 2816
N_MOD = 9
EPS = 1e-6
NEG = -1e30

NSA_WIDTH = NSA_Q_HEADS * NSA_HEAD_DIM
KV_WIDTH = NSA_KV_HEADS * NSA_HEAD_DIM
MLSTM_WIDTH = MLSTM_HEADS * MLSTM_HEAD_DIM
N_GATES = 3 * NSA_Q_HEADS

LANES = 128
QT = 128
TOKEN_TILE = 512
FF_CHUNK = 256
MLSTM_L = 128
SEL_UNROLL = 6
SEL_CHAIN = 2
TOPK_ROWS = 32
VT_ROWS = 80
VMEM_LIMIT = 56 * 1024 * 1024

F32 = jnp.float32
BF16 = jnp.bfloat16
HIGHEST = lax.Precision.HIGHEST
LOG2E = 1.4426950408889634


def _cparams(n_axes):
    return pltpu.CompilerParams(dimension_semantics=("arbitrary",) * n_axes,
                                vmem_limit_bytes=VMEM_LIMIT)


def _dot(a, b):
    return jnp.dot(a, b, preferred_element_type=F32)


def _dot_nt(a, b):
    return lax.dot_general(a, b, (((1,), (1,)), ((), ())), preferred_element_type=F32)


def _dot_tn(a, b):
    return lax.dot_general(a, b, (((0,), (0,)), ((), ())), preferred_element_type=F32)


def _sigmoid(x):
    return 1.0 / (1.0 + jnp.exp(-x))


def _silu(x):
    return x * _sigmoid(x)


def _norm_mod(x, nw, sh, sc):
    ms = jnp.mean(x * x, axis=-1, keepdims=True)
    y = x * lax.rsqrt(ms + EPS) * nw
    return y * (1.0 + sc) + sh


def _adaln_kernel(ct_ref, w_ref, b_ref, o_ref):
    w = w_ref[...]
    for r in range(ct_ref.shape[1]):
        a_col = _silu(ct_ref[:, r:r + 1])
        o_ref[r:r + 1, :] = jnp.sum(a_col * w, axis=0, keepdims=True) + b_ref[...]


def _adaln(c_t, w_ada, b_ada):
    d, rows = c_t.shape
    n = w_ada.shape[1]
    tn = n // N_MOD
    return pl.pallas_call(
        _adaln_kernel,
        grid=(N_MOD,),
        in_specs=[pl.BlockSpec((d, rows), lambda j: (0, 0)),
                  pl.BlockSpec((d, tn), lambda j: (0, j)),
                  pl.BlockSpec((1, tn), lambda j: (0, j))],
        out_specs=pl.BlockSpec((rows, tn), lambda j: (0, j)),
        out_shape=jax.ShapeDtypeStruct((rows, n), F32),
        compiler_params=_cparams(1),
        name="adaln",
    )(c_t, w_ada, b_ada)


def _ffn_kernel(*refs, mod_rows, with_mix, mix_gate_row, with_final):
    it = iter(refs)
    x_ref = next(it)
    mod_ref = next(it)
    nw_ref = next(it)
    win_ref = next(it)
    wo_ref = next(it)
    if with_mix:
        ma_ref = next(it)
        mb_ref = next(it)
        wmix_ref = next(it)
    if with_final:
        fnw_ref = next(it)
    o_ref = next(it)

    sh_row, sc_row, g_row = mod_rows
    x = x_ref[...]
    if with_mix:
        mix_t = jnp.concatenate([ma_ref[0].astype(BF16), mb_ref[0].astype(BF16)], axis=0)
        x = x + mod_ref[0, mix_gate_row:mix_gate_row + 1, :] * _dot_tn(mix_t, wmix_ref[...])
    h = _norm_mod(x, nw_ref[...], mod_ref[0, sh_row:sh_row + 1, :], mod_ref[0, sc_row:sc_row + 1, :])
    hb = h.astype(BF16)
    d_ff = wo_ref.shape[0]
    n_chunks = d_ff // FF_CHUNK
    acc = None
    for j in range(n_chunks):
        c0 = j * FF_CHUNK
        g = _dot(hb, win_ref[:, c0:c0 + FF_CHUNK])
        u = _dot(hb, win_ref[:, d_ff + c0:d_ff + c0 + FF_CHUNK])
        act = (_silu(g) * u).astype(BF16)
        part = _dot(act, wo_ref[c0:c0 + FF_CHUNK, :])
        acc = part if acc is None else acc + part
    y = x + 0.5 * mod_ref[0, g_row:g_row + 1, :] * acc
    if with_final:
        ms = jnp.mean(y * y, axis=-1, keepdims=True)
        y = y * lax.rsqrt(ms + EPS) * fnw_ref[...]
    o_ref[...] = y


def _ffn(x2d, mod, nw, w_in_bf, w_out_bf, mod_rows, tiles_per_batch, mix=None, mix_gate_row=None, final_nw=None):
    t, d = x2d.shape
    tm = TOKEN_TILE
    d_ff = w_out_bf.shape[0]
    assert d_ff % FF_CHUNK == 0
    row = lambda i: (i, 0)
    const = lambda i: (0, 0)
    in_specs = [pl.BlockSpec((tm, d), row),
                pl.BlockSpec((1, N_MOD, d), lambda i: (i // tiles_per_batch, 0, 0)),
                pl.BlockSpec((1, d), const),
                pl.BlockSpec(w_in_bf.shape, const),
                pl.BlockSpec(w_out_bf.shape, const)]
    args = [x2d, mod, nw, w_in_bf, w_out_bf]
    if mix is not None:
        ma, mb, wmix = mix
        trow = lambda i: (i // tiles_per_batch, 0, i % tiles_per_batch)
        in_specs += [pl.BlockSpec((1, ma.shape[1], tm), trow),
                     pl.BlockSpec((1, mb.shape[1], tm), trow),
                     pl.BlockSpec(wmix.shape, const)]
        args += [ma, mb, wmix]
    if final_nw is not None:
        in_specs.append(pl.BlockSpec((1, d), const))
        args.append(final_nw)
    kern = functools.partial(_ffn_kernel, mod_rows=mod_rows, with_mix=mix is not None,
                             mix_gate_row=mix_gate_row, with_final=final_nw is not None)
    return pl.pallas_call(
        kern,
        grid=(t // tm,),
        in_specs=in_specs,
        out_specs=pl.BlockSpec((tm, d), row),
        out_shape=jax.ShapeDtypeStruct((t, d), F32),
        compiler_params=_cparams(1),
        name="ffn_mix" if mix is not None else "ffn",
    )(*args)


QPAD_W = NSA_Q_HEADS * LANES
MISC_I = N_GATES
MISC_F = N_GATES + MLSTM_HEADS
HALF_BLOCK_W = CMP_STRIDE * KV_WIDTH
B_KV = NSA_WIDTH
B_ML = B_KV + 6 * KV_WIDTH
B_MISC = B_ML + 4 * MLSTM_WIDTH


def _vt_with_ones(v):
    vt = v.T
    n = vt.shape[1]
    pad = jnp.where(lax.broadcasted_iota(jnp.int32, (VT_ROWS - NSA_HEAD_DIM, n), 0) == 0, 1.0, 0.0)
    parts = []
    for h in range(NSA_KV_HEADS):
        parts += [vt[h * NSA_HEAD_DIM:(h + 1) * NSA_HEAD_DIM, :], pad]
    return jnp.concatenate(parts, axis=0).astype(BF16)


def _inproj_kernel(x_ref, mod_ref, nw_ref, wq_ref, wkv_ref, wml_ref, wmisc_ref, b_ref,
                   qpad_ref, kc_ref, vc_ref, ksel_ref, kwin_ref, vtsel_ref, vtwin_ref,
                   ml_ref, vtm_ref, ogt_ref, misc_ref, misct_ref, rows_ref):
    tm = x_ref.shape[0]
    h = _norm_mod(x_ref[...], nw_ref[...], mod_ref[0, 3:4, :], mod_ref[0, 4:5, :])
    hb = h.astype(BF16)

    def proj(w_ref, c0, width, b0):
        return _dot(hb, w_ref[:, c0:c0 + width]) + b_ref[:, b0 + c0:b0 + c0 + width]

    def half_blocks(val, out_ref):
        rows_ref[...] = val
        pieces = [rows_ref[pl.ds(l, tm // CMP_STRIDE, stride=CMP_STRIDE), :] for l in range(CMP_STRIDE)]
        out_ref[0] = jnp.concatenate(pieces, axis=1).astype(BF16)

    scale = NSA_HEAD_DIM ** -0.5 * LOG2E
    qt = (proj(wq_ref, 0, NSA_WIDTH, 0) * scale).T.astype(BF16)
    zero_rows = jnp.zeros((NSA_HEAD_DIM, tm), BF16)
    for i in range(NSA_Q_HEADS):
        slots = [zero_rows] * NSA_KV_HEADS
        slots[i // NSA_GROUP] = qt[i * NSA_HEAD_DIM:(i + 1) * NSA_HEAD_DIM, :]
        qpad_ref[0, i * LANES:(i + 1) * LANES, :] = jnp.concatenate(slots, axis=0)
    kv_c = proj(wkv_ref, 0, 2 * KV_WIDTH, B_KV)
    half_blocks(kv_c[:, :KV_WIDTH], kc_ref)
    half_blocks(kv_c[:, KV_WIDTH:], vc_ref)
    kv_s = proj(wkv_ref, 2 * KV_WIDTH, 2 * KV_WIDTH, B_KV)
    ksel_ref[...] = kv_s[:, :KV_WIDTH].astype(BF16)
    vtsel_ref[0] = _vt_with_ones(kv_s[:, KV_WIDTH:])
    kv_w = proj(wkv_ref, 4 * KV_WIDTH, 2 * KV_WIDTH, B_KV)
    kwin_ref[...] = kv_w[:, :KV_WIDTH].astype(BF16)
    vtwin_ref[0] = _vt_with_ones(kv_w[:, KV_WIDTH:])
    for i in range(2):
        ml_ref[:, i * MLSTM_WIDTH:(i + 1) * MLSTM_WIDTH] = proj(wml_ref, i * MLSTM_WIDTH, MLSTM_WIDTH, B_ML)
    vtm_ref[0] = proj(wml_ref, 2 * MLSTM_WIDTH, MLSTM_WIDTH, B_ML).T.astype(BF16)
    ogt_ref[0] = proj(wml_ref, 3 * MLSTM_WIDTH, MLSTM_WIDTH, B_ML).T
    misc = proj(wmisc_ref, 0, LANES, B_MISC)
    misc_ref[...] = misc
    misct_ref[0] = misc.T


def _inproj(x2d, mod, nw, weights, b_pack, batch, seq):
    t, d = x2d.shape
    tm = TOKEN_TILE
    tpb = seq // tm
    row = lambda i: (i, 0)
    const = lambda i: (0, 0)
    trow = lambda i: (i // tpb, 0, i % tpb)
    hrow = lambda i: (i // tpb, i % tpb, 0)
    vt_rows = NSA_KV_HEADS * VT_ROWS
    n_half = seq // CMP_STRIDE
    out_shapes = [
        jax.ShapeDtypeStruct((batch, QPAD_W, seq), BF16),
        jax.ShapeDtypeStruct((batch, n_half, HALF_BLOCK_W), BF16),
        jax.ShapeDtypeStruct((batch, n_half, HALF_BLOCK_W), BF16),
        jax.ShapeDtypeStruct((t, KV_WIDTH), BF16),
        jax.ShapeDtypeStruct((t, KV_WIDTH), BF16),
        jax.ShapeDtypeStruct((batch, vt_rows, seq), BF16),
        jax.ShapeDtypeStruct((batch, vt_rows, seq), BF16),
        jax.ShapeDtypeStruct((t, 2 * MLSTM_WIDTH), F32),
        jax.ShapeDtypeStruct((batch, MLSTM_WIDTH, seq), BF16),
        jax.ShapeDtypeStruct((batch, MLSTM_WIDTH, seq), F32),
        jax.ShapeDtypeStruct((t, LANES), F32),
        jax.ShapeDtypeStruct((batch, LANES, seq), F32),
    ]
    out_specs = [
        pl.BlockSpec((1, QPAD_W, tm), trow),
        pl.BlockSpec((1, tm // CMP_STRIDE, HALF_BLOCK_W), hrow),
        pl.BlockSpec((1, tm // CMP_STRIDE, HALF_BLOCK_W), hrow),
        pl.BlockSpec((tm, KV_WIDTH), row),
        pl.BlockSpec((tm, KV_WIDTH), row),
        pl.BlockSpec((1, vt_rows, tm), trow),
        pl.BlockSpec((1, vt_rows, tm), trow),
        pl.BlockSpec((tm, 2 * MLSTM_WIDTH), row),
        pl.BlockSpec((1, MLSTM_WIDTH, tm), trow),
        pl.BlockSpec((1, MLSTM_WIDTH, tm), trow),
        pl.BlockSpec((tm, LANES), row),
        pl.BlockSpec((1, LANES, tm), trow),
    ]
    return pl.pallas_call(
        _inproj_kernel,
        grid=(t // tm,),
        in_specs=[pl.BlockSpec((tm, d), row),
                  pl.BlockSpec((1, N_MOD, d), lambda i: (i // tpb, 0, 0)),
                  pl.BlockSpec((1, d), const)]
                 + [pl.BlockSpec(w.shape, const) for w in weights]
                 + [pl.BlockSpec(b_pack.shape, const)],
        out_specs=out_specs,
        out_shape=out_shapes,
        scratch_shapes=[pltpu.VMEM((tm, KV_WIDTH), F32)],
        compiler_params=_cparams(1),
        name="inproj",
    )(x2d, mod, nw, *weights, b_pack)


def _split_inproj(w_in, b_in):
    d = w_in.shape[0]
    offs = np.cumsum([0, NSA_WIDTH] + [KV_WIDTH] * 6 + [N_GATES] + [MLSTM_WIDTH] * 4 + [MLSTM_HEADS] * 2)
    g0, m0, i0 = int(offs[7]), int(offs[8]), int(offs[12])
    pad = LANES - N_GATES - 2 * MLSTM_HEADS
    misc_w = jnp.concatenate([w_in[:, g0:g0 + N_GATES], w_in[:, i0:i0 + 2 * MLSTM_HEADS],
                              jnp.zeros((d, pad), w_in.dtype)], axis=1)
    misc_b = jnp.concatenate([b_in[g0:g0 + N_GATES], b_in[i0:i0 + 2 * MLSTM_HEADS], jnp.zeros((pad,), b_in.dtype)])
    weights = [w_in[:, :NSA_WIDTH].astype(BF16), w_in[:, NSA_WIDTH:g0].astype(BF16),
               w_in[:, m0:i0].astype(BF16), misc_w.astype(BF16)]
    b_pack = jnp.concatenate([b_in[:g0], b_in[m0:i0], misc_b]).reshape(1, -1).astype(F32)
    assert b_pack.shape[1] == B_MISC + LANES
    return weights, b_pack


def _compress_kernel(xk_ref, xv_ref, wa_ref, wb_ref, pe_ref, w1_ref, w2_ref, kc_ref, vtc_ref):
    n_half = xk_ref.shape[1]

    def one(x_ref, s):
        x = x_ref[0]
        a = _dot(x, wa_ref[s])
        b = _dot(x, wb_ref[s])
        pe_term = jnp.dot(pe_ref[s], w1_ref[s], preferred_element_type=F32, precision=HIGHEST)[0:1, :]
        pe2 = jnp.concatenate([pe_term] * NSA_KV_HEADS, axis=1)
        pre = a + pltpu.roll(b, n_half - 1, 0) + pe2
        hid = 0.5 * pre * (1.0 + jnp.tanh(0.7978845608028654 * (pre + 0.044715 * pre * pre * pre)))
        return _dot(hid.astype(BF16), w2_ref[s])

    kc_ref[0] = one(xk_ref, 0).astype(BF16)
    vtc_ref[0] = _vt_with_ones(one(xv_ref, 1))


def _compress(xk, xv, wa, wb, pe8, w1, w2e):
    batch, n_half, width = xk.shape
    bsel = lambda b: (b, 0, 0)
    c3 = lambda b: (0, 0, 0)
    vt_rows = NSA_KV_HEADS * VT_ROWS
    return pl.pallas_call(
        _compress_kernel,
        grid=(batch,),
        in_specs=[pl.BlockSpec((1, n_half, width), bsel), pl.BlockSpec((1, n_half, width), bsel),
                  pl.BlockSpec(wa.shape, c3), pl.BlockSpec(wb.shape, c3),
                  pl.BlockSpec(pe8.shape, c3), pl.BlockSpec(w1.shape, c3), pl.BlockSpec(w2e.shape, c3)],
        out_specs=[pl.BlockSpec((1, n_half, KV_WIDTH), bsel), pl.BlockSpec((1, vt_rows, n_half), bsel)],
        out_shape=[jax.ShapeDtypeStruct((batch, n_half, KV_WIDTH), BF16),
                   jax.ShapeDtypeStruct((batch, vt_rows, n_half), BF16)],
        compiler_params=_cparams(1),
        name="compress",
    )(xk, xv, wa, wb, pe8, w1, w2e)


def _pack_compress(pe, w1, w2):
    half = CMP_BLOCK // 2
    w1r = w1.reshape(CMP_BLOCK, NSA_HEAD_DIM, CMP_HIDDEN)
    eye = jnp.eye(NSA_KV_HEADS, dtype=w1.dtype)

    def expand(w_half):
        z = w_half[:, None, :, None, :] * eye[None, :, None, :, None]
        return z.reshape(half * KV_WIDTH, NSA_KV_HEADS * CMP_HIDDEN)

    wa = expand(w1r[:half]).astype(BF16)
    wb = expand(w1r[half:]).astype(BF16)
    w2e = (w2[None, :, None, :] * eye[:, None, :, None]).reshape(NSA_KV_HEADS * CMP_HIDDEN, KV_WIDTH).astype(BF16)
    pe8 = jnp.concatenate([pe.reshape(1, CMP_BLOCK * NSA_HEAD_DIM),
                           jnp.zeros((7, CMP_BLOCK * NSA_HEAD_DIM), pe.dtype)], axis=0)
    return wa, wb, pe8, w1, w2e


def _q_aug(q_ref, qs_ref, h):
    qt = jnp.concatenate(
        [q_ref[0, (NSA_GROUP * h + g) * LANES:(NSA_GROUP * h + g + 1) * LANES, :] for g in range(NSA_GROUP)], axis=1)
    return jnp.concatenate([qt, qs_ref[h]], axis=0)


def _slope2(head):
    return float(2.0 ** (-8.0 * (head + 1) / NSA_Q_HEADS)) * LOG2E


def _bf16_split3(x):
    parts = []
    for _ in range(3):
        p = float(np.asarray(x, np.float32).astype(BF16).astype(np.float32))
        parts.append(p)
        x = x - p
    return parts


def _init_alibi_operands(qs_ref, kpos_ref, key_stride):
    lane = lax.broadcasted_iota(jnp.int32, (QT, LANES), 1)
    row = lax.broadcasted_iota(jnp.int32, (QT, LANES), 0)
    kpos_ref[...] = jnp.where(lane < 3, row.astype(F32), 0.0).astype(BF16)
    srow = lax.broadcasted_iota(jnp.int32, (LANES, QT), 0)
    for h in range(NSA_KV_HEADS):
        for g in range(NSA_GROUP):
            hi, mid, lo = _bf16_split3(key_stride * _slope2(NSA_GROUP * h + g))
            blk = jnp.where(srow == 0, hi, jnp.where(srow == 1, mid, jnp.where(srow == 2, lo, 0.0)))
            qs_ref[h, :, g * QT:(g + 1) * QT] = blk.astype(BF16)


def _gate_row(gt_ref, head, branch):
    r = head * 3 + branch
    return _sigmoid(gt_ref[0, r:r + 1, :])


def _softmax_jobs(jobs):
    scores = [_dot(jnp.concatenate(job[1], axis=0), job[0]) for job in jobs]
    staged = []
    for (q_aug, k_tiles, masks, offs, value_groups, state), st_all in zip(jobs, scores):
        n_t = len(k_tiles)
        per_g = []
        for g in range(NSA_GROUP):
            sl = slice(g * QT, (g + 1) * QT)
            ss = []
            for u in range(n_t):
                s = st_all[u * QT:(u + 1) * QT, sl]
                ss.append(s if masks[u] is None else jnp.where(masks[u], s, NEG))
            mx = None
            for u in range(n_t):
                cand = jnp.max(ss[u], axis=0, keepdims=True) + offs[u][g]
                mx = cand if mx is None else jnp.maximum(mx, cand)
            alpha = None
            if state is None:
                m_new = mx
            else:
                m_ref, _, h = state
                m_old = m_ref[h, :, sl]
                m_new = jnp.maximum(m_old, mx)
                alpha = jnp.exp2(m_old - m_new)
                m_ref[h, :, sl] = m_new
            m_use = jnp.where(m_new < 0.5 * NEG, 0.0, m_new)
            per_g.append((ss, m_use, alpha))
        staged.append(per_g)
    ets = []
    for (q_aug, k_tiles, masks, offs, value_groups, state), per_g in zip(jobs, staged):
        n_t = len(k_tiles)
        rows = [jnp.concatenate([jnp.exp2(per_g[g][0][u] - (per_g[g][1] - offs[u][g])).astype(BF16)
                                 for g in range(NSA_GROUP)], axis=1) for u in range(n_t)]
        ets.append(jnp.concatenate(rows, axis=0))
    results = []
    for (q_aug, k_tiles, masks, offs, value_groups, state), per_g, et in zip(jobs, staged, ets):
        mats = [jnp.concatenate(grp, axis=1) for grp in value_groups]
        pv_all = _dot(jnp.concatenate(mats, axis=0), et)
        pvs, r0 = [], 0
        for mat in mats:
            pvs.append(pv_all[r0:r0 + mat.shape[0], :])
            r0 += mat.shape[0]
        if state is None:
            results.append(pvs)
        else:
            _, acc_ref, h = state
            acc_ref[h] = jnp.concatenate([pg[2] for pg in per_g], axis=1) * acc_ref[h] + pvs[0]
            results.append(None)
    return results


def _gated(gt_ref, h, branch, pv):
    inv = 1.0 / jnp.maximum(pv[NSA_HEAD_DIM:NSA_HEAD_DIM + 1, :], 1e-30)
    out = []
    for g in range(NSA_GROUP):
        sl = slice(g * QT, (g + 1) * QT)
        out.append(pv[0:NSA_HEAD_DIM, sl] * (inv[:, sl] * _gate_row(gt_ref, NSA_GROUP * h + g, branch)))
    return out


def _nsa_cmp_kernel(q_ref, kc_ref, vtc_ref, gt_ref, oc_ref, sel_ref, cnt_ref,
                    qs_ref, kpos_ref, ovt_ref, pvo_ref, pvi_ref):
    ncp = kc_ref.shape[1]
    n_ct = ncp // QT
    n_sel = sel_ref.shape[3]
    start = pl.program_id(1) * QT
    tile_span = QT * CMP_STRIDE

    @pl.when((pl.program_id(0) == 0) & (pl.program_id(1) == 0))
    def _():
        _init_alibi_operands(qs_ref, kpos_ref, float(CMP_STRIDE))
        j_o = lax.broadcasted_iota(jnp.int32, (n_sel, ncp), 0)
        c_o = lax.broadcasted_iota(jnp.int32, (n_sel, ncp), 1)
        ov = ((c_o * CMP_STRIDE <= j_o * SEL_BLOCK + (SEL_BLOCK - 1))
              & (c_o * CMP_STRIDE + (CMP_BLOCK - 1) >= j_o * SEL_BLOCK))
        ovt_ref[...] = jnp.where(ov, 1.0, 0.0).astype(BF16)

    a0 = (lax.broadcasted_iota(jnp.int32, (QT, QT), 1)
          - CMP_STRIDE * lax.broadcasted_iota(jnp.int32, (QT, QT), 0))
    q_augs = [_q_aug(q_ref, qs_ref, h) for h in range(NSA_KV_HEADS)]
    n_vis = jnp.minimum((start + QT - CMP_BLOCK) // tile_span + 1, n_ct)

    for k in range(1, n_ct + 1):
        @pl.when(n_vis == k)
        def _(k=k):
            jobs = []
            for h in range(NSA_KV_HEADS):
                k_tiles, masks, offs, vts, ovs = [], [], [], [], []
                for ci in range(k):
                    cs = slice(ci * QT, (ci + 1) * QT)
                    first_end = ci * tile_span + CMP_BLOCK - 1
                    k_tiles.append(jnp.concatenate([kc_ref[0, cs, :], kpos_ref[...]], axis=1))
                    masks.append(a0 + (start - first_end) >= 0)
                    offs.append([_slope2(NSA_GROUP * h + g) * first_end for g in range(NSA_GROUP)])
                    vts.append(vtc_ref[0, h * VT_ROWS:(h + 1) * VT_ROWS, cs])
                    ovs.append(ovt_ref[:, cs])
                jobs.append((q_augs[h], k_tiles, masks, offs, [vts, ovs], None))
            for h, (pvo, pvi) in enumerate(_softmax_jobs(jobs)):
                pvo_ref[h] = pvo
                pvi_ref[h] = pvi

    j_io = lax.broadcasted_iota(jnp.int32, (n_sel, QT), 0)
    t_j = start + lax.broadcasted_iota(jnp.int32, (n_sel, QT), 1)
    cur = t_j // SEL_BLOCK
    valid = j_io * SEL_BLOCK <= t_j
    forced = (j_io == 0) | (j_io == cur) | (j_io == cur - 1)
    ones8 = jnp.ones((8, QT), BF16)

    taken = -(2.0 ** 126)
    scores = []
    for h in range(NSA_KV_HEADS):
        pvo = pvo_ref[h]
        o_c = _gated(gt_ref, h, 0, pvo)
        inv = 1.0 / jnp.maximum(pvo[NSA_HEAD_DIM:NSA_HEAD_DIM + 1, :], 1e-30)
        imp = None
        for g in range(NSA_GROUP):
            sl = slice(g * QT, (g + 1) * QT)
            oc_ref[0, NSA_GROUP * h + g] = o_c[g]
            term = pvi_ref[h, :, sl] * inv[:, sl]
            imp = term if imp is None else imp + term
        scores.append(jnp.where(forced, taken, jnp.where(valid, imp, -FORCE_BONUS)))

    n_cls = n_sel // TOPK_ROWS
    cls = jnp.minimum((start + QT + SEL_BLOCK * TOPK_ROWS - 1) // (SEL_BLOCK * TOPK_ROWS), n_cls)

    for k in range(1, n_cls + 1):
        @pl.when(cls == k)
        def _(rows=k * TOPK_ROWS):
            j_sub = j_io[:rows]

            def pick(_, carry):
                out = []
                for sc in carry:
                    mx = jnp.max(sc, axis=0, keepdims=True)
                    idx = jnp.min(jnp.where(sc == mx, j_sub, n_sel), axis=0, keepdims=True)
                    out.append(jnp.where(j_sub == idx, taken, sc))
                return tuple(out)

            picked = lax.fori_loop(0, SEL_TOPK - N_FORCED, pick, tuple(sc[:rows] for sc in scores))
            for h in range(NSA_KV_HEADS):
                sel = jnp.where(picked[h] <= 0.5 * taken, 1.0, 0.0)
                if rows < n_sel:
                    sel = jnp.concatenate([sel, jnp.zeros((n_sel - rows, QT), F32)], axis=0)
                sel_ref[0, h, 0] = sel
                cnt = _dot_nt(ones8, sel.astype(BF16))
                cnt_ref[0, 0, h] = cnt[0:1, :]


def _nsa_cmp(qpad3, kc, vtc, gates_t, batch, seq):
    nb = seq // QT
    n_sel = seq // SEL_BLOCK
    ncp = kc.shape[1]
    assert ncp % QT == 0
    vt_rows = NSA_KV_HEADS * VT_ROWS
    return pl.pallas_call(
        _nsa_cmp_kernel,
        grid=(batch, nb),
        in_specs=[pl.BlockSpec((1, QPAD_W, QT), lambda b, i: (b, 0, i)),
                  pl.BlockSpec((1, ncp, KV_WIDTH), lambda b, i: (b, 0, 0)),
                  pl.BlockSpec((1, vt_rows, ncp), lambda b, i: (b, 0, 0)),
                  pl.BlockSpec((1, N_GATES, QT), lambda b, i: (b, 0, i))],
        out_specs=[pl.BlockSpec((1, NSA_Q_HEADS, NSA_HEAD_DIM, QT), lambda b, i: (b, 0, 0, i)),
                   pl.BlockSpec((1, NSA_KV_HEADS, 1, n_sel, QT), lambda b, i: (b, 0, i, 0, 0)),
                   pl.BlockSpec((1, 1, NSA_KV_HEADS, 1, n_sel), lambda b, i: (b, i, 0, 0, 0))],
        out_shape=[jax.ShapeDtypeStruct((batch, NSA_Q_HEADS, NSA_HEAD_DIM, seq), F32),
                   jax.ShapeDtypeStruct((batch, NSA_KV_HEADS, nb, n_sel, QT), F32),
                   jax.ShapeDtypeStruct((batch, nb, NSA_KV_HEADS, 1, n_sel), F32)],
        scratch_shapes=[pltpu.VMEM((NSA_KV_HEADS, LANES, NSA_GROUP * QT), BF16),
                        pltpu.VMEM((QT, LANES), BF16),
                        pltpu.VMEM((n_sel, ncp), BF16),
                        pltpu.VMEM((NSA_KV_HEADS, VT_ROWS, NSA_GROUP * QT), F32),
                        pltpu.VMEM((NSA_KV_HEADS, n_sel, NSA_GROUP * QT), F32)],
        compiler_params=_cparams(2),
        name="nsa_cmp",
    )(qpad3, kc, vtc, gates_t)


def _nsa_sel_kernel(count_ref, list_ref, q_ref, ksel_ref, vtsel_ref, kwin_ref, vtwin_ref, selm_ref, gt_ref,
                    oc_ref, o_ref, m_sel, acc_sel, qs_ref, kpos_ref, part_ref):
    b = pl.program_id(0)
    qb = pl.program_id(1)
    nb = pl.num_programs(1)
    heads = range(NSA_KV_HEADS)
    k_io = lax.broadcasted_iota(jnp.int32, (QT, QT), 0)
    q_io = lax.broadcasted_iota(jnp.int32, (QT, QT), 1)
    causal = q_io >= k_io
    in_window = q_io < k_io
    first_half = k_io < SEL_BLOCK

    @pl.when((b == 0) & (qb == 0))
    def _():
        _init_alibi_operands(qs_ref, kpos_ref, 1.0)

    def k_aug(k_ref, t):
        off = pl.multiple_of(t * QT, QT)
        return jnp.concatenate([k_ref[0, pl.ds(off, QT), :], kpos_ref[...]], axis=1)

    def v_ext(vt_ref, h, t):
        return vt_ref[0, h * VT_ROWS:(h + 1) * VT_ROWS, pl.ds(pl.multiple_of(t * QT, QT), QT)]

    def tile_offs(h, t, live=None):
        pos = (t * QT).astype(F32)
        out = [_slope2(NSA_GROUP * h + g) * pos for g in range(NSA_GROUP)]
        return out if live is None else [jnp.where(live, o, NEG) for o in out]

    def chosen(h, t):
        mrow = selm_ref[0, h, 0, pl.ds(2 * t, 2), :]
        return jnp.where(first_half, mrow[0:1, :], mrow[1:2, :]) > 0.5

    q_augs = [_q_aug(q_ref, qs_ref, h) for h in heads]

    slot = [(b * nb + qb) * NSA_KV_HEADS + h for h in heads]
    n_tiles = [count_ref[slot[h]] for h in heads]

    m_sel[...] = jnp.full(m_sel.shape, NEG, F32)
    acc_sel[...] = jnp.zeros(acc_sel.shape, F32)
    n_win = WINDOW // QT + 1
    jobs = []
    for h in heads:
        jobs.append((q_augs[h], [k_aug(ksel_ref, qb)], [chosen(h, qb) & causal], [tile_offs(h, qb)],
                     [[v_ext(vtsel_ref, h, qb)]], (m_sel, acc_sel, h)))
    for h in heads:
        k_tiles, masks, offs, vts = [], [], [], []
        for u in range(n_win):
            raw = qb - (n_win - 1) + u
            t = jnp.maximum(raw, 0)
            k_tiles.append(k_aug(kwin_ref, t))
            masks.append(in_window if u == 0 else causal if u == n_win - 1 else None)
            offs.append(tile_offs(h, t, raw >= 0))
            vts.append(v_ext(vtwin_ref, h, t))
        jobs.append((q_augs[h], k_tiles, masks, offs, [vts], None))
    results = _softmax_jobs(jobs)
    for h in heads:
        o_w = _gated(gt_ref, h, 2, results[NSA_KV_HEADS + h][0])
        for g in range(NSA_GROUP):
            head = NSA_GROUP * h + g
            part_ref[head] = oc_ref[0, head] + o_w[g]

    def sel_body(i, carry):
        jobs = []
        for step in range(SEL_CHAIN):
            for h in heads:
                k_tiles, masks, offs, vts = [], [], [], []
                for u in range(SEL_UNROLL):
                    idx = (i * SEL_CHAIN + step) * SEL_UNROLL + u
                    live = idx < n_tiles[h]
                    t = jnp.where(live, list_ref[slot[h] * nb + jnp.minimum(idx, nb - 1)], 0)
                    k_tiles.append(k_aug(ksel_ref, t))
                    masks.append(chosen(h, t))
                    offs.append(tile_offs(h, t, live))
                    vts.append(v_ext(vtsel_ref, h, t))
                jobs.append((q_augs[h], k_tiles, masks, offs, [vts], (m_sel, acc_sel, h)))
        _softmax_jobs(jobs)
        return carry

    per_trip = SEL_UNROLL * SEL_CHAIN
    n_trips = (functools.reduce(jnp.maximum, n_tiles) + per_trip - 1) // per_trip
    lax.fori_loop(0, n_trips, sel_body, 0)

    for h in heads:
        o_s = _gated(gt_ref, h, 1, acc_sel[h])
        for g in range(NSA_GROUP):
            head = NSA_GROUP * h + g
            o_ref[0, head] = part_ref[head] + o_s[g]


def _tile_lists(cnt, batch, nb):
    pair = cnt.reshape(batch, nb, NSA_KV_HEADS, nb, 2).sum(axis=-1)
    p_io = lax.broadcasted_iota(jnp.int32, pair.shape, 3)
    flag = (pair > 0.5) & (p_io < lax.broadcasted_iota(jnp.int32, pair.shape, 1))
    rank = jnp.cumsum(flag.astype(jnp.int32), axis=-1) - 1
    slot_io = lax.broadcasted_iota(jnp.int32, pair.shape + (nb,), 4)
    hit = flag[..., None] & (rank[..., None] == slot_io)
    lists = jnp.sum(jnp.where(hit, p_io[..., None], 0), axis=3)
    counts = jnp.sum(flag.astype(jnp.int32), axis=-1)
    return counts.reshape(-1), lists.reshape(-1)


def _nsa_sel(counts, lists, qpad3, ksel, vtsel, kwin, vtwin, selm, gates_t, oc, batch, seq):
    nb = seq // QT
    res = lambda b, i, c, l: (b, 0, 0)
    vt_rows = NSA_KV_HEADS * VT_ROWS
    grid_spec = pltpu.PrefetchScalarGridSpec(
        num_scalar_prefetch=2,
        grid=(batch, nb),
        in_specs=[pl.BlockSpec((1, QPAD_W, QT), lambda b, i, c, l: (b, 0, i)),
                  pl.BlockSpec((1, seq, KV_WIDTH), res),
                  pl.BlockSpec((1, vt_rows, seq), res),
                  pl.BlockSpec((1, seq, KV_WIDTH), res),
                  pl.BlockSpec((1, vt_rows, seq), res),
                  pl.BlockSpec((1, NSA_KV_HEADS, 1, 2 * nb, QT), lambda b, i, c, l: (b, 0, i, 0, 0)),
                  pl.BlockSpec((1, N_GATES, QT), lambda b, i, c, l: (b, 0, i)),
                  pl.BlockSpec((1, NSA_Q_HEADS, NSA_HEAD_DIM, QT), lambda b, i, c, l: (b, 0, 0, i))],
        out_specs=pl.BlockSpec((1, NSA_Q_HEADS, NSA_HEAD_DIM, QT), lambda b, i, c, l: (b, 0, 0, i)),
        scratch_shapes=[pltpu.VMEM((NSA_KV_HEADS, 1, NSA_GROUP * QT), F32),
                        pltpu.VMEM((NSA_KV_HEADS, VT_ROWS, NSA_GROUP * QT), F32),
                        pltpu.VMEM((NSA_KV_HEADS, LANES, NSA_GROUP * QT), BF16),
                        pltpu.VMEM((QT, LANES), BF16),
                        pltpu.VMEM((NSA_Q_HEADS, NSA_HEAD_DIM, QT), F32)],
    )
    return pl.pallas_call(
        _nsa_sel_kernel,
        grid_spec=grid_spec,
        out_shape=jax.ShapeDtypeStruct((batch, NSA_Q_HEADS, NSA_HEAD_DIM, seq), F32),
        compiler_params=_cparams(2),
        name="nsa_sel",
    )(counts, lists, qpad3, ksel, vtsel, kwin, vtwin, selm, gates_t, oc)


def _mlstm_kernel(q_ref, k_ref, vt_ref, ogt_ref, misc_ref, misct_ref, cw_ref, cb_ref, nw_ref,
                  o_ref, xq_ref, xk_ref, c_ref, n_ref, m_ref):
    tm = q_ref.shape[0]
    L = MLSTM_L
    hd = MLSTM_HEAD_DIM
    halo = 8

    @pl.when(pl.program_id(1) == 0)
    def _():
        xq_ref[0:halo, :] = jnp.zeros((halo, MLSTM_WIDTH), F32)
        xk_ref[0:halo, :] = jnp.zeros((halo, MLSTM_WIDTH), F32)
        c_ref[...] = jnp.zeros_like(c_ref)
        n_ref[...] = jnp.zeros_like(n_ref)
        m_ref[...] = jnp.zeros_like(m_ref)

    def conv(x_ref, buf_ref, col0):
        buf_ref[halo:halo + tm, :] = x_ref[...]
        y = cb_ref[:, col0:col0 + MLSTM_WIDTH]
        for kk in range(CONV_WIDTH):
            r0 = halo - (CONV_WIDTH - 1) + kk
            y = y + cw_ref[kk:kk + 1, col0:col0 + MLSTM_WIDTH] * buf_ref[r0:r0 + tm, :]
        tail = buf_ref[tm:tm + halo, :]
        buf_ref[0:halo, :] = tail
        return _silu(y)

    qc = conv(q_ref, xq_ref, 0)
    kc = conv(k_ref, xk_ref, MLSTM_WIDTH) * (hd ** -0.5)

    misc = misc_ref[...]
    misct = misct_ref[0]
    lf_c = -jnp.log(1.0 + jnp.exp(-misc))
    lf_r = -jnp.log(1.0 + jnp.exp(-misct))
    row_io = lax.broadcasted_iota(jnp.int32, (L, L), 0)
    col_io = lax.broadcasted_iota(jnp.int32, (L, L), 1)
    tril = jnp.where(col_io <= row_io, 1.0, 0.0).astype(BF16)
    triu = jnp.where(row_io <= col_io, 1.0, 0.0).astype(BF16)

    def split3(x):
        hi = x.astype(BF16)
        r1 = x - hi.astype(F32)
        mid = r1.astype(BF16)
        return hi, mid, (r1 - mid.astype(F32)).astype(BF16)

    lf_c3 = split3(lf_c)
    lf_r3 = split3(lf_r)
    b_c = jnp.concatenate([sum(_dot(tril, p[ci * L:(ci + 1) * L, :]) for p in lf_c3)
                           for ci in range(tm // L)], axis=0)
    b_r = jnp.concatenate([sum(_dot(p[:, ci * L:(ci + 1) * L], triu) for p in lf_r3)
                           for ci in range(tm // L)], axis=1)

    tri_t = lax.broadcasted_iota(jnp.int32, (L, L), 0) <= lax.broadcasted_iota(jnp.int32, (L, L), 1)
    nw_cols = [jnp.broadcast_to(nw_ref[:, h * hd:(h + 1) * hd], (hd, hd)).T for h in range(MLSTM_HEADS)]

    def split3_rows(row):
        hi = row.astype(BF16)
        r1 = row - hi.astype(F32)
        mid = r1.astype(BF16)
        lo = (r1 - mid.astype(F32)).astype(BF16)
        sub = lax.broadcasted_iota(jnp.int32, (8, row.shape[1]), 0)
        out = jnp.where(sub == 0, hi.astype(F32), jnp.where(sub == 1, mid.astype(F32),
                                                            jnp.where(sub == 2, lo.astype(F32), 0.0)))
        return out.astype(BF16)

    def rows_sum3(x8):
        return x8[0:1, :] + x8[1:2, :] + x8[2:3, :]

    for ci in range(tm // L):
        r0 = ci * L
        for h in range(MLSTM_HEADS):
            cs = slice(h * hd, (h + 1) * hd)
            qb16 = qc[r0:r0 + L, cs].astype(BF16)
            kb16 = kc[r0:r0 + L, cs].astype(BF16)
            vt = vt_ref[0, cs, r0:r0 + L]
            a_col = (misc[r0:r0 + L, MISC_I + h:MISC_I + h + 1]
                     - b_c[r0:r0 + L, MISC_F + h:MISC_F + h + 1])
            brow = b_r[MISC_F + h:MISC_F + h + 1, r0:r0 + L]
            lirow = misct[MISC_I + h:MISC_I + h + 1, r0:r0 + L]
            m_prev = m_ref[h]
            ct = c_ref[h]
            nrow = n_ref[h]

            dmat = jnp.where(tri_t, brow + a_col, NEG)
            m_inter = brow + m_prev
            m_t = jnp.maximum(jnp.max(dmat, axis=0, keepdims=True), m_inter)
            w = jnp.exp(dmat - m_t) * _dot_nt(kb16, qb16)
            decay = jnp.exp(m_inter - m_t)
            num = _dot(vt, w.astype(BF16)) + decay * _dot_nt(ct.astype(BF16), qb16)
            nq = rows_sum3(_dot_nt(split3_rows(nrow), qb16))
            den = jnp.sum(w, axis=0, keepdims=True) + decay * nq
            hout = num * (1.0 / jnp.maximum(jnp.abs(den), jnp.exp(-m_t)))

            b_last = brow[:, L - 1:L]
            grow = b_last - brow + lirow
            m_new = jnp.maximum(b_last + m_prev, jnp.max(grow, axis=1, keepdims=True))
            wk = jnp.exp(grow - m_new)
            d_c = jnp.exp(b_last + m_prev - m_new)
            c_ref[h] = d_c * ct + _dot((vt.astype(F32) * wk).astype(BF16), kb16)
            n_ref[h] = d_c * nrow + rows_sum3(_dot(split3_rows(wk), kb16))
            m_ref[h] = m_new

            hn = hout * lax.rsqrt(jnp.mean(hout * hout, axis=0, keepdims=True) + EPS) * nw_cols[h]
            o_ref[0, cs, r0:r0 + L] = (hn * _sigmoid(ogt_ref[0, cs, r0:r0 + L])).astype(o_ref.dtype)


def _mlstm(ml, vt_m, og_t, misc, misc_t, conv_w, conv_b, norm_w, batch, seq):
    tm = TOKEN_TILE
    tpb = seq // tm
    col = lambda j: (lambda b, i: (b * tpb + i, j))
    const = lambda b, i: (0, 0)
    trow = lambda b, i: (b, 0, i)
    return pl.pallas_call(
        _mlstm_kernel,
        grid=(batch, tpb),
        in_specs=[pl.BlockSpec((tm, MLSTM_WIDTH), col(0)),
                  pl.BlockSpec((tm, MLSTM_WIDTH), col(1)),
                  pl.BlockSpec((1, MLSTM_WIDTH, tm), trow),
                  pl.BlockSpec((1, MLSTM_WIDTH, tm), trow),
                  pl.BlockSpec((tm, LANES), col(0)),
                  pl.BlockSpec((1, LANES, tm), trow),
                  pl.BlockSpec(conv_w.shape, const),
                  pl.BlockSpec(conv_b.shape, const),
                  pl.BlockSpec(norm_w.shape, const)],
        out_specs=pl.BlockSpec((1, MLSTM_WIDTH, tm), trow),
        out_shape=jax.ShapeDtypeStruct((batch, MLSTM_WIDTH, seq), BF16),
        scratch_shapes=[pltpu.VMEM((tm + 8, MLSTM_WIDTH), F32),
                        pltpu.VMEM((tm + 8, MLSTM_WIDTH), F32),
                        pltpu.VMEM((MLSTM_HEADS, MLSTM_HEAD_DIM, MLSTM_HEAD_DIM), F32),
                        pltpu.VMEM((MLSTM_HEADS, 1, MLSTM_HEAD_DIM), F32),
                        pltpu.VMEM((MLSTM_HEADS, 1, 1), F32)],
        compiler_params=_cparams(2),
        name="mlstm",
    )(ml, ml, vt_m, og_t, misc, misc_t, conv_w, conv_b, norm_w)


def _layer(x2d, c, batch, seq, w_ada, b_ada, norm_ffn1_w, ffn1_w_in, ffn1_w_out, norm_mix_w, w_in, b_in,
           cmp_k_pe, cmp_k_w1, cmp_k_w2, cmp_v_pe, cmp_v_w1, cmp_v_w2, conv_w, conv_b, mlstm_norm_w, w_out,
           norm_ffn2_w, ffn2_w_in, ffn2_w_out, final_nw):
    d = x2d.shape[1]
    tpb = seq // TOKEN_TILE
    mod = _adaln(c.T, w_ada, b_ada.reshape(1, -1)).reshape(batch, N_MOD, d)

    x1 = _ffn(x2d, mod, norm_ffn1_w.reshape(1, d), ffn1_w_in.astype(BF16), ffn1_w_out.astype(BF16),
              (0, 1, 2), tpb)

    weights, b_pack = _split_inproj(w_in, b_in)
    (qpad, kc, vc, ksel, kwin, vtsel, vtwin, ml, vt_m, og_t, misc, misc_t) = _inproj(
        x1, mod, norm_mix_w.reshape(1, d), weights, b_pack, batch, seq)

    pk = _pack_compress(cmp_k_pe, cmp_k_w1, cmp_k_w2)
    pv = _pack_compress(cmp_v_pe, cmp_v_w1, cmp_v_w2)
    stacked = [jnp.stack([a, bb]) for a, bb in zip(pk, pv)]
    kcmp, vtcmp = _compress(kc, vc, *stacked)

    nb = seq // QT
    qpad3 = qpad
    gates_t = misc_t
    oc, sel, cnt = _nsa_cmp(qpad3, kcmp, vtcmp, gates_t, batch, seq)
    counts, lists = _tile_lists(cnt, batch, nb)
    o_t = _nsa_sel(counts, lists, qpad3, ksel.reshape(batch, seq, KV_WIDTH), vtsel, kwin.reshape(batch, seq, KV_WIDTH),
                   vtwin, sel, gates_t, oc, batch, seq)
    o_nsa = o_t.reshape(batch, NSA_WIDTH, seq)

    o_ml = _mlstm(ml, vt_m, og_t, misc, misc_t, conv_w, conv_b.reshape(1, -1), mlstm_norm_w.reshape(1, -1), batch, seq)

    return _ffn(x1, mod, norm_ffn2_w.reshape(1, d), ffn2_w_in.astype(BF16), ffn2_w_out.astype(BF16),
                (6, 7, 8), tpb, mix=(o_nsa, o_ml, w_out.astype(BF16)), mix_gate_row=5, final_nw=final_nw)


def kernel(x, c, w_ada, b_ada, norm_ffn1_w, ffn1_w_in, ffn1_w_out, norm_mix_w, w_in, b_in, cmp_k_pe, cmp_k_w1, cmp_k_w2, cmp_v_pe, cmp_v_w1, cmp_v_w2, conv_w, conv_b, mlstm_norm_w, w_out, norm_ffn2_w, ffn2_w_in, ffn2_w_out, final_norm_w):
    batch, seq, d = x.shape
    depth = w_ada.shape[0]
    assert depth == 1 and seq % (QT * CMP_STRIDE) == 0 and seq // SEL_BLOCK >= SEL_TOPK
    y = _layer(x.reshape(batch * seq, d), c, batch, seq, w_ada[0], b_ada[0], norm_ffn1_w[0], ffn1_w_in[0],
               ffn1_w_out[0], norm_mix_w[0], w_in[0], b_in[0], cmp_k_pe[0], cmp_k_w1[0], cmp_k_w2[0],
               cmp_v_pe[0], cmp_v_w1[0], cmp_v_w2[0], conv_w[0], conv_b[0], mlstm_norm_w[0], w_out[0],
               norm_ffn2_w[0], ffn2_w_in[0], ffn2_w_out[0], final_norm_w.reshape(1, d))
    return y.reshape(batch, seq, d)
```

```python
import functools

import numpy as np
import jax
import jax.numpy as jnp
from jax import lax
from jax.experimental import pallas as pl
from jax.experimental.pallas import tpu as pltpu

NSA_Q_HEADS = 8
NSA_KV_HEADS = 2
NSA_GROUP = NSA_Q_HEADS // NSA_KV_HEADS
NSA_HEAD_DIM = 64
CMP_BLOCK = 32
CMP_STRIDE = 16
CMP_HIDDEN = 128
SEL_BLOCK = 64
SEL_TOPK = 16
WINDOW = 512
FORCE_BONUS = 1.0e4
N_FORCED = 3
assert FORCE_BONUS > NSA_GROUP
MLSTM_HEADS = 4
MLSTM_HEAD_DIM = 128
CONV_WIDTH = 4
D_FF = 2816
N_MOD = 9
EPS = 1e-6
NEG = -1e30

NSA_WIDTH = NSA_Q_HEADS * NSA_HEAD_DIM
KV_WIDTH = NSA_KV_HEADS * NSA_HEAD_DIM
MLSTM_WIDTH = MLSTM_HEADS * MLSTM_HEAD_DIM
N_GATES = 3 * NSA_Q_HEADS

LANES = 128
QT = 128
TOKEN_TILE = 512
FF_CHUNK = 256
MLSTM_L = 128
SEL_UNROLL = 6
SEL_CHAIN = 2
TOPK_ROWS = 32
VT_ROWS = 80
VMEM_LIMIT = 56 * 1024 * 1024

F32 = jnp.float32
BF16 = jnp.bfloat16
HIGHEST = lax.Precision.HIGHEST
LOG2E = 1.4426950408889634


def _cparams(n_axes):
    return pltpu.CompilerParams(dimension_semantics=("arbitrary",) * n_axes,
                                vmem_limit_bytes=VMEM_LIMIT)


def _dot(a, b):
    return jnp.dot(a, b, preferred_element_type=F32)


def _dot_nt(a, b):
    return lax.dot_general(a, b, (((1,), (1,)), ((), ())), preferred_element_type=F32)


def _dot_tn(a, b):
    return lax.dot_general(a, b, (((0,), (0,)), ((), ())), preferred_element_type=F32)


def _sigmoid(x):
    return 1.0 / (1.0 + jnp.exp(-x))


def _silu(x):
    return x * _sigmoid(x)


def _norm_mod(x, nw, sh, sc):
    ms = jnp.mean(x * x, axis=-1, keepdims=True)
    y = x * lax.rsqrt(ms + EPS) * nw
    return y * (1.0 + sc) + sh


def _adaln_kernel(ct_ref, w_ref, b_ref, o_ref):
    w = w_ref[...]
    for r in range(ct_ref.shape[1]):
        a_col = _silu(ct_ref[:, r:r + 1])
        o_ref[r:r + 1, :] = jnp.sum(a_col * w, axis=0, keepdims=True) + b_ref[...]


def _adaln(c_t, w_ada, b_ada):
    d, rows = c_t.shape
    n = w_ada.shape[1]
    tn = n // N_MOD
    return pl.pallas_call(
        _adaln_kernel,
        grid=(N_MOD,),
        in_specs=[pl.BlockSpec((d, rows), lambda j: (0, 0)),
                  pl.BlockSpec((d, tn), lambda j: (0, j)),
                  pl.BlockSpec((1, tn), lambda j: (0, j))],
        out_specs=pl.BlockSpec((rows, tn), lambda j: (0, j)),
        out_shape=jax.ShapeDtypeStruct((rows, n), F32),
        compiler_params=_cparams(1),
        name="adaln",
    )(c_t, w_ada, b_ada)


def _ffn_kernel(*refs, mod_rows, with_mix, mix_gate_row, with_final):
    it = iter(refs)
    x_ref = next(it)
    mod_ref = next(it)
    nw_ref = next(it)
    win_ref = next(it)
    wo_ref = next(it)
    if with_mix:
        ma_ref = next(it)
        mb_ref = next(it)
        wmix_ref = next(it)
    if with_final:
        fnw_ref = next(it)
    o_ref = next(it)

    sh_row, sc_row, g_row = mod_rows
    x = x_ref[...]
    if with_mix:
        mix_t = jnp.concatenate([ma_ref[0].astype(BF16), mb_ref[0].astype(BF16)], axis=0)
        x = x + mod_ref[0, mix_gate_row:mix_gate_row + 1, :] * _dot_tn(mix_t, wmix_ref[...])
    h = _norm_mod(x, nw_ref[...], mod_ref[0, sh_row:sh_row + 1, :], mod_ref[0, sc_row:sc_row + 1, :])
    hb = h.astype(BF16)
    d_ff = wo_ref.shape[0]
    n_chunks = d_ff // FF_CHUNK
    acc = None
    for j in range(n_chunks):
        c0 = j * FF_CHUNK
        g = _dot(hb, win_ref[:, c0:c0 + FF_CHUNK])
        u = _dot(hb, win_ref[:, d_ff + c0:d_ff + c0 + FF_CHUNK])
        act = (_silu(g) * u).astype(BF16)
        part = _dot(act, wo_ref[c0:c0 + FF_CHUNK, :])
        acc = part if acc is None else acc + part
    y = x + 0.5 * mod_ref[0, g_row:g_row + 1, :] * acc
    if with_final:
        ms = jnp.mean(y * y, axis=-1, keepdims=True)
        y = y * lax.rsqrt(ms + EPS) * fnw_ref[...]
    o_ref[...] = y


def _ffn(x2d, mod, nw, w_in_bf, w_out_bf, mod_rows, tiles_per_batch, mix=None, mix_gate_row=None, final_nw=None):
    t, d = x2d.shape
    tm = TOKEN_TILE
    d_ff = w_out_bf.shape[0]
    assert d_ff % FF_CHUNK == 0
    row = lambda i: (i, 0)
    const = lambda i: (0, 0)
    in_specs = [pl.BlockSpec((tm, d), row),
                pl.BlockSpec((1, N_MOD, d), lambda i: (i // tiles_per_batch, 0, 0)),
                pl.BlockSpec((1, d), const),
                pl.BlockSpec(w_in_bf.shape, const),
                pl.BlockSpec(w_out_bf.shape, const)]
    args = [x2d, mod, nw, w_in_bf, w_out_bf]
    if mix is not None:
        ma, mb, wmix = mix
        trow = lambda i: (i // tiles_per_batch, 0, i % tiles_per_batch)
        in_specs += [pl.BlockSpec((1, ma.shape[1], tm), trow),
                     pl.BlockSpec((1, mb.shape[1], tm), trow),
                     pl.BlockSpec(wmix.shape, const)]
        args += [ma, mb, wmix]
    if final_nw is not None:
        in_specs.append(pl.BlockSpec((1, d), const))
        args.append(final_nw)
    kern = functools.partial(_ffn_kernel, mod_rows=mod_rows, with_mix=mix is not None,
                             mix_gate_row=mix_gate_row, with_final=final_nw is not None)
    return pl.pallas_call(
        kern,
        grid=(t // tm,),
        in_specs=in_specs,
        out_specs=pl.BlockSpec((tm, d), row),
        out_shape=jax.ShapeDtypeStruct((t, d), F32),
        compiler_params=_cparams(1),
        name="ffn_mix" if mix is not None else "ffn",
    )(*args)


QPAD_W = NSA_Q_HEADS * LANES
MISC_I = N_GATES
MISC_F = N_GATES + MLSTM_HEADS
HALF_BLOCK_W = CMP_STRIDE * KV_WIDTH
B_KV = NSA_WIDTH
B_ML = B_KV + 6 * KV_WIDTH
B_MISC = B_ML + 4 * MLSTM_WIDTH


def _vt_with_ones(v):
    vt = v.T
    n = vt.shape[1]
    pad = jnp.where(lax.broadcasted_iota(jnp.int32, (VT_ROWS - NSA_HEAD_DIM, n), 0) == 0, 1.0, 0.0)
    parts = []
    for h in range(NSA_KV_HEADS):
        parts += [vt[h * NSA_HEAD_DIM:(h + 1) * NSA_HEAD_DIM, :], pad]
    return jnp.concatenate(parts, axis=0).astype(BF16)


def _inproj_kernel(x_ref, mod_ref, nw_ref, wq_ref, wkv_ref, wml_ref, wmisc_ref, b_ref,
                   qpad_ref, kc_ref, vc_ref, ksel_ref, kwin_ref, vtsel_ref, vtwin_ref,
                   ml_ref, vtm_ref, ogt_ref, misc_ref, misct_ref, rows_ref):
    tm = x_ref.shape[0]
    h = _norm_mod(x_ref[...], nw_ref[...], mod_ref[0, 3:4, :], mod_ref[0, 4:5, :])
    hb = h.astype(BF16)

    def proj(w_ref, c0, width, b0):
        return _dot(hb, w_ref[:, c0:c0 + width]) + b_ref[:, b0 + c0:b0 + c0 + width]

    def half_blocks(val, out_ref):
        rows_ref[...] = val
        pieces = [rows_ref[pl.ds(l, tm // CMP_STRIDE, stride=CMP_STRIDE), :] for l in range(CMP_STRIDE)]
        out_ref[0] = jnp.concatenate(pieces, axis=1).astype(BF16)

    scale = NSA_HEAD_DIM ** -0.5 * LOG2E
    qt = (proj(wq_ref, 0, NSA_WIDTH, 0) * scale).T.astype(BF16)
    zero_rows = jnp.zeros((NSA_HEAD_DIM, tm), BF16)
    for i in range(NSA_Q_HEADS):
        slots = [zero_rows] * NSA_KV_HEADS
        slots[i // NSA_GROUP] = qt[i * NSA_HEAD_DIM:(i + 1) * NSA_HEAD_DIM, :]
        qpad_ref[0, i * LANES:(i + 1) * LANES, :] = jnp.concatenate(slots, axis=0)
    kv_c = proj(wkv_ref, 0, 2 * KV_WIDTH, B_KV)
    half_blocks(kv_c[:, :KV_WIDTH], kc_ref)
    half_blocks(kv_c[:, KV_WIDTH:], vc_ref)
    kv_s = proj(wkv_ref, 2 * KV_WIDTH, 2 * KV_WIDTH, B_KV)
    ksel_ref[...] = kv_s[:, :KV_WIDTH].astype(BF16)
    vtsel_ref[0] = _vt_with_ones(kv_s[:, KV_WIDTH:])
    kv_w = proj(wkv_ref, 4 * KV_WIDTH, 2 * KV_WIDTH, B_KV)
    kwin_ref[...] = kv_w[:, :KV_WIDTH].astype(BF16)
    vtwin_ref[0] = _vt_with_ones(kv_w[:, KV_WIDTH:])
    for i in range(2):
        ml_ref[:, i * MLSTM_WIDTH:(i + 1) * MLSTM_WIDTH] = proj(wml_ref, i * MLSTM_WIDTH, MLSTM_WIDTH, B_ML)
    vtm_ref[0] = proj(wml_ref, 2 * MLSTM_WIDTH, MLSTM_WIDTH, B_ML).T.astype(BF16)
    ogt_ref[0] = proj(wml_ref, 3 * MLSTM_WIDTH, MLSTM_WIDTH, B_ML).T
    misc = proj(wmisc_ref, 0, LANES, B_MISC)
    misc_ref[...] = misc
    misct_ref[0] = misc.T


def _inproj(x2d, mod, nw, weights, b_pack, batch, seq):
    t, d = x2d.shape
    tm = TOKEN_TILE
    tpb = seq // tm
    row = lambda i: (i, 0)
    const = lambda i: (0, 0)
    trow = lambda i: (i // tpb, 0, i % tpb)
    hrow = lambda i: (i // tpb, i % tpb, 0)
    vt_rows = NSA_KV_HEADS * VT_ROWS
    n_half = seq // CMP_STRIDE
    out_shapes = [
        jax.ShapeDtypeStruct((batch, QPAD_W, seq), BF16),
        jax.ShapeDtypeStruct((batch, n_half, HALF_BLOCK_W), BF16),
        jax.ShapeDtypeStruct((batch, n_half, HALF_BLOCK_W), BF16),
        jax.ShapeDtypeStruct((t, KV_WIDTH), BF16),
        jax.ShapeDtypeStruct((t, KV_WIDTH), BF16),
        jax.ShapeDtypeStruct((batch, vt_rows, seq), BF16),
        jax.ShapeDtypeStruct((batch, vt_rows, seq), BF16),
        jax.ShapeDtypeStruct((t, 2 * MLSTM_WIDTH), F32),
        jax.ShapeDtypeStruct((batch, MLSTM_WIDTH, seq), BF16),
        jax.ShapeDtypeStruct((batch, MLSTM_WIDTH, seq), F32),
        jax.ShapeDtypeStruct((t, LANES), F32),
        jax.ShapeDtypeStruct((batch, LANES, seq), F32),
    ]
    out_specs = [
        pl.BlockSpec((1, QPAD_W, tm), trow),
        pl.BlockSpec((1, tm // CMP_STRIDE, HALF_BLOCK_W), hrow),
        pl.BlockSpec((1, tm // CMP_STRIDE, HALF_BLOCK_W), hrow),
        pl.BlockSpec((tm, KV_WIDTH), row),
        pl.BlockSpec((tm, KV_WIDTH), row),
        pl.BlockSpec((1, vt_rows, tm), trow),
        pl.BlockSpec((1, vt_rows, tm), trow),
        pl.BlockSpec((tm, 2 * MLSTM_WIDTH), row),
        pl.BlockSpec((1, MLSTM_WIDTH, tm), trow),
        pl.BlockSpec((1, MLSTM_WIDTH, tm), trow),
        pl.BlockSpec((tm, LANES), row),
        pl.BlockSpec((1, LANES, tm), trow),
    ]
    return pl.pallas_call(
        _inproj_kernel,
        grid=(t // tm,),
        in_specs=[pl.BlockSpec((tm, d), row),
                  pl.BlockSpec((1, N_MOD, d), lambda i: (i // tpb, 0, 0)),
                  pl.BlockSpec((1, d), const)]
                 + [pl.BlockSpec(w.shape, const) for w in weights]
                 + [pl.BlockSpec(b_pack.shape, const)],
        out_specs=out_specs,
        out_shape=out_shapes,
        scratch_shapes=[pltpu.VMEM((tm, KV_WIDTH), F32)],
        compiler_params=_cparams(1),
        name="inproj",
    )(x2d, mod, nw, *weights, b_pack)


def _split_inproj(w_in, b_in):
    d = w_in.shape[0]
    offs = np.cumsum([0, NSA_WIDTH] + [KV_WIDTH] * 6 + [N_GATES] + [MLSTM_WIDTH] * 4 + [MLSTM_HEADS] * 2)
    g0, m0, i0 = int(offs[7]), int(offs[8]), int(offs[12])
    pad = LANES - N_GATES - 2 * MLSTM_HEADS
    misc_w = jnp.concatenate([w_in[:, g0:g0 + N_GATES], w_in[:, i0:i0 + 2 * MLSTM_HEADS],
                              jnp.zeros((d, pad), w_in.dtype)], axis=1)
    misc_b = jnp.concatenate([b_in[g0:g0 + N_GATES], b_in[i0:i0 + 2 * MLSTM_HEADS], jnp.zeros((pad,), b_in.dtype)])
    weights = [w_in[:, :NSA_WIDTH].astype(BF16), w_in[:, NSA_WIDTH:g0].astype(BF16),
               w_in[:, m0:i0].astype(BF16), misc_w.astype(BF16)]
    b_pack = jnp.concatenate([b_in[:g0], b_in[m0:i0], misc_b]).reshape(1, -1).astype(F32)
    assert b_pack.shape[1] == B_MISC + LANES
    return weights, b_pack


def _compress_kernel(xk_ref, xv_ref, wa_ref, wb_ref, pe_ref, w1_ref, w2_ref, kc_ref, vtc_ref):
    n_half = xk_ref.shape[1]

    def one(x_ref, s):
        x = x_ref[0]
        a = _dot(x, wa_ref[s])
        b = _dot(x, wb_ref[s])
        pe_term = jnp.dot(pe_ref[s], w1_ref[s], preferred_element_type=F32, precision=HIGHEST)[0:1, :]
        pe2 = jnp.concatenate([pe_term] * NSA_KV_HEADS, axis=1)
        pre = a + pltpu.roll(b, n_half - 1, 0) + pe2
        hid = 0.5 * pre * (1.0 + jnp.tanh(0.7978845608028654 * (pre + 0.044715 * pre * pre * pre)))
        return _dot(hid.astype(BF16), w2_ref[s])

    kc_ref[0] = one(xk_ref, 0).astype(BF16)
    vtc_ref[0] = _vt_with_ones(one(xv_ref, 1))


def _compress(xk, xv, wa, wb, pe8, w1, w2e):
    batch, n_half, width = xk.shape
    bsel = lambda b: (b, 0, 0)
    c3 = lambda b: (0, 0, 0)
    vt_rows = NSA_KV_HEADS * VT_ROWS
    return pl.pallas_call(
        _compress_kernel,
        grid=(batch,),
        in_specs=[pl.BlockSpec((1, n_half, width), bsel), pl.BlockSpec((1, n_half, width), bsel),
                  pl.BlockSpec(wa.shape, c3), pl.BlockSpec(wb.shape, c3),
                  pl.BlockSpec(pe8.shape, c3), pl.BlockSpec(w1.shape, c3), pl.BlockSpec(w2e.shape, c3)],
        out_specs=[pl.BlockSpec((1, n_half, KV_WIDTH), bsel), pl.BlockSpec((1, vt_rows, n_half), bsel)],
        out_shape=[jax.ShapeDtypeStruct((batch, n_half, KV_WIDTH), BF16),
                   jax.ShapeDtypeStruct((batch, vt_rows, n_half), BF16)],
        compiler_params=_cparams(1),
        name="compress",
    )(xk, xv, wa, wb, pe8, w1, w2e)


def _pack_compress(pe, w1, w2):
    half = CMP_BLOCK // 2
    w1r = w1.reshape(CMP_BLOCK, NSA_HEAD_DIM, CMP_HIDDEN)
    eye = jnp.eye(NSA_KV_HEADS, dtype=w1.dtype)

    def expand(w_half):
        z = w_half[:, None, :, None, :] * eye[None, :, None, :, None]
        return z.reshape(half * KV_WIDTH, NSA_KV_HEADS * CMP_HIDDEN)

    wa = expand(w1r[:half]).astype(BF16)
    wb = expand(w1r[half:]).astype(BF16)
    w2e = (w2[None, :, None, :] * eye[:, None, :, None]).reshape(NSA_KV_HEADS * CMP_HIDDEN, KV_WIDTH).astype(BF16)
    pe8 = jnp.concatenate([pe.reshape(1, CMP_BLOCK * NSA_HEAD_DIM),
                           jnp.zeros((7, CMP_BLOCK * NSA_HEAD_DIM), pe.dtype)], axis=0)
    return wa, wb, pe8, w1, w2e


def _q_aug(q_ref, qs_ref, h):
    qt = jnp.concatenate(
        [q_ref[0, (NSA_GROUP * h + g) * LANES:(NSA_GROUP * h + g + 1) * LANES, :] for g in range(NSA_GROUP)], axis=1)
    return jnp.concatenate([qt, qs_ref[h]], axis=0)


def _slope2(head):
    return float(2.0 ** (-8.0 * (head + 1) / NSA_Q_HEADS)) * LOG2E


def _bf16_split3(x):
    parts = []
    for _ in range(3):
        p = float(np.asarray(x, np.float32).astype(BF16).astype(np.float32))
        parts.append(p)
        x = x - p
    return parts


def _init_alibi_operands(qs_ref, kpos_ref, key_stride):
    lane = lax.broadcasted_iota(jnp.int32, (QT, LANES), 1)
    row = lax.broadcasted_iota(jnp.int32, (QT, LANES), 0)
    kpos_ref[...] = jnp.where(lane < 3, row.astype(F32), 0.0).astype(BF16)
    srow = lax.broadcasted_iota(jnp.int32, (LANES, QT), 0)
    for h in range(NSA_KV_HEADS):
        for g in range(NSA_GROUP):
            hi, mid, lo = _bf16_split3(key_stride * _slope2(NSA_GROUP * h + g))
            blk = jnp.where(srow == 0, hi, jnp.where(srow == 1, mid, jnp.where(srow == 2, lo, 0.0)))
            qs_ref[h, :, g * QT:(g + 1) * QT] = blk.astype(BF16)


def _gate_row(gt_ref, head, branch):
    r = head * 3 + branch
    return _sigmoid(gt_ref[0, r:r + 1, :])


def _softmax_jobs(jobs):
    scores = [_dot(jnp.concatenate(job[1], axis=0), job[0]) for job in jobs]
    staged = []
    for (q_aug, k_tiles, masks, offs, value_groups, state), st_all in zip(jobs, scores):
        n_t = len(k_tiles)
        per_g = []
        for g in range(NSA_GROUP):
            sl = slice(g * QT, (g + 1) * QT)
            ss = []
            for u in range(n_t):
                s = st_all[u * QT:(u + 1) * QT, sl]
                ss.append(s if masks[u] is None else jnp.where(masks[u], s, NEG))
            mx = None
            for u in range(n_t):
                cand = jnp.max(ss[u], axis=0, keepdims=True) + offs[u][g]
                mx = cand if mx is None else jnp.maximum(mx, cand)
            alpha = None
            if state is None:
                m_new = mx
            else:
                m_ref, _, h = state
                m_old = m_ref[h, :, sl]
                m_new = jnp.maximum(m_old, mx)
                alpha = jnp.exp2(m_old - m_new)
                m_ref[h, :, sl] = m_new
            m_use = jnp.where(m_new < 0.5 * NEG, 0.0, m_new)
            per_g.append((ss, m_use, alpha))
        staged.append(per_g)
    ets = []
    for (q_aug, k_tiles, masks, offs, value_groups, state), per_g in zip(jobs, staged):
        n_t = len(k_tiles)
        rows = [jnp.concatenate([jnp.exp2(per_g[g][0][u] - (per_g[g][1] - offs[u][g])).astype(BF16)
                                 for g in range(NSA_GROUP)], axis=1) for u in range(n_t)]
        ets.append(jnp.concatenate(rows, axis=0))
    results = []
    for (q_aug, k_tiles, masks, offs, value_groups, state), per_g, et in zip(jobs, staged, ets):
        mats = [jnp.concatenate(grp, axis=1) for grp in value_groups]
        pv_all = _dot(jnp.concatenate(mats, axis=0), et)
        pvs, r0 = [], 0
        for mat in mats:
            pvs.append(pv_all[r0:r0 + mat.shape[0], :])
            r0 += mat.shape[0]
        if state is None:
            results.append(pvs)
        else:
            _, acc_ref, h = state
            acc_ref[h] = jnp.concatenate([pg[2] for pg in per_g], axis=1) * acc_ref[h] + pvs[0]
            results.append(None)
    return results


def _gated(gt_ref, h, branch, pv):
    inv = 1.0 / jnp.maximum(pv[NSA_HEAD_DIM:NSA_HEAD_DIM + 1, :], 1e-30)
    out = []
    for g in range(NSA_GROUP):
        sl = slice(g * QT, (g + 1) * QT)
        out.append(pv[0:NSA_HEAD_DIM, sl] * (inv[:, sl] * _gate_row(gt_ref, NSA_GROUP * h + g, branch)))
    return out


def _nsa_cmp_kernel(q_ref, kc_ref, vtc_ref, gt_ref, oc_ref, sel_ref, cnt_ref,
                    qs_ref, kpos_ref, ovt_ref, pvo_ref, pvi_ref):
    ncp = kc_ref.shape[1]
    n_ct = ncp // QT
    n_sel = sel_ref.shape[3]
    start = pl.program_id(1) * QT
    tile_span = QT * CMP_STRIDE

    @pl.when((pl.program_id(0) == 0) & (pl.program_id(1) == 0))
    def _():
        _init_alibi_operands(qs_ref, kpos_ref, float(CMP_STRIDE))
        j_o = lax.broadcasted_iota(jnp.int32, (n_sel, ncp), 0)
        c_o = lax.broadcasted_iota(jnp.int32, (n_sel, ncp), 1)
        ov = ((c_o * CMP_STRIDE <= j_o * SEL_BLOCK + (SEL_BLOCK - 1))
              & (c_o * CMP_STRIDE + (CMP_BLOCK - 1) >= j_o * SEL_BLOCK))
        ovt_ref[...] = jnp.where(ov, 1.0, 0.0).astype(BF16)

    a0 = (lax.broadcasted_iota(jnp.int32, (QT, QT), 1)
          - CMP_STRIDE * lax.broadcasted_iota(jnp.int32, (QT, QT), 0))
    q_augs = [_q_aug(q_ref, qs_ref, h) for h in range(NSA_KV_HEADS)]
    n_vis = jnp.minimum((start + QT - CMP_BLOCK) // tile_span + 1, n_ct)

    for k in range(1, n_ct + 1):
        @pl.when(n_vis == k)
        def _(k=k):
            jobs = []
            for h in range(NSA_KV_HEADS):
                k_tiles, masks, offs, vts, ovs = [], [], [], [], []
                for ci in range(k):
                    cs = slice(ci * QT, (ci + 1) * QT)
                    first_end = ci * tile_span + CMP_BLOCK - 1
                    k_tiles.append(jnp.concatenate([kc_ref[0, cs, :], kpos_ref[...]], axis=1))
                    masks.append(a0 + (start - first_end) >= 0)
                    offs.append([_slope2(NSA_GROUP * h + g) * first_end for g in range(NSA_GROUP)])
                    vts.append(vtc_ref[0, h * VT_ROWS:(h + 1) * VT_ROWS, cs])
                    ovs.append(ovt_ref[:, cs])
                jobs.append((q_augs[h], k_tiles, masks, offs, [vts, ovs], None))
            for h, (pvo, pvi) in enumerate(_softmax_jobs(jobs)):
                pvo_ref[h] = pvo
                pvi_ref[h] = pvi

    j_io = lax.broadcasted_iota(jnp.int32, (n_sel, QT), 0)
    t_j = start + lax.broadcasted_iota(jnp.int32, (n_sel, QT), 1)
    cur = t_j // SEL_BLOCK
    valid = j_io * SEL_BLOCK <= t_j
    forced = (j_io == 0) | (j_io == cur) | (j_io == cur - 1)
    ones8 = jnp.ones((8, QT), BF16)

    taken = -(2.0 ** 126)
    scores = []
    for h in range(NSA_KV_HEADS):
        pvo = pvo_ref[h]
        o_c = _gated(gt_ref, h, 0, pvo)
        inv = 1.0 / jnp.maximum(pvo[NSA_HEAD_DIM:NSA_HEAD_DIM + 1, :], 1e-30)
        imp = None
        for g in range(NSA_GROUP):
            sl = slice(g * QT, (g + 1) * QT)
            oc_ref[0, NSA_GROUP * h + g] = o_c[g]
            term = pvi_ref[h, :, sl] * inv[:, sl]
            imp = term if imp is None else imp + term
        scores.append(jnp.where(forced, taken, jnp.where(valid, imp, -FORCE_BONUS)))

    n_cls = n_sel // TOPK_ROWS
    cls = jnp.minimum((start + QT + SEL_BLOCK * TOPK_ROWS - 1) // (SEL_BLOCK * TOPK_ROWS), n_cls)

    for k in range(1, n_cls + 1):
        @pl.when(cls == k)
        def _(rows=k * TOPK_ROWS):
            j_sub = j_io[:rows]

            def pick(_, carry):
                out = []
                for sc in carry:
                    mx = jnp.max(sc, axis=0, keepdims=True)
                    idx = jnp.min(jnp.where(sc == mx, j_sub, n_sel), axis=0, keepdims=True)
                    out.append(jnp.where(j_sub == idx, taken, sc))
                return tuple(out)

            picked = lax.fori_loop(0, SEL_TOPK - N_FORCED, pick, tuple(sc[:rows] for sc in scores))
            for h in range(NSA_KV_HEADS):
                sel = jnp.where(picked[h] <= 0.5 * taken, 1.0, 0.0)
                if rows < n_sel:
                    sel = jnp.concatenate([sel, jnp.zeros((n_sel - rows, QT), F32)], axis=0)
                sel_ref[0, h, 0] = sel
                cnt = _dot_nt(ones8, sel.astype(BF16))
                cnt_ref[0, 0, h] = cnt[0:1, :]


def _nsa_cmp(qpad3, kc, vtc, gates_t, batch, seq):
    nb = seq // QT
    n_sel = seq // SEL_BLOCK
    ncp = kc.shape[1]
    assert ncp % QT == 0
    vt_rows = NSA_KV_HEADS * VT_ROWS
    return pl.pallas_call(
        _nsa_cmp_kernel,
        grid=(batch, nb),
        in_specs=[pl.BlockSpec((1, QPAD_W, QT), lambda b, i: (b, 0, i)),
                  pl.BlockSpec((1, ncp, KV_WIDTH), lambda b, i: (b, 0, 0)),
                  pl.BlockSpec((1, vt_rows, ncp), lambda b, i: (b, 0, 0)),
                  pl.BlockSpec((1, N_GATES, QT), lambda b, i: (b, 0, i))],
        out_specs=[pl.BlockSpec((1, NSA_Q_HEADS, NSA_HEAD_DIM, QT), lambda b, i: (b, 0, 0, i)),
                   pl.BlockSpec((1, NSA_KV_HEADS, 1, n_sel, QT), lambda b, i: (b, 0, i, 0, 0)),
                   pl.BlockSpec((1, 1, NSA_KV_HEADS, 1, n_sel), lambda b, i: (b, i, 0, 0, 0))],
        out_shape=[jax.ShapeDtypeStruct((batch, NSA_Q_HEADS, NSA_HEAD_DIM, seq), F32),
                   jax.ShapeDtypeStruct((batch, NSA_KV_HEADS, nb, n_sel, QT), F32),
                   jax.ShapeDtypeStruct((batch, nb, NSA_KV_HEADS, 1, n_sel), F32)],
        scratch_shapes=[pltpu.VMEM((NSA_KV_HEADS, LANES, NSA_GROUP * QT), BF16),
                        pltpu.VMEM((QT, LANES), BF16),
                        pltpu.VMEM((n_sel, ncp), BF16),
                        pltpu.VMEM((NSA_KV_HEADS, VT_ROWS, NSA_GROUP * QT), F32),
                        pltpu.VMEM((NSA_KV_HEADS, n_sel, NSA_GROUP * QT), F32)],
        compiler_params=_cparams(2),
        name="nsa_cmp",
    )(qpad3, kc, vtc, gates_t)


def _nsa_sel_kernel(count_ref, list_ref, q_ref, ksel_ref, vtsel_ref, kwin_ref, vtwin_ref, selm_ref, gt_ref,
                    oc_ref, o_ref, m_sel, acc_sel, qs_ref, kpos_ref, part_ref):
    b = pl.program_id(0)
    qb = pl.program_id(1)
    nb = pl.num_programs(1)
    heads = range(NSA_KV_HEADS)
    k_io = lax.broadcasted_iota(jnp.int32, (QT, QT), 0)
    q_io = lax.broadcasted_iota(jnp.int32, (QT, QT), 1)
    causal = q_io >= k_io
    in_window = q_io < k_io
    first_half = k_io < SEL_BLOCK

    @pl.when((b == 0) & (qb == 0))
    def _():
        _init_alibi_operands(qs_ref, kpos_ref, 1.0)

    def k_aug(k_ref, t):
        off = pl.multiple_of(t * QT, QT)
        return jnp.concatenate([k_ref[0, pl.ds(off, QT), :], kpos_ref[...]], axis=1)

    def v_ext(vt_ref, h, t):
        return vt_ref[0, h * VT_ROWS:(h + 1) * VT_ROWS, pl.ds(pl.multiple_of(t * QT, QT), QT)]

    def tile_offs(h, t, live=None):
        pos = (t * QT).astype(F32)
        out = [_slope2(NSA_GROUP * h + g) * pos for g in range(NSA_GROUP)]
        return out if live is None else [jnp.where(live, o, NEG) for o in out]

    def chosen(h, t):
        mrow = selm_ref[0, h, 0, pl.ds(2 * t, 2), :]
        return jnp.where(first_half, mrow[0:1, :], mrow[1:2, :]) > 0.5

    q_augs = [_q_aug(q_ref, qs_ref, h) for h in heads]

    slot = [(b * nb + qb) * NSA_KV_HEADS + h for h in heads]
    n_tiles = [count_ref[slot[h]] for h in heads]

    m_sel[...] = jnp.full(m_sel.shape, NEG, F32)
    acc_sel[...] = jnp.zeros(acc_sel.shape, F32)

    def sel_jobs(trip, first):
        jobs = []
        for step in range(SEL_CHAIN):
            for h in heads:
                k_tiles, masks, offs, vts = [], [], [], []
                for u in range(SEL_UNROLL):
                    if first and step == 0 and u == 0:
                        t, mask, off = qb, chosen(h, qb) & causal, tile_offs(h, qb)
                    else:
                        idx = (trip * SEL_CHAIN + step) * SEL_UNROLL + u - 1
                        live = idx < n_tiles[h]
                        t = jnp.where(live, list_ref[slot[h] * nb + jnp.minimum(idx, nb - 1)], 0)
                        mask, off = chosen(h, t), tile_offs(h, t, live)
                    k_tiles.append(k_aug(ksel_ref, t))
                    masks.append(mask)
                    offs.append(off)
                    vts.append(v_ext(vtsel_ref, h, t))
                jobs.append((q_augs[h], k_tiles, masks, offs, [vts], (m_sel, acc_sel, h)))
        return jobs

    n_win = WINDOW // QT + 1
    win_jobs = []
    for h in heads:
        k_tiles, masks, offs, vts = [], [], [], []
        for u in range(n_win):
            raw = qb - (n_win - 1) + u
            t = jnp.maximum(raw, 0)
            k_tiles.append(k_aug(kwin_ref, t))
            masks.append(in_window if u == 0 else causal if u == n_win - 1 else None)
            offs.append(tile_offs(h, t, raw >= 0))
            vts.append(v_ext(vtwin_ref, h, t))
        win_jobs.append((q_augs[h], k_tiles, masks, offs, [vts], None))
    first = sel_jobs(0, True)
    n_heads = NSA_KV_HEADS
    results = _softmax_jobs(first[:n_heads] + win_jobs + first[n_heads:])
    for h in heads:
        o_w = _gated(gt_ref, h, 2, results[n_heads + h][0])
        for g in range(NSA_GROUP):
            head = NSA_GROUP * h + g
            part_ref[head] = oc_ref[0, head] + o_w[g]

    def sel_body(i, carry):
        _softmax_jobs(sel_jobs(i, False))
        return carry

    per_trip = SEL_UNROLL * SEL_CHAIN
    n_trips = (functools.reduce(jnp.maximum, n_tiles) + per_trip) // per_trip
    lax.fori_loop(1, n_trips, sel_body, 0)

    for h in heads:
        o_s = _gated(gt_ref, h, 1, acc_sel[h])
        for g in range(NSA_GROUP):
            head = NSA_GROUP * h + g
            o_ref[0, head] = part_ref[head] + o_s[g]


def _tile_lists(cnt, batch, nb):
    pair = cnt.reshape(batch, nb, NSA_KV_HEADS, nb, 2).sum(axis=-1)
    p_io = lax.broadcasted_iota(jnp.int32, pair.shape, 3)
    flag = (pair > 0.5) & (p_io < lax.broadcasted_iota(jnp.int32, pair.shape, 1))
    rank = jnp.cumsum(flag.astype(jnp.int32), axis=-1) - 1
    slot_io = lax.broadcasted_iota(jnp.int32, pair.shape + (nb,), 4)
    hit = flag[..., None] & (rank[..., None] == slot_io)
    lists = jnp.sum(jnp.where(hit, p_io[..., None], 0), axis=3)
    counts = jnp.sum(flag.astype(jnp.int32), axis=-1)
    return counts.reshape(-1), lists.reshape(-1)


def _nsa_sel(counts, lists, qpad3, ksel, vtsel, kwin, vtwin, selm, gates_t, oc, batch, seq):
    nb = seq // QT
    res = lambda b, i, c, l: (b, 0, 0)
    vt_rows = NSA_KV_HEADS * VT_ROWS
    grid_spec = pltpu.PrefetchScalarGridSpec(
        num_scalar_prefetch=2,
        grid=(batch, nb),
        in_specs=[pl.BlockSpec((1, QPAD_W, QT), lambda b, i, c, l: (b, 0, i)),
                  pl.BlockSpec((1, seq, KV_WIDTH), res),
                  pl.BlockSpec((1, vt_rows, seq), res),
                  pl.BlockSpec((1, seq, KV_WIDTH), res),
                  pl.BlockSpec((1, vt_rows, seq), res),
                  pl.BlockSpec((1, NSA_KV_HEADS, 1, 2 * nb, QT), lambda b, i, c, l: (b, 0, i, 0, 0)),
                  pl.BlockSpec((1, N_GATES, QT), lambda b, i, c, l: (b, 0, i)),
                  pl.BlockSpec((1, NSA_Q_HEADS, NSA_HEAD_DIM, QT), lambda b, i, c, l: (b, 0, 0, i))],
        out_specs=pl.BlockSpec((1, NSA_Q_HEADS, NSA_HEAD_DIM, QT), lambda b, i, c, l: (b, 0, 0, i)),
        scratch_shapes=[pltpu.VMEM((NSA_KV_HEADS, 1, NSA_GROUP * QT), F32),
                        pltpu.VMEM((NSA_KV_HEADS, VT_ROWS, NSA_GROUP * QT), F32),
                        pltpu.VMEM((NSA_KV_HEADS, LANES, NSA_GROUP * QT), BF16),
                        pltpu.VMEM((QT, LANES), BF16),
                        pltpu.VMEM((NSA_Q_HEADS, NSA_HEAD_DIM, QT), F32)],
    )
    return pl.pallas_call(
        _nsa_sel_kernel,
        grid_spec=grid_spec,
        out_shape=jax.ShapeDtypeStruct((batch, NSA_Q_HEADS, NSA_HEAD_DIM, seq), F32),
        compiler_params=_cparams(2),
        name="nsa_sel",
    )(counts, lists, qpad3, ksel, vtsel, kwin, vtwin, selm, gates_t, oc)


def _mlstm_kernel(q_ref, k_ref, vt_ref, ogt_ref, misc_ref, misct_ref, cw_ref, cb_ref, nw_ref,
                  o_ref, xq_ref, xk_ref, c_ref, n_ref, m_ref):
    tm = q_ref.shape[0]
    L = MLSTM_L
    hd = MLSTM_HEAD_DIM
    halo = 8

    @pl.when(pl.program_id(1) == 0)
    def _():
        xq_ref[0:halo, :] = jnp.zeros((halo, MLSTM_WIDTH), F32)
        xk_ref[0:halo, :] = jnp.zeros((halo, MLSTM_WIDTH), F32)
        c_ref[...] = jnp.zeros_like(c_ref)
        n_ref[...] = jnp.zeros_like(n_ref)
        m_ref[...] = jnp.zeros_like(m_ref)

    def conv(x_ref, buf_ref, col0):
        buf_ref[halo:halo + tm, :] = x_ref[...]
        y = cb_ref[:, col0:col0 + MLSTM_WIDTH]
        for kk in range(CONV_WIDTH):
            r0 = halo - (CONV_WIDTH - 1) + kk
            y = y + cw_ref[kk:kk + 1, col0:col0 + MLSTM_WIDTH] * buf_ref[r0:r0 + tm, :]
        tail = buf_ref[tm:tm + halo, :]
        buf_ref[0:halo, :] = tail
        return _silu(y)

    qc = conv(q_ref, xq_ref, 0)
    kc = conv(k_ref, xk_ref, MLSTM_WIDTH) * (hd ** -0.5)

    misc = misc_ref[...]
    misct = misct_ref[0]
    lf_c = -jnp.log(1.0 + jnp.exp(-misc))
    lf_r = -jnp.log(1.0 + jnp.exp(-misct))
    row_io = lax.broadcasted_iota(jnp.int32, (L, L), 0)
    col_io = lax.broadcasted_iota(jnp.int32, (L, L), 1)
    tril = jnp.where(col_io <= row_io, 1.0, 0.0).astype(BF16)
    triu = jnp.where(row_io <= col_io, 1.0, 0.0).astype(BF16)

    def split3(x):
        hi = x.astype(BF16)
        r1 = x - hi.astype(F32)
        mid = r1.astype(BF16)
        return hi, mid, (r1 - mid.astype(F32)).astype(BF16)

    lf_c3 = split3(lf_c)
    lf_r3 = split3(lf_r)
    b_c = jnp.concatenate([sum(_dot(tril, p[ci * L:(ci + 1) * L, :]) for p in lf_c3)
                           for ci in range(tm // L)], axis=0)
    b_r = jnp.concatenate([sum(_dot(p[:, ci * L:(ci + 1) * L], triu) for p in lf_r3)
                           for ci in range(tm // L)], axis=1)

    tri_t = lax.broadcasted_iota(jnp.int32, (L, L), 0) <= lax.broadcasted_iota(jnp.int32, (L, L), 1)
    nw_cols = [jnp.broadcast_to(nw_ref[:, h * hd:(h + 1) * hd], (hd, hd)).T for h in range(MLSTM_HEADS)]

    def split3_rows(row):
        hi = row.astype(BF16)
        r1 = row - hi.astype(F32)
        mid = r1.astype(BF16)
        lo = (r1 - mid.astype(F32)).astype(BF16)
        sub = lax.broadcasted_iota(jnp.int32, (8, row.shape[1]), 0)
        out = jnp.where(sub == 0, hi.astype(F32), jnp.where(sub == 1, mid.astype(F32),
                                                            jnp.where(sub == 2, lo.astype(F32), 0.0)))
        return out.astype(BF16)

    def rows_sum3(x8):
        return x8[0:1, :] + x8[1:2, :] + x8[2:3, :]

    for ci in range(tm // L):
        r0 = ci * L
        for h in range(MLSTM_HEADS):
            cs = slice(h * hd, (h + 1) * hd)
            qb16 = qc[r0:r0 + L, cs].astype(BF16)
            kb16 = kc[r0:r0 + L, cs].astype(BF16)
            vt = vt_ref[0, cs, r0:r0 + L]
            a_col = (misc[r0:r0 + L, MISC_I + h:MISC_I + h + 1]
                     - b_c[r0:r0 + L, MISC_F + h:MISC_F + h + 1])
            brow = b_r[MISC_F + h:MISC_F + h + 1, r0:r0 + L]
            lirow = misct[MISC_I + h:MISC_I + h + 1, r0:r0 + L]
            m_prev = m_ref[h]
            ct = c_ref[h]
            nrow = n_ref[h]

            dmat = jnp.where(tri_t, brow + a_col, NEG)
            m_inter = brow + m_prev
            m_t = jnp.maximum(jnp.max(dmat, axis=0, keepdims=True), m_inter)
            w = jnp.exp(dmat - m_t) * _dot_nt(kb16, qb16)
            decay = jnp.exp(m_inter - m_t)
            num = _dot(vt, w.astype(BF16)) + decay * _dot_nt(ct.astype(BF16), qb16)
            nq = rows_sum3(_dot_nt(split3_rows(nrow), qb16))
            den = jnp.sum(w, axis=0, keepdims=True) + decay * nq
            hout = num * (1.0 / jnp.maximum(jnp.abs(den), jnp.exp(-m_t)))

            b_last = brow[:, L - 1:L]
            grow = b_last - brow + lirow
            m_new = jnp.maximum(b_last + m_prev, jnp.max(grow, axis=1, keepdims=True))
            wk = jnp.exp(grow - m_new)
            d_c = jnp.exp(b_last + m_prev - m_new)
            c_ref[h] = d_c * ct + _dot((vt.astype(F32) * wk).astype(BF16), kb16)
            n_ref[h] = d_c * nrow + rows_sum3(_dot(split3_rows(wk), kb16))
            m_ref[h] = m_new

            hn = hout * lax.rsqrt(jnp.mean(hout * hout, axis=0, keepdims=True) + EPS) * nw_cols[h]
            o_ref[0, cs, r0:r0 + L] = (hn * _sigmoid(ogt_ref[0, cs, r0:r0 + L])).astype(o_ref.dtype)


def _mlstm(ml, vt_m, og_t, misc, misc_t, conv_w, conv_b, norm_w, batch, seq):
    tm = TOKEN_TILE
    tpb = seq // tm
    col = lambda j: (lambda b, i: (b * tpb + i, j))
    const = lambda b, i: (0, 0)
    trow = lambda b, i: (b, 0, i)
    return pl.pallas_call(
        _mlstm_kernel,
        grid=(batch, tpb),
        in_specs=[pl.BlockSpec((tm, MLSTM_WIDTH), col(0)),
                  pl.BlockSpec((tm, MLSTM_WIDTH), col(1)),
                  pl.BlockSpec((1, MLSTM_WIDTH, tm), trow),
                  pl.BlockSpec((1, MLSTM_WIDTH, tm), trow),
                  pl.BlockSpec((tm, LANES), col(0)),
                  pl.BlockSpec((1, LANES, tm), trow),
                  pl.BlockSpec(conv_w.shape, const),
                  pl.BlockSpec(conv_b.shape, const),
                  pl.BlockSpec(norm_w.shape, const)],
        out_specs=pl.BlockSpec((1, MLSTM_WIDTH, tm), trow),
        out_shape=jax.ShapeDtypeStruct((batch, MLSTM_WIDTH, seq), BF16),
        scratch_shapes=[pltpu.VMEM((tm + 8, MLSTM_WIDTH), F32),
                        pltpu.VMEM((tm + 8, MLSTM_WIDTH), F32),
                        pltpu.VMEM((MLSTM_HEADS, MLSTM_HEAD_DIM, MLSTM_HEAD_DIM), F32),
                        pltpu.VMEM((MLSTM_HEADS, 1, MLSTM_HEAD_DIM), F32),
                        pltpu.VMEM((MLSTM_HEADS, 1, 1), F32)],
        compiler_params=_cparams(2),
        name="mlstm",
    )(ml, ml, vt_m, og_t, misc, misc_t, conv_w, conv_b, norm_w)


def _layer(x2d, c, batch, seq, w_ada, b_ada, norm_ffn1_w, ffn1_w_in, ffn1_w_out, norm_mix_w, w_in, b_in,
           cmp_k_pe, cmp_k_w1, cmp_k_w2, cmp_v_pe, cmp_v_w1, cmp_v_w2, conv_w, conv_b, mlstm_norm_w, w_out,
           norm_ffn2_w, ffn2_w_in, ffn2_w_out, final_nw):
    d = x2d.shape[1]
    tpb = seq // TOKEN_TILE
    mod = _adaln(c.T, w_ada, b_ada.reshape(1, -1)).reshape(batch, N_MOD, d)

    x1 = _ffn(x2d, mod, norm_ffn1_w.reshape(1, d), ffn1_w_in.astype(BF16), ffn1_w_out.astype(BF16),
              (0, 1, 2), tpb)

    weights, b_pack = _split_inproj(w_in, b_in)
    (qpad, kc, vc, ksel, kwin, vtsel, vtwin, ml, vt_m, og_t, misc, misc_t) = _inproj(
        x1, mod, norm_mix_w.reshape(1, d), weights, b_pack, batch, seq)

    pk = _pack_compress(cmp_k_pe, cmp_k_w1, cmp_k_w2)
    pv = _pack_compress(cmp_v_pe, cmp_v_w1, cmp_v_w2)
    stacked = [jnp.stack([a, bb]) for a, bb in zip(pk, pv)]
    kcmp, vtcmp = _compress(kc, vc, *stacked)

    nb = seq // QT
    qpad3 = qpad
    gates_t = misc_t
    oc, sel, cnt = _nsa_cmp(qpad3, kcmp, vtcmp, gates_t, batch, seq)
    counts, lists = _tile_lists(cnt, batch, nb)
    o_t = _nsa_sel(counts, lists, qpad3, ksel.reshape(batch, seq, KV_WIDTH), vtsel, kwin.reshape(batch, seq, KV_WIDTH),
                   vtwin, sel, gates_t, oc, batch, seq)
    o_nsa = o_t.reshape(batch, NSA_WIDTH, seq)

    o_ml = _mlstm(ml, vt_m, og_t, misc, misc_t, conv_w, conv_b.reshape(1, -1), mlstm_norm_w.reshape(1, -1), batch, seq)

    return _ffn(x1, mod, norm_ffn2_w.reshape(1, d), ffn2_w_in.astype(BF16), ffn2_w_out.astype(BF16),
                (6, 7, 8), tpb, mix=(o_nsa, o_ml, w_out.astype(BF16)), mix_gate_row=5, final_nw=final_nw)


def kernel(x, c, w_ada, b_ada, norm_ffn1_w, ffn1_w_in, ffn1_w_out, norm_mix_w, w_in, b_in, cmp_k_pe, cmp_k_w1, cmp_k_w2, cmp_v_pe, cmp_v_w1, cmp_v_w2, conv_w, conv_b, mlstm_norm_w, w_out, norm_ffn2_w, ffn2_w_in, ffn2_w_out, final_norm_w):
    batch, seq, d = x.shape
    depth = w_ada.shape[0]
    assert depth == 1 and seq % (QT * CMP_STRIDE) == 0 and seq // SEL_BLOCK >= SEL_TOPK
    y = _layer(x.reshape(batch * seq, d), c, batch, seq, w_ada[0], b_ada[0], norm_ffn1_w[0], ffn1_w_in[0],
               ffn1_w_out[0], norm_mix_w[0], w_in[0], b_in[0], cmp_k_pe[0], cmp_k_w1[0], cmp_k_w2[0],
               cmp_v_pe[0], cmp_v_w1[0], cmp_v_w2[0], conv_w[0], conv_b[0], mlstm_norm_w[0], w_out[0],
               norm_ffn2_w[0], ffn2_w_in[0], ffn2_w_out[0], final_norm_w.reshape(1, d))
    return y.reshape(batch, seq, d)
```

```python
import functools

import numpy as np
import jax
import jax.numpy as jnp
from jax import lax
from jax.experimental import pallas as pl
from jax.experimental.pallas import tpu as pltpu

NSA_Q_HEADS = 8
NSA_KV_HEADS = 2
NSA_GROUP = NSA_Q_HEADS // NSA_KV_HEADS
NSA_HEAD_DIM = 64
CMP_BLOCK = 32
CMP_STRIDE = 16
CMP_HIDDEN = 128
SEL_BLOCK = 64
SEL_TOPK = 16
WINDOW = 512
FORCE_BONUS = 1.0e4
N_FORCED = 3
assert FORCE_BONUS > NSA_GROUP
MLSTM_HEADS = 4
MLSTM_HEAD_DIM = 128
CONV_WIDTH = 4
D_FF = 2816
N_MOD = 9
EPS = 1e-6
NEG = -1e30

NSA_WIDTH = NSA_Q_HEADS * NSA_HEAD_DIM
KV_WIDTH = NSA_KV_HEADS * NSA_HEAD_DIM
MLSTM_WIDTH = MLSTM_HEADS * MLSTM_HEAD_DIM
N_GATES = 3 * NSA_Q_HEADS

LANES = 128
QT = 128
TOKEN_TILE = 512
FF_CHUNK = 256
MLSTM_L = 128
SEL_UNROLL = 6
SEL_CHAIN = 2
CMP_TILES = 2
TOPK_ROWS = 32
VT_ROWS = 80
VMEM_LIMIT = 56 * 1024 * 1024

F32 = jnp.float32
BF16 = jnp.bfloat16
HIGHEST = lax.Precision.HIGHEST
LOG2E = 1.4426950408889634


def _cparams(n_axes):
    return pltpu.CompilerParams(dimension_semantics=("arbitrary",) * n_axes,
                                vmem_limit_bytes=VMEM_LIMIT)


def _dot(a, b):
    return jnp.dot(a, b, preferred_element_type=F32)


def _dot_nt(a, b):
    return lax.dot_general(a, b, (((1,), (1,)), ((), ())), preferred_element_type=F32)


def _dot_tn(a, b):
    return lax.dot_general(a, b, (((0,), (0,)), ((), ())), preferred_element_type=F32)


def _sigmoid(x):
    return 1.0 / (1.0 + jnp.exp(-x))


def _silu(x):
    return x * _sigmoid(x)


def _norm_mod(x, nw, sh, sc):
    ms = jnp.mean(x * x, axis=-1, keepdims=True)
    y = x * lax.rsqrt(ms + EPS) * nw
    return y * (1.0 + sc) + sh


def _adaln_kernel(ct_ref, w_ref, b_ref, o_ref):
    w = w_ref[...]
    for r in range(ct_ref.shape[1]):
        a_col = _silu(ct_ref[:, r:r + 1])
        o_ref[r:r + 1, :] = jnp.sum(a_col * w, axis=0, keepdims=True) + b_ref[...]


def _adaln(c_t, w_ada, b_ada):
    d, rows = c_t.shape
    n = w_ada.shape[1]
    tn = n // N_MOD
    return pl.pallas_call(
        _adaln_kernel,
        grid=(N_MOD,),
        in_specs=[pl.BlockSpec((d, rows), lambda j: (0, 0)),
                  pl.BlockSpec((d, tn), lambda j: (0, j)),
                  pl.BlockSpec((1, tn), lambda j: (0, j))],
        out_specs=pl.BlockSpec((rows, tn), lambda j: (0, j)),
        out_shape=jax.ShapeDtypeStruct((rows, n), F32),
        compiler_params=_cparams(1),
        name="adaln",
    )(c_t, w_ada, b_ada)


def _ffn_kernel(*refs, mod_rows, with_mix, mix_gate_row, with_final):
    it = iter(refs)
    x_ref = next(it)
    mod_ref = next(it)
    nw_ref = next(it)
    win_ref = next(it)
    wo_ref = next(it)
    if with_mix:
        ma_ref = next(it)
        mb_ref = next(it)
        wmix_ref = next(it)
    if with_final:
        fnw_ref = next(it)
    o_ref = next(it)

    sh_row, sc_row, g_row = mod_rows
    x = x_ref[...]
    if with_mix:
        mix_t = jnp.concatenate([ma_ref[0].astype(BF16), mb_ref[0].astype(BF16)], axis=0)
        x = x + mod_ref[0, mix_gate_row:mix_gate_row + 1, :] * _dot_tn(mix_t, wmix_ref[...])
    h = _norm_mod(x, nw_ref[...], mod_ref[0, sh_row:sh_row + 1, :], mod_ref[0, sc_row:sc_row + 1, :])
    hb = h.astype(BF16)
    d_ff = wo_ref.shape[0]
    n_chunks = d_ff // FF_CHUNK
    acc = None
    for j in range(n_chunks):
        c0 = j * FF_CHUNK
        g = _dot(hb, win_ref[:, c0:c0 + FF_CHUNK])
        u = _dot(hb, win_ref[:, d_ff + c0:d_ff + c0 + FF_CHUNK])
        act = (_silu(g) * u).astype(BF16)
        part = _dot(act, wo_ref[c0:c0 + FF_CHUNK, :])
        acc = part if acc is None else acc + part
    y = x + 0.5 * mod_ref[0, g_row:g_row + 1, :] * acc
    if with_final:
        ms = jnp.mean(y * y, axis=-1, keepdims=True)
        y = y * lax.rsqrt(ms + EPS) * fnw_ref[...]
    o_ref[...] = y


def _ffn(x2d, mod, nw, w_in_bf, w_out_bf, mod_rows, tiles_per_batch, mix=None, mix_gate_row=None, final_nw=None):
    t, d = x2d.shape
    tm = TOKEN_TILE
    d_ff = w_out_bf.shape[0]
    assert d_ff % FF_CHUNK == 0
    row = lambda i: (i, 0)
    const = lambda i: (0, 0)
    in_specs = [pl.BlockSpec((tm, d), row),
                pl.BlockSpec((1, N_MOD, d), lambda i: (i // tiles_per_batch, 0, 0)),
                pl.BlockSpec((1, d), const),
                pl.BlockSpec(w_in_bf.shape, const),
                pl.BlockSpec(w_out_bf.shape, const)]
    args = [x2d, mod, nw, w_in_bf, w_out_bf]
    if mix is not None:
        ma, mb, wmix = mix
        trow = lambda i: (i // tiles_per_batch, 0, i % tiles_per_batch)
        in_specs += [pl.BlockSpec((1, ma.shape[1], tm), trow),
                     pl.BlockSpec((1, mb.shape[1], tm), trow),
                     pl.BlockSpec(wmix.shape, const)]
        args += [ma, mb, wmix]
    if final_nw is not None:
        in_specs.append(pl.BlockSpec((1, d), const))
        args.append(final_nw)
    kern = functools.partial(_ffn_kernel, mod_rows=mod_rows, with_mix=mix is not None,
                             mix_gate_row=mix_gate_row, with_final=final_nw is not None)
    return pl.pallas_call(
        kern,
        grid=(t // tm,),
        in_specs=in_specs,
        out_specs=pl.BlockSpec((tm, d), row),
        out_shape=jax.ShapeDtypeStruct((t, d), F32),
        compiler_params=_cparams(1),
        name="ffn_mix" if mix is not None else "ffn",
    )(*args)


QPAD_W = NSA_Q_HEADS * LANES
MISC_I = N_GATES
MISC_F = N_GATES + MLSTM_HEADS
HALF_BLOCK_W = CMP_STRIDE * KV_WIDTH
B_KV = NSA_WIDTH
B_ML = B_KV + 6 * KV_WIDTH
B_MISC = B_ML + 4 * MLSTM_WIDTH


def _vt_with_ones(v):
    vt = v.T
    n = vt.shape[1]
    pad = jnp.where(lax.broadcasted_iota(jnp.int32, (VT_ROWS - NSA_HEAD_DIM, n), 0) == 0, 1.0, 0.0)
    parts = []
    for h in range(NSA_KV_HEADS):
        parts += [vt[h * NSA_HEAD_DIM:(h + 1) * NSA_HEAD_DIM, :], pad]
    return jnp.concatenate(parts, axis=0).astype(BF16)


def _inproj_kernel(x_ref, mod_ref, nw_ref, wq_ref, wkv_ref, wml_ref, wmisc_ref, b_ref,
                   qpad_ref, kc_ref, vc_ref, ksel_ref, kwin_ref, vtsel_ref, vtwin_ref,
                   ml_ref, vtm_ref, ogt_ref, misc_ref, misct_ref, rows_ref):
    tm = x_ref.shape[0]
    h = _norm_mod(x_ref[...], nw_ref[...], mod_ref[0, 3:4, :], mod_ref[0, 4:5, :])
    hb = h.astype(BF16)

    def proj(w_ref, c0, width, b0):
        return _dot(hb, w_ref[:, c0:c0 + width]) + b_ref[:, b0 + c0:b0 + c0 + width]

    def half_blocks(val, out_ref):
        rows_ref[...] = val
        pieces = [rows_ref[pl.ds(l, tm // CMP_STRIDE, stride=CMP_STRIDE), :] for l in range(CMP_STRIDE)]
        out_ref[0] = jnp.concatenate(pieces, axis=1).astype(BF16)

    scale = NSA_HEAD_DIM ** -0.5 * LOG2E
    qt = (proj(wq_ref, 0, NSA_WIDTH, 0) * scale).T.astype(BF16)
    zero_rows = jnp.zeros((NSA_HEAD_DIM, tm), BF16)
    for i in range(NSA_Q_HEADS):
        slots = [zero_rows] * NSA_KV_HEADS
        slots[i // NSA_GROUP] = qt[i * NSA_HEAD_DIM:(i + 1) * NSA_HEAD_DIM, :]
        qpad_ref[0, i * LANES:(i + 1) * LANES, :] = jnp.concatenate(slots, axis=0)
    kv_c = proj(wkv_ref, 0, 2 * KV_WIDTH, B_KV)
    half_blocks(kv_c[:, :KV_WIDTH], kc_ref)
    half_blocks(kv_c[:, KV_WIDTH:], vc_ref)
    kv_s = proj(wkv_ref, 2 * KV_WIDTH, 2 * KV_WIDTH, B_KV)
    ksel_ref[...] = kv_s[:, :KV_WIDTH].astype(BF16)
    vtsel_ref[0] = _vt_with_ones(kv_s[:, KV_WIDTH:])
    kv_w = proj(wkv_ref, 4 * KV_WIDTH, 2 * KV_WIDTH, B_KV)
    kwin_ref[...] = kv_w[:, :KV_WIDTH].astype(BF16)
    vtwin_ref[0] = _vt_with_ones(kv_w[:, KV_WIDTH:])
    for i in range(2):
        ml_ref[:, i * MLSTM_WIDTH:(i + 1) * MLSTM_WIDTH] = proj(wml_ref, i * MLSTM_WIDTH, MLSTM_WIDTH, B_ML)
    vtm_ref[0] = proj(wml_ref, 2 * MLSTM_WIDTH, MLSTM_WIDTH, B_ML).T.astype(BF16)
    ogt_ref[0] = proj(wml_ref, 3 * MLSTM_WIDTH, MLSTM_WIDTH, B_ML).T
    misc = proj(wmisc_ref, 0, LANES, B_MISC)
    misc_ref[...] = misc
    misct_ref[0] = misc.T


def _inproj(x2d, mod, nw, weights, b_pack, batch, seq):
    t, d = x2d.shape
    tm = TOKEN_TILE
    tpb = seq // tm
    row = lambda i: (i, 0)
    const = lambda i: (0, 0)
    trow = lambda i: (i // tpb, 0, i % tpb)
    hrow = lambda i: (i // tpb, i % tpb, 0)
    vt_rows = NSA_KV_HEADS * VT_ROWS
    n_half = seq // CMP_STRIDE
    out_shapes = [
        jax.ShapeDtypeStruct((batch, QPAD_W, seq), BF16),
        jax.ShapeDtypeStruct((batch, n_half, HALF_BLOCK_W), BF16),
        jax.ShapeDtypeStruct((batch, n_half, HALF_BLOCK_W), BF16),
        jax.ShapeDtypeStruct((t, KV_WIDTH), BF16),
        jax.ShapeDtypeStruct((t, KV_WIDTH), BF16),
        jax.ShapeDtypeStruct((batch, vt_rows, seq), BF16),
        jax.ShapeDtypeStruct((batch, vt_rows, seq), BF16),
        jax.ShapeDtypeStruct((t, 2 * MLSTM_WIDTH), F32),
        jax.ShapeDtypeStruct((batch, MLSTM_WIDTH, seq), BF16),
        jax.ShapeDtypeStruct((batch, MLSTM_WIDTH, seq), F32),
        jax.ShapeDtypeStruct((t, LANES), F32),
        jax.ShapeDtypeStruct((batch, LANES, seq), F32),
    ]
    out_specs = [
        pl.BlockSpec((1, QPAD_W, tm), trow),
        pl.BlockSpec((1, tm // CMP_STRIDE, HALF_BLOCK_W), hrow),
        pl.BlockSpec((1, tm // CMP_STRIDE, HALF_BLOCK_W), hrow),
        pl.BlockSpec((tm, KV_WIDTH), row),
        pl.BlockSpec((tm, KV_WIDTH), row),
        pl.BlockSpec((1, vt_rows, tm), trow),
        pl.BlockSpec((1, vt_rows, tm), trow),
        pl.BlockSpec((tm, 2 * MLSTM_WIDTH), row),
        pl.BlockSpec((1, MLSTM_WIDTH, tm), trow),
        pl.BlockSpec((1, MLSTM_WIDTH, tm), trow),
        pl.BlockSpec((tm, LANES), row),
        pl.BlockSpec((1, LANES, tm), trow),
    ]
    return pl.pallas_call(
        _inproj_kernel,
        grid=(t // tm,),
        in_specs=[pl.BlockSpec((tm, d), row),
                  pl.BlockSpec((1, N_MOD, d), lambda i: (i // tpb, 0, 0)),
                  pl.BlockSpec((1, d), const)]
                 + [pl.BlockSpec(w.shape, const) for w in weights]
                 + [pl.BlockSpec(b_pack.shape, const)],
        out_specs=out_specs,
        out_shape=out_shapes,
        scratch_shapes=[pltpu.VMEM((tm, KV_WIDTH), F32)],
        compiler_params=_cparams(1),
        name="inproj",
    )(x2d, mod, nw, *weights, b_pack)


def _split_inproj(w_in, b_in):
    d = w_in.shape[0]
    offs = np.cumsum([0, NSA_WIDTH] + [KV_WIDTH] * 6 + [N_GATES] + [MLSTM_WIDTH] * 4 + [MLSTM_HEADS] * 2)
    g0, m0, i0 = int(offs[7]), int(offs[8]), int(offs[12])
    pad = LANES - N_GATES - 2 * MLSTM_HEADS
    misc_w = jnp.concatenate([w_in[:, g0:g0 + N_GATES], w_in[:, i0:i0 + 2 * MLSTM_HEADS],
                              jnp.zeros((d, pad), w_in.dtype)], axis=1)
    misc_b = jnp.concatenate([b_in[g0:g0 + N_GATES], b_in[i0:i0 + 2 * MLSTM_HEADS], jnp.zeros((pad,), b_in.dtype)])
    weights = [w_in[:, :NSA_WIDTH].astype(BF16), w_in[:, NSA_WIDTH:g0].astype(BF16),
               w_in[:, m0:i0].astype(BF16), misc_w.astype(BF16)]
    b_pack = jnp.concatenate([b_in[:g0], b_in[m0:i0], misc_b]).reshape(1, -1).astype(F32)
    assert b_pack.shape[1] == B_MISC + LANES
    return weights, b_pack


def _compress_kernel(xk_ref, xv_ref, wa_ref, wb_ref, pe_ref, w1_ref, w2_ref, kc_ref, vtc_ref):
    n_half = xk_ref.shape[1]

    def one(x_ref, s):
        x = x_ref[0]
        a = _dot(x, wa_ref[s])
        b = _dot(x, wb_ref[s])
        pe_term = jnp.dot(pe_ref[s], w1_ref[s], preferred_element_type=F32, precision=HIGHEST)[0:1, :]
        pe2 = jnp.concatenate([pe_term] * NSA_KV_HEADS, axis=1)
        pre = a + pltpu.roll(b, n_half - 1, 0) + pe2
        hid = 0.5 * pre * (1.0 + jnp.tanh(0.7978845608028654 * (pre + 0.044715 * pre * pre * pre)))
        return _dot(hid.astype(BF16), w2_ref[s])

    kc_ref[0] = one(xk_ref, 0).astype(BF16)
    vtc_ref[0] = _vt_with_ones(one(xv_ref, 1))


def _compress(xk, xv, wa, wb, pe8, w1, w2e):
    batch, n_half, width = xk.shape
    bsel = lambda b: (b, 0, 0)
    c3 = lambda b: (0, 0, 0)
    vt_rows = NSA_KV_HEADS * VT_ROWS
    return pl.pallas_call(
        _compress_kernel,
        grid=(batch,),
        in_specs=[pl.BlockSpec((1, n_half, width), bsel), pl.BlockSpec((1, n_half, width), bsel),
                  pl.BlockSpec(wa.shape, c3), pl.BlockSpec(wb.shape, c3),
                  pl.BlockSpec(pe8.shape, c3), pl.BlockSpec(w1.shape, c3), pl.BlockSpec(w2e.shape, c3)],
        out_specs=[pl.BlockSpec((1, n_half, KV_WIDTH), bsel), pl.BlockSpec((1, vt_rows, n_half), bsel)],
        out_shape=[jax.ShapeDtypeStruct((batch, n_half, KV_WIDTH), BF16),
                   jax.ShapeDtypeStruct((batch, vt_rows, n_half), BF16)],
        compiler_params=_cparams(1),
        name="compress",
    )(xk, xv, wa, wb, pe8, w1, w2e)


def _pack_compress(pe, w1, w2):
    half = CMP_BLOCK // 2
    w1r = w1.reshape(CMP_BLOCK, NSA_HEAD_DIM, CMP_HIDDEN)
    eye = jnp.eye(NSA_KV_HEADS, dtype=w1.dtype)

    def expand(w_half):
        z = w_half[:, None, :, None, :] * eye[None, :, None, :, None]
        return z.reshape(half * KV_WIDTH, NSA_KV_HEADS * CMP_HIDDEN)

    wa = expand(w1r[:half]).astype(BF16)
    wb = expand(w1r[half:]).astype(BF16)
    w2e = (w2[None, :, None, :] * eye[:, None, :, None]).reshape(NSA_KV_HEADS * CMP_HIDDEN, KV_WIDTH).astype(BF16)
    pe8 = jnp.concatenate([pe.reshape(1, CMP_BLOCK * NSA_HEAD_DIM),
                           jnp.zeros((7, CMP_BLOCK * NSA_HEAD_DIM), pe.dtype)], axis=0)
    return wa, wb, pe8, w1, w2e


def _q_aug(q_ref, qs_ref, h, cols=slice(0, QT)):
    qt = jnp.concatenate(
        [q_ref[0, (NSA_GROUP * h + g) * LANES:(NSA_GROUP * h + g + 1) * LANES, cols] for g in range(NSA_GROUP)],
        axis=1)
    return jnp.concatenate([qt, qs_ref[h]], axis=0)


def _slope2(head):
    return float(2.0 ** (-8.0 * (head + 1) / NSA_Q_HEADS)) * LOG2E


def _bf16_split3(x):
    parts = []
    for _ in range(3):
        p = float(np.asarray(x, np.float32).astype(BF16).astype(np.float32))
        parts.append(p)
        x = x - p
    return parts


def _init_alibi_operands(qs_ref, kpos_ref, key_stride):
    lane = lax.broadcasted_iota(jnp.int32, (QT, LANES), 1)
    row = lax.broadcasted_iota(jnp.int32, (QT, LANES), 0)
    kpos_ref[...] = jnp.where(lane < 3, row.astype(F32), 0.0).astype(BF16)
    srow = lax.broadcasted_iota(jnp.int32, (LANES, QT), 0)
    for h in range(NSA_KV_HEADS):
        for g in range(NSA_GROUP):
            hi, mid, lo = _bf16_split3(key_stride * _slope2(NSA_GROUP * h + g))
            blk = jnp.where(srow == 0, hi, jnp.where(srow == 1, mid, jnp.where(srow == 2, lo, 0.0)))
            qs_ref[h, :, g * QT:(g + 1) * QT] = blk.astype(BF16)


def _gate_row(gt_ref, head, branch, cols):
    r = head * 3 + branch
    return _sigmoid(gt_ref[0, r:r + 1, cols])


def _softmax_jobs(jobs):
    scores = [_dot(jnp.concatenate(job[1], axis=0), job[0]) for job in jobs]
    staged = []
    for (q_aug, k_tiles, masks, offs, value_groups, state), st_all in zip(jobs, scores):
        n_t = len(k_tiles)
        per_g = []
        for g in range(NSA_GROUP):
            sl = slice(g * QT, (g + 1) * QT)
            ss = []
            for u in range(n_t):
                s = st_all[u * QT:(u + 1) * QT, sl]
                ss.append(s if masks[u] is None else jnp.where(masks[u], s, NEG))
            mx = None
            for u in range(n_t):
                cand = jnp.max(ss[u], axis=0, keepdims=True) + offs[u][g]
                mx = cand if mx is None else jnp.maximum(mx, cand)
            alpha = None
            if state is None:
                m_new = mx
            else:
                m_ref, _, h = state
                m_old = m_ref[h, :, sl]
                m_new = jnp.maximum(m_old, mx)
                alpha = jnp.exp2(m_old - m_new)
                m_ref[h, :, sl] = m_new
            m_use = jnp.where(m_new < 0.5 * NEG, 0.0, m_new)
            per_g.append((ss, m_use, alpha))
        staged.append(per_g)
    ets = []
    for (q_aug, k_tiles, masks, offs, value_groups, state), per_g in zip(jobs, staged):
        n_t = len(k_tiles)
        rows = [jnp.concatenate([jnp.exp2(per_g[g][0][u] - (per_g[g][1] - offs[u][g])).astype(BF16)
                                 for g in range(NSA_GROUP)], axis=1) for u in range(n_t)]
        ets.append(jnp.concatenate(rows, axis=0))
    results = []
    for (q_aug, k_tiles, masks, offs, value_groups, state), per_g, et in zip(jobs, staged, ets):
        mats = [jnp.concatenate(grp, axis=1) for grp in value_groups]
        pv_all = _dot(jnp.concatenate(mats, axis=0), et)
        pvs, r0 = [], 0
        for mat in mats:
            pvs.append(pv_all[r0:r0 + mat.shape[0], :])
            r0 += mat.shape[0]
        if state is None:
            results.append(pvs)
        else:
            _, acc_ref, h = state
            acc_ref[h] = jnp.concatenate([pg[2] for pg in per_g], axis=1) * acc_ref[h] + pvs[0]
            results.append(None)
    return results


def _gated(gt_ref, h, branch, pv, cols=slice(0, QT)):
    inv = 1.0 / jnp.maximum(pv[NSA_HEAD_DIM:NSA_HEAD_DIM + 1, :], 1e-30)
    out = []
    for g in range(NSA_GROUP):
        sl = slice(g * QT, (g + 1) * QT)
        out.append(pv[0:NSA_HEAD_DIM, sl] * (inv[:, sl] * _gate_row(gt_ref, NSA_GROUP * h + g, branch, cols)))
    return out


def _nsa_cmp_kernel(q_ref, kc_ref, vtc_ref, gt_ref, oc_ref, sel_ref, cnt_ref,
                    qs_ref, kpos_ref, ovt_ref, pvo_ref, pvi_ref):
    ncp = kc_ref.shape[1]
    n_ct = ncp // QT
    n_sel = sel_ref.shape[3]
    tiles = range(CMP_TILES)
    starts = [(pl.program_id(1) * CMP_TILES + s) * QT for s in tiles]
    cols = [slice(s * QT, (s + 1) * QT) for s in tiles]
    units = [(s, h) for s in tiles for h in range(NSA_KV_HEADS)]
    tile_span = QT * CMP_STRIDE

    @pl.when((pl.program_id(0) == 0) & (pl.program_id(1) == 0))
    def _():
        _init_alibi_operands(qs_ref, kpos_ref, float(CMP_STRIDE))
        j_o = lax.broadcasted_iota(jnp.int32, (n_sel, ncp), 0)
        c_o = lax.broadcasted_iota(jnp.int32, (n_sel, ncp), 1)
        ov = ((c_o * CMP_STRIDE <= j_o * SEL_BLOCK + (SEL_BLOCK - 1))
              & (c_o * CMP_STRIDE + (CMP_BLOCK - 1) >= j_o * SEL_BLOCK))
        ovt_ref[...] = jnp.where(ov, 1.0, 0.0).astype(BF16)

    a0 = (lax.broadcasted_iota(jnp.int32, (QT, QT), 1)
          - CMP_STRIDE * lax.broadcasted_iota(jnp.int32, (QT, QT), 0))
    q_augs = [_q_aug(q_ref, qs_ref, h, cols[s]) for s, h in units]
    n_vis = jnp.minimum((starts[-1] + QT - CMP_BLOCK) // tile_span + 1, n_ct)

    for k in range(1, n_ct + 1):
        @pl.when(n_vis == k)
        def _(k=k):
            jobs = []
            for i, (s, h) in enumerate(units):
                k_tiles, masks, offs, vts, ovs = [], [], [], [], []
                for ci in range(k):
                    cs = slice(ci * QT, (ci + 1) * QT)
                    first_end = ci * tile_span + CMP_BLOCK - 1
                    k_tiles.append(jnp.concatenate([kc_ref[0, cs, :], kpos_ref[...]], axis=1))
                    masks.append(a0 + (starts[s] - first_end) >= 0)
                    offs.append([_slope2(NSA_GROUP * h + g) * first_end for g in range(NSA_GROUP)])
                    vts.append(vtc_ref[0, h * VT_ROWS:(h + 1) * VT_ROWS, cs])
                    ovs.append(ovt_ref[:, cs])
                jobs.append((q_augs[i], k_tiles, masks, offs, [vts, ovs], None))
            for i, (pvo, pvi) in enumerate(_softmax_jobs(jobs)):
                pvo_ref[i] = pvo
                pvi_ref[i] = pvi

    j_io = lax.broadcasted_iota(jnp.int32, (n_sel, QT), 0)
    q_io = lax.broadcasted_iota(jnp.int32, (n_sel, QT), 1)
    ones8 = jnp.ones((8, QT), BF16)

    taken = -(2.0 ** 126)
    scores = []
    for i, (s, h) in enumerate(units):
        t_j = starts[s] + q_io
        cur = t_j // SEL_BLOCK
        valid = j_io * SEL_BLOCK <= t_j
        forced = (j_io == 0) | (j_io == cur) | (j_io == cur - 1)
        pvo = pvo_ref[i]
        o_c = _gated(gt_ref, h, 0, pvo, cols[s])
        inv = 1.0 / jnp.maximum(pvo[NSA_HEAD_DIM:NSA_HEAD_DIM + 1, :], 1e-30)
        imp = None
        for g in range(NSA_GROUP):
            sl = slice(g * QT, (g + 1) * QT)
            oc_ref[0, NSA_GROUP * h + g, :, cols[s]] = o_c[g]
            term = pvi_ref[i, :, sl] * inv[:, sl]
            imp = term if imp is None else imp + term
        scores.append(jnp.where(forced, taken, jnp.where(valid, imp, -FORCE_BONUS)))

    n_cls = n_sel // TOPK_ROWS
    cls = jnp.minimum((starts[-1] + QT + SEL_BLOCK * TOPK_ROWS - 1) // (SEL_BLOCK * TOPK_ROWS), n_cls)

    for k in range(1, n_cls + 1):
        @pl.when(cls == k)
        def _(rows=k * TOPK_ROWS):
            j_sub = j_io[:rows]

            def pick(_, carry):
                out = []
                for sc in carry:
                    mx = jnp.max(sc, axis=0, keepdims=True)
                    idx = jnp.min(jnp.where(sc == mx, j_sub, n_sel), axis=0, keepdims=True)
                    out.append(jnp.where(j_sub == idx, taken, sc))
                return tuple(out)

            picked = lax.fori_loop(0, SEL_TOPK - N_FORCED, pick, tuple(sc[:rows] for sc in scores))
            for i, (s, h) in enumerate(units):
                sel = jnp.where(picked[i] <= 0.5 * taken, 1.0, 0.0)
                if rows < n_sel:
                    sel = jnp.concatenate([sel, jnp.zeros((n_sel - rows, QT), F32)], axis=0)
                sel_ref[0, h, s] = sel
                cnt = _dot_nt(ones8, sel.astype(BF16))
                cnt_ref[0, s, h] = cnt[0:1, :]


def _nsa_cmp(qpad3, kc, vtc, gates_t, batch, seq):
    nb = seq // QT
    n_sel = seq // SEL_BLOCK
    ncp = kc.shape[1]
    assert ncp % QT == 0 and nb % CMP_TILES == 0
    vt_rows = NSA_KV_HEADS * VT_ROWS
    qw = QT * CMP_TILES
    n_units = CMP_TILES * NSA_KV_HEADS
    return pl.pallas_call(
        _nsa_cmp_kernel,
        grid=(batch, nb // CMP_TILES),
        in_specs=[pl.BlockSpec((1, QPAD_W, qw), lambda b, i: (b, 0, i)),
                  pl.BlockSpec((1, ncp, KV_WIDTH), lambda b, i: (b, 0, 0)),
                  pl.BlockSpec((1, vt_rows, ncp), lambda b, i: (b, 0, 0)),
                  pl.BlockSpec((1, N_GATES, qw), lambda b, i: (b, 0, i))],
        out_specs=[pl.BlockSpec((1, NSA_Q_HEADS, NSA_HEAD_DIM, qw), lambda b, i: (b, 0, 0, i)),
                   pl.BlockSpec((1, NSA_KV_HEADS, CMP_TILES, n_sel, QT), lambda b, i: (b, 0, i, 0, 0)),
                   pl.BlockSpec((1, CMP_TILES, NSA_KV_HEADS, 1, n_sel), lambda b, i: (b, i, 0, 0, 0))],
        out_shape=[jax.ShapeDtypeStruct((batch, NSA_Q_HEADS, NSA_HEAD_DIM, seq), F32),
                   jax.ShapeDtypeStruct((batch, NSA_KV_HEADS, nb, n_sel, QT), F32),
                   jax.ShapeDtypeStruct((batch, nb, NSA_KV_HEADS, 1, n_sel), F32)],
        scratch_shapes=[pltpu.VMEM((NSA_KV_HEADS, LANES, NSA_GROUP * QT), BF16),
                        pltpu.VMEM((QT, LANES), BF16),
                        pltpu.VMEM((n_sel, ncp), BF16),
                        pltpu.VMEM((n_units, VT_ROWS, NSA_GROUP * QT), F32),
                        pltpu.VMEM((n_units, n_sel, NSA_GROUP * QT), F32)],
        compiler_params=_cparams(2),
        name="nsa_cmp",
    )(qpad3, kc, vtc, gates_t)


def _nsa_sel_kernel(count_ref, list_ref, q_ref, ksel_ref, vtsel_ref, kwin_ref, vtwin_ref, selm_ref, gt_ref,
                    oc_ref, o_ref, m_sel, acc_sel, qs_ref, kpos_ref, part_ref):
    b = pl.program_id(0)
    qb = pl.program_id(1)
    nb = pl.num_programs(1)
    heads = range(NSA_KV_HEADS)
    k_io = lax.broadcasted_iota(jnp.int32, (QT, QT), 0)
    q_io = lax.broadcasted_iota(jnp.int32, (QT, QT), 1)
    causal = q_io >= k_io
    in_window = q_io < k_io
    first_half = k_io < SEL_BLOCK

    @pl.when((b == 0) & (qb == 0))
    def _():
        _init_alibi_operands(qs_ref, kpos_ref, 1.0)

    def k_aug(k_ref, t):
        off = pl.multiple_of(t * QT, QT)
        return jnp.concatenate([k_ref[0, pl.ds(off, QT), :], kpos_ref[...]], axis=1)

    def v_ext(vt_ref, h, t):
        return vt_ref[0, h * VT_ROWS:(h + 1) * VT_ROWS, pl.ds(pl.multiple_of(t * QT, QT), QT)]

    def tile_offs(h, t, live=None):
        pos = (t * QT).astype(F32)
        out = [_slope2(NSA_GROUP * h + g) * pos for g in range(NSA_GROUP)]
        return out if live is None else [jnp.where(live, o, NEG) for o in out]

    def chosen(h, t):
        mrow = selm_ref[0, h, 0, pl.ds(2 * t, 2), :]
        return jnp.where(first_half, mrow[0:1, :], mrow[1:2, :]) > 0.5

    q_augs = [_q_aug(q_ref, qs_ref, h) for h in heads]

    slot = [(b * nb + qb) * NSA_KV_HEADS + h for h in heads]
    n_tiles = [count_ref[slot[h]] for h in heads]

    m_sel[...] = jnp.full(m_sel.shape, NEG, F32)
    acc_sel[...] = jnp.zeros(acc_sel.shape, F32)

    def sel_jobs(trip, first):
        jobs = []
        for step in range(SEL_CHAIN):
            for h in heads:
                k_tiles, masks, offs, vts = [], [], [], []
                for u in range(SEL_UNROLL):
                    if first and step == 0 and u == 0:
                        t, mask, off = qb, chosen(h, qb) & causal, tile_offs(h, qb)
                    else:
                        idx = (trip * SEL_CHAIN + step) * SEL_UNROLL + u - 1
                        live = idx < n_tiles[h]
                        t = jnp.where(live, list_ref[slot[h] * nb + jnp.minimum(idx, nb - 1)], 0)
                        mask, off = chosen(h, t), tile_offs(h, t, live)
                    k_tiles.append(k_aug(ksel_ref, t))
                    masks.append(mask)
                    offs.append(off)
                    vts.append(v_ext(vtsel_ref, h, t))
                jobs.append((q_augs[h], k_tiles, masks, offs, [vts], (m_sel, acc_sel, h)))
        return jobs

    n_win = WINDOW // QT + 1
    win_jobs = []
    for h in heads:
        k_tiles, masks, offs, vts = [], [], [], []
        for u in range(n_win):
            raw = qb - (n_win - 1) + u
            t = jnp.maximum(raw, 0)
            k_tiles.append(k_aug(kwin_ref, t))
            masks.append(in_window if u == 0 else causal if u == n_win - 1 else None)
            offs.append(tile_offs(h, t, raw >= 0))
            vts.append(v_ext(vtwin_ref, h, t))
        win_jobs.append((q_augs[h], k_tiles, masks, offs, [vts], None))
    first = sel_jobs(0, True)
    n_heads = NSA_KV_HEADS
    results = _softmax_jobs(first[:n_heads] + win_jobs + first[n_heads:])
    for h in heads:
        o_w = _gated(gt_ref, h, 2, results[n_heads + h][0])
        for g in range(NSA_GROUP):
            head = NSA_GROUP * h + g
            part_ref[head] = oc_ref[0, head] + o_w[g]

    def sel_body(i, carry):
        _softmax_jobs(sel_jobs(i, False))
        return carry

    per_trip = SEL_UNROLL * SEL_CHAIN
    n_trips = (functools.reduce(jnp.maximum, n_tiles) + per_trip) // per_trip
    lax.fori_loop(1, n_trips, sel_body, 0)

    for h in heads:
        o_s = _gated(gt_ref, h, 1, acc_sel[h])
        for g in range(NSA_GROUP):
            head = NSA_GROUP * h + g
            o_ref[0, head] = part_ref[head] + o_s[g]


def _tile_lists(cnt, batch, nb):
    pair = cnt.reshape(batch, nb, NSA_KV_HEADS, nb, 2).sum(axis=-1)
    p_io = lax.broadcasted_iota(jnp.int32, pair.shape, 3)
    flag = (pair > 0.5) & (p_io < lax.broadcasted_iota(jnp.int32, pair.shape, 1))
    rank = jnp.cumsum(flag.astype(jnp.int32), axis=-1) - 1
    slot_io = lax.broadcasted_iota(jnp.int32, pair.shape + (nb,), 4)
    hit = flag[..., None] & (rank[..., None] == slot_io)
    lists = jnp.sum(jnp.where(hit, p_io[..., None], 0), axis=3)
    counts = jnp.sum(flag.astype(jnp.int32), axis=-1)
    return counts.reshape(-1), lists.reshape(-1)


def _nsa_sel(counts, lists, qpad3, ksel, vtsel, kwin, vtwin, selm, gates_t, oc, batch, seq):
    nb = seq // QT
    res = lambda b, i, c, l: (b, 0, 0)
    vt_rows = NSA_KV_HEADS * VT_ROWS
    grid_spec = pltpu.PrefetchScalarGridSpec(
        num_scalar_prefetch=2,
        grid=(batch, nb),
        in_specs=[pl.BlockSpec((1, QPAD_W, QT), lambda b, i, c, l: (b, 0, i)),
                  pl.BlockSpec((1, seq, KV_WIDTH), res),
                  pl.BlockSpec((1, vt_rows, seq), res),
                  pl.BlockSpec((1, seq, KV_WIDTH), res),
                  pl.BlockSpec((1, vt_rows, seq), res),
                  pl.BlockSpec((1, NSA_KV_HEADS, 1, 2 * nb, QT), lambda b, i, c, l: (b, 0, i, 0, 0)),
                  pl.BlockSpec((1, N_GATES, QT), lambda b, i, c, l: (b, 0, i)),
                  pl.BlockSpec((1, NSA_Q_HEADS, NSA_HEAD_DIM, QT), lambda b, i, c, l: (b, 0, 0, i))],
        out_specs=pl.BlockSpec((1, NSA_Q_HEADS, NSA_HEAD_DIM, QT), lambda b, i, c, l: (b, 0, 0, i)),
        scratch_shapes=[pltpu.VMEM((NSA_KV_HEADS, 1, NSA_GROUP * QT), F32),
                        pltpu.VMEM((NSA_KV_HEADS, VT_ROWS, NSA_GROUP * QT), F32),
                        pltpu.VMEM((NSA_KV_HEADS, LANES, NSA_GROUP * QT), BF16),
                        pltpu.VMEM((QT, LANES), BF16),
                        pltpu.VMEM((NSA_Q_HEADS, NSA_HEAD_DIM, QT), F32)],
    )
    return pl.pallas_call(
        _nsa_sel_kernel,
        grid_spec=grid_spec,
        out_shape=jax.ShapeDtypeStruct((batch, NSA_Q_HEADS, NSA_HEAD_DIM, seq), F32),
        compiler_params=_cparams(2),
        name="nsa_sel",
    )(counts, lists, qpad3, ksel, vtsel, kwin, vtwin, selm, gates_t, oc)


def _mlstm_kernel(q_ref, k_ref, vt_ref, ogt_ref, misc_ref, misct_ref, cw_ref, cb_ref, nw_ref,
                  o_ref, xq_ref, xk_ref, c_ref, n_ref, m_ref):
    tm = q_ref.shape[0]
    L = MLSTM_L
    hd = MLSTM_HEAD_DIM
    halo = 8

    @pl.when(pl.program_id(1) == 0)
    def _():
        xq_ref[0:halo, :] = jnp.zeros((halo, MLSTM_WIDTH), F32)
        xk_ref[0:halo, :] = jnp.zeros((halo, MLSTM_WIDTH), F32)
        c_ref[...] = jnp.zeros_like(c_ref)
        n_ref[...] = jnp.zeros_like(n_ref)
        m_ref[...] = jnp.zeros_like(m_ref)

    def conv(x_ref, buf_ref, col0):
        buf_ref[halo:halo + tm, :] = x_ref[...]
        y = cb_ref[:, col0:col0 + MLSTM_WIDTH]
        for kk in range(CONV_WIDTH):
            r0 = halo - (CONV_WIDTH - 1) + kk
            y = y + cw_ref[kk:kk + 1, col0:col0 + MLSTM_WIDTH] * buf_ref[r0:r0 + tm, :]
        tail = buf_ref[tm:tm + halo, :]
        buf_ref[0:halo, :] = tail
        return _silu(y)

    qc = conv(q_ref, xq_ref, 0)
    kc = conv(k_ref, xk_ref, MLSTM_WIDTH) * (hd ** -0.5)

    misc = misc_ref[...]
    misct = misct_ref[0]
    lf_c = -jnp.log(1.0 + jnp.exp(-misc))
    lf_r = -jnp.log(1.0 + jnp.exp(-misct))
    row_io = lax.broadcasted_iota(jnp.int32, (L, L), 0)
    col_io = lax.broadcasted_iota(jnp.int32, (L, L), 1)
    tril = jnp.where(col_io <= row_io, 1.0, 0.0).astype(BF16)
    triu = jnp.where(row_io <= col_io, 1.0, 0.0).astype(BF16)

    def split3(x):
        hi = x.astype(BF16)
        r1 = x - hi.astype(F32)
        mid = r1.astype(BF16)
        return hi, mid, (r1 - mid.astype(F32)).astype(BF16)

    lf_c3 = split3(lf_c)
    lf_r3 = split3(lf_r)
    b_c = jnp.concatenate([sum(_dot(tril, p[ci * L:(ci + 1) * L, :]) for p in lf_c3)
                           for ci in range(tm // L)], axis=0)
    b_r = jnp.concatenate([sum(_dot(p[:, ci * L:(ci + 1) * L], triu) for p in lf_r3)
                           for ci in range(tm // L)], axis=1)

    tri_t = lax.broadcasted_iota(jnp.int32, (L, L), 0) <= lax.broadcasted_iota(jnp.int32, (L, L), 1)
    nw_cols = [jnp.broadcast_to(nw_ref[:, h * hd:(h + 1) * hd], (hd, hd)).T for h in range(MLSTM_HEADS)]

    def split3_rows(row):
        hi = row.astype(BF16)
        r1 = row - hi.astype(F32)
        mid = r1.astype(BF16)
        lo = (r1 - mid.astype(F32)).astype(BF16)
        sub = lax.broadcasted_iota(jnp.int32, (8, row.shape[1]), 0)
        out = jnp.where(sub == 0, hi.astype(F32), jnp.where(sub == 1, mid.astype(F32),
                                                            jnp.where(sub == 2, lo.astype(F32), 0.0)))
        return out.astype(BF16)

    def rows_sum3(x8):
        return x8[0:1, :] + x8[1:2, :] + x8[2:3, :]

    for ci in range(tm // L):
        r0 = ci * L
        for h in range(MLSTM_HEADS):
            cs = slice(h * hd, (h + 1) * hd)
            qb16 = qc[r0:r0 + L, cs].astype(BF16)
            kb16 = kc[r0:r0 + L, cs].astype(BF16)
            vt = vt_ref[0, cs, r0:r0 + L]
            a_col = (misc[r0:r0 + L, MISC_I + h:MISC_I + h + 1]
                     - b_c[r0:r0 + L, MISC_F + h:MISC_F + h + 1])
            brow = b_r[MISC_F + h:MISC_F + h + 1, r0:r0 + L]
            lirow = misct[MISC_I + h:MISC_I + h + 1, r0:r0 + L]
            m_prev = m_ref[h]
            ct = c_ref[h]
            nrow = n_ref[h]

            dmat = jnp.where(tri_t, brow + a_col, NEG)
            m_inter = brow + m_prev
            m_t = jnp.maximum(jnp.max(dmat, axis=0, keepdims=True), m_inter)
            w = jnp.exp(dmat - m_t) * _dot_nt(kb16, qb16)
            decay = jnp.exp(m_inter - m_t)
            num = _dot(vt, w.astype(BF16)) + decay * _dot_nt(ct.astype(BF16), qb16)
            nq = rows_sum3(_dot_nt(split3_rows(nrow), qb16))
            den = jnp.sum(w, axis=0, keepdims=True) + decay * nq
            hout = num * (1.0 / jnp.maximum(jnp.abs(den), jnp.exp(-m_t)))

            b_last = brow[:, L - 1:L]
            grow = b_last - brow + lirow
            m_new = jnp.maximum(b_last + m_prev, jnp.max(grow, axis=1, keepdims=True))
            wk = jnp.exp(grow - m_new)
            d_c = jnp.exp(b_last + m_prev - m_new)
            c_ref[h] = d_c * ct + _dot((vt.astype(F32) * wk).astype(BF16), kb16)
            n_ref[h] = d_c * nrow + rows_sum3(_dot(split3_rows(wk), kb16))
            m_ref[h] = m_new

            hn = hout * lax.rsqrt(jnp.mean(hout * hout, axis=0, keepdims=True) + EPS) * nw_cols[h]
            o_ref[0, cs, r0:r0 + L] = (hn * _sigmoid(ogt_ref[0, cs, r0:r0 + L])).astype(o_ref.dtype)


def _mlstm(ml, vt_m, og_t, misc, misc_t, conv_w, conv_b, norm_w, batch, seq):
    tm = TOKEN_TILE
    tpb = seq // tm
    col = lambda j: (lambda b, i: (b * tpb + i, j))
    const = lambda b, i: (0, 0)
    trow = lambda b, i: (b, 0, i)
    return pl.pallas_call(
        _mlstm_kernel,
        grid=(batch, tpb),
        in_specs=[pl.BlockSpec((tm, MLSTM_WIDTH), col(0)),
                  pl.BlockSpec((tm, MLSTM_WIDTH), col(1)),
                  pl.BlockSpec((1, MLSTM_WIDTH, tm), trow),
                  pl.BlockSpec((1, MLSTM_WIDTH, tm), trow),
                  pl.BlockSpec((tm, LANES), col(0)),
                  pl.BlockSpec((1, LANES, tm), trow),
                  pl.BlockSpec(conv_w.shape, const),
                  pl.BlockSpec(conv_b.shape, const),
                  pl.BlockSpec(norm_w.shape, const)],
        out_specs=pl.BlockSpec((1, MLSTM_WIDTH, tm), trow),
        out_shape=jax.ShapeDtypeStruct((batch, MLSTM_WIDTH, seq), BF16),
        scratch_shapes=[pltpu.VMEM((tm + 8, MLSTM_WIDTH), F32),
                        pltpu.VMEM((tm + 8, MLSTM_WIDTH), F32),
                        pltpu.VMEM((MLSTM_HEADS, MLSTM_HEAD_DIM, MLSTM_HEAD_DIM), F32),
                        pltpu.VMEM((MLSTM_HEADS, 1, MLSTM_HEAD_DIM), F32),
                        pltpu.VMEM((MLSTM_HEADS, 1, 1), F32)],
        compiler_params=_cparams(2),
        name="mlstm",
    )(ml, ml, vt_m, og_t, misc, misc_t, conv_w, conv_b, norm_w)


def _layer(x2d, c, batch, seq, w_ada, b_ada, norm_ffn1_w, ffn1_w_in, ffn1_w_out, norm_mix_w, w_in, b_in,
           cmp_k_pe, cmp_k_w1, cmp_k_w2, cmp_v_pe, cmp_v_w1, cmp_v_w2, conv_w, conv_b, mlstm_norm_w, w_out,
           norm_ffn2_w, ffn2_w_in, ffn2_w_out, final_nw):
    d = x2d.shape[1]
    tpb = seq // TOKEN_TILE
    mod = _adaln(c.T, w_ada, b_ada.reshape(1, -1)).reshape(batch, N_MOD, d)

    x1 = _ffn(x2d, mod, norm_ffn1_w.reshape(1, d), ffn1_w_in.astype(BF16), ffn1_w_out.astype(BF16),
              (0, 1, 2), tpb)

    weights, b_pack = _split_inproj(w_in, b_in)
    (qpad, kc, vc, ksel, kwin, vtsel, vtwin, ml, vt_m, og_t, misc, misc_t) = _inproj(
        x1, mod, norm_mix_w.reshape(1, d), weights, b_pack, batch, seq)

    pk = _pack_compress(cmp_k_pe, cmp_k_w1, cmp_k_w2)
    pv = _pack_compress(cmp_v_pe, cmp_v_w1, cmp_v_w2)
    stacked = [jnp.stack([a, bb]) for a, bb in zip(pk, pv)]
    kcmp, vtcmp = _compress(kc, vc, *stacked)

    nb = seq // QT
    qpad3 = qpad
    gates_t = misc_t
    oc, sel, cnt = _nsa_cmp(qpad3, kcmp, vtcmp, gates_t, batch, seq)
    counts, lists = _tile_lists(cnt, batch, nb)
    o_t = _nsa_sel(counts, lists, qpad3, ksel.reshape(batch, seq, KV_WIDTH), vtsel, kwin.reshape(batch, seq, KV_WIDTH),
                   vtwin, sel, gates_t, oc, batch, seq)
    o_nsa = o_t.reshape(batch, NSA_WIDTH, seq)

    o_ml = _mlstm(ml, vt_m, og_t, misc, misc_t, conv_w, conv_b.reshape(1, -1), mlstm_norm_w.reshape(1, -1), batch, seq)

    return _ffn(x1, mod, norm_ffn2_w.reshape(1, d), ffn2_w_in.astype(BF16), ffn2_w_out.astype(BF16),
                (6, 7, 8), tpb, mix=(o_nsa, o_ml, w_out.astype(BF16)), mix_gate_row=5, final_nw=final_nw)


def kernel(x, c, w_ada, b_ada, norm_ffn1_w, ffn1_w_in, ffn1_w_out, norm_mix_w, w_in, b_in, cmp_k_pe, cmp_k_w1, cmp_k_w2, cmp_v_pe, cmp_v_w1, cmp_v_w2, conv_w, conv_b, mlstm_norm_w, w_out, norm_ffn2_w, ffn2_w_in, ffn2_w_out, final_norm_w):
    batch, seq, d = x.shape
    depth = w_ada.shape[0]
    assert depth == 1 and seq % (QT * CMP_STRIDE) == 0 and seq // SEL_BLOCK >= SEL_TOPK
    y = _layer(x.reshape(batch * seq, d), c, batch, seq, w_ada[0], b_ada[0], norm_ffn1_w[0], ffn1_w_in[0],
               ffn1_w_out[0], norm_mix_w[0], w_in[0], b_in[0], cmp_k_pe[0], cmp_k_w1[0], cmp_k_w2[0],
               cmp_v_pe[0], cmp_v_w1[0], cmp_v_w2[0], conv_w[0], conv_b[0], mlstm_norm_w[0], w_out[0],
               norm_ffn2_w[0], ffn2_w_in[0], ffn2_w_out[0], final_norm_w.reshape(1, d))
    return y.reshape(batch, seq, d)
```

```python
import functools

import numpy as np
import jax
import jax.numpy as jnp
from jax import lax
from jax.experimental import pallas as pl
from jax.experimental.pallas import tpu as pltpu

NSA_Q_HEADS = 8
NSA_KV_HEADS = 2
NSA_GROUP = NSA_Q_HEADS // NSA_KV_HEADS
NSA_HEAD_DIM = 64
CMP_BLOCK = 32
CMP_STRIDE = 16
CMP_HIDDEN = 128
SEL_BLOCK = 64
SEL_TOPK = 16
WINDOW = 512
FORCE_BONUS = 1.0e4
N_FORCED = 3
assert FORCE_BONUS > NSA_GROUP
MLSTM_HEADS = 4
MLSTM_HEAD_DIM = 128
CONV_WIDTH = 4
D_FF = 2816
N_MOD = 9
EPS = 1e-6
NEG = -1e30

NSA_WIDTH = NSA_Q_HEADS * NSA_HEAD_DIM
KV_WIDTH = NSA_KV_HEADS * NSA_HEAD_DIM
MLSTM_WIDTH = MLSTM_HEADS * MLSTM_HEAD_DIM
N_GATES = 3 * NSA_Q_HEADS

LANES = 128
QT = 128
TOKEN_TILE = 512
FF_CHUNK = 256
MLSTM_L = 128
SEL_UNROLL = 6
SEL_CHAIN = 2
CMP_TILES = 2
SEL_TILES = 2
TOPK_ROWS = 32
VT_ROWS = 80
VMEM_LIMIT = 56 * 1024 * 1024

F32 = jnp.float32
BF16 = jnp.bfloat16
HIGHEST = lax.Precision.HIGHEST
LOG2E = 1.4426950408889634


def _cparams(n_axes):
    return pltpu.CompilerParams(dimension_semantics=("arbitrary",) * n_axes,
                                vmem_limit_bytes=VMEM_LIMIT)


def _dot(a, b):
    return jnp.dot(a, b, preferred_element_type=F32)


def _dot_nt(a, b):
    return lax.dot_general(a, b, (((1,), (1,)), ((), ())), preferred_element_type=F32)


def _dot_tn(a, b):
    return lax.dot_general(a, b, (((0,), (0,)), ((), ())), preferred_element_type=F32)


def _sigmoid(x):
    return 1.0 / (1.0 + jnp.exp(-x))


def _silu(x):
    return x * _sigmoid(x)


def _norm_mod(x, nw, sh, sc):
    ms = jnp.mean(x * x, axis=-1, keepdims=True)
    y = x * lax.rsqrt(ms + EPS) * nw
    return y * (1.0 + sc) + sh


def _adaln_kernel(ct_ref, w_ref, b_ref, o_ref):
    w = w_ref[...]
    for r in range(ct_ref.shape[1]):
        a_col = _silu(ct_ref[:, r:r + 1])
        o_ref[r:r + 1, :] = jnp.sum(a_col * w, axis=0, keepdims=True) + b_ref[...]


def _adaln(c_t, w_ada, b_ada):
    d, rows = c_t.shape
    n = w_ada.shape[1]
    tn = n // N_MOD
    return pl.pallas_call(
        _adaln_kernel,
        grid=(N_MOD,),
        in_specs=[pl.BlockSpec((d, rows), lambda j: (0, 0)),
                  pl.BlockSpec((d, tn), lambda j: (0, j)),
                  pl.BlockSpec((1, tn), lambda j: (0, j))],
        out_specs=pl.BlockSpec((rows, tn), lambda j: (0, j)),
        out_shape=jax.ShapeDtypeStruct((rows, n), F32),
        compiler_params=_cparams(1),
        name="adaln",
    )(c_t, w_ada, b_ada)


def _ffn_kernel(*refs, mod_rows, with_mix, mix_gate_row, with_final):
    it = iter(refs)
    x_ref = next(it)
    mod_ref = next(it)
    nw_ref = next(it)
    win_ref = next(it)
    wo_ref = next(it)
    if with_mix:
        ma_ref = next(it)
        mb_ref = next(it)
        wmix_ref = next(it)
    if with_final:
        fnw_ref = next(it)
    o_ref = next(it)

    sh_row, sc_row, g_row = mod_rows
    x = x_ref[...]
    if with_mix:
        mix_t = jnp.concatenate([ma_ref[0].astype(BF16), mb_ref[0].astype(BF16)], axis=0)
        x = x + mod_ref[0, mix_gate_row:mix_gate_row + 1, :] * _dot_tn(mix_t, wmix_ref[...])
    h = _norm_mod(x, nw_ref[...], mod_ref[0, sh_row:sh_row + 1, :], mod_ref[0, sc_row:sc_row + 1, :])
    hb = h.astype(BF16)
    d_ff = wo_ref.shape[0]
    n_chunks = d_ff // FF_CHUNK
    acc = None
    for j in range(n_chunks):
        c0 = j * FF_CHUNK
        g = _dot(hb, win_ref[:, c0:c0 + FF_CHUNK])
        u = _dot(hb, win_ref[:, d_ff + c0:d_ff + c0 + FF_CHUNK])
        act = (_silu(g) * u).astype(BF16)
        part = _dot(act, wo_ref[c0:c0 + FF_CHUNK, :])
        acc = part if acc is None else acc + part
    y = x + 0.5 * mod_ref[0, g_row:g_row + 1, :] * acc
    if with_final:
        ms = jnp.mean(y * y, axis=-1, keepdims=True)
        y = y * lax.rsqrt(ms + EPS) * fnw_ref[...]
    o_ref[...] = y


def _ffn(x2d, mod, nw, w_in_bf, w_out_bf, mod_rows, tiles_per_batch, mix=None, mix_gate_row=None, final_nw=None):
    t, d = x2d.shape
    tm = TOKEN_TILE
    d_ff = w_out_bf.shape[0]
    assert d_ff % FF_CHUNK == 0
    row = lambda i: (i, 0)
    const = lambda i: (0, 0)
    in_specs = [pl.BlockSpec((tm, d), row),
                pl.BlockSpec((1, N_MOD, d), lambda i: (i // tiles_per_batch, 0, 0)),
                pl.BlockSpec((1, d), const),
                pl.BlockSpec(w_in_bf.shape, const),
                pl.BlockSpec(w_out_bf.shape, const)]
    args = [x2d, mod, nw, w_in_bf, w_out_bf]
    if mix is not None:
        ma, mb, wmix = mix
        trow = lambda i: (i // tiles_per_batch, 0, i % tiles_per_batch)
        in_specs += [pl.BlockSpec((1, ma.shape[1], tm), trow),
                     pl.BlockSpec((1, mb.shape[1], tm), trow),
                     pl.BlockSpec(wmix.shape, const)]
        args += [ma, mb, wmix]
    if final_nw is not None:
        in_specs.append(pl.BlockSpec((1, d), const))
        args.append(final_nw)
    kern = functools.partial(_ffn_kernel, mod_rows=mod_rows, with_mix=mix is not None,
                             mix_gate_row=mix_gate_row, with_final=final_nw is not None)
    return pl.pallas_call(
        kern,
        grid=(t // tm,),
        in_specs=in_specs,
        out_specs=pl.BlockSpec((tm, d), row),
        out_shape=jax.ShapeDtypeStruct((t, d), F32),
        compiler_params=_cparams(1),
        name="ffn_mix" if mix is not None else "ffn",
    )(*args)


QPAD_W = NSA_Q_HEADS * LANES
MISC_I = N_GATES
MISC_F = N_GATES + MLSTM_HEADS
HALF_BLOCK_W = CMP_STRIDE * KV_WIDTH
B_KV = NSA_WIDTH
B_ML = B_KV + 6 * KV_WIDTH
B_MISC = B_ML + 4 * MLSTM_WIDTH


def _vt_with_ones(v):
    vt = v.T
    n = vt.shape[1]
    pad = jnp.where(lax.broadcasted_iota(jnp.int32, (VT_ROWS - NSA_HEAD_DIM, n), 0) == 0, 1.0, 0.0)
    parts = []
    for h in range(NSA_KV_HEADS):
        parts += [vt[h * NSA_HEAD_DIM:(h + 1) * NSA_HEAD_DIM, :], pad]
    return jnp.concatenate(parts, axis=0).astype(BF16)


def _inproj_kernel(x_ref, mod_ref, nw_ref, wq_ref, wkv_ref, wml_ref, wmisc_ref, b_ref,
                   qpad_ref, kc_ref, vc_ref, ksel_ref, kwin_ref, vtsel_ref, vtwin_ref,
                   ml_ref, vtm_ref, ogt_ref, misc_ref, misct_ref, rows_ref):
    tm = x_ref.shape[0]
    h = _norm_mod(x_ref[...], nw_ref[...], mod_ref[0, 3:4, :], mod_ref[0, 4:5, :])
    hb = h.astype(BF16)

    def proj(w_ref, c0, width, b0):
        return _dot(hb, w_ref[:, c0:c0 + width]) + b_ref[:, b0 + c0:b0 + c0 + width]

    def half_blocks(val, out_ref):
        rows_ref[...] = val
        pieces = [rows_ref[pl.ds(l, tm // CMP_STRIDE, stride=CMP_STRIDE), :] for l in range(CMP_STRIDE)]
        out_ref[0] = jnp.concatenate(pieces, axis=1).astype(BF16)

    scale = NSA_HEAD_DIM ** -0.5 * LOG2E
    qt = (proj(wq_ref, 0, NSA_WIDTH, 0) * scale).T.astype(BF16)
    zero_rows = jnp.zeros((NSA_HEAD_DIM, tm), BF16)
    for i in range(NSA_Q_HEADS):
        slots = [zero_rows] * NSA_KV_HEADS
        slots[i // NSA_GROUP] = qt[i * NSA_HEAD_DIM:(i + 1) * NSA_HEAD_DIM, :]
        qpad_ref[0, i * LANES:(i + 1) * LANES, :] = jnp.concatenate(slots, axis=0)
    kv_c = proj(wkv_ref, 0, 2 * KV_WIDTH, B_KV)
    half_blocks(kv_c[:, :KV_WIDTH], kc_ref)
    half_blocks(kv_c[:, KV_WIDTH:], vc_ref)
    kv_s = proj(wkv_ref, 2 * KV_WIDTH, 2 * KV_WIDTH, B_KV)
    ksel_ref[...] = kv_s[:, :KV_WIDTH].astype(BF16)
    vtsel_ref[0] = _vt_with_ones(kv_s[:, KV_WIDTH:])
    kv_w = proj(wkv_ref, 4 * KV_WIDTH, 2 * KV_WIDTH, B_KV)
    kwin_ref[...] = kv_w[:, :KV_WIDTH].astype(BF16)
    vtwin_ref[0] = _vt_with_ones(kv_w[:, KV_WIDTH:])
    for i in range(2):
        ml_ref[:, i * MLSTM_WIDTH:(i + 1) * MLSTM_WIDTH] = proj(wml_ref, i * MLSTM_WIDTH, MLSTM_WIDTH, B_ML)
    vtm_ref[0] = proj(wml_ref, 2 * MLSTM_WIDTH, MLSTM_WIDTH, B_ML).T.astype(BF16)
    ogt_ref[0] = proj(wml_ref, 3 * MLSTM_WIDTH, MLSTM_WIDTH, B_ML).T
    misc = proj(wmisc_ref, 0, LANES, B_MISC)
    misc_ref[...] = misc
    misct_ref[0] = misc.T


def _inproj(x2d, mod, nw, weights, b_pack, batch, seq):
    t, d = x2d.shape
    tm = TOKEN_TILE
    tpb = seq // tm
    row = lambda i: (i, 0)
    const = lambda i: (0, 0)
    trow = lambda i: (i // tpb, 0, i % tpb)
    hrow = lambda i: (i // tpb, i % tpb, 0)
    vt_rows = NSA_KV_HEADS * VT_ROWS
    n_half = seq // CMP_STRIDE
    out_shapes = [
        jax.ShapeDtypeStruct((batch, QPAD_W, seq), BF16),
        jax.ShapeDtypeStruct((batch, n_half, HALF_BLOCK_W), BF16),
        jax.ShapeDtypeStruct((batch, n_half, HALF_BLOCK_W), BF16),
        jax.ShapeDtypeStruct((t, KV_WIDTH), BF16),
        jax.ShapeDtypeStruct((t, KV_WIDTH), BF16),
        jax.ShapeDtypeStruct((batch, vt_rows, seq), BF16),
        jax.ShapeDtypeStruct((batch, vt_rows, seq), BF16),
        jax.ShapeDtypeStruct((t, 2 * MLSTM_WIDTH), F32),
        jax.ShapeDtypeStruct((batch, MLSTM_WIDTH, seq), BF16),
        jax.ShapeDtypeStruct((batch, MLSTM_WIDTH, seq), F32),
        jax.ShapeDtypeStruct((t, LANES), F32),
        jax.ShapeDtypeStruct((batch, LANES, seq), F32),
    ]
    out_specs = [
        pl.BlockSpec((1, QPAD_W, tm), trow),
        pl.BlockSpec((1, tm // CMP_STRIDE, HALF_BLOCK_W), hrow),
        pl.BlockSpec((1, tm // CMP_STRIDE, HALF_BLOCK_W), hrow),
        pl.BlockSpec((tm, KV_WIDTH), row),
        pl.BlockSpec((tm, KV_WIDTH), row),
        pl.BlockSpec((1, vt_rows, tm), trow),
        pl.BlockSpec((1, vt_rows, tm), trow),
        pl.BlockSpec((tm, 2 * MLSTM_WIDTH), row),
        pl.BlockSpec((1, MLSTM_WIDTH, tm), trow),
        pl.BlockSpec((1, MLSTM_WIDTH, tm), trow),
        pl.BlockSpec((tm, LANES), row),
        pl.BlockSpec((1, LANES, tm), trow),
    ]
    return pl.pallas_call(
        _inproj_kernel,
        grid=(t // tm,),
        in_specs=[pl.BlockSpec((tm, d), row),
                  pl.BlockSpec((1, N_MOD, d), lambda i: (i // tpb, 0, 0)),
                  pl.BlockSpec((1, d), const)]
                 + [pl.BlockSpec(w.shape, const) for w in weights]
                 + [pl.BlockSpec(b_pack.shape, const)],
        out_specs=out_specs,
        out_shape=out_shapes,
        scratch_shapes=[pltpu.VMEM((tm, KV_WIDTH), F32)],
        compiler_params=_cparams(1),
        name="inproj",
    )(x2d, mod, nw, *weights, b_pack)


def _split_inproj(w_in, b_in):
    d = w_in.shape[0]
    offs = np.cumsum([0, NSA_WIDTH] + [KV_WIDTH] * 6 + [N_GATES] + [MLSTM_WIDTH] * 4 + [MLSTM_HEADS] * 2)
    g0, m0, i0 = int(offs[7]), int(offs[8]), int(offs[12])
    pad = LANES - N_GATES - 2 * MLSTM_HEADS
    misc_w = jnp.concatenate([w_in[:, g0:g0 + N_GATES], w_in[:, i0:i0 + 2 * MLSTM_HEADS],
                              jnp.zeros((d, pad), w_in.dtype)], axis=1)
    misc_b = jnp.concatenate([b_in[g0:g0 + N_GATES], b_in[i0:i0 + 2 * MLSTM_HEADS], jnp.zeros((pad,), b_in.dtype)])
    weights = [w_in[:, :NSA_WIDTH].astype(BF16), w_in[:, NSA_WIDTH:g0].astype(BF16),
               w_in[:, m0:i0].astype(BF16), misc_w.astype(BF16)]
    b_pack = jnp.concatenate([b_in[:g0], b_in[m0:i0], misc_b]).reshape(1, -1).astype(F32)
    assert b_pack.shape[1] == B_MISC + LANES
    return weights, b_pack


def _compress_kernel(xk_ref, xv_ref, wa_ref, wb_ref, pe_ref, w1_ref, w2_ref, kc_ref, vtc_ref):
    n_half = xk_ref.shape[1]

    def one(x_ref, s):
        x = x_ref[0]
        a = _dot(x, wa_ref[s])
        b = _dot(x, wb_ref[s])
        pe_term = jnp.dot(pe_ref[s], w1_ref[s], preferred_element_type=F32, precision=HIGHEST)[0:1, :]
        pe2 = jnp.concatenate([pe_term] * NSA_KV_HEADS, axis=1)
        pre = a + pltpu.roll(b, n_half - 1, 0) + pe2
        hid = 0.5 * pre * (1.0 + jnp.tanh(0.7978845608028654 * (pre + 0.044715 * pre * pre * pre)))
        return _dot(hid.astype(BF16), w2_ref[s])

    kc_ref[0] = one(xk_ref, 0).astype(BF16)
    vtc_ref[0] = _vt_with_ones(one(xv_ref, 1))


def _compress(xk, xv, wa, wb, pe8, w1, w2e):
    batch, n_half, width = xk.shape
    bsel = lambda b: (b, 0, 0)
    c3 = lambda b: (0, 0, 0)
    vt_rows = NSA_KV_HEADS * VT_ROWS
    return pl.pallas_call(
        _compress_kernel,
        grid=(batch,),
        in_specs=[pl.BlockSpec((1, n_half, width), bsel), pl.BlockSpec((1, n_half, width), bsel),
                  pl.BlockSpec(wa.shape, c3), pl.BlockSpec(wb.shape, c3),
                  pl.BlockSpec(pe8.shape, c3), pl.BlockSpec(w1.shape, c3), pl.BlockSpec(w2e.shape, c3)],
        out_specs=[pl.BlockSpec((1, n_half, KV_WIDTH), bsel), pl.BlockSpec((1, vt_rows, n_half), bsel)],
        out_shape=[jax.ShapeDtypeStruct((batch, n_half, KV_WIDTH), BF16),
                   jax.ShapeDtypeStruct((batch, vt_rows, n_half), BF16)],
        compiler_params=_cparams(1),
        name="compress",
    )(xk, xv, wa, wb, pe8, w1, w2e)


def _pack_compress(pe, w1, w2):
    half = CMP_BLOCK // 2
    w1r = w1.reshape(CMP_BLOCK, NSA_HEAD_DIM, CMP_HIDDEN)
    eye = jnp.eye(NSA_KV_HEADS, dtype=w1.dtype)

    def expand(w_half):
        z = w_half[:, None, :, None, :] * eye[None, :, None, :, None]
        return z.reshape(half * KV_WIDTH, NSA_KV_HEADS * CMP_HIDDEN)

    wa = expand(w1r[:half]).astype(BF16)
    wb = expand(w1r[half:]).astype(BF16)
    w2e = (w2[None, :, None, :] * eye[:, None, :, None]).reshape(NSA_KV_HEADS * CMP_HIDDEN, KV_WIDTH).astype(BF16)
    pe8 = jnp.concatenate([pe.reshape(1, CMP_BLOCK * NSA_HEAD_DIM),
                           jnp.zeros((7, CMP_BLOCK * NSA_HEAD_DIM), pe.dtype)], axis=0)
    return wa, wb, pe8, w1, w2e


def _q_aug(q_ref, qs_ref, h, cols=slice(0, QT)):
    qt = jnp.concatenate(
        [q_ref[0, (NSA_GROUP * h + g) * LANES:(NSA_GROUP * h + g + 1) * LANES, cols] for g in range(NSA_GROUP)],
        axis=1)
    return jnp.concatenate([qt, qs_ref[h]], axis=0)


def _slope2(head):
    return float(2.0 ** (-8.0 * (head + 1) / NSA_Q_HEADS)) * LOG2E


def _bf16_split3(x):
    parts = []
    for _ in range(3):
        p = float(np.asarray(x, np.float32).astype(BF16).astype(np.float32))
        parts.append(p)
        x = x - p
    return parts


def _init_alibi_operands(qs_ref, kpos_ref, key_stride):
    lane = lax.broadcasted_iota(jnp.int32, (QT, LANES), 1)
    row = lax.broadcasted_iota(jnp.int32, (QT, LANES), 0)
    kpos_ref[...] = jnp.where(lane < 3, row.astype(F32), 0.0).astype(BF16)
    srow = lax.broadcasted_iota(jnp.int32, (LANES, QT), 0)
    for h in range(NSA_KV_HEADS):
        for g in range(NSA_GROUP):
            hi, mid, lo = _bf16_split3(key_stride * _slope2(NSA_GROUP * h + g))
            blk = jnp.where(srow == 0, hi, jnp.where(srow == 1, mid, jnp.where(srow == 2, lo, 0.0)))
            qs_ref[h, :, g * QT:(g + 1) * QT] = blk.astype(BF16)


def _gate_row(gt_ref, head, branch, cols):
    r = head * 3 + branch
    return _sigmoid(gt_ref[0, r:r + 1, cols])


def _softmax_jobs(jobs):
    scores = [_dot(jnp.concatenate(job[1], axis=0), job[0]) for job in jobs]
    staged = []
    for (q_aug, k_tiles, masks, offs, value_groups, state), st_all in zip(jobs, scores):
        n_t = len(k_tiles)
        per_g = []
        for g in range(NSA_GROUP):
            sl = slice(g * QT, (g + 1) * QT)
            ss = []
            for u in range(n_t):
                s = st_all[u * QT:(u + 1) * QT, sl]
                ss.append(s if masks[u] is None else jnp.where(masks[u], s, NEG))
            mx = None
            for u in range(n_t):
                cand = jnp.max(ss[u], axis=0, keepdims=True) + offs[u][g]
                mx = cand if mx is None else jnp.maximum(mx, cand)
            alpha = None
            if state is None:
                m_new = mx
            else:
                m_ref, _, h = state
                m_old = m_ref[h, :, sl]
                m_new = jnp.maximum(m_old, mx)
                alpha = jnp.exp2(m_old - m_new)
                m_ref[h, :, sl] = m_new
            m_use = jnp.where(m_new < 0.5 * NEG, 0.0, m_new)
            per_g.append((ss, m_use, alpha))
        staged.append(per_g)
    ets = []
    for (q_aug, k_tiles, masks, offs, value_groups, state), per_g in zip(jobs, staged):
        n_t = len(k_tiles)
        rows = [jnp.concatenate([jnp.exp2(per_g[g][0][u] - (per_g[g][1] - offs[u][g])).astype(BF16)
                                 for g in range(NSA_GROUP)], axis=1) for u in range(n_t)]
        ets.append(jnp.concatenate(rows, axis=0))
    results = []
    for (q_aug, k_tiles, masks, offs, value_groups, state), per_g, et in zip(jobs, staged, ets):
        mats = [jnp.concatenate(grp, axis=1) for grp in value_groups]
        pv_all = _dot(jnp.concatenate(mats, axis=0), et)
        pvs, r0 = [], 0
        for mat in mats:
            pvs.append(pv_all[r0:r0 + mat.shape[0], :])
            r0 += mat.shape[0]
        if state is None:
            results.append(pvs)
        else:
            _, acc_ref, h = state
            acc_ref[h] = jnp.concatenate([pg[2] for pg in per_g], axis=1) * acc_ref[h] + pvs[0]
            results.append(None)
    return results


def _gated(gt_ref, h, branch, pv, cols=slice(0, QT)):
    inv = 1.0 / jnp.maximum(pv[NSA_HEAD_DIM:NSA_HEAD_DIM + 1, :], 1e-30)
    out = []
    for g in range(NSA_GROUP):
        sl = slice(g * QT, (g + 1) * QT)
        out.append(pv[0:NSA_HEAD_DIM, sl] * (inv[:, sl] * _gate_row(gt_ref, NSA_GROUP * h + g, branch, cols)))
    return out


def _nsa_cmp_kernel(q_ref, kc_ref, vtc_ref, gt_ref, oc_ref, sel_ref, cnt_ref,
                    qs_ref, kpos_ref, ovt_ref, pvo_ref, pvi_ref):
    ncp = kc_ref.shape[1]
    n_ct = ncp // QT
    n_sel = sel_ref.shape[3]
    tiles = range(CMP_TILES)
    starts = [(pl.program_id(1) * CMP_TILES + s) * QT for s in tiles]
    cols = [slice(s * QT, (s + 1) * QT) for s in tiles]
    units = [(s, h) for s in tiles for h in range(NSA_KV_HEADS)]
    tile_span = QT * CMP_STRIDE

    @pl.when((pl.program_id(0) == 0) & (pl.program_id(1) == 0))
    def _():
        _init_alibi_operands(qs_ref, kpos_ref, float(CMP_STRIDE))
        j_o = lax.broadcasted_iota(jnp.int32, (n_sel, ncp), 0)
        c_o = lax.broadcasted_iota(jnp.int32, (n_sel, ncp), 1)
        ov = ((c_o * CMP_STRIDE <= j_o * SEL_BLOCK + (SEL_BLOCK - 1))
              & (c_o * CMP_STRIDE + (CMP_BLOCK - 1) >= j_o * SEL_BLOCK))
        ovt_ref[...] = jnp.where(ov, 1.0, 0.0).astype(BF16)

    a0 = (lax.broadcasted_iota(jnp.int32, (QT, QT), 1)
          - CMP_STRIDE * lax.broadcasted_iota(jnp.int32, (QT, QT), 0))
    q_augs = [_q_aug(q_ref, qs_ref, h, cols[s]) for s, h in units]
    n_vis = jnp.minimum((starts[-1] + QT - CMP_BLOCK) // tile_span + 1, n_ct)

    for k in range(1, n_ct + 1):
        @pl.when(n_vis == k)
        def _(k=k):
            jobs = []
            for i, (s, h) in enumerate(units):
                k_tiles, masks, offs, vts, ovs = [], [], [], [], []
                for ci in range(k):
                    cs = slice(ci * QT, (ci + 1) * QT)
                    first_end = ci * tile_span + CMP_BLOCK - 1
                    k_tiles.append(jnp.concatenate([kc_ref[0, cs, :], kpos_ref[...]], axis=1))
                    masks.append(a0 + (starts[s] - first_end) >= 0)
                    offs.append([_slope2(NSA_GROUP * h + g) * first_end for g in range(NSA_GROUP)])
                    vts.append(vtc_ref[0, h * VT_ROWS:(h + 1) * VT_ROWS, cs])
                    ovs.append(ovt_ref[:, cs])
                jobs.append((q_augs[i], k_tiles, masks, offs, [vts, ovs], None))
            for i, (pvo, pvi) in enumerate(_softmax_jobs(jobs)):
                pvo_ref[i] = pvo
                pvi_ref[i] = pvi

    j_io = lax.broadcasted_iota(jnp.int32, (n_sel, QT), 0)
    q_io = lax.broadcasted_iota(jnp.int32, (n_sel, QT), 1)
    ones8 = jnp.ones((8, QT), BF16)

    taken = -(2.0 ** 126)
    scores = []
    for i, (s, h) in enumerate(units):
        t_j = starts[s] + q_io
        cur = t_j // SEL_BLOCK
        valid = j_io * SEL_BLOCK <= t_j
        forced = (j_io == 0) | (j_io == cur) | (j_io == cur - 1)
        pvo = pvo_ref[i]
        o_c = _gated(gt_ref, h, 0, pvo, cols[s])
        inv = 1.0 / jnp.maximum(pvo[NSA_HEAD_DIM:NSA_HEAD_DIM + 1, :], 1e-30)
        imp = None
        for g in range(NSA_GROUP):
            sl = slice(g * QT, (g + 1) * QT)
            oc_ref[0, NSA_GROUP * h + g, :, cols[s]] = o_c[g]
            term = pvi_ref[i, :, sl] * inv[:, sl]
            imp = term if imp is None else imp + term
        scores.append(jnp.where(forced, taken, jnp.where(valid, imp, -FORCE_BONUS)))

    n_cls = n_sel // TOPK_ROWS
    cls = jnp.minimum((starts[-1] + QT + SEL_BLOCK * TOPK_ROWS - 1) // (SEL_BLOCK * TOPK_ROWS), n_cls)

    for k in range(1, n_cls + 1):
        @pl.when(cls == k)
        def _(rows=k * TOPK_ROWS):
            j_sub = j_io[:rows]

            def pick(_, carry):
                out = []
                for sc in carry:
                    mx = jnp.max(sc, axis=0, keepdims=True)
                    idx = jnp.min(jnp.where(sc == mx, j_sub, n_sel), axis=0, keepdims=True)
                    out.append(jnp.where(j_sub == idx, taken, sc))
                return tuple(out)

            picked = lax.fori_loop(0, SEL_TOPK - N_FORCED, pick, tuple(sc[:rows] for sc in scores))
            for i, (s, h) in enumerate(units):
                sel = jnp.where(picked[i] <= 0.5 * taken, 1.0, 0.0)
                if rows < n_sel:
                    sel = jnp.concatenate([sel, jnp.zeros((n_sel - rows, QT), F32)], axis=0)
                sel_ref[0, h, s] = sel
                cnt = _dot_nt(ones8, sel.astype(BF16))
                cnt_ref[0, s, h] = cnt[0:1, :]


def _nsa_cmp(qpad3, kc, vtc, gates_t, batch, seq):
    nb = seq // QT
    n_sel = seq // SEL_BLOCK
    ncp = kc.shape[1]
    assert ncp % QT == 0 and nb % CMP_TILES == 0
    vt_rows = NSA_KV_HEADS * VT_ROWS
    qw = QT * CMP_TILES
    n_units = CMP_TILES * NSA_KV_HEADS
    return pl.pallas_call(
        _nsa_cmp_kernel,
        grid=(batch, nb // CMP_TILES),
        in_specs=[pl.BlockSpec((1, QPAD_W, qw), lambda b, i: (b, 0, i)),
                  pl.BlockSpec((1, ncp, KV_WIDTH), lambda b, i: (b, 0, 0)),
                  pl.BlockSpec((1, vt_rows, ncp), lambda b, i: (b, 0, 0)),
                  pl.BlockSpec((1, N_GATES, qw), lambda b, i: (b, 0, i))],
        out_specs=[pl.BlockSpec((1, NSA_Q_HEADS, NSA_HEAD_DIM, qw), lambda b, i: (b, 0, 0, i)),
                   pl.BlockSpec((1, NSA_KV_HEADS, CMP_TILES, n_sel, QT), lambda b, i: (b, 0, i, 0, 0)),
                   pl.BlockSpec((1, CMP_TILES, NSA_KV_HEADS, 1, n_sel), lambda b, i: (b, i, 0, 0, 0))],
        out_shape=[jax.ShapeDtypeStruct((batch, NSA_Q_HEADS, NSA_HEAD_DIM, seq), F32),
                   jax.ShapeDtypeStruct((batch, NSA_KV_HEADS, nb, n_sel, QT), F32),
                   jax.ShapeDtypeStruct((batch, nb, NSA_KV_HEADS, 1, n_sel), F32)],
        scratch_shapes=[pltpu.VMEM((NSA_KV_HEADS, LANES, NSA_GROUP * QT), BF16),
                        pltpu.VMEM((QT, LANES), BF16),
                        pltpu.VMEM((n_sel, ncp), BF16),
                        pltpu.VMEM((n_units, VT_ROWS, NSA_GROUP * QT), F32),
                        pltpu.VMEM((n_units, n_sel, NSA_GROUP * QT), F32)],
        compiler_params=_cparams(2),
        name="nsa_cmp",
    )(qpad3, kc, vtc, gates_t)


def _nsa_sel_kernel(count_ref, list_ref, q_ref, ksel_ref, vtsel_ref, kwin_ref, vtwin_ref, selm_ref, gt_ref,
                    oc_ref, o_ref, m_sel, acc_sel, qs_ref, kpos_ref, part_ref):
    b = pl.program_id(0)
    nb = pl.num_programs(1) * SEL_TILES
    tiles = range(SEL_TILES)
    qbs = [pl.program_id(1) * SEL_TILES + s for s in tiles]
    cols = [slice(s * QT, (s + 1) * QT) for s in tiles]
    units = [(s, h) for s in tiles for h in range(NSA_KV_HEADS)]
    k_io = lax.broadcasted_iota(jnp.int32, (QT, QT), 0)
    q_io = lax.broadcasted_iota(jnp.int32, (QT, QT), 1)
    causal = q_io >= k_io
    in_window = q_io < k_io
    first_half = k_io < SEL_BLOCK

    @pl.when((b == 0) & (pl.program_id(1) == 0))
    def _():
        _init_alibi_operands(qs_ref, kpos_ref, 1.0)

    def k_aug(k_ref, t):
        off = pl.multiple_of(t * QT, QT)
        return jnp.concatenate([k_ref[0, pl.ds(off, QT), :], kpos_ref[...]], axis=1)

    def v_ext(vt_ref, h, t):
        return vt_ref[0, h * VT_ROWS:(h + 1) * VT_ROWS, pl.ds(pl.multiple_of(t * QT, QT), QT)]

    def tile_offs(h, t, live=None):
        pos = (t * QT).astype(F32)
        out = [_slope2(NSA_GROUP * h + g) * pos for g in range(NSA_GROUP)]
        return out if live is None else [jnp.where(live, o, NEG) for o in out]

    def chosen(s, h, t):
        mrow = selm_ref[0, h, s, pl.ds(2 * t, 2), :]
        return jnp.where(first_half, mrow[0:1, :], mrow[1:2, :]) > 0.5

    q_augs = [_q_aug(q_ref, qs_ref, h, cols[s]) for s, h in units]

    slot = [(b * nb + qbs[s]) * NSA_KV_HEADS + h for s, h in units]
    n_tiles = [count_ref[sl] for sl in slot]

    m_sel[...] = jnp.full(m_sel.shape, NEG, F32)
    acc_sel[...] = jnp.zeros(acc_sel.shape, F32)

    def sel_jobs(trip, first, step):
        jobs = []
        for i, (s, h) in enumerate(units):
            k_tiles, masks, offs, vts = [], [], [], []
            for u in range(SEL_UNROLL):
                if first and step == 0 and u == 0:
                    t, mask, off = qbs[s], chosen(s, h, qbs[s]) & causal, tile_offs(h, qbs[s])
                else:
                    idx = (trip * SEL_CHAIN + step) * SEL_UNROLL + u - 1
                    live = idx < n_tiles[i]
                    t = jnp.where(live, list_ref[slot[i] * nb + jnp.minimum(idx, nb - 1)], 0)
                    mask, off = chosen(s, h, t), tile_offs(h, t, live)
                k_tiles.append(k_aug(ksel_ref, t))
                masks.append(mask)
                offs.append(off)
                vts.append(v_ext(vtsel_ref, h, t))
            jobs.append((q_augs[i], k_tiles, masks, offs, [vts], (m_sel, acc_sel, i)))
        return jobs

    n_win = WINDOW // QT + 1
    win_jobs = []
    for i, (s, h) in enumerate(units):
        k_tiles, masks, offs, vts = [], [], [], []
        for u in range(n_win):
            raw = qbs[s] - (n_win - 1) + u
            t = jnp.maximum(raw, 0)
            k_tiles.append(k_aug(kwin_ref, t))
            masks.append(in_window if u == 0 else causal if u == n_win - 1 else None)
            offs.append(tile_offs(h, t, raw >= 0))
            vts.append(v_ext(vtwin_ref, h, t))
        win_jobs.append((q_augs[i], k_tiles, masks, offs, [vts], None))
    jobs = sel_jobs(0, True, 0) + win_jobs
    for step in range(1, SEL_CHAIN):
        jobs += sel_jobs(0, True, step)
    results = _softmax_jobs(jobs)
    for i, (s, h) in enumerate(units):
        o_w = _gated(gt_ref, h, 2, results[len(units) + i][0], cols[s])
        for g in range(NSA_GROUP):
            head = NSA_GROUP * h + g
            part_ref[s, head] = oc_ref[0, head, :, cols[s]] + o_w[g]

    def sel_body(trip, carry):
        jobs = []
        for step in range(SEL_CHAIN):
            jobs += sel_jobs(trip, False, step)
        _softmax_jobs(jobs)
        return carry

    per_trip = SEL_UNROLL * SEL_CHAIN
    n_trips = (functools.reduce(jnp.maximum, n_tiles) + per_trip) // per_trip
    lax.fori_loop(1, n_trips, sel_body, 0)

    for i, (s, h) in enumerate(units):
        o_s = _gated(gt_ref, h, 1, acc_sel[i], cols[s])
        for g in range(NSA_GROUP):
            head = NSA_GROUP * h + g
            o_ref[0, head, :, cols[s]] = part_ref[s, head] + o_s[g]


def _tile_lists(cnt, batch, nb):
    pair = cnt.reshape(batch, nb, NSA_KV_HEADS, nb, 2).sum(axis=-1)
    p_io = lax.broadcasted_iota(jnp.int32, pair.shape, 3)
    flag = (pair > 0.5) & (p_io < lax.broadcasted_iota(jnp.int32, pair.shape, 1))
    rank = jnp.cumsum(flag.astype(jnp.int32), axis=-1) - 1
    slot_io = lax.broadcasted_iota(jnp.int32, pair.shape + (nb,), 4)
    hit = flag[..., None] & (rank[..., None] == slot_io)
    lists = jnp.sum(jnp.where(hit, p_io[..., None], 0), axis=3)
    counts = jnp.sum(flag.astype(jnp.int32), axis=-1)
    return counts.reshape(-1), lists.reshape(-1)


def _nsa_sel(counts, lists, qpad3, ksel, vtsel, kwin, vtwin, selm, gates_t, oc, batch, seq):
    nb = seq // QT
    assert nb % SEL_TILES == 0
    res = lambda b, i, c, l: (b, 0, 0)
    vt_rows = NSA_KV_HEADS * VT_ROWS
    qw = QT * SEL_TILES
    n_units = SEL_TILES * NSA_KV_HEADS
    grid_spec = pltpu.PrefetchScalarGridSpec(
        num_scalar_prefetch=2,
        grid=(batch, nb // SEL_TILES),
        in_specs=[pl.BlockSpec((1, QPAD_W, qw), lambda b, i, c, l: (b, 0, i)),
                  pl.BlockSpec((1, seq, KV_WIDTH), res),
                  pl.BlockSpec((1, vt_rows, seq), res),
                  pl.BlockSpec((1, seq, KV_WIDTH), res),
                  pl.BlockSpec((1, vt_rows, seq), res),
                  pl.BlockSpec((1, NSA_KV_HEADS, SEL_TILES, 2 * nb, QT), lambda b, i, c, l: (b, 0, i, 0, 0)),
                  pl.BlockSpec((1, N_GATES, qw), lambda b, i, c, l: (b, 0, i)),
                  pl.BlockSpec((1, NSA_Q_HEADS, NSA_HEAD_DIM, qw), lambda b, i, c, l: (b, 0, 0, i))],
        out_specs=pl.BlockSpec((1, NSA_Q_HEADS, NSA_HEAD_DIM, qw), lambda b, i, c, l: (b, 0, 0, i)),
        scratch_shapes=[pltpu.VMEM((n_units, 1, NSA_GROUP * QT), F32),
                        pltpu.VMEM((n_units, VT_ROWS, NSA_GROUP * QT), F32),
                        pltpu.VMEM((NSA_KV_HEADS, LANES, NSA_GROUP * QT), BF16),
                        pltpu.VMEM((QT, LANES), BF16),
                        pltpu.VMEM((SEL_TILES, NSA_Q_HEADS, NSA_HEAD_DIM, QT), F32)],
    )
    return pl.pallas_call(
        _nsa_sel_kernel,
        grid_spec=grid_spec,
        out_shape=jax.ShapeDtypeStruct((batch, NSA_Q_HEADS, NSA_HEAD_DIM, seq), F32),
        compiler_params=_cparams(2),
        name="nsa_sel",
    )(counts, lists, qpad3, ksel, vtsel, kwin, vtwin, selm, gates_t, oc)


def _mlstm_kernel(q_ref, k_ref, vt_ref, ogt_ref, misc_ref, misct_ref, cw_ref, cb_ref, nw_ref,
                  o_ref, xq_ref, xk_ref, c_ref, n_ref, m_ref):
    tm = q_ref.shape[0]
    L = MLSTM_L
    hd = MLSTM_HEAD_DIM
    halo = 8

    @pl.when(pl.program_id(1) == 0)
    def _():
        xq_ref[0:halo, :] = jnp.zeros((halo, MLSTM_WIDTH), F32)
        xk_ref[0:halo, :] = jnp.zeros((halo, MLSTM_WIDTH), F32)
        c_ref[...] = jnp.zeros_like(c_ref)
        n_ref[...] = jnp.zeros_like(n_ref)
        m_ref[...] = jnp.zeros_like(m_ref)

    def conv(x_ref, buf_ref, col0):
        buf_ref[halo:halo + tm, :] = x_ref[...]
        y = cb_ref[:, col0:col0 + MLSTM_WIDTH]
        for kk in range(CONV_WIDTH):
            r0 = halo - (CONV_WIDTH - 1) + kk
            y = y + cw_ref[kk:kk + 1, col0:col0 + MLSTM_WIDTH] * buf_ref[r0:r0 + tm, :]
        tail = buf_ref[tm:tm + halo, :]
        buf_ref[0:halo, :] = tail
        return _silu(y)

    qc = conv(q_ref, xq_ref, 0)
    kc = conv(k_ref, xk_ref, MLSTM_WIDTH) * (hd ** -0.5)

    misc = misc_ref[...]
    misct = misct_ref[0]
    lf_c = -jnp.log(1.0 + jnp.exp(-misc))
    lf_r = -jnp.log(1.0 + jnp.exp(-misct))
    row_io = lax.broadcasted_iota(jnp.int32, (L, L), 0)
    col_io = lax.broadcasted_iota(jnp.int32, (L, L), 1)
    tril = jnp.where(col_io <= row_io, 1.0, 0.0).astype(BF16)
    triu = jnp.where(row_io <= col_io, 1.0, 0.0).astype(BF16)

    def split3(x):
        hi = x.astype(BF16)
        r1 = x - hi.astype(F32)
        mid = r1.astype(BF16)
        return hi, mid, (r1 - mid.astype(F32)).astype(BF16)

    lf_c3 = split3(lf_c)
    lf_r3 = split3(lf_r)
    b_c = jnp.concatenate([sum(_dot(tril, p[ci * L:(ci + 1) * L, :]) for p in lf_c3)
                           for ci in range(tm // L)], axis=0)
    b_r = jnp.concatenate([sum(_dot(p[:, ci * L:(ci + 1) * L], triu) for p in lf_r3)
                           for ci in range(tm // L)], axis=1)

    tri_t = lax.broadcasted_iota(jnp.int32, (L, L), 0) <= lax.broadcasted_iota(jnp.int32, (L, L), 1)
    nw_cols = [jnp.broadcast_to(nw_ref[:, h * hd:(h + 1) * hd], (hd, hd)).T for h in range(MLSTM_HEADS)]

    def split3_rows(row):
        hi = row.astype(BF16)
        r1 = row - hi.astype(F32)
        mid = r1.astype(BF16)
        lo = (r1 - mid.astype(F32)).astype(BF16)
        sub = lax.broadcasted_iota(jnp.int32, (8, row.shape[1]), 0)
        out = jnp.where(sub == 0, hi.astype(F32), jnp.where(sub == 1, mid.astype(F32),
                                                            jnp.where(sub == 2, lo.astype(F32), 0.0)))
        return out.astype(BF16)

    def rows_sum3(x8):
        return x8[0:1, :] + x8[1:2, :] + x8[2:3, :]

    for ci in range(tm // L):
        r0 = ci * L
        for h in range(MLSTM_HEADS):
            cs = slice(h * hd, (h + 1) * hd)
            qb16 = qc[r0:r0 + L, cs].astype(BF16)
            kb16 = kc[r0:r0 + L, cs].astype(BF16)
            vt = vt_ref[0, cs, r0:r0 + L]
            a_col = (misc[r0:r0 + L, MISC_I + h:MISC_I + h + 1]
                     - b_c[r0:r0 + L, MISC_F + h:MISC_F + h + 1])
            brow = b_r[MISC_F + h:MISC_F + h + 1, r0:r0 + L]
            lirow = misct[MISC_I + h:MISC_I + h + 1, r0:r0 + L]
            m_prev = m_ref[h]
            ct = c_ref[h]
            nrow = n_ref[h]

            dmat = jnp.where(tri_t, brow + a_col, NEG)
            m_inter = brow + m_prev
            m_t = jnp.maximum(jnp.max(dmat, axis=0, keepdims=True), m_inter)
            w = jnp.exp(dmat - m_t) * _dot_nt(kb16, qb16)
            decay = jnp.exp(m_inter - m_t)
            num = _dot(vt, w.astype(BF16)) + decay * _dot_nt(ct.astype(BF16), qb16)
            nq = rows_sum3(_dot_nt(split3_rows(nrow), qb16))
            den = jnp.sum(w, axis=0, keepdims=True) + decay * nq
            hout = num * (1.0 / jnp.maximum(jnp.abs(den), jnp.exp(-m_t)))

            b_last = brow[:, L - 1:L]
            grow = b_last - brow + lirow
            m_new = jnp.maximum(b_last + m_prev, jnp.max(grow, axis=1, keepdims=True))
            wk = jnp.exp(grow - m_new)
            d_c = jnp.exp(b_last + m_prev - m_new)
            c_ref[h] = d_c * ct + _dot((vt.astype(F32) * wk).astype(BF16), kb16)
            n_ref[h] = d_c * nrow + rows_sum3(_dot(split3_rows(wk), kb16))
            m_ref[h] = m_new

            hn = hout * lax.rsqrt(jnp.mean(hout * hout, axis=0, keepdims=True) + EPS) * nw_cols[h]
            o_ref[0, cs, r0:r0 + L] = (hn * _sigmoid(ogt_ref[0, cs, r0:r0 + L])).astype(o_ref.dtype)


def _mlstm(ml, vt_m, og_t, misc, misc_t, conv_w, conv_b, norm_w, batch, seq):
    tm = TOKEN_TILE
    tpb = seq // tm
    col = lambda j: (lambda b, i: (b * tpb + i, j))
    const = lambda b, i: (0, 0)
    trow = lambda b, i: (b, 0, i)
    return pl.pallas_call(
        _mlstm_kernel,
        grid=(batch, tpb),
        in_specs=[pl.BlockSpec((tm, MLSTM_WIDTH), col(0)),
                  pl.BlockSpec((tm, MLSTM_WIDTH), col(1)),
                  pl.BlockSpec((1, MLSTM_WIDTH, tm), trow),
                  pl.BlockSpec((1, MLSTM_WIDTH, tm), trow),
                  pl.BlockSpec((tm, LANES), col(0)),
                  pl.BlockSpec((1, LANES, tm), trow),
                  pl.BlockSpec(conv_w.shape, const),
                  pl.BlockSpec(conv_b.shape, const),
                  pl.BlockSpec(norm_w.shape, const)],
        out_specs=pl.BlockSpec((1, MLSTM_WIDTH, tm), trow),
        out_shape=jax.ShapeDtypeStruct((batch, MLSTM_WIDTH, seq), BF16),
        scratch_shapes=[pltpu.VMEM((tm + 8, MLSTM_WIDTH), F32),
                        pltpu.VMEM((tm + 8, MLSTM_WIDTH), F32),
                        pltpu.VMEM((MLSTM_HEADS, MLSTM_HEAD_DIM, MLSTM_HEAD_DIM), F32),
                        pltpu.VMEM((MLSTM_HEADS, 1, MLSTM_HEAD_DIM), F32),
                        pltpu.VMEM((MLSTM_HEADS, 1, 1), F32)],
        compiler_params=_cparams(2),
        name="mlstm",
    )(ml, ml, vt_m, og_t, misc, misc_t, conv_w, conv_b, norm_w)


def _layer(x2d, c, batch, seq, w_ada, b_ada, norm_ffn1_w, ffn1_w_in, ffn1_w_out, norm_mix_w, w_in, b_in,
           cmp_k_pe, cmp_k_w1, cmp_k_w2, cmp_v_pe, cmp_v_w1, cmp_v_w2, conv_w, conv_b, mlstm_norm_w, w_out,
           norm_ffn2_w, ffn2_w_in, ffn2_w_out, final_nw):
    d = x2d.shape[1]
    tpb = seq // TOKEN_TILE
    mod = _adaln(c.T, w_ada, b_ada.reshape(1, -1)).reshape(batch, N_MOD, d)

    x1 = _ffn(x2d, mod, norm_ffn1_w.reshape(1, d), ffn1_w_in.astype(BF16), ffn1_w_out.astype(BF16),
              (0, 1, 2), tpb)

    weights, b_pack = _split_inproj(w_in, b_in)
    (qpad, kc, vc, ksel, kwin, vtsel, vtwin, ml, vt_m, og_t, misc, misc_t) = _inproj(
        x1, mod, norm_mix_w.reshape(1, d), weights, b_pack, batch, seq)

    pk = _pack_compress(cmp_k_pe, cmp_k_w1, cmp_k_w2)
    pv = _pack_compress(cmp_v_pe, cmp_v_w1, cmp_v_w2)
    stacked = [jnp.stack([a, bb]) for a, bb in zip(pk, pv)]
    kcmp, vtcmp = _compress(kc, vc, *stacked)

    nb = seq // QT
    qpad3 = qpad
    gates_t = misc_t
    oc, sel, cnt = _nsa_cmp(qpad3, kcmp, vtcmp, gates_t, batch, seq)
    counts, lists = _tile_lists(cnt, batch, nb)
    o_t = _nsa_sel(counts, lists, qpad3, ksel.reshape(batch, seq, KV_WIDTH), vtsel, kwin.reshape(batch, seq, KV_WIDTH),
                   vtwin, sel, gates_t, oc, batch, seq)
    o_nsa = o_t.reshape(batch, NSA_WIDTH, seq)

    o_ml = _mlstm(ml, vt_m, og_t, misc, misc_t, conv_w, conv_b.reshape(1, -1), mlstm_norm_w.reshape(1, -1), batch, seq)

    return _ffn(x1, mod, norm_ffn2_w.reshape(1, d), ffn2_w_in.astype(BF16), ffn2_w_out.astype(BF16),
                (6, 7, 8), tpb, mix=(o_nsa, o_ml, w_out.astype(BF16)), mix_gate_row=5, final_nw=final_nw)


def kernel(x, c, w_ada, b_ada, norm_ffn1_w, ffn1_w_in, ffn1_w_out, norm_mix_w, w_in, b_in, cmp_k_pe, cmp_k_w1, cmp_k_w2, cmp_v_pe, cmp_v_w1, cmp_v_w2, conv_w, conv_b, mlstm_norm_w, w_out, norm_ffn2_w, ffn2_w_in, ffn2_w_out, final_norm_w):
    batch, seq, d = x.shape
    depth = w_ada.shape[0]
    assert depth == 1 and seq % (QT * CMP_STRIDE) == 0 and seq // SEL_BLOCK >= SEL_TOPK
    y = _layer(x.reshape(batch * seq, d), c, batch, seq, w_ada[0], b_ada[0], norm_ffn1_w[0], ffn1_w_in[0],
               ffn1_w_out[0], norm_mix_w[0], w_in[0], b_in[0], cmp_k_pe[0], cmp_k_w1[0], cmp_k_w2[0],
               cmp_v_pe[0], cmp_v_w1[0], cmp_v_w2[0], conv_w[0], conv_b[0], mlstm_norm_w[0], w_out[0],
               norm_ffn2_w[0], ffn2_w_in[0], ffn2_w_out[0], final_norm_w.reshape(1, d))
    return y.reshape(batch, seq, d)
```

```python
import functools

import numpy as np
import jax
import jax.numpy as jnp
from jax import lax
from jax.experimental import pallas as pl
from jax.experimental.pallas import tpu as pltpu

NSA_Q_HEADS = 8
NSA_KV_HEADS = 2
NSA_GROUP = NSA_Q_HEADS // NSA_KV_HEADS
NSA_HEAD_DIM = 64
CMP_BLOCK = 32
CMP_STRIDE = 16
CMP_HIDDEN = 128
SEL_BLOCK = 64
SEL_TOPK = 16
WINDOW = 512
FORCE_BONUS = 1.0e4
N_FORCED = 3
assert FORCE_BONUS > NSA_GROUP
MLSTM_HEADS = 4
MLSTM_HEAD_DIM = 128
CONV_WIDTH = 4
D_FF = 2816
N_MOD = 9
EPS = 1e-6
NEG = -1e30

NSA_WIDTH = NSA_Q_HEADS * NSA_HEAD_DIM
KV_WIDTH = NSA_KV_HEADS * NSA_HEAD_DIM
MLSTM_WIDTH = MLSTM_HEADS * MLSTM_HEAD_DIM
N_GATES = 3 * NSA_Q_HEADS

LANES = 128
QT = 128
TOKEN_TILE = 512
FF_CHUNK = 256
MLSTM_L = 128
SEL_UNROLL = 6
SEL_CHAIN = 2
CMP_TILES = 2
SEL_TILES = 2
TOPK_ROWS = 32
VT_ROWS = 80
VMEM_LIMIT = 56 * 1024 * 1024

F32 = jnp.float32
BF16 = jnp.bfloat16
HIGHEST = lax.Precision.HIGHEST
LOG2E = 1.4426950408889634


def _cparams(n_axes):
    return pltpu.CompilerParams(dimension_semantics=("arbitrary",) * n_axes,
                                vmem_limit_bytes=VMEM_LIMIT)


def _dot(a, b):
    return jnp.dot(a, b, preferred_element_type=F32)


def _dot_nt(a, b):
    return lax.dot_general(a, b, (((1,), (1,)), ((), ())), preferred_element_type=F32)


def _dot_tn(a, b):
    return lax.dot_general(a, b, (((0,), (0,)), ((), ())), preferred_element_type=F32)


def _sigmoid(x):
    return 1.0 / (1.0 + jnp.exp(-x))


def _silu(x):
    return x * _sigmoid(x)


def _norm_mod(x, nw, sh, sc):
    ms = jnp.mean(x * x, axis=-1, keepdims=True)
    y = x * lax.rsqrt(ms + EPS) * nw
    return y * (1.0 + sc) + sh


def _adaln_kernel(ct_ref, w_ref, b_ref, o_ref):
    w = w_ref[...]
    for r in range(ct_ref.shape[1]):
        a_col = _silu(ct_ref[:, r:r + 1])
        o_ref[r:r + 1, :] = jnp.sum(a_col * w, axis=0, keepdims=True) + b_ref[...]


def _adaln(c_t, w_ada, b_ada):
    d, rows = c_t.shape
    n = w_ada.shape[1]
    tn = n // N_MOD
    return pl.pallas_call(
        _adaln_kernel,
        grid=(N_MOD,),
        in_specs=[pl.BlockSpec((d, rows), lambda j: (0, 0)),
                  pl.BlockSpec((d, tn), lambda j: (0, j)),
                  pl.BlockSpec((1, tn), lambda j: (0, j))],
        out_specs=pl.BlockSpec((rows, tn), lambda j: (0, j)),
        out_shape=jax.ShapeDtypeStruct((rows, n), F32),
        compiler_params=_cparams(1),
        name="adaln",
    )(c_t, w_ada, b_ada)


def _ffn_kernel(*refs, mod_rows, with_mix, mix_gate_row, with_final):
    it = iter(refs)
    x_ref = next(it)
    mod_ref = next(it)
    nw_ref = next(it)
    win_ref = next(it)
    wo_ref = next(it)
    if with_mix:
        ma_ref = next(it)
        mb_ref = next(it)
        wmix_ref = next(it)
    if with_final:
        fnw_ref = next(it)
    o_ref = next(it)

    sh_row, sc_row, g_row = mod_rows
    x = x_ref[...]
    if with_mix:
        mix_t = jnp.concatenate([ma_ref[0].astype(BF16), mb_ref[0].astype(BF16)], axis=0)
        x = x + mod_ref[0, mix_gate_row:mix_gate_row + 1, :] * _dot_tn(mix_t, wmix_ref[...])
    h = _norm_mod(x, nw_ref[...], mod_ref[0, sh_row:sh_row + 1, :], mod_ref[0, sc_row:sc_row + 1, :])
    hb = h.astype(BF16)
    d_ff = wo_ref.shape[0]
    n_chunks = d_ff // FF_CHUNK
    acc = None
    for j in range(n_chunks):
        c0 = j * FF_CHUNK
        g = _dot(hb, win_ref[:, c0:c0 + FF_CHUNK])
        u = _dot(hb, win_ref[:, d_ff + c0:d_ff + c0 + FF_CHUNK])
        act = (_silu(g) * u).astype(BF16)
        part = _dot(act, wo_ref[c0:c0 + FF_CHUNK, :])
        acc = part if acc is None else acc + part
    y = x + 0.5 * mod_ref[0, g_row:g_row + 1, :] * acc
    if with_final:
        ms = jnp.mean(y * y, axis=-1, keepdims=True)
        y = y * lax.rsqrt(ms + EPS) * fnw_ref[...]
    o_ref[...] = y


def _ffn(x2d, mod, nw, w_in_bf, w_out_bf, mod_rows, tiles_per_batch, mix=None, mix_gate_row=None, final_nw=None):
    t, d = x2d.shape
    tm = TOKEN_TILE
    d_ff = w_out_bf.shape[0]
    assert d_ff % FF_CHUNK == 0
    row = lambda i: (i, 0)
    const = lambda i: (0, 0)
    in_specs = [pl.BlockSpec((tm, d), row),
                pl.BlockSpec((1, N_MOD, d), lambda i: (i // tiles_per_batch, 0, 0)),
                pl.BlockSpec((1, d), const),
                pl.BlockSpec(w_in_bf.shape, const),
                pl.BlockSpec(w_out_bf.shape, const)]
    args = [x2d, mod, nw, w_in_bf, w_out_bf]
    if mix is not None:
        ma, mb, wmix = mix
        trow = lambda i: (i // tiles_per_batch, 0, i % tiles_per_batch)
        in_specs += [pl.BlockSpec((1, ma.shape[1], tm), trow),
                     pl.BlockSpec((1, mb.shape[1], tm), trow),
                     pl.BlockSpec(wmix.shape, const)]
        args += [ma, mb, wmix]
    if final_nw is not None:
        in_specs.append(pl.BlockSpec((1, d), const))
        args.append(final_nw)
    kern = functools.partial(_ffn_kernel, mod_rows=mod_rows, with_mix=mix is not None,
                             mix_gate_row=mix_gate_row, with_final=final_nw is not None)
    return pl.pallas_call(
        kern,
        grid=(t // tm,),
        in_specs=in_specs,
        out_specs=pl.BlockSpec((tm, d), row),
        out_shape=jax.ShapeDtypeStruct((t, d), F32),
        compiler_params=_cparams(1),
        name="ffn_mix" if mix is not None else "ffn",
    )(*args)


QPAD_W = NSA_Q_HEADS * LANES
MISC_I = N_GATES
MISC_F = N_GATES + MLSTM_HEADS
HALF_BLOCK_W = CMP_STRIDE * KV_WIDTH
B_KV = NSA_WIDTH
B_ML = B_KV + 6 * KV_WIDTH
B_MISC = B_ML + 4 * MLSTM_WIDTH


def _vt_with_ones(v):
    vt = v.T
    n = vt.shape[1]
    pad = jnp.where(lax.broadcasted_iota(jnp.int32, (VT_ROWS - NSA_HEAD_DIM, n), 0) == 0, 1.0, 0.0)
    parts = []
    for h in range(NSA_KV_HEADS):
        parts += [vt[h * NSA_HEAD_DIM:(h + 1) * NSA_HEAD_DIM, :], pad]
    return jnp.concatenate(parts, axis=0).astype(BF16)


def _inproj_kernel(x_ref, mod_ref, nw_ref, wq_ref, wkv_ref, wml_ref, wmisc_ref, b_ref,
                   qpad_ref, kc_ref, vc_ref, ksel_ref, kwin_ref, vtsel_ref, vtwin_ref,
                   ml_ref, vtm_ref, ogt_ref, misc_ref, misct_ref, rows_ref):
    tm = x_ref.shape[0]
    h = _norm_mod(x_ref[...], nw_ref[...], mod_ref[0, 3:4, :], mod_ref[0, 4:5, :])
    hb = h.astype(BF16)

    def proj(w_ref, c0, width, b0):
        return _dot(hb, w_ref[:, c0:c0 + width]) + b_ref[:, b0 + c0:b0 + c0 + width]

    def half_blocks(val, out_ref):
        rows_ref[...] = val
        pieces = [rows_ref[pl.ds(l, tm // CMP_STRIDE, stride=CMP_STRIDE), :] for l in range(CMP_STRIDE)]
        out_ref[0] = jnp.concatenate(pieces, axis=1).astype(BF16)

    scale = NSA_HEAD_DIM ** -0.5 * LOG2E
    qt = (proj(wq_ref, 0, NSA_WIDTH, 0) * scale).T.astype(BF16)
    zero_rows = jnp.zeros((NSA_HEAD_DIM, tm), BF16)
    for i in range(NSA_Q_HEADS):
        slots = [zero_rows] * NSA_KV_HEADS
        slots[i // NSA_GROUP] = qt[i * NSA_HEAD_DIM:(i + 1) * NSA_HEAD_DIM, :]
        qpad_ref[0, i * LANES:(i + 1) * LANES, :] = jnp.concatenate(slots, axis=0)
    kv_c = proj(wkv_ref, 0, 2 * KV_WIDTH, B_KV)
    half_blocks(kv_c[:, :KV_WIDTH], kc_ref)
    half_blocks(kv_c[:, KV_WIDTH:], vc_ref)
    kv_s = proj(wkv_ref, 2 * KV_WIDTH, 2 * KV_WIDTH, B_KV)
    ksel_ref[...] = kv_s[:, :KV_WIDTH].astype(BF16)
    vtsel_ref[0] = _vt_with_ones(kv_s[:, KV_WIDTH:])
    kv_w = proj(wkv_ref, 4 * KV_WIDTH, 2 * KV_WIDTH, B_KV)
    kwin_ref[...] = kv_w[:, :KV_WIDTH].astype(BF16)
    vtwin_ref[0] = _vt_with_ones(kv_w[:, KV_WIDTH:])
    for i in range(2):
        ml_ref[:, i * MLSTM_WIDTH:(i + 1) * MLSTM_WIDTH] = proj(wml_ref, i * MLSTM_WIDTH, MLSTM_WIDTH, B_ML)
    vtm_ref[0] = proj(wml_ref, 2 * MLSTM_WIDTH, MLSTM_WIDTH, B_ML).T.astype(BF16)
    ogt_ref[0] = proj(wml_ref, 3 * MLSTM_WIDTH, MLSTM_WIDTH, B_ML).T
    misc = proj(wmisc_ref, 0, LANES, B_MISC)
    misc_ref[...] = misc
    misct_ref[0] = misc.T


def _inproj(x2d, mod, nw, weights, b_pack, batch, seq):
    t, d = x2d.shape
    tm = TOKEN_TILE
    tpb = seq // tm
    row = lambda i: (i, 0)
    const = lambda i: (0, 0)
    trow = lambda i: (i // tpb, 0, i % tpb)
    hrow = lambda i: (i // tpb, i % tpb, 0)
    vt_rows = NSA_KV_HEADS * VT_ROWS
    n_half = seq // CMP_STRIDE
    out_shapes = [
        jax.ShapeDtypeStruct((batch, QPAD_W, seq), BF16),
        jax.ShapeDtypeStruct((batch, n_half, HALF_BLOCK_W), BF16),
        jax.ShapeDtypeStruct((batch, n_half, HALF_BLOCK_W), BF16),
        jax.ShapeDtypeStruct((t, KV_WIDTH), BF16),
        jax.ShapeDtypeStruct((t, KV_WIDTH), BF16),
        jax.ShapeDtypeStruct((batch, vt_rows, seq), BF16),
        jax.ShapeDtypeStruct((batch, vt_rows, seq), BF16),
        jax.ShapeDtypeStruct((t, 2 * MLSTM_WIDTH), F32),
        jax.ShapeDtypeStruct((batch, MLSTM_WIDTH, seq), BF16),
        jax.ShapeDtypeStruct((batch, MLSTM_WIDTH, seq), F32),
        jax.ShapeDtypeStruct((t, LANES), F32),
        jax.ShapeDtypeStruct((batch, LANES, seq), F32),
    ]
    out_specs = [
        pl.BlockSpec((1, QPAD_W, tm), trow),
        pl.BlockSpec((1, tm // CMP_STRIDE, HALF_BLOCK_W), hrow),
        pl.BlockSpec((1, tm // CMP_STRIDE, HALF_BLOCK_W), hrow),
        pl.BlockSpec((tm, KV_WIDTH), row),
        pl.BlockSpec((tm, KV_WIDTH), row),
        pl.BlockSpec((1, vt_rows, tm), trow),
        pl.BlockSpec((1, vt_rows, tm), trow),
        pl.BlockSpec((tm, 2 * MLSTM_WIDTH), row),
        pl.BlockSpec((1, MLSTM_WIDTH, tm), trow),
        pl.BlockSpec((1, MLSTM_WIDTH, tm), trow),
        pl.BlockSpec((tm, LANES), row),
        pl.BlockSpec((1, LANES, tm), trow),
    ]
    return pl.pallas_call(
        _inproj_kernel,
        grid=(t // tm,),
        in_specs=[pl.BlockSpec((tm, d), row),
                  pl.BlockSpec((1, N_MOD, d), lambda i: (i // tpb, 0, 0)),
                  pl.BlockSpec((1, d), const)]
                 + [pl.BlockSpec(w.shape, const) for w in weights]
                 + [pl.BlockSpec(b_pack.shape, const)],
        out_specs=out_specs,
        out_shape=out_shapes,
        scratch_shapes=[pltpu.VMEM((tm, KV_WIDTH), F32)],
        compiler_params=_cparams(1),
        name="inproj",
    )(x2d, mod, nw, *weights, b_pack)


def _split_inproj(w_in, b_in):
    d = w_in.shape[0]
    offs = np.cumsum([0, NSA_WIDTH] + [KV_WIDTH] * 6 + [N_GATES] + [MLSTM_WIDTH] * 4 + [MLSTM_HEADS] * 2)
    g0, m0, i0 = int(offs[7]), int(offs[8]), int(offs[12])
    pad = LANES - N_GATES - 2 * MLSTM_HEADS
    misc_w = jnp.concatenate([w_in[:, g0:g0 + N_GATES], w_in[:, i0:i0 + 2 * MLSTM_HEADS],
                              jnp.zeros((d, pad), w_in.dtype)], axis=1)
    misc_b = jnp.concatenate([b_in[g0:g0 + N_GATES], b_in[i0:i0 + 2 * MLSTM_HEADS], jnp.zeros((pad,), b_in.dtype)])
    weights = [w_in[:, :NSA_WIDTH].astype(BF16), w_in[:, NSA_WIDTH:g0].astype(BF16),
               w_in[:, m0:i0].astype(BF16), misc_w.astype(BF16)]
    b_pack = jnp.concatenate([b_in[:g0], b_in[m0:i0], misc_b]).reshape(1, -1).astype(F32)
    assert b_pack.shape[1] == B_MISC + LANES
    return weights, b_pack


def _compress_kernel(xk_ref, xv_ref, wa_ref, wb_ref, pe_ref, w1_ref, w2_ref, kc_ref, vtc_ref):
    n_half = xk_ref.shape[1]

    def one(x_ref, s):
        x = x_ref[0]
        a = _dot(x, wa_ref[s])
        b = _dot(x, wb_ref[s])
        pe_term = jnp.dot(pe_ref[s], w1_ref[s], preferred_element_type=F32, precision=HIGHEST)[0:1, :]
        pe2 = jnp.concatenate([pe_term] * NSA_KV_HEADS, axis=1)
        pre = a + pltpu.roll(b, n_half - 1, 0) + pe2
        hid = 0.5 * pre * (1.0 + jnp.tanh(0.7978845608028654 * (pre + 0.044715 * pre * pre * pre)))
        return _dot(hid.astype(BF16), w2_ref[s])

    kc_ref[0] = one(xk_ref, 0).astype(BF16)
    vtc_ref[0] = _vt_with_ones(one(xv_ref, 1))


def _compress(xk, xv, wa, wb, pe8, w1, w2e):
    batch, n_half, width = xk.shape
    bsel = lambda b: (b, 0, 0)
    c3 = lambda b: (0, 0, 0)
    vt_rows = NSA_KV_HEADS * VT_ROWS
    return pl.pallas_call(
        _compress_kernel,
        grid=(batch,),
        in_specs=[pl.BlockSpec((1, n_half, width), bsel), pl.BlockSpec((1, n_half, width), bsel),
                  pl.BlockSpec(wa.shape, c3), pl.BlockSpec(wb.shape, c3),
                  pl.BlockSpec(pe8.shape, c3), pl.BlockSpec(w1.shape, c3), pl.BlockSpec(w2e.shape, c3)],
        out_specs=[pl.BlockSpec((1, n_half, KV_WIDTH), bsel), pl.BlockSpec((1, vt_rows, n_half), bsel)],
        out_shape=[jax.ShapeDtypeStruct((batch, n_half, KV_WIDTH), BF16),
                   jax.ShapeDtypeStruct((batch, vt_rows, n_half), BF16)],
        compiler_params=_cparams(1),
        name="compress",
    )(xk, xv, wa, wb, pe8, w1, w2e)


def _pack_compress(pe, w1, w2):
    half = CMP_BLOCK // 2
    w1r = w1.reshape(CMP_BLOCK, NSA_HEAD_DIM, CMP_HIDDEN)
    eye = jnp.eye(NSA_KV_HEADS, dtype=w1.dtype)

    def expand(w_half):
        z = w_half[:, None, :, None, :] * eye[None, :, None, :, None]
        return z.reshape(half * KV_WIDTH, NSA_KV_HEADS * CMP_HIDDEN)

    wa = expand(w1r[:half]).astype(BF16)
    wb = expand(w1r[half:]).astype(BF16)
    w2e = (w2[None, :, None, :] * eye[:, None, :, None]).reshape(NSA_KV_HEADS * CMP_HIDDEN, KV_WIDTH).astype(BF16)
    pe8 = jnp.concatenate([pe.reshape(1, CMP_BLOCK * NSA_HEAD_DIM),
                           jnp.zeros((7, CMP_BLOCK * NSA_HEAD_DIM), pe.dtype)], axis=0)
    return wa, wb, pe8, w1, w2e


def _q_cols(q_ref, h, cols):
    return jnp.concatenate(
        [q_ref[0, (NSA_GROUP * h + g) * LANES:(NSA_GROUP * h + g + 1) * LANES, cols] for g in range(NSA_GROUP)],
        axis=1)


def _q_aug(q_ref, qs_ref, h, cols=slice(0, QT)):
    return jnp.concatenate([_q_cols(q_ref, h, cols), qs_ref[h]], axis=0)


def _slope2(head):
    return float(2.0 ** (-8.0 * (head + 1) / NSA_Q_HEADS)) * LOG2E


def _bf16_split3(x):
    parts = []
    for _ in range(3):
        p = float(np.asarray(x, np.float32).astype(BF16).astype(np.float32))
        parts.append(p)
        x = x - p
    return parts


def _init_alibi_operands(qs_ref, kpos_ref, key_stride):
    lane = lax.broadcasted_iota(jnp.int32, (QT, LANES), 1)
    row = lax.broadcasted_iota(jnp.int32, (QT, LANES), 0)
    kpos_ref[...] = jnp.where(lane < 3, row.astype(F32), 0.0).astype(BF16)
    srow = lax.broadcasted_iota(jnp.int32, (LANES, QT), 0)
    for h in range(NSA_KV_HEADS):
        for g in range(NSA_GROUP):
            hi, mid, lo = _bf16_split3(key_stride * _slope2(NSA_GROUP * h + g))
            blk = jnp.where(srow == 0, hi, jnp.where(srow == 1, mid, jnp.where(srow == 2, lo, 0.0)))
            qs_ref[h, :, g * QT:(g + 1) * QT] = blk.astype(BF16)


def _gate_row(gt_ref, head, branch, cols):
    r = head * 3 + branch
    return _sigmoid(gt_ref[0, r:r + 1, cols])


def _softmax_jobs(jobs):
    scores = [_dot(jnp.concatenate(job[1], axis=0), job[0]) for job in jobs]
    staged = []
    for (q_aug, k_tiles, masks, offs, value_groups, state), st_all in zip(jobs, scores):
        n_t = len(k_tiles)
        per_g = []
        for g in range(NSA_GROUP):
            sl = slice(g * QT, (g + 1) * QT)
            ss = []
            for u in range(n_t):
                s = st_all[u * QT:(u + 1) * QT, sl]
                ss.append(s if masks[u] is None else jnp.where(masks[u], s, NEG))
            mx = None
            for u in range(n_t):
                cand = jnp.max(ss[u], axis=0, keepdims=True) + offs[u][g]
                mx = cand if mx is None else jnp.maximum(mx, cand)
            alpha = None
            if state is None:
                m_new = mx
            else:
                m_ref, _, h = state
                m_old = m_ref[h, :, sl]
                m_new = jnp.maximum(m_old, mx)
                alpha = jnp.exp2(m_old - m_new)
                m_ref[h, :, sl] = m_new
            m_use = jnp.where(m_new < 0.5 * NEG, 0.0, m_new)
            per_g.append((ss, m_use, alpha))
        staged.append(per_g)
    ets = []
    for (q_aug, k_tiles, masks, offs, value_groups, state), per_g in zip(jobs, staged):
        n_t = len(k_tiles)
        rows = [jnp.concatenate([jnp.exp2(per_g[g][0][u] - (per_g[g][1] - offs[u][g])).astype(BF16)
                                 for g in range(NSA_GROUP)], axis=1) for u in range(n_t)]
        ets.append(jnp.concatenate(rows, axis=0))
    results = []
    for (q_aug, k_tiles, masks, offs, value_groups, state), per_g, et in zip(jobs, staged, ets):
        mats = [jnp.concatenate(grp, axis=1) for grp in value_groups]
        pv_all = _dot(jnp.concatenate(mats, axis=0), et)
        pvs, r0 = [], 0
        for mat in mats:
            pvs.append(pv_all[r0:r0 + mat.shape[0], :])
            r0 += mat.shape[0]
        if state is None:
            results.append(pvs)
        else:
            _, acc_ref, h = state
            acc_ref[h] = jnp.concatenate([pg[2] for pg in per_g], axis=1) * acc_ref[h] + pvs[0]
            results.append(None)
    return results


def _gated(gt_ref, h, branch, pv, cols=slice(0, QT)):
    inv = 1.0 / jnp.maximum(pv[NSA_HEAD_DIM:NSA_HEAD_DIM + 1, :], 1e-30)
    out = []
    for g in range(NSA_GROUP):
        sl = slice(g * QT, (g + 1) * QT)
        out.append(pv[0:NSA_HEAD_DIM, sl] * (inv[:, sl] * _gate_row(gt_ref, NSA_GROUP * h + g, branch, cols)))
    return out


def _nsa_cmp_kernel(q_ref, kc_ref, vtc_ref, gt_ref, oc_ref, sel_ref, cnt_ref,
                    qs_ref, kpos_ref, ovt_ref, pvo_ref, pvi_ref):
    ncp = kc_ref.shape[1]
    n_ct = ncp // QT
    n_sel = sel_ref.shape[3]
    tiles = range(CMP_TILES)
    starts = [(pl.program_id(1) * CMP_TILES + s) * QT for s in tiles]
    cols = [slice(s * QT, (s + 1) * QT) for s in tiles]
    units = [(s, h) for s in tiles for h in range(NSA_KV_HEADS)]
    tile_span = QT * CMP_STRIDE

    @pl.when((pl.program_id(0) == 0) & (pl.program_id(1) == 0))
    def _():
        _init_alibi_operands(qs_ref, kpos_ref, float(CMP_STRIDE))
        j_o = lax.broadcasted_iota(jnp.int32, (n_sel, ncp), 0)
        c_o = lax.broadcasted_iota(jnp.int32, (n_sel, ncp), 1)
        ov = ((c_o * CMP_STRIDE <= j_o * SEL_BLOCK + (SEL_BLOCK - 1))
              & (c_o * CMP_STRIDE + (CMP_BLOCK - 1) >= j_o * SEL_BLOCK))
        ovt_ref[...] = jnp.where(ov, 1.0, 0.0).astype(BF16)

    a0 = (lax.broadcasted_iota(jnp.int32, (QT, QT), 1)
          - CMP_STRIDE * lax.broadcasted_iota(jnp.int32, (QT, QT), 0))
    q_augs = [_q_aug(q_ref, qs_ref, h, cols[s]) for s, h in units]
    n_vis = jnp.minimum((starts[-1] + QT - CMP_BLOCK) // tile_span + 1, n_ct)

    for k in range(1, n_ct + 1):
        @pl.when(n_vis == k)
        def _(k=k):
            jobs = []
            for i, (s, h) in enumerate(units):
                k_tiles, masks, offs, vts, ovs = [], [], [], [], []
                for ci in range(k):
                    cs = slice(ci * QT, (ci + 1) * QT)
                    first_end = ci * tile_span + CMP_BLOCK - 1
                    k_tiles.append(jnp.concatenate([kc_ref[0, cs, :], kpos_ref[...]], axis=1))
                    masks.append(a0 + (starts[s] - first_end) >= 0)
                    offs.append([_slope2(NSA_GROUP * h + g) * first_end for g in range(NSA_GROUP)])
                    vts.append(vtc_ref[0, h * VT_ROWS:(h + 1) * VT_ROWS, cs])
                    ovs.append(ovt_ref[:, cs])
                jobs.append((q_augs[i], k_tiles, masks, offs, [vts, ovs], None))
            for i, (pvo, pvi) in enumerate(_softmax_jobs(jobs)):
                pvo_ref[i] = pvo
                pvi_ref[i] = pvi

    j_io = lax.broadcasted_iota(jnp.int32, (n_sel, QT), 0)
    q_io = lax.broadcasted_iota(jnp.int32, (n_sel, QT), 1)
    ones8 = jnp.ones((8, QT), BF16)

    taken = -(2.0 ** 126)
    scores = []
    for i, (s, h) in enumerate(units):
        t_j = starts[s] + q_io
        cur = t_j // SEL_BLOCK
        valid = j_io * SEL_BLOCK <= t_j
        forced = (j_io == 0) | (j_io == cur) | (j_io == cur - 1)
        pvo = pvo_ref[i]
        o_c = _gated(gt_ref, h, 0, pvo, cols[s])
        inv = 1.0 / jnp.maximum(pvo[NSA_HEAD_DIM:NSA_HEAD_DIM + 1, :], 1e-30)
        imp = None
        for g in range(NSA_GROUP):
            sl = slice(g * QT, (g + 1) * QT)
            oc_ref[0, NSA_GROUP * h + g, :, cols[s]] = o_c[g]
            term = pvi_ref[i, :, sl] * inv[:, sl]
            imp = term if imp is None else imp + term
        scores.append(jnp.where(forced, taken, jnp.where(valid, imp, -FORCE_BONUS)))

    n_cls = n_sel // TOPK_ROWS
    cls = jnp.minimum((starts[-1] + QT + SEL_BLOCK * TOPK_ROWS - 1) // (SEL_BLOCK * TOPK_ROWS), n_cls)

    for k in range(1, n_cls + 1):
        @pl.when(cls == k)
        def _(rows=k * TOPK_ROWS):
            j_sub = j_io[:rows]

            def pick(_, carry):
                out = []
                for sc in carry:
                    mx = jnp.max(sc, axis=0, keepdims=True)
                    idx = jnp.min(jnp.where(sc == mx, j_sub, n_sel), axis=0, keepdims=True)
                    out.append(jnp.where(j_sub == idx, taken, sc))
                return tuple(out)

            picked = lax.fori_loop(0, SEL_TOPK - N_FORCED, pick, tuple(sc[:rows] for sc in scores))
            for i, (s, h) in enumerate(units):
                sel = jnp.where(picked[i] <= 0.5 * taken, 1.0, 0.0)
                if rows < n_sel:
                    sel = jnp.concatenate([sel, jnp.zeros((n_sel - rows, QT), F32)], axis=0)
                sel_ref[0, h, s] = sel
                cnt = _dot_nt(ones8, sel.astype(BF16))
                cnt_ref[0, s, h] = cnt[0:1, :]


def _nsa_cmp(qpad3, kc, vtc, gates_t, batch, seq):
    nb = seq // QT
    n_sel = seq // SEL_BLOCK
    ncp = kc.shape[1]
    assert ncp % QT == 0 and nb % CMP_TILES == 0
    vt_rows = NSA_KV_HEADS * VT_ROWS
    qw = QT * CMP_TILES
    n_units = CMP_TILES * NSA_KV_HEADS
    return pl.pallas_call(
        _nsa_cmp_kernel,
        grid=(batch, nb // CMP_TILES),
        in_specs=[pl.BlockSpec((1, QPAD_W, qw), lambda b, i: (b, 0, i)),
                  pl.BlockSpec((1, ncp, KV_WIDTH), lambda b, i: (b, 0, 0)),
                  pl.BlockSpec((1, vt_rows, ncp), lambda b, i: (b, 0, 0)),
                  pl.BlockSpec((1, N_GATES, qw), lambda b, i: (b, 0, i))],
        out_specs=[pl.BlockSpec((1, NSA_Q_HEADS, NSA_HEAD_DIM, qw), lambda b, i: (b, 0, 0, i)),
                   pl.BlockSpec((1, NSA_KV_HEADS, CMP_TILES, n_sel, QT), lambda b, i: (b, 0, i, 0, 0)),
                   pl.BlockSpec((1, CMP_TILES, NSA_KV_HEADS, 1, n_sel), lambda b, i: (b, i, 0, 0, 0))],
        out_shape=[jax.ShapeDtypeStruct((batch, NSA_Q_HEADS, NSA_HEAD_DIM, seq), F32),
                   jax.ShapeDtypeStruct((batch, NSA_KV_HEADS, nb, n_sel, QT), F32),
                   jax.ShapeDtypeStruct((batch, nb, NSA_KV_HEADS, 1, n_sel), F32)],
        scratch_shapes=[pltpu.VMEM((NSA_KV_HEADS, LANES, NSA_GROUP * QT), BF16),
                        pltpu.VMEM((QT, LANES), BF16),
                        pltpu.VMEM((n_sel, ncp), BF16),
                        pltpu.VMEM((n_units, VT_ROWS, NSA_GROUP * QT), F32),
                        pltpu.VMEM((n_units, n_sel, NSA_GROUP * QT), F32)],
        compiler_params=_cparams(2),
        name="nsa_cmp",
    )(qpad3, kc, vtc, gates_t)


def _nsa_sel_kernel(count_ref, list_ref, q_ref, ksel_ref, vtsel_ref, kwin_ref, vtwin_ref, selm_ref, gt_ref,
                    oc_ref, o_ref, m_sel, acc_sel, qs_ref, kpos_ref, part_ref, kposm_ref):
    b = pl.program_id(0)
    nb = pl.num_programs(1) * SEL_TILES
    tiles = range(SEL_TILES)
    qbs = [pl.program_id(1) * SEL_TILES + s for s in tiles]
    cols = [slice(s * QT, (s + 1) * QT) for s in tiles]
    units = [(s, h) for s in tiles for h in range(NSA_KV_HEADS)]
    k_io = lax.broadcasted_iota(jnp.int32, (QT, QT), 0)
    q_io = lax.broadcasted_iota(jnp.int32, (QT, QT), 1)
    causal = q_io >= k_io
    in_window = q_io < k_io

    pen_row0 = 3
    pen_rows = 16
    assert pen_row0 + 2 * SEL_UNROLL <= pen_rows

    @pl.when((b == 0) & (pl.program_id(1) == 0))
    def _():
        _init_alibi_operands(qs_ref, kpos_ref, 1.0)
        lane = lax.broadcasted_iota(jnp.int32, (QT, LANES), 1)
        row = lax.broadcasted_iota(jnp.int32, (QT, LANES), 0)
        base = jnp.where(lane < pen_row0, row.astype(F32), 0.0)
        for u in range(SEL_UNROLL):
            ind = jnp.where(lane == pen_row0 + 2 * u, jnp.where(row < SEL_BLOCK, 1.0, 0.0),
                            jnp.where(lane == pen_row0 + 2 * u + 1, jnp.where(row >= SEL_BLOCK, 1.0, 0.0), 0.0))
            kposm_ref[u] = (base + ind).astype(BF16)

    def k_aug(k_ref, t, u=None):
        off = pl.multiple_of(t * QT, QT)
        tail = kpos_ref[...] if u is None else kposm_ref[u]
        return jnp.concatenate([k_ref[0, pl.ds(off, QT), :], tail], axis=1)

    def v_ext(vt_ref, h, t):
        return vt_ref[0, h * VT_ROWS:(h + 1) * VT_ROWS, pl.ds(pl.multiple_of(t * QT, QT), QT)]

    def tile_offs(h, t, live=None):
        pos = (t * QT).astype(F32)
        out = [_slope2(NSA_GROUP * h + g) * pos for g in range(NSA_GROUP)]
        return out if live is None else [jnp.where(live, o, NEG) for o in out]

    def penalties(s, h, t):
        mrow = selm_ref[0, h, s, pl.ds(2 * t, 2), :]
        return (mrow - 1.0) * (-NEG)

    q_tops = [_q_cols(q_ref, h, cols[s]) for s, h in units]
    q_augs = [jnp.concatenate([q_tops[i], qs_ref[h]], axis=0) for i, (s, h) in enumerate(units)]
    zero_rows = jnp.zeros((LANES - pen_rows, NSA_GROUP * QT), BF16)

    def q_aug_sel(i, h, pens):
        blk = jnp.concatenate([jnp.zeros((pen_row0, QT), F32)] + pens
                              + [jnp.zeros((pen_rows - pen_row0 - 2 * len(pens), QT), F32)], axis=0)
        ext = qs_ref[h, 0:pen_rows, :].astype(F32) + jnp.concatenate([blk] * NSA_GROUP, axis=1)
        return jnp.concatenate([q_tops[i], ext.astype(BF16), zero_rows], axis=0)

    slot = [(b * nb + qbs[s]) * NSA_KV_HEADS + h for s, h in units]
    n_tiles = [count_ref[sl] for sl in slot]

    m_sel[...] = jnp.full(m_sel.shape, NEG, F32)
    acc_sel[...] = jnp.zeros(acc_sel.shape, F32)

    def sel_jobs(trip, first, step):
        jobs = []
        for i, (s, h) in enumerate(units):
            k_tiles, masks, offs, vts, pens = [], [], [], [], []
            for u in range(SEL_UNROLL):
                if first and step == 0 and u == 0:
                    t, mask, off = qbs[s], causal, tile_offs(h, qbs[s])
                else:
                    idx = (trip * SEL_CHAIN + step) * SEL_UNROLL + u - 1
                    live = idx < n_tiles[i]
                    t = jnp.where(live, list_ref[slot[i] * nb + jnp.minimum(idx, nb - 1)], 0)
                    mask, off = None, tile_offs(h, t, live)
                k_tiles.append(k_aug(ksel_ref, t, u))
                masks.append(mask)
                offs.append(off)
                vts.append(v_ext(vtsel_ref, h, t))
                pens.append(penalties(s, h, t))
            jobs.append((q_aug_sel(i, h, pens), k_tiles, masks, offs, [vts], (m_sel, acc_sel, i)))
        return jobs

    n_win = WINDOW // QT + 1
    win_jobs = []
    for i, (s, h) in enumerate(units):
        k_tiles, masks, offs, vts = [], [], [], []
        for u in range(n_win):
            raw = qbs[s] - (n_win - 1) + u
            t = jnp.maximum(raw, 0)
            k_tiles.append(k_aug(kwin_ref, t))
            masks.append(in_window if u == 0 else causal if u == n_win - 1 else None)
            offs.append(tile_offs(h, t, raw >= 0))
            vts.append(v_ext(vtwin_ref, h, t))
        win_jobs.append((q_augs[i], k_tiles, masks, offs, [vts], None))
    jobs = sel_jobs(0, True, 0) + win_jobs
    for step in range(1, SEL_CHAIN):
        jobs += sel_jobs(0, True, step)
    results = _softmax_jobs(jobs)
    for i, (s, h) in enumerate(units):
        o_w = _gated(gt_ref, h, 2, results[len(units) + i][0], cols[s])
        for g in range(NSA_GROUP):
            head = NSA_GROUP * h + g
            part_ref[s, head] = oc_ref[0, head, :, cols[s]] + o_w[g]

    def sel_body(trip, carry):
        jobs = []
        for step in range(SEL_CHAIN):
            jobs += sel_jobs(trip, False, step)
        _softmax_jobs(jobs)
        return carry

    per_trip = SEL_UNROLL * SEL_CHAIN
    n_trips = (functools.reduce(jnp.maximum, n_tiles) + per_trip) // per_trip
    lax.fori_loop(1, n_trips, sel_body, 0)

    for i, (s, h) in enumerate(units):
        o_s = _gated(gt_ref, h, 1, acc_sel[i], cols[s])
        for g in range(NSA_GROUP):
            head = NSA_GROUP * h + g
            o_ref[0, head, :, cols[s]] = part_ref[s, head] + o_s[g]


def _tile_lists(cnt, batch, nb):
    pair = cnt.reshape(batch, nb, NSA_KV_HEADS, nb, 2).sum(axis=-1)
    p_io = lax.broadcasted_iota(jnp.int32, pair.shape, 3)
    flag = (pair > 0.5) & (p_io < lax.broadcasted_iota(jnp.int32, pair.shape, 1))
    rank = jnp.cumsum(flag.astype(jnp.int32), axis=-1) - 1
    slot_io = lax.broadcasted_iota(jnp.int32, pair.shape + (nb,), 4)
    hit = flag[..., None] & (rank[..., None] == slot_io)
    lists = jnp.sum(jnp.where(hit, p_io[..., None], 0), axis=3)
    counts = jnp.sum(flag.astype(jnp.int32), axis=-1)
    return counts.reshape(-1), lists.reshape(-1)


def _nsa_sel(counts, lists, qpad3, ksel, vtsel, kwin, vtwin, selm, gates_t, oc, batch, seq):
    nb = seq // QT
    assert nb % SEL_TILES == 0
    res = lambda b, i, c, l: (b, 0, 0)
    vt_rows = NSA_KV_HEADS * VT_ROWS
    qw = QT * SEL_TILES
    n_units = SEL_TILES * NSA_KV_HEADS
    grid_spec = pltpu.PrefetchScalarGridSpec(
        num_scalar_prefetch=2,
        grid=(batch, nb // SEL_TILES),
        in_specs=[pl.BlockSpec((1, QPAD_W, qw), lambda b, i, c, l: (b, 0, i)),
                  pl.BlockSpec((1, seq, KV_WIDTH), res),
                  pl.BlockSpec((1, vt_rows, seq), res),
                  pl.BlockSpec((1, seq, KV_WIDTH), res),
                  pl.BlockSpec((1, vt_rows, seq), res),
                  pl.BlockSpec((1, NSA_KV_HEADS, SEL_TILES, 2 * nb, QT), lambda b, i, c, l: (b, 0, i, 0, 0)),
                  pl.BlockSpec((1, N_GATES, qw), lambda b, i, c, l: (b, 0, i)),
                  pl.BlockSpec((1, NSA_Q_HEADS, NSA_HEAD_DIM, qw), lambda b, i, c, l: (b, 0, 0, i))],
        out_specs=pl.BlockSpec((1, NSA_Q_HEADS, NSA_HEAD_DIM, qw), lambda b, i, c, l: (b, 0, 0, i)),
        scratch_shapes=[pltpu.VMEM((n_units, 1, NSA_GROUP * QT), F32),
                        pltpu.VMEM((n_units, VT_ROWS, NSA_GROUP * QT), F32),
                        pltpu.VMEM((NSA_KV_HEADS, LANES, NSA_GROUP * QT), BF16),
                        pltpu.VMEM((QT, LANES), BF16),
                        pltpu.VMEM((SEL_TILES, NSA_Q_HEADS, NSA_HEAD_DIM, QT), F32),
                        pltpu.VMEM((SEL_UNROLL, QT, LANES), BF16)],
    )
    return pl.pallas_call(
        _nsa_sel_kernel,
        grid_spec=grid_spec,
        out_shape=jax.ShapeDtypeStruct((batch, NSA_Q_HEADS, NSA_HEAD_DIM, seq), F32),
        compiler_params=_cparams(2),
        name="nsa_sel",
    )(counts, lists, qpad3, ksel, vtsel, kwin, vtwin, selm, gates_t, oc)


def _mlstm_kernel(q_ref, k_ref, vt_ref, ogt_ref, misc_ref, misct_ref, cw_ref, cb_ref, nw_ref,
                  o_ref, xq_ref, xk_ref, c_ref, n_ref, m_ref):
    tm = q_ref.shape[0]
    L = MLSTM_L
    hd = MLSTM_HEAD_DIM
    halo = 8

    @pl.when(pl.program_id(1) == 0)
    def _():
        xq_ref[0:halo, :] = jnp.zeros((halo, MLSTM_WIDTH), F32)
        xk_ref[0:halo, :] = jnp.zeros((halo, MLSTM_WIDTH), F32)
        c_ref[...] = jnp.zeros_like(c_ref)
        n_ref[...] = jnp.zeros_like(n_ref)
        m_ref[...] = jnp.zeros_like(m_ref)

    def conv(x_ref, buf_ref, col0):
        buf_ref[halo:halo + tm, :] = x_ref[...]
        y = cb_ref[:, col0:col0 + MLSTM_WIDTH]
        for kk in range(CONV_WIDTH):
            r0 = halo - (CONV_WIDTH - 1) + kk
            y = y + cw_ref[kk:kk + 1, col0:col0 + MLSTM_WIDTH] * buf_ref[r0:r0 + tm, :]
        tail = buf_ref[tm:tm + halo, :]
        buf_ref[0:halo, :] = tail
        return _silu(y)

    qc = conv(q_ref, xq_ref, 0)
    kc = conv(k_ref, xk_ref, MLSTM_WIDTH) * (hd ** -0.5)

    misc = misc_ref[...]
    misct = misct_ref[0]
    lf_c = -jnp.log(1.0 + jnp.exp(-misc))
    lf_r = -jnp.log(1.0 + jnp.exp(-misct))
    row_io = lax.broadcasted_iota(jnp.int32, (L, L), 0)
    col_io = lax.broadcasted_iota(jnp.int32, (L, L), 1)
    tril = jnp.where(col_io <= row_io, 1.0, 0.0).astype(BF16)
    triu = jnp.where(row_io <= col_io, 1.0, 0.0).astype(BF16)

    def split3(x):
        hi = x.astype(BF16)
        r1 = x - hi.astype(F32)
        mid = r1.astype(BF16)
        return hi, mid, (r1 - mid.astype(F32)).astype(BF16)

    lf_c3 = split3(lf_c)
    lf_r3 = split3(lf_r)
    b_c = jnp.concatenate([sum(_dot(tril, p[ci * L:(ci + 1) * L, :]) for p in lf_c3)
                           for ci in range(tm // L)], axis=0)
    b_r = jnp.concatenate([sum(_dot(p[:, ci * L:(ci + 1) * L], triu) for p in lf_r3)
                           for ci in range(tm // L)], axis=1)

    tri_t = lax.broadcasted_iota(jnp.int32, (L, L), 0) <= lax.broadcasted_iota(jnp.int32, (L, L), 1)
    nw_cols = [jnp.broadcast_to(nw_ref[:, h * hd:(h + 1) * hd], (hd, hd)).T for h in range(MLSTM_HEADS)]

    def split3_rows(row):
        hi = row.astype(BF16)
        r1 = row - hi.astype(F32)
        mid = r1.astype(BF16)
        lo = (r1 - mid.astype(F32)).astype(BF16)
        sub = lax.broadcasted_iota(jnp.int32, (8, row.shape[1]), 0)
        out = jnp.where(sub == 0, hi.astype(F32), jnp.where(sub == 1, mid.astype(F32),
                                                            jnp.where(sub == 2, lo.astype(F32), 0.0)))
        return out.astype(BF16)

    def rows_sum3(x8):
        return x8[0:1, :] + x8[1:2, :] + x8[2:3, :]

    for ci in range(tm // L):
        r0 = ci * L
        for h in range(MLSTM_HEADS):
            cs = slice(h * hd, (h + 1) * hd)
            qb16 = qc[r0:r0 + L, cs].astype(BF16)
            kb16 = kc[r0:r0 + L, cs].astype(BF16)
            vt = vt_ref[0, cs, r0:r0 + L]
            a_col = (misc[r0:r0 + L, MISC_I + h:MISC_I + h + 1]
                     - b_c[r0:r0 + L, MISC_F + h:MISC_F + h + 1])
            brow = b_r[MISC_F + h:MISC_F + h + 1, r0:r0 + L]
            lirow = misct[MISC_I + h:MISC_I + h + 1, r0:r0 + L]
            m_prev = m_ref[h]
            ct = c_ref[h]
            nrow = n_ref[h]

            dmat = jnp.where(tri_t, brow + a_col, NEG)
            m_inter = brow + m_prev
            m_t = jnp.maximum(jnp.max(dmat, axis=0, keepdims=True), m_inter)
            w = jnp.exp(dmat - m_t) * _dot_nt(kb16, qb16)
            decay = jnp.exp(m_inter - m_t)
            num = _dot(vt, w.astype(BF16)) + decay * _dot_nt(ct.astype(BF16), qb16)
            nq = rows_sum3(_dot_nt(split3_rows(nrow), qb16))
            den = jnp.sum(w, axis=0, keepdims=True) + decay * nq
            hout = num * (1.0 / jnp.maximum(jnp.abs(den), jnp.exp(-m_t)))

            b_last = brow[:, L - 1:L]
            grow = b_last - brow + lirow
            m_new = jnp.maximum(b_last + m_prev, jnp.max(grow, axis=1, keepdims=True))
            wk = jnp.exp(grow - m_new)
            d_c = jnp.exp(b_last + m_prev - m_new)
            c_ref[h] = d_c * ct + _dot((vt.astype(F32) * wk).astype(BF16), kb16)
            n_ref[h] = d_c * nrow + rows_sum3(_dot(split3_rows(wk), kb16))
            m_ref[h] = m_new

            hn = hout * lax.rsqrt(jnp.mean(hout * hout, axis=0, keepdims=True) + EPS) * nw_cols[h]
            o_ref[0, cs, r0:r0 + L] = (hn * _sigmoid(ogt_ref[0, cs, r0:r0 + L])).astype(o_ref.dtype)


def _mlstm(ml, vt_m, og_t, misc, misc_t, conv_w, conv_b, norm_w, batch, seq):
    tm = TOKEN_TILE
    tpb = seq // tm
    col = lambda j: (lambda b, i: (b * tpb + i, j))
    const = lambda b, i: (0, 0)
    trow = lambda b, i: (b, 0, i)
    return pl.pallas_call(
        _mlstm_kernel,
        grid=(batch, tpb),
        in_specs=[pl.BlockSpec((tm, MLSTM_WIDTH), col(0)),
                  pl.BlockSpec((tm, MLSTM_WIDTH), col(1)),
                  pl.BlockSpec((1, MLSTM_WIDTH, tm), trow),
                  pl.BlockSpec((1, MLSTM_WIDTH, tm), trow),
                  pl.BlockSpec((tm, LANES), col(0)),
                  pl.BlockSpec((1, LANES, tm), trow),
                  pl.BlockSpec(conv_w.shape, const),
                  pl.BlockSpec(conv_b.shape, const),
                  pl.BlockSpec(norm_w.shape, const)],
        out_specs=pl.BlockSpec((1, MLSTM_WIDTH, tm), trow),
        out_shape=jax.ShapeDtypeStruct((batch, MLSTM_WIDTH, seq), BF16),
        scratch_shapes=[pltpu.VMEM((tm + 8, MLSTM_WIDTH), F32),
                        pltpu.VMEM((tm + 8, MLSTM_WIDTH), F32),
                        pltpu.VMEM((MLSTM_HEADS, MLSTM_HEAD_DIM, MLSTM_HEAD_DIM), F32),
                        pltpu.VMEM((MLSTM_HEADS, 1, MLSTM_HEAD_DIM), F32),
                        pltpu.VMEM((MLSTM_HEADS, 1, 1), F32)],
        compiler_params=_cparams(2),
        name="mlstm",
    )(ml, ml, vt_m, og_t, misc, misc_t, conv_w, conv_b, norm_w)


def _layer(x2d, c, batch, seq, w_ada, b_ada, norm_ffn1_w, ffn1_w_in, ffn1_w_out, norm_mix_w, w_in, b_in,
           cmp_k_pe, cmp_k_w1, cmp_k_w2, cmp_v_pe, cmp_v_w1, cmp_v_w2, conv_w, conv_b, mlstm_norm_w, w_out,
           norm_ffn2_w, ffn2_w_in, ffn2_w_out, final_nw):
    d = x2d.shape[1]
    tpb = seq // TOKEN_TILE
    mod = _adaln(c.T, w_ada, b_ada.reshape(1, -1)).reshape(batch, N_MOD, d)

    x1 = _ffn(x2d, mod, norm_ffn1_w.reshape(1, d), ffn1_w_in.astype(BF16), ffn1_w_out.astype(BF16),
              (0, 1, 2), tpb)

    weights, b_pack = _split_inproj(w_in, b_in)
    (qpad, kc, vc, ksel, kwin, vtsel, vtwin, ml, vt_m, og_t, misc, misc_t) = _inproj(
        x1, mod, norm_mix_w.reshape(1, d), weights, b_pack, batch, seq)

    pk = _pack_compress(cmp_k_pe, cmp_k_w1, cmp_k_w2)
    pv = _pack_compress(cmp_v_pe, cmp_v_w1, cmp_v_w2)
    stacked = [jnp.stack([a, bb]) for a, bb in zip(pk, pv)]
    kcmp, vtcmp = _compress(kc, vc, *stacked)

    nb = seq // QT
    qpad3 = qpad
    gates_t = misc_t
    oc, sel, cnt = _nsa_cmp(qpad3, kcmp, vtcmp, gates_t, batch, seq)
    counts, lists = _tile_lists(cnt, batch, nb)
    o_t = _nsa_sel(counts, lists, qpad3, ksel.reshape(batch, seq, KV_WIDTH), vtsel, kwin.reshape(batch, seq, KV_WIDTH),
                   vtwin, sel, gates_t, oc, batch, seq)
    o_nsa = o_t.reshape(batch, NSA_WIDTH, seq)

    o_ml = _mlstm(ml, vt_m, og_t, misc, misc_t, conv_w, conv_b.reshape(1, -1), mlstm_norm_w.reshape(1, -1), batch, seq)

    return _ffn(x1, mod, norm_ffn2_w.reshape(1, d), ffn2_w_in.astype(BF16), ffn2_w_out.astype(BF16),
                (6, 7, 8), tpb, mix=(o_nsa, o_ml, w_out.astype(BF16)), mix_gate_row=5, final_nw=final_nw)


def kernel(x, c, w_ada, b_ada, norm_ffn1_w, ffn1_w_in, ffn1_w_out, norm_mix_w, w_in, b_in, cmp_k_pe, cmp_k_w1, cmp_k_w2, cmp_v_pe, cmp_v_w1, cmp_v_w2, conv_w, conv_b, mlstm_norm_w, w_out, norm_ffn2_w, ffn2_w_in, ffn2_w_out, final_norm_w):
    batch, seq, d = x.shape
    depth = w_ada.shape[0]
    assert depth == 1 and seq % (QT * CMP_STRIDE) == 0 and seq // SEL_BLOCK >= SEL_TOPK
    y = _layer(x.reshape(batch * seq, d), c, batch, seq, w_ada[0], b_ada[0], norm_ffn1_w[0], ffn1_w_in[0],
               ffn1_w_out[0], norm_mix_w[0], w_in[0], b_in[0], cmp_k_pe[0], cmp_k_w1[0], cmp_k_w2[0],
               cmp_v_pe[0], cmp_v_w1[0], cmp_v_w2[0], conv_w[0], conv_b[0], mlstm_norm_w[0], w_out[0],
               norm_ffn2_w[0], ffn2_w_in[0], ffn2_w_out[0], final_norm_w.reshape(1, d))
    return y.reshape(batch, seq, d)
```

```python
import functools

import numpy as np
import jax
import jax.numpy as jnp
from jax import lax
from jax.experimental import pallas as pl
from jax.experimental.pallas import tpu as pltpu

NSA_Q_HEADS = 8
NSA_KV_HEADS = 2
NSA_GROUP = NSA_Q_HEADS // NSA_KV_HEADS
NSA_HEAD_DIM = 64
CMP_BLOCK = 32
CMP_STRIDE = 16
CMP_HIDDEN = 128
SEL_BLOCK = 64
SEL_TOPK = 16
WINDOW = 512
FORCE_BONUS = 1.0e4
N_FORCED = 3
assert FORCE_BONUS > NSA_GROUP
MLSTM_HEADS = 4
MLSTM_HEAD_DIM = 128
CONV_WIDTH = 4
N_MOD = 9
EPS = 1e-6
NEG = -1e30

NSA_WIDTH = NSA_Q_HEADS * NSA_HEAD_DIM
KV_WIDTH = NSA_KV_HEADS * NSA_HEAD_DIM
MLSTM_WIDTH = MLSTM_HEADS * MLSTM_HEAD_DIM
N_GATES = 3 * NSA_Q_HEADS

LANES = 128
QT = 128
TOKEN_TILE = 512
FF_CHUNK = 256
MLSTM_L = 128
SEL_UNROLL = 6
SEL_CHAIN = 2
CMP_TILES = 2
SEL_TILES = 2
TOPK_ROWS = 32
VT_ROWS = 80
VMEM_LIMIT = 56 * 1024 * 1024

F32 = jnp.float32
BF16 = jnp.bfloat16
HIGHEST = lax.Precision.HIGHEST
LOG2E = 1.4426950408889634


def _cparams(n_axes):
    return pltpu.CompilerParams(dimension_semantics=("arbitrary",) * n_axes,
                                vmem_limit_bytes=VMEM_LIMIT)


def _dot(a, b):
    return jnp.dot(a, b, preferred_element_type=F32)


def _dot_nt(a, b):
    return lax.dot_general(a, b, (((1,), (1,)), ((), ())), preferred_element_type=F32)


def _dot_tn(a, b):
    return lax.dot_general(a, b, (((0,), (0,)), ((), ())), preferred_element_type=F32)


def _sigmoid(x):
    return 1.0 / (1.0 + jnp.exp(-x))


def _silu(x):
    return x * _sigmoid(x)


def _norm_mod(x, nw, sh, sc):
    ms = jnp.mean(x * x, axis=-1, keepdims=True)
    y = x * lax.rsqrt(ms + EPS) * nw
    return y * (1.0 + sc) + sh


def _adaln_kernel(ct_ref, w_ref, b_ref, o_ref):
    w = w_ref[...]
    for r in range(ct_ref.shape[1]):
        a_col = _silu(ct_ref[:, r:r + 1])
        o_ref[r:r + 1, :] = jnp.sum(a_col * w, axis=0, keepdims=True) + b_ref[...]


def _adaln(c_t, w_ada, b_ada):
    d, rows = c_t.shape
    n = w_ada.shape[1]
    tn = n // N_MOD
    return pl.pallas_call(
        _adaln_kernel,
        grid=(N_MOD,),
        in_specs=[pl.BlockSpec((d, rows), lambda j: (0, 0)),
                  pl.BlockSpec((d, tn), lambda j: (0, j)),
                  pl.BlockSpec((1, tn), lambda j: (0, j))],
        out_specs=pl.BlockSpec((rows, tn), lambda j: (0, j)),
        out_shape=jax.ShapeDtypeStruct((rows, n), F32),
        compiler_params=_cparams(1),
        name="adaln",
    )(c_t, w_ada, b_ada)


def _ffn_kernel(*refs, mod_rows, with_mix, mix_gate_row, with_final):
    it = iter(refs)
    x_ref = next(it)
    mod_ref = next(it)
    nw_ref = next(it)
    win_ref = next(it)
    wo_ref = next(it)
    if with_mix:
        ma_ref = next(it)
        mb_ref = next(it)
        wmix_ref = next(it)
    if with_final:
        fnw_ref = next(it)
    o_ref = next(it)

    sh_row, sc_row, g_row = mod_rows
    x = x_ref[...]
    if with_mix:
        mix_t = jnp.concatenate([ma_ref[0].astype(BF16), mb_ref[0].astype(BF16)], axis=0)
        x = x + mod_ref[0, mix_gate_row:mix_gate_row + 1, :] * _dot_tn(mix_t, wmix_ref[...])
    h = _norm_mod(x, nw_ref[...], mod_ref[0, sh_row:sh_row + 1, :], mod_ref[0, sc_row:sc_row + 1, :])
    hb = h.astype(BF16)
    d_ff = wo_ref.shape[0]
    n_chunks = d_ff // FF_CHUNK
    acc = None
    for j in range(n_chunks):
        c0 = j * FF_CHUNK
        g = _dot(hb, win_ref[:, c0:c0 + FF_CHUNK])
        u = _dot(hb, win_ref[:, d_ff + c0:d_ff + c0 + FF_CHUNK])
        act = (_silu(g) * u).astype(BF16)
        part = _dot(act, wo_ref[c0:c0 + FF_CHUNK, :])
        acc = part if acc is None else acc + part
    y = x + 0.5 * mod_ref[0, g_row:g_row + 1, :] * acc
    if with_final:
        ms = jnp.mean(y * y, axis=-1, keepdims=True)
        y = y * lax.rsqrt(ms + EPS) * fnw_ref[...]
    o_ref[...] = y


def _ffn(x2d, mod, nw, w_in_bf, w_out_bf, mod_rows, tiles_per_batch, mix=None, mix_gate_row=None, final_nw=None):
    t, d = x2d.shape
    tm = TOKEN_TILE
    d_ff = w_out_bf.shape[0]
    assert d_ff % FF_CHUNK == 0
    row = lambda i: (i, 0)
    const = lambda i: (0, 0)
    in_specs = [pl.BlockSpec((tm, d), row),
                pl.BlockSpec((1, N_MOD, d), lambda i: (i // tiles_per_batch, 0, 0)),
                pl.BlockSpec((1, d), const),
                pl.BlockSpec(w_in_bf.shape, const),
                pl.BlockSpec(w_out_bf.shape, const)]
    args = [x2d, mod, nw, w_in_bf, w_out_bf]
    if mix is not None:
        ma, mb, wmix = mix
        trow = lambda i: (i // tiles_per_batch, 0, i % tiles_per_batch)
        in_specs += [pl.BlockSpec((1, ma.shape[1], tm), trow),
                     pl.BlockSpec((1, mb.shape[1], tm), trow),
                     pl.BlockSpec(wmix.shape, const)]
        args += [ma, mb, wmix]
    if final_nw is not None:
        in_specs.append(pl.BlockSpec((1, d), const))
        args.append(final_nw)
    kern = functools.partial(_ffn_kernel, mod_rows=mod_rows, with_mix=mix is not None,
                             mix_gate_row=mix_gate_row, with_final=final_nw is not None)
    return pl.pallas_call(
        kern,
        grid=(t // tm,),
        in_specs=in_specs,
        out_specs=pl.BlockSpec((tm, d), row),
        out_shape=jax.ShapeDtypeStruct((t, d), F32),
        compiler_params=_cparams(1),
        name="ffn_mix" if mix is not None else "ffn",
    )(*args)


QPAD_W = NSA_Q_HEADS * LANES
MISC_I = N_GATES
MISC_F = N_GATES + MLSTM_HEADS
HALF_BLOCK_W = CMP_STRIDE * KV_WIDTH
B_KV = NSA_WIDTH
B_ML = B_KV + 6 * KV_WIDTH
B_MISC = B_ML + 4 * MLSTM_WIDTH


def _vt_with_ones(v):
    vt = v.T
    n = vt.shape[1]
    pad = jnp.where(lax.broadcasted_iota(jnp.int32, (VT_ROWS - NSA_HEAD_DIM, n), 0) == 0, 1.0, 0.0)
    parts = []
    for h in range(NSA_KV_HEADS):
        parts += [vt[h * NSA_HEAD_DIM:(h + 1) * NSA_HEAD_DIM, :], pad]
    return jnp.concatenate(parts, axis=0).astype(BF16)


def _inproj_kernel(x_ref, mod_ref, nw_ref, w_ref, wml_ref, wmisc_ref, b_ref,
                   qpad_ref, kc_ref, vc_ref, ksel_ref, kwin_ref, vtsel_ref, vtwin_ref,
                   ml_ref, vtm_ref, ogt_ref, misc_ref, misct_ref, rows_ref):
    tm = x_ref.shape[0]
    h = _norm_mod(x_ref[...], nw_ref[...], mod_ref[0, 3:4, :], mod_ref[0, 4:5, :])
    hb = h.astype(BF16)

    def proj(w_ref, c0, width, b0):
        return _dot(hb, w_ref[:, c0:c0 + width]) + b_ref[:, b0 + c0:b0 + c0 + width]

    def half_blocks(val, out_ref):
        rows_ref[...] = val
        pieces = [rows_ref[pl.ds(l, tm // CMP_STRIDE, stride=CMP_STRIDE), :] for l in range(CMP_STRIDE)]
        out_ref[0] = jnp.concatenate(pieces, axis=1).astype(BF16)

    scale = NSA_HEAD_DIM ** -0.5 * LOG2E
    qt = (proj(w_ref, 0, NSA_WIDTH, 0) * scale).T.astype(BF16)
    zero_rows = jnp.zeros((NSA_HEAD_DIM, tm), BF16)
    for i in range(NSA_Q_HEADS):
        slots = [zero_rows] * NSA_KV_HEADS
        slots[i // NSA_GROUP] = qt[i * NSA_HEAD_DIM:(i + 1) * NSA_HEAD_DIM, :]
        qpad_ref[0, i * LANES:(i + 1) * LANES, :] = jnp.concatenate(slots, axis=0)
    kv_c = proj(w_ref, B_KV, 2 * KV_WIDTH, 0)
    half_blocks(kv_c[:, :KV_WIDTH], kc_ref)
    half_blocks(kv_c[:, KV_WIDTH:], vc_ref)
    kv_s = proj(w_ref, B_KV + 2 * KV_WIDTH, 2 * KV_WIDTH, 0)
    ksel_ref[...] = kv_s[:, :KV_WIDTH].astype(BF16)
    vtsel_ref[0] = _vt_with_ones(kv_s[:, KV_WIDTH:])
    kv_w = proj(w_ref, B_KV + 4 * KV_WIDTH, 2 * KV_WIDTH, 0)
    kwin_ref[...] = kv_w[:, :KV_WIDTH].astype(BF16)
    vtwin_ref[0] = _vt_with_ones(kv_w[:, KV_WIDTH:])
    for i in range(2):
        ml_ref[:, i * MLSTM_WIDTH:(i + 1) * MLSTM_WIDTH] = proj(wml_ref, i * MLSTM_WIDTH, MLSTM_WIDTH, B_ML)
    vtm_ref[0] = proj(wml_ref, 2 * MLSTM_WIDTH, MLSTM_WIDTH, B_ML).T.astype(BF16)
    ogt_ref[0] = proj(wml_ref, 3 * MLSTM_WIDTH, MLSTM_WIDTH, B_ML).T
    misc = proj(wmisc_ref, 0, LANES, B_MISC)
    misc_ref[...] = misc
    misct_ref[0] = misc.T


def _inproj(x2d, mod, nw, weights, b_pack, batch, seq):
    t, d = x2d.shape
    tm = TOKEN_TILE
    tpb = seq // tm
    row = lambda i: (i, 0)
    const = lambda i: (0, 0)
    trow = lambda i: (i // tpb, 0, i % tpb)
    hrow = lambda i: (i // tpb, i % tpb, 0)
    vt_rows = NSA_KV_HEADS * VT_ROWS
    n_half = seq // CMP_STRIDE
    out_shapes = [
        jax.ShapeDtypeStruct((batch, QPAD_W, seq), BF16),
        jax.ShapeDtypeStruct((batch, n_half, HALF_BLOCK_W), BF16),
        jax.ShapeDtypeStruct((batch, n_half, HALF_BLOCK_W), BF16),
        jax.ShapeDtypeStruct((t, KV_WIDTH), BF16),
        jax.ShapeDtypeStruct((t, KV_WIDTH), BF16),
        jax.ShapeDtypeStruct((batch, vt_rows, seq), BF16),
        jax.ShapeDtypeStruct((batch, vt_rows, seq), BF16),
        jax.ShapeDtypeStruct((t, 2 * MLSTM_WIDTH), F32),
        jax.ShapeDtypeStruct((batch, MLSTM_WIDTH, seq), BF16),
        jax.ShapeDtypeStruct((batch, MLSTM_WIDTH, seq), F32),
        jax.ShapeDtypeStruct((t, LANES), F32),
        jax.ShapeDtypeStruct((batch, LANES, seq), F32),
    ]
    out_specs = [
        pl.BlockSpec((1, QPAD_W, tm), trow),
        pl.BlockSpec((1, tm // CMP_STRIDE, HALF_BLOCK_W), hrow),
        pl.BlockSpec((1, tm // CMP_STRIDE, HALF_BLOCK_W), hrow),
        pl.BlockSpec((tm, KV_WIDTH), row),
        pl.BlockSpec((tm, KV_WIDTH), row),
        pl.BlockSpec((1, vt_rows, tm), trow),
        pl.BlockSpec((1, vt_rows, tm), trow),
        pl.BlockSpec((tm, 2 * MLSTM_WIDTH), row),
        pl.BlockSpec((1, MLSTM_WIDTH, tm), trow),
        pl.BlockSpec((1, MLSTM_WIDTH, tm), trow),
        pl.BlockSpec((tm, LANES), row),
        pl.BlockSpec((1, LANES, tm), trow),
    ]
    return pl.pallas_call(
        _inproj_kernel,
        grid=(t // tm,),
        in_specs=[pl.BlockSpec((tm, d), row),
                  pl.BlockSpec((1, N_MOD, d), lambda i: (i // tpb, 0, 0)),
                  pl.BlockSpec((1, d), const)]
                 + [pl.BlockSpec(w.shape, const) for w in weights]
                 + [pl.BlockSpec(b_pack.shape, const)],
        out_specs=out_specs,
        out_shape=out_shapes,
        scratch_shapes=[pltpu.VMEM((tm, KV_WIDTH), F32)],
        compiler_params=_cparams(1),
        name="inproj",
    )(x2d, mod, nw, *weights, b_pack)


def _split_inproj(w_in, b_in):
    d = w_in.shape[0]
    offs = np.cumsum([0, NSA_WIDTH] + [KV_WIDTH] * 6 + [N_GATES] + [MLSTM_WIDTH] * 4 + [MLSTM_HEADS] * 2)
    g0, m0, i0 = int(offs[7]), int(offs[8]), int(offs[12])
    pad = LANES - N_GATES - 2 * MLSTM_HEADS
    misc_w = jnp.concatenate([w_in[:, g0:g0 + N_GATES], w_in[:, i0:i0 + 2 * MLSTM_HEADS],
                              jnp.zeros((d, pad), w_in.dtype)], axis=1)
    misc_b = jnp.concatenate([b_in[g0:g0 + N_GATES], b_in[i0:i0 + 2 * MLSTM_HEADS], jnp.zeros((pad,), b_in.dtype)])
    w_bf = w_in.astype(BF16)
    weights = [w_bf, w_bf[:, m0:i0], misc_w.astype(BF16)]
    b_pack = jnp.concatenate([b_in[:g0], b_in[m0:i0], misc_b]).reshape(1, -1).astype(F32)
    assert b_pack.shape[1] == B_MISC + LANES
    return weights, b_pack


def _compress_kernel(xk_ref, xv_ref, wa_ref, wb_ref, pe_ref, w1_ref, w2_ref, kc_ref, vtc_ref):
    n_half = xk_ref.shape[1]

    def one(x_ref, s):
        x = x_ref[0]
        a = _dot(x, wa_ref[s])
        b = _dot(x, wb_ref[s])
        pe_term = jnp.dot(pe_ref[s], w1_ref[s], preferred_element_type=F32, precision=HIGHEST)[0:1, :]
        pe2 = jnp.concatenate([pe_term] * NSA_KV_HEADS, axis=1)
        pre = a + pltpu.roll(b, n_half - 1, 0) + pe2
        hid = 0.5 * pre * (1.0 + jnp.tanh(0.7978845608028654 * (pre + 0.044715 * pre * pre * pre)))
        return _dot(hid.astype(BF16), w2_ref[s])

    kc_ref[0] = one(xk_ref, 0).astype(BF16)
    vtc_ref[0] = _vt_with_ones(one(xv_ref, 1))


def _compress(xk, xv, wa, wb, pe8, w1, w2e):
    batch, n_half, width = xk.shape
    bsel = lambda b: (b, 0, 0)
    c3 = lambda b: (0, 0, 0)
    vt_rows = NSA_KV_HEADS * VT_ROWS
    return pl.pallas_call(
        _compress_kernel,
        grid=(batch,),
        in_specs=[pl.BlockSpec((1, n_half, width), bsel), pl.BlockSpec((1, n_half, width), bsel),
                  pl.BlockSpec(wa.shape, c3), pl.BlockSpec(wb.shape, c3),
                  pl.BlockSpec(pe8.shape, c3), pl.BlockSpec(w1.shape, c3), pl.BlockSpec(w2e.shape, c3)],
        out_specs=[pl.BlockSpec((1, n_half, KV_WIDTH), bsel), pl.BlockSpec((1, vt_rows, n_half), bsel)],
        out_shape=[jax.ShapeDtypeStruct((batch, n_half, KV_WIDTH), BF16),
                   jax.ShapeDtypeStruct((batch, vt_rows, n_half), BF16)],
        compiler_params=_cparams(1),
        name="compress",
    )(xk, xv, wa, wb, pe8, w1, w2e)


def _pack_compress(pe, w1, w2):
    half = CMP_BLOCK // 2
    w1r = w1.reshape(CMP_BLOCK, NSA_HEAD_DIM, CMP_HIDDEN)
    eye = jnp.eye(NSA_KV_HEADS, dtype=w1.dtype)

    def expand(w_half):
        z = w_half[:, None, :, None, :] * eye[None, :, None, :, None]
        return z.reshape(half * KV_WIDTH, NSA_KV_HEADS * CMP_HIDDEN)

    wa = expand(w1r[:half]).astype(BF16)
    wb = expand(w1r[half:]).astype(BF16)
    w2e = (w2[None, :, None, :] * eye[:, None, :, None]).reshape(NSA_KV_HEADS * CMP_HIDDEN, KV_WIDTH).astype(BF16)
    pe8 = jnp.concatenate([pe.reshape(1, CMP_BLOCK * NSA_HEAD_DIM),
                           jnp.zeros((7, CMP_BLOCK * NSA_HEAD_DIM), pe.dtype)], axis=0)
    return wa, wb, pe8, w1, w2e


def _q_cols(q_ref, h, cols):
    return jnp.concatenate(
        [q_ref[0, (NSA_GROUP * h + g) * LANES:(NSA_GROUP * h + g + 1) * LANES, cols] for g in range(NSA_GROUP)],
        axis=1)


def _q_aug(q_ref, qs_ref, h, cols=slice(0, QT)):
    return jnp.concatenate([_q_cols(q_ref, h, cols), qs_ref[h]], axis=0)


def _slope2(head):
    return float(2.0 ** (-8.0 * (head + 1) / NSA_Q_HEADS)) * LOG2E


def _bf16_split3(x):
    parts = []
    for _ in range(3):
        p = float(np.asarray(x, np.float32).astype(BF16).astype(np.float32))
        parts.append(p)
        x = x - p
    return parts


def _init_alibi_operands(qs_ref, kpos_ref, key_stride):
    lane = lax.broadcasted_iota(jnp.int32, (QT, LANES), 1)
    row = lax.broadcasted_iota(jnp.int32, (QT, LANES), 0)
    kpos_ref[...] = jnp.where(lane < 3, row.astype(F32), 0.0).astype(BF16)
    srow = lax.broadcasted_iota(jnp.int32, (LANES, QT), 0)
    for h in range(NSA_KV_HEADS):
        for g in range(NSA_GROUP):
            hi, mid, lo = _bf16_split3(key_stride * _slope2(NSA_GROUP * h + g))
            blk = jnp.where(srow == 0, hi, jnp.where(srow == 1, mid, jnp.where(srow == 2, lo, 0.0)))
            qs_ref[h, :, g * QT:(g + 1) * QT] = blk.astype(BF16)


def _gate_row(gt_ref, head, branch, cols):
    r = head * 3 + branch
    return _sigmoid(gt_ref[0, r:r + 1, cols])


def _softmax_jobs(jobs):
    scores = [_dot(jnp.concatenate(job[1], axis=0), job[0]) for job in jobs]
    staged = []
    for (q_aug, k_tiles, masks, offs, value_groups, state), st_all in zip(jobs, scores):
        n_t = len(k_tiles)
        per_g = []
        for g in range(NSA_GROUP):
            sl = slice(g * QT, (g + 1) * QT)
            ss = []
            for u in range(n_t):
                s = st_all[u * QT:(u + 1) * QT, sl]
                ss.append(s if masks[u] is None else jnp.where(masks[u], s, NEG))
            mx = None
            for u in range(n_t):
                cand = jnp.max(ss[u], axis=0, keepdims=True) + offs[u][g]
                mx = cand if mx is None else jnp.maximum(mx, cand)
            alpha = None
            if state is None:
                m_new = mx
            else:
                m_ref, _, h = state
                m_old = m_ref[h, :, sl]
                m_new = jnp.maximum(m_old, mx)
                alpha = jnp.exp2(m_old - m_new)
                m_ref[h, :, sl] = m_new
            m_use = jnp.where(m_new < 0.5 * NEG, 0.0, m_new)
            per_g.append((ss, m_use, alpha))
        staged.append(per_g)
    ets = []
    for (q_aug, k_tiles, masks, offs, value_groups, state), per_g in zip(jobs, staged):
        n_t = len(k_tiles)
        rows = [jnp.concatenate([jnp.exp2(per_g[g][0][u] - (per_g[g][1] - offs[u][g])).astype(BF16)
                                 for g in range(NSA_GROUP)], axis=1) for u in range(n_t)]
        ets.append(jnp.concatenate(rows, axis=0))
    results = []
    for (q_aug, k_tiles, masks, offs, value_groups, state), per_g, et in zip(jobs, staged, ets):
        mats = [jnp.concatenate(grp, axis=1) for grp in value_groups]
        pv_all = _dot(jnp.concatenate(mats, axis=0), et)
        pvs, r0 = [], 0
        for mat in mats:
            pvs.append(pv_all[r0:r0 + mat.shape[0], :])
            r0 += mat.shape[0]
        if state is None:
            results.append(pvs)
        else:
            _, acc_ref, h = state
            acc_ref[h] = jnp.concatenate([pg[2] for pg in per_g], axis=1) * acc_ref[h] + pvs[0]
            results.append(None)
    return results


def _gated(gt_ref, h, branch, pv, cols=slice(0, QT)):
    inv = 1.0 / jnp.maximum(pv[NSA_HEAD_DIM:NSA_HEAD_DIM + 1, :], 1e-30)
    out = []
    for g in range(NSA_GROUP):
        sl = slice(g * QT, (g + 1) * QT)
        out.append(pv[0:NSA_HEAD_DIM, sl] * (inv[:, sl] * _gate_row(gt_ref, NSA_GROUP * h + g, branch, cols)))
    return out


def _nsa_cmp_kernel(q_ref, kc_ref, vtc_ref, gt_ref, oc_ref, sel_ref, cnt_ref,
                    qs_ref, kpos_ref, ovt_ref, pvo_ref, pvi_ref):
    ncp = kc_ref.shape[1]
    n_ct = ncp // QT
    n_sel = sel_ref.shape[3]
    tiles = range(CMP_TILES)
    starts = [(pl.program_id(1) * CMP_TILES + s) * QT for s in tiles]
    cols = [slice(s * QT, (s + 1) * QT) for s in tiles]
    units = [(s, h) for s in tiles for h in range(NSA_KV_HEADS)]
    tile_span = QT * CMP_STRIDE

    @pl.when((pl.program_id(0) == 0) & (pl.program_id(1) == 0))
    def _():
        _init_alibi_operands(qs_ref, kpos_ref, float(CMP_STRIDE))
        j_o = lax.broadcasted_iota(jnp.int32, (n_sel, ncp), 0)
        c_o = lax.broadcasted_iota(jnp.int32, (n_sel, ncp), 1)
        ov = ((c_o * CMP_STRIDE <= j_o * SEL_BLOCK + (SEL_BLOCK - 1))
              & (c_o * CMP_STRIDE + (CMP_BLOCK - 1) >= j_o * SEL_BLOCK))
        ovt_ref[...] = jnp.where(ov, 1.0, 0.0).astype(BF16)

    a0 = (lax.broadcasted_iota(jnp.int32, (QT, QT), 1)
          - CMP_STRIDE * lax.broadcasted_iota(jnp.int32, (QT, QT), 0))
    q_augs = [_q_aug(q_ref, qs_ref, h, cols[s]) for s, h in units]
    n_vis = jnp.minimum((starts[-1] + QT - CMP_BLOCK) // tile_span + 1, n_ct)

    for k in range(1, n_ct + 1):
        @pl.when(n_vis == k)
        def _(k=k):
            jobs = []
            for i, (s, h) in enumerate(units):
                k_tiles, masks, offs, vts, ovs = [], [], [], [], []
                for ci in range(k):
                    cs = slice(ci * QT, (ci + 1) * QT)
                    first_end = ci * tile_span + CMP_BLOCK - 1
                    k_tiles.append(jnp.concatenate([kc_ref[0, cs, :], kpos_ref[...]], axis=1))
                    masks.append(a0 + (starts[s] - first_end) >= 0)
                    offs.append([_slope2(NSA_GROUP * h + g) * first_end for g in range(NSA_GROUP)])
                    vts.append(vtc_ref[0, h * VT_ROWS:(h + 1) * VT_ROWS, cs])
                    ovs.append(ovt_ref[:, cs])
                jobs.append((q_augs[i], k_tiles, masks, offs, [vts, ovs], None))
            for i, (pvo, pvi) in enumerate(_softmax_jobs(jobs)):
                pvo_ref[i] = pvo
                pvi_ref[i] = pvi

    j_io = lax.broadcasted_iota(jnp.int32, (n_sel, QT), 0)
    q_io = lax.broadcasted_iota(jnp.int32, (n_sel, QT), 1)
    ones8 = jnp.ones((8, QT), BF16)

    taken = -(2.0 ** 126)
    scores = []
    for i, (s, h) in enumerate(units):
        t_j = starts[s] + q_io
        cur = t_j // SEL_BLOCK
        valid = j_io * SEL_BLOCK <= t_j
        forced = (j_io == 0) | (j_io == cur) | (j_io == cur - 1)
        pvo = pvo_ref[i]
        o_c = _gated(gt_ref, h, 0, pvo, cols[s])
        inv = 1.0 / jnp.maximum(pvo[NSA_HEAD_DIM:NSA_HEAD_DIM + 1, :], 1e-30)
        imp = None
        for g in range(NSA_GROUP):
            sl = slice(g * QT, (g + 1) * QT)
            oc_ref[0, NSA_GROUP * h + g, :, cols[s]] = o_c[g]
            term = pvi_ref[i, :, sl] * inv[:, sl]
            imp = term if imp is None else imp + term
        scores.append(jnp.where(forced, taken, jnp.where(valid, imp, -FORCE_BONUS)))

    n_cls = n_sel // TOPK_ROWS
    cls = jnp.minimum((starts[-1] + QT + SEL_BLOCK * TOPK_ROWS - 1) // (SEL_BLOCK * TOPK_ROWS), n_cls)

    for k in range(1, n_cls + 1):
        @pl.when(cls == k)
        def _(rows=k * TOPK_ROWS):
            j_sub = j_io[:rows]

            def pick(_, carry):
                out = []
                for sc in carry:
                    mx = jnp.max(sc, axis=0, keepdims=True)
                    idx = jnp.min(jnp.where(sc == mx, j_sub, n_sel), axis=0, keepdims=True)
                    out.append(jnp.where(j_sub == idx, taken, sc))
                return tuple(out)

            picked = lax.fori_loop(0, SEL_TOPK - N_FORCED, pick, tuple(sc[:rows] for sc in scores))
            for i, (s, h) in enumerate(units):
                sel = jnp.where(picked[i] <= 0.5 * taken, 1.0, 0.0)
                if rows < n_sel:
                    sel = jnp.concatenate([sel, jnp.zeros((n_sel - rows, QT), F32)], axis=0)
                sel_ref[0, h, s] = sel
                cnt = _dot_nt(ones8, sel.astype(BF16))
                cnt_ref[0, s, h] = cnt[0:1, :]


def _nsa_cmp(qpad3, kc, vtc, gates_t, batch, seq):
    nb = seq // QT
    n_sel = seq // SEL_BLOCK
    ncp = kc.shape[1]
    assert ncp % QT == 0 and nb % CMP_TILES == 0
    vt_rows = NSA_KV_HEADS * VT_ROWS
    qw = QT * CMP_TILES
    n_units = CMP_TILES * NSA_KV_HEADS
    return pl.pallas_call(
        _nsa_cmp_kernel,
        grid=(batch, nb // CMP_TILES),
        in_specs=[pl.BlockSpec((1, QPAD_W, qw), lambda b, i: (b, 0, i)),
                  pl.BlockSpec((1, ncp, KV_WIDTH), lambda b, i: (b, 0, 0)),
                  pl.BlockSpec((1, vt_rows, ncp), lambda b, i: (b, 0, 0)),
                  pl.BlockSpec((1, N_GATES, qw), lambda b, i: (b, 0, i))],
        out_specs=[pl.BlockSpec((1, NSA_Q_HEADS, NSA_HEAD_DIM, qw), lambda b, i: (b, 0, 0, i)),
                   pl.BlockSpec((1, NSA_KV_HEADS, CMP_TILES, n_sel, QT), lambda b, i: (b, 0, i, 0, 0)),
                   pl.BlockSpec((1, CMP_TILES, NSA_KV_HEADS, 1, n_sel), lambda b, i: (b, i, 0, 0, 0))],
        out_shape=[jax.ShapeDtypeStruct((batch, NSA_Q_HEADS, NSA_HEAD_DIM, seq), F32),
                   jax.ShapeDtypeStruct((batch, NSA_KV_HEADS, nb, n_sel, QT), F32),
                   jax.ShapeDtypeStruct((batch, nb, NSA_KV_HEADS, 1, n_sel), F32)],
        scratch_shapes=[pltpu.VMEM((NSA_KV_HEADS, LANES, NSA_GROUP * QT), BF16),
                        pltpu.VMEM((QT, LANES), BF16),
                        pltpu.VMEM((n_sel, ncp), BF16),
                        pltpu.VMEM((n_units, VT_ROWS, NSA_GROUP * QT), F32),
                        pltpu.VMEM((n_units, n_sel, NSA_GROUP * QT), F32)],
        compiler_params=_cparams(2),
        name="nsa_cmp",
    )(qpad3, kc, vtc, gates_t)


def _nsa_sel_kernel(count_ref, list_ref, q_ref, ksel_ref, vtsel_ref, kwin_ref, vtwin_ref, selm_ref, gt_ref,
                    oc_ref, o_ref, m_sel, acc_sel, qs_ref, kpos_ref, part_ref, kposm_ref):
    b = pl.program_id(0)
    nb = pl.num_programs(1) * SEL_TILES
    tiles = range(SEL_TILES)
    qbs = [pl.program_id(1) * SEL_TILES + s for s in tiles]
    cols = [slice(s * QT, (s + 1) * QT) for s in tiles]
    units = [(s, h) for s in tiles for h in range(NSA_KV_HEADS)]
    k_io = lax.broadcasted_iota(jnp.int32, (QT, QT), 0)
    q_io = lax.broadcasted_iota(jnp.int32, (QT, QT), 1)
    causal = q_io >= k_io
    in_window = q_io < k_io

    pen_row0 = 3
    pen_rows = 16
    assert pen_row0 + 2 * SEL_UNROLL <= pen_rows

    @pl.when((b == 0) & (pl.program_id(1) == 0))
    def _():
        _init_alibi_operands(qs_ref, kpos_ref, 1.0)
        lane = lax.broadcasted_iota(jnp.int32, (QT, LANES), 1)
        row = lax.broadcasted_iota(jnp.int32, (QT, LANES), 0)
        base = jnp.where(lane < pen_row0, row.astype(F32), 0.0)
        for u in range(SEL_UNROLL):
            ind = jnp.where(lane == pen_row0 + 2 * u, jnp.where(row < SEL_BLOCK, 1.0, 0.0),
                            jnp.where(lane == pen_row0 + 2 * u + 1, jnp.where(row >= SEL_BLOCK, 1.0, 0.0), 0.0))
            kposm_ref[u] = (base + ind).astype(BF16)

    def k_aug(k_ref, t, u=None):
        off = pl.multiple_of(t * QT, QT)
        tail = kpos_ref[...] if u is None else kposm_ref[u]
        return jnp.concatenate([k_ref[0, pl.ds(off, QT), :], tail], axis=1)

    def v_ext(vt_ref, h, t):
        return vt_ref[0, h * VT_ROWS:(h + 1) * VT_ROWS, pl.ds(pl.multiple_of(t * QT, QT), QT)]

    def tile_offs(h, t, live=None):
        pos = (t * QT).astype(F32)
        out = [_slope2(NSA_GROUP * h + g) * pos for g in range(NSA_GROUP)]
        return out if live is None else [jnp.where(live, o, NEG) for o in out]

    def penalties(s, h, t):
        mrow = selm_ref[0, h, s, pl.ds(2 * t, 2), :]
        return (mrow - 1.0) * (-NEG)

    q_tops = [_q_cols(q_ref, h, cols[s]) for s, h in units]
    q_augs = [jnp.concatenate([q_tops[i], qs_ref[h]], axis=0) for i, (s, h) in enumerate(units)]
    zero_rows = jnp.zeros((LANES - pen_rows, NSA_GROUP * QT), BF16)

    def q_aug_sel(i, h, pens):
        blk = jnp.concatenate([jnp.zeros((pen_row0, QT), F32)] + pens
                              + [jnp.zeros((pen_rows - pen_row0 - 2 * len(pens), QT), F32)], axis=0)
        ext = qs_ref[h, 0:pen_rows, :].astype(F32) + jnp.concatenate([blk] * NSA_GROUP, axis=1)
        return jnp.concatenate([q_tops[i], ext.astype(BF16), zero_rows], axis=0)

    slot = [(b * nb + qbs[s]) * NSA_KV_HEADS + h for s, h in units]
    n_tiles = [count_ref[sl] for sl in slot]

    m_sel[...] = jnp.full(m_sel.shape, NEG, F32)
    acc_sel[...] = jnp.zeros(acc_sel.shape, F32)

    def sel_jobs(trip, first, step):
        jobs = []
        for i, (s, h) in enumerate(units):
            k_tiles, masks, offs, vts, pens = [], [], [], [], []
            for u in range(SEL_UNROLL):
                if first and step == 0 and u == 0:
                    t, mask, off = qbs[s], causal, tile_offs(h, qbs[s])
                else:
                    idx = (trip * SEL_CHAIN + step) * SEL_UNROLL + u - 1
                    live = idx < n_tiles[i]
                    t = jnp.where(live, list_ref[slot[i] * nb + jnp.minimum(idx, nb - 1)], 0)
                    mask, off = None, tile_offs(h, t, live)
                k_tiles.append(k_aug(ksel_ref, t, u))
                masks.append(mask)
                offs.append(off)
                vts.append(v_ext(vtsel_ref, h, t))
                pens.append(penalties(s, h, t))
            jobs.append((q_aug_sel(i, h, pens), k_tiles, masks, offs, [vts], (m_sel, acc_sel, i)))
        return jobs

    n_win = WINDOW // QT + 1
    win_jobs = []
    for i, (s, h) in enumerate(units):
        k_tiles, masks, offs, vts = [], [], [], []
        for u in range(n_win):
            raw = qbs[s] - (n_win - 1) + u
            t = jnp.maximum(raw, 0)
            k_tiles.append(k_aug(kwin_ref, t))
            masks.append(in_window if u == 0 else causal if u == n_win - 1 else None)
            offs.append(tile_offs(h, t, raw >= 0))
            vts.append(v_ext(vtwin_ref, h, t))
        win_jobs.append((q_augs[i], k_tiles, masks, offs, [vts], None))
    jobs = sel_jobs(0, True, 0) + win_jobs
    for step in range(1, SEL_CHAIN):
        jobs += sel_jobs(0, True, step)
    results = _softmax_jobs(jobs)
    for i, (s, h) in enumerate(units):
        o_w = _gated(gt_ref, h, 2, results[len(units) + i][0], cols[s])
        for g in range(NSA_GROUP):
            head = NSA_GROUP * h + g
            part_ref[s, head] = oc_ref[0, head, :, cols[s]] + o_w[g]

    def sel_body(trip, carry):
        jobs = []
        for step in range(SEL_CHAIN):
            jobs += sel_jobs(trip, False, step)
        _softmax_jobs(jobs)
        return carry

    per_trip = SEL_UNROLL * SEL_CHAIN
    n_trips = (functools.reduce(jnp.maximum, n_tiles) + per_trip) // per_trip
    lax.fori_loop(1, n_trips, sel_body, 0)

    for i, (s, h) in enumerate(units):
        o_s = _gated(gt_ref, h, 1, acc_sel[i], cols[s])
        for g in range(NSA_GROUP):
            head = NSA_GROUP * h + g
            o_ref[0, head, :, cols[s]] = part_ref[s, head] + o_s[g]


def _tile_lists(cnt, batch, nb):
    pair = cnt.reshape(batch, nb, NSA_KV_HEADS, nb, 2).sum(axis=-1)
    p_io = lax.broadcasted_iota(jnp.int32, pair.shape, 3)
    flag = (pair > 0.5) & (p_io < lax.broadcasted_iota(jnp.int32, pair.shape, 1))
    rank = jnp.cumsum(flag.astype(jnp.int32), axis=-1) - 1
    slot_io = lax.broadcasted_iota(jnp.int32, pair.shape + (nb,), 4)
    hit = flag[..., None] & (rank[..., None] == slot_io)
    lists = jnp.sum(jnp.where(hit, p_io[..., None], 0), axis=3)
    counts = jnp.sum(flag.astype(jnp.int32), axis=-1)
    return counts.reshape(-1), lists.reshape(-1)


def _nsa_sel(counts, lists, qpad3, ksel, vtsel, kwin, vtwin, selm, gates_t, oc, batch, seq):
    nb = seq // QT
    assert nb % SEL_TILES == 0
    res = lambda b, i, c, l: (b, 0, 0)
    vt_rows = NSA_KV_HEADS * VT_ROWS
    qw = QT * SEL_TILES
    n_units = SEL_TILES * NSA_KV_HEADS
    grid_spec = pltpu.PrefetchScalarGridSpec(
        num_scalar_prefetch=2,
        grid=(batch, nb // SEL_TILES),
        in_specs=[pl.BlockSpec((1, QPAD_W, qw), lambda b, i, c, l: (b, 0, i)),
                  pl.BlockSpec((1, seq, KV_WIDTH), res),
                  pl.BlockSpec((1, vt_rows, seq), res),
                  pl.BlockSpec((1, seq, KV_WIDTH), res),
                  pl.BlockSpec((1, vt_rows, seq), res),
                  pl.BlockSpec((1, NSA_KV_HEADS, SEL_TILES, 2 * nb, QT), lambda b, i, c, l: (b, 0, i, 0, 0)),
                  pl.BlockSpec((1, N_GATES, qw), lambda b, i, c, l: (b, 0, i)),
                  pl.BlockSpec((1, NSA_Q_HEADS, NSA_HEAD_DIM, qw), lambda b, i, c, l: (b, 0, 0, i))],
        out_specs=pl.BlockSpec((1, NSA_Q_HEADS, NSA_HEAD_DIM, qw), lambda b, i, c, l: (b, 0, 0, i)),
        scratch_shapes=[pltpu.VMEM((n_units, 1, NSA_GROUP * QT), F32),
                        pltpu.VMEM((n_units, VT_ROWS, NSA_GROUP * QT), F32),
                        pltpu.VMEM((NSA_KV_HEADS, LANES, NSA_GROUP * QT), BF16),
                        pltpu.VMEM((QT, LANES), BF16),
                        pltpu.VMEM((SEL_TILES, NSA_Q_HEADS, NSA_HEAD_DIM, QT), F32),
                        pltpu.VMEM((SEL_UNROLL, QT, LANES), BF16)],
    )
    return pl.pallas_call(
        _nsa_sel_kernel,
        grid_spec=grid_spec,
        out_shape=jax.ShapeDtypeStruct((batch, NSA_Q_HEADS, NSA_HEAD_DIM, seq), F32),
        compiler_params=_cparams(2),
        name="nsa_sel",
    )(counts, lists, qpad3, ksel, vtsel, kwin, vtwin, selm, gates_t, oc)


def _mlstm_kernel(q_ref, k_ref, vt_ref, ogt_ref, misc_ref, misct_ref, cw_ref, cb_ref, nw_ref,
                  o_ref, xq_ref, xk_ref, c_ref, n_ref, m_ref):
    tm = q_ref.shape[0]
    L = MLSTM_L
    hd = MLSTM_HEAD_DIM
    halo = 8

    @pl.when(pl.program_id(1) == 0)
    def _():
        xq_ref[0:halo, :] = jnp.zeros((halo, MLSTM_WIDTH), F32)
        xk_ref[0:halo, :] = jnp.zeros((halo, MLSTM_WIDTH), F32)
        c_ref[...] = jnp.zeros_like(c_ref)
        n_ref[...] = jnp.zeros_like(n_ref)
        m_ref[...] = jnp.zeros_like(m_ref)

    def conv(x_ref, buf_ref, col0):
        buf_ref[halo:halo + tm, :] = x_ref[...]
        y = cb_ref[:, col0:col0 + MLSTM_WIDTH]
        for kk in range(CONV_WIDTH):
            r0 = halo - (CONV_WIDTH - 1) + kk
            y = y + cw_ref[kk:kk + 1, col0:col0 + MLSTM_WIDTH] * buf_ref[r0:r0 + tm, :]
        tail = buf_ref[tm:tm + halo, :]
        buf_ref[0:halo, :] = tail
        return _silu(y)

    qc = conv(q_ref, xq_ref, 0)
    kc = conv(k_ref, xk_ref, MLSTM_WIDTH) * (hd ** -0.5)

    misc = misc_ref[...]
    misct = misct_ref[0]
    lf_c = -jnp.log(1.0 + jnp.exp(-misc))
    lf_r = -jnp.log(1.0 + jnp.exp(-misct))
    row_io = lax.broadcasted_iota(jnp.int32, (L, L), 0)
    col_io = lax.broadcasted_iota(jnp.int32, (L, L), 1)
    tril = jnp.where(col_io <= row_io, 1.0, 0.0).astype(BF16)
    triu = jnp.where(row_io <= col_io, 1.0, 0.0).astype(BF16)

    def split3(x):
        hi = x.astype(BF16)
        r1 = x - hi.astype(F32)
        mid = r1.astype(BF16)
        return hi, mid, (r1 - mid.astype(F32)).astype(BF16)

    lf_c3 = split3(lf_c)
    lf_r3 = split3(lf_r)
    b_c = jnp.concatenate([sum(_dot(tril, p[ci * L:(ci + 1) * L, :]) for p in lf_c3)
                           for ci in range(tm // L)], axis=0)
    b_r = jnp.concatenate([sum(_dot(p[:, ci * L:(ci + 1) * L], triu) for p in lf_r3)
                           for ci in range(tm // L)], axis=1)

    tri_t = lax.broadcasted_iota(jnp.int32, (L, L), 0) <= lax.broadcasted_iota(jnp.int32, (L, L), 1)
    nw_cols = [jnp.broadcast_to(nw_ref[:, h * hd:(h + 1) * hd], (hd, hd)).T for h in range(MLSTM_HEADS)]

    def split3_rows(row):
        hi = row.astype(BF16)
        r1 = row - hi.astype(F32)
        mid = r1.astype(BF16)
        lo = (r1 - mid.astype(F32)).astype(BF16)
        sub = lax.broadcasted_iota(jnp.int32, (8, row.shape[1]), 0)
        out = jnp.where(sub == 0, hi.astype(F32), jnp.where(sub == 1, mid.astype(F32),
                                                            jnp.where(sub == 2, lo.astype(F32), 0.0)))
        return out.astype(BF16)

    def rows_sum3(x8):
        return x8[0:1, :] + x8[1:2, :] + x8[2:3, :]

    for ci in range(tm // L):
        r0 = ci * L
        for h in range(MLSTM_HEADS):
            cs = slice(h * hd, (h + 1) * hd)
            qb16 = qc[r0:r0 + L, cs].astype(BF16)
            kb16 = kc[r0:r0 + L, cs].astype(BF16)
            vt = vt_ref[0, cs, r0:r0 + L]
            a_col = (misc[r0:r0 + L, MISC_I + h:MISC_I + h + 1]
                     - b_c[r0:r0 + L, MISC_F + h:MISC_F + h + 1])
            brow = b_r[MISC_F + h:MISC_F + h + 1, r0:r0 + L]
            lirow = misct[MISC_I + h:MISC_I + h + 1, r0:r0 + L]
            m_prev = m_ref[h]
            ct = c_ref[h]
            nrow = n_ref[h]

            dmat = jnp.where(tri_t, brow + a_col, NEG)
            m_inter = brow + m_prev
            m_t = jnp.maximum(jnp.max(dmat, axis=0, keepdims=True), m_inter)
            w = jnp.exp(dmat - m_t) * _dot_nt(kb16, qb16)
            decay = jnp.exp(m_inter - m_t)
            num = _dot(vt, w.astype(BF16)) + decay * _dot_nt(ct.astype(BF16), qb16)
            nq = rows_sum3(_dot_nt(split3_rows(nrow), qb16))
            den = jnp.sum(w, axis=0, keepdims=True) + decay * nq
            hout = num * (1.0 / jnp.maximum(jnp.abs(den), jnp.exp(-m_t)))

            b_last = brow[:, L - 1:L]
            grow = b_last - brow + lirow
            m_new = jnp.maximum(b_last + m_prev, jnp.max(grow, axis=1, keepdims=True))
            wk = jnp.exp(grow - m_new)
            d_c = jnp.exp(b_last + m_prev - m_new)
            c_ref[h] = d_c * ct + _dot((vt.astype(F32) * wk).astype(BF16), kb16)
            n_ref[h] = d_c * nrow + rows_sum3(_dot(split3_rows(wk), kb16))
            m_ref[h] = m_new

            hn = hout * lax.rsqrt(jnp.mean(hout * hout, axis=0, keepdims=True) + EPS) * nw_cols[h]
            o_ref[0, cs, r0:r0 + L] = (hn * _sigmoid(ogt_ref[0, cs, r0:r0 + L])).astype(o_ref.dtype)


def _mlstm(ml, vt_m, og_t, misc, misc_t, conv_w, conv_b, norm_w, batch, seq):
    tm = TOKEN_TILE
    tpb = seq // tm
    col = lambda j: (lambda b, i: (b * tpb + i, j))
    const = lambda b, i: (0, 0)
    trow = lambda b, i: (b, 0, i)
    return pl.pallas_call(
        _mlstm_kernel,
        grid=(batch, tpb),
        in_specs=[pl.BlockSpec((tm, MLSTM_WIDTH), col(0)),
                  pl.BlockSpec((tm, MLSTM_WIDTH), col(1)),
                  pl.BlockSpec((1, MLSTM_WIDTH, tm), trow),
                  pl.BlockSpec((1, MLSTM_WIDTH, tm), trow),
                  pl.BlockSpec((tm, LANES), col(0)),
                  pl.BlockSpec((1, LANES, tm), trow),
                  pl.BlockSpec(conv_w.shape, const),
                  pl.BlockSpec(conv_b.shape, const),
                  pl.BlockSpec(norm_w.shape, const)],
        out_specs=pl.BlockSpec((1, MLSTM_WIDTH, tm), trow),
        out_shape=jax.ShapeDtypeStruct((batch, MLSTM_WIDTH, seq), BF16),
        scratch_shapes=[pltpu.VMEM((tm + 8, MLSTM_WIDTH), F32),
                        pltpu.VMEM((tm + 8, MLSTM_WIDTH), F32),
                        pltpu.VMEM((MLSTM_HEADS, MLSTM_HEAD_DIM, MLSTM_HEAD_DIM), F32),
                        pltpu.VMEM((MLSTM_HEADS, 1, MLSTM_HEAD_DIM), F32),
                        pltpu.VMEM((MLSTM_HEADS, 1, 1), F32)],
        compiler_params=_cparams(2),
        name="mlstm",
    )(ml, ml, vt_m, og_t, misc, misc_t, conv_w, conv_b, norm_w)


def _layer(x2d, c, batch, seq, w_ada, b_ada, norm_ffn1_w, ffn1_w_in, ffn1_w_out, norm_mix_w, w_in, b_in,
           cmp_k_pe, cmp_k_w1, cmp_k_w2, cmp_v_pe, cmp_v_w1, cmp_v_w2, conv_w, conv_b, mlstm_norm_w, w_out,
           norm_ffn2_w, ffn2_w_in, ffn2_w_out, final_nw):
    d = x2d.shape[1]
    tpb = seq // TOKEN_TILE
    mod = _adaln(c.T, w_ada, b_ada.reshape(1, -1)).reshape(batch, N_MOD, d)

    x1 = _ffn(x2d, mod, norm_ffn1_w.reshape(1, d), ffn1_w_in.astype(BF16), ffn1_w_out.astype(BF16),
              (0, 1, 2), tpb)

    weights, b_pack = _split_inproj(w_in, b_in)
    (qpad, kc, vc, ksel, kwin, vtsel, vtwin, ml, vt_m, og_t, misc, misc_t) = _inproj(
        x1, mod, norm_mix_w.reshape(1, d), weights, b_pack, batch, seq)

    pk = _pack_compress(cmp_k_pe, cmp_k_w1, cmp_k_w2)
    pv = _pack_compress(cmp_v_pe, cmp_v_w1, cmp_v_w2)
    stacked = [jnp.stack([a, bb]) for a, bb in zip(pk, pv)]
    kcmp, vtcmp = _compress(kc, vc, *stacked)

    nb = seq // QT
    qpad3 = qpad
    gates_t = misc_t
    oc, sel, cnt = _nsa_cmp(qpad3, kcmp, vtcmp, gates_t, batch, seq)
    counts, lists = _tile_lists(cnt, batch, nb)
    o_t = _nsa_sel(counts, lists, qpad3, ksel.reshape(batch, seq, KV_WIDTH), vtsel, kwin.reshape(batch, seq, KV_WIDTH),
                   vtwin, sel, gates_t, oc, batch, seq)
    o_nsa = o_t.reshape(batch, NSA_WIDTH, seq)

    o_ml = _mlstm(ml, vt_m, og_t, misc, misc_t, conv_w, conv_b.reshape(1, -1), mlstm_norm_w.reshape(1, -1), batch, seq)

    return _ffn(x1, mod, norm_ffn2_w.reshape(1, d), ffn2_w_in.astype(BF16), ffn2_w_out.astype(BF16),
                (6, 7, 8), tpb, mix=(o_nsa, o_ml, w_out.astype(BF16)), mix_gate_row=5, final_nw=final_nw)


def kernel(x, c, w_ada, b_ada, norm_ffn1_w, ffn1_w_in, ffn1_w_out, norm_mix_w, w_in, b_in, cmp_k_pe, cmp_k_w1, cmp_k_w2, cmp_v_pe, cmp_v_w1, cmp_v_w2, conv_w, conv_b, mlstm_norm_w, w_out, norm_ffn2_w, ffn2_w_in, ffn2_w_out, final_norm_w):
    batch, seq, d = x.shape
    depth = w_ada.shape[0]
    assert depth == 1 and seq % (QT * CMP_STRIDE) == 0 and seq // SEL_BLOCK >= SEL_TOPK
    y = _layer(x.reshape(batch * seq, d), c, batch, seq, w_ada[0], b_ada[0], norm_ffn1_w[0], ffn1_w_in[0],
               ffn1_w_out[0], norm_mix_w[0], w_in[0], b_in[0], cmp_k_pe[0], cmp_k_w1[0], cmp_k_w2[0],
               cmp_v_pe[0], cmp_v_w1[0], cmp_v_w2[0], conv_w[0], conv_b[0], mlstm_norm_w[0], w_out[0],
               norm_ffn2_w[0], ffn2_w_in[0], ffn2_w_out[0], final_norm_w.reshape(1, d))
    return y.reshape(batch, seq, d)
```

```python
import functools

import numpy as np
import jax
import jax.numpy as jnp
from jax import lax
from jax.experimental import pallas as pl
from jax.experimental.pallas import tpu as pltpu

NSA_Q_HEADS = 8
NSA_KV_HEADS = 2
NSA_GROUP = NSA_Q_HEADS // NSA_KV_HEADS
NSA_HEAD_DIM = 64
CMP_BLOCK = 32
CMP_STRIDE = 16
CMP_HIDDEN = 128
SEL_BLOCK = 64
SEL_TOPK = 16
WINDOW = 512
FORCE_BONUS = 1.0e4
N_FORCED = 3
assert FORCE_BONUS > NSA_GROUP
MLSTM_HEADS = 4
MLSTM_HEAD_DIM = 128
CONV_WIDTH = 4
N_MOD = 9
EPS = 1e-6
NEG = -1e30

NSA_WIDTH = NSA_Q_HEADS * NSA_HEAD_DIM
KV_WIDTH = NSA_KV_HEADS * NSA_HEAD_DIM
MLSTM_WIDTH = MLSTM_HEADS * MLSTM_HEAD_DIM
N_GATES = 3 * NSA_Q_HEADS

LANES = 128
QT = 128
TOKEN_TILE = 512
FF_CHUNK = 256
MLSTM_L = 128
SEL_UNROLL = 6
SEL_CHAIN = 2
CMP_TILES = 2
SEL_TILES = 2
TOPK_ROWS = 32
VT_ROWS = 80
VMEM_LIMIT = 56 * 1024 * 1024

F32 = jnp.float32
BF16 = jnp.bfloat16
HIGHEST = lax.Precision.HIGHEST
LOG2E = 1.4426950408889634


def _cparams(n_axes):
    return pltpu.CompilerParams(dimension_semantics=("arbitrary",) * n_axes,
                                vmem_limit_bytes=VMEM_LIMIT)


def _dot(a, b):
    return jnp.dot(a, b, preferred_element_type=F32)


def _dot_nt(a, b):
    return lax.dot_general(a, b, (((1,), (1,)), ((), ())), preferred_element_type=F32)


def _dot_tn(a, b):
    return lax.dot_general(a, b, (((0,), (0,)), ((), ())), preferred_element_type=F32)


def _sigmoid(x):
    return 1.0 / (1.0 + jnp.exp(-x))


def _silu(x):
    return x * _sigmoid(x)


def _norm_mod(x, nw, sh, sc):
    ms = jnp.mean(x * x, axis=-1, keepdims=True)
    y = x * lax.rsqrt(ms + EPS) * nw
    return y * (1.0 + sc) + sh


def _adaln_kernel(ct_ref, w_ref, b_ref, o_ref):
    w = w_ref[...]
    for r in range(ct_ref.shape[1]):
        a_col = _silu(ct_ref[:, r:r + 1])
        o_ref[r:r + 1, :] = jnp.sum(a_col * w, axis=0, keepdims=True) + b_ref[...]


def _adaln(c_t, w_ada, b_ada):
    d, rows = c_t.shape
    n = w_ada.shape[1]
    tn = n // N_MOD
    return pl.pallas_call(
        _adaln_kernel,
        grid=(N_MOD,),
        in_specs=[pl.BlockSpec((d, rows), lambda j: (0, 0)),
                  pl.BlockSpec((d, tn), lambda j: (0, j)),
                  pl.BlockSpec((1, tn), lambda j: (0, j))],
        out_specs=pl.BlockSpec((rows, tn), lambda j: (0, j)),
        out_shape=jax.ShapeDtypeStruct((rows, n), F32),
        compiler_params=_cparams(1),
        name="adaln",
    )(c_t, w_ada, b_ada)


def _ffn_kernel(*refs, mod_rows, with_mix, mix_gate_row, with_final):
    it = iter(refs)
    x_ref = next(it)
    mod_ref = next(it)
    nw_ref = next(it)
    win_ref = next(it)
    wo_ref = next(it)
    if with_mix:
        ma_ref = next(it)
        mb_ref = next(it)
        wmix_ref = next(it)
    if with_final:
        fnw_ref = next(it)
    o_ref = next(it)

    sh_row, sc_row, g_row = mod_rows
    x = x_ref[...]
    if with_mix:
        mix_t = jnp.concatenate([ma_ref[0].astype(BF16), mb_ref[0].astype(BF16)], axis=0)
        x = x + mod_ref[0, mix_gate_row:mix_gate_row + 1, :] * _dot_tn(mix_t, wmix_ref[...])
    h = _norm_mod(x, nw_ref[...], mod_ref[0, sh_row:sh_row + 1, :], mod_ref[0, sc_row:sc_row + 1, :])
    hb = h.astype(BF16)
    d_ff = wo_ref.shape[0]
    n_chunks = d_ff // FF_CHUNK
    acc = None
    for j in range(n_chunks):
        c0 = j * FF_CHUNK
        g = _dot(hb, win_ref[:, c0:c0 + FF_CHUNK].astype(BF16))
        u = _dot(hb, win_ref[:, d_ff + c0:d_ff + c0 + FF_CHUNK].astype(BF16))
        act = (_silu(g) * u).astype(BF16)
        part = _dot(act, wo_ref[c0:c0 + FF_CHUNK, :].astype(BF16))
        acc = part if acc is None else acc + part
    y = x + 0.5 * mod_ref[0, g_row:g_row + 1, :] * acc
    if with_final:
        ms = jnp.mean(y * y, axis=-1, keepdims=True)
        y = y * lax.rsqrt(ms + EPS) * fnw_ref[...]
    o_ref[...] = y


def _ffn(x2d, mod, nw, w_in, w_out, mod_rows, tiles_per_batch, mix=None, mix_gate_row=None, final_nw=None):
    t, d = x2d.shape
    tm = TOKEN_TILE
    d_ff = w_out.shape[0]
    assert d_ff % FF_CHUNK == 0
    row = lambda i: (i, 0)
    const = lambda i: (0, 0)
    in_specs = [pl.BlockSpec((tm, d), row),
                pl.BlockSpec((1, N_MOD, d), lambda i: (i // tiles_per_batch, 0, 0)),
                pl.BlockSpec((1, d), const),
                pl.BlockSpec(w_in.shape, const, pipeline_mode=pl.Buffered(1)),
                pl.BlockSpec(w_out.shape, const, pipeline_mode=pl.Buffered(1))]
    args = [x2d, mod, nw, w_in, w_out]
    if mix is not None:
        ma, mb, wmix = mix
        trow = lambda i: (i // tiles_per_batch, 0, i % tiles_per_batch)
        in_specs += [pl.BlockSpec((1, ma.shape[1], tm), trow),
                     pl.BlockSpec((1, mb.shape[1], tm), trow),
                     pl.BlockSpec(wmix.shape, const)]
        args += [ma, mb, wmix]
    if final_nw is not None:
        in_specs.append(pl.BlockSpec((1, d), const))
        args.append(final_nw)
    kern = functools.partial(_ffn_kernel, mod_rows=mod_rows, with_mix=mix is not None,
                             mix_gate_row=mix_gate_row, with_final=final_nw is not None)
    return pl.pallas_call(
        kern,
        grid=(t // tm,),
        in_specs=in_specs,
        out_specs=pl.BlockSpec((tm, d), row),
        out_shape=jax.ShapeDtypeStruct((t, d), F32),
        compiler_params=_cparams(1),
        name="ffn_mix" if mix is not None else "ffn",
    )(*args)


QPAD_W = NSA_Q_HEADS * LANES
MISC_I = N_GATES
MISC_F = N_GATES + MLSTM_HEADS
HALF_BLOCK_W = CMP_STRIDE * KV_WIDTH
B_KV = NSA_WIDTH
B_ML = B_KV + 6 * KV_WIDTH
B_MISC = B_ML + 4 * MLSTM_WIDTH


def _vt_with_ones(v):
    vt = v.T
    n = vt.shape[1]
    pad = jnp.where(lax.broadcasted_iota(jnp.int32, (VT_ROWS - NSA_HEAD_DIM, n), 0) == 0, 1.0, 0.0)
    parts = []
    for h in range(NSA_KV_HEADS):
        parts += [vt[h * NSA_HEAD_DIM:(h + 1) * NSA_HEAD_DIM, :], pad]
    return jnp.concatenate(parts, axis=0).astype(BF16)


def _inproj_kernel(x_ref, mod_ref, nw_ref, w_ref, wml_ref, wmisc_ref, b_ref,
                   qpad_ref, kc_ref, vc_ref, ksel_ref, kwin_ref, vtsel_ref, vtwin_ref,
                   ml_ref, vtm_ref, ogt_ref, misc_ref, misct_ref, rows_ref):
    tm = x_ref.shape[0]
    h = _norm_mod(x_ref[...], nw_ref[...], mod_ref[0, 3:4, :], mod_ref[0, 4:5, :])
    hb = h.astype(BF16)

    def proj(w_ref, c0, width, b0):
        return _dot(hb, w_ref[:, c0:c0 + width]) + b_ref[:, b0 + c0:b0 + c0 + width]

    def half_blocks(val, out_ref):
        rows_ref[...] = val
        pieces = [rows_ref[pl.ds(l, tm // CMP_STRIDE, stride=CMP_STRIDE), :] for l in range(CMP_STRIDE)]
        out_ref[0] = jnp.concatenate(pieces, axis=1).astype(BF16)

    scale = NSA_HEAD_DIM ** -0.5 * LOG2E
    qt = (proj(w_ref, 0, NSA_WIDTH, 0) * scale).T.astype(BF16)
    zero_rows = jnp.zeros((NSA_HEAD_DIM, tm), BF16)
    for i in range(NSA_Q_HEADS):
        slots = [zero_rows] * NSA_KV_HEADS
        slots[i // NSA_GROUP] = qt[i * NSA_HEAD_DIM:(i + 1) * NSA_HEAD_DIM, :]
        qpad_ref[0, i * LANES:(i + 1) * LANES, :] = jnp.concatenate(slots, axis=0)
    kv_c = proj(w_ref, B_KV, 2 * KV_WIDTH, 0)
    half_blocks(kv_c[:, :KV_WIDTH], kc_ref)
    half_blocks(kv_c[:, KV_WIDTH:], vc_ref)
    kv_s = proj(w_ref, B_KV + 2 * KV_WIDTH, 2 * KV_WIDTH, 0)
    ksel_ref[...] = kv_s[:, :KV_WIDTH].astype(BF16)
    vtsel_ref[0] = _vt_with_ones(kv_s[:, KV_WIDTH:])
    kv_w = proj(w_ref, B_KV + 4 * KV_WIDTH, 2 * KV_WIDTH, 0)
    kwin_ref[...] = kv_w[:, :KV_WIDTH].astype(BF16)
    vtwin_ref[0] = _vt_with_ones(kv_w[:, KV_WIDTH:])
    for i in range(2):
        ml_ref[:, i * MLSTM_WIDTH:(i + 1) * MLSTM_WIDTH] = proj(wml_ref, i * MLSTM_WIDTH, MLSTM_WIDTH, B_ML)
    vtm_ref[0] = proj(wml_ref, 2 * MLSTM_WIDTH, MLSTM_WIDTH, B_ML).T.astype(BF16)
    ogt_ref[0] = proj(wml_ref, 3 * MLSTM_WIDTH, MLSTM_WIDTH, B_ML).T
    misc = proj(wmisc_ref, 0, LANES, B_MISC)
    misc_ref[...] = misc
    misct_ref[0] = misc.T


def _inproj(x2d, mod, nw, weights, b_pack, batch, seq):
    t, d = x2d.shape
    tm = TOKEN_TILE
    tpb = seq // tm
    row = lambda i: (i, 0)
    const = lambda i: (0, 0)
    trow = lambda i: (i // tpb, 0, i % tpb)
    hrow = lambda i: (i // tpb, i % tpb, 0)
    vt_rows = NSA_KV_HEADS * VT_ROWS
    n_half = seq // CMP_STRIDE
    out_shapes = [
        jax.ShapeDtypeStruct((batch, QPAD_W, seq), BF16),
        jax.ShapeDtypeStruct((batch, n_half, HALF_BLOCK_W), BF16),
        jax.ShapeDtypeStruct((batch, n_half, HALF_BLOCK_W), BF16),
        jax.ShapeDtypeStruct((t, KV_WIDTH), BF16),
        jax.ShapeDtypeStruct((t, KV_WIDTH), BF16),
        jax.ShapeDtypeStruct((batch, vt_rows, seq), BF16),
        jax.ShapeDtypeStruct((batch, vt_rows, seq), BF16),
        jax.ShapeDtypeStruct((t, 2 * MLSTM_WIDTH), F32),
        jax.ShapeDtypeStruct((batch, MLSTM_WIDTH, seq), BF16),
        jax.ShapeDtypeStruct((batch, MLSTM_WIDTH, seq), F32),
        jax.ShapeDtypeStruct((t, LANES), F32),
        jax.ShapeDtypeStruct((batch, LANES, seq), F32),
    ]
    out_specs = [
        pl.BlockSpec((1, QPAD_W, tm), trow),
        pl.BlockSpec((1, tm // CMP_STRIDE, HALF_BLOCK_W), hrow),
        pl.BlockSpec((1, tm // CMP_STRIDE, HALF_BLOCK_W), hrow),
        pl.BlockSpec((tm, KV_WIDTH), row),
        pl.BlockSpec((tm, KV_WIDTH), row),
        pl.BlockSpec((1, vt_rows, tm), trow),
        pl.BlockSpec((1, vt_rows, tm), trow),
        pl.BlockSpec((tm, 2 * MLSTM_WIDTH), row),
        pl.BlockSpec((1, MLSTM_WIDTH, tm), trow),
        pl.BlockSpec((1, MLSTM_WIDTH, tm), trow),
        pl.BlockSpec((tm, LANES), row),
        pl.BlockSpec((1, LANES, tm), trow),
    ]
    return pl.pallas_call(
        _inproj_kernel,
        grid=(t // tm,),
        in_specs=[pl.BlockSpec((tm, d), row),
                  pl.BlockSpec((1, N_MOD, d), lambda i: (i // tpb, 0, 0)),
                  pl.BlockSpec((1, d), const)]
                 + [pl.BlockSpec(w.shape, const) for w in weights]
                 + [pl.BlockSpec(b_pack.shape, const)],
        out_specs=out_specs,
        out_shape=out_shapes,
        scratch_shapes=[pltpu.VMEM((tm, KV_WIDTH), F32)],
        compiler_params=_cparams(1),
        name="inproj",
    )(x2d, mod, nw, *weights, b_pack)


def _split_inproj(w_in, b_in):
    d = w_in.shape[0]
    offs = np.cumsum([0, NSA_WIDTH] + [KV_WIDTH] * 6 + [N_GATES] + [MLSTM_WIDTH] * 4 + [MLSTM_HEADS] * 2)
    g0, m0, i0 = int(offs[7]), int(offs[8]), int(offs[12])
    pad = LANES - N_GATES - 2 * MLSTM_HEADS
    misc_w = jnp.concatenate([w_in[:, g0:g0 + N_GATES], w_in[:, i0:i0 + 2 * MLSTM_HEADS],
                              jnp.zeros((d, pad), w_in.dtype)], axis=1)
    misc_b = jnp.concatenate([b_in[g0:g0 + N_GATES], b_in[i0:i0 + 2 * MLSTM_HEADS], jnp.zeros((pad,), b_in.dtype)])
    w_bf = w_in.astype(BF16)
    weights = [w_bf, w_bf[:, m0:i0], misc_w.astype(BF16)]
    b_pack = jnp.concatenate([b_in[:g0], b_in[m0:i0], misc_b]).reshape(1, -1).astype(F32)
    assert b_pack.shape[1] == B_MISC + LANES
    return weights, b_pack


def _compress_kernel(xk_ref, xv_ref, wa_ref, wb_ref, pe_ref, w1_ref, w2_ref, kc_ref, vtc_ref):
    n_half = xk_ref.shape[1]

    def one(x_ref, s):
        x = x_ref[0]
        a = _dot(x, wa_ref[s])
        b = _dot(x, wb_ref[s])
        pe_term = jnp.dot(pe_ref[s], w1_ref[s], preferred_element_type=F32, precision=HIGHEST)[0:1, :]
        pe2 = jnp.concatenate([pe_term] * NSA_KV_HEADS, axis=1)
        pre = a + pltpu.roll(b, n_half - 1, 0) + pe2
        hid = 0.5 * pre * (1.0 + jnp.tanh(0.7978845608028654 * (pre + 0.044715 * pre * pre * pre)))
        return _dot(hid.astype(BF16), w2_ref[s])

    kc_ref[0] = one(xk_ref, 0).astype(BF16)
    vtc_ref[0] = _vt_with_ones(one(xv_ref, 1))


def _compress(xk, xv, wa, wb, pe8, w1, w2e):
    batch, n_half, width = xk.shape
    bsel = lambda b: (b, 0, 0)
    c3 = lambda b: (0, 0, 0)
    vt_rows = NSA_KV_HEADS * VT_ROWS
    return pl.pallas_call(
        _compress_kernel,
        grid=(batch,),
        in_specs=[pl.BlockSpec((1, n_half, width), bsel), pl.BlockSpec((1, n_half, width), bsel),
                  pl.BlockSpec(wa.shape, c3), pl.BlockSpec(wb.shape, c3),
                  pl.BlockSpec(pe8.shape, c3), pl.BlockSpec(w1.shape, c3), pl.BlockSpec(w2e.shape, c3)],
        out_specs=[pl.BlockSpec((1, n_half, KV_WIDTH), bsel), pl.BlockSpec((1, vt_rows, n_half), bsel)],
        out_shape=[jax.ShapeDtypeStruct((batch, n_half, KV_WIDTH), BF16),
                   jax.ShapeDtypeStruct((batch, vt_rows, n_half), BF16)],
        compiler_params=_cparams(1),
        name="compress",
    )(xk, xv, wa, wb, pe8, w1, w2e)


def _pack_compress(pe, w1, w2):
    half = CMP_BLOCK // 2
    w1r = w1.reshape(CMP_BLOCK, NSA_HEAD_DIM, CMP_HIDDEN)
    eye = jnp.eye(NSA_KV_HEADS, dtype=w1.dtype)

    def expand(w_half):
        z = w_half[:, None, :, None, :] * eye[None, :, None, :, None]
        return z.reshape(half * KV_WIDTH, NSA_KV_HEADS * CMP_HIDDEN)

    wa = expand(w1r[:half]).astype(BF16)
    wb = expand(w1r[half:]).astype(BF16)
    w2e = (w2[None, :, None, :] * eye[:, None, :, None]).reshape(NSA_KV_HEADS * CMP_HIDDEN, KV_WIDTH).astype(BF16)
    pe8 = jnp.concatenate([pe.reshape(1, CMP_BLOCK * NSA_HEAD_DIM),
                           jnp.zeros((7, CMP_BLOCK * NSA_HEAD_DIM), pe.dtype)], axis=0)
    return wa, wb, pe8, w1, w2e


def _q_cols(q_ref, h, cols):
    return jnp.concatenate(
        [q_ref[0, (NSA_GROUP * h + g) * LANES:(NSA_GROUP * h + g + 1) * LANES, cols] for g in range(NSA_GROUP)],
        axis=1)


def _q_aug(q_ref, qs_ref, h, cols=slice(0, QT)):
    return jnp.concatenate([_q_cols(q_ref, h, cols), qs_ref[h]], axis=0)


def _slope2(head):
    return float(2.0 ** (-8.0 * (head + 1) / NSA_Q_HEADS)) * LOG2E


def _bf16_split3(x):
    parts = []
    for _ in range(3):
        p = float(np.asarray(x, np.float32).astype(BF16).astype(np.float32))
        parts.append(p)
        x = x - p
    return parts


def _init_alibi_operands(qs_ref, kpos_ref, key_stride):
    lane = lax.broadcasted_iota(jnp.int32, (QT, LANES), 1)
    row = lax.broadcasted_iota(jnp.int32, (QT, LANES), 0)
    kpos_ref[...] = jnp.where(lane < 3, row.astype(F32), 0.0).astype(BF16)
    srow = lax.broadcasted_iota(jnp.int32, (LANES, QT), 0)
    for h in range(NSA_KV_HEADS):
        for g in range(NSA_GROUP):
            hi, mid, lo = _bf16_split3(key_stride * _slope2(NSA_GROUP * h + g))
            blk = jnp.where(srow == 0, hi, jnp.where(srow == 1, mid, jnp.where(srow == 2, lo, 0.0)))
            qs_ref[h, :, g * QT:(g + 1) * QT] = blk.astype(BF16)


def _gate_row(gt_ref, head, branch, cols):
    r = head * 3 + branch
    return _sigmoid(gt_ref[0, r:r + 1, cols])


def _softmax_jobs(jobs):
    scores = [_dot(jnp.concatenate(job[1], axis=0), job[0]) for job in jobs]
    staged = []
    for (q_aug, k_tiles, masks, offs, value_groups, state), st_all in zip(jobs, scores):
        n_t = len(k_tiles)
        per_g = []
        for g in range(NSA_GROUP):
            sl = slice(g * QT, (g + 1) * QT)
            ss = []
            for u in range(n_t):
                s = st_all[u * QT:(u + 1) * QT, sl]
                ss.append(s if masks[u] is None else jnp.where(masks[u], s, NEG))
            mx = None
            for u in range(n_t):
                cand = jnp.max(ss[u], axis=0, keepdims=True) + offs[u][g]
                mx = cand if mx is None else jnp.maximum(mx, cand)
            alpha = None
            if state is None:
                m_new = mx
            else:
                m_ref, _, h = state
                m_old = m_ref[h, :, sl]
                m_new = jnp.maximum(m_old, mx)
                alpha = jnp.exp2(m_old - m_new)
                m_ref[h, :, sl] = m_new
            m_use = jnp.where(m_new < 0.5 * NEG, 0.0, m_new)
            per_g.append((ss, m_use, alpha))
        staged.append(per_g)
    ets = []
    for (q_aug, k_tiles, masks, offs, value_groups, state), per_g in zip(jobs, staged):
        n_t = len(k_tiles)
        rows = [jnp.concatenate([jnp.exp2(per_g[g][0][u] - (per_g[g][1] - offs[u][g])).astype(BF16)
                                 for g in range(NSA_GROUP)], axis=1) for u in range(n_t)]
        ets.append(jnp.concatenate(rows, axis=0))
    results = []
    for (q_aug, k_tiles, masks, offs, value_groups, state), per_g, et in zip(jobs, staged, ets):
        mats = [jnp.concatenate(grp, axis=1) for grp in value_groups]
        pv_all = _dot(jnp.concatenate(mats, axis=0), et)
        pvs, r0 = [], 0
        for mat in mats:
            pvs.append(pv_all[r0:r0 + mat.shape[0], :])
            r0 += mat.shape[0]
        if state is None:
            results.append(pvs)
        else:
            _, acc_ref, h = state
            acc_ref[h] = jnp.concatenate([pg[2] for pg in per_g], axis=1) * acc_ref[h] + pvs[0]
            results.append(None)
    return results


def _gated(gt_ref, h, branch, pv, cols=slice(0, QT)):
    inv = 1.0 / jnp.maximum(pv[NSA_HEAD_DIM:NSA_HEAD_DIM + 1, :], 1e-30)
    out = []
    for g in range(NSA_GROUP):
        sl = slice(g * QT, (g + 1) * QT)
        out.append(pv[0:NSA_HEAD_DIM, sl] * (inv[:, sl] * _gate_row(gt_ref, NSA_GROUP * h + g, branch, cols)))
    return out


def _nsa_cmp_kernel(q_ref, kc_ref, vtc_ref, gt_ref, oc_ref, sel_ref, cnt_ref,
                    qs_ref, kpos_ref, ovt_ref, pvo_ref, pvi_ref):
    ncp = kc_ref.shape[1]
    n_ct = ncp // QT
    n_sel = sel_ref.shape[3]
    tiles = range(CMP_TILES)
    starts = [(pl.program_id(1) * CMP_TILES + s) * QT for s in tiles]
    cols = [slice(s * QT, (s + 1) * QT) for s in tiles]
    units = [(s, h) for s in tiles for h in range(NSA_KV_HEADS)]
    tile_span = QT * CMP_STRIDE

    @pl.when((pl.program_id(0) == 0) & (pl.program_id(1) == 0))
    def _():
        _init_alibi_operands(qs_ref, kpos_ref, float(CMP_STRIDE))
        j_o = lax.broadcasted_iota(jnp.int32, (n_sel, ncp), 0)
        c_o = lax.broadcasted_iota(jnp.int32, (n_sel, ncp), 1)
        ov = ((c_o * CMP_STRIDE <= j_o * SEL_BLOCK + (SEL_BLOCK - 1))
              & (c_o * CMP_STRIDE + (CMP_BLOCK - 1) >= j_o * SEL_BLOCK))
        ovt_ref[...] = jnp.where(ov, 1.0, 0.0).astype(BF16)

    a0 = (lax.broadcasted_iota(jnp.int32, (QT, QT), 1)
          - CMP_STRIDE * lax.broadcasted_iota(jnp.int32, (QT, QT), 0))
    q_augs = [_q_aug(q_ref, qs_ref, h, cols[s]) for s, h in units]
    n_vis = jnp.minimum((starts[-1] + QT - CMP_BLOCK) // tile_span + 1, n_ct)

    for k in range(1, n_ct + 1):
        @pl.when(n_vis == k)
        def _(k=k):
            jobs = []
            for i, (s, h) in enumerate(units):
                k_tiles, masks, offs, vts, ovs = [], [], [], [], []
                for ci in range(k):
                    cs = slice(ci * QT, (ci + 1) * QT)
                    first_end = ci * tile_span + CMP_BLOCK - 1
                    k_tiles.append(jnp.concatenate([kc_ref[0, cs, :], kpos_ref[...]], axis=1))
                    masks.append(a0 + (starts[s] - first_end) >= 0)
                    offs.append([_slope2(NSA_GROUP * h + g) * first_end for g in range(NSA_GROUP)])
                    vts.append(vtc_ref[0, h * VT_ROWS:(h + 1) * VT_ROWS, cs])
                    ovs.append(ovt_ref[:, cs])
                jobs.append((q_augs[i], k_tiles, masks, offs, [vts, ovs], None))
            for i, (pvo, pvi) in enumerate(_softmax_jobs(jobs)):
                pvo_ref[i] = pvo
                pvi_ref[i] = pvi

    j_io = lax.broadcasted_iota(jnp.int32, (n_sel, QT), 0)
    q_io = lax.broadcasted_iota(jnp.int32, (n_sel, QT), 1)
    ones8 = jnp.ones((8, QT), BF16)

    taken = -(2.0 ** 126)
    scores = []
    for i, (s, h) in enumerate(units):
        t_j = starts[s] + q_io
        cur = t_j // SEL_BLOCK
        valid = j_io * SEL_BLOCK <= t_j
        forced = (j_io == 0) | (j_io == cur) | (j_io == cur - 1)
        pvo = pvo_ref[i]
        o_c = _gated(gt_ref, h, 0, pvo, cols[s])
        inv = 1.0 / jnp.maximum(pvo[NSA_HEAD_DIM:NSA_HEAD_DIM + 1, :], 1e-30)
        imp = None
        for g in range(NSA_GROUP):
            sl = slice(g * QT, (g + 1) * QT)
            oc_ref[0, NSA_GROUP * h + g, :, cols[s]] = o_c[g]
            term = pvi_ref[i, :, sl] * inv[:, sl]
            imp = term if imp is None else imp + term
        scores.append(jnp.where(forced, taken, jnp.where(valid, imp, -FORCE_BONUS)))

    n_cls = n_sel // TOPK_ROWS
    cls = jnp.minimum((starts[-1] + QT + SEL_BLOCK * TOPK_ROWS - 1) // (SEL_BLOCK * TOPK_ROWS), n_cls)

    for k in range(1, n_cls + 1):
        @pl.when(cls == k)
        def _(rows=k * TOPK_ROWS):
            j_sub = j_io[:rows]

            def pick(_, carry):
                out = []
                for sc in carry:
                    mx = jnp.max(sc, axis=0, keepdims=True)
                    idx = jnp.min(jnp.where(sc == mx, j_sub, n_sel), axis=0, keepdims=True)
                    out.append(jnp.where(j_sub == idx, taken, sc))
                return tuple(out)

            picked = lax.fori_loop(0, SEL_TOPK - N_FORCED, pick, tuple(sc[:rows] for sc in scores))
            for i, (s, h) in enumerate(units):
                sel = jnp.where(picked[i] <= 0.5 * taken, 1.0, 0.0)
                if rows < n_sel:
                    sel = jnp.concatenate([sel, jnp.zeros((n_sel - rows, QT), F32)], axis=0)
                sel_ref[0, h, s] = sel
                cnt = _dot_nt(ones8, sel.astype(BF16))
                cnt_ref[0, s, h] = cnt[0:1, :]


def _nsa_cmp(qpad3, kc, vtc, gates_t, batch, seq):
    nb = seq // QT
    n_sel = seq // SEL_BLOCK
    ncp = kc.shape[1]
    assert ncp % QT == 0 and nb % CMP_TILES == 0
    vt_rows = NSA_KV_HEADS * VT_ROWS
    qw = QT * CMP_TILES
    n_units = CMP_TILES * NSA_KV_HEADS
    return pl.pallas_call(
        _nsa_cmp_kernel,
        grid=(batch, nb // CMP_TILES),
        in_specs=[pl.BlockSpec((1, QPAD_W, qw), lambda b, i: (b, 0, i)),
                  pl.BlockSpec((1, ncp, KV_WIDTH), lambda b, i: (b, 0, 0)),
                  pl.BlockSpec((1, vt_rows, ncp), lambda b, i: (b, 0, 0)),
                  pl.BlockSpec((1, N_GATES, qw), lambda b, i: (b, 0, i))],
        out_specs=[pl.BlockSpec((1, NSA_Q_HEADS, NSA_HEAD_DIM, qw), lambda b, i: (b, 0, 0, i)),
                   pl.BlockSpec((1, NSA_KV_HEADS, CMP_TILES, n_sel, QT), lambda b, i: (b, 0, i, 0, 0)),
                   pl.BlockSpec((1, CMP_TILES, NSA_KV_HEADS, 1, n_sel), lambda b, i: (b, i, 0, 0, 0))],
        out_shape=[jax.ShapeDtypeStruct((batch, NSA_Q_HEADS, NSA_HEAD_DIM, seq), F32),
                   jax.ShapeDtypeStruct((batch, NSA_KV_HEADS, nb, n_sel, QT), F32),
                   jax.ShapeDtypeStruct((batch, nb, NSA_KV_HEADS, 1, n_sel), F32)],
        scratch_shapes=[pltpu.VMEM((NSA_KV_HEADS, LANES, NSA_GROUP * QT), BF16),
                        pltpu.VMEM((QT, LANES), BF16),
                        pltpu.VMEM((n_sel, ncp), BF16),
                        pltpu.VMEM((n_units, VT_ROWS, NSA_GROUP * QT), F32),
                        pltpu.VMEM((n_units, n_sel, NSA_GROUP * QT), F32)],
        compiler_params=_cparams(2),
        name="nsa_cmp",
    )(qpad3, kc, vtc, gates_t)


def _nsa_sel_kernel(count_ref, list_ref, q_ref, ksel_ref, vtsel_ref, kwin_ref, vtwin_ref, selm_ref, gt_ref,
                    oc_ref, o_ref, m_sel, acc_sel, qs_ref, kpos_ref, part_ref, kposm_ref):
    b = pl.program_id(0)
    nb = pl.num_programs(1) * SEL_TILES
    tiles = range(SEL_TILES)
    qbs = [pl.program_id(1) * SEL_TILES + s for s in tiles]
    cols = [slice(s * QT, (s + 1) * QT) for s in tiles]
    units = [(s, h) for s in tiles for h in range(NSA_KV_HEADS)]
    k_io = lax.broadcasted_iota(jnp.int32, (QT, QT), 0)
    q_io = lax.broadcasted_iota(jnp.int32, (QT, QT), 1)
    causal = q_io >= k_io
    in_window = q_io < k_io

    pen_row0 = 3
    pen_rows = 16
    assert pen_row0 + 2 * SEL_UNROLL <= pen_rows

    @pl.when((b == 0) & (pl.program_id(1) == 0))
    def _():
        _init_alibi_operands(qs_ref, kpos_ref, 1.0)
        lane = lax.broadcasted_iota(jnp.int32, (QT, LANES), 1)
        row = lax.broadcasted_iota(jnp.int32, (QT, LANES), 0)
        base = jnp.where(lane < pen_row0, row.astype(F32), 0.0)
        for u in range(SEL_UNROLL):
            ind = jnp.where(lane == pen_row0 + 2 * u, jnp.where(row < SEL_BLOCK, 1.0, 0.0),
                            jnp.where(lane == pen_row0 + 2 * u + 1, jnp.where(row >= SEL_BLOCK, 1.0, 0.0), 0.0))
            kposm_ref[u] = (base + ind).astype(BF16)

    def k_aug(k_ref, t, u=None):
        off = pl.multiple_of(t * QT, QT)
        tail = kpos_ref[...] if u is None else kposm_ref[u]
        return jnp.concatenate([k_ref[0, pl.ds(off, QT), :], tail], axis=1)

    def v_ext(vt_ref, h, t):
        return vt_ref[0, h * VT_ROWS:(h + 1) * VT_ROWS, pl.ds(pl.multiple_of(t * QT, QT), QT)]

    def tile_offs(h, t, live=None):
        pos = (t * QT).astype(F32)
        out = [_slope2(NSA_GROUP * h + g) * pos for g in range(NSA_GROUP)]
        return out if live is None else [jnp.where(live, o, NEG) for o in out]

    def penalties(s, h, t):
        mrow = selm_ref[0, h, s, pl.ds(2 * t, 2), :]
        return (mrow - 1.0) * (-NEG)

    q_tops = [_q_cols(q_ref, h, cols[s]) for s, h in units]
    q_augs = [jnp.concatenate([q_tops[i], qs_ref[h]], axis=0) for i, (s, h) in enumerate(units)]
    zero_rows = jnp.zeros((LANES - pen_rows, NSA_GROUP * QT), BF16)

    def q_aug_sel(i, h, pens):
        blk = jnp.concatenate([jnp.zeros((pen_row0, QT), F32)] + pens
                              + [jnp.zeros((pen_rows - pen_row0 - 2 * len(pens), QT), F32)], axis=0)
        ext = qs_ref[h, 0:pen_rows, :].astype(F32) + jnp.concatenate([blk] * NSA_GROUP, axis=1)
        return jnp.concatenate([q_tops[i], ext.astype(BF16), zero_rows], axis=0)

    slot = [(b * nb + qbs[s]) * NSA_KV_HEADS + h for s, h in units]
    n_tiles = [count_ref[sl] for sl in slot]

    m_sel[...] = jnp.full(m_sel.shape, NEG, F32)
    acc_sel[...] = jnp.zeros(acc_sel.shape, F32)

    def sel_jobs(trip, first, step):
        jobs = []
        for i, (s, h) in enumerate(units):
            k_tiles, masks, offs, vts, pens = [], [], [], [], []
            for u in range(SEL_UNROLL):
                if first and step == 0 and u == 0:
                    t, mask, off = qbs[s], causal, tile_offs(h, qbs[s])
                else:
                    idx = (trip * SEL_CHAIN + step) * SEL_UNROLL + u - 1
                    live = idx < n_tiles[i]
                    t = jnp.where(live, list_ref[slot[i] * nb + jnp.minimum(idx, nb - 1)], 0)
                    mask, off = None, tile_offs(h, t, live)
                k_tiles.append(k_aug(ksel_ref, t, u))
                masks.append(mask)
                offs.append(off)
                vts.append(v_ext(vtsel_ref, h, t))
                pens.append(penalties(s, h, t))
            jobs.append((q_aug_sel(i, h, pens), k_tiles, masks, offs, [vts], (m_sel, acc_sel, i)))
        return jobs

    n_win = WINDOW // QT + 1
    win_jobs = []
    for i, (s, h) in enumerate(units):
        k_tiles, masks, offs, vts = [], [], [], []
        for u in range(n_win):
            raw = qbs[s] - (n_win - 1) + u
            t = jnp.maximum(raw, 0)
            k_tiles.append(k_aug(kwin_ref, t))
            masks.append(in_window if u == 0 else causal if u == n_win - 1 else None)
            offs.append(tile_offs(h, t, raw >= 0))
            vts.append(v_ext(vtwin_ref, h, t))
        win_jobs.append((q_augs[i], k_tiles, masks, offs, [vts], None))
    jobs = sel_jobs(0, True, 0) + win_jobs
    for step in range(1, SEL_CHAIN):
        jobs += sel_jobs(0, True, step)
    results = _softmax_jobs(jobs)
    for i, (s, h) in enumerate(units):
        o_w = _gated(gt_ref, h, 2, results[len(units) + i][0], cols[s])
        for g in range(NSA_GROUP):
            head = NSA_GROUP * h + g
            part_ref[s, head] = oc_ref[0, head, :, cols[s]] + o_w[g]

    def sel_body(trip, carry):
        jobs = []
        for step in range(SEL_CHAIN):
            jobs += sel_jobs(trip, False, step)
        _softmax_jobs(jobs)
        return carry

    per_trip = SEL_UNROLL * SEL_CHAIN
    n_trips = (functools.reduce(jnp.maximum, n_tiles) + per_trip) // per_trip
    lax.fori_loop(1, n_trips, sel_body, 0)

    for i, (s, h) in enumerate(units):
        o_s = _gated(gt_ref, h, 1, acc_sel[i], cols[s])
        for g in range(NSA_GROUP):
            head = NSA_GROUP * h + g
            o_ref[0, head, :, cols[s]] = part_ref[s, head] + o_s[g]


def _tile_lists(cnt, batch, nb):
    pair = cnt.reshape(batch, nb, NSA_KV_HEADS, nb, 2).sum(axis=-1)
    p_io = lax.broadcasted_iota(jnp.int32, pair.shape, 3)
    flag = (pair > 0.5) & (p_io < lax.broadcasted_iota(jnp.int32, pair.shape, 1))
    rank = jnp.cumsum(flag.astype(jnp.int32), axis=-1) - 1
    slot_io = lax.broadcasted_iota(jnp.int32, pair.shape + (nb,), 4)
    hit = flag[..., None] & (rank[..., None] == slot_io)
    lists = jnp.sum(jnp.where(hit, p_io[..., None], 0), axis=3)
    counts = jnp.sum(flag.astype(jnp.int32), axis=-1)
    return counts.reshape(-1), lists.reshape(-1)


def _nsa_sel(counts, lists, qpad3, ksel, vtsel, kwin, vtwin, selm, gates_t, oc, batch, seq):
    nb = seq // QT
    assert nb % SEL_TILES == 0
    res = lambda b, i, c, l: (b, 0, 0)
    vt_rows = NSA_KV_HEADS * VT_ROWS
    qw = QT * SEL_TILES
    n_units = SEL_TILES * NSA_KV_HEADS
    grid_spec = pltpu.PrefetchScalarGridSpec(
        num_scalar_prefetch=2,
        grid=(batch, nb // SEL_TILES),
        in_specs=[pl.BlockSpec((1, QPAD_W, qw), lambda b, i, c, l: (b, 0, i)),
                  pl.BlockSpec((1, seq, KV_WIDTH), res),
                  pl.BlockSpec((1, vt_rows, seq), res),
                  pl.BlockSpec((1, seq, KV_WIDTH), res),
                  pl.BlockSpec((1, vt_rows, seq), res),
                  pl.BlockSpec((1, NSA_KV_HEADS, SEL_TILES, 2 * nb, QT), lambda b, i, c, l: (b, 0, i, 0, 0)),
                  pl.BlockSpec((1, N_GATES, qw), lambda b, i, c, l: (b, 0, i)),
                  pl.BlockSpec((1, NSA_Q_HEADS, NSA_HEAD_DIM, qw), lambda b, i, c, l: (b, 0, 0, i))],
        out_specs=pl.BlockSpec((1, NSA_Q_HEADS, NSA_HEAD_DIM, qw), lambda b, i, c, l: (b, 0, 0, i)),
        scratch_shapes=[pltpu.VMEM((n_units, 1, NSA_GROUP * QT), F32),
                        pltpu.VMEM((n_units, VT_ROWS, NSA_GROUP * QT), F32),
                        pltpu.VMEM((NSA_KV_HEADS, LANES, NSA_GROUP * QT), BF16),
                        pltpu.VMEM((QT, LANES), BF16),
                        pltpu.VMEM((SEL_TILES, NSA_Q_HEADS, NSA_HEAD_DIM, QT), F32),
                        pltpu.VMEM((SEL_UNROLL, QT, LANES), BF16)],
    )
    return pl.pallas_call(
        _nsa_sel_kernel,
        grid_spec=grid_spec,
        out_shape=jax.ShapeDtypeStruct((batch, NSA_Q_HEADS, NSA_HEAD_DIM, seq), F32),
        compiler_params=_cparams(2),
        name="nsa_sel",
    )(counts, lists, qpad3, ksel, vtsel, kwin, vtwin, selm, gates_t, oc)


def _mlstm_kernel(q_ref, k_ref, vt_ref, ogt_ref, misc_ref, misct_ref, cw_ref, cb_ref, nw_ref,
                  o_ref, xq_ref, xk_ref, c_ref, n_ref, m_ref):
    tm = q_ref.shape[0]
    L = MLSTM_L
    hd = MLSTM_HEAD_DIM
    halo = 8

    @pl.when(pl.program_id(1) == 0)
    def _():
        xq_ref[0:halo, :] = jnp.zeros((halo, MLSTM_WIDTH), F32)
        xk_ref[0:halo, :] = jnp.zeros((halo, MLSTM_WIDTH), F32)
        c_ref[...] = jnp.zeros_like(c_ref)
        n_ref[...] = jnp.zeros_like(n_ref)
        m_ref[...] = jnp.zeros_like(m_ref)

    def conv(x_ref, buf_ref, col0):
        buf_ref[halo:halo + tm, :] = x_ref[...]
        y = cb_ref[:, col0:col0 + MLSTM_WIDTH]
        for kk in range(CONV_WIDTH):
            r0 = halo - (CONV_WIDTH - 1) + kk
            y = y + cw_ref[kk:kk + 1, col0:col0 + MLSTM_WIDTH] * buf_ref[r0:r0 + tm, :]
        tail = buf_ref[tm:tm + halo, :]
        buf_ref[0:halo, :] = tail
        return _silu(y)

    qc = conv(q_ref, xq_ref, 0)
    kc = conv(k_ref, xk_ref, MLSTM_WIDTH) * (hd ** -0.5)

    misc = misc_ref[...]
    misct = misct_ref[0]
    lf_c = -jnp.log(1.0 + jnp.exp(-misc))
    lf_r = -jnp.log(1.0 + jnp.exp(-misct))
    row_io = lax.broadcasted_iota(jnp.int32, (L, L), 0)
    col_io = lax.broadcasted_iota(jnp.int32, (L, L), 1)
    tril = jnp.where(col_io <= row_io, 1.0, 0.0).astype(BF16)
    triu = jnp.where(row_io <= col_io, 1.0, 0.0).astype(BF16)

    def split3(x):
        hi = x.astype(BF16)
        r1 = x - hi.astype(F32)
        mid = r1.astype(BF16)
        return hi, mid, (r1 - mid.astype(F32)).astype(BF16)

    lf_c3 = split3(lf_c)
    lf_r3 = split3(lf_r)
    b_c = jnp.concatenate([sum(_dot(tril, p[ci * L:(ci + 1) * L, :]) for p in lf_c3)
                           for ci in range(tm // L)], axis=0)
    b_r = jnp.concatenate([sum(_dot(p[:, ci * L:(ci + 1) * L], triu) for p in lf_r3)
                           for ci in range(tm // L)], axis=1)

    tri_t = lax.broadcasted_iota(jnp.int32, (L, L), 0) <= lax.broadcasted_iota(jnp.int32, (L, L), 1)
    nw_cols = [jnp.broadcast_to(nw_ref[:, h * hd:(h + 1) * hd], (hd, hd)).T for h in range(MLSTM_HEADS)]

    def split3_rows(row):
        hi = row.astype(BF16)
        r1 = row - hi.astype(F32)
        mid = r1.astype(BF16)
        lo = (r1 - mid.astype(F32)).astype(BF16)
        sub = lax.broadcasted_iota(jnp.int32, (8, row.shape[1]), 0)
        out = jnp.where(sub == 0, hi.astype(F32), jnp.where(sub == 1, mid.astype(F32),
                                                            jnp.where(sub == 2, lo.astype(F32), 0.0)))
        return out.astype(BF16)

    def rows_sum3(x8):
        return x8[0:1, :] + x8[1:2, :] + x8[2:3, :]

    for ci in range(tm // L):
        r0 = ci * L
        for h in range(MLSTM_HEADS):
            cs = slice(h * hd, (h + 1) * hd)
            qb16 = qc[r0:r0 + L, cs].astype(BF16)
            kb16 = kc[r0:r0 + L, cs].astype(BF16)
            vt = vt_ref[0, cs, r0:r0 + L]
            a_col = (misc[r0:r0 + L, MISC_I + h:MISC_I + h + 1]
                     - b_c[r0:r0 + L, MISC_F + h:MISC_F + h + 1])
            brow = b_r[MISC_F + h:MISC_F + h + 1, r0:r0 + L]
            lirow = misct[MISC_I + h:MISC_I + h + 1, r0:r0 + L]
            m_prev = m_ref[h]
            ct = c_ref[h]
            nrow = n_ref[h]

            dmat = jnp.where(tri_t, brow + a_col, NEG)
            m_inter = brow + m_prev
            m_t = jnp.maximum(jnp.max(dmat, axis=0, keepdims=True), m_inter)
            w = jnp.exp(dmat - m_t) * _dot_nt(kb16, qb16)
            decay = jnp.exp(m_inter - m_t)
            num = _dot(vt, w.astype(BF16)) + decay * _dot_nt(ct.astype(BF16), qb16)
            nq = rows_sum3(_dot_nt(split3_rows(nrow), qb16))
            den = jnp.sum(w, axis=0, keepdims=True) + decay * nq
            hout = num * (1.0 / jnp.maximum(jnp.abs(den), jnp.exp(-m_t)))

            b_last = brow[:, L - 1:L]
            grow = b_last - brow + lirow
            m_new = jnp.maximum(b_last + m_prev, jnp.max(grow, axis=1, keepdims=True))
            wk = jnp.exp(grow - m_new)
            d_c = jnp.exp(b_last + m_prev - m_new)
            c_ref[h] = d_c * ct + _dot((vt.astype(F32) * wk).astype(BF16), kb16)
            n_ref[h] = d_c * nrow + rows_sum3(_dot(split3_rows(wk), kb16))
            m_ref[h] = m_new

            hn = hout * lax.rsqrt(jnp.mean(hout * hout, axis=0, keepdims=True) + EPS) * nw_cols[h]
            o_ref[0, cs, r0:r0 + L] = (hn * _sigmoid(ogt_ref[0, cs, r0:r0 + L])).astype(o_ref.dtype)


def _mlstm(ml, vt_m, og_t, misc, misc_t, conv_w, conv_b, norm_w, batch, seq):
    tm = TOKEN_TILE
    tpb = seq // tm
    col = lambda j: (lambda b, i: (b * tpb + i, j))
    const = lambda b, i: (0, 0)
    trow = lambda b, i: (b, 0, i)
    return pl.pallas_call(
        _mlstm_kernel,
        grid=(batch, tpb),
        in_specs=[pl.BlockSpec((tm, MLSTM_WIDTH), col(0)),
                  pl.BlockSpec((tm, MLSTM_WIDTH), col(1)),
                  pl.BlockSpec((1, MLSTM_WIDTH, tm), trow),
                  pl.BlockSpec((1, MLSTM_WIDTH, tm), trow),
                  pl.BlockSpec((tm, LANES), col(0)),
                  pl.BlockSpec((1, LANES, tm), trow),
                  pl.BlockSpec(conv_w.shape, const),
                  pl.BlockSpec(conv_b.shape, const),
                  pl.BlockSpec(norm_w.shape, const)],
        out_specs=pl.BlockSpec((1, MLSTM_WIDTH, tm), trow),
        out_shape=jax.ShapeDtypeStruct((batch, MLSTM_WIDTH, seq), BF16),
        scratch_shapes=[pltpu.VMEM((tm + 8, MLSTM_WIDTH), F32),
                        pltpu.VMEM((tm + 8, MLSTM_WIDTH), F32),
                        pltpu.VMEM((MLSTM_HEADS, MLSTM_HEAD_DIM, MLSTM_HEAD_DIM), F32),
                        pltpu.VMEM((MLSTM_HEADS, 1, MLSTM_HEAD_DIM), F32),
                        pltpu.VMEM((MLSTM_HEADS, 1, 1), F32)],
        compiler_params=_cparams(2),
        name="mlstm",
    )(ml, ml, vt_m, og_t, misc, misc_t, conv_w, conv_b, norm_w)


def _layer(x2d, c, batch, seq, w_ada, b_ada, norm_ffn1_w, ffn1_w_in, ffn1_w_out, norm_mix_w, w_in, b_in,
           cmp_k_pe, cmp_k_w1, cmp_k_w2, cmp_v_pe, cmp_v_w1, cmp_v_w2, conv_w, conv_b, mlstm_norm_w, w_out,
           norm_ffn2_w, ffn2_w_in, ffn2_w_out, final_nw):
    d = x2d.shape[1]
    tpb = seq // TOKEN_TILE
    mod = _adaln(c.T, w_ada, b_ada.reshape(1, -1)).reshape(batch, N_MOD, d)

    x1 = _ffn(x2d, mod, norm_ffn1_w.reshape(1, d), ffn1_w_in, ffn1_w_out,
              (0, 1, 2), tpb)

    weights, b_pack = _split_inproj(w_in, b_in)
    (qpad, kc, vc, ksel, kwin, vtsel, vtwin, ml, vt_m, og_t, misc, misc_t) = _inproj(
        x1, mod, norm_mix_w.reshape(1, d), weights, b_pack, batch, seq)

    pk = _pack_compress(cmp_k_pe, cmp_k_w1, cmp_k_w2)
    pv = _pack_compress(cmp_v_pe, cmp_v_w1, cmp_v_w2)
    stacked = [jnp.stack([a, bb]) for a, bb in zip(pk, pv)]
    kcmp, vtcmp = _compress(kc, vc, *stacked)

    nb = seq // QT
    qpad3 = qpad
    gates_t = misc_t
    oc, sel, cnt = _nsa_cmp(qpad3, kcmp, vtcmp, gates_t, batch, seq)
    counts, lists = _tile_lists(cnt, batch, nb)
    o_t = _nsa_sel(counts, lists, qpad3, ksel.reshape(batch, seq, KV_WIDTH), vtsel, kwin.reshape(batch, seq, KV_WIDTH),
                   vtwin, sel, gates_t, oc, batch, seq)
    o_nsa = o_t.reshape(batch, NSA_WIDTH, seq)

    o_ml = _mlstm(ml, vt_m, og_t, misc, misc_t, conv_w, conv_b.reshape(1, -1), mlstm_norm_w.reshape(1, -1), batch, seq)

    return _ffn(x1, mod, norm_ffn2_w.reshape(1, d), ffn2_w_in, ffn2_w_out,
                (6, 7, 8), tpb, mix=(o_nsa, o_ml, w_out.astype(BF16)), mix_gate_row=5, final_nw=final_nw)


def kernel(x, c, w_ada, b_ada, norm_ffn1_w, ffn1_w_in, ffn1_w_out, norm_mix_w, w_in, b_in, cmp_k_pe, cmp_k_w1, cmp_k_w2, cmp_v_pe, cmp_v_w1, cmp_v_w2, conv_w, conv_b, mlstm_norm_w, w_out, norm_ffn2_w, ffn2_w_in, ffn2_w_out, final_norm_w):
    batch, seq, d = x.shape
    depth = w_ada.shape[0]
    assert depth == 1 and seq % (QT * CMP_STRIDE) == 0 and seq // SEL_BLOCK >= SEL_TOPK
    y = _layer(x.reshape(batch * seq, d), c, batch, seq, w_ada[0], b_ada[0], norm_ffn1_w[0], ffn1_w_in[0],
               ffn1_w_out[0], norm_mix_w[0], w_in[0], b_in[0], cmp_k_pe[0], cmp_k_w1[0], cmp_k_w2[0],
               cmp_v_pe[0], cmp_v_w1[0], cmp_v_w2[0], conv_w[0], conv_b[0], mlstm_norm_w[0], w_out[0],
               norm_ffn2_w[0], ffn2_w_in[0], ffn2_w_out[0], final_norm_w.reshape(1, d))
    return y.reshape(batch, seq, d)
```

```python
import functools

import numpy as np
import jax
import jax.numpy as jnp
from jax import lax
from jax.experimental import pallas as pl
from jax.experimental.pallas import tpu as pltpu

NSA_Q_HEADS = 8
NSA_KV_HEADS = 2
NSA_GROUP = NSA_Q_HEADS // NSA_KV_HEADS
NSA_HEAD_DIM = 64
CMP_BLOCK = 32
CMP_STRIDE = 16
CMP_HIDDEN = 128
SEL_BLOCK = 64
SEL_TOPK = 16
WINDOW = 512
FORCE_BONUS = 1.0e4
N_FORCED = 3
assert FORCE_BONUS > NSA_GROUP
MLSTM_HEADS = 4
MLSTM_HEAD_DIM = 128
CONV_WIDTH = 4
N_MOD = 9
EPS = 1e-6
NEG = -1e30

NSA_WIDTH = NSA_Q_HEADS * NSA_HEAD_DIM
KV_WIDTH = NSA_KV_HEADS * NSA_HEAD_DIM
MLSTM_WIDTH = MLSTM_HEADS * MLSTM_HEAD_DIM
N_GATES = 3 * NSA_Q_HEADS

LANES = 128
QT = 128
TOKEN_TILE = 512
FF_CHUNK = 256
MLSTM_L = 128
SEL_UNROLL = 6
SEL_CHAIN = 2
CMP_TILES = 2
SEL_TILES = 2
TOPK_ROWS = 32
VT_ROWS = 80
VMEM_LIMIT = 56 * 1024 * 1024

F32 = jnp.float32
BF16 = jnp.bfloat16
HIGHEST = lax.Precision.HIGHEST
LOG2E = 1.4426950408889634


def _cparams(n_axes):
    return pltpu.CompilerParams(dimension_semantics=("arbitrary",) * n_axes,
                                vmem_limit_bytes=VMEM_LIMIT)


def _dot(a, b):
    return jnp.dot(a, b, preferred_element_type=F32)


def _dot_nt(a, b):
    return lax.dot_general(a, b, (((1,), (1,)), ((), ())), preferred_element_type=F32)


def _dot_tn(a, b):
    return lax.dot_general(a, b, (((0,), (0,)), ((), ())), preferred_element_type=F32)


def _sigmoid(x):
    return 1.0 / (1.0 + jnp.exp(-x))


def _silu(x):
    return x * _sigmoid(x)


def _norm_mod(x, nw, sh, sc):
    ms = jnp.mean(x * x, axis=-1, keepdims=True)
    y = x * lax.rsqrt(ms + EPS) * nw
    return y * (1.0 + sc) + sh


def _adaln_kernel(ct_ref, w_ref, b_ref, o_ref):
    w = w_ref[...]
    for r in range(ct_ref.shape[1]):
        a_col = _silu(ct_ref[:, r:r + 1])
        o_ref[r:r + 1, :] = jnp.sum(a_col * w, axis=0, keepdims=True) + b_ref[...]


def _adaln(c_t, w_ada, b_ada):
    d, rows = c_t.shape
    n = w_ada.shape[1]
    tn = n // N_MOD
    return pl.pallas_call(
        _adaln_kernel,
        grid=(N_MOD,),
        in_specs=[pl.BlockSpec((d, rows), lambda j: (0, 0)),
                  pl.BlockSpec((d, tn), lambda j: (0, j)),
                  pl.BlockSpec((1, tn), lambda j: (0, j))],
        out_specs=pl.BlockSpec((rows, tn), lambda j: (0, j)),
        out_shape=jax.ShapeDtypeStruct((rows, n), F32),
        compiler_params=_cparams(1),
        name="adaln",
    )(c_t, w_ada, b_ada)


def _ffn_kernel(*refs, mod_rows, with_mix, mix_gate_row, with_final):
    it = iter(refs)
    x_ref = next(it)
    mod_ref = next(it)
    nw_ref = next(it)
    win_ref = next(it)
    wo_ref = next(it)
    if with_mix:
        ma_ref = next(it)
        mb_ref = next(it)
        wmix_ref = next(it)
    if with_final:
        fnw_ref = next(it)
    o_ref = next(it)

    sh_row, sc_row, g_row = mod_rows
    x = x_ref[...]
    if with_mix:
        mix_t = jnp.concatenate([ma_ref[0].astype(BF16), mb_ref[0].astype(BF16)], axis=0)
        x = x + mod_ref[0, mix_gate_row:mix_gate_row + 1, :] * _dot_tn(mix_t, wmix_ref[...])
    h = _norm_mod(x, nw_ref[...], mod_ref[0, sh_row:sh_row + 1, :], mod_ref[0, sc_row:sc_row + 1, :])
    hb = h.astype(BF16)
    d_ff = wo_ref.shape[0]
    n_chunks = d_ff // FF_CHUNK
    acc = None
    for j in range(n_chunks):
        c0 = j * FF_CHUNK
        g = _dot(hb, win_ref[:, c0:c0 + FF_CHUNK].astype(BF16))
        u = _dot(hb, win_ref[:, d_ff + c0:d_ff + c0 + FF_CHUNK].astype(BF16))
        act = (_silu(g) * u).astype(BF16)
        part = _dot(act, wo_ref[c0:c0 + FF_CHUNK, :].astype(BF16))
        acc = part if acc is None else acc + part
    y = x + 0.5 * mod_ref[0, g_row:g_row + 1, :] * acc
    if with_final:
        ms = jnp.mean(y * y, axis=-1, keepdims=True)
        y = y * lax.rsqrt(ms + EPS) * fnw_ref[...]
    o_ref[...] = y


def _ffn(x2d, mod, nw, w_in, w_out, mod_rows, tiles_per_batch, mix=None, mix_gate_row=None, final_nw=None):
    t, d = x2d.shape
    tm = TOKEN_TILE
    d_ff = w_out.shape[0]
    assert d_ff % FF_CHUNK == 0
    row = lambda i: (i, 0)
    const = lambda i: (0, 0)
    in_specs = [pl.BlockSpec((tm, d), row),
                pl.BlockSpec((1, N_MOD, d), lambda i: (i // tiles_per_batch, 0, 0)),
                pl.BlockSpec((1, d), const),
                pl.BlockSpec(w_in.shape, const, pipeline_mode=pl.Buffered(1)),
                pl.BlockSpec(w_out.shape, const, pipeline_mode=pl.Buffered(1))]
    args = [x2d, mod, nw, w_in, w_out]
    if mix is not None:
        ma, mb, wmix = mix
        trow = lambda i: (i // tiles_per_batch, 0, i % tiles_per_batch)
        in_specs += [pl.BlockSpec((1, ma.shape[1], tm), trow),
                     pl.BlockSpec((1, mb.shape[1], tm), trow),
                     pl.BlockSpec(wmix.shape, const)]
        args += [ma, mb, wmix]
    if final_nw is not None:
        in_specs.append(pl.BlockSpec((1, d), const))
        args.append(final_nw)
    kern = functools.partial(_ffn_kernel, mod_rows=mod_rows, with_mix=mix is not None,
                             mix_gate_row=mix_gate_row, with_final=final_nw is not None)
    return pl.pallas_call(
        kern,
        grid=(t // tm,),
        in_specs=in_specs,
        out_specs=pl.BlockSpec((tm, d), row),
        out_shape=jax.ShapeDtypeStruct((t, d), F32),
        compiler_params=_cparams(1),
        name="ffn_mix" if mix is not None else "ffn",
    )(*args)


QPAD_W = NSA_Q_HEADS * LANES
MISC_I = N_GATES
MISC_F = N_GATES + MLSTM_HEADS
HALF_BLOCK_W = CMP_STRIDE * KV_WIDTH
B_KV = NSA_WIDTH
B_ML = B_KV + 6 * KV_WIDTH
B_MISC = B_ML + 4 * MLSTM_WIDTH
W_ML = B_ML + N_GATES


def _vt_with_ones(v):
    vt = v.T
    n = vt.shape[1]
    pad = jnp.where(lax.broadcasted_iota(jnp.int32, (VT_ROWS - NSA_HEAD_DIM, n), 0) == 0, 1.0, 0.0)
    parts = []
    for h in range(NSA_KV_HEADS):
        parts += [vt[h * NSA_HEAD_DIM:(h + 1) * NSA_HEAD_DIM, :], pad]
    return jnp.concatenate(parts, axis=0).astype(BF16)


def _inproj_kernel(x_ref, mod_ref, nw_ref, w_ref, wmisc_ref, b_ref,
                   qpad_ref, kc_ref, vc_ref, ksel_ref, kwin_ref, vtsel_ref, vtwin_ref,
                   ml_ref, vtm_ref, ogt_ref, misc_ref, misct_ref, rows_ref):
    tm = x_ref.shape[0]
    h = _norm_mod(x_ref[...], nw_ref[...], mod_ref[0, 3:4, :], mod_ref[0, 4:5, :])
    hb = h.astype(BF16)

    def proj(w_ref, c0, width, b0):
        return _dot(hb, w_ref[:, c0:c0 + width].astype(BF16)) + b_ref[:, b0 + c0:b0 + c0 + width]

    def half_blocks(val, out_ref):
        rows_ref[...] = val
        pieces = [rows_ref[pl.ds(l, tm // CMP_STRIDE, stride=CMP_STRIDE), :] for l in range(CMP_STRIDE)]
        out_ref[0] = jnp.concatenate(pieces, axis=1).astype(BF16)

    scale = NSA_HEAD_DIM ** -0.5 * LOG2E
    qt = (proj(w_ref, 0, NSA_WIDTH, 0) * scale).T.astype(BF16)
    zero_rows = jnp.zeros((NSA_HEAD_DIM, tm), BF16)
    for i in range(NSA_Q_HEADS):
        slots = [zero_rows] * NSA_KV_HEADS
        slots[i // NSA_GROUP] = qt[i * NSA_HEAD_DIM:(i + 1) * NSA_HEAD_DIM, :]
        qpad_ref[0, i * LANES:(i + 1) * LANES, :] = jnp.concatenate(slots, axis=0)
    kv_c = proj(w_ref, B_KV, 2 * KV_WIDTH, 0)
    half_blocks(kv_c[:, :KV_WIDTH], kc_ref)
    half_blocks(kv_c[:, KV_WIDTH:], vc_ref)
    kv_s = proj(w_ref, B_KV + 2 * KV_WIDTH, 2 * KV_WIDTH, 0)
    ksel_ref[...] = kv_s[:, :KV_WIDTH].astype(BF16)
    vtsel_ref[0] = _vt_with_ones(kv_s[:, KV_WIDTH:])
    kv_w = proj(w_ref, B_KV + 4 * KV_WIDTH, 2 * KV_WIDTH, 0)
    kwin_ref[...] = kv_w[:, :KV_WIDTH].astype(BF16)
    vtwin_ref[0] = _vt_with_ones(kv_w[:, KV_WIDTH:])
    for i in range(2):
        ml_ref[:, i * MLSTM_WIDTH:(i + 1) * MLSTM_WIDTH] = proj(w_ref, W_ML + i * MLSTM_WIDTH, MLSTM_WIDTH,
                                                                 B_ML - W_ML)
    vtm_ref[0] = proj(w_ref, W_ML + 2 * MLSTM_WIDTH, MLSTM_WIDTH, B_ML - W_ML).T.astype(BF16)
    ogt_ref[0] = proj(w_ref, W_ML + 3 * MLSTM_WIDTH, MLSTM_WIDTH, B_ML - W_ML).T
    misc = proj(wmisc_ref, 0, LANES, B_MISC)
    misc_ref[...] = misc
    misct_ref[0] = misc.T


def _inproj(x2d, mod, nw, weights, b_pack, batch, seq):
    t, d = x2d.shape
    tm = TOKEN_TILE
    tpb = seq // tm
    row = lambda i: (i, 0)
    const = lambda i: (0, 0)
    trow = lambda i: (i // tpb, 0, i % tpb)
    hrow = lambda i: (i // tpb, i % tpb, 0)
    vt_rows = NSA_KV_HEADS * VT_ROWS
    n_half = seq // CMP_STRIDE
    out_shapes = [
        jax.ShapeDtypeStruct((batch, QPAD_W, seq), BF16),
        jax.ShapeDtypeStruct((batch, n_half, HALF_BLOCK_W), BF16),
        jax.ShapeDtypeStruct((batch, n_half, HALF_BLOCK_W), BF16),
        jax.ShapeDtypeStruct((t, KV_WIDTH), BF16),
        jax.ShapeDtypeStruct((t, KV_WIDTH), BF16),
        jax.ShapeDtypeStruct((batch, vt_rows, seq), BF16),
        jax.ShapeDtypeStruct((batch, vt_rows, seq), BF16),
        jax.ShapeDtypeStruct((t, 2 * MLSTM_WIDTH), F32),
        jax.ShapeDtypeStruct((batch, MLSTM_WIDTH, seq), BF16),
        jax.ShapeDtypeStruct((batch, MLSTM_WIDTH, seq), F32),
        jax.ShapeDtypeStruct((t, LANES), F32),
        jax.ShapeDtypeStruct((batch, LANES, seq), F32),
    ]
    out_specs = [
        pl.BlockSpec((1, QPAD_W, tm), trow),
        pl.BlockSpec((1, tm // CMP_STRIDE, HALF_BLOCK_W), hrow),
        pl.BlockSpec((1, tm // CMP_STRIDE, HALF_BLOCK_W), hrow),
        pl.BlockSpec((tm, KV_WIDTH), row),
        pl.BlockSpec((tm, KV_WIDTH), row),
        pl.BlockSpec((1, vt_rows, tm), trow),
        pl.BlockSpec((1, vt_rows, tm), trow),
        pl.BlockSpec((tm, 2 * MLSTM_WIDTH), row),
        pl.BlockSpec((1, MLSTM_WIDTH, tm), trow),
        pl.BlockSpec((1, MLSTM_WIDTH, tm), trow),
        pl.BlockSpec((tm, LANES), row),
        pl.BlockSpec((1, LANES, tm), trow),
    ]
    return pl.pallas_call(
        _inproj_kernel,
        grid=(t // tm,),
        in_specs=[pl.BlockSpec((tm, d), row),
                  pl.BlockSpec((1, N_MOD, d), lambda i: (i // tpb, 0, 0)),
                  pl.BlockSpec((1, d), const)]
                 + [pl.BlockSpec(w.shape, const, pipeline_mode=pl.Buffered(1)) for w in weights]
                 + [pl.BlockSpec(b_pack.shape, const)],
        out_specs=out_specs,
        out_shape=out_shapes,
        scratch_shapes=[pltpu.VMEM((tm, KV_WIDTH), F32)],
        compiler_params=_cparams(1),
        name="inproj",
    )(x2d, mod, nw, *weights, b_pack)


def _split_inproj(w_in, b_in):
    d = w_in.shape[0]
    offs = np.cumsum([0, NSA_WIDTH] + [KV_WIDTH] * 6 + [N_GATES] + [MLSTM_WIDTH] * 4 + [MLSTM_HEADS] * 2)
    g0, m0, i0 = int(offs[7]), int(offs[8]), int(offs[12])
    pad = LANES - N_GATES - 2 * MLSTM_HEADS
    misc_w = jnp.concatenate([w_in[:, g0:g0 + N_GATES], w_in[:, i0:i0 + 2 * MLSTM_HEADS],
                              jnp.zeros((d, pad), w_in.dtype)], axis=1)
    misc_b = jnp.concatenate([b_in[g0:g0 + N_GATES], b_in[i0:i0 + 2 * MLSTM_HEADS], jnp.zeros((pad,), b_in.dtype)])
    assert m0 == W_ML
    weights = [w_in, misc_w]
    b_pack = jnp.concatenate([b_in[:g0], b_in[m0:i0], misc_b]).reshape(1, -1).astype(F32)
    assert b_pack.shape[1] == B_MISC + LANES
    return weights, b_pack


def _compress_kernel(xk_ref, xv_ref, wa_ref, wb_ref, pe_ref, w1_ref, w2_ref, kc_ref, vtc_ref):
    n_half = xk_ref.shape[1]

    def one(x_ref, s):
        x = x_ref[0]
        a = _dot(x, wa_ref[s])
        b = _dot(x, wb_ref[s])
        pe_term = jnp.dot(pe_ref[s], w1_ref[s], preferred_element_type=F32, precision=HIGHEST)[0:1, :]
        pe2 = jnp.concatenate([pe_term] * NSA_KV_HEADS, axis=1)
        pre = a + pltpu.roll(b, n_half - 1, 0) + pe2
        hid = 0.5 * pre * (1.0 + jnp.tanh(0.7978845608028654 * (pre + 0.044715 * pre * pre * pre)))
        return _dot(hid.astype(BF16), w2_ref[s])

    kc_ref[0] = one(xk_ref, 0).astype(BF16)
    vtc_ref[0] = _vt_with_ones(one(xv_ref, 1))


def _compress(xk, xv, wa, wb, pe8, w1, w2e):
    batch, n_half, width = xk.shape
    bsel = lambda b: (b, 0, 0)
    c3 = lambda b: (0, 0, 0)
    vt_rows = NSA_KV_HEADS * VT_ROWS
    return pl.pallas_call(
        _compress_kernel,
        grid=(batch,),
        in_specs=[pl.BlockSpec((1, n_half, width), bsel), pl.BlockSpec((1, n_half, width), bsel),
                  pl.BlockSpec(wa.shape, c3), pl.BlockSpec(wb.shape, c3),
                  pl.BlockSpec(pe8.shape, c3), pl.BlockSpec(w1.shape, c3), pl.BlockSpec(w2e.shape, c3)],
        out_specs=[pl.BlockSpec((1, n_half, KV_WIDTH), bsel), pl.BlockSpec((1, vt_rows, n_half), bsel)],
        out_shape=[jax.ShapeDtypeStruct((batch, n_half, KV_WIDTH), BF16),
                   jax.ShapeDtypeStruct((batch, vt_rows, n_half), BF16)],
        compiler_params=_cparams(1),
        name="compress",
    )(xk, xv, wa, wb, pe8, w1, w2e)


def _pack_compress(pe, w1, w2):
    half = CMP_BLOCK // 2
    w1r = w1.reshape(CMP_BLOCK, NSA_HEAD_DIM, CMP_HIDDEN)
    eye = jnp.eye(NSA_KV_HEADS, dtype=w1.dtype)

    def expand(w_half):
        z = w_half[:, None, :, None, :] * eye[None, :, None, :, None]
        return z.reshape(half * KV_WIDTH, NSA_KV_HEADS * CMP_HIDDEN)

    wa = expand(w1r[:half]).astype(BF16)
    wb = expand(w1r[half:]).astype(BF16)
    w2e = (w2[None, :, None, :] * eye[:, None, :, None]).reshape(NSA_KV_HEADS * CMP_HIDDEN, KV_WIDTH).astype(BF16)
    pe8 = jnp.concatenate([pe.reshape(1, CMP_BLOCK * NSA_HEAD_DIM),
                           jnp.zeros((7, CMP_BLOCK * NSA_HEAD_DIM), pe.dtype)], axis=0)
    return wa, wb, pe8, w1, w2e


def _q_cols(q_ref, h, cols):
    return jnp.concatenate(
        [q_ref[0, (NSA_GROUP * h + g) * LANES:(NSA_GROUP * h + g + 1) * LANES, cols] for g in range(NSA_GROUP)],
        axis=1)


def _q_aug(q_ref, qs_ref, h, cols=slice(0, QT)):
    return jnp.concatenate([_q_cols(q_ref, h, cols), qs_ref[h]], axis=0)


def _slope2(head):
    return float(2.0 ** (-8.0 * (head + 1) / NSA_Q_HEADS)) * LOG2E


def _bf16_split3(x):
    parts = []
    for _ in range(3):
        p = float(np.asarray(x, np.float32).astype(BF16).astype(np.float32))
        parts.append(p)
        x = x - p
    return parts


def _init_alibi_operands(qs_ref, kpos_ref, key_stride):
    lane = lax.broadcasted_iota(jnp.int32, (QT, LANES), 1)
    row = lax.broadcasted_iota(jnp.int32, (QT, LANES), 0)
    kpos_ref[...] = jnp.where(lane < 3, row.astype(F32), 0.0).astype(BF16)
    srow = lax.broadcasted_iota(jnp.int32, (LANES, QT), 0)
    for h in range(NSA_KV_HEADS):
        for g in range(NSA_GROUP):
            hi, mid, lo = _bf16_split3(key_stride * _slope2(NSA_GROUP * h + g))
            blk = jnp.where(srow == 0, hi, jnp.where(srow == 1, mid, jnp.where(srow == 2, lo, 0.0)))
            qs_ref[h, :, g * QT:(g + 1) * QT] = blk.astype(BF16)


def _gate_row(gt_ref, head, branch, cols):
    r = head * 3 + branch
    return _sigmoid(gt_ref[0, r:r + 1, cols])


def _softmax_jobs(jobs):
    scores = [_dot(jnp.concatenate(job[1], axis=0), job[0]) for job in jobs]
    staged = []
    for (q_aug, k_tiles, masks, offs, value_groups, state), st_all in zip(jobs, scores):
        n_t = len(k_tiles)
        per_g = []
        for g in range(NSA_GROUP):
            sl = slice(g * QT, (g + 1) * QT)
            ss = []
            for u in range(n_t):
                s = st_all[u * QT:(u + 1) * QT, sl]
                ss.append(s if masks[u] is None else jnp.where(masks[u], s, NEG))
            mx = None
            for u in range(n_t):
                cand = jnp.max(ss[u], axis=0, keepdims=True) + offs[u][g]
                mx = cand if mx is None else jnp.maximum(mx, cand)
            alpha = None
            if state is None:
                m_new = mx
            else:
                m_ref, _, h = state
                m_old = m_ref[h, :, sl]
                m_new = jnp.maximum(m_old, mx)
                alpha = jnp.exp2(m_old - m_new)
                m_ref[h, :, sl] = m_new
            m_use = jnp.where(m_new < 0.5 * NEG, 0.0, m_new)
            per_g.append((ss, m_use, alpha))
        staged.append(per_g)
    ets = []
    for (q_aug, k_tiles, masks, offs, value_groups, state), per_g in zip(jobs, staged):
        n_t = len(k_tiles)
        rows = [jnp.concatenate([jnp.exp2(per_g[g][0][u] - (per_g[g][1] - offs[u][g])).astype(BF16)
                                 for g in range(NSA_GROUP)], axis=1) for u in range(n_t)]
        ets.append(jnp.concatenate(rows, axis=0))
    results = []
    for (q_aug, k_tiles, masks, offs, value_groups, state), per_g, et in zip(jobs, staged, ets):
        mats = [jnp.concatenate(grp, axis=1) for grp in value_groups]
        pv_all = _dot(jnp.concatenate(mats, axis=0), et)
        pvs, r0 = [], 0
        for mat in mats:
            pvs.append(pv_all[r0:r0 + mat.shape[0], :])
            r0 += mat.shape[0]
        if state is None:
            results.append(pvs)
        else:
            _, acc_ref, h = state
            acc_ref[h] = jnp.concatenate([pg[2] for pg in per_g], axis=1) * acc_ref[h] + pvs[0]
            results.append(None)
    return results


def _gated(gt_ref, h, branch, pv, cols=slice(0, QT)):
    inv = 1.0 / jnp.maximum(pv[NSA_HEAD_DIM:NSA_HEAD_DIM + 1, :], 1e-30)
    out = []
    for g in range(NSA_GROUP):
        sl = slice(g * QT, (g + 1) * QT)
        out.append(pv[0:NSA_HEAD_DIM, sl] * (inv[:, sl] * _gate_row(gt_ref, NSA_GROUP * h + g, branch, cols)))
    return out


def _nsa_cmp_kernel(q_ref, kc_ref, vtc_ref, gt_ref, oc_ref, sel_ref, cnt_ref,
                    qs_ref, kpos_ref, ovt_ref, pvo_ref, pvi_ref):
    ncp = kc_ref.shape[1]
    n_ct = ncp // QT
    n_sel = sel_ref.shape[3]
    tiles = range(CMP_TILES)
    starts = [(pl.program_id(1) * CMP_TILES + s) * QT for s in tiles]
    cols = [slice(s * QT, (s + 1) * QT) for s in tiles]
    units = [(s, h) for s in tiles for h in range(NSA_KV_HEADS)]
    tile_span = QT * CMP_STRIDE

    @pl.when((pl.program_id(0) == 0) & (pl.program_id(1) == 0))
    def _():
        _init_alibi_operands(qs_ref, kpos_ref, float(CMP_STRIDE))
        j_o = lax.broadcasted_iota(jnp.int32, (n_sel, ncp), 0)
        c_o = lax.broadcasted_iota(jnp.int32, (n_sel, ncp), 1)
        ov = ((c_o * CMP_STRIDE <= j_o * SEL_BLOCK + (SEL_BLOCK - 1))
              & (c_o * CMP_STRIDE + (CMP_BLOCK - 1) >= j_o * SEL_BLOCK))
        ovt_ref[...] = jnp.where(ov, 1.0, 0.0).astype(BF16)

    a0 = (lax.broadcasted_iota(jnp.int32, (QT, QT), 1)
          - CMP_STRIDE * lax.broadcasted_iota(jnp.int32, (QT, QT), 0))
    q_augs = [_q_aug(q_ref, qs_ref, h, cols[s]) for s, h in units]
    n_vis = jnp.minimum((starts[-1] + QT - CMP_BLOCK) // tile_span + 1, n_ct)

    for k in range(1, n_ct + 1):
        @pl.when(n_vis == k)
        def _(k=k):
            jobs = []
            for i, (s, h) in enumerate(units):
                k_tiles, masks, offs, vts, ovs = [], [], [], [], []
                for ci in range(k):
                    cs = slice(ci * QT, (ci + 1) * QT)
                    first_end = ci * tile_span + CMP_BLOCK - 1
                    k_tiles.append(jnp.concatenate([kc_ref[0, cs, :], kpos_ref[...]], axis=1))
                    masks.append(a0 + (starts[s] - first_end) >= 0)
                    offs.append([_slope2(NSA_GROUP * h + g) * first_end for g in range(NSA_GROUP)])
                    vts.append(vtc_ref[0, h * VT_ROWS:(h + 1) * VT_ROWS, cs])
                    ovs.append(ovt_ref[:, cs])
                jobs.append((q_augs[i], k_tiles, masks, offs, [vts, ovs], None))
            for i, (pvo, pvi) in enumerate(_softmax_jobs(jobs)):
                pvo_ref[i] = pvo
                pvi_ref[i] = pvi

    j_io = lax.broadcasted_iota(jnp.int32, (n_sel, QT), 0)
    q_io = lax.broadcasted_iota(jnp.int32, (n_sel, QT), 1)
    ones8 = jnp.ones((8, QT), BF16)

    taken = -(2.0 ** 126)
    scores = []
    for i, (s, h) in enumerate(units):
        t_j = starts[s] + q_io
        cur = t_j // SEL_BLOCK
        valid = j_io * SEL_BLOCK <= t_j
        forced = (j_io == 0) | (j_io == cur) | (j_io == cur - 1)
        pvo = pvo_ref[i]
        o_c = _gated(gt_ref, h, 0, pvo, cols[s])
        inv = 1.0 / jnp.maximum(pvo[NSA_HEAD_DIM:NSA_HEAD_DIM + 1, :], 1e-30)
        imp = None
        for g in range(NSA_GROUP):
            sl = slice(g * QT, (g + 1) * QT)
            oc_ref[0, NSA_GROUP * h + g, :, cols[s]] = o_c[g]
            term = pvi_ref[i, :, sl] * inv[:, sl]
            imp = term if imp is None else imp + term
        scores.append(jnp.where(forced, taken, jnp.where(valid, imp, -FORCE_BONUS)))

    n_cls = n_sel // TOPK_ROWS
    cls = jnp.minimum((starts[-1] + QT + SEL_BLOCK * TOPK_ROWS - 1) // (SEL_BLOCK * TOPK_ROWS), n_cls)

    for k in range(1, n_cls + 1):
        @pl.when(cls == k)
        def _(rows=k * TOPK_ROWS):
            j_sub = j_io[:rows]

            def pick(_, carry):
                out = []
                for sc in carry:
                    mx = jnp.max(sc, axis=0, keepdims=True)
                    idx = jnp.min(jnp.where(sc == mx, j_sub, n_sel), axis=0, keepdims=True)
                    out.append(jnp.where(j_sub == idx, taken, sc))
                return tuple(out)

            picked = lax.fori_loop(0, SEL_TOPK - N_FORCED, pick, tuple(sc[:rows] for sc in scores))
            for i, (s, h) in enumerate(units):
                sel = jnp.where(picked[i] <= 0.5 * taken, 1.0, 0.0)
                if rows < n_sel:
                    sel = jnp.concatenate([sel, jnp.zeros((n_sel - rows, QT), F32)], axis=0)
                sel_ref[0, h, s] = sel
                cnt = _dot_nt(ones8, sel.astype(BF16))
                cnt_ref[0, s, h] = cnt[0:1, :]


def _nsa_cmp(qpad3, kc, vtc, gates_t, batch, seq):
    nb = seq // QT
    n_sel = seq // SEL_BLOCK
    ncp = kc.shape[1]
    assert ncp % QT == 0 and nb % CMP_TILES == 0
    vt_rows = NSA_KV_HEADS * VT_ROWS
    qw = QT * CMP_TILES
    n_units = CMP_TILES * NSA_KV_HEADS
    return pl.pallas_call(
        _nsa_cmp_kernel,
        grid=(batch, nb // CMP_TILES),
        in_specs=[pl.BlockSpec((1, QPAD_W, qw), lambda b, i: (b, 0, i)),
                  pl.BlockSpec((1, ncp, KV_WIDTH), lambda b, i: (b, 0, 0)),
                  pl.BlockSpec((1, vt_rows, ncp), lambda b, i: (b, 0, 0)),
                  pl.BlockSpec((1, N_GATES, qw), lambda b, i: (b, 0, i))],
        out_specs=[pl.BlockSpec((1, NSA_Q_HEADS, NSA_HEAD_DIM, qw), lambda b, i: (b, 0, 0, i)),
                   pl.BlockSpec((1, NSA_KV_HEADS, CMP_TILES, n_sel, QT), lambda b, i: (b, 0, i, 0, 0)),
                   pl.BlockSpec((1, CMP_TILES, NSA_KV_HEADS, 1, n_sel), lambda b, i: (b, i, 0, 0, 0))],
        out_shape=[jax.ShapeDtypeStruct((batch, NSA_Q_HEADS, NSA_HEAD_DIM, seq), F32),
                   jax.ShapeDtypeStruct((batch, NSA_KV_HEADS, nb, n_sel, QT), F32),
                   jax.ShapeDtypeStruct((batch, nb, NSA_KV_HEADS, 1, n_sel), F32)],
        scratch_shapes=[pltpu.VMEM((NSA_KV_HEADS, LANES, NSA_GROUP * QT), BF16),
                        pltpu.VMEM((QT, LANES), BF16),
                        pltpu.VMEM((n_sel, ncp), BF16),
                        pltpu.VMEM((n_units, VT_ROWS, NSA_GROUP * QT), F32),
                        pltpu.VMEM((n_units, n_sel, NSA_GROUP * QT), F32)],
        compiler_params=_cparams(2),
        name="nsa_cmp",
    )(qpad3, kc, vtc, gates_t)


def _nsa_sel_kernel(count_ref, list_ref, q_ref, ksel_ref, vtsel_ref, kwin_ref, vtwin_ref, selm_ref, gt_ref,
                    oc_ref, o_ref, m_sel, acc_sel, qs_ref, kpos_ref, part_ref, kposm_ref):
    b = pl.program_id(0)
    nb = pl.num_programs(1) * SEL_TILES
    tiles = range(SEL_TILES)
    qbs = [pl.program_id(1) * SEL_TILES + s for s in tiles]
    cols = [slice(s * QT, (s + 1) * QT) for s in tiles]
    units = [(s, h) for s in tiles for h in range(NSA_KV_HEADS)]
    k_io = lax.broadcasted_iota(jnp.int32, (QT, QT), 0)
    q_io = lax.broadcasted_iota(jnp.int32, (QT, QT), 1)
    causal = q_io >= k_io
    in_window = q_io < k_io

    pen_row0 = 3
    pen_rows = 16
    assert pen_row0 + 2 * SEL_UNROLL <= pen_rows

    @pl.when((b == 0) & (pl.program_id(1) == 0))
    def _():
        _init_alibi_operands(qs_ref, kpos_ref, 1.0)
        lane = lax.broadcasted_iota(jnp.int32, (QT, LANES), 1)
        row = lax.broadcasted_iota(jnp.int32, (QT, LANES), 0)
        base = jnp.where(lane < pen_row0, row.astype(F32), 0.0)
        for u in range(SEL_UNROLL):
            ind = jnp.where(lane == pen_row0 + 2 * u, jnp.where(row < SEL_BLOCK, 1.0, 0.0),
                            jnp.where(lane == pen_row0 + 2 * u + 1, jnp.where(row >= SEL_BLOCK, 1.0, 0.0), 0.0))
            kposm_ref[u] = (base + ind).astype(BF16)

    def k_aug(k_ref, t, u=None):
        off = pl.multiple_of(t * QT, QT)
        tail = kpos_ref[...] if u is None else kposm_ref[u]
        return jnp.concatenate([k_ref[0, pl.ds(off, QT), :], tail], axis=1)

    def v_ext(vt_ref, h, t):
        return vt_ref[0, h * VT_ROWS:(h + 1) * VT_ROWS, pl.ds(pl.multiple_of(t * QT, QT), QT)]

    def tile_offs(h, t, live=None):
        pos = (t * QT).astype(F32)
        out = [_slope2(NSA_GROUP * h + g) * pos for g in range(NSA_GROUP)]
        return out if live is None else [jnp.where(live, o, NEG) for o in out]

    def penalties(s, h, t):
        mrow = selm_ref[0, h, s, pl.ds(2 * t, 2), :]
        return (mrow - 1.0) * (-NEG)

    q_tops = [_q_cols(q_ref, h, cols[s]) for s, h in units]
    q_augs = [jnp.concatenate([q_tops[i], qs_ref[h]], axis=0) for i, (s, h) in enumerate(units)]
    zero_rows = jnp.zeros((LANES - pen_rows, NSA_GROUP * QT), BF16)

    def q_aug_sel(i, h, pens):
        blk = jnp.concatenate([jnp.zeros((pen_row0, QT), F32)] + pens
                              + [jnp.zeros((pen_rows - pen_row0 - 2 * len(pens), QT), F32)], axis=0)
        ext = qs_ref[h, 0:pen_rows, :].astype(F32) + jnp.concatenate([blk] * NSA_GROUP, axis=1)
        return jnp.concatenate([q_tops[i], ext.astype(BF16), zero_rows], axis=0)

    slot = [(b * nb + qbs[s]) * NSA_KV_HEADS + h for s, h in units]
    n_tiles = [count_ref[sl] for sl in slot]

    m_sel[...] = jnp.full(m_sel.shape, NEG, F32)
    acc_sel[...] = jnp.zeros(acc_sel.shape, F32)

    def sel_jobs(trip, first, step):
        jobs = []
        for i, (s, h) in enumerate(units):
            k_tiles, masks, offs, vts, pens = [], [], [], [], []
            for u in range(SEL_UNROLL):
                if first and step == 0 and u == 0:
                    t, mask, off = qbs[s], causal, tile_offs(h, qbs[s])
                else:
                    idx = (trip * SEL_CHAIN + step) * SEL_UNROLL + u - 1
                    live = idx < n_tiles[i]
                    t = jnp.where(live, list_ref[slot[i] * nb + jnp.minimum(idx, nb - 1)], 0)
                    mask, off = None, tile_offs(h, t, live)
                k_tiles.append(k_aug(ksel_ref, t, u))
                masks.append(mask)
                offs.append(off)
                vts.append(v_ext(vtsel_ref, h, t))
                pens.append(penalties(s, h, t))
            jobs.append((q_aug_sel(i, h, pens), k_tiles, masks, offs, [vts], (m_sel, acc_sel, i)))
        return jobs

    n_win = WINDOW // QT + 1
    win_jobs = []
    for i, (s, h) in enumerate(units):
        k_tiles, masks, offs, vts = [], [], [], []
        for u in range(n_win):
            raw = qbs[s] - (n_win - 1) + u
            t = jnp.maximum(raw, 0)
            k_tiles.append(k_aug(kwin_ref, t))
            masks.append(in_window if u == 0 else causal if u == n_win - 1 else None)
            offs.append(tile_offs(h, t, raw >= 0))
            vts.append(v_ext(vtwin_ref, h, t))
        win_jobs.append((q_augs[i], k_tiles, masks, offs, [vts], None))
    jobs = sel_jobs(0, True, 0) + win_jobs
    for step in range(1, SEL_CHAIN):
        jobs += sel_jobs(0, True, step)
    results = _softmax_jobs(jobs)
    for i, (s, h) in enumerate(units):
        o_w = _gated(gt_ref, h, 2, results[len(units) + i][0], cols[s])
        for g in range(NSA_GROUP):
            head = NSA_GROUP * h + g
            part_ref[s, head] = oc_ref[0, head, :, cols[s]] + o_w[g]

    def sel_body(trip, carry):
        jobs = []
        for step in range(SEL_CHAIN):
            jobs += sel_jobs(trip, False, step)
        _softmax_jobs(jobs)
        return carry

    per_trip = SEL_UNROLL * SEL_CHAIN
    n_trips = (functools.reduce(jnp.maximum, n_tiles) + per_trip) // per_trip
    lax.fori_loop(1, n_trips, sel_body, 0)

    for i, (s, h) in enumerate(units):
        o_s = _gated(gt_ref, h, 1, acc_sel[i], cols[s])
        for g in range(NSA_GROUP):
            head = NSA_GROUP * h + g
            o_ref[0, head, :, cols[s]] = part_ref[s, head] + o_s[g]


def _tile_lists(cnt, batch, nb):
    pair = cnt.reshape(batch, nb, NSA_KV_HEADS, nb, 2).sum(axis=-1)
    p_io = lax.broadcasted_iota(jnp.int32, pair.shape, 3)
    flag = (pair > 0.5) & (p_io < lax.broadcasted_iota(jnp.int32, pair.shape, 1))
    rank = jnp.cumsum(flag.astype(jnp.int32), axis=-1) - 1
    slot_io = lax.broadcasted_iota(jnp.int32, pair.shape + (nb,), 4)
    hit = flag[..., None] & (rank[..., None] == slot_io)
    lists = jnp.sum(jnp.where(hit, p_io[..., None], 0), axis=3)
    counts = jnp.sum(flag.astype(jnp.int32), axis=-1)
    return counts.reshape(-1), lists.reshape(-1)


def _nsa_sel(counts, lists, qpad3, ksel, vtsel, kwin, vtwin, selm, gates_t, oc, batch, seq):
    nb = seq // QT
    assert nb % SEL_TILES == 0
    res = lambda b, i, c, l: (b, 0, 0)
    vt_rows = NSA_KV_HEADS * VT_ROWS
    qw = QT * SEL_TILES
    n_units = SEL_TILES * NSA_KV_HEADS
    grid_spec = pltpu.PrefetchScalarGridSpec(
        num_scalar_prefetch=2,
        grid=(batch, nb // SEL_TILES),
        in_specs=[pl.BlockSpec((1, QPAD_W, qw), lambda b, i, c, l: (b, 0, i)),
                  pl.BlockSpec((1, seq, KV_WIDTH), res),
                  pl.BlockSpec((1, vt_rows, seq), res),
                  pl.BlockSpec((1, seq, KV_WIDTH), res),
                  pl.BlockSpec((1, vt_rows, seq), res),
                  pl.BlockSpec((1, NSA_KV_HEADS, SEL_TILES, 2 * nb, QT), lambda b, i, c, l: (b, 0, i, 0, 0)),
                  pl.BlockSpec((1, N_GATES, qw), lambda b, i, c, l: (b, 0, i)),
                  pl.BlockSpec((1, NSA_Q_HEADS, NSA_HEAD_DIM, qw), lambda b, i, c, l: (b, 0, 0, i))],
        out_specs=pl.BlockSpec((1, NSA_Q_HEADS, NSA_HEAD_DIM, qw), lambda b, i, c, l: (b, 0, 0, i)),
        scratch_shapes=[pltpu.VMEM((n_units, 1, NSA_GROUP * QT), F32),
                        pltpu.VMEM((n_units, VT_ROWS, NSA_GROUP * QT), F32),
                        pltpu.VMEM((NSA_KV_HEADS, LANES, NSA_GROUP * QT), BF16),
                        pltpu.VMEM((QT, LANES), BF16),
                        pltpu.VMEM((SEL_TILES, NSA_Q_HEADS, NSA_HEAD_DIM, QT), F32),
                        pltpu.VMEM((SEL_UNROLL, QT, LANES), BF16)],
    )
    return pl.pallas_call(
        _nsa_sel_kernel,
        grid_spec=grid_spec,
        out_shape=jax.ShapeDtypeStruct((batch, NSA_Q_HEADS, NSA_HEAD_DIM, seq), F32),
        compiler_params=_cparams(2),
        name="nsa_sel",
    )(counts, lists, qpad3, ksel, vtsel, kwin, vtwin, selm, gates_t, oc)


def _mlstm_kernel(q_ref, k_ref, vt_ref, ogt_ref, misc_ref, misct_ref, cw_ref, cb_ref, nw_ref,
                  o_ref, xq_ref, xk_ref, c_ref, n_ref, m_ref):
    tm = q_ref.shape[0]
    L = MLSTM_L
    hd = MLSTM_HEAD_DIM
    halo = 8

    @pl.when(pl.program_id(1) == 0)
    def _():
        xq_ref[0:halo, :] = jnp.zeros((halo, MLSTM_WIDTH), F32)
        xk_ref[0:halo, :] = jnp.zeros((halo, MLSTM_WIDTH), F32)
        c_ref[...] = jnp.zeros_like(c_ref)
        n_ref[...] = jnp.zeros_like(n_ref)
        m_ref[...] = jnp.zeros_like(m_ref)

    def conv(x_ref, buf_ref, col0):
        buf_ref[halo:halo + tm, :] = x_ref[...]
        y = cb_ref[:, col0:col0 + MLSTM_WIDTH]
        for kk in range(CONV_WIDTH):
            r0 = halo - (CONV_WIDTH - 1) + kk
            y = y + cw_ref[kk:kk + 1, col0:col0 + MLSTM_WIDTH] * buf_ref[r0:r0 + tm, :]
        tail = buf_ref[tm:tm + halo, :]
        buf_ref[0:halo, :] = tail
        return _silu(y)

    qc = conv(q_ref, xq_ref, 0)
    kc = conv(k_ref, xk_ref, MLSTM_WIDTH) * (hd ** -0.5)

    misc = misc_ref[...]
    misct = misct_ref[0]
    lf_c = -jnp.log(1.0 + jnp.exp(-misc))
    lf_r = -jnp.log(1.0 + jnp.exp(-misct))
    row_io = lax.broadcasted_iota(jnp.int32, (L, L), 0)
    col_io = lax.broadcasted_iota(jnp.int32, (L, L), 1)
    tril = jnp.where(col_io <= row_io, 1.0, 0.0).astype(BF16)
    triu = jnp.where(row_io <= col_io, 1.0, 0.0).astype(BF16)

    def split3(x):
        hi = x.astype(BF16)
        r1 = x - hi.astype(F32)
        mid = r1.astype(BF16)
        return hi, mid, (r1 - mid.astype(F32)).astype(BF16)

    lf_c3 = split3(lf_c)
    lf_r3 = split3(lf_r)
    b_c = jnp.concatenate([sum(_dot(tril, p[ci * L:(ci + 1) * L, :]) for p in lf_c3)
                           for ci in range(tm // L)], axis=0)
    b_r = jnp.concatenate([sum(_dot(p[:, ci * L:(ci + 1) * L], triu) for p in lf_r3)
                           for ci in range(tm // L)], axis=1)

    tri_t = lax.broadcasted_iota(jnp.int32, (L, L), 0) <= lax.broadcasted_iota(jnp.int32, (L, L), 1)
    nw_cols = [jnp.broadcast_to(nw_ref[:, h * hd:(h + 1) * hd], (hd, hd)).T for h in range(MLSTM_HEADS)]

    def split3_rows(row):
        hi = row.astype(BF16)
        r1 = row - hi.astype(F32)
        mid = r1.astype(BF16)
        lo = (r1 - mid.astype(F32)).astype(BF16)
        sub = lax.broadcasted_iota(jnp.int32, (8, row.shape[1]), 0)
        out = jnp.where(sub == 0, hi.astype(F32), jnp.where(sub == 1, mid.astype(F32),
                                                            jnp.where(sub == 2, lo.astype(F32), 0.0)))
        return out.astype(BF16)

    def rows_sum3(x8):
        return x8[0:1, :] + x8[1:2, :] + x8[2:3, :]

    for ci in range(tm // L):
        r0 = ci * L
        for h in range(MLSTM_HEADS):
            cs = slice(h * hd, (h + 1) * hd)
            qb16 = qc[r0:r0 + L, cs].astype(BF16)
            kb16 = kc[r0:r0 + L, cs].astype(BF16)
            vt = vt_ref[0, cs, r0:r0 + L]
            a_col = (misc[r0:r0 + L, MISC_I + h:MISC_I + h + 1]
                     - b_c[r0:r0 + L, MISC_F + h:MISC_F + h + 1])
            brow = b_r[MISC_F + h:MISC_F + h + 1, r0:r0 + L]
            lirow = misct[MISC_I + h:MISC_I + h + 1, r0:r0 + L]
            m_prev = m_ref[h]
            ct = c_ref[h]
            nrow = n_ref[h]

            dmat = jnp.where(tri_t, brow + a_col, NEG)
            m_inter = brow + m_prev
            m_t = jnp.maximum(jnp.max(dmat, axis=0, keepdims=True), m_inter)
            w = jnp.exp(dmat - m_t) * _dot_nt(kb16, qb16)
            decay = jnp.exp(m_inter - m_t)
            num = _dot(vt, w.astype(BF16)) + decay * _dot_nt(ct.astype(BF16), qb16)
            nq = rows_sum3(_dot_nt(split3_rows(nrow), qb16))
            den = jnp.sum(w, axis=0, keepdims=True) + decay * nq
            hout = num * (1.0 / jnp.maximum(jnp.abs(den), jnp.exp(-m_t)))

            b_last = brow[:, L - 1:L]
            grow = b_last - brow + lirow
            m_new = jnp.maximum(b_last + m_prev, jnp.max(grow, axis=1, keepdims=True))
            wk = jnp.exp(grow - m_new)
            d_c = jnp.exp(b_last + m_prev - m_new)
            c_ref[h] = d_c * ct + _dot((vt.astype(F32) * wk).astype(BF16), kb16)
            n_ref[h] = d_c * nrow + rows_sum3(_dot(split3_rows(wk), kb16))
            m_ref[h] = m_new

            hn = hout * lax.rsqrt(jnp.mean(hout * hout, axis=0, keepdims=True) + EPS) * nw_cols[h]
            o_ref[0, cs, r0:r0 + L] = (hn * _sigmoid(ogt_ref[0, cs, r0:r0 + L])).astype(o_ref.dtype)


def _mlstm(ml, vt_m, og_t, misc, misc_t, conv_w, conv_b, norm_w, batch, seq):
    tm = TOKEN_TILE
    tpb = seq // tm
    col = lambda j: (lambda b, i: (b * tpb + i, j))
    const = lambda b, i: (0, 0)
    trow = lambda b, i: (b, 0, i)
    return pl.pallas_call(
        _mlstm_kernel,
        grid=(batch, tpb),
        in_specs=[pl.BlockSpec((tm, MLSTM_WIDTH), col(0)),
                  pl.BlockSpec((tm, MLSTM_WIDTH), col(1)),
                  pl.BlockSpec((1, MLSTM_WIDTH, tm), trow),
                  pl.BlockSpec((1, MLSTM_WIDTH, tm), trow),
                  pl.BlockSpec((tm, LANES), col(0)),
                  pl.BlockSpec((1, LANES, tm), trow),
                  pl.BlockSpec(conv_w.shape, const),
                  pl.BlockSpec(conv_b.shape, const),
                  pl.BlockSpec(norm_w.shape, const)],
        out_specs=pl.BlockSpec((1, MLSTM_WIDTH, tm), trow),
        out_shape=jax.ShapeDtypeStruct((batch, MLSTM_WIDTH, seq), BF16),
        scratch_shapes=[pltpu.VMEM((tm + 8, MLSTM_WIDTH), F32),
                        pltpu.VMEM((tm + 8, MLSTM_WIDTH), F32),
                        pltpu.VMEM((MLSTM_HEADS, MLSTM_HEAD_DIM, MLSTM_HEAD_DIM), F32),
                        pltpu.VMEM((MLSTM_HEADS, 1, MLSTM_HEAD_DIM), F32),
                        pltpu.VMEM((MLSTM_HEADS, 1, 1), F32)],
        compiler_params=_cparams(2),
        name="mlstm",
    )(ml, ml, vt_m, og_t, misc, misc_t, conv_w, conv_b, norm_w)


def _layer(x2d, c, batch, seq, w_ada, b_ada, norm_ffn1_w, ffn1_w_in, ffn1_w_out, norm_mix_w, w_in, b_in,
           cmp_k_pe, cmp_k_w1, cmp_k_w2, cmp_v_pe, cmp_v_w1, cmp_v_w2, conv_w, conv_b, mlstm_norm_w, w_out,
           norm_ffn2_w, ffn2_w_in, ffn2_w_out, final_nw):
    d = x2d.shape[1]
    tpb = seq // TOKEN_TILE
    mod = _adaln(c.T, w_ada, b_ada.reshape(1, -1)).reshape(batch, N_MOD, d)

    x1 = _ffn(x2d, mod, norm_ffn1_w.reshape(1, d), ffn1_w_in, ffn1_w_out,
              (0, 1, 2), tpb)

    weights, b_pack = _split_inproj(w_in, b_in)
    (qpad, kc, vc, ksel, kwin, vtsel, vtwin, ml, vt_m, og_t, misc, misc_t) = _inproj(
        x1, mod, norm_mix_w.reshape(1, d), weights, b_pack, batch, seq)

    pk = _pack_compress(cmp_k_pe, cmp_k_w1, cmp_k_w2)
    pv = _pack_compress(cmp_v_pe, cmp_v_w1, cmp_v_w2)
    stacked = [jnp.stack([a, bb]) for a, bb in zip(pk, pv)]
    kcmp, vtcmp = _compress(kc, vc, *stacked)

    nb = seq // QT
    qpad3 = qpad
    gates_t = misc_t
    oc, sel, cnt = _nsa_cmp(qpad3, kcmp, vtcmp, gates_t, batch, seq)
    counts, lists = _tile_lists(cnt, batch, nb)
    o_t = _nsa_sel(counts, lists, qpad3, ksel.reshape(batch, seq, KV_WIDTH), vtsel, kwin.reshape(batch, seq, KV_WIDTH),
                   vtwin, sel, gates_t, oc, batch, seq)
    o_nsa = o_t.reshape(batch, NSA_WIDTH, seq)

    o_ml = _mlstm(ml, vt_m, og_t, misc, misc_t, conv_w, conv_b.reshape(1, -1), mlstm_norm_w.reshape(1, -1), batch, seq)

    return _ffn(x1, mod, norm_ffn2_w.reshape(1, d), ffn2_w_in, ffn2_w_out,
                (6, 7, 8), tpb, mix=(o_nsa, o_ml, w_out.astype(BF16)), mix_gate_row=5, final_nw=final_nw)


def kernel(x, c, w_ada, b_ada, norm_ffn1_w, ffn1_w_in, ffn1_w_out, norm_mix_w, w_in, b_in, cmp_k_pe, cmp_k_w1, cmp_k_w2, cmp_v_pe, cmp_v_w1, cmp_v_w2, conv_w, conv_b, mlstm_norm_w, w_out, norm_ffn2_w, ffn2_w_in, ffn2_w_out, final_norm_w):
    batch, seq, d = x.shape
    depth = w_ada.shape[0]
    assert depth == 1 and seq % (QT * CMP_STRIDE) == 0 and seq // SEL_BLOCK >= SEL_TOPK
    y = _layer(x.reshape(batch * seq, d), c, batch, seq, w_ada[0], b_ada[0], norm_ffn1_w[0], ffn1_w_in[0],
               ffn1_w_out[0], norm_mix_w[0], w_in[0], b_in[0], cmp_k_pe[0], cmp_k_w1[0], cmp_k_w2[0],
               cmp_v_pe[0], cmp_v_w1[0], cmp_v_w2[0], conv_w[0], conv_b[0], mlstm_norm_w[0], w_out[0],
               norm_ffn2_w[0], ffn2_w_in[0], ffn2_w_out[0], final_norm_w.reshape(1, d))
    return y.reshape(batch, seq, d)
```

```python
import functools

import numpy as np
import jax
import jax.numpy as jnp
from jax import lax
from jax.experimental import pallas as pl
from jax.experimental.pallas import tpu as pltpu

NSA_Q_HEADS = 8
NSA_KV_HEADS = 2
NSA_GROUP = NSA_Q_HEADS // NSA_KV_HEADS
NSA_HEAD_DIM = 64
CMP_BLOCK = 32
CMP_STRIDE = 16
CMP_HIDDEN = 128
SEL_BLOCK = 64
SEL_TOPK = 16
WINDOW = 512
FORCE_BONUS = 1.0e4
N_FORCED = 3
assert FORCE_BONUS > NSA_GROUP
MLSTM_HEADS = 4
MLSTM_HEAD_DIM = 128
CONV_WIDTH = 4
N_MOD = 9
EPS = 1e-6
NEG = -1e30

NSA_WIDTH = NSA_Q_HEADS * NSA_HEAD_DIM
KV_WIDTH = NSA_KV_HEADS * NSA_HEAD_DIM
MLSTM_WIDTH = MLSTM_HEADS * MLSTM_HEAD_DIM
N_GATES = 3 * NSA_Q_HEADS

LANES = 128
QT = 128
TOKEN_TILE = 512
WIDE_TOKEN_TILE = 1024
FF_CHUNK = 256
MLSTM_L = 128
SEL_UNROLL = 6
SEL_CHAIN = 2
CMP_TILES = 2
SEL_TILES = 2
TOPK_ROWS = 32
VT_ROWS = 80
VMEM_LIMIT = 56 * 1024 * 1024

F32 = jnp.float32
BF16 = jnp.bfloat16
HIGHEST = lax.Precision.HIGHEST
LOG2E = 1.4426950408889634


def _cparams(n_axes):
    return pltpu.CompilerParams(dimension_semantics=("arbitrary",) * n_axes,
                                vmem_limit_bytes=VMEM_LIMIT)


def _dot(a, b):
    return jnp.dot(a, b, preferred_element_type=F32)


def _dot_nt(a, b):
    return lax.dot_general(a, b, (((1,), (1,)), ((), ())), preferred_element_type=F32)


def _dot_tn(a, b):
    return lax.dot_general(a, b, (((0,), (0,)), ((), ())), preferred_element_type=F32)


def _sigmoid(x):
    return 1.0 / (1.0 + jnp.exp(-x))


def _silu(x):
    return x * _sigmoid(x)


def _norm_mod(x, nw, sh, sc):
    ms = jnp.mean(x * x, axis=-1, keepdims=True)
    y = x * lax.rsqrt(ms + EPS) * nw
    return y * (1.0 + sc) + sh


def _adaln_kernel(ct_ref, w_ref, b_ref, o_ref):
    w = w_ref[...]
    for r in range(ct_ref.shape[1]):
        a_col = _silu(ct_ref[:, r:r + 1])
        o_ref[r:r + 1, :] = jnp.sum(a_col * w, axis=0, keepdims=True) + b_ref[...]


def _adaln(c_t, w_ada, b_ada):
    d, rows = c_t.shape
    n = w_ada.shape[1]
    tn = n // N_MOD
    return pl.pallas_call(
        _adaln_kernel,
        grid=(N_MOD,),
        in_specs=[pl.BlockSpec((d, rows), lambda j: (0, 0)),
                  pl.BlockSpec((d, tn), lambda j: (0, j)),
                  pl.BlockSpec((1, tn), lambda j: (0, j))],
        out_specs=pl.BlockSpec((rows, tn), lambda j: (0, j)),
        out_shape=jax.ShapeDtypeStruct((rows, n), F32),
        compiler_params=_cparams(1),
        name="adaln",
    )(c_t, w_ada, b_ada)


def _ffn_kernel(*refs, mod_rows, with_mix, mix_gate_row, with_final):
    it = iter(refs)
    x_ref = next(it)
    mod_ref = next(it)
    nw_ref = next(it)
    win_ref = next(it)
    wo_ref = next(it)
    if with_mix:
        ma_ref = next(it)
        mb_ref = next(it)
        wmix_ref = next(it)
    if with_final:
        fnw_ref = next(it)
    o_ref = next(it)

    sh_row, sc_row, g_row = mod_rows
    x = x_ref[...]
    if with_mix:
        mix_t = jnp.concatenate([ma_ref[0].astype(BF16), mb_ref[0].astype(BF16)], axis=0)
        x = x + mod_ref[0, mix_gate_row:mix_gate_row + 1, :] * _dot_tn(mix_t, wmix_ref[...])
    h = _norm_mod(x, nw_ref[...], mod_ref[0, sh_row:sh_row + 1, :], mod_ref[0, sc_row:sc_row + 1, :])
    hb = h.astype(BF16)
    d_ff = wo_ref.shape[0]
    n_chunks = d_ff // FF_CHUNK
    acc = None
    for j in range(n_chunks):
        c0 = j * FF_CHUNK
        g = _dot(hb, win_ref[:, c0:c0 + FF_CHUNK].astype(BF16))
        u = _dot(hb, win_ref[:, d_ff + c0:d_ff + c0 + FF_CHUNK].astype(BF16))
        act = (_silu(g) * u).astype(BF16)
        part = _dot(act, wo_ref[c0:c0 + FF_CHUNK, :].astype(BF16))
        acc = part if acc is None else acc + part
    y = x + 0.5 * mod_ref[0, g_row:g_row + 1, :] * acc
    if with_final:
        ms = jnp.mean(y * y, axis=-1, keepdims=True)
        y = y * lax.rsqrt(ms + EPS) * fnw_ref[...]
    o_ref[...] = y


def _ffn(x2d, mod, nw, w_in, w_out, mod_rows, tiles_per_batch, mix=None, mix_gate_row=None, final_nw=None):
    t, d = x2d.shape
    tm = TOKEN_TILE
    d_ff = w_out.shape[0]
    assert d_ff % FF_CHUNK == 0
    row = lambda i: (i, 0)
    const = lambda i: (0, 0)
    in_specs = [pl.BlockSpec((tm, d), row),
                pl.BlockSpec((1, N_MOD, d), lambda i: (i // tiles_per_batch, 0, 0)),
                pl.BlockSpec((1, d), const),
                pl.BlockSpec(w_in.shape, const, pipeline_mode=pl.Buffered(1)),
                pl.BlockSpec(w_out.shape, const, pipeline_mode=pl.Buffered(1))]
    args = [x2d, mod, nw, w_in, w_out]
    if mix is not None:
        ma, mb, wmix = mix
        trow = lambda i: (i // tiles_per_batch, 0, i % tiles_per_batch)
        in_specs += [pl.BlockSpec((1, ma.shape[1], tm), trow),
                     pl.BlockSpec((1, mb.shape[1], tm), trow),
                     pl.BlockSpec(wmix.shape, const)]
        args += [ma, mb, wmix]
    if final_nw is not None:
        in_specs.append(pl.BlockSpec((1, d), const))
        args.append(final_nw)
    kern = functools.partial(_ffn_kernel, mod_rows=mod_rows, with_mix=mix is not None,
                             mix_gate_row=mix_gate_row, with_final=final_nw is not None)
    return pl.pallas_call(
        kern,
        grid=(t // tm,),
        in_specs=in_specs,
        out_specs=pl.BlockSpec((tm, d), row),
        out_shape=jax.ShapeDtypeStruct((t, d), F32),
        compiler_params=_cparams(1),
        name="ffn_mix" if mix is not None else "ffn",
    )(*args)


QPAD_W = NSA_Q_HEADS * LANES
MISC_I = N_GATES
MISC_F = N_GATES + MLSTM_HEADS
HALF_BLOCK_W = CMP_STRIDE * KV_WIDTH
B_KV = NSA_WIDTH
B_ML = B_KV + 6 * KV_WIDTH
B_MISC = B_ML + 4 * MLSTM_WIDTH


def _vt_with_ones(v):
    vt = v.T
    n = vt.shape[1]
    pad = jnp.where(lax.broadcasted_iota(jnp.int32, (VT_ROWS - NSA_HEAD_DIM, n), 0) == 0, 1.0, 0.0)
    parts = []
    for h in range(NSA_KV_HEADS):
        parts += [vt[h * NSA_HEAD_DIM:(h + 1) * NSA_HEAD_DIM, :], pad]
    return jnp.concatenate(parts, axis=0).astype(BF16)


def _inproj_kernel(x_ref, mod_ref, nw_ref, w_ref, wml_ref, wmisc_ref, b_ref,
                   qpad_ref, kc_ref, vc_ref, ksel_ref, kwin_ref, vtsel_ref, vtwin_ref,
                   ml_ref, vtm_ref, ogt_ref, misc_ref, misct_ref, rows_ref):
    tm = x_ref.shape[0]
    h = _norm_mod(x_ref[...], nw_ref[...], mod_ref[0, 3:4, :], mod_ref[0, 4:5, :])
    hb = h.astype(BF16)

    def proj(w_ref, c0, width, b0):
        return _dot(hb, w_ref[:, c0:c0 + width]) + b_ref[:, b0 + c0:b0 + c0 + width]

    def half_blocks(val, out_ref):
        rows_ref[...] = val
        pieces = [rows_ref[pl.ds(l, tm // CMP_STRIDE, stride=CMP_STRIDE), :] for l in range(CMP_STRIDE)]
        out_ref[0] = jnp.concatenate(pieces, axis=1).astype(BF16)

    scale = NSA_HEAD_DIM ** -0.5 * LOG2E
    qt = (proj(w_ref, 0, NSA_WIDTH, 0) * scale).T.astype(BF16)
    zero_rows = jnp.zeros((NSA_HEAD_DIM, tm), BF16)
    for i in range(NSA_Q_HEADS):
        slots = [zero_rows] * NSA_KV_HEADS
        slots[i // NSA_GROUP] = qt[i * NSA_HEAD_DIM:(i + 1) * NSA_HEAD_DIM, :]
        qpad_ref[0, i * LANES:(i + 1) * LANES, :] = jnp.concatenate(slots, axis=0)
    kv_c = proj(w_ref, B_KV, 2 * KV_WIDTH, 0)
    half_blocks(kv_c[:, :KV_WIDTH], kc_ref)
    half_blocks(kv_c[:, KV_WIDTH:], vc_ref)
    kv_s = proj(w_ref, B_KV + 2 * KV_WIDTH, 2 * KV_WIDTH, 0)
    ksel_ref[...] = kv_s[:, :KV_WIDTH].astype(BF16)
    vtsel_ref[0] = _vt_with_ones(kv_s[:, KV_WIDTH:])
    kv_w = proj(w_ref, B_KV + 4 * KV_WIDTH, 2 * KV_WIDTH, 0)
    kwin_ref[...] = kv_w[:, :KV_WIDTH].astype(BF16)
    vtwin_ref[0] = _vt_with_ones(kv_w[:, KV_WIDTH:])
    for i in range(2):
        ml_ref[:, i * MLSTM_WIDTH:(i + 1) * MLSTM_WIDTH] = proj(wml_ref, i * MLSTM_WIDTH, MLSTM_WIDTH, B_ML)
    vtm_ref[0] = proj(wml_ref, 2 * MLSTM_WIDTH, MLSTM_WIDTH, B_ML).T.astype(BF16)
    ogt_ref[0] = proj(wml_ref, 3 * MLSTM_WIDTH, MLSTM_WIDTH, B_ML).T
    misc = proj(wmisc_ref, 0, LANES, B_MISC)
    misc_ref[...] = misc
    misct_ref[0] = misc.T


def _inproj(x2d, mod, nw, weights, b_pack, batch, seq):
    t, d = x2d.shape
    tm = WIDE_TOKEN_TILE
    tpb = seq // tm
    row = lambda i: (i, 0)
    const = lambda i: (0, 0)
    trow = lambda i: (i // tpb, 0, i % tpb)
    hrow = lambda i: (i // tpb, i % tpb, 0)
    vt_rows = NSA_KV_HEADS * VT_ROWS
    n_half = seq // CMP_STRIDE
    out_shapes = [
        jax.ShapeDtypeStruct((batch, QPAD_W, seq), BF16),
        jax.ShapeDtypeStruct((batch, n_half, HALF_BLOCK_W), BF16),
        jax.ShapeDtypeStruct((batch, n_half, HALF_BLOCK_W), BF16),
        jax.ShapeDtypeStruct((t, KV_WIDTH), BF16),
        jax.ShapeDtypeStruct((t, KV_WIDTH), BF16),
        jax.ShapeDtypeStruct((batch, vt_rows, seq), BF16),
        jax.ShapeDtypeStruct((batch, vt_rows, seq), BF16),
        jax.ShapeDtypeStruct((t, 2 * MLSTM_WIDTH), F32),
        jax.ShapeDtypeStruct((batch, MLSTM_WIDTH, seq), BF16),
        jax.ShapeDtypeStruct((batch, MLSTM_WIDTH, seq), F32),
        jax.ShapeDtypeStruct((t, LANES), F32),
        jax.ShapeDtypeStruct((batch, LANES, seq), F32),
    ]
    out_specs = [
        pl.BlockSpec((1, QPAD_W, tm), trow),
        pl.BlockSpec((1, tm // CMP_STRIDE, HALF_BLOCK_W), hrow),
        pl.BlockSpec((1, tm // CMP_STRIDE, HALF_BLOCK_W), hrow),
        pl.BlockSpec((tm, KV_WIDTH), row),
        pl.BlockSpec((tm, KV_WIDTH), row),
        pl.BlockSpec((1, vt_rows, tm), trow),
        pl.BlockSpec((1, vt_rows, tm), trow),
        pl.BlockSpec((tm, 2 * MLSTM_WIDTH), row),
        pl.BlockSpec((1, MLSTM_WIDTH, tm), trow),
        pl.BlockSpec((1, MLSTM_WIDTH, tm), trow),
        pl.BlockSpec((tm, LANES), row),
        pl.BlockSpec((1, LANES, tm), trow),
    ]
    return pl.pallas_call(
        _inproj_kernel,
        grid=(t // tm,),
        in_specs=[pl.BlockSpec((tm, d), row),
                  pl.BlockSpec((1, N_MOD, d), lambda i: (i // tpb, 0, 0)),
                  pl.BlockSpec((1, d), const)]
                 + [pl.BlockSpec(w.shape, const) for w in weights]
                 + [pl.BlockSpec(b_pack.shape, const)],
        out_specs=out_specs,
        out_shape=out_shapes,
        scratch_shapes=[pltpu.VMEM((tm, KV_WIDTH), F32)],
        compiler_params=_cparams(1),
        name="inproj",
    )(x2d, mod, nw, *weights, b_pack)


def _split_inproj(w_in, b_in):
    d = w_in.shape[0]
    offs = np.cumsum([0, NSA_WIDTH] + [KV_WIDTH] * 6 + [N_GATES] + [MLSTM_WIDTH] * 4 + [MLSTM_HEADS] * 2)
    g0, m0, i0 = int(offs[7]), int(offs[8]), int(offs[12])
    pad = LANES - N_GATES - 2 * MLSTM_HEADS
    misc_w = jnp.concatenate([w_in[:, g0:g0 + N_GATES], w_in[:, i0:i0 + 2 * MLSTM_HEADS],
                              jnp.zeros((d, pad), w_in.dtype)], axis=1)
    misc_b = jnp.concatenate([b_in[g0:g0 + N_GATES], b_in[i0:i0 + 2 * MLSTM_HEADS], jnp.zeros((pad,), b_in.dtype)])
    w_bf = w_in.astype(BF16)
    weights = [w_bf, w_bf[:, m0:i0], misc_w.astype(BF16)]
    b_pack = jnp.concatenate([b_in[:g0], b_in[m0:i0], misc_b]).reshape(1, -1).astype(F32)
    assert b_pack.shape[1] == B_MISC + LANES
    return weights, b_pack


def _compress_kernel(xk_ref, xv_ref, wa_ref, wb_ref, pe_ref, w1_ref, w2_ref, kc_ref, vtc_ref):
    n_half = xk_ref.shape[1]

    def one(x_ref, s):
        x = x_ref[0]
        a = _dot(x, wa_ref[s])
        b = _dot(x, wb_ref[s])
        pe_term = jnp.dot(pe_ref[s], w1_ref[s], preferred_element_type=F32, precision=HIGHEST)[0:1, :]
        pe2 = jnp.concatenate([pe_term] * NSA_KV_HEADS, axis=1)
        pre = a + pltpu.roll(b, n_half - 1, 0) + pe2
        hid = 0.5 * pre * (1.0 + jnp.tanh(0.7978845608028654 * (pre + 0.044715 * pre * pre * pre)))
        return _dot(hid.astype(BF16), w2_ref[s])

    kc_ref[0] = one(xk_ref, 0).astype(BF16)
    vtc_ref[0] = _vt_with_ones(one(xv_ref, 1))


def _compress(xk, xv, wa, wb, pe8, w1, w2e):
    batch, n_half, width = xk.shape
    bsel = lambda b: (b, 0, 0)
    c3 = lambda b: (0, 0, 0)
    vt_rows = NSA_KV_HEADS * VT_ROWS
    return pl.pallas_call(
        _compress_kernel,
        grid=(batch,),
        in_specs=[pl.BlockSpec((1, n_half, width), bsel), pl.BlockSpec((1, n_half, width), bsel),
                  pl.BlockSpec(wa.shape, c3), pl.BlockSpec(wb.shape, c3),
                  pl.BlockSpec(pe8.shape, c3), pl.BlockSpec(w1.shape, c3), pl.BlockSpec(w2e.shape, c3)],
        out_specs=[pl.BlockSpec((1, n_half, KV_WIDTH), bsel), pl.BlockSpec((1, vt_rows, n_half), bsel)],
        out_shape=[jax.ShapeDtypeStruct((batch, n_half, KV_WIDTH), BF16),
                   jax.ShapeDtypeStruct((batch, vt_rows, n_half), BF16)],
        compiler_params=_cparams(1),
        name="compress",
    )(xk, xv, wa, wb, pe8, w1, w2e)


def _pack_compress(pe, w1, w2):
    half = CMP_BLOCK // 2
    w1r = w1.reshape(CMP_BLOCK, NSA_HEAD_DIM, CMP_HIDDEN)
    eye = jnp.eye(NSA_KV_HEADS, dtype=w1.dtype)

    def expand(w_half):
        z = w_half[:, None, :, None, :] * eye[None, :, None, :, None]
        return z.reshape(half * KV_WIDTH, NSA_KV_HEADS * CMP_HIDDEN)

    wa = expand(w1r[:half]).astype(BF16)
    wb = expand(w1r[half:]).astype(BF16)
    w2e = (w2[None, :, None, :] * eye[:, None, :, None]).reshape(NSA_KV_HEADS * CMP_HIDDEN, KV_WIDTH).astype(BF16)
    pe8 = jnp.concatenate([pe.reshape(1, CMP_BLOCK * NSA_HEAD_DIM),
                           jnp.zeros((7, CMP_BLOCK * NSA_HEAD_DIM), pe.dtype)], axis=0)
    return wa, wb, pe8, w1, w2e


def _q_cols(q_ref, h, cols):
    return jnp.concatenate(
        [q_ref[0, (NSA_GROUP * h + g) * LANES:(NSA_GROUP * h + g + 1) * LANES, cols] for g in range(NSA_GROUP)],
        axis=1)


def _q_aug(q_ref, qs_ref, h, cols=slice(0, QT)):
    return jnp.concatenate([_q_cols(q_ref, h, cols), qs_ref[h]], axis=0)


def _slope2(head):
    return float(2.0 ** (-8.0 * (head + 1) / NSA_Q_HEADS)) * LOG2E


def _bf16_split3(x):
    parts = []
    for _ in range(3):
        p = float(np.asarray(x, np.float32).astype(BF16).astype(np.float32))
        parts.append(p)
        x = x - p
    return parts


def _init_alibi_operands(qs_ref, kpos_ref, key_stride):
    lane = lax.broadcasted_iota(jnp.int32, (QT, LANES), 1)
    row = lax.broadcasted_iota(jnp.int32, (QT, LANES), 0)
    kpos_ref[...] = jnp.where(lane < 3, row.astype(F32), 0.0).astype(BF16)
    srow = lax.broadcasted_iota(jnp.int32, (LANES, QT), 0)
    for h in range(NSA_KV_HEADS):
        for g in range(NSA_GROUP):
            hi, mid, lo = _bf16_split3(key_stride * _slope2(NSA_GROUP * h + g))
            blk = jnp.where(srow == 0, hi, jnp.where(srow == 1, mid, jnp.where(srow == 2, lo, 0.0)))
            qs_ref[h, :, g * QT:(g + 1) * QT] = blk.astype(BF16)


def _gate_row(gt_ref, head, branch, cols):
    r = head * 3 + branch
    return _sigmoid(gt_ref[0, r:r + 1, cols])


def _softmax_jobs(jobs):
    scores = [_dot(jnp.concatenate(job[1], axis=0), job[0]) for job in jobs]
    staged = []
    for (q_aug, k_tiles, masks, offs, value_groups, state), st_all in zip(jobs, scores):
        n_t = len(k_tiles)
        per_g = []
        for g in range(NSA_GROUP):
            sl = slice(g * QT, (g + 1) * QT)
            ss = []
            for u in range(n_t):
                s = st_all[u * QT:(u + 1) * QT, sl]
                ss.append(s if masks[u] is None else jnp.where(masks[u], s, NEG))
            mx = None
            for u in range(n_t):
                cand = jnp.max(ss[u], axis=0, keepdims=True) + offs[u][g]
                mx = cand if mx is None else jnp.maximum(mx, cand)
            alpha = None
            if state is None:
                m_new = mx
            else:
                m_ref, _, h = state
                m_old = m_ref[h, :, sl]
                m_new = jnp.maximum(m_old, mx)
                alpha = jnp.exp2(m_old - m_new)
                m_ref[h, :, sl] = m_new
            m_use = jnp.where(m_new < 0.5 * NEG, 0.0, m_new)
            per_g.append((ss, m_use, alpha))
        staged.append(per_g)
    ets = []
    for (q_aug, k_tiles, masks, offs, value_groups, state), per_g in zip(jobs, staged):
        n_t = len(k_tiles)
        rows = [jnp.concatenate([jnp.exp2(per_g[g][0][u] - (per_g[g][1] - offs[u][g])).astype(BF16)
                                 for g in range(NSA_GROUP)], axis=1) for u in range(n_t)]
        ets.append(jnp.concatenate(rows, axis=0))
    results = []
    for (q_aug, k_tiles, masks, offs, value_groups, state), per_g, et in zip(jobs, staged, ets):
        mats = [jnp.concatenate(grp, axis=1) for grp in value_groups]
        pv_all = _dot(jnp.concatenate(mats, axis=0), et)
        pvs, r0 = [], 0
        for mat in mats:
            pvs.append(pv_all[r0:r0 + mat.shape[0], :])
            r0 += mat.shape[0]
        if state is None:
            results.append(pvs)
        else:
            _, acc_ref, h = state
            acc_ref[h] = jnp.concatenate([pg[2] for pg in per_g], axis=1) * acc_ref[h] + pvs[0]
            results.append(None)
    return results


def _gated(gt_ref, h, branch, pv, cols=slice(0, QT)):
    inv = 1.0 / jnp.maximum(pv[NSA_HEAD_DIM:NSA_HEAD_DIM + 1, :], 1e-30)
    out = []
    for g in range(NSA_GROUP):
        sl = slice(g * QT, (g + 1) * QT)
        out.append(pv[0:NSA_HEAD_DIM, sl] * (inv[:, sl] * _gate_row(gt_ref, NSA_GROUP * h + g, branch, cols)))
    return out


def _nsa_cmp_kernel(q_ref, kc_ref, vtc_ref, gt_ref, oc_ref, sel_ref, cnt_ref,
                    qs_ref, kpos_ref, ovt_ref, pvo_ref, pvi_ref):
    ncp = kc_ref.shape[1]
    n_ct = ncp // QT
    n_sel = sel_ref.shape[3]
    tiles = range(CMP_TILES)
    starts = [(pl.program_id(1) * CMP_TILES + s) * QT for s in tiles]
    cols = [slice(s * QT, (s + 1) * QT) for s in tiles]
    units = [(s, h) for s in tiles for h in range(NSA_KV_HEADS)]
    tile_span = QT * CMP_STRIDE

    @pl.when((pl.program_id(0) == 0) & (pl.program_id(1) == 0))
    def _():
        _init_alibi_operands(qs_ref, kpos_ref, float(CMP_STRIDE))
        j_o = lax.broadcasted_iota(jnp.int32, (n_sel, ncp), 0)
        c_o = lax.broadcasted_iota(jnp.int32, (n_sel, ncp), 1)
        ov = ((c_o * CMP_STRIDE <= j_o * SEL_BLOCK + (SEL_BLOCK - 1))
              & (c_o * CMP_STRIDE + (CMP_BLOCK - 1) >= j_o * SEL_BLOCK))
        ovt_ref[...] = jnp.where(ov, 1.0, 0.0).astype(BF16)

    a0 = (lax.broadcasted_iota(jnp.int32, (QT, QT), 1)
          - CMP_STRIDE * lax.broadcasted_iota(jnp.int32, (QT, QT), 0))
    q_augs = [_q_aug(q_ref, qs_ref, h, cols[s]) for s, h in units]
    n_vis = jnp.minimum((starts[-1] + QT - CMP_BLOCK) // tile_span + 1, n_ct)

    for k in range(1, n_ct + 1):
        @pl.when(n_vis == k)
        def _(k=k):
            jobs = []
            for i, (s, h) in enumerate(units):
                k_tiles, masks, offs, vts, ovs = [], [], [], [], []
                for ci in range(k):
                    cs = slice(ci * QT, (ci + 1) * QT)
                    first_end = ci * tile_span + CMP_BLOCK - 1
                    k_tiles.append(jnp.concatenate([kc_ref[0, cs, :], kpos_ref[...]], axis=1))
                    masks.append(a0 + (starts[s] - first_end) >= 0)
                    offs.append([_slope2(NSA_GROUP * h + g) * first_end for g in range(NSA_GROUP)])
                    vts.append(vtc_ref[0, h * VT_ROWS:(h + 1) * VT_ROWS, cs])
                    ovs.append(ovt_ref[:, cs])
                jobs.append((q_augs[i], k_tiles, masks, offs, [vts, ovs], None))
            for i, (pvo, pvi) in enumerate(_softmax_jobs(jobs)):
                pvo_ref[i] = pvo
                pvi_ref[i] = pvi

    j_io = lax.broadcasted_iota(jnp.int32, (n_sel, QT), 0)
    q_io = lax.broadcasted_iota(jnp.int32, (n_sel, QT), 1)
    ones8 = jnp.ones((8, QT), BF16)

    taken = -(2.0 ** 126)
    scores = []
    for i, (s, h) in enumerate(units):
        t_j = starts[s] + q_io
        cur = t_j // SEL_BLOCK
        valid = j_io * SEL_BLOCK <= t_j
        forced = (j_io == 0) | (j_io == cur) | (j_io == cur - 1)
        pvo = pvo_ref[i]
        o_c = _gated(gt_ref, h, 0, pvo, cols[s])
        inv = 1.0 / jnp.maximum(pvo[NSA_HEAD_DIM:NSA_HEAD_DIM + 1, :], 1e-30)
        imp = None
        for g in range(NSA_GROUP):
            sl = slice(g * QT, (g + 1) * QT)
            oc_ref[0, NSA_GROUP * h + g, :, cols[s]] = o_c[g]
            term = pvi_ref[i, :, sl] * inv[:, sl]
            imp = term if imp is None else imp + term
        scores.append(jnp.where(forced, taken, jnp.where(valid, imp, -FORCE_BONUS)))

    n_cls = n_sel // TOPK_ROWS
    cls = jnp.minimum((starts[-1] + QT + SEL_BLOCK * TOPK_ROWS - 1) // (SEL_BLOCK * TOPK_ROWS), n_cls)

    for k in range(1, n_cls + 1):
        @pl.when(cls == k)
        def _(rows=k * TOPK_ROWS):
            j_sub = j_io[:rows]

            def pick(_, carry):
                out = []
                for sc in carry:
                    mx = jnp.max(sc, axis=0, keepdims=True)
                    idx = jnp.min(jnp.where(sc == mx, j_sub, n_sel), axis=0, keepdims=True)
                    out.append(jnp.where(j_sub == idx, taken, sc))
                return tuple(out)

            picked = lax.fori_loop(0, SEL_TOPK - N_FORCED, pick, tuple(sc[:rows] for sc in scores))
            for i, (s, h) in enumerate(units):
                sel = jnp.where(picked[i] <= 0.5 * taken, 1.0, 0.0)
                if rows < n_sel:
                    sel = jnp.concatenate([sel, jnp.zeros((n_sel - rows, QT), F32)], axis=0)
                sel_ref[0, h, s] = sel
                cnt = _dot_nt(ones8, sel.astype(BF16))
                cnt_ref[0, s, h] = cnt[0:1, :]


def _nsa_cmp(qpad3, kc, vtc, gates_t, batch, seq):
    nb = seq // QT
    n_sel = seq // SEL_BLOCK
    ncp = kc.shape[1]
    assert ncp % QT == 0 and nb % CMP_TILES == 0
    vt_rows = NSA_KV_HEADS * VT_ROWS
    qw = QT * CMP_TILES
    n_units = CMP_TILES * NSA_KV_HEADS
    return pl.pallas_call(
        _nsa_cmp_kernel,
        grid=(batch, nb // CMP_TILES),
        in_specs=[pl.BlockSpec((1, QPAD_W, qw), lambda b, i: (b, 0, i)),
                  pl.BlockSpec((1, ncp, KV_WIDTH), lambda b, i: (b, 0, 0)),
                  pl.BlockSpec((1, vt_rows, ncp), lambda b, i: (b, 0, 0)),
                  pl.BlockSpec((1, N_GATES, qw), lambda b, i: (b, 0, i))],
        out_specs=[pl.BlockSpec((1, NSA_Q_HEADS, NSA_HEAD_DIM, qw), lambda b, i: (b, 0, 0, i)),
                   pl.BlockSpec((1, NSA_KV_HEADS, CMP_TILES, n_sel, QT), lambda b, i: (b, 0, i, 0, 0)),
                   pl.BlockSpec((1, CMP_TILES, NSA_KV_HEADS, 1, n_sel), lambda b, i: (b, i, 0, 0, 0))],
        out_shape=[jax.ShapeDtypeStruct((batch, NSA_Q_HEADS, NSA_HEAD_DIM, seq), F32),
                   jax.ShapeDtypeStruct((batch, NSA_KV_HEADS, nb, n_sel, QT), F32),
                   jax.ShapeDtypeStruct((batch, nb, NSA_KV_HEADS, 1, n_sel), F32)],
        scratch_shapes=[pltpu.VMEM((NSA_KV_HEADS, LANES, NSA_GROUP * QT), BF16),
                        pltpu.VMEM((QT, LANES), BF16),
                        pltpu.VMEM((n_sel, ncp), BF16),
                        pltpu.VMEM((n_units, VT_ROWS, NSA_GROUP * QT), F32),
                        pltpu.VMEM((n_units, n_sel, NSA_GROUP * QT), F32)],
        compiler_params=_cparams(2),
        name="nsa_cmp",
    )(qpad3, kc, vtc, gates_t)


def _nsa_sel_kernel(count_ref, list_ref, q_ref, ksel_ref, vtsel_ref, kwin_ref, vtwin_ref, selm_ref, gt_ref,
                    oc_ref, o_ref, m_sel, acc_sel, qs_ref, kpos_ref, part_ref, kposm_ref):
    b = pl.program_id(0)
    nb = pl.num_programs(1) * SEL_TILES
    tiles = range(SEL_TILES)
    qbs = [pl.program_id(1) * SEL_TILES + s for s in tiles]
    cols = [slice(s * QT, (s + 1) * QT) for s in tiles]
    units = [(s, h) for s in tiles for h in range(NSA_KV_HEADS)]
    k_io = lax.broadcasted_iota(jnp.int32, (QT, QT), 0)
    q_io = lax.broadcasted_iota(jnp.int32, (QT, QT), 1)
    causal = q_io >= k_io
    in_window = q_io < k_io

    pen_row0 = 3
    pen_rows = 16
    assert pen_row0 + 2 * SEL_UNROLL <= pen_rows

    @pl.when((b == 0) & (pl.program_id(1) == 0))
    def _():
        _init_alibi_operands(qs_ref, kpos_ref, 1.0)
        lane = lax.broadcasted_iota(jnp.int32, (QT, LANES), 1)
        row = lax.broadcasted_iota(jnp.int32, (QT, LANES), 0)
        base = jnp.where(lane < pen_row0, row.astype(F32), 0.0)
        for u in range(SEL_UNROLL):
            ind = jnp.where(lane == pen_row0 + 2 * u, jnp.where(row < SEL_BLOCK, 1.0, 0.0),
                            jnp.where(lane == pen_row0 + 2 * u + 1, jnp.where(row >= SEL_BLOCK, 1.0, 0.0), 0.0))
            kposm_ref[u] = (base + ind).astype(BF16)

    def k_aug(k_ref, t, u=None):
        off = pl.multiple_of(t * QT, QT)
        tail = kpos_ref[...] if u is None else kposm_ref[u]
        return jnp.concatenate([k_ref[0, pl.ds(off, QT), :], tail], axis=1)

    def v_ext(vt_ref, h, t):
        return vt_ref[0, h * VT_ROWS:(h + 1) * VT_ROWS, pl.ds(pl.multiple_of(t * QT, QT), QT)]

    def tile_offs(h, t, live=None):
        pos = (t * QT).astype(F32)
        out = [_slope2(NSA_GROUP * h + g) * pos for g in range(NSA_GROUP)]
        return out if live is None else [jnp.where(live, o, NEG) for o in out]

    def penalties(s, h, t):
        mrow = selm_ref[0, h, s, pl.ds(2 * t, 2), :]
        return (mrow - 1.0) * (-NEG)

    q_tops = [_q_cols(q_ref, h, cols[s]) for s, h in units]
    q_augs = [jnp.concatenate([q_tops[i], qs_ref[h]], axis=0) for i, (s, h) in enumerate(units)]
    zero_rows = jnp.zeros((LANES - pen_rows, NSA_GROUP * QT), BF16)

    def q_aug_sel(i, h, pens):
        blk = jnp.concatenate([jnp.zeros((pen_row0, QT), F32)] + pens
                              + [jnp.zeros((pen_rows - pen_row0 - 2 * len(pens), QT), F32)], axis=0)
        ext = qs_ref[h, 0:pen_rows, :].astype(F32) + jnp.concatenate([blk] * NSA_GROUP, axis=1)
        return jnp.concatenate([q_tops[i], ext.astype(BF16), zero_rows], axis=0)

    slot = [(b * nb + qbs[s]) * NSA_KV_HEADS + h for s, h in units]
    n_tiles = [count_ref[sl] for sl in slot]

    m_sel[...] = jnp.full(m_sel.shape, NEG, F32)
    acc_sel[...] = jnp.zeros(acc_sel.shape, F32)

    def sel_jobs(trip, first, step):
        jobs = []
        for i, (s, h) in enumerate(units):
            k_tiles, masks, offs, vts, pens = [], [], [], [], []
            for u in range(SEL_UNROLL):
                if first and step == 0 and u == 0:
                    t, mask, off = qbs[s], causal, tile_offs(h, qbs[s])
                else:
                    idx = (trip * SEL_CHAIN + step) * SEL_UNROLL + u - 1
                    live = idx < n_tiles[i]
                    t = jnp.where(live, list_ref[slot[i] * nb + jnp.minimum(idx, nb - 1)], 0)
                    mask, off = None, tile_offs(h, t, live)
                k_tiles.append(k_aug(ksel_ref, t, u))
                masks.append(mask)
                offs.append(off)
                vts.append(v_ext(vtsel_ref, h, t))
                pens.append(penalties(s, h, t))
            jobs.append((q_aug_sel(i, h, pens), k_tiles, masks, offs, [vts], (m_sel, acc_sel, i)))
        return jobs

    n_win = WINDOW // QT + 1
    win_jobs = []
    for i, (s, h) in enumerate(units):
        k_tiles, masks, offs, vts = [], [], [], []
        for u in range(n_win):
            raw = qbs[s] - (n_win - 1) + u
            t = jnp.maximum(raw, 0)
            k_tiles.append(k_aug(kwin_ref, t))
            masks.append(in_window if u == 0 else causal if u == n_win - 1 else None)
            offs.append(tile_offs(h, t, raw >= 0))
            vts.append(v_ext(vtwin_ref, h, t))
        win_jobs.append((q_augs[i], k_tiles, masks, offs, [vts], None))
    jobs = sel_jobs(0, True, 0) + win_jobs
    for step in range(1, SEL_CHAIN):
        jobs += sel_jobs(0, True, step)
    results = _softmax_jobs(jobs)
    for i, (s, h) in enumerate(units):
        o_w = _gated(gt_ref, h, 2, results[len(units) + i][0], cols[s])
        for g in range(NSA_GROUP):
            head = NSA_GROUP * h + g
            part_ref[s, head] = oc_ref[0, head, :, cols[s]] + o_w[g]

    def sel_body(trip, carry):
        jobs = []
        for step in range(SEL_CHAIN):
            jobs += sel_jobs(trip, False, step)
        _softmax_jobs(jobs)
        return carry

    per_trip = SEL_UNROLL * SEL_CHAIN
    n_trips = (functools.reduce(jnp.maximum, n_tiles) + per_trip) // per_trip
    lax.fori_loop(1, n_trips, sel_body, 0)

    for i, (s, h) in enumerate(units):
        o_s = _gated(gt_ref, h, 1, acc_sel[i], cols[s])
        for g in range(NSA_GROUP):
            head = NSA_GROUP * h + g
            o_ref[0, head, :, cols[s]] = part_ref[s, head] + o_s[g]


def _tile_lists(cnt, batch, nb):
    pair = cnt.reshape(batch, nb, NSA_KV_HEADS, nb, 2).sum(axis=-1)
    p_io = lax.broadcasted_iota(jnp.int32, pair.shape, 3)
    flag = (pair > 0.5) & (p_io < lax.broadcasted_iota(jnp.int32, pair.shape, 1))
    rank = jnp.cumsum(flag.astype(jnp.int32), axis=-1) - 1
    slot_io = lax.broadcasted_iota(jnp.int32, pair.shape + (nb,), 4)
    hit = flag[..., None] & (rank[..., None] == slot_io)
    lists = jnp.sum(jnp.where(hit, p_io[..., None], 0), axis=3)
    counts = jnp.sum(flag.astype(jnp.int32), axis=-1)
    return counts.reshape(-1), lists.reshape(-1)


def _nsa_sel(counts, lists, qpad3, ksel, vtsel, kwin, vtwin, selm, gates_t, oc, batch, seq):
    nb = seq // QT
    assert nb % SEL_TILES == 0
    res = lambda b, i, c, l: (b, 0, 0)
    vt_rows = NSA_KV_HEADS * VT_ROWS
    qw = QT * SEL_TILES
    n_units = SEL_TILES * NSA_KV_HEADS
    grid_spec = pltpu.PrefetchScalarGridSpec(
        num_scalar_prefetch=2,
        grid=(batch, nb // SEL_TILES),
        in_specs=[pl.BlockSpec((1, QPAD_W, qw), lambda b, i, c, l: (b, 0, i)),
                  pl.BlockSpec((1, seq, KV_WIDTH), res),
                  pl.BlockSpec((1, vt_rows, seq), res),
                  pl.BlockSpec((1, seq, KV_WIDTH), res),
                  pl.BlockSpec((1, vt_rows, seq), res),
                  pl.BlockSpec((1, NSA_KV_HEADS, SEL_TILES, 2 * nb, QT), lambda b, i, c, l: (b, 0, i, 0, 0)),
                  pl.BlockSpec((1, N_GATES, qw), lambda b, i, c, l: (b, 0, i)),
                  pl.BlockSpec((1, NSA_Q_HEADS, NSA_HEAD_DIM, qw), lambda b, i, c, l: (b, 0, 0, i))],
        out_specs=pl.BlockSpec((1, NSA_Q_HEADS, NSA_HEAD_DIM, qw), lambda b, i, c, l: (b, 0, 0, i)),
        scratch_shapes=[pltpu.VMEM((n_units, 1, NSA_GROUP * QT), F32),
                        pltpu.VMEM((n_units, VT_ROWS, NSA_GROUP * QT), F32),
                        pltpu.VMEM((NSA_KV_HEADS, LANES, NSA_GROUP * QT), BF16),
                        pltpu.VMEM((QT, LANES), BF16),
                        pltpu.VMEM((SEL_TILES, NSA_Q_HEADS, NSA_HEAD_DIM, QT), F32),
                        pltpu.VMEM((SEL_UNROLL, QT, LANES), BF16)],
    )
    return pl.pallas_call(
        _nsa_sel_kernel,
        grid_spec=grid_spec,
        out_shape=jax.ShapeDtypeStruct((batch, NSA_Q_HEADS, NSA_HEAD_DIM, seq), F32),
        compiler_params=_cparams(2),
        name="nsa_sel",
    )(counts, lists, qpad3, ksel, vtsel, kwin, vtwin, selm, gates_t, oc)


def _mlstm_kernel(q_ref, k_ref, vt_ref, ogt_ref, misc_ref, misct_ref, cw_ref, cb_ref, nw_ref,
                  o_ref, xq_ref, xk_ref, c_ref, n_ref, m_ref):
    tm = q_ref.shape[0]
    L = MLSTM_L
    hd = MLSTM_HEAD_DIM
    halo = 8

    @pl.when(pl.program_id(1) == 0)
    def _():
        xq_ref[0:halo, :] = jnp.zeros((halo, MLSTM_WIDTH), F32)
        xk_ref[0:halo, :] = jnp.zeros((halo, MLSTM_WIDTH), F32)
        c_ref[...] = jnp.zeros_like(c_ref)
        n_ref[...] = jnp.zeros_like(n_ref)
        m_ref[...] = jnp.zeros_like(m_ref)

    def conv(x_ref, buf_ref, col0):
        buf_ref[halo:halo + tm, :] = x_ref[...]
        y = cb_ref[:, col0:col0 + MLSTM_WIDTH]
        for kk in range(CONV_WIDTH):
            r0 = halo - (CONV_WIDTH - 1) + kk
            y = y + cw_ref[kk:kk + 1, col0:col0 + MLSTM_WIDTH] * buf_ref[r0:r0 + tm, :]
        tail = buf_ref[tm:tm + halo, :]
        buf_ref[0:halo, :] = tail
        return _silu(y)

    qc = conv(q_ref, xq_ref, 0)
    kc = conv(k_ref, xk_ref, MLSTM_WIDTH) * (hd ** -0.5)

    misc = misc_ref[...]
    misct = misct_ref[0]
    lf_c = -jnp.log(1.0 + jnp.exp(-misc))
    lf_r = -jnp.log(1.0 + jnp.exp(-misct))
    row_io = lax.broadcasted_iota(jnp.int32, (L, L), 0)
    col_io = lax.broadcasted_iota(jnp.int32, (L, L), 1)
    tril = jnp.where(col_io <= row_io, 1.0, 0.0).astype(BF16)
    triu = jnp.where(row_io <= col_io, 1.0, 0.0).astype(BF16)

    def split3(x):
        hi = x.astype(BF16)
        r1 = x - hi.astype(F32)
        mid = r1.astype(BF16)
        return hi, mid, (r1 - mid.astype(F32)).astype(BF16)

    lf_c3 = split3(lf_c)
    lf_r3 = split3(lf_r)
    b_c = jnp.concatenate([sum(_dot(tril, p[ci * L:(ci + 1) * L, :]) for p in lf_c3)
                           for ci in range(tm // L)], axis=0)
    b_r = jnp.concatenate([sum(_dot(p[:, ci * L:(ci + 1) * L], triu) for p in lf_r3)
                           for ci in range(tm // L)], axis=1)

    tri_t = lax.broadcasted_iota(jnp.int32, (L, L), 0) <= lax.broadcasted_iota(jnp.int32, (L, L), 1)
    nw_cols = [jnp.broadcast_to(nw_ref[:, h * hd:(h + 1) * hd], (hd, hd)).T for h in range(MLSTM_HEADS)]

    def split3_rows(row):
        hi = row.astype(BF16)
        r1 = row - hi.astype(F32)
        mid = r1.astype(BF16)
        lo = (r1 - mid.astype(F32)).astype(BF16)
        sub = lax.broadcasted_iota(jnp.int32, (8, row.shape[1]), 0)
        out = jnp.where(sub == 0, hi.astype(F32), jnp.where(sub == 1, mid.astype(F32),
                                                            jnp.where(sub == 2, lo.astype(F32), 0.0)))
        return out.astype(BF16)

    def rows_sum3(x8):
        return x8[0:1, :] + x8[1:2, :] + x8[2:3, :]

    for ci in range(tm // L):
        r0 = ci * L
        for h in range(MLSTM_HEADS):
            cs = slice(h * hd, (h + 1) * hd)
            qb16 = qc[r0:r0 + L, cs].astype(BF16)
            kb16 = kc[r0:r0 + L, cs].astype(BF16)
            vt = vt_ref[0, cs, r0:r0 + L]
            a_col = (misc[r0:r0 + L, MISC_I + h:MISC_I + h + 1]
                     - b_c[r0:r0 + L, MISC_F + h:MISC_F + h + 1])
            brow = b_r[MISC_F + h:MISC_F + h + 1, r0:r0 + L]
            lirow = misct[MISC_I + h:MISC_I + h + 1, r0:r0 + L]
            m_prev = m_ref[h]
            ct = c_ref[h]
            nrow = n_ref[h]

            dmat = jnp.where(tri_t, brow + a_col, NEG)
            m_inter = brow + m_prev
            m_t = jnp.maximum(jnp.max(dmat, axis=0, keepdims=True), m_inter)
            w = jnp.exp(dmat - m_t) * _dot_nt(kb16, qb16)
            decay = jnp.exp(m_inter - m_t)
            num = _dot(vt, w.astype(BF16)) + decay * _dot_nt(ct.astype(BF16), qb16)
            nq = rows_sum3(_dot_nt(split3_rows(nrow), qb16))
            den = jnp.sum(w, axis=0, keepdims=True) + decay * nq
            hout = num * (1.0 / jnp.maximum(jnp.abs(den), jnp.exp(-m_t)))

            b_last = brow[:, L - 1:L]
            grow = b_last - brow + lirow
            m_new = jnp.maximum(b_last + m_prev, jnp.max(grow, axis=1, keepdims=True))
            wk = jnp.exp(grow - m_new)
            d_c = jnp.exp(b_last + m_prev - m_new)
            c_ref[h] = d_c * ct + _dot((vt.astype(F32) * wk).astype(BF16), kb16)
            n_ref[h] = d_c * nrow + rows_sum3(_dot(split3_rows(wk), kb16))
            m_ref[h] = m_new

            hn = hout * lax.rsqrt(jnp.mean(hout * hout, axis=0, keepdims=True) + EPS) * nw_cols[h]
            o_ref[0, cs, r0:r0 + L] = (hn * _sigmoid(ogt_ref[0, cs, r0:r0 + L])).astype(o_ref.dtype)


def _mlstm(ml, vt_m, og_t, misc, misc_t, conv_w, conv_b, norm_w, batch, seq):
    tm = WIDE_TOKEN_TILE
    tpb = seq // tm
    col = lambda j: (lambda b, i: (b * tpb + i, j))
    const = lambda b, i: (0, 0)
    trow = lambda b, i: (b, 0, i)
    return pl.pallas_call(
        _mlstm_kernel,
        grid=(batch, tpb),
        in_specs=[pl.BlockSpec((tm, MLSTM_WIDTH), col(0)),
                  pl.BlockSpec((tm, MLSTM_WIDTH), col(1)),
                  pl.BlockSpec((1, MLSTM_WIDTH, tm), trow),
                  pl.BlockSpec((1, MLSTM_WIDTH, tm), trow),
                  pl.BlockSpec((tm, LANES), col(0)),
                  pl.BlockSpec((1, LANES, tm), trow),
                  pl.BlockSpec(conv_w.shape, const),
                  pl.BlockSpec(conv_b.shape, const),
                  pl.BlockSpec(norm_w.shape, const)],
        out_specs=pl.BlockSpec((1, MLSTM_WIDTH, tm), trow),
        out_shape=jax.ShapeDtypeStruct((batch, MLSTM_WIDTH, seq), BF16),
        scratch_shapes=[pltpu.VMEM((tm + 8, MLSTM_WIDTH), F32),
                        pltpu.VMEM((tm + 8, MLSTM_WIDTH), F32),
                        pltpu.VMEM((MLSTM_HEADS, MLSTM_HEAD_DIM, MLSTM_HEAD_DIM), F32),
                        pltpu.VMEM((MLSTM_HEADS, 1, MLSTM_HEAD_DIM), F32),
                        pltpu.VMEM((MLSTM_HEADS, 1, 1), F32)],
        compiler_params=_cparams(2),
        name="mlstm",
    )(ml, ml, vt_m, og_t, misc, misc_t, conv_w, conv_b, norm_w)


def _layer(x2d, c, batch, seq, w_ada, b_ada, norm_ffn1_w, ffn1_w_in, ffn1_w_out, norm_mix_w, w_in, b_in,
           cmp_k_pe, cmp_k_w1, cmp_k_w2, cmp_v_pe, cmp_v_w1, cmp_v_w2, conv_w, conv_b, mlstm_norm_w, w_out,
           norm_ffn2_w, ffn2_w_in, ffn2_w_out, final_nw):
    d = x2d.shape[1]
    tpb = seq // TOKEN_TILE
    mod = _adaln(c.T, w_ada, b_ada.reshape(1, -1)).reshape(batch, N_MOD, d)

    x1 = _ffn(x2d, mod, norm_ffn1_w.reshape(1, d), ffn1_w_in, ffn1_w_out,
              (0, 1, 2), tpb)

    weights, b_pack = _split_inproj(w_in, b_in)
    (qpad, kc, vc, ksel, kwin, vtsel, vtwin, ml, vt_m, og_t, misc, misc_t) = _inproj(
        x1, mod, norm_mix_w.reshape(1, d), weights, b_pack, batch, seq)

    pk = _pack_compress(cmp_k_pe, cmp_k_w1, cmp_k_w2)
    pv = _pack_compress(cmp_v_pe, cmp_v_w1, cmp_v_w2)
    stacked = [jnp.stack([a, bb]) for a, bb in zip(pk, pv)]
    kcmp, vtcmp = _compress(kc, vc, *stacked)

    nb = seq // QT
    qpad3 = qpad
    gates_t = misc_t
    oc, sel, cnt = _nsa_cmp(qpad3, kcmp, vtcmp, gates_t, batch, seq)
    counts, lists = _tile_lists(cnt, batch, nb)
    o_t = _nsa_sel(counts, lists, qpad3, ksel.reshape(batch, seq, KV_WIDTH), vtsel, kwin.reshape(batch, seq, KV_WIDTH),
                   vtwin, sel, gates_t, oc, batch, seq)
    o_nsa = o_t.reshape(batch, NSA_WIDTH, seq)

    o_ml = _mlstm(ml, vt_m, og_t, misc, misc_t, conv_w, conv_b.reshape(1, -1), mlstm_norm_w.reshape(1, -1), batch, seq)

    return _ffn(x1, mod, norm_ffn2_w.reshape(1, d), ffn2_w_in, ffn2_w_out,
                (6, 7, 8), tpb, mix=(o_nsa, o_ml, w_out.astype(BF16)), mix_gate_row=5, final_nw=final_nw)


def kernel(x, c, w_ada, b_ada, norm_ffn1_w, ffn1_w_in, ffn1_w_out, norm_mix_w, w_in, b_in, cmp_k_pe, cmp_k_w1, cmp_k_w2, cmp_v_pe, cmp_v_w1, cmp_v_w2, conv_w, conv_b, mlstm_norm_w, w_out, norm_ffn2_w, ffn2_w_in, ffn2_w_out, final_norm_w):
    batch, seq, d = x.shape
    depth = w_ada.shape[0]
    assert depth == 1 and seq % (QT * CMP_STRIDE) == 0 and seq // SEL_BLOCK >= SEL_TOPK
    y = _layer(x.reshape(batch * seq, d), c, batch, seq, w_ada[0], b_ada[0], norm_ffn1_w[0], ffn1_w_in[0],
               ffn1_w_out[0], norm_mix_w[0], w_in[0], b_in[0], cmp_k_pe[0], cmp_k_w1[0], cmp_k_w2[0],
               cmp_v_pe[0], cmp_v_w1[0], cmp_v_w2[0], conv_w[0], conv_b[0], mlstm_norm_w[0], w_out[0],
               norm_ffn2_w[0], ffn2_w_in[0], ffn2_w_out[0], final_norm_w.reshape(1, d))
    return y.reshape(batch, seq, d)
```

```python
import functools

import numpy as np
import jax
import jax.numpy as jnp
from jax import lax
from jax.experimental import pallas as pl
from jax.experimental.pallas import tpu as pltpu

NSA_Q_HEADS = 8
NSA_KV_HEADS = 2
NSA_GROUP = NSA_Q_HEADS // NSA_KV_HEADS
NSA_HEAD_DIM = 64
CMP_BLOCK = 32
CMP_STRIDE = 16
CMP_HIDDEN = 128
SEL_BLOCK = 64
SEL_TOPK = 16
WINDOW = 512
FORCE_BONUS = 1.0e4
N_FORCED = 3
assert FORCE_BONUS > NSA_GROUP
MLSTM_HEADS = 4
MLSTM_HEAD_DIM = 128
CONV_WIDTH = 4
N_MOD = 9
EPS = 1e-6
NEG = -1e30

NSA_WIDTH = NSA_Q_HEADS * NSA_HEAD_DIM
KV_WIDTH = NSA_KV_HEADS * NSA_HEAD_DIM
MLSTM_WIDTH = MLSTM_HEADS * MLSTM_HEAD_DIM
N_GATES = 3 * NSA_Q_HEADS

LANES = 128
QT = 128
TOKEN_TILE = 512
WIDE_TOKEN_TILE = 1024
FF_CHUNK = 256
MLSTM_L = 128
SEL_UNROLL = 6
SEL_CHAIN = 2
CMP_TILES = 2
SEL_TILES = 2
TOPK_ROWS = 32
VT_ROWS = 80
VMEM_LIMIT = 56 * 1024 * 1024

F32 = jnp.float32
BF16 = jnp.bfloat16
HIGHEST = lax.Precision.HIGHEST
LOG2E = 1.4426950408889634


def _cparams(n_axes):
    return pltpu.CompilerParams(dimension_semantics=("arbitrary",) * n_axes,
                                vmem_limit_bytes=VMEM_LIMIT)


def _dot(a, b):
    return jnp.dot(a, b, preferred_element_type=F32)


def _dot_nt(a, b):
    return lax.dot_general(a, b, (((1,), (1,)), ((), ())), preferred_element_type=F32)


def _dot_tn(a, b):
    return lax.dot_general(a, b, (((0,), (0,)), ((), ())), preferred_element_type=F32)


def _sigmoid(x):
    return 1.0 / (1.0 + jnp.exp(-x))


def _silu(x):
    return x * _sigmoid(x)


def _norm_mod(x, nw, sh, sc):
    ms = jnp.mean(x * x, axis=-1, keepdims=True)
    y = x * lax.rsqrt(ms + EPS) * nw
    return y * (1.0 + sc) + sh


def _adaln_kernel(ct_ref, w_ref, b_ref, o_ref):
    w = w_ref[...]
    for r in range(ct_ref.shape[1]):
        a_col = _silu(ct_ref[:, r:r + 1])
        o_ref[r:r + 1, :] = jnp.sum(a_col * w, axis=0, keepdims=True) + b_ref[...]


def _adaln(c_t, w_ada, b_ada):
    d, rows = c_t.shape
    n = w_ada.shape[1]
    tn = n // N_MOD
    return pl.pallas_call(
        _adaln_kernel,
        grid=(N_MOD,),
        in_specs=[pl.BlockSpec((d, rows), lambda j: (0, 0)),
                  pl.BlockSpec((d, tn), lambda j: (0, j)),
                  pl.BlockSpec((1, tn), lambda j: (0, j))],
        out_specs=pl.BlockSpec((rows, tn), lambda j: (0, j)),
        out_shape=jax.ShapeDtypeStruct((rows, n), F32),
        compiler_params=_cparams(1),
        name="adaln",
    )(c_t, w_ada, b_ada)


def _ffn_kernel(*refs, mod_rows, with_mix, mix_gate_row, with_final):
    it = iter(refs)
    x_ref = next(it)
    mod_ref = next(it)
    nw_ref = next(it)
    win_ref = next(it)
    wo_ref = next(it)
    if with_mix:
        ma_ref = next(it)
        mb_ref = next(it)
        wmix_ref = next(it)
    if with_final:
        fnw_ref = next(it)
    o_ref = next(it)

    sh_row, sc_row, g_row = mod_rows
    x = x_ref[...]
    if with_mix:
        mix_t = jnp.concatenate([ma_ref[0].astype(BF16), mb_ref[0].astype(BF16)], axis=0)
        x = x + mod_ref[0, mix_gate_row:mix_gate_row + 1, :] * _dot_tn(mix_t, wmix_ref[...])
    h = _norm_mod(x, nw_ref[...], mod_ref[0, sh_row:sh_row + 1, :], mod_ref[0, sc_row:sc_row + 1, :])
    hb = h.astype(BF16)
    d_ff = wo_ref.shape[0]
    n_chunks = d_ff // FF_CHUNK
    acc = None
    for j in range(n_chunks):
        c0 = j * FF_CHUNK
        g = _dot(hb, win_ref[:, c0:c0 + FF_CHUNK].astype(BF16))
        u = _dot(hb, win_ref[:, d_ff + c0:d_ff + c0 + FF_CHUNK].astype(BF16))
        act = (_silu(g) * u).astype(BF16)
        part = _dot(act, wo_ref[c0:c0 + FF_CHUNK, :].astype(BF16))
        acc = part if acc is None else acc + part
    y = x + 0.5 * mod_ref[0, g_row:g_row + 1, :] * acc
    if with_final:
        ms = jnp.mean(y * y, axis=-1, keepdims=True)
        y = y * lax.rsqrt(ms + EPS) * fnw_ref[...]
    o_ref[...] = y


def _ffn(x2d, mod, nw, w_in, w_out, mod_rows, tiles_per_batch, mix=None, mix_gate_row=None, final_nw=None):
    t, d = x2d.shape
    tm = TOKEN_TILE
    d_ff = w_out.shape[0]
    assert d_ff % FF_CHUNK == 0
    row = lambda i: (i, 0)
    const = lambda i: (0, 0)
    in_specs = [pl.BlockSpec((tm, d), row),
                pl.BlockSpec((1, N_MOD, d), lambda i: (i // tiles_per_batch, 0, 0)),
                pl.BlockSpec((1, d), const),
                pl.BlockSpec(w_in.shape, const, pipeline_mode=pl.Buffered(1)),
                pl.BlockSpec(w_out.shape, const, pipeline_mode=pl.Buffered(1))]
    args = [x2d, mod, nw, w_in, w_out]
    if mix is not None:
        ma, mb, wmix = mix
        trow = lambda i: (i // tiles_per_batch, 0, i % tiles_per_batch)
        in_specs += [pl.BlockSpec((1, ma.shape[1], tm), trow),
                     pl.BlockSpec((1, mb.shape[1], tm), trow),
                     pl.BlockSpec(wmix.shape, const)]
        args += [ma, mb, wmix]
    if final_nw is not None:
        in_specs.append(pl.BlockSpec((1, d), const))
        args.append(final_nw)
    kern = functools.partial(_ffn_kernel, mod_rows=mod_rows, with_mix=mix is not None,
                             mix_gate_row=mix_gate_row, with_final=final_nw is not None)
    return pl.pallas_call(
        kern,
        grid=(t // tm,),
        in_specs=in_specs,
        out_specs=pl.BlockSpec((tm, d), row),
        out_shape=jax.ShapeDtypeStruct((t, d), F32),
        compiler_params=_cparams(1),
        name="ffn_mix" if mix is not None else "ffn",
    )(*args)


QPAD_W = NSA_Q_HEADS * LANES
MISC_I = N_GATES
MISC_F = N_GATES + MLSTM_HEADS
HALF_BLOCK_W = CMP_STRIDE * KV_WIDTH
B_KV = NSA_WIDTH
B_ML = B_KV + 6 * KV_WIDTH
B_MISC = B_ML + 4 * MLSTM_WIDTH


def _vt_with_ones(v):
    vt = v.T
    n = vt.shape[1]
    pad = jnp.where(lax.broadcasted_iota(jnp.int32, (VT_ROWS - NSA_HEAD_DIM, n), 0) == 0, 1.0, 0.0)
    parts = []
    for h in range(NSA_KV_HEADS):
        parts += [vt[h * NSA_HEAD_DIM:(h + 1) * NSA_HEAD_DIM, :], pad]
    return jnp.concatenate(parts, axis=0).astype(BF16)


def _inproj_kernel(x_ref, mod_ref, nw_ref, w_ref, wml_ref, wmisc_ref, b_ref,
                   qpad_ref, kc_ref, vc_ref, ksel_ref, kwin_ref, vtsel_ref, vtwin_ref,
                   ml_ref, vtm_ref, ogt_ref, misc_ref, misct_ref, rows_ref):
    tm = x_ref.shape[0]
    h = _norm_mod(x_ref[...], nw_ref[...], mod_ref[0, 3:4, :], mod_ref[0, 4:5, :])
    hb = h.astype(BF16)

    def proj(w_ref, c0, width, b0):
        return _dot(hb, w_ref[:, c0:c0 + width]) + b_ref[:, b0 + c0:b0 + c0 + width]

    def half_blocks(val, out_ref):
        rows_ref[...] = val
        pieces = [rows_ref[pl.ds(l, tm // CMP_STRIDE, stride=CMP_STRIDE), :] for l in range(CMP_STRIDE)]
        out_ref[0] = jnp.concatenate(pieces, axis=1).astype(BF16)

    scale = NSA_HEAD_DIM ** -0.5 * LOG2E
    qt = (proj(w_ref, 0, NSA_WIDTH, 0) * scale).T.astype(BF16)
    zero_rows = jnp.zeros((NSA_HEAD_DIM, tm), BF16)
    for i in range(NSA_Q_HEADS):
        slots = [zero_rows] * NSA_KV_HEADS
        slots[i // NSA_GROUP] = qt[i * NSA_HEAD_DIM:(i + 1) * NSA_HEAD_DIM, :]
        qpad_ref[0, i * LANES:(i + 1) * LANES, :] = jnp.concatenate(slots, axis=0)
    kv_c = proj(w_ref, B_KV, 2 * KV_WIDTH, 0)
    half_blocks(kv_c[:, :KV_WIDTH], kc_ref)
    half_blocks(kv_c[:, KV_WIDTH:], vc_ref)
    kv_s = proj(w_ref, B_KV + 2 * KV_WIDTH, 2 * KV_WIDTH, 0)
    ksel_ref[...] = kv_s[:, :KV_WIDTH].astype(BF16)
    vtsel_ref[0] = _vt_with_ones(kv_s[:, KV_WIDTH:])
    kv_w = proj(w_ref, B_KV + 4 * KV_WIDTH, 2 * KV_WIDTH, 0)
    kwin_ref[...] = kv_w[:, :KV_WIDTH].astype(BF16)
    vtwin_ref[0] = _vt_with_ones(kv_w[:, KV_WIDTH:])
    for i in range(2):
        ml_ref[:, i * MLSTM_WIDTH:(i + 1) * MLSTM_WIDTH] = proj(wml_ref, i * MLSTM_WIDTH, MLSTM_WIDTH, B_ML)
    vtm_ref[0] = proj(wml_ref, 2 * MLSTM_WIDTH, MLSTM_WIDTH, B_ML).T.astype(BF16)
    ogt_ref[0] = proj(wml_ref, 3 * MLSTM_WIDTH, MLSTM_WIDTH, B_ML).T
    misc = proj(wmisc_ref, 0, LANES, B_MISC)
    misc_ref[...] = misc
    misct_ref[0] = misc.T


def _inproj(x2d, mod, nw, weights, b_pack, batch, seq):
    t, d = x2d.shape
    tm = WIDE_TOKEN_TILE
    tpb = seq // tm
    row = lambda i: (i, 0)
    const = lambda i: (0, 0)
    trow = lambda i: (i // tpb, 0, i % tpb)
    hrow = lambda i: (i // tpb, i % tpb, 0)
    vt_rows = NSA_KV_HEADS * VT_ROWS
    n_half = seq // CMP_STRIDE
    out_shapes = [
        jax.ShapeDtypeStruct((batch, QPAD_W, seq), BF16),
        jax.ShapeDtypeStruct((batch, n_half, HALF_BLOCK_W), BF16),
        jax.ShapeDtypeStruct((batch, n_half, HALF_BLOCK_W), BF16),
        jax.ShapeDtypeStruct((t, KV_WIDTH), BF16),
        jax.ShapeDtypeStruct((t, KV_WIDTH), BF16),
        jax.ShapeDtypeStruct((batch, vt_rows, seq), BF16),
        jax.ShapeDtypeStruct((batch, vt_rows, seq), BF16),
        jax.ShapeDtypeStruct((t, 2 * MLSTM_WIDTH), F32),
        jax.ShapeDtypeStruct((batch, MLSTM_WIDTH, seq), BF16),
        jax.ShapeDtypeStruct((batch, MLSTM_WIDTH, seq), F32),
        jax.ShapeDtypeStruct((t, LANES), F32),
        jax.ShapeDtypeStruct((batch, LANES, seq), F32),
    ]
    out_specs = [
        pl.BlockSpec((1, QPAD_W, tm), trow),
        pl.BlockSpec((1, tm // CMP_STRIDE, HALF_BLOCK_W), hrow),
        pl.BlockSpec((1, tm // CMP_STRIDE, HALF_BLOCK_W), hrow),
        pl.BlockSpec((tm, KV_WIDTH), row),
        pl.BlockSpec((tm, KV_WIDTH), row),
        pl.BlockSpec((1, vt_rows, tm), trow),
        pl.BlockSpec((1, vt_rows, tm), trow),
        pl.BlockSpec((tm, 2 * MLSTM_WIDTH), row),
        pl.BlockSpec((1, MLSTM_WIDTH, tm), trow),
        pl.BlockSpec((1, MLSTM_WIDTH, tm), trow),
        pl.BlockSpec((tm, LANES), row),
        pl.BlockSpec((1, LANES, tm), trow),
    ]
    return pl.pallas_call(
        _inproj_kernel,
        grid=(t // tm,),
        in_specs=[pl.BlockSpec((tm, d), row),
                  pl.BlockSpec((1, N_MOD, d), lambda i: (i // tpb, 0, 0)),
                  pl.BlockSpec((1, d), const)]
                 + [pl.BlockSpec(w.shape, const) for w in weights]
                 + [pl.BlockSpec(b_pack.shape, const)],
        out_specs=out_specs,
        out_shape=out_shapes,
        scratch_shapes=[pltpu.VMEM((tm, KV_WIDTH), F32)],
        compiler_params=_cparams(1),
        name="inproj",
    )(x2d, mod, nw, *weights, b_pack)


def _split_inproj(w_in, b_in):
    d = w_in.shape[0]
    offs = np.cumsum([0, NSA_WIDTH] + [KV_WIDTH] * 6 + [N_GATES] + [MLSTM_WIDTH] * 4 + [MLSTM_HEADS] * 2)
    g0, m0, i0 = int(offs[7]), int(offs[8]), int(offs[12])
    pad = LANES - N_GATES - 2 * MLSTM_HEADS
    misc_w = jnp.concatenate([w_in[:, g0:g0 + N_GATES], w_in[:, i0:i0 + 2 * MLSTM_HEADS],
                              jnp.zeros((d, pad), w_in.dtype)], axis=1)
    misc_b = jnp.concatenate([b_in[g0:g0 + N_GATES], b_in[i0:i0 + 2 * MLSTM_HEADS], jnp.zeros((pad,), b_in.dtype)])
    w_bf = w_in.astype(BF16)
    weights = [w_bf, w_bf[:, m0:i0], misc_w.astype(BF16)]
    b_pack = jnp.concatenate([b_in[:g0], b_in[m0:i0], misc_b]).reshape(1, -1).astype(F32)
    assert b_pack.shape[1] == B_MISC + LANES
    return weights, b_pack


def _compress_kernel(xk_ref, xv_ref, wa_ref, wb_ref, pe_ref, w1_ref, w2_ref, kc_ref, vtc_ref):
    n_half = xk_ref.shape[1]

    def one(x_ref, s):
        x = x_ref[0]
        a = _dot(x, wa_ref[s])
        b = _dot(x, wb_ref[s])
        pe_term = jnp.dot(pe_ref[s], w1_ref[s], preferred_element_type=F32, precision=HIGHEST)[0:1, :]
        pe2 = jnp.concatenate([pe_term] * NSA_KV_HEADS, axis=1)
        pre = a + pltpu.roll(b, n_half - 1, 0) + pe2
        hid = 0.5 * pre * (1.0 + jnp.tanh(0.7978845608028654 * (pre + 0.044715 * pre * pre * pre)))
        return _dot(hid.astype(BF16), w2_ref[s])

    kc_ref[0] = one(xk_ref, 0).astype(BF16)
    vtc_ref[0] = _vt_with_ones(one(xv_ref, 1))


def _compress(xk, xv, wa, wb, pe8, w1, w2e):
    batch, n_half, width = xk.shape
    bsel = lambda b: (b, 0, 0)
    c3 = lambda b: (0, 0, 0)
    vt_rows = NSA_KV_HEADS * VT_ROWS
    return pl.pallas_call(
        _compress_kernel,
        grid=(batch,),
        in_specs=[pl.BlockSpec((1, n_half, width), bsel), pl.BlockSpec((1, n_half, width), bsel),
                  pl.BlockSpec(wa.shape, c3), pl.BlockSpec(wb.shape, c3),
                  pl.BlockSpec(pe8.shape, c3), pl.BlockSpec(w1.shape, c3), pl.BlockSpec(w2e.shape, c3)],
        out_specs=[pl.BlockSpec((1, n_half, KV_WIDTH), bsel), pl.BlockSpec((1, vt_rows, n_half), bsel)],
        out_shape=[jax.ShapeDtypeStruct((batch, n_half, KV_WIDTH), BF16),
                   jax.ShapeDtypeStruct((batch, vt_rows, n_half), BF16)],
        compiler_params=_cparams(1),
        name="compress",
    )(xk, xv, wa, wb, pe8, w1, w2e)


def _pack_compress(pe, w1, w2):
    half = CMP_BLOCK // 2
    w1r = w1.reshape(CMP_BLOCK, NSA_HEAD_DIM, CMP_HIDDEN)
    def block_diag(w, axis):
        z = jnp.zeros_like(w)
        rows = [jnp.concatenate([w if hh == h2 else z for h2 in range(NSA_KV_HEADS)], axis=-1)
                for hh in range(NSA_KV_HEADS)]
        return jnp.concatenate(rows, axis=axis)

    def expand(w_half):
        return block_diag(w_half, 1).reshape(half * KV_WIDTH, NSA_KV_HEADS * CMP_HIDDEN)

    wa = expand(w1r[:half]).astype(BF16)
    wb = expand(w1r[half:]).astype(BF16)
    w2e = block_diag(w2, 0).astype(BF16)
    pe8 = jnp.concatenate([pe.reshape(1, CMP_BLOCK * NSA_HEAD_DIM),
                           jnp.zeros((7, CMP_BLOCK * NSA_HEAD_DIM), pe.dtype)], axis=0)
    return wa, wb, pe8, w1, w2e


def _q_cols(q_ref, h, cols):
    return jnp.concatenate(
        [q_ref[0, (NSA_GROUP * h + g) * LANES:(NSA_GROUP * h + g + 1) * LANES, cols] for g in range(NSA_GROUP)],
        axis=1)


def _q_aug(q_ref, qs_ref, h, cols=slice(0, QT)):
    return jnp.concatenate([_q_cols(q_ref, h, cols), qs_ref[h]], axis=0)


def _slope2(head):
    return float(2.0 ** (-8.0 * (head + 1) / NSA_Q_HEADS)) * LOG2E


def _bf16_split3(x):
    parts = []
    for _ in range(3):
        p = float(np.asarray(x, np.float32).astype(BF16).astype(np.float32))
        parts.append(p)
        x = x - p
    return parts


def _init_alibi_operands(qs_ref, kpos_ref, key_stride):
    lane = lax.broadcasted_iota(jnp.int32, (QT, LANES), 1)
    row = lax.broadcasted_iota(jnp.int32, (QT, LANES), 0)
    kpos_ref[...] = jnp.where(lane < 3, row.astype(F32), 0.0).astype(BF16)
    srow = lax.broadcasted_iota(jnp.int32, (LANES, QT), 0)
    for h in range(NSA_KV_HEADS):
        for g in range(NSA_GROUP):
            hi, mid, lo = _bf16_split3(key_stride * _slope2(NSA_GROUP * h + g))
            blk = jnp.where(srow == 0, hi, jnp.where(srow == 1, mid, jnp.where(srow == 2, lo, 0.0)))
            qs_ref[h, :, g * QT:(g + 1) * QT] = blk.astype(BF16)


def _gate_row(gt_ref, head, branch, cols):
    r = head * 3 + branch
    return _sigmoid(gt_ref[0, r:r + 1, cols])


def _softmax_jobs(jobs):
    scores = [_dot(jnp.concatenate(job[1], axis=0), job[0]) for job in jobs]
    staged = []
    for (q_aug, k_tiles, masks, offs, value_groups, state), st_all in zip(jobs, scores):
        n_t = len(k_tiles)
        per_g = []
        for g in range(NSA_GROUP):
            sl = slice(g * QT, (g + 1) * QT)
            ss = []
            for u in range(n_t):
                s = st_all[u * QT:(u + 1) * QT, sl]
                ss.append(s if masks[u] is None else jnp.where(masks[u], s, NEG))
            mx = None
            for u in range(n_t):
                cand = jnp.max(ss[u], axis=0, keepdims=True) + offs[u][g]
                mx = cand if mx is None else jnp.maximum(mx, cand)
            alpha = None
            if state is None:
                m_new = mx
            else:
                m_ref, _, h = state
                m_old = m_ref[h, :, sl]
                m_new = jnp.maximum(m_old, mx)
                alpha = jnp.exp2(m_old - m_new)
                m_ref[h, :, sl] = m_new
            m_use = jnp.where(m_new < 0.5 * NEG, 0.0, m_new)
            per_g.append((ss, m_use, alpha))
        staged.append(per_g)
    ets = []
    for (q_aug, k_tiles, masks, offs, value_groups, state), per_g in zip(jobs, staged):
        n_t = len(k_tiles)
        rows = [jnp.concatenate([jnp.exp2(per_g[g][0][u] - (per_g[g][1] - offs[u][g])).astype(BF16)
                                 for g in range(NSA_GROUP)], axis=1) for u in range(n_t)]
        ets.append(jnp.concatenate(rows, axis=0))
    results = []
    for (q_aug, k_tiles, masks, offs, value_groups, state), per_g, et in zip(jobs, staged, ets):
        mats = [jnp.concatenate(grp, axis=1) for grp in value_groups]
        pv_all = _dot(jnp.concatenate(mats, axis=0), et)
        pvs, r0 = [], 0
        for mat in mats:
            pvs.append(pv_all[r0:r0 + mat.shape[0], :])
            r0 += mat.shape[0]
        if state is None:
            results.append(pvs)
        else:
            _, acc_ref, h = state
            acc_ref[h] = jnp.concatenate([pg[2] for pg in per_g], axis=1) * acc_ref[h] + pvs[0]
            results.append(None)
    return results


def _gated(gt_ref, h, branch, pv, cols=slice(0, QT)):
    inv = 1.0 / jnp.maximum(pv[NSA_HEAD_DIM:NSA_HEAD_DIM + 1, :], 1e-30)
    out = []
    for g in range(NSA_GROUP):
        sl = slice(g * QT, (g + 1) * QT)
        out.append(pv[0:NSA_HEAD_DIM, sl] * (inv[:, sl] * _gate_row(gt_ref, NSA_GROUP * h + g, branch, cols)))
    return out


def _nsa_cmp_kernel(q_ref, kc_ref, vtc_ref, gt_ref, oc_ref, sel_ref, cnt_ref,
                    qs_ref, kpos_ref, ovt_ref, pvo_ref, pvi_ref):
    ncp = kc_ref.shape[1]
    n_ct = ncp // QT
    n_sel = sel_ref.shape[3]
    tiles = range(CMP_TILES)
    starts = [(pl.program_id(1) * CMP_TILES + s) * QT for s in tiles]
    cols = [slice(s * QT, (s + 1) * QT) for s in tiles]
    units = [(s, h) for s in tiles for h in range(NSA_KV_HEADS)]
    tile_span = QT * CMP_STRIDE

    @pl.when((pl.program_id(0) == 0) & (pl.program_id(1) == 0))
    def _():
        _init_alibi_operands(qs_ref, kpos_ref, float(CMP_STRIDE))
        j_o = lax.broadcasted_iota(jnp.int32, (n_sel, ncp), 0)
        c_o = lax.broadcasted_iota(jnp.int32, (n_sel, ncp), 1)
        ov = ((c_o * CMP_STRIDE <= j_o * SEL_BLOCK + (SEL_BLOCK - 1))
              & (c_o * CMP_STRIDE + (CMP_BLOCK - 1) >= j_o * SEL_BLOCK))
        ovt_ref[...] = jnp.where(ov, 1.0, 0.0).astype(BF16)

    a0 = (lax.broadcasted_iota(jnp.int32, (QT, QT), 1)
          - CMP_STRIDE * lax.broadcasted_iota(jnp.int32, (QT, QT), 0))
    q_augs = [_q_aug(q_ref, qs_ref, h, cols[s]) for s, h in units]
    n_vis = jnp.minimum((starts[-1] + QT - CMP_BLOCK) // tile_span + 1, n_ct)

    for k in range(1, n_ct + 1):
        @pl.when(n_vis == k)
        def _(k=k):
            jobs = []
            for i, (s, h) in enumerate(units):
                k_tiles, masks, offs, vts, ovs = [], [], [], [], []
                for ci in range(k):
                    cs = slice(ci * QT, (ci + 1) * QT)
                    first_end = ci * tile_span + CMP_BLOCK - 1
                    k_tiles.append(jnp.concatenate([kc_ref[0, cs, :], kpos_ref[...]], axis=1))
                    masks.append(a0 + (starts[s] - first_end) >= 0)
                    offs.append([_slope2(NSA_GROUP * h + g) * first_end for g in range(NSA_GROUP)])
                    vts.append(vtc_ref[0, h * VT_ROWS:(h + 1) * VT_ROWS, cs])
                    ovs.append(ovt_ref[:, cs])
                jobs.append((q_augs[i], k_tiles, masks, offs, [vts, ovs], None))
            for i, (pvo, pvi) in enumerate(_softmax_jobs(jobs)):
                pvo_ref[i] = pvo
                pvi_ref[i] = pvi

    j_io = lax.broadcasted_iota(jnp.int32, (n_sel, QT), 0)
    q_io = lax.broadcasted_iota(jnp.int32, (n_sel, QT), 1)
    ones8 = jnp.ones((8, QT), BF16)

    taken = -(2.0 ** 126)
    scores = []
    for i, (s, h) in enumerate(units):
        t_j = starts[s] + q_io
        cur = t_j // SEL_BLOCK
        valid = j_io * SEL_BLOCK <= t_j
        forced = (j_io == 0) | (j_io == cur) | (j_io == cur - 1)
        pvo = pvo_ref[i]
        o_c = _gated(gt_ref, h, 0, pvo, cols[s])
        inv = 1.0 / jnp.maximum(pvo[NSA_HEAD_DIM:NSA_HEAD_DIM + 1, :], 1e-30)
        imp = None
        for g in range(NSA_GROUP):
            sl = slice(g * QT, (g + 1) * QT)
            oc_ref[0, NSA_GROUP * h + g, :, cols[s]] = o_c[g]
            term = pvi_ref[i, :, sl] * inv[:, sl]
            imp = term if imp is None else imp + term
        scores.append(jnp.where(forced, taken, jnp.where(valid, imp, -FORCE_BONUS)))

    n_cls = n_sel // TOPK_ROWS
    cls = jnp.minimum((starts[-1] + QT + SEL_BLOCK * TOPK_ROWS - 1) // (SEL_BLOCK * TOPK_ROWS), n_cls)

    for k in range(1, n_cls + 1):
        @pl.when(cls == k)
        def _(rows=k * TOPK_ROWS):
            j_sub = j_io[:rows]

            def pick(_, carry):
                out = []
                for sc in carry:
                    mx = jnp.max(sc, axis=0, keepdims=True)
                    idx = jnp.min(jnp.where(sc == mx, j_sub, n_sel), axis=0, keepdims=True)
                    out.append(jnp.where(j_sub == idx, taken, sc))
                return tuple(out)

            picked = lax.fori_loop(0, SEL_TOPK - N_FORCED, pick, tuple(sc[:rows] for sc in scores))
            for i, (s, h) in enumerate(units):
                sel = jnp.where(picked[i] <= 0.5 * taken, 1.0, 0.0)
                if rows < n_sel:
                    sel = jnp.concatenate([sel, jnp.zeros((n_sel - rows, QT), F32)], axis=0)
                sel_ref[0, h, s] = sel
                cnt = _dot_nt(ones8, sel.astype(BF16))
                cnt_ref[0, s, h] = cnt[0:1, :]


def _nsa_cmp(qpad3, kc, vtc, gates_t, batch, seq):
    nb = seq // QT
    n_sel = seq // SEL_BLOCK
    ncp = kc.shape[1]
    assert ncp % QT == 0 and nb % CMP_TILES == 0
    vt_rows = NSA_KV_HEADS * VT_ROWS
    qw = QT * CMP_TILES
    n_units = CMP_TILES * NSA_KV_HEADS
    return pl.pallas_call(
        _nsa_cmp_kernel,
        grid=(batch, nb // CMP_TILES),
        in_specs=[pl.BlockSpec((1, QPAD_W, qw), lambda b, i: (b, 0, i)),
                  pl.BlockSpec((1, ncp, KV_WIDTH), lambda b, i: (b, 0, 0)),
                  pl.BlockSpec((1, vt_rows, ncp), lambda b, i: (b, 0, 0)),
                  pl.BlockSpec((1, N_GATES, qw), lambda b, i: (b, 0, i))],
        out_specs=[pl.BlockSpec((1, NSA_Q_HEADS, NSA_HEAD_DIM, qw), lambda b, i: (b, 0, 0, i)),
                   pl.BlockSpec((1, NSA_KV_HEADS, CMP_TILES, n_sel, QT), lambda b, i: (b, 0, i, 0, 0)),
                   pl.BlockSpec((1, CMP_TILES, NSA_KV_HEADS, 1, n_sel), lambda b, i: (b, i, 0, 0, 0))],
        out_shape=[jax.ShapeDtypeStruct((batch, NSA_Q_HEADS, NSA_HEAD_DIM, seq), F32),
                   jax.ShapeDtypeStruct((batch, NSA_KV_HEADS, nb, n_sel, QT), F32),
                   jax.ShapeDtypeStruct((batch, nb, NSA_KV_HEADS, 1, n_sel), F32)],
        scratch_shapes=[pltpu.VMEM((NSA_KV_HEADS, LANES, NSA_GROUP * QT), BF16),
                        pltpu.VMEM((QT, LANES), BF16),
                        pltpu.VMEM((n_sel, ncp), BF16),
                        pltpu.VMEM((n_units, VT_ROWS, NSA_GROUP * QT), F32),
                        pltpu.VMEM((n_units, n_sel, NSA_GROUP * QT), F32)],
        compiler_params=_cparams(2),
        name="nsa_cmp",
    )(qpad3, kc, vtc, gates_t)


def _nsa_sel_kernel(count_ref, list_ref, q_ref, ksel_ref, vtsel_ref, kwin_ref, vtwin_ref, selm_ref, gt_ref,
                    oc_ref, o_ref, m_sel, acc_sel, qs_ref, kpos_ref, part_ref, kposm_ref):
    b = pl.program_id(0)
    nb = pl.num_programs(1) * SEL_TILES
    tiles = range(SEL_TILES)
    qbs = [pl.program_id(1) * SEL_TILES + s for s in tiles]
    cols = [slice(s * QT, (s + 1) * QT) for s in tiles]
    units = [(s, h) for s in tiles for h in range(NSA_KV_HEADS)]
    k_io = lax.broadcasted_iota(jnp.int32, (QT, QT), 0)
    q_io = lax.broadcasted_iota(jnp.int32, (QT, QT), 1)
    causal = q_io >= k_io
    in_window = q_io < k_io

    pen_row0 = 3
    pen_rows = 16
    assert pen_row0 + 2 * SEL_UNROLL <= pen_rows

    @pl.when((b == 0) & (pl.program_id(1) == 0))
    def _():
        _init_alibi_operands(qs_ref, kpos_ref, 1.0)
        lane = lax.broadcasted_iota(jnp.int32, (QT, LANES), 1)
        row = lax.broadcasted_iota(jnp.int32, (QT, LANES), 0)
        base = jnp.where(lane < pen_row0, row.astype(F32), 0.0)
        for u in range(SEL_UNROLL):
            ind = jnp.where(lane == pen_row0 + 2 * u, jnp.where(row < SEL_BLOCK, 1.0, 0.0),
                            jnp.where(lane == pen_row0 + 2 * u + 1, jnp.where(row >= SEL_BLOCK, 1.0, 0.0), 0.0))
            kposm_ref[u] = (base + ind).astype(BF16)

    def k_aug(k_ref, t, u=None):
        off = pl.multiple_of(t * QT, QT)
        tail = kpos_ref[...] if u is None else kposm_ref[u]
        return jnp.concatenate([k_ref[0, pl.ds(off, QT), :], tail], axis=1)

    def v_ext(vt_ref, h, t):
        return vt_ref[0, h * VT_ROWS:(h + 1) * VT_ROWS, pl.ds(pl.multiple_of(t * QT, QT), QT)]

    def tile_offs(h, t, live=None):
        pos = (t * QT).astype(F32)
        out = [_slope2(NSA_GROUP * h + g) * pos for g in range(NSA_GROUP)]
        return out if live is None else [jnp.where(live, o, NEG) for o in out]

    def penalties(s, h, t):
        mrow = selm_ref[0, h, s, pl.ds(2 * t, 2), :]
        return (mrow - 1.0) * (-NEG)

    q_tops = [_q_cols(q_ref, h, cols[s]) for s, h in units]
    q_augs = [jnp.concatenate([q_tops[i], qs_ref[h]], axis=0) for i, (s, h) in enumerate(units)]
    zero_rows = jnp.zeros((LANES - pen_rows, NSA_GROUP * QT), BF16)

    def q_aug_sel(i, h, pens):
        blk = jnp.concatenate([jnp.zeros((pen_row0, QT), F32)] + pens
                              + [jnp.zeros((pen_rows - pen_row0 - 2 * len(pens), QT), F32)], axis=0)
        ext = qs_ref[h, 0:pen_rows, :].astype(F32) + jnp.concatenate([blk] * NSA_GROUP, axis=1)
        return jnp.concatenate([q_tops[i], ext.astype(BF16), zero_rows], axis=0)

    slot = [(b * nb + qbs[s]) * NSA_KV_HEADS + h for s, h in units]
    n_tiles = [count_ref[sl] for sl in slot]

    m_sel[...] = jnp.full(m_sel.shape, NEG, F32)
    acc_sel[...] = jnp.zeros(acc_sel.shape, F32)

    def sel_jobs(trip, first, step):
        jobs = []
        for i, (s, h) in enumerate(units):
            k_tiles, masks, offs, vts, pens = [], [], [], [], []
            for u in range(SEL_UNROLL):
                if first and step == 0 and u == 0:
                    t, mask, off = qbs[s], causal, tile_offs(h, qbs[s])
                else:
                    idx = (trip * SEL_CHAIN + step) * SEL_UNROLL + u - 1
                    live = idx < n_tiles[i]
                    t = jnp.where(live, list_ref[slot[i] * nb + jnp.minimum(idx, nb - 1)], 0)
                    mask, off = None, tile_offs(h, t, live)
                k_tiles.append(k_aug(ksel_ref, t, u))
                masks.append(mask)
                offs.append(off)
                vts.append(v_ext(vtsel_ref, h, t))
                pens.append(penalties(s, h, t))
            jobs.append((q_aug_sel(i, h, pens), k_tiles, masks, offs, [vts], (m_sel, acc_sel, i)))
        return jobs

    n_win = WINDOW // QT + 1
    win_jobs = []
    for i, (s, h) in enumerate(units):
        k_tiles, masks, offs, vts = [], [], [], []
        for u in range(n_win):
            raw = qbs[s] - (n_win - 1) + u
            t = jnp.maximum(raw, 0)
            k_tiles.append(k_aug(kwin_ref, t))
            masks.append(in_window if u == 0 else causal if u == n_win - 1 else None)
            offs.append(tile_offs(h, t, raw >= 0))
            vts.append(v_ext(vtwin_ref, h, t))
        win_jobs.append((q_augs[i], k_tiles, masks, offs, [vts], None))
    jobs = sel_jobs(0, True, 0) + win_jobs
    for step in range(1, SEL_CHAIN):
        jobs += sel_jobs(0, True, step)
    results = _softmax_jobs(jobs)
    for i, (s, h) in enumerate(units):
        o_w = _gated(gt_ref, h, 2, results[len(units) + i][0], cols[s])
        for g in range(NSA_GROUP):
            head = NSA_GROUP * h + g
            part_ref[s, head] = oc_ref[0, head, :, cols[s]] + o_w[g]

    def sel_body(trip, carry):
        jobs = []
        for step in range(SEL_CHAIN):
            jobs += sel_jobs(trip, False, step)
        _softmax_jobs(jobs)
        return carry

    per_trip = SEL_UNROLL * SEL_CHAIN
    n_trips = (functools.reduce(jnp.maximum, n_tiles) + per_trip) // per_trip
    lax.fori_loop(1, n_trips, sel_body, 0)

    for i, (s, h) in enumerate(units):
        o_s = _gated(gt_ref, h, 1, acc_sel[i], cols[s])
        for g in range(NSA_GROUP):
            head = NSA_GROUP * h + g
            o_ref[0, head, :, cols[s]] = part_ref[s, head] + o_s[g]


def _tile_lists(cnt, batch, nb):
    pair = cnt.reshape(batch, nb, NSA_KV_HEADS, nb, 2).sum(axis=-1)
    p_io = lax.broadcasted_iota(jnp.int32, pair.shape, 3)
    flag = (pair > 0.5) & (p_io < lax.broadcasted_iota(jnp.int32, pair.shape, 1))
    rank = jnp.cumsum(flag.astype(jnp.int32), axis=-1) - 1
    slot_io = lax.broadcasted_iota(jnp.int32, pair.shape + (nb,), 4)
    hit = flag[..., None] & (rank[..., None] == slot_io)
    lists = jnp.sum(jnp.where(hit, p_io[..., None], 0), axis=3)
    counts = jnp.sum(flag.astype(jnp.int32), axis=-1)
    return counts.reshape(-1), lists.reshape(-1)


def _nsa_sel(counts, lists, qpad3, ksel, vtsel, kwin, vtwin, selm, gates_t, oc, batch, seq):
    nb = seq // QT
    assert nb % SEL_TILES == 0
    res = lambda b, i, c, l: (b, 0, 0)
    vt_rows = NSA_KV_HEADS * VT_ROWS
    qw = QT * SEL_TILES
    n_units = SEL_TILES * NSA_KV_HEADS
    grid_spec = pltpu.PrefetchScalarGridSpec(
        num_scalar_prefetch=2,
        grid=(batch, nb // SEL_TILES),
        in_specs=[pl.BlockSpec((1, QPAD_W, qw), lambda b, i, c, l: (b, 0, i)),
                  pl.BlockSpec((1, seq, KV_WIDTH), res),
                  pl.BlockSpec((1, vt_rows, seq), res),
                  pl.BlockSpec((1, seq, KV_WIDTH), res),
                  pl.BlockSpec((1, vt_rows, seq), res),
                  pl.BlockSpec((1, NSA_KV_HEADS, SEL_TILES, 2 * nb, QT), lambda b, i, c, l: (b, 0, i, 0, 0)),
                  pl.BlockSpec((1, N_GATES, qw), lambda b, i, c, l: (b, 0, i)),
                  pl.BlockSpec((1, NSA_Q_HEADS, NSA_HEAD_DIM, qw), lambda b, i, c, l: (b, 0, 0, i))],
        out_specs=pl.BlockSpec((1, NSA_Q_HEADS, NSA_HEAD_DIM, qw), lambda b, i, c, l: (b, 0, 0, i)),
        scratch_shapes=[pltpu.VMEM((n_units, 1, NSA_GROUP * QT), F32),
                        pltpu.VMEM((n_units, VT_ROWS, NSA_GROUP * QT), F32),
                        pltpu.VMEM((NSA_KV_HEADS, LANES, NSA_GROUP * QT), BF16),
                        pltpu.VMEM((QT, LANES), BF16),
                        pltpu.VMEM((SEL_TILES, NSA_Q_HEADS, NSA_HEAD_DIM, QT), F32),
                        pltpu.VMEM((SEL_UNROLL, QT, LANES), BF16)],
    )
    return pl.pallas_call(
        _nsa_sel_kernel,
        grid_spec=grid_spec,
        out_shape=jax.ShapeDtypeStruct((batch, NSA_Q_HEADS, NSA_HEAD_DIM, seq), F32),
        compiler_params=_cparams(2),
        name="nsa_sel",
    )(counts, lists, qpad3, ksel, vtsel, kwin, vtwin, selm, gates_t, oc)


def _mlstm_kernel(q_ref, k_ref, vt_ref, ogt_ref, misc_ref, misct_ref, cw_ref, cb_ref, nw_ref,
                  o_ref, xq_ref, xk_ref, c_ref, n_ref, m_ref):
    tm = q_ref.shape[0]
    L = MLSTM_L
    hd = MLSTM_HEAD_DIM
    halo = 8

    @pl.when(pl.program_id(1) == 0)
    def _():
        xq_ref[0:halo, :] = jnp.zeros((halo, MLSTM_WIDTH), F32)
        xk_ref[0:halo, :] = jnp.zeros((halo, MLSTM_WIDTH), F32)
        c_ref[...] = jnp.zeros_like(c_ref)
        n_ref[...] = jnp.zeros_like(n_ref)
        m_ref[...] = jnp.zeros_like(m_ref)

    def conv(x_ref, buf_ref, col0):
        buf_ref[halo:halo + tm, :] = x_ref[...]
        y = cb_ref[:, col0:col0 + MLSTM_WIDTH]
        for kk in range(CONV_WIDTH):
            r0 = halo - (CONV_WIDTH - 1) + kk
            y = y + cw_ref[kk:kk + 1, col0:col0 + MLSTM_WIDTH] * buf_ref[r0:r0 + tm, :]
        tail = buf_ref[tm:tm + halo, :]
        buf_ref[0:halo, :] = tail
        return _silu(y)

    qc = conv(q_ref, xq_ref, 0)
    kc = conv(k_ref, xk_ref, MLSTM_WIDTH) * (hd ** -0.5)

    misc = misc_ref[...]
    misct = misct_ref[0]
    lf_c = -jnp.log(1.0 + jnp.exp(-misc))
    lf_r = -jnp.log(1.0 + jnp.exp(-misct))
    row_io = lax.broadcasted_iota(jnp.int32, (L, L), 0)
    col_io = lax.broadcasted_iota(jnp.int32, (L, L), 1)
    tril = jnp.where(col_io <= row_io, 1.0, 0.0).astype(BF16)
    triu = jnp.where(row_io <= col_io, 1.0, 0.0).astype(BF16)

    def split3(x):
        hi = x.astype(BF16)
        r1 = x - hi.astype(F32)
        mid = r1.astype(BF16)
        return hi, mid, (r1 - mid.astype(F32)).astype(BF16)

    lf_c3 = split3(lf_c)
    lf_r3 = split3(lf_r)
    b_c = jnp.concatenate([sum(_dot(tril, p[ci * L:(ci + 1) * L, :]) for p in lf_c3)
                           for ci in range(tm // L)], axis=0)
    b_r = jnp.concatenate([sum(_dot(p[:, ci * L:(ci + 1) * L], triu) for p in lf_r3)
                           for ci in range(tm // L)], axis=1)

    tri_t = lax.broadcasted_iota(jnp.int32, (L, L), 0) <= lax.broadcasted_iota(jnp.int32, (L, L), 1)
    nw_cols = [jnp.broadcast_to(nw_ref[:, h * hd:(h + 1) * hd], (hd, hd)).T for h in range(MLSTM_HEADS)]

    def split3_rows(row):
        hi = row.astype(BF16)
        r1 = row - hi.astype(F32)
        mid = r1.astype(BF16)
        lo = (r1 - mid.astype(F32)).astype(BF16)
        sub = lax.broadcasted_iota(jnp.int32, (8, row.shape[1]), 0)
        out = jnp.where(sub == 0, hi.astype(F32), jnp.where(sub == 1, mid.astype(F32),
                                                            jnp.where(sub == 2, lo.astype(F32), 0.0)))
        return out.astype(BF16)

    def rows_sum3(x8):
        return x8[0:1, :] + x8[1:2, :] + x8[2:3, :]

    for ci in range(tm // L):
        r0 = ci * L
        for h in range(MLSTM_HEADS):
            cs = slice(h * hd, (h + 1) * hd)
            qb16 = qc[r0:r0 + L, cs].astype(BF16)
            kb16 = kc[r0:r0 + L, cs].astype(BF16)
            vt = vt_ref[0, cs, r0:r0 + L]
            a_col = (misc[r0:r0 + L, MISC_I + h:MISC_I + h + 1]
                     - b_c[r0:r0 + L, MISC_F + h:MISC_F + h + 1])
            brow = b_r[MISC_F + h:MISC_F + h + 1, r0:r0 + L]
            lirow = misct[MISC_I + h:MISC_I + h + 1, r0:r0 + L]
            m_prev = m_ref[h]
            ct = c_ref[h]
            nrow = n_ref[h]

            dmat = jnp.where(tri_t, brow + a_col, NEG)
            m_inter = brow + m_prev
            m_t = jnp.maximum(jnp.max(dmat, axis=0, keepdims=True), m_inter)
            w = jnp.exp(dmat - m_t) * _dot_nt(kb16, qb16)
            decay = jnp.exp(m_inter - m_t)
            num = _dot(vt, w.astype(BF16)) + decay * _dot_nt(ct.astype(BF16), qb16)
            nq = rows_sum3(_dot_nt(split3_rows(nrow), qb16))
            den = jnp.sum(w, axis=0, keepdims=True) + decay * nq
            hout = num * (1.0 / jnp.maximum(jnp.abs(den), jnp.exp(-m_t)))

            b_last = brow[:, L - 1:L]
            grow = b_last - brow + lirow
            m_new = jnp.maximum(b_last + m_prev, jnp.max(grow, axis=1, keepdims=True))
            wk = jnp.exp(grow - m_new)
            d_c = jnp.exp(b_last + m_prev - m_new)
            c_ref[h] = d_c * ct + _dot((vt.astype(F32) * wk).astype(BF16), kb16)
            n_ref[h] = d_c * nrow + rows_sum3(_dot(split3_rows(wk), kb16))
            m_ref[h] = m_new

            hn = hout * lax.rsqrt(jnp.mean(hout * hout, axis=0, keepdims=True) + EPS) * nw_cols[h]
            o_ref[0, cs, r0:r0 + L] = (hn * _sigmoid(ogt_ref[0, cs, r0:r0 + L])).astype(o_ref.dtype)


def _mlstm(ml, vt_m, og_t, misc, misc_t, conv_w, conv_b, norm_w, batch, seq):
    tm = WIDE_TOKEN_TILE
    tpb = seq // tm
    col = lambda j: (lambda b, i: (b * tpb + i, j))
    const = lambda b, i: (0, 0)
    trow = lambda b, i: (b, 0, i)
    return pl.pallas_call(
        _mlstm_kernel,
        grid=(batch, tpb),
        in_specs=[pl.BlockSpec((tm, MLSTM_WIDTH), col(0)),
                  pl.BlockSpec((tm, MLSTM_WIDTH), col(1)),
                  pl.BlockSpec((1, MLSTM_WIDTH, tm), trow),
                  pl.BlockSpec((1, MLSTM_WIDTH, tm), trow),
                  pl.BlockSpec((tm, LANES), col(0)),
                  pl.BlockSpec((1, LANES, tm), trow),
                  pl.BlockSpec(conv_w.shape, const),
                  pl.BlockSpec(conv_b.shape, const),
                  pl.BlockSpec(norm_w.shape, const)],
        out_specs=pl.BlockSpec((1, MLSTM_WIDTH, tm), trow),
        out_shape=jax.ShapeDtypeStruct((batch, MLSTM_WIDTH, seq), BF16),
        scratch_shapes=[pltpu.VMEM((tm + 8, MLSTM_WIDTH), F32),
                        pltpu.VMEM((tm + 8, MLSTM_WIDTH), F32),
                        pltpu.VMEM((MLSTM_HEADS, MLSTM_HEAD_DIM, MLSTM_HEAD_DIM), F32),
                        pltpu.VMEM((MLSTM_HEADS, 1, MLSTM_HEAD_DIM), F32),
                        pltpu.VMEM((MLSTM_HEADS, 1, 1), F32)],
        compiler_params=_cparams(2),
        name="mlstm",
    )(ml, ml, vt_m, og_t, misc, misc_t, conv_w, conv_b, norm_w)


def _layer(x2d, c, batch, seq, w_ada, b_ada, norm_ffn1_w, ffn1_w_in, ffn1_w_out, norm_mix_w, w_in, b_in,
           cmp_k_pe, cmp_k_w1, cmp_k_w2, cmp_v_pe, cmp_v_w1, cmp_v_w2, conv_w, conv_b, mlstm_norm_w, w_out,
           norm_ffn2_w, ffn2_w_in, ffn2_w_out, final_nw):
    d = x2d.shape[1]
    tpb = seq // TOKEN_TILE
    mod = _adaln(c.T, w_ada, b_ada.reshape(1, -1)).reshape(batch, N_MOD, d)

    x1 = _ffn(x2d, mod, norm_ffn1_w.reshape(1, d), ffn1_w_in, ffn1_w_out,
              (0, 1, 2), tpb)

    weights, b_pack = _split_inproj(w_in, b_in)
    (qpad, kc, vc, ksel, kwin, vtsel, vtwin, ml, vt_m, og_t, misc, misc_t) = _inproj(
        x1, mod, norm_mix_w.reshape(1, d), weights, b_pack, batch, seq)

    pk = _pack_compress(cmp_k_pe, cmp_k_w1, cmp_k_w2)
    pv = _pack_compress(cmp_v_pe, cmp_v_w1, cmp_v_w2)
    stacked = [jnp.stack([a, bb]) for a, bb in zip(pk, pv)]
    kcmp, vtcmp = _compress(kc, vc, *stacked)

    nb = seq // QT
    qpad3 = qpad
    gates_t = misc_t
    oc, sel, cnt = _nsa_cmp(qpad3, kcmp, vtcmp, gates_t, batch, seq)
    counts, lists = _tile_lists(cnt, batch, nb)
    o_t = _nsa_sel(counts, lists, qpad3, ksel.reshape(batch, seq, KV_WIDTH), vtsel, kwin.reshape(batch, seq, KV_WIDTH),
                   vtwin, sel, gates_t, oc, batch, seq)
    o_nsa = o_t.reshape(batch, NSA_WIDTH, seq)

    o_ml = _mlstm(ml, vt_m, og_t, misc, misc_t, conv_w, conv_b.reshape(1, -1), mlstm_norm_w.reshape(1, -1), batch, seq)

    return _ffn(x1, mod, norm_ffn2_w.reshape(1, d), ffn2_w_in, ffn2_w_out,
                (6, 7, 8), tpb, mix=(o_nsa, o_ml, w_out.astype(BF16)), mix_gate_row=5, final_nw=final_nw)


def kernel(x, c, w_ada, b_ada, norm_ffn1_w, ffn1_w_in, ffn1_w_out, norm_mix_w, w_in, b_in, cmp_k_pe, cmp_k_w1, cmp_k_w2, cmp_v_pe, cmp_v_w1, cmp_v_w2, conv_w, conv_b, mlstm_norm_w, w_out, norm_ffn2_w, ffn2_w_in, ffn2_w_out, final_norm_w):
    batch, seq, d = x.shape
    depth = w_ada.shape[0]
    assert depth == 1 and seq % (QT * CMP_STRIDE) == 0 and seq // SEL_BLOCK >= SEL_TOPK
    y = _layer(x.reshape(batch * seq, d), c, batch, seq, w_ada[0], b_ada[0], norm_ffn1_w[0], ffn1_w_in[0],
               ffn1_w_out[0], norm_mix_w[0], w_in[0], b_in[0], cmp_k_pe[0], cmp_k_w1[0], cmp_k_w2[0],
               cmp_v_pe[0], cmp_v_w1[0], cmp_v_w2[0], conv_w[0], conv_b[0], mlstm_norm_w[0], w_out[0],
               norm_ffn2_w[0], ffn2_w_in[0], ffn2_w_out[0], final_norm_w.reshape(1, d))
    return y.reshape(batch, seq, d)
```

```python
import functools

import numpy as np
import jax
import jax.numpy as jnp
from jax import lax
from jax.experimental import pallas as pl
from jax.experimental.pallas import tpu as pltpu

NSA_Q_HEADS = 8
NSA_KV_HEADS = 2
NSA_GROUP = NSA_Q_HEADS // NSA_KV_HEADS
NSA_HEAD_DIM = 64
CMP_BLOCK = 32
CMP_STRIDE = 16
CMP_HIDDEN = 128
SEL_BLOCK = 64
SEL_TOPK = 16
WINDOW = 512
FORCE_BONUS = 1.0e4
N_FORCED = 3
assert FORCE_BONUS > NSA_GROUP
MLSTM_HEADS = 4
MLSTM_HEAD_DIM = 128
CONV_WIDTH = 4
N_MOD = 9
EPS = 1e-6
NEG = -1e30

NSA_WIDTH = NSA_Q_HEADS * NSA_HEAD_DIM
KV_WIDTH = NSA_KV_HEADS * NSA_HEAD_DIM
MLSTM_WIDTH = MLSTM_HEADS * MLSTM_HEAD_DIM
N_GATES = 3 * NSA_Q_HEADS

LANES = 128
QT = 128
TOKEN_TILE = 512
WIDE_TOKEN_TILE = 1024
FF_CHUNK = 256
MLSTM_L = 128
SEL_UNROLL = 6
SEL_CHAIN = 2
CMP_TILES = 2
SEL_TILES = 2
TOPK_ROWS = 32
VT_ROWS = 80
VMEM_LIMIT = 56 * 1024 * 1024

F32 = jnp.float32
BF16 = jnp.bfloat16
HIGHEST = lax.Precision.HIGHEST
LOG2E = 1.4426950408889634


def _cparams(n_axes):
    return pltpu.CompilerParams(dimension_semantics=("arbitrary",) * n_axes,
                                vmem_limit_bytes=VMEM_LIMIT)


def _dot(a, b):
    return jnp.dot(a, b, preferred_element_type=F32)


def _dot_nt(a, b):
    return lax.dot_general(a, b, (((1,), (1,)), ((), ())), preferred_element_type=F32)


def _dot_tn(a, b):
    return lax.dot_general(a, b, (((0,), (0,)), ((), ())), preferred_element_type=F32)


def _sigmoid(x):
    return 1.0 / (1.0 + jnp.exp(-x))


def _silu(x):
    return x * _sigmoid(x)


def _norm_mod(x, nw, sh, sc):
    ms = jnp.mean(x * x, axis=-1, keepdims=True)
    y = x * lax.rsqrt(ms + EPS) * nw
    return y * (1.0 + sc) + sh


def _adaln_kernel(ct_ref, w_ref, b_ref, o_ref):
    w = w_ref[...]
    for r in range(ct_ref.shape[1]):
        a_col = _silu(ct_ref[:, r:r + 1])
        o_ref[r:r + 1, :] = jnp.sum(a_col * w, axis=0, keepdims=True) + b_ref[...]


def _adaln(c_t, w_ada, b_ada):
    d, rows = c_t.shape
    n = w_ada.shape[1]
    tn = n // N_MOD
    return pl.pallas_call(
        _adaln_kernel,
        grid=(N_MOD,),
        in_specs=[pl.BlockSpec((d, rows), lambda j: (0, 0)),
                  pl.BlockSpec((d, tn), lambda j: (0, j)),
                  pl.BlockSpec((1, tn), lambda j: (0, j))],
        out_specs=pl.BlockSpec((rows, tn), lambda j: (0, j)),
        out_shape=jax.ShapeDtypeStruct((rows, n), F32),
        compiler_params=_cparams(1),
        name="adaln",
    )(c_t, w_ada, b_ada)


def _ffn_kernel(*refs, mod_rows, with_mix, mix_gate_row, with_final):
    it = iter(refs)
    x_ref = next(it)
    mod_ref = next(it)
    nw_ref = next(it)
    win_ref = next(it)
    wo_ref = next(it)
    if with_mix:
        ma_ref = next(it)
        mb_ref = next(it)
        wmix_ref = next(it)
    if with_final:
        fnw_ref = next(it)
    o_ref = next(it)

    sh_row, sc_row, g_row = mod_rows
    x = x_ref[...]
    if with_mix:
        mix_t = jnp.concatenate([ma_ref[0].astype(BF16), mb_ref[0].astype(BF16)], axis=0)
        x = x + mod_ref[0, mix_gate_row:mix_gate_row + 1, :] * _dot_tn(mix_t, wmix_ref[...])
    h = _norm_mod(x, nw_ref[...], mod_ref[0, sh_row:sh_row + 1, :], mod_ref[0, sc_row:sc_row + 1, :])
    hb = h.astype(BF16)
    d_ff = wo_ref.shape[0]
    n_chunks = d_ff // FF_CHUNK
    acc = None
    for j in range(n_chunks):
        c0 = j * FF_CHUNK
        g = _dot(hb, win_ref[:, c0:c0 + FF_CHUNK].astype(BF16))
        u = _dot(hb, win_ref[:, d_ff + c0:d_ff + c0 + FF_CHUNK].astype(BF16))
        act = (_silu(g) * u).astype(BF16)
        part = _dot(act, wo_ref[c0:c0 + FF_CHUNK, :].astype(BF16))
        acc = part if acc is None else acc + part
    y = x + 0.5 * mod_ref[0, g_row:g_row + 1, :] * acc
    if with_final:
        ms = jnp.mean(y * y, axis=-1, keepdims=True)
        y = y * lax.rsqrt(ms + EPS) * fnw_ref[...]
    o_ref[...] = y


def _ffn(x2d, mod, nw, w_in, w_out, mod_rows, tiles_per_batch, mix=None, mix_gate_row=None, final_nw=None):
    t, d = x2d.shape
    tm = TOKEN_TILE
    d_ff = w_out.shape[0]
    assert d_ff % FF_CHUNK == 0
    row = lambda i: (i, 0)
    const = lambda i: (0, 0)
    in_specs = [pl.BlockSpec((tm, d), row),
                pl.BlockSpec((1, N_MOD, d), lambda i: (i // tiles_per_batch, 0, 0)),
                pl.BlockSpec((1, d), const),
                pl.BlockSpec(w_in.shape, const, pipeline_mode=pl.Buffered(1)),
                pl.BlockSpec(w_out.shape, const, pipeline_mode=pl.Buffered(1))]
    args = [x2d, mod, nw, w_in, w_out]
    if mix is not None:
        ma, mb, wmix = mix
        trow = lambda i: (i // tiles_per_batch, 0, i % tiles_per_batch)
        in_specs += [pl.BlockSpec((1, ma.shape[1], tm), trow),
                     pl.BlockSpec((1, mb.shape[1], tm), trow),
                     pl.BlockSpec(wmix.shape, const)]
        args += [ma, mb, wmix]
    if final_nw is not None:
        in_specs.append(pl.BlockSpec((1, d), const))
        args.append(final_nw)
    kern = functools.partial(_ffn_kernel, mod_rows=mod_rows, with_mix=mix is not None,
                             mix_gate_row=mix_gate_row, with_final=final_nw is not None)
    return pl.pallas_call(
        kern,
        grid=(t // tm,),
        in_specs=in_specs,
        out_specs=pl.BlockSpec((tm, d), row),
        out_shape=jax.ShapeDtypeStruct((t, d), F32),
        compiler_params=_cparams(1),
        name="ffn_mix" if mix is not None else "ffn",
    )(*args)


QPAD_W = NSA_Q_HEADS * LANES
MISC_I = N_GATES
MISC_F = N_GATES + MLSTM_HEADS
HALF_BLOCK_W = CMP_STRIDE * KV_WIDTH
B_KV = NSA_WIDTH
B_ML = B_KV + 6 * KV_WIDTH
B_MISC = B_ML + 4 * MLSTM_WIDTH


def _vt_with_ones(v):
    vt = v.T
    n = vt.shape[1]
    pad = jnp.where(lax.broadcasted_iota(jnp.int32, (VT_ROWS - NSA_HEAD_DIM, n), 0) == 0, 1.0, 0.0)
    parts = []
    for h in range(NSA_KV_HEADS):
        parts += [vt[h * NSA_HEAD_DIM:(h + 1) * NSA_HEAD_DIM, :], pad]
    return jnp.concatenate(parts, axis=0).astype(BF16)


def _inproj_kernel(x_ref, mod_ref, nw_ref, w_ref, wml_ref, wmisc_ref, b_ref,
                   qpad_ref, kc_ref, vc_ref, ksel_ref, kwin_ref, vtsel_ref, vtwin_ref,
                   ml_ref, vtm_ref, ogt_ref, misc_ref, misct_ref, rows_ref):
    tm = x_ref.shape[0]
    h = _norm_mod(x_ref[...], nw_ref[...], mod_ref[0, 3:4, :], mod_ref[0, 4:5, :])
    hb = h.astype(BF16)

    def proj(w_ref, c0, width, b0):
        return _dot(hb, w_ref[:, c0:c0 + width].astype(BF16)) + b_ref[:, b0 + c0:b0 + c0 + width]

    def half_blocks(val, out_ref):
        rows_ref[...] = val
        pieces = [rows_ref[pl.ds(l, tm // CMP_STRIDE, stride=CMP_STRIDE), :] for l in range(CMP_STRIDE)]
        out_ref[0] = jnp.concatenate(pieces, axis=1).astype(BF16)

    scale = NSA_HEAD_DIM ** -0.5 * LOG2E
    qt = (proj(w_ref, 0, NSA_WIDTH, 0) * scale).T.astype(BF16)
    zero_rows = jnp.zeros((NSA_HEAD_DIM, tm), BF16)
    for i in range(NSA_Q_HEADS):
        slots = [zero_rows] * NSA_KV_HEADS
        slots[i // NSA_GROUP] = qt[i * NSA_HEAD_DIM:(i + 1) * NSA_HEAD_DIM, :]
        qpad_ref[0, i * LANES:(i + 1) * LANES, :] = jnp.concatenate(slots, axis=0)
    kv_c = proj(w_ref, B_KV, 2 * KV_WIDTH, 0)
    half_blocks(kv_c[:, :KV_WIDTH], kc_ref)
    half_blocks(kv_c[:, KV_WIDTH:], vc_ref)
    kv_s = proj(w_ref, B_KV + 2 * KV_WIDTH, 2 * KV_WIDTH, 0)
    ksel_ref[...] = kv_s[:, :KV_WIDTH].astype(BF16)
    vtsel_ref[0] = _vt_with_ones(kv_s[:, KV_WIDTH:])
    kv_w = proj(w_ref, B_KV + 4 * KV_WIDTH, 2 * KV_WIDTH, 0)
    kwin_ref[...] = kv_w[:, :KV_WIDTH].astype(BF16)
    vtwin_ref[0] = _vt_with_ones(kv_w[:, KV_WIDTH:])
    for i in range(2):
        ml_ref[:, i * MLSTM_WIDTH:(i + 1) * MLSTM_WIDTH] = proj(wml_ref, i * MLSTM_WIDTH, MLSTM_WIDTH, B_ML)
    vtm_ref[0] = proj(wml_ref, 2 * MLSTM_WIDTH, MLSTM_WIDTH, B_ML).T.astype(BF16)
    ogt_ref[0] = proj(wml_ref, 3 * MLSTM_WIDTH, MLSTM_WIDTH, B_ML).T
    misc = proj(wmisc_ref, 0, LANES, B_MISC)
    misc_ref[...] = misc
    misct_ref[0] = misc.T


def _inproj(x2d, mod, nw, weights, b_pack, batch, seq):
    t, d = x2d.shape
    tm = WIDE_TOKEN_TILE
    tpb = seq // tm
    row = lambda i: (i, 0)
    const = lambda i: (0, 0)
    trow = lambda i: (i // tpb, 0, i % tpb)
    hrow = lambda i: (i // tpb, i % tpb, 0)
    vt_rows = NSA_KV_HEADS * VT_ROWS
    n_half = seq // CMP_STRIDE
    out_shapes = [
        jax.ShapeDtypeStruct((batch, QPAD_W, seq), BF16),
        jax.ShapeDtypeStruct((batch, n_half, HALF_BLOCK_W), BF16),
        jax.ShapeDtypeStruct((batch, n_half, HALF_BLOCK_W), BF16),
        jax.ShapeDtypeStruct((t, KV_WIDTH), BF16),
        jax.ShapeDtypeStruct((t, KV_WIDTH), BF16),
        jax.ShapeDtypeStruct((batch, vt_rows, seq), BF16),
        jax.ShapeDtypeStruct((batch, vt_rows, seq), BF16),
        jax.ShapeDtypeStruct((t, 2 * MLSTM_WIDTH), F32),
        jax.ShapeDtypeStruct((batch, MLSTM_WIDTH, seq), BF16),
        jax.ShapeDtypeStruct((batch, MLSTM_WIDTH, seq), F32),
        jax.ShapeDtypeStruct((t, LANES), F32),
        jax.ShapeDtypeStruct((batch, LANES, seq), F32),
    ]
    out_specs = [
        pl.BlockSpec((1, QPAD_W, tm), trow),
        pl.BlockSpec((1, tm // CMP_STRIDE, HALF_BLOCK_W), hrow),
        pl.BlockSpec((1, tm // CMP_STRIDE, HALF_BLOCK_W), hrow),
        pl.BlockSpec((tm, KV_WIDTH), row),
        pl.BlockSpec((tm, KV_WIDTH), row),
        pl.BlockSpec((1, vt_rows, tm), trow),
        pl.BlockSpec((1, vt_rows, tm), trow),
        pl.BlockSpec((tm, 2 * MLSTM_WIDTH), row),
        pl.BlockSpec((1, MLSTM_WIDTH, tm), trow),
        pl.BlockSpec((1, MLSTM_WIDTH, tm), trow),
        pl.BlockSpec((tm, LANES), row),
        pl.BlockSpec((1, LANES, tm), trow),
    ]
    return pl.pallas_call(
        _inproj_kernel,
        grid=(t // tm,),
        in_specs=[pl.BlockSpec((tm, d), row),
                  pl.BlockSpec((1, N_MOD, d), lambda i: (i // tpb, 0, 0)),
                  pl.BlockSpec((1, d), const)]
                 + [pl.BlockSpec(w.shape, const, pipeline_mode=pl.Buffered(1)) for w in weights]
                 + [pl.BlockSpec(b_pack.shape, const)],
        out_specs=out_specs,
        out_shape=out_shapes,
        scratch_shapes=[pltpu.VMEM((tm, KV_WIDTH), F32)],
        compiler_params=_cparams(1),
        name="inproj",
    )(x2d, mod, nw, *weights, b_pack)


def _split_inproj(w_in, b_in):
    d = w_in.shape[0]
    offs = np.cumsum([0, NSA_WIDTH] + [KV_WIDTH] * 6 + [N_GATES] + [MLSTM_WIDTH] * 4 + [MLSTM_HEADS] * 2)
    g0, m0, i0 = int(offs[7]), int(offs[8]), int(offs[12])
    pad = LANES - N_GATES - 2 * MLSTM_HEADS
    misc_w = jnp.concatenate([w_in[:, g0:g0 + N_GATES], w_in[:, i0:i0 + 2 * MLSTM_HEADS],
                              jnp.zeros((d, pad), w_in.dtype)], axis=1)
    misc_b = jnp.concatenate([b_in[g0:g0 + N_GATES], b_in[i0:i0 + 2 * MLSTM_HEADS], jnp.zeros((pad,), b_in.dtype)])
    weights = [w_in[:, :g0], w_in[:, m0:i0], misc_w]
    b_pack = jnp.concatenate([b_in[:g0], b_in[m0:i0], misc_b]).reshape(1, -1).astype(F32)
    assert b_pack.shape[1] == B_MISC + LANES
    return weights, b_pack


def _compress_kernel(xk_ref, xv_ref, wa_ref, wb_ref, pe_ref, w1_ref, w2_ref, kc_ref, vtc_ref):
    n_half = xk_ref.shape[1]

    def one(x_ref, s):
        x = x_ref[0]
        a = _dot(x, wa_ref[s])
        b = _dot(x, wb_ref[s])
        pe_term = jnp.dot(pe_ref[s], w1_ref[s], preferred_element_type=F32, precision=HIGHEST)[0:1, :]
        pe2 = jnp.concatenate([pe_term] * NSA_KV_HEADS, axis=1)
        pre = a + pltpu.roll(b, n_half - 1, 0) + pe2
        hid = 0.5 * pre * (1.0 + jnp.tanh(0.7978845608028654 * (pre + 0.044715 * pre * pre * pre)))
        return _dot(hid.astype(BF16), w2_ref[s])

    kc_ref[0] = one(xk_ref, 0).astype(BF16)
    vtc_ref[0] = _vt_with_ones(one(xv_ref, 1))


def _compress(xk, xv, wa, wb, pe8, w1, w2e):
    batch, n_half, width = xk.shape
    bsel = lambda b: (b, 0, 0)
    c3 = lambda b: (0, 0, 0)
    vt_rows = NSA_KV_HEADS * VT_ROWS
    return pl.pallas_call(
        _compress_kernel,
        grid=(batch,),
        in_specs=[pl.BlockSpec((1, n_half, width), bsel), pl.BlockSpec((1, n_half, width), bsel),
                  pl.BlockSpec(wa.shape, c3), pl.BlockSpec(wb.shape, c3),
                  pl.BlockSpec(pe8.shape, c3), pl.BlockSpec(w1.shape, c3), pl.BlockSpec(w2e.shape, c3)],
        out_specs=[pl.BlockSpec((1, n_half, KV_WIDTH), bsel), pl.BlockSpec((1, vt_rows, n_half), bsel)],
        out_shape=[jax.ShapeDtypeStruct((batch, n_half, KV_WIDTH), BF16),
                   jax.ShapeDtypeStruct((batch, vt_rows, n_half), BF16)],
        compiler_params=_cparams(1),
        name="compress",
    )(xk, xv, wa, wb, pe8, w1, w2e)


def _pack_compress(pe, w1, w2):
    half = CMP_BLOCK // 2
    w1r = w1.reshape(CMP_BLOCK, NSA_HEAD_DIM, CMP_HIDDEN)
    def block_diag(w, axis):
        z = jnp.zeros_like(w)
        rows = [jnp.concatenate([w if hh == h2 else z for h2 in range(NSA_KV_HEADS)], axis=-1)
                for hh in range(NSA_KV_HEADS)]
        return jnp.concatenate(rows, axis=axis)

    def expand(w_half):
        return block_diag(w_half, 1).reshape(half * KV_WIDTH, NSA_KV_HEADS * CMP_HIDDEN)

    wa = expand(w1r[:half]).astype(BF16)
    wb = expand(w1r[half:]).astype(BF16)
    w2e = block_diag(w2, 0).astype(BF16)
    pe8 = jnp.concatenate([pe.reshape(1, CMP_BLOCK * NSA_HEAD_DIM),
                           jnp.zeros((7, CMP_BLOCK * NSA_HEAD_DIM), pe.dtype)], axis=0)
    return wa, wb, pe8, w1, w2e


def _q_cols(q_ref, h, cols):
    return jnp.concatenate(
        [q_ref[0, (NSA_GROUP * h + g) * LANES:(NSA_GROUP * h + g + 1) * LANES, cols] for g in range(NSA_GROUP)],
        axis=1)


def _q_aug(q_ref, qs_ref, h, cols=slice(0, QT)):
    return jnp.concatenate([_q_cols(q_ref, h, cols), qs_ref[h]], axis=0)


def _slope2(head):
    return float(2.0 ** (-8.0 * (head + 1) / NSA_Q_HEADS)) * LOG2E


def _bf16_split3(x):
    parts = []
    for _ in range(3):
        p = float(np.asarray(x, np.float32).astype(BF16).astype(np.float32))
        parts.append(p)
        x = x - p
    return parts


def _init_alibi_operands(qs_ref, kpos_ref, key_stride):
    lane = lax.broadcasted_iota(jnp.int32, (QT, LANES), 1)
    row = lax.broadcasted_iota(jnp.int32, (QT, LANES), 0)
    kpos_ref[...] = jnp.where(lane < 3, row.astype(F32), 0.0).astype(BF16)
    srow = lax.broadcasted_iota(jnp.int32, (LANES, QT), 0)
    for h in range(NSA_KV_HEADS):
        for g in range(NSA_GROUP):
            hi, mid, lo = _bf16_split3(key_stride * _slope2(NSA_GROUP * h + g))
            blk = jnp.where(srow == 0, hi, jnp.where(srow == 1, mid, jnp.where(srow == 2, lo, 0.0)))
            qs_ref[h, :, g * QT:(g + 1) * QT] = blk.astype(BF16)


def _gate_row(gt_ref, head, branch, cols):
    r = head * 3 + branch
    return _sigmoid(gt_ref[0, r:r + 1, cols])


def _softmax_jobs(jobs):
    scores = [_dot(jnp.concatenate(job[1], axis=0), job[0]) for job in jobs]
    staged = []
    for (q_aug, k_tiles, masks, offs, value_groups, state), st_all in zip(jobs, scores):
        n_t = len(k_tiles)
        per_g = []
        for g in range(NSA_GROUP):
            sl = slice(g * QT, (g + 1) * QT)
            ss = []
            for u in range(n_t):
                s = st_all[u * QT:(u + 1) * QT, sl]
                ss.append(s if masks[u] is None else jnp.where(masks[u], s, NEG))
            mx = None
            for u in range(n_t):
                cand = jnp.max(ss[u], axis=0, keepdims=True) + offs[u][g]
                mx = cand if mx is None else jnp.maximum(mx, cand)
            alpha = None
            if state is None:
                m_new = mx
            else:
                m_ref, _, h = state
                m_old = m_ref[h, :, sl]
                m_new = jnp.maximum(m_old, mx)
                alpha = jnp.exp2(m_old - m_new)
                m_ref[h, :, sl] = m_new
            m_use = jnp.where(m_new < 0.5 * NEG, 0.0, m_new)
            per_g.append((ss, m_use, alpha))
        staged.append(per_g)
    ets = []
    for (q_aug, k_tiles, masks, offs, value_groups, state), per_g in zip(jobs, staged):
        n_t = len(k_tiles)
        rows = [jnp.concatenate([jnp.exp2(per_g[g][0][u] - (per_g[g][1] - offs[u][g])).astype(BF16)
                                 for g in range(NSA_GROUP)], axis=1) for u in range(n_t)]
        ets.append(jnp.concatenate(rows, axis=0))
    results = []
    for (q_aug, k_tiles, masks, offs, value_groups, state), per_g, et in zip(jobs, staged, ets):
        mats = [jnp.concatenate(grp, axis=1) for grp in value_groups]
        pv_all = _dot(jnp.concatenate(mats, axis=0), et)
        pvs, r0 = [], 0
        for mat in mats:
            pvs.append(pv_all[r0:r0 + mat.shape[0], :])
            r0 += mat.shape[0]
        if state is None:
            results.append(pvs)
        else:
            _, acc_ref, h = state
            acc_ref[h] = jnp.concatenate([pg[2] for pg in per_g], axis=1) * acc_ref[h] + pvs[0]
            results.append(None)
    return results


def _gated(gt_ref, h, branch, pv, cols=slice(0, QT)):
    inv = 1.0 / jnp.maximum(pv[NSA_HEAD_DIM:NSA_HEAD_DIM + 1, :], 1e-30)
    out = []
    for g in range(NSA_GROUP):
        sl = slice(g * QT, (g + 1) * QT)
        out.append(pv[0:NSA_HEAD_DIM, sl] * (inv[:, sl] * _gate_row(gt_ref, NSA_GROUP * h + g, branch, cols)))
    return out


def _nsa_cmp_kernel(q_ref, kc_ref, vtc_ref, gt_ref, oc_ref, sel_ref, cnt_ref,
                    qs_ref, kpos_ref, ovt_ref, pvo_ref, pvi_ref):
    ncp = kc_ref.shape[1]
    n_ct = ncp // QT
    n_sel = sel_ref.shape[3]
    tiles = range(CMP_TILES)
    starts = [(pl.program_id(1) * CMP_TILES + s) * QT for s in tiles]
    cols = [slice(s * QT, (s + 1) * QT) for s in tiles]
    units = [(s, h) for s in tiles for h in range(NSA_KV_HEADS)]
    tile_span = QT * CMP_STRIDE

    @pl.when((pl.program_id(0) == 0) & (pl.program_id(1) == 0))
    def _():
        _init_alibi_operands(qs_ref, kpos_ref, float(CMP_STRIDE))
        j_o = lax.broadcasted_iota(jnp.int32, (n_sel, ncp), 0)
        c_o = lax.broadcasted_iota(jnp.int32, (n_sel, ncp), 1)
        ov = ((c_o * CMP_STRIDE <= j_o * SEL_BLOCK + (SEL_BLOCK - 1))
              & (c_o * CMP_STRIDE + (CMP_BLOCK - 1) >= j_o * SEL_BLOCK))
        ovt_ref[...] = jnp.where(ov, 1.0, 0.0).astype(BF16)

    a0 = (lax.broadcasted_iota(jnp.int32, (QT, QT), 1)
          - CMP_STRIDE * lax.broadcasted_iota(jnp.int32, (QT, QT), 0))
    q_augs = [_q_aug(q_ref, qs_ref, h, cols[s]) for s, h in units]
    n_vis = jnp.minimum((starts[-1] + QT - CMP_BLOCK) // tile_span + 1, n_ct)

    for k in range(1, n_ct + 1):
        @pl.when(n_vis == k)
        def _(k=k):
            jobs = []
            for i, (s, h) in enumerate(units):
                k_tiles, masks, offs, vts, ovs = [], [], [], [], []
                for ci in range(k):
                    cs = slice(ci * QT, (ci + 1) * QT)
                    first_end = ci * tile_span + CMP_BLOCK - 1
                    k_tiles.append(jnp.concatenate([kc_ref[0, cs, :], kpos_ref[...]], axis=1))
                    masks.append(a0 + (starts[s] - first_end) >= 0)
                    offs.append([_slope2(NSA_GROUP * h + g) * first_end for g in range(NSA_GROUP)])
                    vts.append(vtc_ref[0, h * VT_ROWS:(h + 1) * VT_ROWS, cs])
                    ovs.append(ovt_ref[:, cs])
                jobs.append((q_augs[i], k_tiles, masks, offs, [vts, ovs], None))
            for i, (pvo, pvi) in enumerate(_softmax_jobs(jobs)):
                pvo_ref[i] = pvo
                pvi_ref[i] = pvi

    j_io = lax.broadcasted_iota(jnp.int32, (n_sel, QT), 0)
    q_io = lax.broadcasted_iota(jnp.int32, (n_sel, QT), 1)
    ones8 = jnp.ones((8, QT), BF16)

    taken = -(2.0 ** 126)
    scores = []
    for i, (s, h) in enumerate(units):
        t_j = starts[s] + q_io
        cur = t_j // SEL_BLOCK
        valid = j_io * SEL_BLOCK <= t_j
        forced = (j_io == 0) | (j_io == cur) | (j_io == cur - 1)
        pvo = pvo_ref[i]
        o_c = _gated(gt_ref, h, 0, pvo, cols[s])
        inv = 1.0 / jnp.maximum(pvo[NSA_HEAD_DIM:NSA_HEAD_DIM + 1, :], 1e-30)
        imp = None
        for g in range(NSA_GROUP):
            sl = slice(g * QT, (g + 1) * QT)
            oc_ref[0, NSA_GROUP * h + g, :, cols[s]] = o_c[g]
            term = pvi_ref[i, :, sl] * inv[:, sl]
            imp = term if imp is None else imp + term
        scores.append(jnp.where(forced, taken, jnp.where(valid, imp, -FORCE_BONUS)))

    n_cls = n_sel // TOPK_ROWS
    cls = jnp.minimum((starts[-1] + QT + SEL_BLOCK * TOPK_ROWS - 1) // (SEL_BLOCK * TOPK_ROWS), n_cls)

    for k in range(1, n_cls + 1):
        @pl.when(cls == k)
        def _(rows=k * TOPK_ROWS):
            j_sub = j_io[:rows]

            def pick(_, carry):
                out = []
                for sc in carry:
                    mx = jnp.max(sc, axis=0, keepdims=True)
                    idx = jnp.min(jnp.where(sc == mx, j_sub, n_sel), axis=0, keepdims=True)
                    out.append(jnp.where(j_sub == idx, taken, sc))
                return tuple(out)

            picked = lax.fori_loop(0, SEL_TOPK - N_FORCED, pick, tuple(sc[:rows] for sc in scores))
            for i, (s, h) in enumerate(units):
                sel = jnp.where(picked[i] <= 0.5 * taken, 1.0, 0.0)
                if rows < n_sel:
                    sel = jnp.concatenate([sel, jnp.zeros((n_sel - rows, QT), F32)], axis=0)
                sel_ref[0, h, s] = sel
                cnt = _dot_nt(ones8, sel.astype(BF16))
                cnt_ref[0, s, h] = cnt[0:1, :]


def _nsa_cmp(qpad3, kc, vtc, gates_t, batch, seq):
    nb = seq // QT
    n_sel = seq // SEL_BLOCK
    ncp = kc.shape[1]
    assert ncp % QT == 0 and nb % CMP_TILES == 0
    vt_rows = NSA_KV_HEADS * VT_ROWS
    qw = QT * CMP_TILES
    n_units = CMP_TILES * NSA_KV_HEADS
    return pl.pallas_call(
        _nsa_cmp_kernel,
        grid=(batch, nb // CMP_TILES),
        in_specs=[pl.BlockSpec((1, QPAD_W, qw), lambda b, i: (b, 0, i)),
                  pl.BlockSpec((1, ncp, KV_WIDTH), lambda b, i: (b, 0, 0)),
                  pl.BlockSpec((1, vt_rows, ncp), lambda b, i: (b, 0, 0)),
                  pl.BlockSpec((1, N_GATES, qw), lambda b, i: (b, 0, i))],
        out_specs=[pl.BlockSpec((1, NSA_Q_HEADS, NSA_HEAD_DIM, qw), lambda b, i: (b, 0, 0, i)),
                   pl.BlockSpec((1, NSA_KV_HEADS, CMP_TILES, n_sel, QT), lambda b, i: (b, 0, i, 0, 0)),
                   pl.BlockSpec((1, CMP_TILES, NSA_KV_HEADS, 1, n_sel), lambda b, i: (b, i, 0, 0, 0))],
        out_shape=[jax.ShapeDtypeStruct((batch, NSA_Q_HEADS, NSA_HEAD_DIM, seq), F32),
                   jax.ShapeDtypeStruct((batch, NSA_KV_HEADS, nb, n_sel, QT), F32),
                   jax.ShapeDtypeStruct((batch, nb, NSA_KV_HEADS, 1, n_sel), F32)],
        scratch_shapes=[pltpu.VMEM((NSA_KV_HEADS, LANES, NSA_GROUP * QT), BF16),
                        pltpu.VMEM((QT, LANES), BF16),
                        pltpu.VMEM((n_sel, ncp), BF16),
                        pltpu.VMEM((n_units, VT_ROWS, NSA_GROUP * QT), F32),
                        pltpu.VMEM((n_units, n_sel, NSA_GROUP * QT), F32)],
        compiler_params=_cparams(2),
        name="nsa_cmp",
    )(qpad3, kc, vtc, gates_t)


def _nsa_sel_kernel(count_ref, list_ref, q_ref, ksel_ref, vtsel_ref, kwin_ref, vtwin_ref, selm_ref, gt_ref,
                    oc_ref, o_ref, m_sel, acc_sel, qs_ref, kpos_ref, part_ref, kposm_ref):
    b = pl.program_id(0)
    nb = pl.num_programs(1) * SEL_TILES
    tiles = range(SEL_TILES)
    qbs = [pl.program_id(1) * SEL_TILES + s for s in tiles]
    cols = [slice(s * QT, (s + 1) * QT) for s in tiles]
    units = [(s, h) for s in tiles for h in range(NSA_KV_HEADS)]
    k_io = lax.broadcasted_iota(jnp.int32, (QT, QT), 0)
    q_io = lax.broadcasted_iota(jnp.int32, (QT, QT), 1)
    causal = q_io >= k_io
    in_window = q_io < k_io

    pen_row0 = 3
    pen_rows = 16
    assert pen_row0 + 2 * SEL_UNROLL <= pen_rows

    @pl.when((b == 0) & (pl.program_id(1) == 0))
    def _():
        _init_alibi_operands(qs_ref, kpos_ref, 1.0)
        lane = lax.broadcasted_iota(jnp.int32, (QT, LANES), 1)
        row = lax.broadcasted_iota(jnp.int32, (QT, LANES), 0)
        base = jnp.where(lane < pen_row0, row.astype(F32), 0.0)
        for u in range(SEL_UNROLL):
            ind = jnp.where(lane == pen_row0 + 2 * u, jnp.where(row < SEL_BLOCK, 1.0, 0.0),
                            jnp.where(lane == pen_row0 + 2 * u + 1, jnp.where(row >= SEL_BLOCK, 1.0, 0.0), 0.0))
            kposm_ref[u] = (base + ind).astype(BF16)

    def k_aug(k_ref, t, u=None):
        off = pl.multiple_of(t * QT, QT)
        tail = kpos_ref[...] if u is None else kposm_ref[u]
        return jnp.concatenate([k_ref[0, pl.ds(off, QT), :], tail], axis=1)

    def v_ext(vt_ref, h, t):
        return vt_ref[0, h * VT_ROWS:(h + 1) * VT_ROWS, pl.ds(pl.multiple_of(t * QT, QT), QT)]

    def tile_offs(h, t, live=None):
        pos = (t * QT).astype(F32)
        out = [_slope2(NSA_GROUP * h + g) * pos for g in range(NSA_GROUP)]
        return out if live is None else [jnp.where(live, o, NEG) for o in out]

    def penalties(s, h, t):
        mrow = selm_ref[0, h, s, pl.ds(2 * t, 2), :]
        return (mrow - 1.0) * (-NEG)

    q_tops = [_q_cols(q_ref, h, cols[s]) for s, h in units]
    q_augs = [jnp.concatenate([q_tops[i], qs_ref[h]], axis=0) for i, (s, h) in enumerate(units)]
    zero_rows = jnp.zeros((LANES - pen_rows, NSA_GROUP * QT), BF16)

    def q_aug_sel(i, h, pens):
        blk = jnp.concatenate([jnp.zeros((pen_row0, QT), F32)] + pens
                              + [jnp.zeros((pen_rows - pen_row0 - 2 * len(pens), QT), F32)], axis=0)
        ext = qs_ref[h, 0:pen_rows, :].astype(F32) + jnp.concatenate([blk] * NSA_GROUP, axis=1)
        return jnp.concatenate([q_tops[i], ext.astype(BF16), zero_rows], axis=0)

    slot = [(b * nb + qbs[s]) * NSA_KV_HEADS + h for s, h in units]
    n_tiles = [count_ref[sl] for sl in slot]

    m_sel[...] = jnp.full(m_sel.shape, NEG, F32)
    acc_sel[...] = jnp.zeros(acc_sel.shape, F32)

    def sel_jobs(trip, first, step):
        jobs = []
        for i, (s, h) in enumerate(units):
            k_tiles, masks, offs, vts, pens = [], [], [], [], []
            for u in range(SEL_UNROLL):
                if first and step == 0 and u == 0:
                    t, mask, off = qbs[s], causal, tile_offs(h, qbs[s])
                else:
                    idx = (trip * SEL_CHAIN + step) * SEL_UNROLL + u - 1
                    live = idx < n_tiles[i]
                    t = jnp.where(live, list_ref[slot[i] * nb + jnp.minimum(idx, nb - 1)], 0)
                    mask, off = None, tile_offs(h, t, live)
                k_tiles.append(k_aug(ksel_ref, t, u))
                masks.append(mask)
                offs.append(off)
                vts.append(v_ext(vtsel_ref, h, t))
                pens.append(penalties(s, h, t))
            jobs.append((q_aug_sel(i, h, pens), k_tiles, masks, offs, [vts], (m_sel, acc_sel, i)))
        return jobs

    n_win = WINDOW // QT + 1
    win_jobs = []
    for i, (s, h) in enumerate(units):
        k_tiles, masks, offs, vts = [], [], [], []
        for u in range(n_win):
            raw = qbs[s] - (n_win - 1) + u
            t = jnp.maximum(raw, 0)
            k_tiles.append(k_aug(kwin_ref, t))
            masks.append(in_window if u == 0 else causal if u == n_win - 1 else None)
            offs.append(tile_offs(h, t, raw >= 0))
            vts.append(v_ext(vtwin_ref, h, t))
        win_jobs.append((q_augs[i], k_tiles, masks, offs, [vts], None))
    jobs = sel_jobs(0, True, 0) + win_jobs
    for step in range(1, SEL_CHAIN):
        jobs += sel_jobs(0, True, step)
    results = _softmax_jobs(jobs)
    for i, (s, h) in enumerate(units):
        o_w = _gated(gt_ref, h, 2, results[len(units) + i][0], cols[s])
        for g in range(NSA_GROUP):
            head = NSA_GROUP * h + g
            part_ref[s, head] = oc_ref[0, head, :, cols[s]] + o_w[g]

    def sel_body(trip, carry):
        jobs = []
        for step in range(SEL_CHAIN):
            jobs += sel_jobs(trip, False, step)
        _softmax_jobs(jobs)
        return carry

    per_trip = SEL_UNROLL * SEL_CHAIN
    n_trips = (functools.reduce(jnp.maximum, n_tiles) + per_trip) // per_trip
    lax.fori_loop(1, n_trips, sel_body, 0)

    for i, (s, h) in enumerate(units):
        o_s = _gated(gt_ref, h, 1, acc_sel[i], cols[s])
        for g in range(NSA_GROUP):
            head = NSA_GROUP * h + g
            o_ref[0, head, :, cols[s]] = part_ref[s, head] + o_s[g]


def _tile_lists(cnt, batch, nb):
    pair = cnt.reshape(batch, nb, NSA_KV_HEADS, nb, 2).sum(axis=-1)
    p_io = lax.broadcasted_iota(jnp.int32, pair.shape, 3)
    flag = (pair > 0.5) & (p_io < lax.broadcasted_iota(jnp.int32, pair.shape, 1))
    rank = jnp.cumsum(flag.astype(jnp.int32), axis=-1) - 1
    slot_io = lax.broadcasted_iota(jnp.int32, pair.shape + (nb,), 4)
    hit = flag[..., None] & (rank[..., None] == slot_io)
    lists = jnp.sum(jnp.where(hit, p_io[..., None], 0), axis=3)
    counts = jnp.sum(flag.astype(jnp.int32), axis=-1)
    return counts.reshape(-1), lists.reshape(-1)


def _nsa_sel(counts, lists, qpad3, ksel, vtsel, kwin, vtwin, selm, gates_t, oc, batch, seq):
    nb = seq // QT
    assert nb % SEL_TILES == 0
    res = lambda b, i, c, l: (b, 0, 0)
    vt_rows = NSA_KV_HEADS * VT_ROWS
    qw = QT * SEL_TILES
    n_units = SEL_TILES * NSA_KV_HEADS
    grid_spec = pltpu.PrefetchScalarGridSpec(
        num_scalar_prefetch=2,
        grid=(batch, nb // SEL_TILES),
        in_specs=[pl.BlockSpec((1, QPAD_W, qw), lambda b, i, c, l: (b, 0, i)),
                  pl.BlockSpec((1, seq, KV_WIDTH), res),
                  pl.BlockSpec((1, vt_rows, seq), res),
                  pl.BlockSpec((1, seq, KV_WIDTH), res),
                  pl.BlockSpec((1, vt_rows, seq), res),
                  pl.BlockSpec((1, NSA_KV_HEADS, SEL_TILES, 2 * nb, QT), lambda b, i, c, l: (b, 0, i, 0, 0)),
                  pl.BlockSpec((1, N_GATES, qw), lambda b, i, c, l: (b, 0, i)),
                  pl.BlockSpec((1, NSA_Q_HEADS, NSA_HEAD_DIM, qw), lambda b, i, c, l: (b, 0, 0, i))],
        out_specs=pl.BlockSpec((1, NSA_Q_HEADS, NSA_HEAD_DIM, qw), lambda b, i, c, l: (b, 0, 0, i)),
        scratch_shapes=[pltpu.VMEM((n_units, 1, NSA_GROUP * QT), F32),
                        pltpu.VMEM((n_units, VT_ROWS, NSA_GROUP * QT), F32),
                        pltpu.VMEM((NSA_KV_HEADS, LANES, NSA_GROUP * QT), BF16),
                        pltpu.VMEM((QT, LANES), BF16),
                        pltpu.VMEM((SEL_TILES, NSA_Q_HEADS, NSA_HEAD_DIM, QT), F32),
                        pltpu.VMEM((SEL_UNROLL, QT, LANES), BF16)],
    )
    return pl.pallas_call(
        _nsa_sel_kernel,
        grid_spec=grid_spec,
        out_shape=jax.ShapeDtypeStruct((batch, NSA_Q_HEADS, NSA_HEAD_DIM, seq), F32),
        compiler_params=_cparams(2),
        name="nsa_sel",
    )(counts, lists, qpad3, ksel, vtsel, kwin, vtwin, selm, gates_t, oc)


def _mlstm_kernel(q_ref, k_ref, vt_ref, ogt_ref, misc_ref, misct_ref, cw_ref, cb_ref, nw_ref,
                  o_ref, xq_ref, xk_ref, c_ref, n_ref, m_ref):
    tm = q_ref.shape[0]
    L = MLSTM_L
    hd = MLSTM_HEAD_DIM
    halo = 8

    @pl.when(pl.program_id(1) == 0)
    def _():
        xq_ref[0:halo, :] = jnp.zeros((halo, MLSTM_WIDTH), F32)
        xk_ref[0:halo, :] = jnp.zeros((halo, MLSTM_WIDTH), F32)
        c_ref[...] = jnp.zeros_like(c_ref)
        n_ref[...] = jnp.zeros_like(n_ref)
        m_ref[...] = jnp.zeros_like(m_ref)

    def conv(x_ref, buf_ref, col0):
        buf_ref[halo:halo + tm, :] = x_ref[...]
        y = cb_ref[:, col0:col0 + MLSTM_WIDTH]
        for kk in range(CONV_WIDTH):
            r0 = halo - (CONV_WIDTH - 1) + kk
            y = y + cw_ref[kk:kk + 1, col0:col0 + MLSTM_WIDTH] * buf_ref[r0:r0 + tm, :]
        tail = buf_ref[tm:tm + halo, :]
        buf_ref[0:halo, :] = tail
        return _silu(y)

    qc = conv(q_ref, xq_ref, 0)
    kc = conv(k_ref, xk_ref, MLSTM_WIDTH) * (hd ** -0.5)

    misc = misc_ref[...]
    misct = misct_ref[0]
    lf_c = -jnp.log(1.0 + jnp.exp(-misc))
    lf_r = -jnp.log(1.0 + jnp.exp(-misct))
    row_io = lax.broadcasted_iota(jnp.int32, (L, L), 0)
    col_io = lax.broadcasted_iota(jnp.int32, (L, L), 1)
    tril = jnp.where(col_io <= row_io, 1.0, 0.0).astype(BF16)
    triu = jnp.where(row_io <= col_io, 1.0, 0.0).astype(BF16)

    def split3(x):
        hi = x.astype(BF16)
        r1 = x - hi.astype(F32)
        mid = r1.astype(BF16)
        return hi, mid, (r1 - mid.astype(F32)).astype(BF16)

    lf_c3 = split3(lf_c)
    lf_r3 = split3(lf_r)
    b_c = jnp.concatenate([sum(_dot(tril, p[ci * L:(ci + 1) * L, :]) for p in lf_c3)
                           for ci in range(tm // L)], axis=0)
    b_r = jnp.concatenate([sum(_dot(p[:, ci * L:(ci + 1) * L], triu) for p in lf_r3)
                           for ci in range(tm // L)], axis=1)

    tri_t = lax.broadcasted_iota(jnp.int32, (L, L), 0) <= lax.broadcasted_iota(jnp.int32, (L, L), 1)
    nw_cols = [jnp.broadcast_to(nw_ref[:, h * hd:(h + 1) * hd], (hd, hd)).T for h in range(MLSTM_HEADS)]

    def split3_rows(row):
        hi = row.astype(BF16)
        r1 = row - hi.astype(F32)
        mid = r1.astype(BF16)
        lo = (r1 - mid.astype(F32)).astype(BF16)
        sub = lax.broadcasted_iota(jnp.int32, (8, row.shape[1]), 0)
        out = jnp.where(sub == 0, hi.astype(F32), jnp.where(sub == 1, mid.astype(F32),
                                                            jnp.where(sub == 2, lo.astype(F32), 0.0)))
        return out.astype(BF16)

    def rows_sum3(x8):
        return x8[0:1, :] + x8[1:2, :] + x8[2:3, :]

    for ci in range(tm // L):
        r0 = ci * L
        for h in range(MLSTM_HEADS):
            cs = slice(h * hd, (h + 1) * hd)
            qb16 = qc[r0:r0 + L, cs].astype(BF16)
            kb16 = kc[r0:r0 + L, cs].astype(BF16)
            vt = vt_ref[0, cs, r0:r0 + L]
            a_col = (misc[r0:r0 + L, MISC_I + h:MISC_I + h + 1]
                     - b_c[r0:r0 + L, MISC_F + h:MISC_F + h + 1])
            brow = b_r[MISC_F + h:MISC_F + h + 1, r0:r0 + L]
            lirow = misct[MISC_I + h:MISC_I + h + 1, r0:r0 + L]
            m_prev = m_ref[h]
            ct = c_ref[h]
            nrow = n_ref[h]

            dmat = jnp.where(tri_t, brow + a_col, NEG)
            m_inter = brow + m_prev
            m_t = jnp.maximum(jnp.max(dmat, axis=0, keepdims=True), m_inter)
            w = jnp.exp(dmat - m_t) * _dot_nt(kb16, qb16)
            decay = jnp.exp(m_inter - m_t)
            num = _dot(vt, w.astype(BF16)) + decay * _dot_nt(ct.astype(BF16), qb16)
            nq = rows_sum3(_dot_nt(split3_rows(nrow), qb16))
            den = jnp.sum(w, axis=0, keepdims=True) + decay * nq
            hout = num * (1.0 / jnp.maximum(jnp.abs(den), jnp.exp(-m_t)))

            b_last = brow[:, L - 1:L]
            grow = b_last - brow + lirow
            m_new = jnp.maximum(b_last + m_prev, jnp.max(grow, axis=1, keepdims=True))
            wk = jnp.exp(grow - m_new)
            d_c = jnp.exp(b_last + m_prev - m_new)
            c_ref[h] = d_c * ct + _dot((vt.astype(F32) * wk).astype(BF16), kb16)
            n_ref[h] = d_c * nrow + rows_sum3(_dot(split3_rows(wk), kb16))
            m_ref[h] = m_new

            hn = hout * lax.rsqrt(jnp.mean(hout * hout, axis=0, keepdims=True) + EPS) * nw_cols[h]
            o_ref[0, cs, r0:r0 + L] = (hn * _sigmoid(ogt_ref[0, cs, r0:r0 + L])).astype(o_ref.dtype)


def _mlstm(ml, vt_m, og_t, misc, misc_t, conv_w, conv_b, norm_w, batch, seq):
    tm = WIDE_TOKEN_TILE
    tpb = seq // tm
    col = lambda j: (lambda b, i: (b * tpb + i, j))
    const = lambda b, i: (0, 0)
    trow = lambda b, i: (b, 0, i)
    return pl.pallas_call(
        _mlstm_kernel,
        grid=(batch, tpb),
        in_specs=[pl.BlockSpec((tm, MLSTM_WIDTH), col(0)),
                  pl.BlockSpec((tm, MLSTM_WIDTH), col(1)),
                  pl.BlockSpec((1, MLSTM_WIDTH, tm), trow),
                  pl.BlockSpec((1, MLSTM_WIDTH, tm), trow),
                  pl.BlockSpec((tm, LANES), col(0)),
                  pl.BlockSpec((1, LANES, tm), trow),
                  pl.BlockSpec(conv_w.shape, const),
                  pl.BlockSpec(conv_b.shape, const),
                  pl.BlockSpec(norm_w.shape, const)],
        out_specs=pl.BlockSpec((1, MLSTM_WIDTH, tm), trow),
        out_shape=jax.ShapeDtypeStruct((batch, MLSTM_WIDTH, seq), BF16),
        scratch_shapes=[pltpu.VMEM((tm + 8, MLSTM_WIDTH), F32),
                        pltpu.VMEM((tm + 8, MLSTM_WIDTH), F32),
                        pltpu.VMEM((MLSTM_HEADS, MLSTM_HEAD_DIM, MLSTM_HEAD_DIM), F32),
                        pltpu.VMEM((MLSTM_HEADS, 1, MLSTM_HEAD_DIM), F32),
                        pltpu.VMEM((MLSTM_HEADS, 1, 1), F32)],
        compiler_params=_cparams(2),
        name="mlstm",
    )(ml, ml, vt_m, og_t, misc, misc_t, conv_w, conv_b, norm_w)


def _layer(x2d, c, batch, seq, w_ada, b_ada, norm_ffn1_w, ffn1_w_in, ffn1_w_out, norm_mix_w, w_in, b_in,
           cmp_k_pe, cmp_k_w1, cmp_k_w2, cmp_v_pe, cmp_v_w1, cmp_v_w2, conv_w, conv_b, mlstm_norm_w, w_out,
           norm_ffn2_w, ffn2_w_in, ffn2_w_out, final_nw):
    d = x2d.shape[1]
    tpb = seq // TOKEN_TILE
    mod = _adaln(c.T, w_ada, b_ada.reshape(1, -1)).reshape(batch, N_MOD, d)

    x1 = _ffn(x2d, mod, norm_ffn1_w.reshape(1, d), ffn1_w_in, ffn1_w_out,
              (0, 1, 2), tpb)

    weights, b_pack = _split_inproj(w_in, b_in)
    (qpad, kc, vc, ksel, kwin, vtsel, vtwin, ml, vt_m, og_t, misc, misc_t) = _inproj(
        x1, mod, norm_mix_w.reshape(1, d), weights, b_pack, batch, seq)

    pk = _pack_compress(cmp_k_pe, cmp_k_w1, cmp_k_w2)
    pv = _pack_compress(cmp_v_pe, cmp_v_w1, cmp_v_w2)
    stacked = [jnp.stack([a, bb]) for a, bb in zip(pk, pv)]
    kcmp, vtcmp = _compress(kc, vc, *stacked)

    nb = seq // QT
    qpad3 = qpad
    gates_t = misc_t
    oc, sel, cnt = _nsa_cmp(qpad3, kcmp, vtcmp, gates_t, batch, seq)
    counts, lists = _tile_lists(cnt, batch, nb)
    o_t = _nsa_sel(counts, lists, qpad3, ksel.reshape(batch, seq, KV_WIDTH), vtsel, kwin.reshape(batch, seq, KV_WIDTH),
                   vtwin, sel, gates_t, oc, batch, seq)
    o_nsa = o_t.reshape(batch, NSA_WIDTH, seq)

    o_ml = _mlstm(ml, vt_m, og_t, misc, misc_t, conv_w, conv_b.reshape(1, -1), mlstm_norm_w.reshape(1, -1), batch, seq)

    return _ffn(x1, mod, norm_ffn2_w.reshape(1, d), ffn2_w_in, ffn2_w_out,
                (6, 7, 8), tpb, mix=(o_nsa, o_ml, w_out.astype(BF16)), mix_gate_row=5, final_nw=final_nw)


def kernel(x, c, w_ada, b_ada, norm_ffn1_w, ffn1_w_in, ffn1_w_out, norm_mix_w, w_in, b_in, cmp_k_pe, cmp_k_w1, cmp_k_w2, cmp_v_pe, cmp_v_w1, cmp_v_w2, conv_w, conv_b, mlstm_norm_w, w_out, norm_ffn2_w, ffn2_w_in, ffn2_w_out, final_norm_w):
    batch, seq, d = x.shape
    depth = w_ada.shape[0]
    assert depth == 1 and seq % (QT * CMP_STRIDE) == 0 and seq // SEL_BLOCK >= SEL_TOPK
    y = _layer(x.reshape(batch * seq, d), c, batch, seq, w_ada[0], b_ada[0], norm_ffn1_w[0], ffn1_w_in[0],
               ffn1_w_out[0], norm_mix_w[0], w_in[0], b_in[0], cmp_k_pe[0], cmp_k_w1[0], cmp_k_w2[0],
               cmp_v_pe[0], cmp_v_w1[0], cmp_v_w2[0], conv_w[0], conv_b[0], mlstm_norm_w[0], w_out[0],
               norm_ffn2_w[0], ffn2_w_in[0], ffn2_w_out[0], final_norm_w.reshape(1, d))
    return y.reshape(batch, seq, d)
```

```python
import functools

import numpy as np
import jax
import jax.numpy as jnp
from jax import lax
from jax.experimental import pallas as pl
from jax.experimental.pallas import tpu as pltpu

NSA_Q_HEADS = 8
NSA_KV_HEADS = 2
NSA_GROUP = NSA_Q_HEADS // NSA_KV_HEADS
NSA_HEAD_DIM = 64
CMP_BLOCK = 32
CMP_STRIDE = 16
CMP_HIDDEN = 128
SEL_BLOCK = 64
SEL_TOPK = 16
WINDOW = 512
FORCE_BONUS = 1.0e4
N_FORCED = 3
assert FORCE_BONUS > NSA_GROUP
MLSTM_HEADS = 4
MLSTM_HEAD_DIM = 128
CONV_WIDTH = 4
N_MOD = 9
EPS = 1e-6
NEG = -1e30

NSA_WIDTH = NSA_Q_HEADS * NSA_HEAD_DIM
KV_WIDTH = NSA_KV_HEADS * NSA_HEAD_DIM
MLSTM_WIDTH = MLSTM_HEADS * MLSTM_HEAD_DIM
N_GATES = 3 * NSA_Q_HEADS

LANES = 128
QT = 128
TOKEN_TILE = 512
WIDE_TOKEN_TILE = 1024
FF_CHUNK = 256
MLSTM_L = 128
SEL_UNROLL = 6
SEL_CHAIN = 2
CMP_TILES = 2
SEL_TILES = 2
TOPK_ROWS = 32
VT_ROWS = 80
VMEM_LIMIT = 56 * 1024 * 1024

F32 = jnp.float32
BF16 = jnp.bfloat16
HIGHEST = lax.Precision.HIGHEST
LOG2E = 1.4426950408889634


def _cparams(n_axes):
    return pltpu.CompilerParams(dimension_semantics=("arbitrary",) * n_axes,
                                vmem_limit_bytes=VMEM_LIMIT)


def _dot(a, b):
    return jnp.dot(a, b, preferred_element_type=F32)


def _dot_nt(a, b):
    return lax.dot_general(a, b, (((1,), (1,)), ((), ())), preferred_element_type=F32)


def _dot_tn(a, b):
    return lax.dot_general(a, b, (((0,), (0,)), ((), ())), preferred_element_type=F32)


def _sigmoid(x):
    return 1.0 / (1.0 + jnp.exp(-x))


def _silu(x):
    return x * _sigmoid(x)


def _norm_mod(x, nw, sh, sc):
    ms = jnp.mean(x * x, axis=-1, keepdims=True)
    y = x * lax.rsqrt(ms + EPS) * nw
    return y * (1.0 + sc) + sh


def _adaln_kernel(ct_ref, w_ref, b_ref, o_ref):
    w = w_ref[...]
    for r in range(ct_ref.shape[1]):
        a_col = _silu(ct_ref[:, r:r + 1])
        o_ref[r:r + 1, :] = jnp.sum(a_col * w, axis=0, keepdims=True) + b_ref[...]


def _adaln(c_t, w_ada, b_ada):
    d, rows = c_t.shape
    n = w_ada.shape[1]
    tn = n // N_MOD
    return pl.pallas_call(
        _adaln_kernel,
        grid=(N_MOD,),
        in_specs=[pl.BlockSpec((d, rows), lambda j: (0, 0)),
                  pl.BlockSpec((d, tn), lambda j: (0, j)),
                  pl.BlockSpec((1, tn), lambda j: (0, j))],
        out_specs=pl.BlockSpec((rows, tn), lambda j: (0, j)),
        out_shape=jax.ShapeDtypeStruct((rows, n), F32),
        compiler_params=_cparams(1),
        name="adaln",
    )(c_t, w_ada, b_ada)


def _ffn_kernel(*refs, mod_rows, with_mix, mix_gate_row, with_final):
    it = iter(refs)
    x_ref = next(it)
    mod_ref = next(it)
    nw_ref = next(it)
    win_ref = next(it)
    wo_ref = next(it)
    if with_mix:
        ma_ref = next(it)
        mb_ref = next(it)
        wmix_ref = next(it)
    if with_final:
        fnw_ref = next(it)
    o_ref = next(it)

    sh_row, sc_row, g_row = mod_rows
    x = x_ref[...]
    if with_mix:
        mix_t = jnp.concatenate([ma_ref[0].astype(BF16), mb_ref[0].astype(BF16)], axis=0)
        x = x + mod_ref[0, mix_gate_row:mix_gate_row + 1, :] * _dot_tn(mix_t, wmix_ref[...])
    h = _norm_mod(x, nw_ref[...], mod_ref[0, sh_row:sh_row + 1, :], mod_ref[0, sc_row:sc_row + 1, :])
    hb = h.astype(BF16)
    d_ff = wo_ref.shape[0]
    n_chunks = d_ff // FF_CHUNK
    acts = []
    for j in range(n_chunks):
        c0 = j * FF_CHUNK
        g = _dot(hb, win_ref[:, c0:c0 + FF_CHUNK].astype(BF16))
        u = _dot(hb, win_ref[:, d_ff + c0:d_ff + c0 + FF_CHUNK].astype(BF16))
        acts.append((_silu(g) * u).astype(BF16))
    acc = _dot(jnp.concatenate(acts, axis=1), wo_ref[...].astype(BF16))
    y = x + 0.5 * mod_ref[0, g_row:g_row + 1, :] * acc
    if with_final:
        ms = jnp.mean(y * y, axis=-1, keepdims=True)
        y = y * lax.rsqrt(ms + EPS) * fnw_ref[...]
    o_ref[...] = y


def _ffn(x2d, mod, nw, w_in, w_out, mod_rows, tiles_per_batch, mix=None, mix_gate_row=None, final_nw=None):
    t, d = x2d.shape
    tm = TOKEN_TILE
    d_ff = w_out.shape[0]
    assert d_ff % FF_CHUNK == 0
    row = lambda i: (i, 0)
    const = lambda i: (0, 0)
    in_specs = [pl.BlockSpec((tm, d), row),
                pl.BlockSpec((1, N_MOD, d), lambda i: (i // tiles_per_batch, 0, 0)),
                pl.BlockSpec((1, d), const),
                pl.BlockSpec(w_in.shape, const, pipeline_mode=pl.Buffered(1)),
                pl.BlockSpec(w_out.shape, const, pipeline_mode=pl.Buffered(1))]
    args = [x2d, mod, nw, w_in, w_out]
    if mix is not None:
        ma, mb, wmix = mix
        trow = lambda i: (i // tiles_per_batch, 0, i % tiles_per_batch)
        in_specs += [pl.BlockSpec((1, ma.shape[1], tm), trow),
                     pl.BlockSpec((1, mb.shape[1], tm), trow),
                     pl.BlockSpec(wmix.shape, const)]
        args += [ma, mb, wmix]
    if final_nw is not None:
        in_specs.append(pl.BlockSpec((1, d), const))
        args.append(final_nw)
    kern = functools.partial(_ffn_kernel, mod_rows=mod_rows, with_mix=mix is not None,
                             mix_gate_row=mix_gate_row, with_final=final_nw is not None)
    return pl.pallas_call(
        kern,
        grid=(t // tm,),
        in_specs=in_specs,
        out_specs=pl.BlockSpec((tm, d), row),
        out_shape=jax.ShapeDtypeStruct((t, d), F32),
        compiler_params=_cparams(1),
        name="ffn_mix" if mix is not None else "ffn",
    )(*args)


QPAD_W = NSA_Q_HEADS * LANES
MISC_I = N_GATES
MISC_F = N_GATES + MLSTM_HEADS
HALF_BLOCK_W = CMP_STRIDE * KV_WIDTH
B_KV = NSA_WIDTH
B_ML = B_KV + 6 * KV_WIDTH
B_MISC = B_ML + 4 * MLSTM_WIDTH


def _vt_with_ones(v):
    vt = v.T
    n = vt.shape[1]
    pad = jnp.where(lax.broadcasted_iota(jnp.int32, (VT_ROWS - NSA_HEAD_DIM, n), 0) == 0, 1.0, 0.0)
    parts = []
    for h in range(NSA_KV_HEADS):
        parts += [vt[h * NSA_HEAD_DIM:(h + 1) * NSA_HEAD_DIM, :], pad]
    return jnp.concatenate(parts, axis=0).astype(BF16)


def _inproj_kernel(x_ref, mod_ref, nw_ref, w_ref, wml_ref, wmisc_ref, b_ref,
                   qpad_ref, kc_ref, vc_ref, ksel_ref, kwin_ref, vtsel_ref, vtwin_ref,
                   ml_ref, vtm_ref, ogt_ref, misc_ref, misct_ref, rows_ref):
    tm = x_ref.shape[0]
    h = _norm_mod(x_ref[...], nw_ref[...], mod_ref[0, 3:4, :], mod_ref[0, 4:5, :])
    hb = h.astype(BF16)

    def proj(w_ref, c0, width, b0):
        return _dot(hb, w_ref[:, c0:c0 + width]) + b_ref[:, b0 + c0:b0 + c0 + width]

    def half_blocks(val, out_ref):
        rows_ref[...] = val
        pieces = [rows_ref[pl.ds(l, tm // CMP_STRIDE, stride=CMP_STRIDE), :] for l in range(CMP_STRIDE)]
        out_ref[0] = jnp.concatenate(pieces, axis=1).astype(BF16)

    scale = NSA_HEAD_DIM ** -0.5 * LOG2E
    qt = (proj(w_ref, 0, NSA_WIDTH, 0) * scale).T.astype(BF16)
    zero_rows = jnp.zeros((NSA_HEAD_DIM, tm), BF16)
    for i in range(NSA_Q_HEADS):
        slots = [zero_rows] * NSA_KV_HEADS
        slots[i // NSA_GROUP] = qt[i * NSA_HEAD_DIM:(i + 1) * NSA_HEAD_DIM, :]
        qpad_ref[0, i * LANES:(i + 1) * LANES, :] = jnp.concatenate(slots, axis=0)
    kv_c = proj(w_ref, B_KV, 2 * KV_WIDTH, 0)
    half_blocks(kv_c[:, :KV_WIDTH], kc_ref)
    half_blocks(kv_c[:, KV_WIDTH:], vc_ref)
    kv_s = proj(w_ref, B_KV + 2 * KV_WIDTH, 2 * KV_WIDTH, 0)
    ksel_ref[...] = kv_s[:, :KV_WIDTH].astype(BF16)
    vtsel_ref[0] = _vt_with_ones(kv_s[:, KV_WIDTH:])
    kv_w = proj(w_ref, B_KV + 4 * KV_WIDTH, 2 * KV_WIDTH, 0)
    kwin_ref[...] = kv_w[:, :KV_WIDTH].astype(BF16)
    vtwin_ref[0] = _vt_with_ones(kv_w[:, KV_WIDTH:])
    for i in range(2):
        ml_ref[:, i * MLSTM_WIDTH:(i + 1) * MLSTM_WIDTH] = proj(wml_ref, i * MLSTM_WIDTH, MLSTM_WIDTH, B_ML)
    vtm_ref[0] = proj(wml_ref, 2 * MLSTM_WIDTH, MLSTM_WIDTH, B_ML).T.astype(BF16)
    ogt_ref[0] = proj(wml_ref, 3 * MLSTM_WIDTH, MLSTM_WIDTH, B_ML).T
    misc = proj(wmisc_ref, 0, LANES, B_MISC)
    misc_ref[...] = misc
    misct_ref[0] = misc.T


def _inproj(x2d, mod, nw, weights, b_pack, batch, seq):
    t, d = x2d.shape
    tm = WIDE_TOKEN_TILE
    tpb = seq // tm
    row = lambda i: (i, 0)
    const = lambda i: (0, 0)
    trow = lambda i: (i // tpb, 0, i % tpb)
    hrow = lambda i: (i // tpb, i % tpb, 0)
    vt_rows = NSA_KV_HEADS * VT_ROWS
    n_half = seq // CMP_STRIDE
    out_shapes = [
        jax.ShapeDtypeStruct((batch, QPAD_W, seq), BF16),
        jax.ShapeDtypeStruct((batch, n_half, HALF_BLOCK_W), BF16),
        jax.ShapeDtypeStruct((batch, n_half, HALF_BLOCK_W), BF16),
        jax.ShapeDtypeStruct((t, KV_WIDTH), BF16),
        jax.ShapeDtypeStruct((t, KV_WIDTH), BF16),
        jax.ShapeDtypeStruct((batch, vt_rows, seq), BF16),
        jax.ShapeDtypeStruct((batch, vt_rows, seq), BF16),
        jax.ShapeDtypeStruct((t, 2 * MLSTM_WIDTH), F32),
        jax.ShapeDtypeStruct((batch, MLSTM_WIDTH, seq), BF16),
        jax.ShapeDtypeStruct((batch, MLSTM_WIDTH, seq), F32),
        jax.ShapeDtypeStruct((t, LANES), F32),
        jax.ShapeDtypeStruct((batch, LANES, seq), F32),
    ]
    out_specs = [
        pl.BlockSpec((1, QPAD_W, tm), trow),
        pl.BlockSpec((1, tm // CMP_STRIDE, HALF_BLOCK_W), hrow),
        pl.BlockSpec((1, tm // CMP_STRIDE, HALF_BLOCK_W), hrow),
        pl.BlockSpec((tm, KV_WIDTH), row),
        pl.BlockSpec((tm, KV_WIDTH), row),
        pl.BlockSpec((1, vt_rows, tm), trow),
        pl.BlockSpec((1, vt_rows, tm), trow),
        pl.BlockSpec((tm, 2 * MLSTM_WIDTH), row),
        pl.BlockSpec((1, MLSTM_WIDTH, tm), trow),
        pl.BlockSpec((1, MLSTM_WIDTH, tm), trow),
        pl.BlockSpec((tm, LANES), row),
        pl.BlockSpec((1, LANES, tm), trow),
    ]
    return pl.pallas_call(
        _inproj_kernel,
        grid=(t // tm,),
        in_specs=[pl.BlockSpec((tm, d), row),
                  pl.BlockSpec((1, N_MOD, d), lambda i: (i // tpb, 0, 0)),
                  pl.BlockSpec((1, d), const)]
                 + [pl.BlockSpec(w.shape, const) for w in weights]
                 + [pl.BlockSpec(b_pack.shape, const)],
        out_specs=out_specs,
        out_shape=out_shapes,
        scratch_shapes=[pltpu.VMEM((tm, KV_WIDTH), F32)],
        compiler_params=_cparams(1),
        name="inproj",
    )(x2d, mod, nw, *weights, b_pack)


def _split_inproj(w_in, b_in):
    d = w_in.shape[0]
    offs = np.cumsum([0, NSA_WIDTH] + [KV_WIDTH] * 6 + [N_GATES] + [MLSTM_WIDTH] * 4 + [MLSTM_HEADS] * 2)
    g0, m0, i0 = int(offs[7]), int(offs[8]), int(offs[12])
    pad = LANES - N_GATES - 2 * MLSTM_HEADS
    misc_w = jnp.concatenate([w_in[:, g0:g0 + N_GATES], w_in[:, i0:i0 + 2 * MLSTM_HEADS],
                              jnp.zeros((d, pad), w_in.dtype)], axis=1)
    misc_b = jnp.concatenate([b_in[g0:g0 + N_GATES], b_in[i0:i0 + 2 * MLSTM_HEADS], jnp.zeros((pad,), b_in.dtype)])
    w_bf = w_in.astype(BF16)
    weights = [w_bf, w_bf[:, m0:i0], misc_w.astype(BF16)]
    b_pack = jnp.concatenate([b_in[:g0], b_in[m0:i0], misc_b]).reshape(1, -1).astype(F32)
    assert b_pack.shape[1] == B_MISC + LANES
    return weights, b_pack


def _compress_kernel(xk_ref, xv_ref, wa_ref, wb_ref, pe_ref, w1_ref, w2_ref, kc_ref, vtc_ref):
    n_half = xk_ref.shape[1]

    def one(x_ref, s):
        x = x_ref[0]
        a = _dot(x, wa_ref[s])
        b = _dot(x, wb_ref[s])
        pe_term = jnp.dot(pe_ref[s], w1_ref[s], preferred_element_type=F32, precision=HIGHEST)[0:1, :]
        pe2 = jnp.concatenate([pe_term] * NSA_KV_HEADS, axis=1)
        pre = a + pltpu.roll(b, n_half - 1, 0) + pe2
        hid = 0.5 * pre * (1.0 + jnp.tanh(0.7978845608028654 * (pre + 0.044715 * pre * pre * pre)))
        return _dot(hid.astype(BF16), w2_ref[s])

    kc_ref[0] = one(xk_ref, 0).astype(BF16)
    vtc_ref[0] = _vt_with_ones(one(xv_ref, 1))


def _compress(xk, xv, wa, wb, pe8, w1, w2e):
    batch, n_half, width = xk.shape
    bsel = lambda b: (b, 0, 0)
    c3 = lambda b: (0, 0, 0)
    vt_rows = NSA_KV_HEADS * VT_ROWS
    return pl.pallas_call(
        _compress_kernel,
        grid=(batch,),
        in_specs=[pl.BlockSpec((1, n_half, width), bsel), pl.BlockSpec((1, n_half, width), bsel),
                  pl.BlockSpec(wa.shape, c3), pl.BlockSpec(wb.shape, c3),
                  pl.BlockSpec(pe8.shape, c3), pl.BlockSpec(w1.shape, c3), pl.BlockSpec(w2e.shape, c3)],
        out_specs=[pl.BlockSpec((1, n_half, KV_WIDTH), bsel), pl.BlockSpec((1, vt_rows, n_half), bsel)],
        out_shape=[jax.ShapeDtypeStruct((batch, n_half, KV_WIDTH), BF16),
                   jax.ShapeDtypeStruct((batch, vt_rows, n_half), BF16)],
        compiler_params=_cparams(1),
        name="compress",
    )(xk, xv, wa, wb, pe8, w1, w2e)


def _pack_compress(pe, w1, w2):
    half = CMP_BLOCK // 2
    w1r = w1.reshape(CMP_BLOCK, NSA_HEAD_DIM, CMP_HIDDEN)
    def block_diag(w, axis):
        z = jnp.zeros_like(w)
        rows = [jnp.concatenate([w if hh == h2 else z for h2 in range(NSA_KV_HEADS)], axis=-1)
                for hh in range(NSA_KV_HEADS)]
        return jnp.concatenate(rows, axis=axis)

    def expand(w_half):
        return block_diag(w_half, 1).reshape(half * KV_WIDTH, NSA_KV_HEADS * CMP_HIDDEN)

    wa = expand(w1r[:half]).astype(BF16)
    wb = expand(w1r[half:]).astype(BF16)
    w2e = block_diag(w2, 0).astype(BF16)
    pe8 = jnp.concatenate([pe.reshape(1, CMP_BLOCK * NSA_HEAD_DIM),
                           jnp.zeros((7, CMP_BLOCK * NSA_HEAD_DIM), pe.dtype)], axis=0)
    return wa, wb, pe8, w1, w2e


def _q_cols(q_ref, h, cols):
    return jnp.concatenate(
        [q_ref[0, (NSA_GROUP * h + g) * LANES:(NSA_GROUP * h + g + 1) * LANES, cols] for g in range(NSA_GROUP)],
        axis=1)


def _q_aug(q_ref, qs_ref, h, cols=slice(0, QT)):
    return jnp.concatenate([_q_cols(q_ref, h, cols), qs_ref[h]], axis=0)


def _slope2(head):
    return float(2.0 ** (-8.0 * (head + 1) / NSA_Q_HEADS)) * LOG2E


def _bf16_split3(x):
    parts = []
    for _ in range(3):
        p = float(np.asarray(x, np.float32).astype(BF16).astype(np.float32))
        parts.append(p)
        x = x - p
    return parts


def _init_alibi_operands(qs_ref, kpos_ref, key_stride):
    lane = lax.broadcasted_iota(jnp.int32, (QT, LANES), 1)
    row = lax.broadcasted_iota(jnp.int32, (QT, LANES), 0)
    kpos_ref[...] = jnp.where(lane < 3, row.astype(F32), 0.0).astype(BF16)
    srow = lax.broadcasted_iota(jnp.int32, (LANES, QT), 0)
    for h in range(NSA_KV_HEADS):
        for g in range(NSA_GROUP):
            hi, mid, lo = _bf16_split3(key_stride * _slope2(NSA_GROUP * h + g))
            blk = jnp.where(srow == 0, hi, jnp.where(srow == 1, mid, jnp.where(srow == 2, lo, 0.0)))
            qs_ref[h, :, g * QT:(g + 1) * QT] = blk.astype(BF16)


def _gate_row(gt_ref, head, branch, cols):
    r = head * 3 + branch
    return _sigmoid(gt_ref[0, r:r + 1, cols])


def _softmax_jobs(jobs):
    scores = [_dot(jnp.concatenate(job[1], axis=0), job[0]) for job in jobs]
    staged = []
    for (q_aug, k_tiles, masks, offs, value_groups, state), st_all in zip(jobs, scores):
        n_t = len(k_tiles)
        per_g = []
        for g in range(NSA_GROUP):
            sl = slice(g * QT, (g + 1) * QT)
            ss = []
            for u in range(n_t):
                s = st_all[u * QT:(u + 1) * QT, sl]
                ss.append(s if masks[u] is None else jnp.where(masks[u], s, NEG))
            mx = None
            for u in range(n_t):
                cand = jnp.max(ss[u], axis=0, keepdims=True) + offs[u][g]
                mx = cand if mx is None else jnp.maximum(mx, cand)
            alpha = None
            if state is None:
                m_new = mx
            else:
                m_ref, _, h = state
                m_old = m_ref[h, :, sl]
                m_new = jnp.maximum(m_old, mx)
                alpha = jnp.exp2(m_old - m_new)
                m_ref[h, :, sl] = m_new
            m_use = jnp.where(m_new < 0.5 * NEG, 0.0, m_new)
            per_g.append((ss, m_use, alpha))
        staged.append(per_g)
    ets = []
    for (q_aug, k_tiles, masks, offs, value_groups, state), per_g in zip(jobs, staged):
        n_t = len(k_tiles)
        rows = [jnp.concatenate([jnp.exp2(per_g[g][0][u] - (per_g[g][1] - offs[u][g])).astype(BF16)
                                 for g in range(NSA_GROUP)], axis=1) for u in range(n_t)]
        ets.append(jnp.concatenate(rows, axis=0))
    results = []
    for (q_aug, k_tiles, masks, offs, value_groups, state), per_g, et in zip(jobs, staged, ets):
        mats = [jnp.concatenate(grp, axis=1) for grp in value_groups]
        pv_all = _dot(jnp.concatenate(mats, axis=0), et)
        pvs, r0 = [], 0
        for mat in mats:
            pvs.append(pv_all[r0:r0 + mat.shape[0], :])
            r0 += mat.shape[0]
        if state is None:
            results.append(pvs)
        else:
            _, acc_ref, h = state
            acc_ref[h] = jnp.concatenate([pg[2] for pg in per_g], axis=1) * acc_ref[h] + pvs[0]
            results.append(None)
    return results


def _gated(gt_ref, h, branch, pv, cols=slice(0, QT)):
    inv = 1.0 / jnp.maximum(pv[NSA_HEAD_DIM:NSA_HEAD_DIM + 1, :], 1e-30)
    out = []
    for g in range(NSA_GROUP):
        sl = slice(g * QT, (g + 1) * QT)
        out.append(pv[0:NSA_HEAD_DIM, sl] * (inv[:, sl] * _gate_row(gt_ref, NSA_GROUP * h + g, branch, cols)))
    return out


def _nsa_cmp_kernel(q_ref, kc_ref, vtc_ref, gt_ref, oc_ref, sel_ref, cnt_ref,
                    qs_ref, kpos_ref, ovt_ref, pvo_ref, pvi_ref):
    ncp = kc_ref.shape[1]
    n_ct = ncp // QT
    n_sel = sel_ref.shape[3]
    tiles = range(CMP_TILES)
    starts = [(pl.program_id(1) * CMP_TILES + s) * QT for s in tiles]
    cols = [slice(s * QT, (s + 1) * QT) for s in tiles]
    units = [(s, h) for s in tiles for h in range(NSA_KV_HEADS)]
    tile_span = QT * CMP_STRIDE

    @pl.when((pl.program_id(0) == 0) & (pl.program_id(1) == 0))
    def _():
        _init_alibi_operands(qs_ref, kpos_ref, float(CMP_STRIDE))
        j_o = lax.broadcasted_iota(jnp.int32, (n_sel, ncp), 0)
        c_o = lax.broadcasted_iota(jnp.int32, (n_sel, ncp), 1)
        ov = ((c_o * CMP_STRIDE <= j_o * SEL_BLOCK + (SEL_BLOCK - 1))
              & (c_o * CMP_STRIDE + (CMP_BLOCK - 1) >= j_o * SEL_BLOCK))
        ovt_ref[...] = jnp.where(ov, 1.0, 0.0).astype(BF16)

    a0 = (lax.broadcasted_iota(jnp.int32, (QT, QT), 1)
          - CMP_STRIDE * lax.broadcasted_iota(jnp.int32, (QT, QT), 0))
    q_augs = [_q_aug(q_ref, qs_ref, h, cols[s]) for s, h in units]
    n_vis = jnp.minimum((starts[-1] + QT - CMP_BLOCK) // tile_span + 1, n_ct)

    for k in range(1, n_ct + 1):
        @pl.when(n_vis == k)
        def _(k=k):
            jobs = []
            for i, (s, h) in enumerate(units):
                k_tiles, masks, offs, vts, ovs = [], [], [], [], []
                for ci in range(k):
                    cs = slice(ci * QT, (ci + 1) * QT)
                    first_end = ci * tile_span + CMP_BLOCK - 1
                    k_tiles.append(jnp.concatenate([kc_ref[0, cs, :], kpos_ref[...]], axis=1))
                    masks.append(a0 + (starts[s] - first_end) >= 0)
                    offs.append([_slope2(NSA_GROUP * h + g) * first_end for g in range(NSA_GROUP)])
                    vts.append(vtc_ref[0, h * VT_ROWS:(h + 1) * VT_ROWS, cs])
                    ovs.append(ovt_ref[:, cs])
                jobs.append((q_augs[i], k_tiles, masks, offs, [vts, ovs], None))
            for i, (pvo, pvi) in enumerate(_softmax_jobs(jobs)):
                pvo_ref[i] = pvo
                pvi_ref[i] = pvi

    j_io = lax.broadcasted_iota(jnp.int32, (n_sel, QT), 0)
    q_io = lax.broadcasted_iota(jnp.int32, (n_sel, QT), 1)
    ones8 = jnp.ones((8, QT), BF16)

    taken = -(2.0 ** 126)
    scores = []
    for i, (s, h) in enumerate(units):
        t_j = starts[s] + q_io
        cur = t_j // SEL_BLOCK
        valid = j_io * SEL_BLOCK <= t_j
        forced = (j_io == 0) | (j_io == cur) | (j_io == cur - 1)
        pvo = pvo_ref[i]
        o_c = _gated(gt_ref, h, 0, pvo, cols[s])
        inv = 1.0 / jnp.maximum(pvo[NSA_HEAD_DIM:NSA_HEAD_DIM + 1, :], 1e-30)
        imp = None
        for g in range(NSA_GROUP):
            sl = slice(g * QT, (g + 1) * QT)
            oc_ref[0, NSA_GROUP * h + g, :, cols[s]] = o_c[g]
            term = pvi_ref[i, :, sl] * inv[:, sl]
            imp = term if imp is None else imp + term
        scores.append(jnp.where(forced, taken, jnp.where(valid, imp, -FORCE_BONUS)))

    n_cls = n_sel // TOPK_ROWS
    cls = jnp.minimum((starts[-1] + QT + SEL_BLOCK * TOPK_ROWS - 1) // (SEL_BLOCK * TOPK_ROWS), n_cls)

    for k in range(1, n_cls + 1):
        @pl.when(cls == k)
        def _(rows=k * TOPK_ROWS):
            j_sub = j_io[:rows]

            def pick(_, carry):
                out = []
                for sc in carry:
                    mx = jnp.max(sc, axis=0, keepdims=True)
                    idx = jnp.min(jnp.where(sc == mx, j_sub, n_sel), axis=0, keepdims=True)
                    out.append(jnp.where(j_sub == idx, taken, sc))
                return tuple(out)

            picked = lax.fori_loop(0, SEL_TOPK - N_FORCED, pick, tuple(sc[:rows] for sc in scores))
            for i, (s, h) in enumerate(units):
                sel = jnp.where(picked[i] <= 0.5 * taken, 1.0, 0.0)
                if rows < n_sel:
                    sel = jnp.concatenate([sel, jnp.zeros((n_sel - rows, QT), F32)], axis=0)
                sel_ref[0, h, s] = sel
                cnt = _dot_nt(ones8, sel.astype(BF16))
                cnt_ref[0, s, h] = cnt[0:1, :]


def _nsa_cmp(qpad3, kc, vtc, gates_t, batch, seq):
    nb = seq // QT
    n_sel = seq // SEL_BLOCK
    ncp = kc.shape[1]
    assert ncp % QT == 0 and nb % CMP_TILES == 0
    vt_rows = NSA_KV_HEADS * VT_ROWS
    qw = QT * CMP_TILES
    n_units = CMP_TILES * NSA_KV_HEADS
    return pl.pallas_call(
        _nsa_cmp_kernel,
        grid=(batch, nb // CMP_TILES),
        in_specs=[pl.BlockSpec((1, QPAD_W, qw), lambda b, i: (b, 0, i)),
                  pl.BlockSpec((1, ncp, KV_WIDTH), lambda b, i: (b, 0, 0)),
                  pl.BlockSpec((1, vt_rows, ncp), lambda b, i: (b, 0, 0)),
                  pl.BlockSpec((1, N_GATES, qw), lambda b, i: (b, 0, i))],
        out_specs=[pl.BlockSpec((1, NSA_Q_HEADS, NSA_HEAD_DIM, qw), lambda b, i: (b, 0, 0, i)),
                   pl.BlockSpec((1, NSA_KV_HEADS, CMP_TILES, n_sel, QT), lambda b, i: (b, 0, i, 0, 0)),
                   pl.BlockSpec((1, CMP_TILES, NSA_KV_HEADS, 1, n_sel), lambda b, i: (b, i, 0, 0, 0))],
        out_shape=[jax.ShapeDtypeStruct((batch, NSA_Q_HEADS, NSA_HEAD_DIM, seq), F32),
                   jax.ShapeDtypeStruct((batch, NSA_KV_HEADS, nb, n_sel, QT), F32),
                   jax.ShapeDtypeStruct((batch, nb, NSA_KV_HEADS, 1, n_sel), F32)],
        scratch_shapes=[pltpu.VMEM((NSA_KV_HEADS, LANES, NSA_GROUP * QT), BF16),
                        pltpu.VMEM((QT, LANES), BF16),
                        pltpu.VMEM((n_sel, ncp), BF16),
                        pltpu.VMEM((n_units, VT_ROWS, NSA_GROUP * QT), F32),
                        pltpu.VMEM((n_units, n_sel, NSA_GROUP * QT), F32)],
        compiler_params=_cparams(2),
        name="nsa_cmp",
    )(qpad3, kc, vtc, gates_t)


def _nsa_sel_kernel(count_ref, list_ref, q_ref, ksel_ref, vtsel_ref, kwin_ref, vtwin_ref, selm_ref, gt_ref,
                    oc_ref, o_ref, m_sel, acc_sel, qs_ref, kpos_ref, part_ref, kposm_ref):
    b = pl.program_id(0)
    nb = pl.num_programs(1) * SEL_TILES
    tiles = range(SEL_TILES)
    qbs = [pl.program_id(1) * SEL_TILES + s for s in tiles]
    cols = [slice(s * QT, (s + 1) * QT) for s in tiles]
    units = [(s, h) for s in tiles for h in range(NSA_KV_HEADS)]
    k_io = lax.broadcasted_iota(jnp.int32, (QT, QT), 0)
    q_io = lax.broadcasted_iota(jnp.int32, (QT, QT), 1)
    causal = q_io >= k_io
    in_window = q_io < k_io

    pen_row0 = 3
    pen_rows = 16
    assert pen_row0 + 2 * SEL_UNROLL <= pen_rows

    @pl.when((b == 0) & (pl.program_id(1) == 0))
    def _():
        _init_alibi_operands(qs_ref, kpos_ref, 1.0)
        lane = lax.broadcasted_iota(jnp.int32, (QT, LANES), 1)
        row = lax.broadcasted_iota(jnp.int32, (QT, LANES), 0)
        base = jnp.where(lane < pen_row0, row.astype(F32), 0.0)
        for u in range(SEL_UNROLL):
            ind = jnp.where(lane == pen_row0 + 2 * u, jnp.where(row < SEL_BLOCK, 1.0, 0.0),
                            jnp.where(lane == pen_row0 + 2 * u + 1, jnp.where(row >= SEL_BLOCK, 1.0, 0.0), 0.0))
            kposm_ref[u] = (base + ind).astype(BF16)

    def k_aug(k_ref, t, u=None):
        off = pl.multiple_of(t * QT, QT)
        tail = kpos_ref[...] if u is None else kposm_ref[u]
        return jnp.concatenate([k_ref[0, pl.ds(off, QT), :], tail], axis=1)

    def v_ext(vt_ref, h, t):
        return vt_ref[0, h * VT_ROWS:(h + 1) * VT_ROWS, pl.ds(pl.multiple_of(t * QT, QT), QT)]

    def tile_offs(h, t, live=None):
        pos = (t * QT).astype(F32)
        out = [_slope2(NSA_GROUP * h + g) * pos for g in range(NSA_GROUP)]
        return out if live is None else [jnp.where(live, o, NEG) for o in out]

    def penalties(s, h, t):
        mrow = selm_ref[0, h, s, pl.ds(2 * t, 2), :]
        return (mrow - 1.0) * (-NEG)

    q_tops = [_q_cols(q_ref, h, cols[s]) for s, h in units]
    q_augs = [jnp.concatenate([q_tops[i], qs_ref[h]], axis=0) for i, (s, h) in enumerate(units)]
    zero_rows = jnp.zeros((LANES - pen_rows, NSA_GROUP * QT), BF16)

    def q_aug_sel(i, h, pens):
        blk = jnp.concatenate([jnp.zeros((pen_row0, QT), F32)] + pens
                              + [jnp.zeros((pen_rows - pen_row0 - 2 * len(pens), QT), F32)], axis=0)
        ext = qs_ref[h, 0:pen_rows, :].astype(F32) + jnp.concatenate([blk] * NSA_GROUP, axis=1)
        return jnp.concatenate([q_tops[i], ext.astype(BF16), zero_rows], axis=0)

    slot = [(b * nb + qbs[s]) * NSA_KV_HEADS + h for s, h in units]
    n_tiles = [count_ref[sl] for sl in slot]

    m_sel[...] = jnp.full(m_sel.shape, NEG, F32)
    acc_sel[...] = jnp.zeros(acc_sel.shape, F32)

    def sel_jobs(trip, first, step):
        jobs = []
        for i, (s, h) in enumerate(units):
            k_tiles, masks, offs, vts, pens = [], [], [], [], []
            for u in range(SEL_UNROLL):
                if first and step == 0 and u == 0:
                    t, mask, off = qbs[s], causal, tile_offs(h, qbs[s])
                else:
                    idx = (trip * SEL_CHAIN + step) * SEL_UNROLL + u - 1
                    live = idx < n_tiles[i]
                    t = jnp.where(live, list_ref[slot[i] * nb + jnp.minimum(idx, nb - 1)], 0)
                    mask, off = None, tile_offs(h, t, live)
                k_tiles.append(k_aug(ksel_ref, t, u))
                masks.append(mask)
                offs.append(off)
                vts.append(v_ext(vtsel_ref, h, t))
                pens.append(penalties(s, h, t))
            jobs.append((q_aug_sel(i, h, pens), k_tiles, masks, offs, [vts], (m_sel, acc_sel, i)))
        return jobs

    n_win = WINDOW // QT + 1
    win_jobs = []
    for i, (s, h) in enumerate(units):
        k_tiles, masks, offs, vts = [], [], [], []
        for u in range(n_win):
            raw = qbs[s] - (n_win - 1) + u
            t = jnp.maximum(raw, 0)
            k_tiles.append(k_aug(kwin_ref, t))
            masks.append(in_window if u == 0 else causal if u == n_win - 1 else None)
            offs.append(tile_offs(h, t, raw >= 0))
            vts.append(v_ext(vtwin_ref, h, t))
        win_jobs.append((q_augs[i], k_tiles, masks, offs, [vts], None))
    jobs = sel_jobs(0, True, 0) + win_jobs
    for step in range(1, SEL_CHAIN):
        jobs += sel_jobs(0, True, step)
    results = _softmax_jobs(jobs)
    for i, (s, h) in enumerate(units):
        o_w = _gated(gt_ref, h, 2, results[len(units) + i][0], cols[s])
        for g in range(NSA_GROUP):
            head = NSA_GROUP * h + g
            part_ref[s, head] = oc_ref[0, head, :, cols[s]] + o_w[g]

    def sel_body(trip, carry):
        jobs = []
        for step in range(SEL_CHAIN):
            jobs += sel_jobs(trip, False, step)
        _softmax_jobs(jobs)
        return carry

    per_trip = SEL_UNROLL * SEL_CHAIN
    n_trips = (functools.reduce(jnp.maximum, n_tiles) + per_trip) // per_trip
    lax.fori_loop(1, n_trips, sel_body, 0)

    for i, (s, h) in enumerate(units):
        o_s = _gated(gt_ref, h, 1, acc_sel[i], cols[s])
        for g in range(NSA_GROUP):
            head = NSA_GROUP * h + g
            o_ref[0, head, :, cols[s]] = part_ref[s, head] + o_s[g]


def _tile_lists(cnt, batch, nb):
    pair = cnt.reshape(batch, nb, NSA_KV_HEADS, nb, 2).sum(axis=-1)
    p_io = lax.broadcasted_iota(jnp.int32, pair.shape, 3)
    flag = (pair > 0.5) & (p_io < lax.broadcasted_iota(jnp.int32, pair.shape, 1))
    rank = jnp.cumsum(flag.astype(jnp.int32), axis=-1) - 1
    slot_io = lax.broadcasted_iota(jnp.int32, pair.shape + (nb,), 4)
    hit = flag[..., None] & (rank[..., None] == slot_io)
    lists = jnp.sum(jnp.where(hit, p_io[..., None], 0), axis=3)
    counts = jnp.sum(flag.astype(jnp.int32), axis=-1)
    return counts.reshape(-1), lists.reshape(-1)


def _nsa_sel(counts, lists, qpad3, ksel, vtsel, kwin, vtwin, selm, gates_t, oc, batch, seq):
    nb = seq // QT
    assert nb % SEL_TILES == 0
    res = lambda b, i, c, l: (b, 0, 0)
    vt_rows = NSA_KV_HEADS * VT_ROWS
    qw = QT * SEL_TILES
    n_units = SEL_TILES * NSA_KV_HEADS
    grid_spec = pltpu.PrefetchScalarGridSpec(
        num_scalar_prefetch=2,
        grid=(batch, nb // SEL_TILES),
        in_specs=[pl.BlockSpec((1, QPAD_W, qw), lambda b, i, c, l: (b, 0, i)),
                  pl.BlockSpec((1, seq, KV_WIDTH), res),
                  pl.BlockSpec((1, vt_rows, seq), res),
                  pl.BlockSpec((1, seq, KV_WIDTH), res),
                  pl.BlockSpec((1, vt_rows, seq), res),
                  pl.BlockSpec((1, NSA_KV_HEADS, SEL_TILES, 2 * nb, QT), lambda b, i, c, l: (b, 0, i, 0, 0)),
                  pl.BlockSpec((1, N_GATES, qw), lambda b, i, c, l: (b, 0, i)),
                  pl.BlockSpec((1, NSA_Q_HEADS, NSA_HEAD_DIM, qw), lambda b, i, c, l: (b, 0, 0, i))],
        out_specs=pl.BlockSpec((1, NSA_Q_HEADS, NSA_HEAD_DIM, qw), lambda b, i, c, l: (b, 0, 0, i)),
        scratch_shapes=[pltpu.VMEM((n_units, 1, NSA_GROUP * QT), F32),
                        pltpu.VMEM((n_units, VT_ROWS, NSA_GROUP * QT), F32),
                        pltpu.VMEM((NSA_KV_HEADS, LANES, NSA_GROUP * QT), BF16),
                        pltpu.VMEM((QT, LANES), BF16),
                        pltpu.VMEM((SEL_TILES, NSA_Q_HEADS, NSA_HEAD_DIM, QT), F32),
                        pltpu.VMEM((SEL_UNROLL, QT, LANES), BF16)],
    )
    return pl.pallas_call(
        _nsa_sel_kernel,
        grid_spec=grid_spec,
        out_shape=jax.ShapeDtypeStruct((batch, NSA_Q_HEADS, NSA_HEAD_DIM, seq), F32),
        compiler_params=_cparams(2),
        name="nsa_sel",
    )(counts, lists, qpad3, ksel, vtsel, kwin, vtwin, selm, gates_t, oc)


def _mlstm_kernel(q_ref, k_ref, vt_ref, ogt_ref, misc_ref, misct_ref, cw_ref, cb_ref, nw_ref,
                  o_ref, xq_ref, xk_ref, c_ref, n_ref, m_ref):
    tm = q_ref.shape[0]
    L = MLSTM_L
    hd = MLSTM_HEAD_DIM
    halo = 8

    @pl.when(pl.program_id(1) == 0)
    def _():
        xq_ref[0:halo, :] = jnp.zeros((halo, MLSTM_WIDTH), F32)
        xk_ref[0:halo, :] = jnp.zeros((halo, MLSTM_WIDTH), F32)
        c_ref[...] = jnp.zeros_like(c_ref)
        n_ref[...] = jnp.zeros_like(n_ref)
        m_ref[...] = jnp.zeros_like(m_ref)

    def conv(x_ref, buf_ref, col0):
        buf_ref[halo:halo + tm, :] = x_ref[...]
        y = cb_ref[:, col0:col0 + MLSTM_WIDTH]
        for kk in range(CONV_WIDTH):
            r0 = halo - (CONV_WIDTH - 1) + kk
            y = y + cw_ref[kk:kk + 1, col0:col0 + MLSTM_WIDTH] * buf_ref[r0:r0 + tm, :]
        tail = buf_ref[tm:tm + halo, :]
        buf_ref[0:halo, :] = tail
        return _silu(y)

    qc = conv(q_ref, xq_ref, 0)
    kc = conv(k_ref, xk_ref, MLSTM_WIDTH) * (hd ** -0.5)

    misc = misc_ref[...]
    misct = misct_ref[0]
    lf_c = -jnp.log(1.0 + jnp.exp(-misc))
    lf_r = -jnp.log(1.0 + jnp.exp(-misct))
    row_io = lax.broadcasted_iota(jnp.int32, (L, L), 0)
    col_io = lax.broadcasted_iota(jnp.int32, (L, L), 1)
    tril = jnp.where(col_io <= row_io, 1.0, 0.0).astype(BF16)
    triu = jnp.where(row_io <= col_io, 1.0, 0.0).astype(BF16)

    def split3(x):
        hi = x.astype(BF16)
        r1 = x - hi.astype(F32)
        mid = r1.astype(BF16)
        return hi, mid, (r1 - mid.astype(F32)).astype(BF16)

    lf_c3 = split3(lf_c)
    lf_r3 = split3(lf_r)
    b_c = jnp.concatenate([sum(_dot(tril, p[ci * L:(ci + 1) * L, :]) for p in lf_c3)
                           for ci in range(tm // L)], axis=0)
    b_r = jnp.concatenate([sum(_dot(p[:, ci * L:(ci + 1) * L], triu) for p in lf_r3)
                           for ci in range(tm // L)], axis=1)

    tri_t = lax.broadcasted_iota(jnp.int32, (L, L), 0) <= lax.broadcasted_iota(jnp.int32, (L, L), 1)
    nw_cols = [jnp.broadcast_to(nw_ref[:, h * hd:(h + 1) * hd], (hd, hd)).T for h in range(MLSTM_HEADS)]

    def split3_rows(row):
        hi = row.astype(BF16)
        r1 = row - hi.astype(F32)
        mid = r1.astype(BF16)
        lo = (r1 - mid.astype(F32)).astype(BF16)
        sub = lax.broadcasted_iota(jnp.int32, (8, row.shape[1]), 0)
        out = jnp.where(sub == 0, hi.astype(F32), jnp.where(sub == 1, mid.astype(F32),
                                                            jnp.where(sub == 2, lo.astype(F32), 0.0)))
        return out.astype(BF16)

    def rows_sum3(x8):
        return x8[0:1, :] + x8[1:2, :] + x8[2:3, :]

    for ci in range(tm // L):
        r0 = ci * L
        for h in range(MLSTM_HEADS):
            cs = slice(h * hd, (h + 1) * hd)
            qb16 = qc[r0:r0 + L, cs].astype(BF16)
            kb16 = kc[r0:r0 + L, cs].astype(BF16)
            vt = vt_ref[0, cs, r0:r0 + L]
            a_col = (misc[r0:r0 + L, MISC_I + h:MISC_I + h + 1]
                     - b_c[r0:r0 + L, MISC_F + h:MISC_F + h + 1])
            brow = b_r[MISC_F + h:MISC_F + h + 1, r0:r0 + L]
            lirow = misct[MISC_I + h:MISC_I + h + 1, r0:r0 + L]
            m_prev = m_ref[h]
            ct = c_ref[h]
            nrow = n_ref[h]

            dmat = jnp.where(tri_t, brow + a_col, NEG)
            m_inter = brow + m_prev
            m_t = jnp.maximum(jnp.max(dmat, axis=0, keepdims=True), m_inter)
            w = jnp.exp(dmat - m_t) * _dot_nt(kb16, qb16)
            decay = jnp.exp(m_inter - m_t)
            num = _dot(vt, w.astype(BF16)) + decay * _dot_nt(ct.astype(BF16), qb16)
            nq = rows_sum3(_dot_nt(split3_rows(nrow), qb16))
            den = jnp.sum(w, axis=0, keepdims=True) + decay * nq
            hout = num * (1.0 / jnp.maximum(jnp.abs(den), jnp.exp(-m_t)))

            b_last = brow[:, L - 1:L]
            grow = b_last - brow + lirow
            m_new = jnp.maximum(b_last + m_prev, jnp.max(grow, axis=1, keepdims=True))
            wk = jnp.exp(grow - m_new)
            d_c = jnp.exp(b_last + m_prev - m_new)
            c_ref[h] = d_c * ct + _dot((vt.astype(F32) * wk).astype(BF16), kb16)
            n_ref[h] = d_c * nrow + rows_sum3(_dot(split3_rows(wk), kb16))
            m_ref[h] = m_new

            hn = hout * lax.rsqrt(jnp.mean(hout * hout, axis=0, keepdims=True) + EPS) * nw_cols[h]
            o_ref[0, cs, r0:r0 + L] = (hn * _sigmoid(ogt_ref[0, cs, r0:r0 + L])).astype(o_ref.dtype)


def _mlstm(ml, vt_m, og_t, misc, misc_t, conv_w, conv_b, norm_w, batch, seq):
    tm = WIDE_TOKEN_TILE
    tpb = seq // tm
    col = lambda j: (lambda b, i: (b * tpb + i, j))
    const = lambda b, i: (0, 0)
    trow = lambda b, i: (b, 0, i)
    return pl.pallas_call(
        _mlstm_kernel,
        grid=(batch, tpb),
        in_specs=[pl.BlockSpec((tm, MLSTM_WIDTH), col(0)),
                  pl.BlockSpec((tm, MLSTM_WIDTH), col(1)),
                  pl.BlockSpec((1, MLSTM_WIDTH, tm), trow),
                  pl.BlockSpec((1, MLSTM_WIDTH, tm), trow),
                  pl.BlockSpec((tm, LANES), col(0)),
                  pl.BlockSpec((1, LANES, tm), trow),
                  pl.BlockSpec(conv_w.shape, const),
                  pl.BlockSpec(conv_b.shape, const),
                  pl.BlockSpec(norm_w.shape, const)],
        out_specs=pl.BlockSpec((1, MLSTM_WIDTH, tm), trow),
        out_shape=jax.ShapeDtypeStruct((batch, MLSTM_WIDTH, seq), BF16),
        scratch_shapes=[pltpu.VMEM((tm + 8, MLSTM_WIDTH), F32),
                        pltpu.VMEM((tm + 8, MLSTM_WIDTH), F32),
                        pltpu.VMEM((MLSTM_HEADS, MLSTM_HEAD_DIM, MLSTM_HEAD_DIM), F32),
                        pltpu.VMEM((MLSTM_HEADS, 1, MLSTM_HEAD_DIM), F32),
                        pltpu.VMEM((MLSTM_HEADS, 1, 1), F32)],
        compiler_params=_cparams(2),
        name="mlstm",
    )(ml, ml, vt_m, og_t, misc, misc_t, conv_w, conv_b, norm_w)


def _layer(x2d, c, batch, seq, w_ada, b_ada, norm_ffn1_w, ffn1_w_in, ffn1_w_out, norm_mix_w, w_in, b_in,
           cmp_k_pe, cmp_k_w1, cmp_k_w2, cmp_v_pe, cmp_v_w1, cmp_v_w2, conv_w, conv_b, mlstm_norm_w, w_out,
           norm_ffn2_w, ffn2_w_in, ffn2_w_out, final_nw):
    d = x2d.shape[1]
    tpb = seq // TOKEN_TILE
    mod = _adaln(c.T, w_ada, b_ada.reshape(1, -1)).reshape(batch, N_MOD, d)

    x1 = _ffn(x2d, mod, norm_ffn1_w.reshape(1, d), ffn1_w_in, ffn1_w_out,
              (0, 1, 2), tpb)

    weights, b_pack = _split_inproj(w_in, b_in)
    (qpad, kc, vc, ksel, kwin, vtsel, vtwin, ml, vt_m, og_t, misc, misc_t) = _inproj(
        x1, mod, norm_mix_w.reshape(1, d), weights, b_pack, batch, seq)

    pk = _pack_compress(cmp_k_pe, cmp_k_w1, cmp_k_w2)
    pv = _pack_compress(cmp_v_pe, cmp_v_w1, cmp_v_w2)
    stacked = [jnp.stack([a, bb]) for a, bb in zip(pk, pv)]
    kcmp, vtcmp = _compress(kc, vc, *stacked)

    nb = seq // QT
    qpad3 = qpad
    gates_t = misc_t
    oc, sel, cnt = _nsa_cmp(qpad3, kcmp, vtcmp, gates_t, batch, seq)
    counts, lists = _tile_lists(cnt, batch, nb)
    o_t = _nsa_sel(counts, lists, qpad3, ksel.reshape(batch, seq, KV_WIDTH), vtsel, kwin.reshape(batch, seq, KV_WIDTH),
                   vtwin, sel, gates_t, oc, batch, seq)
    o_nsa = o_t.reshape(batch, NSA_WIDTH, seq)

    o_ml = _mlstm(ml, vt_m, og_t, misc, misc_t, conv_w, conv_b.reshape(1, -1), mlstm_norm_w.reshape(1, -1), batch, seq)

    return _ffn(x1, mod, norm_ffn2_w.reshape(1, d), ffn2_w_in, ffn2_w_out,
                (6, 7, 8), tpb, mix=(o_nsa, o_ml, w_out.astype(BF16)), mix_gate_row=5, final_nw=final_nw)


def kernel(x, c, w_ada, b_ada, norm_ffn1_w, ffn1_w_in, ffn1_w_out, norm_mix_w, w_in, b_in, cmp_k_pe, cmp_k_w1, cmp_k_w2, cmp_v_pe, cmp_v_w1, cmp_v_w2, conv_w, conv_b, mlstm_norm_w, w_out, norm_ffn2_w, ffn2_w_in, ffn2_w_out, final_norm_w):
    batch, seq, d = x.shape
    depth = w_ada.shape[0]
    assert depth == 1 and seq % (QT * CMP_STRIDE) == 0 and seq // SEL_BLOCK >= SEL_TOPK
    y = _layer(x.reshape(batch * seq, d), c, batch, seq, w_ada[0], b_ada[0], norm_ffn1_w[0], ffn1_w_in[0],
               ffn1_w_out[0], norm_mix_w[0], w_in[0], b_in[0], cmp_k_pe[0], cmp_k_w1[0], cmp_k_w2[0],
               cmp_v_pe[0], cmp_v_w1[0], cmp_v_w2[0], conv_w[0], conv_b[0], mlstm_norm_w[0], w_out[0],
               norm_ffn2_w[0], ffn2_w_in[0], ffn2_w_out[0], final_norm_w.reshape(1, d))
    return y.reshape(batch, seq, d)
```
